```python
import jax, jax.numpy as jnp
from jax import lax
import numpy as np

D_MODEL = 1024
BATCH = 8
SEQ = 8192
DEPTH = 1

MLA_HEADS = 8
MLA_Q_LORA = 256
MLA_KV_LORA = 128
MLA_NOPE_DIM = 64
MLA_ROPE_DIM = 32
MLA_V_DIM = 64
FOX_HEADS = 8
FOX_HEAD_DIM = 64
D_FF = 4 * D_MODEL
Q_BLOCK = 128
ROPE_THETA = 10000.0
NORM_EPS = 1e-6
MAX_POS_OFFSET = 1024
FORGET_BIAS_LO = 1.0
FORGET_BIAS_HI = 6.0

IN_SPLITS = (MLA_Q_LORA, MLA_KV_LORA, MLA_ROPE_DIM,
             FOX_HEADS * FOX_HEAD_DIM, FOX_HEADS * FOX_HEAD_DIM, FOX_HEADS * FOX_HEAD_DIM,
             FOX_HEADS, D_MODEL, D_MODEL)
D_IN = sum(IN_SPLITS)
FORGET_COL_START = sum(IN_SPLITS[:6])

kernel_name = "hybrid_mla_fox_gated_block"


def rms_norm(x, g):
    xf = x.astype(jnp.float32)
    y = xf * lax.rsqrt(jnp.mean(xf * xf, axis=-1, keepdims=True) + NORM_EPS)
    return (y * g.astype(jnp.float32)).astype(x.dtype)


def rope(x, positions):
    half = x.shape[-1] // 2
    inv_freq = ROPE_THETA ** (-jnp.arange(half, dtype=jnp.float32) / half)
    ang = positions.astype(jnp.float32)[:, None, :, None] * inv_freq
    cos, sin = jnp.cos(ang), jnp.sin(ang)
    xf = x.astype(jnp.float32)
    x1, x2 = xf[..., :half], xf[..., half:]
    out = jnp.concatenate([x1 * cos - x2 * sin, x2 * cos + x1 * sin], axis=-1)
    return out.astype(x.dtype)


def causal_block_attention(q, k, v, scale, log_f_cum=None):
    B, H, S, _ = q.shape
    dv = v.shape[-1]
    n_blocks = S // Q_BLOCK
    kf = k.astype(jnp.float32)
    vf = v.astype(jnp.float32)
    k_pos = jnp.arange(S)

    def one_block(i):
        start = i * Q_BLOCK
        qb = lax.dynamic_slice_in_dim(q, start, Q_BLOCK, axis=2).astype(jnp.float32)
        s = jnp.einsum('bhqd,bhkd->bhqk', qb, kf) * scale
        if log_f_cum is not None:
            fq = lax.dynamic_slice_in_dim(log_f_cum, start, Q_BLOCK, axis=2)
            s = s + fq[..., :, None] - log_f_cum[..., None, :]
        q_pos = start + jnp.arange(Q_BLOCK)
        mask = k_pos[None, :] <= q_pos[:, None]
        s = jnp.where(mask, s, -jnp.inf)
        p = jax.nn.softmax(s, axis=-1)
        return jnp.einsum('bhqk,bhkd->bhqd', p, vf)

    out = lax.map(one_block, jnp.arange(n_blocks))
    out = jnp.moveaxis(out, 0, 2).reshape(B, H, S, dv)
    return out.astype(v.dtype)


def split_heads(t, n_heads):
    B, S, _ = t.shape
    return t.reshape(B, S, n_heads, -1).transpose(0, 2, 1, 3)


def merge_heads(t):
    B, H, S, d = t.shape
    return t.transpose(0, 2, 1, 3).reshape(B, S, H * d)


def _fwd_setup_inputs(seed: int = 0) -> dict:
    key = jax.random.key(seed)
    ks = jax.random.split(key, 20)
    L = DEPTH

    def w(k, shape, fan_in):
        return jax.random.normal(k, shape, jnp.float32) * fan_in ** -0.5

    def gain(k, shape):
        return 1.0 + 0.05 * jax.random.normal(k, shape, jnp.float32)

    x = jax.random.normal(ks[0], (BATCH, SEQ, D_MODEL), jnp.float32)
    offset = jax.random.randint(ks[1], (BATCH, 1), 0, MAX_POS_OFFSET, dtype=jnp.int32)
    positions = (offset + jnp.arange(SEQ, dtype=jnp.int32)[None, :]).astype(jnp.int32)

    b_in = 0.02 * jax.random.normal(ks[2], (L, D_IN), jnp.float32)
    forget_bias = jnp.linspace(FORGET_BIAS_LO, FORGET_BIAS_HI, FOX_HEADS, dtype=jnp.float32)
    b_in = b_in.at[:, FORGET_COL_START:FORGET_COL_START + FOX_HEADS].add(forget_bias)

    return {
        "x": x,
        "positions": positions,
        "ln_pre_mix": gain(ks[3], (L, D_MODEL)),
        "ln_post_mix": gain(ks[4], (L, D_MODEL)),
        "ln_pre_mlp": gain(ks[5], (L, D_MODEL)),
        "ln_post_mlp": gain(ks[6], (L, D_MODEL)),
        "w_in": w(ks[7], (L, D_MODEL, D_IN), D_MODEL),
        "b_in": b_in,
        "q_a_norm": gain(ks[8], (L, MLA_Q_LORA)),
        "w_uq": w(ks[9], (L, MLA_Q_LORA, MLA_HEADS * (MLA_NOPE_DIM + MLA_ROPE_DIM)), MLA_Q_LORA),
        "kv_a_norm": gain(ks[10], (L, MLA_KV_LORA)),
        "w_uk": w(ks[11], (L, MLA_KV_LORA, MLA_HEADS * MLA_NOPE_DIM), MLA_KV_LORA),
        "w_uv": w(ks[12], (L, MLA_KV_LORA, MLA_HEADS * MLA_V_DIM), MLA_KV_LORA),
        "w_o_mla": w(ks[13], (L, MLA_HEADS * MLA_V_DIM, D_MODEL), MLA_HEADS * MLA_V_DIM),
        "w_o_fox": w(ks[14], (L, FOX_HEADS * FOX_HEAD_DIM, D_MODEL), FOX_HEADS * FOX_HEAD_DIM),
        "w_out": w(ks[15], (L, D_MODEL, D_MODEL), D_MODEL),
        "w_ff1": w(ks[16], (L, D_MODEL, D_FF), D_MODEL),
        "w_ff2": w(ks[17], (L, D_FF, D_MODEL), D_FF),
    }


def _fwd_reference(x, positions, ln_pre_mix, ln_post_mix, ln_pre_mlp, ln_post_mlp,
              w_in, b_in, q_a_norm, w_uq, kv_a_norm, w_uk, w_uv,
              w_o_mla, w_o_fox, w_out, w_ff1, w_ff2):
    split_idx = [int(v) for v in np.cumsum(IN_SPLITS)[:-1]]
    for l in range(DEPTH):
        h = rms_norm(x, ln_pre_mix[l])
        z = jnp.einsum('bsd,de->bse', h, w_in[l]) + b_in[l]
        (c_q, c_kv, k_r, fq, fk, fv, f_logit, ga_logit, gb_logit) = jnp.split(z, split_idx, axis=-1)

        c_q = rms_norm(c_q, q_a_norm[l])
        q_full = split_heads(jnp.einsum('bsr,re->bse', c_q, w_uq[l]), MLA_HEADS)
        q_nope, q_rot = q_full[..., :MLA_NOPE_DIM], q_full[..., MLA_NOPE_DIM:]
        q_rot = rope(q_rot, positions)
        c_kv = rms_norm(c_kv, kv_a_norm[l])
        k_nope = split_heads(jnp.einsum('bsr,re->bse', c_kv, w_uk[l]), MLA_HEADS)
        v_mla = split_heads(jnp.einsum('bsr,re->bse', c_kv, w_uv[l]), MLA_HEADS)
        k_rot = rope(k_r[:, None, :, :], positions)
        B, H, S, _ = k_nope.shape
        q_mla = jnp.concatenate([q_nope, q_rot], axis=-1)
        k_mla = jnp.concatenate([k_nope, jnp.broadcast_to(k_rot, (B, H, S, MLA_ROPE_DIM))], axis=-1)
        o_mla = causal_block_attention(q_mla, k_mla, v_mla, (MLA_NOPE_DIM + MLA_ROPE_DIM) ** -0.5)
        y_mla = jnp.einsum('bse,ed->bsd', merge_heads(o_mla), w_o_mla[l])

        q_fox = split_heads(fq, FOX_HEADS)
        k_fox = split_heads(fk, FOX_HEADS)
        v_fox = split_heads(fv, FOX_HEADS)
        log_f = jax.nn.log_sigmoid(f_logit.astype(jnp.float32))
        log_f_cum = jnp.cumsum(log_f, axis=1).transpose(0, 2, 1)
        o_fox = causal_block_attention(q_fox, k_fox, v_fox, FOX_HEAD_DIM ** -0.5, log_f_cum)
        y_fox = jnp.einsum('bse,ed->bsd', merge_heads(o_fox), w_o_fox[l])

        merged = jax.nn.sigmoid(ga_logit) * y_mla + jax.nn.sigmoid(gb_logit) * y_fox
        mix_out = jnp.einsum('bsd,de->bse', merged, w_out[l])
        x = x + rms_norm(mix_out, ln_post_mix[l])

        h2 = rms_norm(x, ln_pre_mlp[l])
        u = jnp.einsum('bsd,df->bsf', h2, w_ff1[l])
        m = jnp.einsum('bsf,fd->bsd', jnp.square(jax.nn.relu(u)), w_ff2[l])
        x = x + rms_norm(m, ln_post_mlp[l])
    return x


import jax as _jax
import jax.numpy as _jnp

TWIN_FORMAT = 'train_step'
FWD_PARAMS = ['x', 'positions', 'ln_pre_mix', 'ln_post_mix', 'ln_pre_mlp', 'ln_post_mlp', 'w_in', 'b_in', 'q_a_norm', 'w_uq', 'kv_a_norm', 'w_uk', 'w_uv', 'w_o_mla', 'w_o_fox', 'w_out', 'w_ff1', 'w_ff2']
TWIN_WEIGHTS = ['ln_pre_mix', 'ln_post_mix', 'ln_pre_mlp', 'ln_post_mlp', 'w_in', 'b_in', 'q_a_norm', 'w_uq', 'kv_a_norm', 'w_uk', 'w_uv', 'w_o_mla', 'w_o_fox', 'w_out', 'w_ff1', 'w_ff2']
TWIN_DIFF_INPUT = 'x'
TWIN_INPUTS = ['x', 'positions', 'ln_pre_mix', 'ln_post_mix', 'ln_pre_mlp', 'ln_post_mlp', 'w_in', 'b_in', 'q_a_norm', 'w_uq', 'kv_a_norm', 'w_uk', 'w_uv', 'w_o_mla', 'w_o_fox', 'w_out', 'w_ff1', 'w_ff2', 'loss_target', 'm_ln_pre_mix', 'm_ln_post_mix', 'm_ln_pre_mlp', 'm_ln_post_mlp', 'm_w_in', 'm_b_in', 'm_q_a_norm', 'm_w_uq', 'm_kv_a_norm', 'm_w_uk', 'm_w_uv', 'm_w_o_mla', 'm_w_o_fox', 'm_w_out', 'm_w_ff1', 'm_w_ff2', 'v_ln_pre_mix', 'v_ln_post_mix', 'v_ln_pre_mlp', 'v_ln_post_mlp', 'v_w_in', 'v_b_in', 'v_q_a_norm', 'v_w_uq', 'v_kv_a_norm', 'v_w_uk', 'v_w_uv', 'v_w_o_mla', 'v_w_o_fox', 'v_w_out', 'v_w_ff1', 'v_w_ff2']
TWIN_OUTPUTS = ['loss', 'grad_x', 'grad_ln_pre_mix', 'grad_ln_post_mix', 'grad_ln_pre_mlp', 'grad_ln_post_mlp', 'grad_w_in', 'grad_b_in', 'grad_q_a_norm', 'grad_w_uq', 'grad_kv_a_norm', 'grad_w_uk', 'grad_w_uv', 'grad_w_o_mla', 'grad_w_o_fox', 'grad_w_out', 'grad_w_ff1', 'grad_w_ff2', 'delta_ln_pre_mix', 'delta_ln_post_mix', 'delta_ln_pre_mlp', 'delta_ln_post_mlp', 'delta_w_in', 'delta_b_in', 'delta_q_a_norm', 'delta_w_uq', 'delta_kv_a_norm', 'delta_w_uk', 'delta_w_uv', 'delta_w_o_mla', 'delta_w_o_fox', 'delta_w_out', 'delta_w_ff1', 'delta_w_ff2', 'new_m_ln_pre_mix', 'new_m_ln_post_mix', 'new_m_ln_pre_mlp', 'new_m_ln_post_mlp', 'new_m_w_in', 'new_m_b_in', 'new_m_q_a_norm', 'new_m_w_uq', 'new_m_kv_a_norm', 'new_m_w_uk', 'new_m_w_uv', 'new_m_w_o_mla', 'new_m_w_o_fox', 'new_m_w_out', 'new_m_w_ff1', 'new_m_w_ff2', 'new_v_ln_pre_mix', 'new_v_ln_post_mix', 'new_v_ln_pre_mlp', 'new_v_ln_post_mlp', 'new_v_w_in', 'new_v_b_in', 'new_v_q_a_norm', 'new_v_w_uq', 'new_v_kv_a_norm', 'new_v_w_uk', 'new_v_w_uv', 'new_v_w_o_mla', 'new_v_w_o_fox', 'new_v_w_out', 'new_v_w_ff1', 'new_v_w_ff2']
TWIN_LEAF_KINDS = {'loss': 'loss', 'grad_x': 'grad_x', 'grad_ln_pre_mix': 'grad_w', 'grad_ln_post_mix': 'grad_w', 'grad_ln_pre_mlp': 'grad_w', 'grad_ln_post_mlp': 'grad_w', 'grad_w_in': 'grad_w', 'grad_b_in': 'grad_w', 'grad_q_a_norm': 'grad_w', 'grad_w_uq': 'grad_w', 'grad_kv_a_norm': 'grad_w', 'grad_w_uk': 'grad_w', 'grad_w_uv': 'grad_w', 'grad_w_o_mla': 'grad_w', 'grad_w_o_fox': 'grad_w', 'grad_w_out': 'grad_w', 'grad_w_ff1': 'grad_w', 'grad_w_ff2': 'grad_w', 'delta_ln_pre_mix': 'delta_w', 'delta_ln_post_mix': 'delta_w', 'delta_ln_pre_mlp': 'delta_w', 'delta_ln_post_mlp': 'delta_w', 'delta_w_in': 'delta_w', 'delta_b_in': 'delta_w', 'delta_q_a_norm': 'delta_w', 'delta_w_uq': 'delta_w', 'delta_kv_a_norm': 'delta_w', 'delta_w_uk': 'delta_w', 'delta_w_uv': 'delta_w', 'delta_w_o_mla': 'delta_w', 'delta_w_o_fox': 'delta_w', 'delta_w_out': 'delta_w', 'delta_w_ff1': 'delta_w', 'delta_w_ff2': 'delta_w', 'new_m_ln_pre_mix': 'new_m', 'new_m_ln_post_mix': 'new_m', 'new_m_ln_pre_mlp': 'new_m', 'new_m_ln_post_mlp': 'new_m', 'new_m_w_in': 'new_m', 'new_m_b_in': 'new_m', 'new_m_q_a_norm': 'new_m', 'new_m_w_uq': 'new_m', 'new_m_kv_a_norm': 'new_m', 'new_m_w_uk': 'new_m', 'new_m_w_uv': 'new_m', 'new_m_w_o_mla': 'new_m', 'new_m_w_o_fox': 'new_m', 'new_m_w_out': 'new_m', 'new_m_w_ff1': 'new_m', 'new_m_w_ff2': 'new_m', 'new_v_ln_pre_mix': 'new_v', 'new_v_ln_post_mix': 'new_v', 'new_v_ln_pre_mlp': 'new_v', 'new_v_ln_post_mlp': 'new_v', 'new_v_w_in': 'new_v', 'new_v_b_in': 'new_v', 'new_v_q_a_norm': 'new_v', 'new_v_w_uq': 'new_v', 'new_v_kv_a_norm': 'new_v', 'new_v_w_uk': 'new_v', 'new_v_w_uv': 'new_v', 'new_v_w_o_mla': 'new_v', 'new_v_w_o_fox': 'new_v', 'new_v_w_out': 'new_v', 'new_v_w_ff1': 'new_v', 'new_v_w_ff2': 'new_v'}


def _forward(args):
    return _fwd_reference(*[args[k] for k in FWD_PARAMS])


def _output_shape():
    def fwd():
        inp = _fwd_setup_inputs(0)
        return _fwd_reference(*[inp[k] for k in FWD_PARAMS])
    out = _jax.eval_shape(fwd)
    return out.shape, out.dtype

N_MICROBATCH = 1
ADAM_LR = 0.001
ADAM_B1 = 0.9
ADAM_B2 = 0.999
ADAM_EPS = 1e-08
ADAM_WD = 0.01
ADAM_STEP = 10
PER_EXAMPLE_BATCH_AXIS = {'x': 0, 'positions': 0, 'loss_target': 0}
SHARED_INPUTS = []
_WEIGHT_DTYPES = {'ln_pre_mix': _jnp.float32, 'ln_post_mix': _jnp.float32, 'ln_pre_mlp': _jnp.float32, 'ln_post_mlp': _jnp.float32, 'w_in': _jnp.float32, 'b_in': _jnp.float32, 'q_a_norm': _jnp.float32, 'w_uq': _jnp.float32, 'kv_a_norm': _jnp.float32, 'w_uk': _jnp.float32, 'w_uv': _jnp.float32, 'w_o_mla': _jnp.float32, 'w_o_fox': _jnp.float32, 'w_out': _jnp.float32, 'w_ff1': _jnp.float32, 'w_ff2': _jnp.float32}
MOMENT_SCALE = {'ln_pre_mix': 1.329447e+00, 'ln_post_mix': 6.429810e+01, 'ln_pre_mlp': 2.069553e+00, 'ln_post_mlp': 6.589424e+01, 'w_in': 6.482502e-01, 'b_in': 4.827829e+01, 'q_a_norm': 6.264555e-01, 'w_uq': 3.441735e-01, 'kv_a_norm': 5.631699e+00, 'w_uk': 4.608030e-01, 'w_uv': 2.133117e+00, 'w_o_mla': 1.546168e+00, 'w_o_fox': 2.241955e+00, 'w_out': 2.847939e+00, 'w_ff1': 9.996912e-01, 'w_ff2': 3.023350e+00}


def _to_microbatches(a, axis):
    t = _jnp.moveaxis(a, axis, 0)
    t = t.reshape((N_MICROBATCH, t.shape[0] // N_MICROBATCH) + t.shape[1:])
    return _jnp.moveaxis(t, 1, axis + 1)


def setup_inputs(seed: int = 0) -> dict:
    inp = _fwd_setup_inputs(seed)
    key = _jax.random.fold_in(_jax.random.key(seed), 7919)
    shape, _ = _output_shape()
    out = dict(inp)
    out["loss_target"] = _jax.random.normal(_jax.random.fold_in(key, 0), shape, _jnp.float32)
    for i, name in enumerate(TWIN_WEIGHTS):
        w = inp[name].astype(_jnp.float32)
        if MOMENT_SCALE is None:
            s = _jnp.sqrt(_jnp.mean(_jnp.square(w)) + 1e-30)
        else:
            s = MOMENT_SCALE[name]
        km, kv = _jax.random.split(_jax.random.fold_in(key, i + 1))
        out[name] = w
        out["m_" + name] = s * _jax.random.normal(km, w.shape, _jnp.float32)
        out["v_" + name] = (s * s) * _jax.random.uniform(kv, w.shape, _jnp.float32, 0.5, 1.5)
    if N_MICROBATCH > 1:
        for name, axis in PER_EXAMPLE_BATCH_AXIS.items():
            out[name] = _to_microbatches(out[name], axis)
    return {'x': out['x'], 'positions': out['positions'], 'ln_pre_mix': out['ln_pre_mix'], 'ln_post_mix': out['ln_post_mix'], 'ln_pre_mlp': out['ln_pre_mlp'], 'ln_post_mlp': out['ln_post_mlp'], 'w_in': out['w_in'], 'b_in': out['b_in'], 'q_a_norm': out['q_a_norm'], 'w_uq': out['w_uq'], 'kv_a_norm': out['kv_a_norm'], 'w_uk': out['w_uk'], 'w_uv': out['w_uv'], 'w_o_mla': out['w_o_mla'], 'w_o_fox': out['w_o_fox'], 'w_out': out['w_out'], 'w_ff1': out['w_ff1'], 'w_ff2': out['w_ff2'], 'loss_target': out['loss_target'], 'm_ln_pre_mix': out['m_ln_pre_mix'], 'm_ln_post_mix': out['m_ln_post_mix'], 'm_ln_pre_mlp': out['m_ln_pre_mlp'], 'm_ln_post_mlp': out['m_ln_post_mlp'], 'm_w_in': out['m_w_in'], 'm_b_in': out['m_b_in'], 'm_q_a_norm': out['m_q_a_norm'], 'm_w_uq': out['m_w_uq'], 'm_kv_a_norm': out['m_kv_a_norm'], 'm_w_uk': out['m_w_uk'], 'm_w_uv': out['m_w_uv'], 'm_w_o_mla': out['m_w_o_mla'], 'm_w_o_fox': out['m_w_o_fox'], 'm_w_out': out['m_w_out'], 'm_w_ff1': out['m_w_ff1'], 'm_w_ff2': out['m_w_ff2'], 'v_ln_pre_mix': out['v_ln_pre_mix'], 'v_ln_post_mix': out['v_ln_post_mix'], 'v_ln_pre_mlp': out['v_ln_pre_mlp'], 'v_ln_post_mlp': out['v_ln_post_mlp'], 'v_w_in': out['v_w_in'], 'v_b_in': out['v_b_in'], 'v_q_a_norm': out['v_q_a_norm'], 'v_w_uq': out['v_w_uq'], 'v_kv_a_norm': out['v_kv_a_norm'], 'v_w_uk': out['v_w_uk'], 'v_w_uv': out['v_w_uv'], 'v_w_o_mla': out['v_w_o_mla'], 'v_w_o_fox': out['v_w_o_fox'], 'v_w_out': out['v_w_out'], 'v_w_ff1': out['v_w_ff1'], 'v_w_ff2': out['v_w_ff2']}


def _loss(weights, diff, rest, loss_target):
    with _jax.named_scope("forward"):
        args = {**rest, TWIN_DIFF_INPUT: diff, **{k: w.astype(_WEIGHT_DTYPES[k]) for k, w in weights.items()}}
        y = _forward(args)
    with _jax.named_scope("loss_head"):
        err = _jnp.square(y.astype(_jnp.float32) - loss_target)
        return 0.5 * _jnp.sum(_jnp.mean(err, axis=-1)) if err.ndim else 0.5 * err


def _adamw(w, g, m, v):
    m = ADAM_B1 * m + (1.0 - ADAM_B1) * g
    v = ADAM_B2 * v + (1.0 - ADAM_B2) * _jnp.square(g)
    m_hat = m / (1.0 - ADAM_B1 ** ADAM_STEP)
    v_hat = v / (1.0 - ADAM_B2 ** ADAM_STEP)
    delta = -ADAM_LR * (m_hat / (_jnp.sqrt(v_hat) + ADAM_EPS) + ADAM_WD * w)
    return delta, m, v


def reference(x, positions, ln_pre_mix, ln_post_mix, ln_pre_mlp, ln_post_mlp, w_in, b_in, q_a_norm, w_uq, kv_a_norm, w_uk, w_uv, w_o_mla, w_o_fox, w_out, w_ff1, w_ff2, loss_target, m_ln_pre_mix, m_ln_post_mix, m_ln_pre_mlp, m_ln_post_mlp, m_w_in, m_b_in, m_q_a_norm, m_w_uq, m_kv_a_norm, m_w_uk, m_w_uv, m_w_o_mla, m_w_o_fox, m_w_out, m_w_ff1, m_w_ff2, v_ln_pre_mix, v_ln_post_mix, v_ln_pre_mlp, v_ln_post_mlp, v_w_in, v_b_in, v_q_a_norm, v_w_uq, v_kv_a_norm, v_w_uk, v_w_uv, v_w_o_mla, v_w_o_fox, v_w_out, v_w_ff1, v_w_ff2):
    given = dict(x=x, positions=positions, ln_pre_mix=ln_pre_mix, ln_post_mix=ln_post_mix, ln_pre_mlp=ln_pre_mlp, ln_post_mlp=ln_post_mlp, w_in=w_in, b_in=b_in, q_a_norm=q_a_norm, w_uq=w_uq, kv_a_norm=kv_a_norm, w_uk=w_uk, w_uv=w_uv, w_o_mla=w_o_mla, w_o_fox=w_o_fox, w_out=w_out, w_ff1=w_ff1, w_ff2=w_ff2, loss_target=loss_target, m_ln_pre_mix=m_ln_pre_mix, m_ln_post_mix=m_ln_post_mix, m_ln_pre_mlp=m_ln_pre_mlp, m_ln_post_mlp=m_ln_post_mlp, m_w_in=m_w_in, m_b_in=m_b_in, m_q_a_norm=m_q_a_norm, m_w_uq=m_w_uq, m_kv_a_norm=m_kv_a_norm, m_w_uk=m_w_uk, m_w_uv=m_w_uv, m_w_o_mla=m_w_o_mla, m_w_o_fox=m_w_o_fox, m_w_out=m_w_out, m_w_ff1=m_w_ff1, m_w_ff2=m_w_ff2, v_ln_pre_mix=v_ln_pre_mix, v_ln_post_mix=v_ln_post_mix, v_ln_pre_mlp=v_ln_pre_mlp, v_ln_post_mlp=v_ln_post_mlp, v_w_in=v_w_in, v_b_in=v_b_in, v_q_a_norm=v_q_a_norm, v_w_uq=v_w_uq, v_kv_a_norm=v_kv_a_norm, v_w_uk=v_w_uk, v_w_uv=v_w_uv, v_w_o_mla=v_w_o_mla, v_w_o_fox=v_w_o_fox, v_w_out=v_w_out, v_w_ff1=v_w_ff1, v_w_ff2=v_w_ff2)
    weights = {n: given[n] for n in TWIN_WEIGHTS}
    shared = {n: given[n] for n in SHARED_INPUTS}
    per_example = {n: given[n] for n in ['x', 'positions']}
    grad_fn = _jax.value_and_grad(_loss, argnums=(0, 1))

    def one_microbatch(ex, loss_target):
        ex = dict(ex)
        diff = ex.pop(TWIN_DIFF_INPUT)
        return grad_fn(weights, diff, {**shared, **ex}, loss_target)

    if N_MICROBATCH == 1:
        loss, (grad_w, grad_x) = one_microbatch(per_example, given["loss_target"])
    else:
        def body(carry, xs):
            loss_sum, grad_sum = carry
            l_k, (gw_k, gx_k) = one_microbatch(xs[0], xs[1])
            with _jax.named_scope("update"):
                return (loss_sum + l_k, _jax.tree.map(_jnp.add, grad_sum, gw_k)), gx_k

        init = (_jnp.zeros((), _jnp.float32), _jax.tree.map(_jnp.zeros_like, weights))
        (loss, grad_w), grad_x = _jax.lax.scan(body, init, (per_example, given["loss_target"]))
    with _jax.named_scope("update"):
        delta_w, new_m, new_v = {}, {}, {}
        for n in TWIN_WEIGHTS:
            delta_w[n], new_m[n], new_v[n] = _adamw(weights[n], grad_w[n], given["m_" + n], given["v_" + n])
    return (loss, grad_x, *[grad_w[n] for n in TWIN_WEIGHTS], *[delta_w[n] for n in TWIN_WEIGHTS],
            *[new_m[n] for n in TWIN_WEIGHTS], *[new_v[n] for n in TWIN_WEIGHTS])
```

```python
import functools

import jax
import jax.numpy as jnp
from jax import lax
from jax.experimental import pallas as pl
from jax.experimental.pallas import tpu as pltpu

F32 = jnp.float32
BF16 = jnp.bfloat16

MLA_HEADS = 8
MLA_Q_LORA = 256
MLA_KV_LORA = 128
MLA_NOPE = 64
MLA_ROPE = 32
MLA_V = 64
FOX_HEADS = 8
FOX_DIM = 64
ROPE_THETA = 10000.0
NORM_EPS = 1e-6
HALF_ROPE = MLA_ROPE // 2

ADAM_LR = 0.001
ADAM_B1 = 0.9
ADAM_B2 = 0.999
ADAM_EPS = 1e-08
ADAM_WD = 0.01
ADAM_STEP = 10

LANES = 128
VMEM_LIMIT = 56 * 1024 * 1024
ATT_TILE = 512
NEG = -1e30
MESH = pl.DeviceIdType.MESH

BIG = (("w_in", 2), ("w_uq", 2), ("w_uk", 2), ("w_uv", 2), ("w_o_mla", 2), ("w_o_fox", 2),
       ("w_out", 1), ("w_ff1", 2), ("w_ff2", 1))
SMALL = ("ln_pre_mix", "ln_post_mix", "ln_pre_mlp", "ln_post_mlp", "b_in", "q_a_norm", "kv_a_norm")
ALL_W = ("ln_pre_mix", "ln_post_mix", "ln_pre_mlp", "ln_post_mlp", "w_in", "b_in", "q_a_norm", "w_uq",
         "kv_a_norm", "w_uk", "w_uv", "w_o_mla", "w_o_fox", "w_out", "w_ff1", "w_ff2")
N_CHIPS = 4
PACK_W = 1024
SMALL_ROWS = 16


def _cparams(sem=None):
    return pltpu.CompilerParams(dimension_semantics=sem, vmem_limit_bytes=VMEM_LIMIT)


def _divisor_tile(n, limit, mult):
    if n <= limit:
        return n
    best = None
    t = mult
    while t <= limit:
        if n % t == 0:
            best = t
        t += mult
    assert best is not None, (n, limit, mult)
    return best


def _mm(a, b, *, out_dtype, name, bias=None):
    m, k = a.shape
    k2, n = b.shape
    assert k == k2 and a.dtype == BF16 and b.dtype == BF16
    tm = _divisor_tile(m, 512, 16)
    tn = _divisor_tile(n, 512, LANES)

    def body(*refs):
        if bias is None:
            a_ref, b_ref, o_ref = refs
        else:
            a_ref, b_ref, bias_ref, o_ref = refs
        acc = jnp.dot(a_ref[...], b_ref[...], preferred_element_type=F32)
        if bias is not None:
            acc = acc + bias_ref[...]
        o_ref[...] = acc.astype(o_ref.dtype)

    in_specs = [pl.BlockSpec((tm, k), lambda i, j: (i, 0)), pl.BlockSpec((k, tn), lambda i, j: (0, j))]
    args = [a, b]
    if bias is not None:
        in_specs.append(pl.BlockSpec((1, tn), lambda i, j: (0, j)))
        args.append(bias)
    return pl.pallas_call(
        body, grid=(m // tm, n // tn), in_specs=in_specs,
        out_specs=pl.BlockSpec((tm, tn), lambda i, j: (i, j)),
        out_shape=jax.ShapeDtypeStruct((m, n), out_dtype),
        compiler_params=_cparams(("parallel", "parallel")), name=name)(*args)


def _mm_tn(a, b, *, name):
    s, m = a.shape
    s2, n = b.shape
    assert s == s2 and a.dtype == BF16 and b.dtype == BF16
    tm = _divisor_tile(m, 1024, LANES)
    tn = _divisor_tile(n, 512, LANES)
    tk = _divisor_tile(s, 512, 16)

    def body(a_ref, b_ref, o_ref):
        @pl.when(pl.program_id(2) == 0)
        def _():
            o_ref[...] = jnp.zeros_like(o_ref)

        o_ref[...] += lax.dot_general(a_ref[...], b_ref[...], (((0,), (0,)), ((), ())),
                                      preferred_element_type=F32)

    return pl.pallas_call(
        body, grid=(m // tm, n // tn, s // tk),
        in_specs=[pl.BlockSpec((tk, tm), lambda i, j, k: (k, i)), pl.BlockSpec((tk, tn), lambda i, j, k: (k, j))],
        out_specs=pl.BlockSpec((tm, tn), lambda i, j, k: (i, j)),
        out_shape=jax.ShapeDtypeStruct((m, n), F32),
        compiler_params=_cparams(("parallel", "parallel", "arbitrary")), name=name)(a, b)


def _rowwise(name, fn, tiled, params, outs, reds=()):
    wins = [t if isinstance(t, tuple) else (t, 0, t.shape[1]) for t in tiled]
    s = wins[0][0].shape[0]
    row_bytes = sum(w * arr.dtype.itemsize for arr, _, w in wins) + sum(w * jnp.dtype(d).itemsize for w, d in outs)
    ts = _divisor_tile(s, max(16, min(1024, (6 * 1024 * 1024) // row_bytes)), 16)
    nt, npar, nout = len(wins), len(params), len(outs)

    def body(*refs):
        tin = [r[...] for r in refs[:nt]]
        par = [r[...] for r in refs[nt:nt + npar]]
        out_refs = refs[nt + npar:nt + npar + nout]
        red_refs = refs[nt + npar + nout:]
        o, r = fn(tin, par)
        for ref, val in zip(out_refs, o, strict=True):
            ref[...] = val.astype(ref.dtype)
        if red_refs:
            @pl.when(pl.program_id(0) == 0)
            def _():
                for ref in red_refs:
                    ref[...] = jnp.zeros_like(ref)

            for ref, val in zip(red_refs, r, strict=True):
                ref[...] += val

    in_specs = [pl.BlockSpec((ts, w), functools.partial(lambda i, cb: (i, cb), cb=cb)) for _, cb, w in wins]
    in_specs += [pl.BlockSpec(p.shape, lambda i: (0, 0)) for p in params]
    out_specs = [pl.BlockSpec((ts, w), lambda i: (i, 0)) for w, _ in outs]
    out_specs += [pl.BlockSpec((1, w), lambda i: (0, 0)) for w in reds]
    out_shape = [jax.ShapeDtypeStruct((s, w), d) for w, d in outs]
    out_shape += [jax.ShapeDtypeStruct((1, w), F32) for w in reds]
    res = pl.pallas_call(
        body, grid=(s // ts,), in_specs=in_specs, out_specs=out_specs, out_shape=out_shape,
        compiler_params=_cparams(("arbitrary",)), name=name)(*[w[0] for w in wins], *params)
    return res


def _rms(x, g):
    r = lax.rsqrt(jnp.mean(x * x, axis=-1, keepdims=True) + NORM_EPS)
    return x * r * g, r


def _rms_bwd(x, g, dy):
    r = lax.rsqrt(jnp.mean(x * x, axis=-1, keepdims=True) + NORM_EPS)
    gy = dy * g
    dx = r * gy - x * (r * r * r) * jnp.mean(x * gy, axis=-1, keepdims=True)
    dg = jnp.sum(dy * (x * r), axis=0, keepdims=True)
    return dx, dg


def _sigmoid(x):
    return 1.0 / (1.0 + jnp.exp(-x))


def _forget_cumsum(z, cb):
    s = z.shape[0]
    ts = _divisor_tile(s, 512, LANES)

    def body(x_ref, col_ref, row_ref, carry):
        @pl.when(pl.program_id(0) == 0)
        def _():
            carry[...] = jnp.zeros_like(carry)

        x = x_ref[...]
        lf = jnp.minimum(x, 0.0) - jnp.log1p(jnp.exp(-jnp.abs(x)))
        r = lax.broadcasted_iota(jnp.int32, (ts, ts), 0)
        c = lax.broadcasted_iota(jnp.int32, (ts, ts), 1)
        tri = jnp.where(c <= r, 1.0, 0.0).astype(F32)
        cs = jnp.dot(tri, lf, preferred_element_type=F32, precision=lax.Precision.HIGHEST) + carry[...]
        carry[...] += jnp.sum(lf, axis=0, keepdims=True)
        col_ref[...] = cs
        row_ref[...] = cs.T

    return pl.pallas_call(
        body, grid=(s // ts,),
        in_specs=[pl.BlockSpec((ts, LANES), lambda i: (i, cb))],
        out_specs=[pl.BlockSpec((ts, LANES), lambda i: (i, 0)), pl.BlockSpec((LANES, ts), lambda i: (0, i))],
        out_shape=[jax.ShapeDtypeStruct((s, LANES), F32), jax.ShapeDtypeStruct((LANES, s), F32)],
        scratch_shapes=[pltpu.VMEM((1, LANES), F32)],
        compiler_params=_cparams(("arbitrary",)), name="forget_cumsum")(z)


def _forget_cumsum_bwd(dfq, dfs, z, cb):
    s = z.shape[0]
    ts = _divisor_tile(s, 512, LANES)
    nt = s // ts

    def body(dq_ref, d_ref, x_ref, o_ref, carry):
        @pl.when(pl.program_id(0) == 0)
        def _():
            carry[...] = jnp.zeros_like(carry)

        r = lax.broadcasted_iota(jnp.int32, (ts, ts), 0)
        c = lax.broadcasted_iota(jnp.int32, (ts, ts), 1)
        tri = jnp.where(c >= r, 1.0, 0.0).astype(F32)
        dv = dq_ref[...] - d_ref[...]
        rc = jnp.dot(tri, dv, preferred_element_type=F32, precision=lax.Precision.HIGHEST) + carry[...]
        carry[...] += jnp.sum(dv, axis=0, keepdims=True)
        o_ref[...] = (rc * (1.0 / (1.0 + jnp.exp(x_ref[...])))).astype(o_ref.dtype)

    return pl.pallas_call(
        body, grid=(nt,),
        in_specs=[pl.BlockSpec((ts, LANES), lambda i: (nt - 1 - i, 0)),
                  pl.BlockSpec((ts, LANES), lambda i: (nt - 1 - i, 0)),
                  pl.BlockSpec((ts, LANES), lambda i: (nt - 1 - i, cb))],
        out_specs=pl.BlockSpec((ts, LANES), lambda i: (nt - 1 - i, 0)),
        out_shape=jax.ShapeDtypeStruct((s, LANES), BF16),
        scratch_shapes=[pltpu.VMEM((1, LANES), F32)],
        compiler_params=_cparams(("arbitrary",)), name="forget_cumsum_bwd")(dfq, dfs, z)


_NT = (((1,), (1,)), ((), ()))


def _head_column(fcol_tile, h):
    lane = lax.broadcasted_iota(jnp.int32, fcol_tile.shape, 1)
    return jnp.sum(jnp.where(lane == h, fcol_tile, 0.0), axis=1, keepdims=True)


def _flash_fwd(q, k, v, scale, fcol=None, frow=None, *, name):
    nh, s, dk = q.shape
    dv = v.shape[2]
    t = min(ATT_TILE, s)
    nq = s // t
    fox = fcol is not None

    def body(*refs):
        if fox:
            q_ref, k_ref, v_ref, fcol_ref, frow_ref, o_ref, lse_ref = refs
        else:
            q_ref, k_ref, v_ref, o_ref, lse_ref = refs
        h = pl.program_id(0)
        i = pl.program_id(1)
        qb = q_ref[0]
        if fox:
            fq = _head_column(fcol_ref[...], h)

        def scores(j):
            kb = k_ref[0, pl.ds(pl.multiple_of(j * t, t), t), :]
            sc = lax.dot_general(qb, kb, _NT, preferred_element_type=F32) * scale
            if fox:
                sc = sc + fq - frow_ref[0, j]
            return sc

        def update(j, sc, carry):
            m, l, acc = carry
            m_new = jnp.maximum(m, jnp.max(sc, axis=1, keepdims=True))
            p = jnp.exp(sc - m_new)
            alpha = jnp.exp(m - m_new)
            l = alpha * l + jnp.sum(p, axis=1, keepdims=True)
            vb = v_ref[0, pl.ds(pl.multiple_of(j * t, t), t), :]
            acc = alpha * acc + jnp.dot(p.astype(BF16), vb, preferred_element_type=F32)
            return m_new, l, acc

        init = (jnp.full((t, 1), NEG, F32), jnp.zeros((t, 1), F32), jnp.zeros((t, dv), F32))
        carry = lax.fori_loop(0, i, lambda j, c: update(j, scores(j), c), init)
        row = lax.broadcasted_iota(jnp.int32, (t, t), 0)
        col = lax.broadcasted_iota(jnp.int32, (t, t), 1)
        m, l, acc = update(i, jnp.where(col <= row, scores(i), NEG), carry)
        o_ref[0] = (acc / l).astype(o_ref.dtype)
        lse_ref[0] = m + jnp.log(l)

    in_specs = [pl.BlockSpec((1, t, dk), lambda h, i: (h, i, 0)),
                pl.BlockSpec((1, s, dk), lambda h, i: (h, 0, 0)),
                pl.BlockSpec((1, s, dv), lambda h, i: (h, 0, 0))]
    args = [q, k, v]
    if fox:
        in_specs += [pl.BlockSpec((t, LANES), lambda h, i: (i, 0)),
                     pl.BlockSpec((1, nq, 1, t), lambda h, i: (h, 0, 0, 0))]
        args += [fcol, frow]
    return pl.pallas_call(
        body, grid=(nh, nq), in_specs=in_specs,
        out_specs=[pl.BlockSpec((1, t, dv), lambda h, i: (h, i, 0)), pl.BlockSpec((1, t, 1), lambda h, i: (h, i, 0))],
        out_shape=[jax.ShapeDtypeStruct((nh, s, dv), BF16), jax.ShapeDtypeStruct((nh, s, 1), F32)],
        compiler_params=_cparams(("parallel", "arbitrary")), name=name)(*args)


def _flash_dq(q, k, v, do, o, lse, scale, fcol=None, frow=None, *, name):
    nh, s, dk = q.shape
    dv = v.shape[2]
    t = min(ATT_TILE, s)
    nq = s // t
    fox = fcol is not None

    def body(*refs):
        if fox:
            q_ref, k_ref, v_ref, do_ref, o_ref, lse_ref, fcol_ref, frow_ref, dq_ref, delta_ref = refs
        else:
            q_ref, k_ref, v_ref, do_ref, o_ref, lse_ref, dq_ref, delta_ref = refs
        h = pl.program_id(0)
        i = pl.program_id(1)
        qb = q_ref[0]
        dob = do_ref[0]
        lse_b = lse_ref[0]
        delta = jnp.sum(dob.astype(F32) * o_ref[0].astype(F32), axis=1, keepdims=True)
        if fox:
            fq = _head_column(fcol_ref[...], h)

        def block(j, dq, masked):
            kb = k_ref[0, pl.ds(pl.multiple_of(j * t, t), t), :]
            vb = v_ref[0, pl.ds(pl.multiple_of(j * t, t), t), :]
            sc = lax.dot_general(qb, kb, _NT, preferred_element_type=F32) * scale
            if fox:
                sc = sc + fq - frow_ref[0, j]
            if masked:
                row = lax.broadcasted_iota(jnp.int32, (t, t), 0)
                col = lax.broadcasted_iota(jnp.int32, (t, t), 1)
                sc = jnp.where(col <= row, sc, NEG)
            p = jnp.exp(sc - lse_b)
            dp = lax.dot_general(dob, vb, _NT, preferred_element_type=F32)
            ds = p * (dp - delta)
            return dq + jnp.dot(ds.astype(BF16), kb, preferred_element_type=F32)

        dq = lax.fori_loop(0, i, lambda j, c: block(j, c, False), jnp.zeros((t, dk), F32))
        dq = block(i, dq, True)
        dq_ref[0] = dq * scale
        delta_ref[0] = delta

    in_specs = [pl.BlockSpec((1, t, dk), lambda h, i: (h, i, 0)),
                pl.BlockSpec((1, s, dk), lambda h, i: (h, 0, 0)),
                pl.BlockSpec((1, s, dv), lambda h, i: (h, 0, 0)),
                pl.BlockSpec((1, t, dv), lambda h, i: (h, i, 0)),
                pl.BlockSpec((1, t, dv), lambda h, i: (h, i, 0)),
                pl.BlockSpec((1, t, 1), lambda h, i: (h, i, 0))]
    args = [q, k, v, do, o, lse]
    if fox:
        in_specs += [pl.BlockSpec((t, LANES), lambda h, i: (i, 0)),
                     pl.BlockSpec((1, nq, 1, t), lambda h, i: (h, 0, 0, 0))]
        args += [fcol, frow]
    return pl.pallas_call(
        body, grid=(nh, nq), in_specs=in_specs,
        out_specs=[pl.BlockSpec((1, t, dk), lambda h, i: (h, i, 0)), pl.BlockSpec((1, t, 1), lambda h, i: (h, i, 0))],
        out_shape=[jax.ShapeDtypeStruct((nh, s, dk), F32), jax.ShapeDtypeStruct((nh, s, 1), F32)],
        compiler_params=_cparams(("parallel", "arbitrary")), name=name)(*args)


def _flash_dkdv(q, k, v, do, lse_row, delta_row, scale, fcol=None, frow=None, *, name):
    nh, s, dk = q.shape
    dv = v.shape[2]
    t = min(ATT_TILE, s)
    nq = s // t
    fox = fcol is not None

    def body(*refs):
        if fox:
            q_ref, k_ref, v_ref, do_ref, lse_ref, delta_ref, fcol_ref, frow_ref, dk_ref, dv_ref, dfs_ref, dfq_ref = refs
        else:
            q_ref, k_ref, v_ref, do_ref, lse_ref, delta_ref, dk_ref, dv_ref = refs
        h = pl.program_id(0)
        j = pl.program_id(1)
        kb = k_ref[0]
        vb = v_ref[0]
        if fox:
            fk = _head_column(fcol_ref[...], h)

            @pl.when(j == 0)
            def _():
                dfq_ref[...] = jnp.zeros_like(dfq_ref)

        def block(i, carry, masked):
            dk_acc, dv_acc, dfs_acc = carry
            qb = q_ref[0, pl.ds(pl.multiple_of(i * t, t), t), :]
            dob = do_ref[0, pl.ds(pl.multiple_of(i * t, t), t), :]
            st = lax.dot_general(kb, qb, _NT, preferred_element_type=F32) * scale
            if fox:
                st = st + frow_ref[0, i] - fk
            if masked:
                key = lax.broadcasted_iota(jnp.int32, (t, t), 0)
                qry = lax.broadcasted_iota(jnp.int32, (t, t), 1)
                st = jnp.where(key <= qry, st, NEG)
            pt = jnp.exp(st - lse_ref[0, i])
            dv_acc = dv_acc + jnp.dot(pt.astype(BF16), dob, preferred_element_type=F32)
            dpt = lax.dot_general(vb, dob, _NT, preferred_element_type=F32)
            dst = pt * (dpt - delta_ref[0, i])
            dk_acc = dk_acc + jnp.dot(dst.astype(BF16), qb, preferred_element_type=F32)
            if fox:
                dfs_acc = dfs_acc + jnp.sum(dst, axis=1, keepdims=True)
                dfq_ref[0, i] += jnp.sum(dst, axis=0, keepdims=True)
            return dk_acc, dv_acc, dfs_acc

        init = (jnp.zeros((t, dk), F32), jnp.zeros((t, dv), F32), jnp.zeros((t, 1), F32))
        carry = block(j, init, True)
        dk_acc, dv_acc, dfs_acc = lax.fori_loop(j + 1, nq, lambda i, c: block(i, c, False), carry)
        dk_ref[0] = dk_acc * scale
        dv_ref[0] = dv_acc
        if fox:
            dfs_ref[0] = dfs_acc

    in_specs = [pl.BlockSpec((1, s, dk), lambda h, j: (h, 0, 0)),
                pl.BlockSpec((1, t, dk), lambda h, j: (h, j, 0)),
                pl.BlockSpec((1, t, dv), lambda h, j: (h, j, 0)),
                pl.BlockSpec((1, s, dv), lambda h, j: (h, 0, 0)),
                pl.BlockSpec((1, nq, 1, t), lambda h, j: (h, 0, 0, 0)),
                pl.BlockSpec((1, nq, 1, t), lambda h, j: (h, 0, 0, 0))]
    args = [q, k, v, do, lse_row, delta_row]
    out_specs = [pl.BlockSpec((1, t, dk), lambda h, j: (h, j, 0)), pl.BlockSpec((1, t, dv), lambda h, j: (h, j, 0))]
    out_shape = [jax.ShapeDtypeStruct((nh, s, dk), F32), jax.ShapeDtypeStruct((nh, s, dv), F32)]
    if fox:
        in_specs += [pl.BlockSpec((t, LANES), lambda h, j: (j, 0)),
                     pl.BlockSpec((1, nq, 1, t), lambda h, j: (h, 0, 0, 0))]
        args += [fcol, frow]
        out_specs += [pl.BlockSpec((1, t, 1), lambda h, j: (h, j, 0)),
                      pl.BlockSpec((1, nq, 1, t), lambda h, j: (h, 0, 0, 0))]
        out_shape += [jax.ShapeDtypeStruct((nh, s, 1), F32), jax.ShapeDtypeStruct((nh, nq, 1, t), F32)]
    return pl.pallas_call(
        body, grid=(nh, nq), in_specs=in_specs, out_specs=out_specs, out_shape=out_shape,
        compiler_params=_cparams(("parallel", "arbitrary")), name=name)(*args)


class _ZLayout:
    def __init__(self, d):
        fw = FOX_HEADS * FOX_DIM
        self.d = d
        self.src = {}
        off = 0
        for nm, w in (("cq", MLA_Q_LORA), ("ckv", MLA_KV_LORA), ("kr", MLA_ROPE), ("fq", fw), ("fk", fw),
                      ("fv", fw), ("fl", FOX_HEADS), ("ga", d), ("gb", d)):
            self.src[nm] = (off, w)
            off += w
        self.d_in = off
        self.dst = {}
        off = 0
        for nm, w in (("ga", d), ("gb", d), ("fq", fw), ("fk", fw), ("fv", fw), ("cq", MLA_Q_LORA),
                      ("ckv", MLA_KV_LORA), ("kr1", LANES), ("kr2", LANES), ("fl", LANES)):
            assert off % w == 0
            self.dst[nm] = (off, w)
            off += w
        self.width = off

    def win(self, z, nm):
        off, w = self.dst[nm]
        return (z, off // w, w)

    def cols(self, z, nm):
        off, w = self.dst[nm]
        return z[:, off:off + w]

    def to_kernel(self, w):
        def seg(nm, lo=0, hi=None):
            off, wd = self.src[nm]
            hi = wd if hi is None else hi
            return w[..., off + lo:off + hi]

        def pad(a):
            return jnp.pad(a, [(0, 0)] * (a.ndim - 1) + [(0, LANES - a.shape[-1])])

        return jnp.concatenate([seg("ga"), seg("gb"), seg("fq"), seg("fk"), seg("fv"), seg("cq"), seg("ckv"),
                                pad(seg("kr", 0, HALF_ROPE)), pad(seg("kr", HALF_ROPE, MLA_ROPE)), pad(seg("fl"))],
                               axis=-1)

    def from_kernel(self, g):
        def seg(nm, hi=None):
            off, wd = self.dst[nm]
            return g[..., off:off + (wd if hi is None else hi)]

        return jnp.concatenate([seg("cq"), seg("ckv"), seg("kr1", HALF_ROPE), seg("kr2", HALF_ROPE), seg("fq"),
                                seg("fk"), seg("fv"), seg("fl", FOX_HEADS), seg("ga"), seg("gb")], axis=-1)


def _uq_to_kernel(w):
    r = w.shape[0]
    w3 = w.reshape(r, MLA_HEADS, MLA_NOPE + MLA_ROPE)
    return jnp.concatenate([w3[:, :, :MLA_NOPE].reshape(r, -1),
                            w3[:, :, MLA_NOPE:MLA_NOPE + HALF_ROPE].reshape(r, -1),
                            w3[:, :, MLA_NOPE + HALF_ROPE:].reshape(r, -1)], axis=1)


def _uq_from_kernel(g):
    r = g.shape[0]
    n0 = MLA_HEADS * MLA_NOPE
    n1 = MLA_HEADS * HALF_ROPE
    return jnp.concatenate([g[:, :n0].reshape(r, MLA_HEADS, MLA_NOPE),
                            g[:, n0:n0 + n1].reshape(r, MLA_HEADS, HALF_ROPE),
                            g[:, n0 + n1:].reshape(r, MLA_HEADS, HALF_ROPE)], axis=2).reshape(r, -1)


def _heads(a, nh):
    s = a.shape[0]
    return a.reshape(s, nh, -1).transpose(1, 0, 2)


def _merge(a):
    nh, s, d = a.shape
    return a.transpose(1, 0, 2).reshape(s, nh * d)


def _rows_of(col, t):
    nh, s, _ = col.shape
    return col.reshape(nh, s // t, 1, t)


def _local_step(x, positions, target, wts):
    s, d = x.shape
    zl = _ZLayout(d)
    t_att = min(ATT_TILE, s)
    nope_w = MLA_HEADS * MLA_NOPE
    rope_w = MLA_HEADS * HALF_ROPE
    assert rope_w == LANES

    inv_freq = ROPE_THETA ** (-jnp.arange(HALF_ROPE, dtype=F32) / HALF_ROPE)
    ang = positions.astype(F32)[:, None] * inv_freq
    cos, sin = jnp.cos(ang), jnp.sin(ang)
    cos_q, sin_q = jnp.tile(cos, (1, MLA_HEADS)), jnp.tile(sin, (1, MLA_HEADS))
    cos_k = jnp.pad(cos, ((0, 0), (0, LANES - HALF_ROPE)))
    sin_k = jnp.pad(sin, ((0, 0), (0, LANES - HALF_ROPE)))

    w_in = zl.to_kernel(wts["w_in"])
    b_in = zl.to_kernel(wts["b_in"])
    w_uq = _uq_to_kernel(wts["w_uq"])
    w_ukv = jnp.concatenate([wts["w_uk"], wts["w_uv"]], axis=1)

    def f_norm_in(ti, pa):
        y, _ = _rms(ti[0], pa[0])
        return [y], []

    (h,) = _rowwise("norm_in", f_norm_in, [x], [wts["ln_pre_mix"]], [(d, BF16)])
    z = _mm(h, w_in, bias=b_in, out_dtype=F32, name="proj_in")

    def f_mla_prep(ti, pa):
        cq, ckv, k1, k2, ck, sk = ti
        cqn, _ = _rms(cq, pa[0])
        ckvn, _ = _rms(ckv, pa[1])
        return [cqn, ckvn, k1 * ck - k2 * sk, k2 * ck + k1 * sk], []

    cqn, ckvn, kro1, kro2 = _rowwise(
        "mla_prep", f_mla_prep,
        [zl.win(z, "cq"), zl.win(z, "ckv"), zl.win(z, "kr1"), zl.win(z, "kr2"), cos_k, sin_k],
        [wts["q_a_norm"], wts["kv_a_norm"]],
        [(MLA_Q_LORA, BF16), (MLA_KV_LORA, BF16), (LANES, BF16), (LANES, BF16)])
    qf = _mm(cqn, w_uq, out_dtype=F32, name="proj_uq")
    kv = _mm(ckvn, w_ukv, out_dtype=BF16, name="proj_ukv")

    def f_rope(ti, pa):
        a1, a2, c, sn = ti
        return [a1 * c - a2 * sn, a2 * c + a1 * sn], []

    qr1, qr2 = _rowwise("rope_q", f_rope, [(qf, nope_w // LANES, LANES), (qf, nope_w // LANES + 1, LANES), cos_q, sin_q],
                        [], [(LANES, BF16), (LANES, BF16)])
    zpad = jnp.zeros((s, MLA_HEADS, LANES - MLA_NOPE - MLA_ROPE), BF16)
    q_mla = jnp.concatenate([qf[:, :nope_w].astype(BF16).reshape(s, MLA_HEADS, MLA_NOPE),
                             qr1.reshape(s, MLA_HEADS, HALF_ROPE), qr2.reshape(s, MLA_HEADS, HALF_ROPE), zpad],
                            axis=2).transpose(1, 0, 2)
    k_mla = jnp.concatenate([kv[:, :nope_w].reshape(s, MLA_HEADS, MLA_NOPE),
                             jnp.broadcast_to(kro1[:, None, :HALF_ROPE], (s, MLA_HEADS, HALF_ROPE)),
                             jnp.broadcast_to(kro2[:, None, :HALF_ROPE], (s, MLA_HEADS, HALF_ROPE)), zpad],
                            axis=2).transpose(1, 0, 2)
    v_mla = _heads(kv[:, nope_w:], MLA_HEADS)
    scale_mla = (MLA_NOPE + MLA_ROPE) ** -0.5
    o_mla, lse_mla = _flash_fwd(q_mla, k_mla, v_mla, scale_mla, name="mla_fwd")

    fcol, frow_t = _forget_cumsum(z, zl.dst["fl"][0] // LANES)
    frow = frow_t[:FOX_HEADS].reshape(FOX_HEADS, s // t_att, 1, t_att)
    q_fox = _heads(zl.cols(z, "fq").astype(BF16), FOX_HEADS)
    k_fox = _heads(zl.cols(z, "fk").astype(BF16), FOX_HEADS)
    v_fox = _heads(zl.cols(z, "fv").astype(BF16), FOX_HEADS)
    scale_fox = FOX_DIM ** -0.5
    o_fox, lse_fox = _flash_fwd(q_fox, k_fox, v_fox, scale_fox, fcol, frow, name="fox_fwd")

    om = _merge(o_mla)
    ox = _merge(o_fox)
    y_mla = _mm(om, wts["w_o_mla"], out_dtype=F32, name="proj_o_mla")
    y_fox = _mm(ox, wts["w_o_fox"], out_dtype=F32, name="proj_o_fox")

    def f_gate(ti, pa):
        ga, gb, ya, yb = ti
        return [_sigmoid(ga) * ya + _sigmoid(gb) * yb], []

    (merged,) = _rowwise("gate", f_gate, [zl.win(z, "ga"), zl.win(z, "gb"), y_mla, y_fox], [], [(d, BF16)])
    mix = _mm(merged, wts["w_out"], out_dtype=F32, name="proj_out")

    def f_resid1(ti, pa):
        xa, mx = ti
        y, _ = _rms(mx, pa[0])
        x1 = xa + y
        h2, _ = _rms(x1, pa[1])
        return [x1, h2], []

    x1, h2 = _rowwise("resid_mix", f_resid1, [x, mix], [wts["ln_post_mix"], wts["ln_pre_mlp"]], [(d, F32), (d, BF16)])
    u = _mm(h2, wts["w_ff1"], out_dtype=F32, name="ff1")

    def f_act(ti, pa):
        r = jnp.maximum(ti[0], 0.0)
        return [r * r], []

    (act,) = _rowwise("relu2", f_act, [u], [], [(u.shape[1], BF16)])
    mo = _mm(act, wts["w_ff2"], out_dtype=F32, name="ff2")

    def f_loss(ti, pa):
        xa, mv, tg = ti
        y, _ = _rms(mv, pa[0])
        err = (xa + y) - tg
        g2 = err / d
        dmo, dg = _rms_bwd(mv, pa[0], g2)
        return [g2, dmo], [jnp.sum(err * err, axis=0, keepdims=True), dg]

    g2, d_mo, loss_cols, g_ln_post_mlp = _rowwise("loss", f_loss, [x1, mo, target], [wts["ln_post_mlp"]],
                                                  [(d, F32), (d, BF16)], [d, d])
    loss = 0.5 * jnp.sum(loss_cols) / d

    grads = {"ln_post_mlp": g_ln_post_mlp}
    grads["w_ff2"] = _mm_tn(act, d_mo, name="grad_ff2")
    d_act = _mm(d_mo, wts["w_ff2"].T, out_dtype=F32, name="ff2_bwd")

    def f_act_bwd(ti, pa):
        return [ti[0] * (2.0 * jnp.maximum(ti[1], 0.0))], []

    (d_u,) = _rowwise("relu2_bwd", f_act_bwd, [d_act, u], [], [(u.shape[1], BF16)])
    grads["w_ff1"] = _mm_tn(h2, d_u, name="grad_ff1")
    d_h2 = _mm(d_u, wts["w_ff1"].T, out_dtype=F32, name="ff1_bwd")

    def f_resid1_bwd(ti, pa):
        gres, dh2, x1v, mx = ti
        dx1n, dg_pre_mlp = _rms_bwd(x1v, pa[1], dh2)
        dx1 = gres + dx1n
        dmix, dg_post_mix = _rms_bwd(mx, pa[0], dx1)
        return [dx1, dmix], [dg_post_mix, dg_pre_mlp]

    d_x1, d_mix, grads["ln_post_mix"], grads["ln_pre_mlp"] = _rowwise(
        "resid_mix_bwd", f_resid1_bwd, [g2, d_h2, x1, mix], [wts["ln_post_mix"], wts["ln_pre_mlp"]],
        [(d, F32), (d, BF16)], [d, d])
    grads["w_out"] = _mm_tn(merged, d_mix, name="grad_out")
    d_merged = _mm(d_mix, wts["w_out"].T, out_dtype=F32, name="proj_out_bwd")

    def f_gate_bwd(ti, pa):
        dm, ga, gb, ya, yb = ti
        sa, sb = _sigmoid(ga), _sigmoid(gb)
        return [dm * sa, dm * sb, dm * ya * (sa * (1.0 - sa)), dm * yb * (sb * (1.0 - sb))], []

    d_ya, d_yb, d_ga, d_gb = _rowwise("gate_bwd", f_gate_bwd,
                                      [d_merged, zl.win(z, "ga"), zl.win(z, "gb"), y_mla, y_fox], [],
                                      [(d, BF16)] * 4)
    grads["w_o_mla"] = _mm_tn(om, d_ya, name="grad_o_mla")
    grads["w_o_fox"] = _mm_tn(ox, d_yb, name="grad_o_fox")
    do_mla = _heads(_mm(d_ya, wts["w_o_mla"].T, out_dtype=BF16, name="proj_o_mla_bwd"), MLA_HEADS)
    do_fox = _heads(_mm(d_yb, wts["w_o_fox"].T, out_dtype=BF16, name="proj_o_fox_bwd"), FOX_HEADS)

    dq_mla, delta_mla = _flash_dq(q_mla, k_mla, v_mla, do_mla, o_mla, lse_mla, scale_mla, name="mla_dq")
    dk_mla, dv_mla = _flash_dkdv(q_mla, k_mla, v_mla, do_mla, _rows_of(lse_mla, t_att), _rows_of(delta_mla, t_att),
                                 scale_mla, name="mla_dkdv")
    dq_fox, delta_fox = _flash_dq(q_fox, k_fox, v_fox, do_fox, o_fox, lse_fox, scale_fox, fcol, frow, name="fox_dq")
    dk_fox, dv_fox, dfs, dfq = _flash_dkdv(q_fox, k_fox, v_fox, do_fox, _rows_of(lse_fox, t_att),
                                           _rows_of(delta_fox, t_att), scale_fox, fcol, frow, name="fox_dkdv")

    dfq_cols = jnp.pad(dfq.reshape(FOX_HEADS, s).T, ((0, 0), (0, LANES - FOX_HEADS)))
    dfs_cols = jnp.pad(dfs[:, :, 0].T, ((0, 0), (0, LANES - FOX_HEADS)))
    d_fl = _forget_cumsum_bwd(dfq_cols, dfs_cols, z, zl.dst["fl"][0] // LANES)

    dq3 = dq_mla.transpose(1, 0, 2)
    dq1 = dq3[:, :, MLA_NOPE:MLA_NOPE + HALF_ROPE].reshape(s, rope_w)
    dq2 = dq3[:, :, MLA_NOPE + HALF_ROPE:MLA_NOPE + MLA_ROPE].reshape(s, rope_w)

    def f_rope_bwd(ti, pa):
        g1, g2_, c, sn = ti
        return [g1 * c + g2_ * sn, g2_ * c - g1 * sn], []

    dqr1, dqr2 = _rowwise("rope_q_bwd", f_rope_bwd, [dq1, dq2, cos_q, sin_q], [], [(LANES, BF16), (LANES, BF16)])
    d_qf = jnp.concatenate([dq3[:, :, :MLA_NOPE].reshape(s, nope_w).astype(BF16), dqr1, dqr2], axis=1)
    grads["w_uq"] = _uq_from_kernel(_mm_tn(cqn, d_qf, name="grad_uq"))
    d_cqn = _mm(d_qf, w_uq.T, out_dtype=F32, name="proj_uq_bwd")

    dk3 = dk_mla.transpose(1, 0, 2)

    def f_head_sum(ti, pa):
        a = ti[0]
        tot = a[:, 0:LANES]
        for hh in range(1, MLA_HEADS):
            tot = tot + a[:, hh * LANES:(hh + 1) * LANES]
        return [tot], []

    (dk_sum,) = _rowwise("rope_k_head_sum", f_head_sum, [dk3.reshape(s, MLA_HEADS * LANES)], [], [(LANES, F32)])
    dkr1 = jnp.pad(dk_sum[:, MLA_NOPE:MLA_NOPE + HALF_ROPE], ((0, 0), (0, LANES - HALF_ROPE)))
    dkr2 = jnp.pad(dk_sum[:, MLA_NOPE + HALF_ROPE:MLA_NOPE + MLA_ROPE], ((0, 0), (0, LANES - HALF_ROPE)))
    d_kv = jnp.concatenate([dk3[:, :, :MLA_NOPE].reshape(s, nope_w).astype(BF16), _merge(dv_mla).astype(BF16)], axis=1)
    g_ukv = _mm_tn(ckvn, d_kv, name="grad_ukv")
    grads["w_uk"], grads["w_uv"] = g_ukv[:, :nope_w], g_ukv[:, nope_w:]
    d_ckvn = _mm(d_kv, w_ukv.T, out_dtype=F32, name="proj_ukv_bwd")

    def f_mla_prep_bwd(ti, pa):
        cq, ckv, dcqn, dckvn, g1, g2_, ck, sk = ti
        dcq, dg_q = _rms_bwd(cq, pa[0], dcqn)
        dckv, dg_kv = _rms_bwd(ckv, pa[1], dckvn)
        return [dcq, dckv, g1 * ck + g2_ * sk, g2_ * ck - g1 * sk], [dg_q, dg_kv]

    d_cq, d_ckv, d_kr1, d_kr2, grads["q_a_norm"], grads["kv_a_norm"] = _rowwise(
        "mla_prep_bwd", f_mla_prep_bwd,
        [zl.win(z, "cq"), zl.win(z, "ckv"), d_cqn, d_ckvn, dkr1, dkr2, cos_k, sin_k],
        [wts["q_a_norm"], wts["kv_a_norm"]],
        [(MLA_Q_LORA, BF16), (MLA_KV_LORA, BF16), (LANES, BF16), (LANES, BF16)], [MLA_Q_LORA, MLA_KV_LORA])

    d_z = jnp.concatenate([d_ga, d_gb, _merge(dq_fox).astype(BF16), _merge(dk_fox).astype(BF16),
                           _merge(dv_fox).astype(BF16), d_cq, d_ckv, d_kr1, d_kr2, d_fl], axis=1)
    assert d_z.shape[1] == zl.width

    def f_bias(ti, pa):
        return [], [jnp.sum(ti[0].astype(F32), axis=0, keepdims=True)]

    (g_b_in,) = _rowwise("grad_b_in", f_bias, [d_z], [], [], [zl.width])
    grads["b_in"] = zl.from_kernel(g_b_in)
    grads["w_in"] = zl.from_kernel(_mm_tn(h, d_z, name="grad_in"))
    d_h = _mm(d_z, w_in.T, out_dtype=F32, name="proj_in_bwd")

    def f_norm_in_bwd(ti, pa):
        dx1v, dh, xa = ti
        dxn, dg = _rms_bwd(xa, pa[0], dh)
        return [dx1v + dxn], [dg]

    grad_x, grads["ln_pre_mix"] = _rowwise("norm_in_bwd", f_norm_in_bwd, [d_x1, d_h, x], [wts["ln_pre_mix"]],
                                           [(d, F32)], [d])
    return loss, grad_x, grads


def _pack_rows(n_elems):
    rows = -(-n_elems // PACK_W)
    return -(-rows // 64) * 64


def _pack(arrs, rows):
    flat = jnp.concatenate([a.reshape(-1) for a in arrs])
    return jnp.pad(flat, (0, rows * PACK_W - flat.shape[0])).reshape(rows, PACK_W)


def _unpack(packed, shapes):
    flat = packed.reshape(-1)
    out, off = [], 0
    for shp in shapes:
        n = 1
        for v in shp:
            n *= v
        out.append(flat[off:off + n].reshape(shp))
        off += n
    return out


ANY = pl.BlockSpec(memory_space=pl.ANY)


def _place():
    return lax.axis_index("x"), lax.axis_index("y"), lax.axis_index("c")


def _gather_weights(wpk):
    rows, wd = wpk.shape
    half = rows // 2

    def body(w_ref, out_ref, send_sems, recv_sems, local_sem):
        x, y, c = _place()
        sibling = (x, y, 1 - c)
        chips = [(1 - x, y), (x, 1 - y), (1 - x, 1 - y)]

        def slab(chip, hf):
            return out_ref.at[2 * chip[0] + chip[1], pl.ds(hf * half, half), :]

        def copy(k, chip, hf, to, src=None):
            return pltpu.make_async_remote_copy(
                src_ref=slab(chip, hf) if src is None else src, dst_ref=slab(chip, hf),
                send_sem=send_sems.at[k], recv_sem=recv_sems.at[k], device_id=to, device_id_type=MESH)

        mine = pltpu.make_async_copy(w_ref, out_ref.at[2 * x + y], local_sem)
        mine.start()
        first = [copy(j, (x, y), c, (*chip, c), src=w_ref.at[pl.ds(c * half, half), :]) for j, chip in enumerate(chips)]
        for cp in first:
            cp.start()
        passed = [copy(3 + j, chip, c, sibling) for j, chip in enumerate(chips)]
        for j, chip in enumerate(chips):
            copy(j, chip, c, (x, y, c)).wait_recv()
            passed[j].start()
        for j, chip in enumerate(chips):
            copy(3 + j, chip, 1 - c, (x, y, c)).wait_recv()
        for cp in first + passed:
            cp.wait_send()
        mine.wait()

    return pl.pallas_call(
        body, out_shape=jax.ShapeDtypeStruct((N_CHIPS, rows, wd), wpk.dtype),
        in_specs=[ANY], out_specs=ANY,
        scratch_shapes=[pltpu.SemaphoreType.DMA((6,)), pltpu.SemaphoreType.DMA((6,)), pltpu.SemaphoreType.DMA],
        name="gather_weights")(wpk)


def _sibling_exchange(g):
    n, rows, wd = g.shape
    half = rows // 2

    def body(g_ref, out_ref, send_sem, recv_sem):
        x, y, c = _place()
        cp = pltpu.make_async_remote_copy(
            src_ref=g_ref.at[:, pl.ds((1 - c) * half, half), :], dst_ref=out_ref,
            send_sem=send_sem, recv_sem=recv_sem, device_id=(x, y, 1 - c), device_id_type=MESH)
        cp.start()
        cp.wait()

    return pl.pallas_call(
        body, out_shape=jax.ShapeDtypeStruct((n, half, wd), g.dtype), in_specs=[ANY], out_specs=ANY,
        scratch_shapes=[pltpu.SemaphoreType.DMA, pltpu.SemaphoreType.DMA], name="grad_sibling_exchange")(g)


def _chip_exchange(part):
    n, half, wd = part.shape

    def body(p_ref, out_ref, send_sems, recv_sems):
        x, y, c = _place()
        chips = [(1 - x, y), (x, 1 - y), (1 - x, 1 - y)]

        def copy(j, chip):
            return pltpu.make_async_remote_copy(
                src_ref=p_ref.at[2 * chip[0] + chip[1]], dst_ref=out_ref.at[j],
                send_sem=send_sems.at[j], recv_sem=recv_sems.at[j], device_id=(*chip, c), device_id_type=MESH)

        cps = [copy(j, chip) for j, chip in enumerate(chips)]
        for cp in cps:
            cp.start()
        for cp in cps:
            cp.wait()

    return pl.pallas_call(
        body, out_shape=jax.ShapeDtypeStruct((3, half, wd), part.dtype), in_specs=[ANY], out_specs=ANY,
        scratch_shapes=[pltpu.SemaphoreType.DMA((3,)), pltpu.SemaphoreType.DMA((3,))], name="grad_chip_exchange")(part)


def _sibling_concat(mine):
    half, wd = mine.shape

    def body(m_ref, out_ref, send_sem, recv_sem, local_sem):
        x, y, c = _place()
        loc = pltpu.make_async_copy(m_ref, out_ref.at[c], local_sem)
        loc.start()
        cp = pltpu.make_async_remote_copy(
            src_ref=m_ref, dst_ref=out_ref.at[c], send_sem=send_sem, recv_sem=recv_sem,
            device_id=(x, y, 1 - c), device_id_type=MESH)
        cp.start()
        pltpu.make_async_remote_copy(
            src_ref=m_ref, dst_ref=out_ref.at[1 - c], send_sem=send_sem, recv_sem=recv_sem,
            device_id=(x, y, 1 - c), device_id_type=MESH).wait_recv()
        cp.wait_send()
        loc.wait()

    return pl.pallas_call(
        body, out_shape=jax.ShapeDtypeStruct((2, half, wd), mine.dtype), in_specs=[ANY], out_specs=ANY,
        scratch_shapes=[pltpu.SemaphoreType.DMA, pltpu.SemaphoreType.DMA, pltpu.SemaphoreType.DMA],
        name="grad_sibling_concat")(mine)


def _adamw(w, g, m, v):
    m = ADAM_B1 * m + (1.0 - ADAM_B1) * g
    v = ADAM_B2 * v + (1.0 - ADAM_B2) * (g * g)
    m_hat = m / (1.0 - ADAM_B1 ** ADAM_STEP)
    v_hat = v / (1.0 - ADAM_B2 ** ADAM_STEP)
    delta = -ADAM_LR * (m_hat / (jnp.sqrt(v_hat) + ADAM_EPS) + ADAM_WD * w)
    return delta, m, v


def _small_allreduce_adamw(g, w, m, v):
    rows, wd = g.shape
    n_dev = 8

    def body(g_ref, w_ref, m_ref, v_ref, go_ref, d_ref, mo_ref, vo_ref, all_ref, send_sems, recv_sems):
        x, y, c = _place()
        me, sibling = (x, y, c), (x, y, 1 - c)
        chips = [(1 - x, y), (x, 1 - y), (1 - x, 1 - y)]

        def slot(px, py, pc):
            return all_ref.at[4 * px + 2 * py + pc]

        def copy(k, block, to, src=None):
            return pltpu.make_async_remote_copy(
                src_ref=slot(*block) if src is None else src, dst_ref=slot(*block),
                send_sem=send_sems.at[k], recv_sem=recv_sems.at[k], device_id=to, device_id_type=MESH)

        all_ref[4 * x + 2 * y + c] = g_ref[...]
        first = [copy(0, me, sibling, src=g_ref)]
        first += [copy(1 + j, me, (*chip, c), src=g_ref) for j, chip in enumerate(chips)]
        for cp in first:
            cp.start()
        passed = [copy(4 + j, (*chip, c), sibling) for j, chip in enumerate(chips)]
        for j, chip in enumerate(chips):
            copy(1 + j, (*chip, c), me).wait_recv()
            passed[j].start()
        copy(0, sibling, me).wait_recv()
        for j, chip in enumerate(chips):
            copy(4 + j, (*chip, 1 - c), me).wait_recv()
        for cp in first + passed:
            cp.wait_send()
        tot = jnp.zeros((rows, wd), F32)
        for dev in range(n_dev):
            tot = tot + all_ref[dev]
        delta, m_new, v_new = _adamw(w_ref[...], tot, m_ref[...], v_ref[...])
        go_ref[...] = tot
        d_ref[...] = delta
        mo_ref[...] = m_new
        vo_ref[...] = v_new

    vm = pl.BlockSpec(memory_space=pltpu.VMEM)
    shp = jax.ShapeDtypeStruct((rows, wd), F32)
    return pl.pallas_call(
        body, out_shape=[shp, shp, shp, shp], in_specs=[vm] * 4, out_specs=[vm] * 4,
        scratch_shapes=[pltpu.VMEM((n_dev, rows, wd), F32), pltpu.SemaphoreType.DMA((7,)), pltpu.SemaphoreType.DMA((7,))],
        name="small_allreduce_adamw")(g, w, m, v)


def kernel(x, positions, ln_pre_mix, ln_post_mix, ln_pre_mlp, ln_post_mlp, w_in, b_in, q_a_norm, w_uq, kv_a_norm, w_uk, w_uv, w_o_mla, w_o_fox, w_out, w_ff1, w_ff2, loss_target, m_ln_pre_mix, m_ln_post_mix, m_ln_pre_mlp, m_ln_post_mlp, m_w_in, m_b_in, m_q_a_norm, m_w_uq, m_kv_a_norm, m_w_uk, m_w_uv, m_w_o_mla, m_w_o_fox, m_w_out, m_w_ff1, m_w_ff2, v_ln_pre_mix, v_ln_post_mix, v_ln_pre_mlp, v_ln_post_mlp, v_w_in, v_b_in, v_q_a_norm, v_w_uq, v_kv_a_norm, v_w_uk, v_w_uv, v_w_o_mla, v_w_o_fox, v_w_out, v_w_ff1, v_w_ff2):
    w = dict(ln_pre_mix=ln_pre_mix, ln_post_mix=ln_post_mix, ln_pre_mlp=ln_pre_mlp, ln_post_mlp=ln_post_mlp, w_in=w_in,
             b_in=b_in, q_a_norm=q_a_norm, w_uq=w_uq, kv_a_norm=kv_a_norm, w_uk=w_uk, w_uv=w_uv, w_o_mla=w_o_mla,
             w_o_fox=w_o_fox, w_out=w_out, w_ff1=w_ff1, w_ff2=w_ff2)
    mom = dict(ln_pre_mix=m_ln_pre_mix, ln_post_mix=m_ln_post_mix, ln_pre_mlp=m_ln_pre_mlp, ln_post_mlp=m_ln_post_mlp,
               w_in=m_w_in, b_in=m_b_in, q_a_norm=m_q_a_norm, w_uq=m_w_uq, kv_a_norm=m_kv_a_norm, w_uk=m_w_uk,
               w_uv=m_w_uv, w_o_mla=m_w_o_mla, w_o_fox=m_w_o_fox, w_out=m_w_out, w_ff1=m_w_ff1, w_ff2=m_w_ff2)
    var = dict(ln_pre_mix=v_ln_pre_mix, ln_post_mix=v_ln_post_mix, ln_pre_mlp=v_ln_pre_mlp, ln_post_mlp=v_ln_post_mlp,
               w_in=v_w_in, b_in=v_b_in, q_a_norm=v_q_a_norm, w_uq=v_w_uq, kv_a_norm=v_kv_a_norm, w_uk=v_w_uk,
               w_uv=v_w_uv, w_o_mla=v_w_o_mla, w_o_fox=v_w_o_fox, w_out=v_w_out, w_ff1=v_w_ff1, w_ff2=v_w_ff2)

    big_names = [nm for nm, _ in BIG]
    shard_shapes = [w[nm].shape[1:] for nm in big_names]
    n_shard = sum(a * b for a, b in shard_shapes)
    rows = _pack_rows(n_shard)
    half = rows // 2

    gathered = _gather_weights(_pack([w[nm][0].astype(BF16) for nm in big_names], rows))
    per_chip = [_unpack(gathered[ch], shard_shapes) for ch in range(N_CHIPS)]
    full = {nm: wv for nm, wv in w.items() if nm in SMALL}
    for k, (nm, axis) in enumerate(BIG):
        full[nm] = jnp.concatenate([per_chip[ch][k] for ch in range(N_CHIPS)], axis=axis - 1)

    loss_local, grad_x, grads = _local_step(x[0], positions[0], loss_target[0], full)
    loss = lax.psum(loss_local, ("x", "y", "c"))

    gpk = jnp.stack([
        _pack([jnp.split(grads[nm], N_CHIPS, axis=axis - 1)[ch] for nm, axis in BIG], rows) for ch in range(N_CHIPS)])
    c = lax.axis_index("c")
    chip = 2 * lax.axis_index("x") + lax.axis_index("y")
    theirs = _sibling_exchange(gpk)
    mine_half = lax.dynamic_slice_in_dim(gpk, c * half, half, axis=1)

    def f_add2(ti, pa):
        return [ti[0] + ti[1]], []

    (chip_part,) = _rowwise("grad_add_sibling", f_add2, [mine_half.reshape(N_CHIPS * half, PACK_W),
                                                       theirs.reshape(N_CHIPS * half, PACK_W)], [], [(PACK_W, F32)])
    chip_part = chip_part.reshape(N_CHIPS, half, PACK_W)
    others = _chip_exchange(chip_part)
    own = lax.dynamic_index_in_dim(chip_part, chip, axis=0, keepdims=False)

    def f_add4(ti, pa):
        return [((ti[0] + ti[1]) + ti[2]) + ti[3]], []

    (red_half,) = _rowwise("grad_add_chips", f_add4, [own, others[0], others[1], others[2]], [], [(PACK_W, F32)])
    g_shard = _sibling_concat(red_half).reshape(rows, PACK_W)

    def f_adamw(ti, pa):
        wv, gv, mv, vv = ti
        return list(_adamw(wv, gv, mv, vv)), []

    w_shard = _pack([w[nm][0] for nm in big_names], rows)
    m_shard = _pack([mom[nm][0] for nm in big_names], rows)
    v_shard = _pack([var[nm][0] for nm in big_names], rows)
    d_shard, m_new, v_new = _rowwise("adamw", f_adamw, [w_shard, g_shard, m_shard, v_shard], [], [(PACK_W, F32)] * 3)
    out = {"grad": {}, "delta": {}, "m": {}, "v": {}}
    for kind, packed in (("grad", g_shard), ("delta", d_shard), ("m", m_new), ("v", v_new)):
        for nm, arr in zip(big_names, _unpack(packed, shard_shapes), strict=True):
            out[kind][nm] = arr[None]

    small_shapes = [w[nm].shape for nm in SMALL]
    s_rows = SMALL_ROWS
    assert sum(a * b for a, b in small_shapes) <= s_rows * PACK_W
    sg, sd, sm, sv = _small_allreduce_adamw(
        _pack([grads[nm] for nm in SMALL], s_rows), _pack([w[nm] for nm in SMALL], s_rows),
        _pack([mom[nm] for nm in SMALL], s_rows), _pack([var[nm] for nm in SMALL], s_rows))
    for kind, packed in (("grad", sg), ("delta", sd), ("m", sm), ("v", sv)):
        for nm, arr in zip(SMALL, _unpack(packed, small_shapes), strict=True):
            out[kind][nm] = arr

    return (loss, grad_x[None], *[out["grad"][nm] for nm in ALL_W], *[out["delta"][nm] for nm in ALL_W],
            *[out["m"][nm] for nm in ALL_W], *[out["v"][nm] for nm in ALL_W])
```

```python
import functools

import jax
import jax.numpy as jnp
from jax import lax
from jax.experimental import pallas as pl
from jax.experimental.pallas import tpu as pltpu

F32 = jnp.float32
BF16 = jnp.bfloat16

MLA_HEADS = 8
MLA_Q_LORA = 256
MLA_KV_LORA = 128
MLA_NOPE = 64
MLA_ROPE = 32
MLA_V = 64
FOX_HEADS = 8
FOX_DIM = 64
ROPE_THETA = 10000.0
NORM_EPS = 1e-6
HALF_ROPE = MLA_ROPE // 2

ADAM_LR = 0.001
ADAM_B1 = 0.9
ADAM_B2 = 0.999
ADAM_EPS = 1e-08
ADAM_WD = 0.01
ADAM_STEP = 10

LANES = 128
VMEM_LIMIT = 56 * 1024 * 1024
ATT_TILE = 512
MM_LHS_BLOCK_BYTES = 9 * 1024 * 1024
NEG = -1e30
MESH = pl.DeviceIdType.MESH

BIG = (("w_in", 2), ("w_uq", 2), ("w_uk", 2), ("w_uv", 2), ("w_o_mla", 2), ("w_o_fox", 2),
       ("w_out", 1), ("w_ff1", 2), ("w_ff2", 1))
SMALL = ("ln_pre_mix", "ln_post_mix", "ln_pre_mlp", "ln_post_mlp", "b_in", "q_a_norm", "kv_a_norm")
ALL_W = ("ln_pre_mix", "ln_post_mix", "ln_pre_mlp", "ln_post_mlp", "w_in", "b_in", "q_a_norm", "w_uq",
         "kv_a_norm", "w_uk", "w_uv", "w_o_mla", "w_o_fox", "w_out", "w_ff1", "w_ff2")
N_CHIPS = 4
PACK_W = 1024
SMALL_ROWS = 16


def _cparams(sem=None):
    return pltpu.CompilerParams(dimension_semantics=sem, vmem_limit_bytes=VMEM_LIMIT)


def _divisor_tile(n, limit, mult):
    if n <= limit:
        return n
    best = None
    t = mult
    while t <= limit:
        if n % t == 0:
            best = t
        t += mult
    assert best is not None, (n, limit, mult)
    return best


def _mm(a, b, *, out_dtype, name, bias=None):
    m, k = a.shape
    k2, n = b.shape
    assert k == k2 and a.dtype == BF16 and b.dtype == BF16
    tm = _divisor_tile(m, min(2048, max(512, MM_LHS_BLOCK_BYTES // (2 * k))), 16)
    tn = _divisor_tile(n, 512, LANES)

    def body(*refs):
        if bias is None:
            a_ref, b_ref, o_ref = refs
        else:
            a_ref, b_ref, bias_ref, o_ref = refs
        acc = jnp.dot(a_ref[...], b_ref[...], preferred_element_type=F32)
        if bias is not None:
            acc = acc + bias_ref[...]
        o_ref[...] = acc.astype(o_ref.dtype)

    in_specs = [pl.BlockSpec((tm, k), lambda i, j: (i, 0)), pl.BlockSpec((k, tn), lambda i, j: (0, j))]
    args = [a, b]
    if bias is not None:
        in_specs.append(pl.BlockSpec((1, tn), lambda i, j: (0, j)))
        args.append(bias)
    return pl.pallas_call(
        body, grid=(m // tm, n // tn), in_specs=in_specs,
        out_specs=pl.BlockSpec((tm, tn), lambda i, j: (i, j)),
        out_shape=jax.ShapeDtypeStruct((m, n), out_dtype),
        compiler_params=_cparams(("parallel", "parallel")), name=name)(*args)


def _mm_tn(a, b, *, name):
    s, m = a.shape
    s2, n = b.shape
    assert s == s2 and a.dtype == BF16 and b.dtype == BF16
    tm = _divisor_tile(m, 1024, LANES)
    tn = _divisor_tile(n, 2304, LANES)
    tk = _divisor_tile(s, 512, 16)

    def body(a_ref, b_ref, o_ref):
        @pl.when(pl.program_id(2) == 0)
        def _():
            o_ref[...] = jnp.zeros_like(o_ref)

        o_ref[...] += lax.dot_general(a_ref[...], b_ref[...], (((0,), (0,)), ((), ())),
                                      preferred_element_type=F32)

    return pl.pallas_call(
        body, grid=(m // tm, n // tn, s // tk),
        in_specs=[pl.BlockSpec((tk, tm), lambda i, j, k: (k, i)), pl.BlockSpec((tk, tn), lambda i, j, k: (k, j))],
        out_specs=pl.BlockSpec((tm, tn), lambda i, j, k: (i, j)),
        out_shape=jax.ShapeDtypeStruct((m, n), F32),
        compiler_params=_cparams(("parallel", "parallel", "arbitrary")), name=name)(a, b)


def _rowwise(name, fn, tiled, params, outs, reds=()):
    wins = [t if isinstance(t, tuple) else (t, 0, t.shape[1]) for t in tiled]
    s = wins[0][0].shape[0]
    row_bytes = sum(w * arr.dtype.itemsize for arr, _, w in wins) + sum(w * jnp.dtype(d).itemsize for w, d in outs)
    ts = _divisor_tile(s, max(16, min(1024, (6 * 1024 * 1024) // row_bytes)), 16)
    nt, npar, nout = len(wins), len(params), len(outs)

    def body(*refs):
        tin = [r[...] for r in refs[:nt]]
        par = [r[...] for r in refs[nt:nt + npar]]
        out_refs = refs[nt + npar:nt + npar + nout]
        red_refs = refs[nt + npar + nout:]
        o, r = fn(tin, par)
        for ref, val in zip(out_refs, o, strict=True):
            ref[...] = val.astype(ref.dtype)
        if red_refs:
            @pl.when(pl.program_id(0) == 0)
            def _():
                for ref in red_refs:
                    ref[...] = jnp.zeros_like(ref)

            for ref, val in zip(red_refs, r, strict=True):
                ref[...] += val

    in_specs = [pl.BlockSpec((ts, w), functools.partial(lambda i, cb: (i, cb), cb=cb)) for _, cb, w in wins]
    in_specs += [pl.BlockSpec(p.shape, lambda i: (0, 0)) for p in params]
    out_specs = [pl.BlockSpec((ts, w), lambda i: (i, 0)) for w, _ in outs]
    out_specs += [pl.BlockSpec((1, w), lambda i: (0, 0)) for w in reds]
    out_shape = [jax.ShapeDtypeStruct((s, w), d) for w, d in outs]
    out_shape += [jax.ShapeDtypeStruct((1, w), F32) for w in reds]
    res = pl.pallas_call(
        body, grid=(s // ts,), in_specs=in_specs, out_specs=out_specs, out_shape=out_shape,
        compiler_params=_cparams(("arbitrary",)), name=name)(*[w[0] for w in wins], *params)
    return res


def _rms(x, g):
    r = lax.rsqrt(jnp.mean(x * x, axis=-1, keepdims=True) + NORM_EPS)
    return x * r * g, r


def _rms_bwd(x, g, dy):
    r = lax.rsqrt(jnp.mean(x * x, axis=-1, keepdims=True) + NORM_EPS)
    gy = dy * g
    dx = r * gy - x * (r * r * r) * jnp.mean(x * gy, axis=-1, keepdims=True)
    dg = jnp.sum(dy * (x * r), axis=0, keepdims=True)
    return dx, dg


def _sigmoid(x):
    return 1.0 / (1.0 + jnp.exp(-x))


def _forget_cumsum(z, cb):
    s = z.shape[0]
    ts = _divisor_tile(s, 512, LANES)

    def body(x_ref, col_ref, row_ref, carry):
        @pl.when(pl.program_id(0) == 0)
        def _():
            carry[...] = jnp.zeros_like(carry)

        x = x_ref[...]
        lf = jnp.minimum(x, 0.0) - jnp.log1p(jnp.exp(-jnp.abs(x)))
        r = lax.broadcasted_iota(jnp.int32, (ts, ts), 0)
        c = lax.broadcasted_iota(jnp.int32, (ts, ts), 1)
        tri = jnp.where(c <= r, 1.0, 0.0).astype(F32)
        cs = jnp.dot(tri, lf, preferred_element_type=F32, precision=lax.Precision.HIGHEST) + carry[...]
        carry[...] += jnp.sum(lf, axis=0, keepdims=True)
        col_ref[...] = cs
        row_ref[...] = cs.T

    return pl.pallas_call(
        body, grid=(s // ts,),
        in_specs=[pl.BlockSpec((ts, LANES), lambda i: (i, cb))],
        out_specs=[pl.BlockSpec((ts, LANES), lambda i: (i, 0)), pl.BlockSpec((LANES, ts), lambda i: (0, i))],
        out_shape=[jax.ShapeDtypeStruct((s, LANES), F32), jax.ShapeDtypeStruct((LANES, s), F32)],
        scratch_shapes=[pltpu.VMEM((1, LANES), F32)],
        compiler_params=_cparams(("arbitrary",)), name="forget_cumsum")(z)


def _forget_cumsum_bwd(dfq, dfs, z, cb):
    s = z.shape[0]
    ts = _divisor_tile(s, 512, LANES)
    nt = s // ts

    def body(dq_ref, d_ref, x_ref, o_ref, carry):
        @pl.when(pl.program_id(0) == 0)
        def _():
            carry[...] = jnp.zeros_like(carry)

        r = lax.broadcasted_iota(jnp.int32, (ts, ts), 0)
        c = lax.broadcasted_iota(jnp.int32, (ts, ts), 1)
        tri = jnp.where(c >= r, 1.0, 0.0).astype(F32)
        dv = dq_ref[...] - d_ref[...]
        rc = jnp.dot(tri, dv, preferred_element_type=F32, precision=lax.Precision.HIGHEST) + carry[...]
        carry[...] += jnp.sum(dv, axis=0, keepdims=True)
        o_ref[...] = (rc * (1.0 / (1.0 + jnp.exp(x_ref[...])))).astype(o_ref.dtype)

    return pl.pallas_call(
        body, grid=(nt,),
        in_specs=[pl.BlockSpec((ts, LANES), lambda i: (nt - 1 - i, 0)),
                  pl.BlockSpec((ts, LANES), lambda i: (nt - 1 - i, 0)),
                  pl.BlockSpec((ts, LANES), lambda i: (nt - 1 - i, cb))],
        out_specs=pl.BlockSpec((ts, LANES), lambda i: (nt - 1 - i, 0)),
        out_shape=jax.ShapeDtypeStruct((s, LANES), BF16),
        scratch_shapes=[pltpu.VMEM((1, LANES), F32)],
        compiler_params=_cparams(("arbitrary",)), name="forget_cumsum_bwd")(dfq, dfs, z)


_NT = (((1,), (1,)), ((), ()))


def _head_column(fcol_tile, h):
    lane = lax.broadcasted_iota(jnp.int32, fcol_tile.shape, 1)
    return jnp.sum(jnp.where(lane == h, fcol_tile, 0.0), axis=1, keepdims=True)


def _flash_fwd(q, k, v, scale, fcol=None, frow=None, *, name):
    nh, s, dk = q.shape
    dv = v.shape[2]
    t = min(ATT_TILE, s)
    nq = s // t
    fox = fcol is not None

    def body(*refs):
        if fox:
            q_ref, k_ref, v_ref, fcol_ref, frow_ref, o_ref, lse_ref = refs
        else:
            q_ref, k_ref, v_ref, o_ref, lse_ref = refs
        h = pl.program_id(0)
        i = pl.program_id(1)
        qb = q_ref[0]
        if fox:
            fq = _head_column(fcol_ref[...], h)

        def scores(j):
            kb = k_ref[0, pl.ds(pl.multiple_of(j * t, t), t), :]
            sc = lax.dot_general(qb, kb, _NT, preferred_element_type=F32) * scale
            if fox:
                sc = sc + fq - frow_ref[0, j]
            return sc

        def update(j, sc, carry):
            m, l, acc = carry
            m_new = jnp.maximum(m, jnp.max(sc, axis=1, keepdims=True))
            p = jnp.exp(sc - m_new)
            alpha = jnp.exp(m - m_new)
            l = alpha * l + jnp.sum(p, axis=1, keepdims=True)
            vb = v_ref[0, pl.ds(pl.multiple_of(j * t, t), t), :]
            acc = alpha * acc + jnp.dot(p.astype(BF16), vb, preferred_element_type=F32)
            return m_new, l, acc

        init = (jnp.full((t, 1), NEG, F32), jnp.zeros((t, 1), F32), jnp.zeros((t, dv), F32))
        carry = lax.fori_loop(0, i, lambda j, c: update(j, scores(j), c), init)
        row = lax.broadcasted_iota(jnp.int32, (t, t), 0)
        col = lax.broadcasted_iota(jnp.int32, (t, t), 1)
        m, l, acc = update(i, jnp.where(col <= row, scores(i), NEG), carry)
        o_ref[0] = (acc / l).astype(o_ref.dtype)
        lse_ref[0] = m + jnp.log(l)

    in_specs = [pl.BlockSpec((1, t, dk), lambda h, i: (h, i, 0)),
                pl.BlockSpec((1, s, dk), lambda h, i: (h, 0, 0)),
                pl.BlockSpec((1, s, dv), lambda h, i: (h, 0, 0))]
    args = [q, k, v]
    if fox:
        in_specs += [pl.BlockSpec((t, LANES), lambda h, i: (i, 0)),
                     pl.BlockSpec((1, nq, 1, t), lambda h, i: (h, 0, 0, 0))]
        args += [fcol, frow]
    return pl.pallas_call(
        body, grid=(nh, nq), in_specs=in_specs,
        out_specs=[pl.BlockSpec((1, t, dv), lambda h, i: (h, i, 0)), pl.BlockSpec((1, t, 1), lambda h, i: (h, i, 0))],
        out_shape=[jax.ShapeDtypeStruct((nh, s, dv), BF16), jax.ShapeDtypeStruct((nh, s, 1), F32)],
        compiler_params=_cparams(("parallel", "arbitrary")), name=name)(*args)


def _head_dots(a, b, nh, *, name):
    wd = a.shape[1]
    d = wd // nh

    def fn(ti, pa):
        prod = ti[0].astype(F32) * ti[1].astype(F32)
        col = lax.broadcasted_iota(jnp.int32, (wd, LANES), 0)
        lane = lax.broadcasted_iota(jnp.int32, (wd, LANES), 1)
        sel = jnp.where((col >= lane * d) & (col < (lane + 1) * d), 1.0, 0.0).astype(F32)
        return [jnp.dot(prod, sel, preferred_element_type=F32, precision=lax.Precision.HIGHEST)], []

    (out,) = _rowwise(name, fn, [a, b], [], [(LANES, F32)])
    return out


def _flash_bwd(q, k, v, do, lse_row, delta_row, scale, fcol=None, frow=None, *, name):
    nh, s, dk = q.shape
    dv = v.shape[2]
    t = min(ATT_TILE, s)
    nq = s // t
    fox = fcol is not None
    _tn = (((0,), (0,)), ((), ()))

    def body(*refs):
        if fox:
            (q_ref, k_ref, v_ref, do_ref, lse_ref, delta_ref, fcol_ref, frow_ref,
             dk_ref, dv_ref, dq_ref, dfs_ref, dfq_ref) = refs
        else:
            q_ref, k_ref, v_ref, do_ref, lse_ref, delta_ref, dk_ref, dv_ref, dq_ref = refs
        h = pl.program_id(0)
        j = pl.program_id(1)
        kb = k_ref[0]
        vb = v_ref[0]

        @pl.when(j == 0)
        def _():
            dq_ref[...] = jnp.zeros_like(dq_ref)
            if fox:
                dfq_ref[...] = jnp.zeros_like(dfq_ref)

        if fox:
            fk = _head_column(fcol_ref[...], h)

        def block(i, carry, masked):
            dk_acc, dv_acc, dfs_acc = carry
            qb = q_ref[0, pl.ds(pl.multiple_of(i * t, t), t), :]
            dob = do_ref[0, pl.ds(pl.multiple_of(i * t, t), t), :]
            st = lax.dot_general(kb, qb, _NT, preferred_element_type=F32) * scale
            if fox:
                st = st + frow_ref[0, i] - fk
            if masked:
                key = lax.broadcasted_iota(jnp.int32, (t, t), 0)
                qry = lax.broadcasted_iota(jnp.int32, (t, t), 1)
                st = jnp.where(key <= qry, st, NEG)
            pt = jnp.exp(st - lse_ref[0, i])
            dv_acc = dv_acc + jnp.dot(pt.astype(BF16), dob, preferred_element_type=F32)
            dpt = lax.dot_general(vb, dob, _NT, preferred_element_type=F32)
            dst = pt * (dpt - delta_ref[0, i])
            dsb = dst.astype(BF16)
            dk_acc = dk_acc + jnp.dot(dsb, qb, preferred_element_type=F32)
            rows = pl.ds(pl.multiple_of(i * t, t), t)
            dq_ref[0, rows, :] += lax.dot_general(dsb, kb, _tn, preferred_element_type=F32)
            if fox:
                dfs_acc = dfs_acc + jnp.sum(dst, axis=1, keepdims=True)
                dfq_ref[0, i] += jnp.sum(dst, axis=0, keepdims=True)
            return dk_acc, dv_acc, dfs_acc

        init = (jnp.zeros((t, dk), F32), jnp.zeros((t, dv), F32), jnp.zeros((t, 1), F32))
        carry = block(j, init, True)
        dk_acc, dv_acc, dfs_acc = lax.fori_loop(j + 1, nq, lambda i, c: block(i, c, False), carry)
        dk_ref[0] = dk_acc * scale
        dv_ref[0] = dv_acc
        if fox:
            dfs_ref[0] = dfs_acc

        @pl.when(j == nq - 1)
        def _():
            dq_ref[...] = dq_ref[...] * scale

    in_specs = [pl.BlockSpec((1, s, dk), lambda h, j: (h, 0, 0)),
                pl.BlockSpec((1, t, dk), lambda h, j: (h, j, 0)),
                pl.BlockSpec((1, t, dv), lambda h, j: (h, j, 0)),
                pl.BlockSpec((1, s, dv), lambda h, j: (h, 0, 0)),
                pl.BlockSpec((1, nq, 1, t), lambda h, j: (h, 0, 0, 0)),
                pl.BlockSpec((1, nq, 1, t), lambda h, j: (h, 0, 0, 0))]
    args = [q, k, v, do, lse_row, delta_row]
    out_specs = [pl.BlockSpec((1, t, dk), lambda h, j: (h, j, 0)), pl.BlockSpec((1, t, dv), lambda h, j: (h, j, 0)),
                 pl.BlockSpec((1, s, dk), lambda h, j: (h, 0, 0))]
    out_shape = [jax.ShapeDtypeStruct((nh, s, dk), F32), jax.ShapeDtypeStruct((nh, s, dv), F32),
                 jax.ShapeDtypeStruct((nh, s, dk), F32)]
    if fox:
        in_specs += [pl.BlockSpec((t, LANES), lambda h, j: (j, 0)),
                     pl.BlockSpec((1, nq, 1, t), lambda h, j: (h, 0, 0, 0))]
        args += [fcol, frow]
        out_specs += [pl.BlockSpec((1, t, 1), lambda h, j: (h, j, 0)),
                      pl.BlockSpec((1, nq, 1, t), lambda h, j: (h, 0, 0, 0))]
        out_shape += [jax.ShapeDtypeStruct((nh, s, 1), F32), jax.ShapeDtypeStruct((nh, nq, 1, t), F32)]
    return pl.pallas_call(
        body, grid=(nh, nq), in_specs=in_specs, out_specs=out_specs, out_shape=out_shape,
        compiler_params=_cparams(("parallel", "arbitrary")), name=name)(*args)


class _ZLayout:
    def __init__(self, d):
        fw = FOX_HEADS * FOX_DIM
        self.d = d
        self.src = {}
        off = 0
        for nm, w in (("cq", MLA_Q_LORA), ("ckv", MLA_KV_LORA), ("kr", MLA_ROPE), ("fq", fw), ("fk", fw),
                      ("fv", fw), ("fl", FOX_HEADS), ("ga", d), ("gb", d)):
            self.src[nm] = (off, w)
            off += w
        self.d_in = off
        self.dst = {}
        off = 0
        for nm, w in (("ga", d), ("gb", d), ("fq", fw), ("fk", fw), ("fv", fw), ("cq", MLA_Q_LORA),
                      ("ckv", MLA_KV_LORA), ("kr1", LANES), ("kr2", LANES), ("fl", LANES)):
            assert off % w == 0
            self.dst[nm] = (off, w)
            off += w
        self.width = off

    def win(self, z, nm):
        off, w = self.dst[nm]
        return (z, off // w, w)

    def cols(self, z, nm):
        off, w = self.dst[nm]
        return z[:, off:off + w]

    def to_kernel(self, w):
        def seg(nm, lo=0, hi=None):
            off, wd = self.src[nm]
            hi = wd if hi is None else hi
            return w[..., off + lo:off + hi]

        def pad(a):
            return jnp.pad(a, [(0, 0)] * (a.ndim - 1) + [(0, LANES - a.shape[-1])])

        return jnp.concatenate([seg("ga"), seg("gb"), seg("fq"), seg("fk"), seg("fv"), seg("cq"), seg("ckv"),
                                pad(seg("kr", 0, HALF_ROPE)), pad(seg("kr", HALF_ROPE, MLA_ROPE)), pad(seg("fl"))],
                               axis=-1)

    def from_kernel(self, g):
        def seg(nm, hi=None):
            off, wd = self.dst[nm]
            return g[..., off:off + (wd if hi is None else hi)]

        return jnp.concatenate([seg("cq"), seg("ckv"), seg("kr1", HALF_ROPE), seg("kr2", HALF_ROPE), seg("fq"),
                                seg("fk"), seg("fv"), seg("fl", FOX_HEADS), seg("ga"), seg("gb")], axis=-1)


def _uq_to_kernel(w):
    r = w.shape[0]
    w3 = w.reshape(r, MLA_HEADS, MLA_NOPE + MLA_ROPE)
    return jnp.concatenate([w3[:, :, :MLA_NOPE].reshape(r, -1),
                            w3[:, :, MLA_NOPE:MLA_NOPE + HALF_ROPE].reshape(r, -1),
                            w3[:, :, MLA_NOPE + HALF_ROPE:].reshape(r, -1)], axis=1)


def _uq_from_kernel(g):
    r = g.shape[0]
    n0 = MLA_HEADS * MLA_NOPE
    n1 = MLA_HEADS * HALF_ROPE
    return jnp.concatenate([g[:, :n0].reshape(r, MLA_HEADS, MLA_NOPE),
                            g[:, n0:n0 + n1].reshape(r, MLA_HEADS, HALF_ROPE),
                            g[:, n0 + n1:].reshape(r, MLA_HEADS, HALF_ROPE)], axis=2).reshape(r, -1)


def _heads(a, nh):
    s = a.shape[0]
    return a.reshape(s, nh, -1).transpose(1, 0, 2)


def _merge(a):
    nh, s, d = a.shape
    return a.transpose(1, 0, 2).reshape(s, nh * d)


def _rows_of(col, t):
    nh, s, _ = col.shape
    return col.reshape(nh, s // t, 1, t)


def _local_step(x, positions, target, wts):
    s, d = x.shape
    zl = _ZLayout(d)
    t_att = min(ATT_TILE, s)
    nope_w = MLA_HEADS * MLA_NOPE
    rope_w = MLA_HEADS * HALF_ROPE
    assert rope_w == LANES

    inv_freq = ROPE_THETA ** (-jnp.arange(HALF_ROPE, dtype=F32) / HALF_ROPE)
    ang = positions.astype(F32)[:, None] * inv_freq
    cos, sin = jnp.cos(ang), jnp.sin(ang)
    cos_q, sin_q = jnp.tile(cos, (1, MLA_HEADS)), jnp.tile(sin, (1, MLA_HEADS))
    cos_k = jnp.pad(cos, ((0, 0), (0, LANES - HALF_ROPE)))
    sin_k = jnp.pad(sin, ((0, 0), (0, LANES - HALF_ROPE)))

    w_in = zl.to_kernel(wts["w_in"])
    b_in = zl.to_kernel(wts["b_in"])
    w_uq = _uq_to_kernel(wts["w_uq"])
    w_ukv = jnp.concatenate([wts["w_uk"], wts["w_uv"]], axis=1)

    def f_norm_in(ti, pa):
        y, _ = _rms(ti[0], pa[0])
        return [y], []

    (h,) = _rowwise("norm_in", f_norm_in, [x], [wts["ln_pre_mix"]], [(d, BF16)])
    z = _mm(h, w_in, bias=b_in, out_dtype=F32, name="proj_in")

    def f_mla_prep(ti, pa):
        cq, ckv, k1, k2, ck, sk = ti
        cqn, _ = _rms(cq, pa[0])
        ckvn, _ = _rms(ckv, pa[1])
        return [cqn, ckvn, k1 * ck - k2 * sk, k2 * ck + k1 * sk], []

    cqn, ckvn, kro1, kro2 = _rowwise(
        "mla_prep", f_mla_prep,
        [zl.win(z, "cq"), zl.win(z, "ckv"), zl.win(z, "kr1"), zl.win(z, "kr2"), cos_k, sin_k],
        [wts["q_a_norm"], wts["kv_a_norm"]],
        [(MLA_Q_LORA, BF16), (MLA_KV_LORA, BF16), (LANES, BF16), (LANES, BF16)])
    qf = _mm(cqn, w_uq, out_dtype=F32, name="proj_uq")
    kv = _mm(ckvn, w_ukv, out_dtype=BF16, name="proj_ukv")

    def f_rope(ti, pa):
        a1, a2, c, sn = ti
        return [a1 * c - a2 * sn, a2 * c + a1 * sn], []

    qr1, qr2 = _rowwise("rope_q", f_rope, [(qf, nope_w // LANES, LANES), (qf, nope_w // LANES + 1, LANES), cos_q, sin_q],
                        [], [(LANES, BF16), (LANES, BF16)])
    zpad = jnp.zeros((s, MLA_HEADS, LANES - MLA_NOPE - MLA_ROPE), BF16)
    q_mla = jnp.concatenate([qf[:, :nope_w].astype(BF16).reshape(s, MLA_HEADS, MLA_NOPE),
                             qr1.reshape(s, MLA_HEADS, HALF_ROPE), qr2.reshape(s, MLA_HEADS, HALF_ROPE), zpad],
                            axis=2).transpose(1, 0, 2)
    k_mla = jnp.concatenate([kv[:, :nope_w].reshape(s, MLA_HEADS, MLA_NOPE),
                             jnp.broadcast_to(kro1[:, None, :HALF_ROPE], (s, MLA_HEADS, HALF_ROPE)),
                             jnp.broadcast_to(kro2[:, None, :HALF_ROPE], (s, MLA_HEADS, HALF_ROPE)), zpad],
                            axis=2).transpose(1, 0, 2)
    v_mla = _heads(kv[:, nope_w:], MLA_HEADS)
    scale_mla = (MLA_NOPE + MLA_ROPE) ** -0.5
    o_mla, lse_mla = _flash_fwd(q_mla, k_mla, v_mla, scale_mla, name="mla_fwd")

    fcol, frow_t = _forget_cumsum(z, zl.dst["fl"][0] // LANES)
    frow = frow_t[:FOX_HEADS].reshape(FOX_HEADS, s // t_att, 1, t_att)
    q_fox = _heads(zl.cols(z, "fq").astype(BF16), FOX_HEADS)
    k_fox = _heads(zl.cols(z, "fk").astype(BF16), FOX_HEADS)
    v_fox = _heads(zl.cols(z, "fv").astype(BF16), FOX_HEADS)
    scale_fox = FOX_DIM ** -0.5
    o_fox, lse_fox = _flash_fwd(q_fox, k_fox, v_fox, scale_fox, fcol, frow, name="fox_fwd")

    om = _merge(o_mla)
    ox = _merge(o_fox)
    y_mla = _mm(om, wts["w_o_mla"], out_dtype=F32, name="proj_o_mla")
    y_fox = _mm(ox, wts["w_o_fox"], out_dtype=F32, name="proj_o_fox")

    def f_gate(ti, pa):
        ga, gb, ya, yb = ti
        return [_sigmoid(ga) * ya + _sigmoid(gb) * yb], []

    (merged,) = _rowwise("gate", f_gate, [zl.win(z, "ga"), zl.win(z, "gb"), y_mla, y_fox], [], [(d, BF16)])
    mix = _mm(merged, wts["w_out"], out_dtype=F32, name="proj_out")

    def f_resid1(ti, pa):
        xa, mx = ti
        y, _ = _rms(mx, pa[0])
        x1 = xa + y
        h2, _ = _rms(x1, pa[1])
        return [x1, h2], []

    x1, h2 = _rowwise("resid_mix", f_resid1, [x, mix], [wts["ln_post_mix"], wts["ln_pre_mlp"]], [(d, F32), (d, BF16)])
    u = _mm(h2, wts["w_ff1"], out_dtype=F32, name="ff1")

    def f_act(ti, pa):
        r = jnp.maximum(ti[0], 0.0)
        return [r * r], []

    (act,) = _rowwise("relu2", f_act, [u], [], [(u.shape[1], BF16)])
    mo = _mm(act, wts["w_ff2"], out_dtype=F32, name="ff2")

    def f_loss(ti, pa):
        xa, mv, tg = ti
        y, _ = _rms(mv, pa[0])
        err = (xa + y) - tg
        g2 = err / d
        dmo, dg = _rms_bwd(mv, pa[0], g2)
        return [g2, dmo], [jnp.sum(err * err, axis=0, keepdims=True), dg]

    g2, d_mo, loss_cols, g_ln_post_mlp = _rowwise("loss", f_loss, [x1, mo, target], [wts["ln_post_mlp"]],
                                                  [(d, F32), (d, BF16)], [d, d])
    loss = 0.5 * jnp.sum(loss_cols) / d

    grads = {"ln_post_mlp": g_ln_post_mlp}
    grads["w_ff2"] = _mm_tn(act, d_mo, name="grad_ff2")
    d_act = _mm(d_mo, wts["w_ff2"].T, out_dtype=F32, name="ff2_bwd")

    def f_act_bwd(ti, pa):
        return [ti[0] * (2.0 * jnp.maximum(ti[1], 0.0))], []

    (d_u,) = _rowwise("relu2_bwd", f_act_bwd, [d_act, u], [], [(u.shape[1], BF16)])
    grads["w_ff1"] = _mm_tn(h2, d_u, name="grad_ff1")
    d_h2 = _mm(d_u, wts["w_ff1"].T, out_dtype=F32, name="ff1_bwd")

    def f_resid1_bwd(ti, pa):
        gres, dh2, x1v, mx = ti
        dx1n, dg_pre_mlp = _rms_bwd(x1v, pa[1], dh2)
        dx1 = gres + dx1n
        dmix, dg_post_mix = _rms_bwd(mx, pa[0], dx1)
        return [dx1, dmix], [dg_post_mix, dg_pre_mlp]

    d_x1, d_mix, grads["ln_post_mix"], grads["ln_pre_mlp"] = _rowwise(
        "resid_mix_bwd", f_resid1_bwd, [g2, d_h2, x1, mix], [wts["ln_post_mix"], wts["ln_pre_mlp"]],
        [(d, F32), (d, BF16)], [d, d])
    grads["w_out"] = _mm_tn(merged, d_mix, name="grad_out")
    d_merged = _mm(d_mix, wts["w_out"].T, out_dtype=F32, name="proj_out_bwd")

    def f_gate_bwd(ti, pa):
        dm, ga, gb, ya, yb = ti
        sa, sb = _sigmoid(ga), _sigmoid(gb)
        return [dm * sa, dm * sb, dm * ya * (sa * (1.0 - sa)), dm * yb * (sb * (1.0 - sb))], []

    d_ya, d_yb, d_ga, d_gb = _rowwise("gate_bwd", f_gate_bwd,
                                      [d_merged, zl.win(z, "ga"), zl.win(z, "gb"), y_mla, y_fox], [],
                                      [(d, BF16)] * 4)
    grads["w_o_mla"] = _mm_tn(om, d_ya, name="grad_o_mla")
    grads["w_o_fox"] = _mm_tn(ox, d_yb, name="grad_o_fox")
    dom = _mm(d_ya, wts["w_o_mla"].T, out_dtype=BF16, name="proj_o_mla_bwd")
    dox = _mm(d_yb, wts["w_o_fox"].T, out_dtype=BF16, name="proj_o_fox_bwd")
    do_mla = _heads(dom, MLA_HEADS)
    do_fox = _heads(dox, FOX_HEADS)

    def delta_rows(dm, om_, nh, name):
        return _head_dots(dm, om_, nh, name=name)[:, :nh].T.reshape(nh, s // t_att, 1, t_att)

    dk_mla, dv_mla, dq_mla = _flash_bwd(q_mla, k_mla, v_mla, do_mla, _rows_of(lse_mla, t_att),
                                        delta_rows(dom, om, MLA_HEADS, "mla_delta"), scale_mla, name="mla_bwd")
    dk_fox, dv_fox, dq_fox, dfs, dfq = _flash_bwd(q_fox, k_fox, v_fox, do_fox, _rows_of(lse_fox, t_att),
                                                  delta_rows(dox, ox, FOX_HEADS, "fox_delta"), scale_fox, fcol, frow,
                                                  name="fox_bwd")

    dfq_cols = jnp.pad(dfq.reshape(FOX_HEADS, s).T, ((0, 0), (0, LANES - FOX_HEADS)))
    dfs_cols = jnp.pad(dfs[:, :, 0].T, ((0, 0), (0, LANES - FOX_HEADS)))
    d_fl = _forget_cumsum_bwd(dfq_cols, dfs_cols, z, zl.dst["fl"][0] // LANES)

    dq3 = dq_mla.transpose(1, 0, 2)
    dq1 = dq3[:, :, MLA_NOPE:MLA_NOPE + HALF_ROPE].reshape(s, rope_w)
    dq2 = dq3[:, :, MLA_NOPE + HALF_ROPE:MLA_NOPE + MLA_ROPE].reshape(s, rope_w)

    def f_rope_bwd(ti, pa):
        g1, g2_, c, sn = ti
        return [g1 * c + g2_ * sn, g2_ * c - g1 * sn], []

    dqr1, dqr2 = _rowwise("rope_q_bwd", f_rope_bwd, [dq1, dq2, cos_q, sin_q], [], [(LANES, BF16), (LANES, BF16)])
    d_qf = jnp.concatenate([dq3[:, :, :MLA_NOPE].reshape(s, nope_w).astype(BF16), dqr1, dqr2], axis=1)
    grads["w_uq"] = _uq_from_kernel(_mm_tn(cqn, d_qf, name="grad_uq"))
    d_cqn = _mm(d_qf, w_uq.T, out_dtype=F32, name="proj_uq_bwd")

    dk3 = dk_mla.transpose(1, 0, 2)

    def f_head_sum(ti, pa):
        a = ti[0]
        tot = a[:, 0:LANES]
        for hh in range(1, MLA_HEADS):
            tot = tot + a[:, hh * LANES:(hh + 1) * LANES]
        return [tot], []

    (dk_sum,) = _rowwise("rope_k_head_sum", f_head_sum, [dk3.reshape(s, MLA_HEADS * LANES)], [], [(LANES, F32)])
    dkr1 = jnp.pad(dk_sum[:, MLA_NOPE:MLA_NOPE + HALF_ROPE], ((0, 0), (0, LANES - HALF_ROPE)))
    dkr2 = jnp.pad(dk_sum[:, MLA_NOPE + HALF_ROPE:MLA_NOPE + MLA_ROPE], ((0, 0), (0, LANES - HALF_ROPE)))
    d_kv = jnp.concatenate([dk3[:, :, :MLA_NOPE].reshape(s, nope_w).astype(BF16), _merge(dv_mla).astype(BF16)], axis=1)
    g_ukv = _mm_tn(ckvn, d_kv, name="grad_ukv")
    grads["w_uk"], grads["w_uv"] = g_ukv[:, :nope_w], g_ukv[:, nope_w:]
    d_ckvn = _mm(d_kv, w_ukv.T, out_dtype=F32, name="proj_ukv_bwd")

    def f_mla_prep_bwd(ti, pa):
        cq, ckv, dcqn, dckvn, g1, g2_, ck, sk = ti
        dcq, dg_q = _rms_bwd(cq, pa[0], dcqn)
        dckv, dg_kv = _rms_bwd(ckv, pa[1], dckvn)
        return [dcq, dckv, g1 * ck + g2_ * sk, g2_ * ck - g1 * sk], [dg_q, dg_kv]

    d_cq, d_ckv, d_kr1, d_kr2, grads["q_a_norm"], grads["kv_a_norm"] = _rowwise(
        "mla_prep_bwd", f_mla_prep_bwd,
        [zl.win(z, "cq"), zl.win(z, "ckv"), d_cqn, d_ckvn, dkr1, dkr2, cos_k, sin_k],
        [wts["q_a_norm"], wts["kv_a_norm"]],
        [(MLA_Q_LORA, BF16), (MLA_KV_LORA, BF16), (LANES, BF16), (LANES, BF16)], [MLA_Q_LORA, MLA_KV_LORA])

    d_z = jnp.concatenate([d_ga, d_gb, _merge(dq_fox).astype(BF16), _merge(dk_fox).astype(BF16),
                           _merge(dv_fox).astype(BF16), d_cq, d_ckv, d_kr1, d_kr2, d_fl], axis=1)
    assert d_z.shape[1] == zl.width

    def f_bias(ti, pa):
        return [], [jnp.sum(ti[0].astype(F32), axis=0, keepdims=True)]

    (g_b_in,) = _rowwise("grad_b_in", f_bias, [d_z], [], [], [zl.width])
    grads["b_in"] = zl.from_kernel(g_b_in)
    grads["w_in"] = zl.from_kernel(_mm_tn(h, d_z, name="grad_in"))
    d_h = _mm(d_z, w_in.T, out_dtype=F32, name="proj_in_bwd")

    def f_norm_in_bwd(ti, pa):
        dx1v, dh, xa = ti
        dxn, dg = _rms_bwd(xa, pa[0], dh)
        return [dx1v + dxn], [dg]

    grad_x, grads["ln_pre_mix"] = _rowwise("norm_in_bwd", f_norm_in_bwd, [d_x1, d_h, x], [wts["ln_pre_mix"]],
                                           [(d, F32)], [d])
    return loss, grad_x, grads


def _pack_rows(n_elems):
    rows = -(-n_elems // PACK_W)
    return -(-rows // 64) * 64


def _pack(arrs, rows):
    flat = jnp.concatenate([a.reshape(-1) for a in arrs])
    return jnp.pad(flat, (0, rows * PACK_W - flat.shape[0])).reshape(rows, PACK_W)


def _unpack(packed, shapes):
    flat = packed.reshape(-1)
    out, off = [], 0
    for shp in shapes:
        n = 1
        for v in shp:
            n *= v
        out.append(flat[off:off + n].reshape(shp))
        off += n
    return out


ANY = pl.BlockSpec(memory_space=pl.ANY)


def _place():
    return lax.axis_index("x"), lax.axis_index("y"), lax.axis_index("c")


def _gather_weights(wpk):
    rows, wd = wpk.shape
    half = rows // 2

    def body(w_ref, out_ref, send_sems, recv_sems, local_sem):
        x, y, c = _place()
        sibling = (x, y, 1 - c)
        chips = [(1 - x, y), (x, 1 - y), (1 - x, 1 - y)]

        def slab(chip, hf):
            return out_ref.at[2 * chip[0] + chip[1], pl.ds(hf * half, half), :]

        def copy(k, chip, hf, to, src=None):
            return pltpu.make_async_remote_copy(
                src_ref=slab(chip, hf) if src is None else src, dst_ref=slab(chip, hf),
                send_sem=send_sems.at[k], recv_sem=recv_sems.at[k], device_id=to, device_id_type=MESH)

        mine = pltpu.make_async_copy(w_ref, out_ref.at[2 * x + y], local_sem)
        mine.start()
        first = [copy(j, (x, y), c, (*chip, c), src=w_ref.at[pl.ds(c * half, half), :]) for j, chip in enumerate(chips)]
        for cp in first:
            cp.start()
        passed = [copy(3 + j, chip, c, sibling) for j, chip in enumerate(chips)]
        for j, chip in enumerate(chips):
            copy(j, chip, c, (x, y, c)).wait_recv()
            passed[j].start()
        for j, chip in enumerate(chips):
            copy(3 + j, chip, 1 - c, (x, y, c)).wait_recv()
        for cp in first + passed:
            cp.wait_send()
        mine.wait()

    return pl.pallas_call(
        body, out_shape=jax.ShapeDtypeStruct((N_CHIPS, rows, wd), wpk.dtype),
        in_specs=[ANY], out_specs=ANY,
        scratch_shapes=[pltpu.SemaphoreType.DMA((6,)), pltpu.SemaphoreType.DMA((6,)), pltpu.SemaphoreType.DMA],
        name="gather_weights")(wpk)


def _sibling_exchange(g):
    n, rows, wd = g.shape
    half = rows // 2

    def body(g_ref, out_ref, send_sem, recv_sem):
        x, y, c = _place()
        cp = pltpu.make_async_remote_copy(
            src_ref=g_ref.at[:, pl.ds((1 - c) * half, half), :], dst_ref=out_ref,
            send_sem=send_sem, recv_sem=recv_sem, device_id=(x, y, 1 - c), device_id_type=MESH)
        cp.start()
        cp.wait()

    return pl.pallas_call(
        body, out_shape=jax.ShapeDtypeStruct((n, half, wd), g.dtype), in_specs=[ANY], out_specs=ANY,
        scratch_shapes=[pltpu.SemaphoreType.DMA, pltpu.SemaphoreType.DMA], name="grad_sibling_exchange")(g)


def _chip_exchange(part):
    n, half, wd = part.shape

    def body(p_ref, out_ref, send_sems, recv_sems):
        x, y, c = _place()
        chips = [(1 - x, y), (x, 1 - y), (1 - x, 1 - y)]

        def copy(j, chip):
            return pltpu.make_async_remote_copy(
                src_ref=p_ref.at[2 * chip[0] + chip[1]], dst_ref=out_ref.at[j],
                send_sem=send_sems.at[j], recv_sem=recv_sems.at[j], device_id=(*chip, c), device_id_type=MESH)

        cps = [copy(j, chip) for j, chip in enumerate(chips)]
        for cp in cps:
            cp.start()
        for cp in cps:
            cp.wait()

    return pl.pallas_call(
        body, out_shape=jax.ShapeDtypeStruct((3, half, wd), part.dtype), in_specs=[ANY], out_specs=ANY,
        scratch_shapes=[pltpu.SemaphoreType.DMA((3,)), pltpu.SemaphoreType.DMA((3,))], name="grad_chip_exchange")(part)


def _sibling_concat(mine):
    half, wd = mine.shape

    def body(m_ref, out_ref, send_sem, recv_sem, local_sem):
        x, y, c = _place()
        loc = pltpu.make_async_copy(m_ref, out_ref.at[c], local_sem)
        loc.start()
        cp = pltpu.make_async_remote_copy(
            src_ref=m_ref, dst_ref=out_ref.at[c], send_sem=send_sem, recv_sem=recv_sem,
            device_id=(x, y, 1 - c), device_id_type=MESH)
        cp.start()
        pltpu.make_async_remote_copy(
            src_ref=m_ref, dst_ref=out_ref.at[1 - c], send_sem=send_sem, recv_sem=recv_sem,
            device_id=(x, y, 1 - c), device_id_type=MESH).wait_recv()
        cp.wait_send()
        loc.wait()

    return pl.pallas_call(
        body, out_shape=jax.ShapeDtypeStruct((2, half, wd), mine.dtype), in_specs=[ANY], out_specs=ANY,
        scratch_shapes=[pltpu.SemaphoreType.DMA, pltpu.SemaphoreType.DMA, pltpu.SemaphoreType.DMA],
        name="grad_sibling_concat")(mine)


def _adamw(w, g, m, v):
    m = ADAM_B1 * m + (1.0 - ADAM_B1) * g
    v = ADAM_B2 * v + (1.0 - ADAM_B2) * (g * g)
    m_hat = m / (1.0 - ADAM_B1 ** ADAM_STEP)
    v_hat = v / (1.0 - ADAM_B2 ** ADAM_STEP)
    delta = -ADAM_LR * (m_hat / (jnp.sqrt(v_hat) + ADAM_EPS) + ADAM_WD * w)
    return delta, m, v


def _small_allreduce_adamw(g, w, m, v):
    rows, wd = g.shape
    n_dev = 8

    def body(g_ref, w_ref, m_ref, v_ref, go_ref, d_ref, mo_ref, vo_ref, all_ref, send_sems, recv_sems):
        x, y, c = _place()
        me, sibling = (x, y, c), (x, y, 1 - c)
        chips = [(1 - x, y), (x, 1 - y), (1 - x, 1 - y)]

        def slot(px, py, pc):
            return all_ref.at[4 * px + 2 * py + pc]

        def copy(k, block, to, src=None):
            return pltpu.make_async_remote_copy(
                src_ref=slot(*block) if src is None else src, dst_ref=slot(*block),
                send_sem=send_sems.at[k], recv_sem=recv_sems.at[k], device_id=to, device_id_type=MESH)

        all_ref[4 * x + 2 * y + c] = g_ref[...]
        first = [copy(0, me, sibling, src=g_ref)]
        first += [copy(1 + j, me, (*chip, c), src=g_ref) for j, chip in enumerate(chips)]
        for cp in first:
            cp.start()
        passed = [copy(4 + j, (*chip, c), sibling) for j, chip in enumerate(chips)]
        for j, chip in enumerate(chips):
            copy(1 + j, (*chip, c), me).wait_recv()
            passed[j].start()
        copy(0, sibling, me).wait_recv()
        for j, chip in enumerate(chips):
            copy(4 + j, (*chip, 1 - c), me).wait_recv()
        for cp in first + passed:
            cp.wait_send()
        tot = jnp.zeros((rows, wd), F32)
        for dev in range(n_dev):
            tot = tot + all_ref[dev]
        delta, m_new, v_new = _adamw(w_ref[...], tot, m_ref[...], v_ref[...])
        go_ref[...] = tot
        d_ref[...] = delta
        mo_ref[...] = m_new
        vo_ref[...] = v_new

    vm = pl.BlockSpec(memory_space=pltpu.VMEM)
    shp = jax.ShapeDtypeStruct((rows, wd), F32)
    return pl.pallas_call(
        body, out_shape=[shp, shp, shp, shp], in_specs=[vm] * 4, out_specs=[vm] * 4,
        scratch_shapes=[pltpu.VMEM((n_dev, rows, wd), F32), pltpu.SemaphoreType.DMA((7,)), pltpu.SemaphoreType.DMA((7,))],
        name="small_allreduce_adamw")(g, w, m, v)


def kernel(x, positions, ln_pre_mix, ln_post_mix, ln_pre_mlp, ln_post_mlp, w_in, b_in, q_a_norm, w_uq, kv_a_norm, w_uk, w_uv, w_o_mla, w_o_fox, w_out, w_ff1, w_ff2, loss_target, m_ln_pre_mix, m_ln_post_mix, m_ln_pre_mlp, m_ln_post_mlp, m_w_in, m_b_in, m_q_a_norm, m_w_uq, m_kv_a_norm, m_w_uk, m_w_uv, m_w_o_mla, m_w_o_fox, m_w_out, m_w_ff1, m_w_ff2, v_ln_pre_mix, v_ln_post_mix, v_ln_pre_mlp, v_ln_post_mlp, v_w_in, v_b_in, v_q_a_norm, v_w_uq, v_kv_a_norm, v_w_uk, v_w_uv, v_w_o_mla, v_w_o_fox, v_w_out, v_w_ff1, v_w_ff2):
    w = dict(ln_pre_mix=ln_pre_mix, ln_post_mix=ln_post_mix, ln_pre_mlp=ln_pre_mlp, ln_post_mlp=ln_post_mlp, w_in=w_in,
             b_in=b_in, q_a_norm=q_a_norm, w_uq=w_uq, kv_a_norm=kv_a_norm, w_uk=w_uk, w_uv=w_uv, w_o_mla=w_o_mla,
             w_o_fox=w_o_fox, w_out=w_out, w_ff1=w_ff1, w_ff2=w_ff2)
    mom = dict(ln_pre_mix=m_ln_pre_mix, ln_post_mix=m_ln_post_mix, ln_pre_mlp=m_ln_pre_mlp, ln_post_mlp=m_ln_post_mlp,
               w_in=m_w_in, b_in=m_b_in, q_a_norm=m_q_a_norm, w_uq=m_w_uq, kv_a_norm=m_kv_a_norm, w_uk=m_w_uk,
               w_uv=m_w_uv, w_o_mla=m_w_o_mla, w_o_fox=m_w_o_fox, w_out=m_w_out, w_ff1=m_w_ff1, w_ff2=m_w_ff2)
    var = dict(ln_pre_mix=v_ln_pre_mix, ln_post_mix=v_ln_post_mix, ln_pre_mlp=v_ln_pre_mlp, ln_post_mlp=v_ln_post_mlp,
               w_in=v_w_in, b_in=v_b_in, q_a_norm=v_q_a_norm, w_uq=v_w_uq, kv_a_norm=v_kv_a_norm, w_uk=v_w_uk,
               w_uv=v_w_uv, w_o_mla=v_w_o_mla, w_o_fox=v_w_o_fox, w_out=v_w_out, w_ff1=v_w_ff1, w_ff2=v_w_ff2)

    big_names = [nm for nm, _ in BIG]
    shard_shapes = [w[nm].shape[1:] for nm in big_names]
    n_shard = sum(a * b for a, b in shard_shapes)
    rows = _pack_rows(n_shard)
    half = rows // 2

    gathered = _gather_weights(_pack([w[nm][0].astype(BF16) for nm in big_names], rows))
    per_chip = [_unpack(gathered[ch], shard_shapes) for ch in range(N_CHIPS)]
    full = {nm: wv for nm, wv in w.items() if nm in SMALL}
    for k, (nm, axis) in enumerate(BIG):
        full[nm] = jnp.concatenate([per_chip[ch][k] for ch in range(N_CHIPS)], axis=axis - 1)

    loss_local, grad_x, grads = _local_step(x[0], positions[0], loss_target[0], full)
    loss = lax.psum(loss_local, ("x", "y", "c"))

    gpk = jnp.stack([
        _pack([jnp.split(grads[nm], N_CHIPS, axis=axis - 1)[ch] for nm, axis in BIG], rows) for ch in range(N_CHIPS)])
    c = lax.axis_index("c")
    chip = 2 * lax.axis_index("x") + lax.axis_index("y")
    theirs = _sibling_exchange(gpk)
    mine_half = lax.dynamic_slice_in_dim(gpk, c * half, half, axis=1)

    def f_add2(ti, pa):
        tot = ti[0] + ti[1]
        return [tot, tot], []

    chip_part, chip_part_bf = _rowwise("grad_add_sibling", f_add2, [mine_half.reshape(N_CHIPS * half, PACK_W),
                                                                  theirs.reshape(N_CHIPS * half, PACK_W)], [],
                                       [(PACK_W, F32), (PACK_W, BF16)])
    others = _chip_exchange(chip_part_bf.reshape(N_CHIPS, half, PACK_W))
    own = lax.dynamic_index_in_dim(chip_part.reshape(N_CHIPS, half, PACK_W), chip, axis=0, keepdims=False)

    def f_add4(ti, pa):
        return [((ti[0] + ti[1].astype(F32)) + ti[2].astype(F32)) + ti[3].astype(F32)], []

    (red_half,) = _rowwise("grad_add_chips", f_add4, [own, others[0], others[1], others[2]], [], [(PACK_W, F32)])
    g_shard = _sibling_concat(red_half).reshape(rows, PACK_W)

    def f_adamw(ti, pa):
        wv, gv, mv, vv = ti
        return list(_adamw(wv, gv, mv, vv)), []

    w_shard = _pack([w[nm][0] for nm in big_names], rows)
    m_shard = _pack([mom[nm][0] for nm in big_names], rows)
    v_shard = _pack([var[nm][0] for nm in big_names], rows)
    d_shard, m_new, v_new = _rowwise("adamw", f_adamw, [w_shard, g_shard, m_shard, v_shard], [], [(PACK_W, F32)] * 3)
    out = {"grad": {}, "delta": {}, "m": {}, "v": {}}
    for kind, packed in (("grad", g_shard), ("delta", d_shard), ("m", m_new), ("v", v_new)):
        for nm, arr in zip(big_names, _unpack(packed, shard_shapes), strict=True):
            out[kind][nm] = arr[None]

    small_shapes = [w[nm].shape for nm in SMALL]
    s_rows = SMALL_ROWS
    assert sum(a * b for a, b in small_shapes) <= s_rows * PACK_W
    sg, sd, sm, sv = _small_allreduce_adamw(
        _pack([grads[nm] for nm in SMALL], s_rows), _pack([w[nm] for nm in SMALL], s_rows),
        _pack([mom[nm] for nm in SMALL], s_rows), _pack([var[nm] for nm in SMALL], s_rows))
    for kind, packed in (("grad", sg), ("delta", sd), ("m", sm), ("v", sv)):
        for nm, arr in zip(SMALL, _unpack(packed, small_shapes), strict=True):
            out[kind][nm] = arr

    return (loss, grad_x[None], *[out["grad"][nm] for nm in ALL_W], *[out["delta"][nm] for nm in ALL_W],
            *[out["m"][nm] for nm in ALL_W], *[out["v"][nm] for nm in ALL_W])
```

```python
import functools

import jax
import jax.numpy as jnp
from jax import lax
from jax.experimental import pallas as pl
from jax.experimental.pallas import tpu as pltpu

F32 = jnp.float32
BF16 = jnp.bfloat16

MLA_HEADS = 8
MLA_Q_LORA = 256
MLA_KV_LORA = 128
MLA_NOPE = 64
MLA_ROPE = 32
MLA_V = 64
FOX_HEADS = 8
FOX_DIM = 64
ROPE_THETA = 10000.0
NORM_EPS = 1e-6
HALF_ROPE = MLA_ROPE // 2

ADAM_LR = 0.001
ADAM_B1 = 0.9
ADAM_B2 = 0.999
ADAM_EPS = 1e-08
ADAM_WD = 0.01
ADAM_STEP = 10

LANES = 128
VMEM_LIMIT = 56 * 1024 * 1024
ATT_TILE = 512
MM_LHS_BLOCK_BYTES = 9 * 1024 * 1024
NEG = -1e30
MESH = pl.DeviceIdType.MESH

BIG = (("w_in", 2), ("w_uq", 2), ("w_uk", 2), ("w_uv", 2), ("w_o_mla", 2), ("w_o_fox", 2),
       ("w_out", 1), ("w_ff1", 2), ("w_ff2", 1))
SMALL = ("ln_pre_mix", "ln_post_mix", "ln_pre_mlp", "ln_post_mlp", "b_in", "q_a_norm", "kv_a_norm")
ALL_W = ("ln_pre_mix", "ln_post_mix", "ln_pre_mlp", "ln_post_mlp", "w_in", "b_in", "q_a_norm", "w_uq",
         "kv_a_norm", "w_uk", "w_uv", "w_o_mla", "w_o_fox", "w_out", "w_ff1", "w_ff2")
N_CHIPS = 4
PACK_W = 1024


def _cparams(sem=None):
    return pltpu.CompilerParams(dimension_semantics=sem, vmem_limit_bytes=VMEM_LIMIT)


def _divisor_tile(n, limit, mult):
    if n <= limit:
        return n
    best = None
    t = mult
    while t <= limit:
        if n % t == 0:
            best = t
        t += mult
    assert best is not None, (n, limit, mult)
    return best


def _mm(a, b, *, out_dtype, name, bias=None, transpose_b=False, extras=(), epilogue=None):
    m, k = a.shape
    n = b.shape[0] if transpose_b else b.shape[1]
    assert (b.shape[1] if transpose_b else b.shape[0]) == k and a.dtype == BF16 and b.dtype == BF16
    out_dtypes = list(out_dtype) if isinstance(out_dtype, (list, tuple)) else [out_dtype]
    tm = _divisor_tile(m, min(2048, max(512, MM_LHS_BLOCK_BYTES // (2 * k))), 16)
    tn = _divisor_tile(n, 512, LANES)
    n_ex = len(extras)

    def body(*refs):
        a_ref, b_ref = refs[:2]
        pos = 2
        bias_ref = None
        if bias is not None:
            bias_ref = refs[pos]
            pos += 1
        ex_refs = refs[pos:pos + n_ex]
        o_refs = refs[pos + n_ex:]
        if transpose_b:
            acc = lax.dot_general(a_ref[...], b_ref[...], _NT, preferred_element_type=F32)
        else:
            acc = jnp.dot(a_ref[...], b_ref[...], preferred_element_type=F32)
        if bias_ref is not None:
            acc = acc + bias_ref[...]
        vals = [acc] if epilogue is None else epilogue(acc, [r[...] for r in ex_refs])
        for ref, val in zip(o_refs, vals, strict=True):
            ref[...] = val.astype(ref.dtype)

    b_spec = pl.BlockSpec((tn, k), lambda i, j: (j, 0)) if transpose_b else pl.BlockSpec((k, tn), lambda i, j: (0, j))
    in_specs = [pl.BlockSpec((tm, k), lambda i, j: (i, 0)), b_spec]
    args = [a, b]
    if bias is not None:
        in_specs.append(pl.BlockSpec((1, tn), lambda i, j: (0, j)))
        args.append(bias)
    in_specs += [pl.BlockSpec((tm, tn), lambda i, j: (i, j)) for _ in extras]
    args += list(extras)
    res = pl.pallas_call(
        body, grid=(m // tm, n // tn), in_specs=in_specs,
        out_specs=[pl.BlockSpec((tm, tn), lambda i, j: (i, j)) for _ in out_dtypes],
        out_shape=[jax.ShapeDtypeStruct((m, n), dt) for dt in out_dtypes],
        compiler_params=_cparams(("parallel", "parallel")), name=name)(*args)
    return res if isinstance(out_dtype, (list, tuple)) else res[0]


def _mm_tn(a, b, *, name):
    s, m = a.shape
    s2, n = b.shape
    assert s == s2 and a.dtype == BF16 and b.dtype == BF16
    tm = _divisor_tile(m, 1024, LANES)
    tn = _divisor_tile(n, 2304, LANES)
    tk = _divisor_tile(s, 512, 16)

    def body(a_ref, b_ref, o_ref):
        @pl.when(pl.program_id(2) == 0)
        def _():
            o_ref[...] = jnp.zeros_like(o_ref)

        o_ref[...] += lax.dot_general(a_ref[...], b_ref[...], (((0,), (0,)), ((), ())),
                                      preferred_element_type=F32)

    return pl.pallas_call(
        body, grid=(m // tm, n // tn, s // tk),
        in_specs=[pl.BlockSpec((tk, tm), lambda i, j, k: (k, i)), pl.BlockSpec((tk, tn), lambda i, j, k: (k, j))],
        out_specs=pl.BlockSpec((tm, tn), lambda i, j, k: (i, j)),
        out_shape=jax.ShapeDtypeStruct((m, n), F32),
        compiler_params=_cparams(("parallel", "parallel", "arbitrary")), name=name)(a, b)


def _rowwise(name, fn, tiled, params, outs, reds=()):
    wins = [t if isinstance(t, tuple) else (t, 0, t.shape[1]) for t in tiled]
    s = wins[0][0].shape[0]
    row_bytes = sum(w * arr.dtype.itemsize for arr, _, w in wins) + sum(w * jnp.dtype(d).itemsize for w, d in outs)
    ts = _divisor_tile(s, max(16, min(1024, (6 * 1024 * 1024) // row_bytes)), 16)
    nt, npar, nout = len(wins), len(params), len(outs)

    def body(*refs):
        tin = [r[...] for r in refs[:nt]]
        par = [r[...] for r in refs[nt:nt + npar]]
        out_refs = refs[nt + npar:nt + npar + nout]
        red_refs = refs[nt + npar + nout:]
        o, r = fn(tin, par)
        for ref, val in zip(out_refs, o, strict=True):
            ref[...] = val.astype(ref.dtype)
        if red_refs:
            @pl.when(pl.program_id(0) == 0)
            def _():
                for ref in red_refs:
                    ref[...] = jnp.zeros_like(ref)

            for ref, val in zip(red_refs, r, strict=True):
                ref[...] += val

    in_specs = [pl.BlockSpec((ts, w), functools.partial(lambda i, cb: (i, cb), cb=cb)) for _, cb, w in wins]
    in_specs += [pl.BlockSpec(p.shape, lambda i: (0, 0)) for p in params]
    out_specs = [pl.BlockSpec((ts, w), lambda i: (i, 0)) for w, _ in outs]
    out_specs += [pl.BlockSpec((1, w), lambda i: (0, 0)) for w in reds]
    out_shape = [jax.ShapeDtypeStruct((s, w), d) for w, d in outs]
    out_shape += [jax.ShapeDtypeStruct((1, w), F32) for w in reds]
    res = pl.pallas_call(
        body, grid=(s // ts,), in_specs=in_specs, out_specs=out_specs, out_shape=out_shape,
        compiler_params=_cparams(("arbitrary",)), name=name)(*[w[0] for w in wins], *params)
    return res


def _rms(x, g):
    r = lax.rsqrt(jnp.mean(x * x, axis=-1, keepdims=True) + NORM_EPS)
    return x * r * g, r


def _rms_bwd(x, g, dy):
    r = lax.rsqrt(jnp.mean(x * x, axis=-1, keepdims=True) + NORM_EPS)
    gy = dy * g
    dx = r * gy - x * (r * r * r) * jnp.mean(x * gy, axis=-1, keepdims=True)
    dg = jnp.sum(dy * (x * r), axis=0, keepdims=True)
    return dx, dg


def _sigmoid(x):
    return 1.0 / (1.0 + jnp.exp(-x))


def _forget_cumsum(z, cb):
    s = z.shape[0]
    ts = _divisor_tile(s, 512, LANES)

    def body(x_ref, col_ref, row_ref, carry):
        @pl.when(pl.program_id(0) == 0)
        def _():
            carry[...] = jnp.zeros_like(carry)

        x = x_ref[...]
        lf = jnp.minimum(x, 0.0) - jnp.log1p(jnp.exp(-jnp.abs(x)))
        r = lax.broadcasted_iota(jnp.int32, (ts, ts), 0)
        c = lax.broadcasted_iota(jnp.int32, (ts, ts), 1)
        tri = jnp.where(c <= r, 1.0, 0.0).astype(F32)
        cs = jnp.dot(tri, lf, preferred_element_type=F32, precision=lax.Precision.HIGHEST) + carry[...]
        carry[...] += jnp.sum(lf, axis=0, keepdims=True)
        col_ref[...] = cs
        row_ref[...] = cs.T

    return pl.pallas_call(
        body, grid=(s // ts,),
        in_specs=[pl.BlockSpec((ts, LANES), lambda i: (i, cb))],
        out_specs=[pl.BlockSpec((ts, LANES), lambda i: (i, 0)), pl.BlockSpec((LANES, ts), lambda i: (0, i))],
        out_shape=[jax.ShapeDtypeStruct((s, LANES), F32), jax.ShapeDtypeStruct((LANES, s), F32)],
        scratch_shapes=[pltpu.VMEM((1, LANES), F32)],
        compiler_params=_cparams(("arbitrary",)), name="forget_cumsum")(z)


def _forget_cumsum_bwd(dfq, dfs, z, cb):
    s = z.shape[0]
    ts = _divisor_tile(s, 512, LANES)
    nt = s // ts

    def body(dq_ref, d_ref, x_ref, o_ref, carry):
        @pl.when(pl.program_id(0) == 0)
        def _():
            carry[...] = jnp.zeros_like(carry)

        r = lax.broadcasted_iota(jnp.int32, (ts, ts), 0)
        c = lax.broadcasted_iota(jnp.int32, (ts, ts), 1)
        tri = jnp.where(c >= r, 1.0, 0.0).astype(F32)
        dv = dq_ref[...] - d_ref[...]
        rc = jnp.dot(tri, dv, preferred_element_type=F32, precision=lax.Precision.HIGHEST) + carry[...]
        carry[...] += jnp.sum(dv, axis=0, keepdims=True)
        o_ref[...] = (rc * (1.0 / (1.0 + jnp.exp(x_ref[...])))).astype(o_ref.dtype)

    return pl.pallas_call(
        body, grid=(nt,),
        in_specs=[pl.BlockSpec((ts, LANES), lambda i: (nt - 1 - i, 0)),
                  pl.BlockSpec((ts, LANES), lambda i: (nt - 1 - i, 0)),
                  pl.BlockSpec((ts, LANES), lambda i: (nt - 1 - i, cb))],
        out_specs=pl.BlockSpec((ts, LANES), lambda i: (nt - 1 - i, 0)),
        out_shape=jax.ShapeDtypeStruct((s, LANES), BF16),
        scratch_shapes=[pltpu.VMEM((1, LANES), F32)],
        compiler_params=_cparams(("arbitrary",)), name="forget_cumsum_bwd")(dfq, dfs, z)


_NT = (((1,), (1,)), ((), ()))


def _head_column(fcol_tile, h):
    lane = lax.broadcasted_iota(jnp.int32, fcol_tile.shape, 1)
    return jnp.sum(jnp.where(lane == h, fcol_tile, 0.0), axis=1, keepdims=True)


STAT_ROWS = 8


def _as_rows(col):
    return jnp.broadcast_to(col, (col.shape[0], LANES)).T[0:STAT_ROWS, :]


def _flash_fwd(q, k, v, scale, fcol=None, frow=None, *, name):
    nh, s, dk = q.shape
    dv = v.shape[2]
    t = min(ATT_TILE, s)
    nq = s // t
    fox = fcol is not None

    def body(*refs):
        if fox:
            q_ref, k_ref, v_ref, fcol_ref, frow_ref, o_ref, lse_ref = refs
        else:
            q_ref, k_ref, v_ref, o_ref, lse_ref = refs
        h = pl.program_id(0)
        i = pl.program_id(1)
        qb = q_ref[0]
        if fox:
            fq = _head_column(fcol_ref[...], h)

        def scores(j):
            kb = k_ref[0, pl.ds(pl.multiple_of(j * t, t), t), :]
            sc = lax.dot_general(qb, kb, _NT, preferred_element_type=F32) * scale
            if fox:
                sc = sc + fq - frow_ref[0, j]
            return sc

        def update(j, sc, carry):
            m, l, acc = carry
            m_new = jnp.maximum(m, jnp.max(sc, axis=1, keepdims=True))
            p = jnp.exp(sc - m_new)
            alpha = jnp.exp(m - m_new)
            l = alpha * l + jnp.sum(p, axis=1, keepdims=True)
            vb = v_ref[0, pl.ds(pl.multiple_of(j * t, t), t), :]
            acc = alpha * acc + jnp.dot(p.astype(BF16), vb, preferred_element_type=F32)
            return m_new, l, acc

        init = (jnp.full((t, 1), NEG, F32), jnp.zeros((t, 1), F32), jnp.zeros((t, dv), F32))
        carry = lax.fori_loop(0, i, lambda j, c: update(j, scores(j), c), init)
        row = lax.broadcasted_iota(jnp.int32, (t, t), 0)
        col = lax.broadcasted_iota(jnp.int32, (t, t), 1)
        m, l, acc = update(i, jnp.where(col <= row, scores(i), NEG), carry)
        o_ref[0] = (acc / l).astype(o_ref.dtype)
        lse_ref[0, 0] = _as_rows(m + jnp.log(l))

    in_specs = [pl.BlockSpec((1, t, dk), lambda h, i: (h, i, 0)),
                pl.BlockSpec((1, s, dk), lambda h, i: (h, 0, 0)),
                pl.BlockSpec((1, s, dv), lambda h, i: (h, 0, 0))]
    args = [q, k, v]
    if fox:
        in_specs += [pl.BlockSpec((t, LANES), lambda h, i: (i, 0)),
                     pl.BlockSpec((1, nq, 1, t), lambda h, i: (h, 0, 0, 0))]
        args += [fcol, frow]
    return pl.pallas_call(
        body, grid=(nh, nq), in_specs=in_specs,
        out_specs=[pl.BlockSpec((1, t, dv), lambda h, i: (h, i, 0)),
                   pl.BlockSpec((1, 1, STAT_ROWS, t), lambda h, i: (h, i, 0, 0))],
        out_shape=[jax.ShapeDtypeStruct((nh, s, dv), BF16), jax.ShapeDtypeStruct((nh, nq, STAT_ROWS, t), F32)],
        compiler_params=_cparams(("parallel", "arbitrary")), name=name)(*args)


def _head_dots(a, b, nh, *, name):
    wd = a.shape[1]
    d = wd // nh

    def fn(ti, pa):
        prod = ti[0].astype(F32) * ti[1].astype(F32)
        col = lax.broadcasted_iota(jnp.int32, (wd, LANES), 0)
        lane = lax.broadcasted_iota(jnp.int32, (wd, LANES), 1)
        sel = jnp.where((col >= lane * d) & (col < (lane + 1) * d), 1.0, 0.0).astype(F32)
        return [jnp.dot(prod, sel, preferred_element_type=F32, precision=lax.Precision.HIGHEST)], []

    (out,) = _rowwise(name, fn, [a, b], [], [(LANES, F32)])
    return out


def _flash_bwd(q, k, v, do, lse_row, delta_row, scale, fcol=None, frow=None, *, name):
    nh, s, dk = q.shape
    dv = v.shape[2]
    t = min(ATT_TILE, s)
    nq = s // t
    fox = fcol is not None
    _tn = (((0,), (0,)), ((), ()))

    def body(*refs):
        if fox:
            (q_ref, k_ref, v_ref, do_ref, lse_ref, delta_ref, fcol_ref, frow_ref,
             dk_ref, dv_ref, dq_ref, dfs_ref, dfq_ref) = refs
        else:
            q_ref, k_ref, v_ref, do_ref, lse_ref, delta_ref, dk_ref, dv_ref, dq_ref = refs
        h = pl.program_id(0)
        j = pl.program_id(1)
        kb = k_ref[0]
        vb = v_ref[0]

        @pl.when(j == 0)
        def _():
            dq_ref[...] = jnp.zeros_like(dq_ref)
            if fox:
                dfq_ref[...] = jnp.zeros_like(dfq_ref)

        if fox:
            fk = _head_column(fcol_ref[...], h)

        def block(i, carry, masked):
            dk_acc, dv_acc, dfs_acc = carry
            qb = q_ref[0, pl.ds(pl.multiple_of(i * t, t), t), :]
            dob = do_ref[0, pl.ds(pl.multiple_of(i * t, t), t), :]
            st = lax.dot_general(kb, qb, _NT, preferred_element_type=F32) * scale
            if fox:
                st = st + frow_ref[0, i] - fk
            if masked:
                key = lax.broadcasted_iota(jnp.int32, (t, t), 0)
                qry = lax.broadcasted_iota(jnp.int32, (t, t), 1)
                st = jnp.where(key <= qry, st, NEG)
            pt = jnp.exp(st - lse_ref[0, i, 0:1, :])
            dv_acc = dv_acc + jnp.dot(pt.astype(BF16), dob, preferred_element_type=F32)
            dpt = lax.dot_general(vb, dob, _NT, preferred_element_type=F32)
            dst = pt * (dpt - delta_ref[0, i, 0:1, :])
            dsb = dst.astype(BF16)
            dk_acc = dk_acc + jnp.dot(dsb, qb, preferred_element_type=F32)
            rows = pl.ds(pl.multiple_of(i * t, t), t)
            dq_ref[0, rows, :] += lax.dot_general(dsb, kb, _tn, preferred_element_type=F32)
            if fox:
                dfs_acc = dfs_acc + jnp.sum(dst, axis=1, keepdims=True)
                dfq_ref[0, i] += jnp.sum(dst, axis=0, keepdims=True)
            return dk_acc, dv_acc, dfs_acc

        init = (jnp.zeros((t, dk), F32), jnp.zeros((t, dv), F32), jnp.zeros((t, 1), F32))
        carry = block(j, init, True)
        dk_acc, dv_acc, dfs_acc = lax.fori_loop(j + 1, nq, lambda i, c: block(i, c, False), carry)
        dk_ref[0] = dk_acc * scale
        dv_ref[0] = dv_acc
        if fox:
            dfs_ref[0, 0] = _as_rows(dfs_acc)

        @pl.when(j == nq - 1)
        def _():
            dq_ref[...] = dq_ref[...] * scale

    in_specs = [pl.BlockSpec((1, s, dk), lambda h, j: (h, 0, 0)),
                pl.BlockSpec((1, t, dk), lambda h, j: (h, j, 0)),
                pl.BlockSpec((1, t, dv), lambda h, j: (h, j, 0)),
                pl.BlockSpec((1, s, dv), lambda h, j: (h, 0, 0)),
                pl.BlockSpec((1, nq, lse_row.shape[2], t), lambda h, j: (h, 0, 0, 0)),
                pl.BlockSpec((1, nq, delta_row.shape[2], t), lambda h, j: (h, 0, 0, 0))]
    args = [q, k, v, do, lse_row, delta_row]
    out_specs = [pl.BlockSpec((1, t, dk), lambda h, j: (h, j, 0)), pl.BlockSpec((1, t, dv), lambda h, j: (h, j, 0)),
                 pl.BlockSpec((1, s, dk), lambda h, j: (h, 0, 0))]
    out_shape = [jax.ShapeDtypeStruct((nh, s, dk), F32), jax.ShapeDtypeStruct((nh, s, dv), F32),
                 jax.ShapeDtypeStruct((nh, s, dk), F32)]
    if fox:
        in_specs += [pl.BlockSpec((t, LANES), lambda h, j: (j, 0)),
                     pl.BlockSpec((1, nq, 1, t), lambda h, j: (h, 0, 0, 0))]
        args += [fcol, frow]
        out_specs += [pl.BlockSpec((1, 1, STAT_ROWS, t), lambda h, j: (h, j, 0, 0)),
                      pl.BlockSpec((1, nq, 1, t), lambda h, j: (h, 0, 0, 0))]
        out_shape += [jax.ShapeDtypeStruct((nh, nq, STAT_ROWS, t), F32), jax.ShapeDtypeStruct((nh, nq, 1, t), F32)]
    return pl.pallas_call(
        body, grid=(nh, nq), in_specs=in_specs, out_specs=out_specs, out_shape=out_shape,
        compiler_params=_cparams(("parallel", "arbitrary")), name=name)(*args)


class _ZLayout:
    def __init__(self, d):
        fw = FOX_HEADS * FOX_DIM
        self.d = d
        self.src = {}
        off = 0
        for nm, w in (("cq", MLA_Q_LORA), ("ckv", MLA_KV_LORA), ("kr", MLA_ROPE), ("fq", fw), ("fk", fw),
                      ("fv", fw), ("fl", FOX_HEADS), ("ga", d), ("gb", d)):
            self.src[nm] = (off, w)
            off += w
        self.d_in = off
        self.dst = {}
        off = 0
        for nm, w in (("ga", d), ("gb", d), ("fq", fw), ("fk", fw), ("fv", fw), ("cq", MLA_Q_LORA),
                      ("ckv", MLA_KV_LORA), ("kr1", LANES), ("kr2", LANES), ("fl", LANES)):
            assert off % w == 0
            self.dst[nm] = (off, w)
            off += w
        self.width = off

    def win(self, z, nm):
        off, w = self.dst[nm]
        return (z, off // w, w)

    def cols(self, z, nm):
        off, w = self.dst[nm]
        return z[:, off:off + w]

    def to_kernel(self, w):
        def seg(nm, lo=0, hi=None):
            off, wd = self.src[nm]
            hi = wd if hi is None else hi
            return w[..., off + lo:off + hi]

        def pad(a):
            return jnp.pad(a, [(0, 0)] * (a.ndim - 1) + [(0, LANES - a.shape[-1])])

        return jnp.concatenate([seg("ga"), seg("gb"), seg("fq"), seg("fk"), seg("fv"), seg("cq"), seg("ckv"),
                                pad(seg("kr", 0, HALF_ROPE)), pad(seg("kr", HALF_ROPE, MLA_ROPE)), pad(seg("fl"))],
                               axis=-1)

    def from_kernel(self, g):
        def seg(nm, hi=None):
            off, wd = self.dst[nm]
            return g[..., off:off + (wd if hi is None else hi)]

        return jnp.concatenate([seg("cq"), seg("ckv"), seg("kr1", HALF_ROPE), seg("kr2", HALF_ROPE), seg("fq"),
                                seg("fk"), seg("fv"), seg("fl", FOX_HEADS), seg("ga"), seg("gb")], axis=-1)


def _uq_to_kernel(w):
    r = w.shape[0]
    w3 = w.reshape(r, MLA_HEADS, MLA_NOPE + MLA_ROPE)
    return jnp.concatenate([w3[:, :, :MLA_NOPE].reshape(r, -1),
                            w3[:, :, MLA_NOPE:MLA_NOPE + HALF_ROPE].reshape(r, -1),
                            w3[:, :, MLA_NOPE + HALF_ROPE:].reshape(r, -1)], axis=1)


def _uq_from_kernel(g):
    r = g.shape[0]
    n0 = MLA_HEADS * MLA_NOPE
    n1 = MLA_HEADS * HALF_ROPE
    return jnp.concatenate([g[:, :n0].reshape(r, MLA_HEADS, MLA_NOPE),
                            g[:, n0:n0 + n1].reshape(r, MLA_HEADS, HALF_ROPE),
                            g[:, n0 + n1:].reshape(r, MLA_HEADS, HALF_ROPE)], axis=2).reshape(r, -1)


def _heads(a, nh):
    s = a.shape[0]
    return a.reshape(s, nh, -1).transpose(1, 0, 2)


def _merge(a):
    nh, s, d = a.shape
    return a.transpose(1, 0, 2).reshape(s, nh * d)


def _local_step(x, positions, target, wts):
    s, d = x.shape
    zl = _ZLayout(d)
    t_att = min(ATT_TILE, s)
    nope_w = MLA_HEADS * MLA_NOPE
    rope_w = MLA_HEADS * HALF_ROPE
    assert rope_w == LANES

    inv_freq = ROPE_THETA ** (-jnp.arange(HALF_ROPE, dtype=F32) / HALF_ROPE)
    ang = positions.astype(F32)[:, None] * inv_freq
    cos, sin = jnp.cos(ang), jnp.sin(ang)
    cos_q, sin_q = jnp.tile(cos, (1, MLA_HEADS)), jnp.tile(sin, (1, MLA_HEADS))
    cos_k = jnp.pad(cos, ((0, 0), (0, LANES - HALF_ROPE)))
    sin_k = jnp.pad(sin, ((0, 0), (0, LANES - HALF_ROPE)))

    w_in = zl.to_kernel(wts["w_in"])
    b_in = zl.to_kernel(wts["b_in"])
    w_uq = _uq_to_kernel(wts["w_uq"])
    w_ukv = jnp.concatenate([wts["w_uk"], wts["w_uv"]], axis=1)

    def f_norm_in(ti, pa):
        y, _ = _rms(ti[0], pa[0])
        return [y], []

    (h,) = _rowwise("norm_in", f_norm_in, [x], [wts["ln_pre_mix"]], [(d, BF16)])
    z = _mm(h, w_in, bias=b_in, out_dtype=F32, name="proj_in")

    def f_mla_prep(ti, pa):
        cq, ckv, k1, k2, ck, sk = ti
        cqn, _ = _rms(cq, pa[0])
        ckvn, _ = _rms(ckv, pa[1])
        return [cqn, ckvn, k1 * ck - k2 * sk, k2 * ck + k1 * sk], []

    cqn, ckvn, kro1, kro2 = _rowwise(
        "mla_prep", f_mla_prep,
        [zl.win(z, "cq"), zl.win(z, "ckv"), zl.win(z, "kr1"), zl.win(z, "kr2"), cos_k, sin_k],
        [wts["q_a_norm"], wts["kv_a_norm"]],
        [(MLA_Q_LORA, BF16), (MLA_KV_LORA, BF16), (LANES, BF16), (LANES, BF16)])
    qf = _mm(cqn, w_uq, out_dtype=F32, name="proj_uq")
    kv = _mm(ckvn, w_ukv, out_dtype=BF16, name="proj_ukv")

    def f_rope(ti, pa):
        a1, a2, c, sn = ti
        return [a1 * c - a2 * sn, a2 * c + a1 * sn], []

    qr1, qr2 = _rowwise("rope_q", f_rope, [(qf, nope_w // LANES, LANES), (qf, nope_w // LANES + 1, LANES), cos_q, sin_q],
                        [], [(LANES, BF16), (LANES, BF16)])
    zpad = jnp.zeros((s, MLA_HEADS, LANES - MLA_NOPE - MLA_ROPE), BF16)
    q_mla = jnp.concatenate([qf[:, :nope_w].astype(BF16).reshape(s, MLA_HEADS, MLA_NOPE),
                             qr1.reshape(s, MLA_HEADS, HALF_ROPE), qr2.reshape(s, MLA_HEADS, HALF_ROPE), zpad],
                            axis=2).transpose(1, 0, 2)
    k_mla = jnp.concatenate([kv[:, :nope_w].reshape(s, MLA_HEADS, MLA_NOPE),
                             jnp.broadcast_to(kro1[:, None, :HALF_ROPE], (s, MLA_HEADS, HALF_ROPE)),
                             jnp.broadcast_to(kro2[:, None, :HALF_ROPE], (s, MLA_HEADS, HALF_ROPE)), zpad],
                            axis=2).transpose(1, 0, 2)
    v_mla = _heads(kv[:, nope_w:], MLA_HEADS)
    scale_mla = (MLA_NOPE + MLA_ROPE) ** -0.5
    o_mla, lse_mla = _flash_fwd(q_mla, k_mla, v_mla, scale_mla, name="mla_fwd")

    fcol, frow_t = _forget_cumsum(z, zl.dst["fl"][0] // LANES)
    frow = frow_t[:FOX_HEADS].reshape(FOX_HEADS, s // t_att, 1, t_att)
    q_fox = _heads(zl.cols(z, "fq").astype(BF16), FOX_HEADS)
    k_fox = _heads(zl.cols(z, "fk").astype(BF16), FOX_HEADS)
    v_fox = _heads(zl.cols(z, "fv").astype(BF16), FOX_HEADS)
    scale_fox = FOX_DIM ** -0.5
    o_fox, lse_fox = _flash_fwd(q_fox, k_fox, v_fox, scale_fox, fcol, frow, name="fox_fwd")

    om = _merge(o_mla)
    ox = _merge(o_fox)
    y_mla = _mm(om, wts["w_o_mla"], out_dtype=F32, name="proj_o_mla")
    y_fox = _mm(ox, wts["w_o_fox"], out_dtype=F32, name="proj_o_fox")

    def f_gate(ti, pa):
        ga, gb, ya, yb = ti
        return [_sigmoid(ga) * ya + _sigmoid(gb) * yb], []

    (merged,) = _rowwise("gate", f_gate, [zl.win(z, "ga"), zl.win(z, "gb"), y_mla, y_fox], [], [(d, BF16)])
    mix = _mm(merged, wts["w_out"], out_dtype=F32, name="proj_out")

    def f_resid1(ti, pa):
        xa, mx = ti
        y, _ = _rms(mx, pa[0])
        x1 = xa + y
        h2, _ = _rms(x1, pa[1])
        return [x1, h2], []

    x1, h2 = _rowwise("resid_mix", f_resid1, [x, mix], [wts["ln_post_mix"], wts["ln_pre_mlp"]], [(d, F32), (d, BF16)])
    def relu2(acc, ex):
        r = jnp.maximum(acc, 0.0)
        return [acc, r * r]

    u, act = _mm(h2, wts["w_ff1"], out_dtype=[F32, BF16], name="ff1", epilogue=relu2)
    mo = _mm(act, wts["w_ff2"], out_dtype=F32, name="ff2")

    def f_loss(ti, pa):
        xa, mv, tg = ti
        y, _ = _rms(mv, pa[0])
        err = (xa + y) - tg
        g2 = err / d
        dmo, dg = _rms_bwd(mv, pa[0], g2)
        return [g2, dmo], [jnp.sum(err * err, axis=0, keepdims=True), dg]

    g2, d_mo, loss_cols, g_ln_post_mlp = _rowwise("loss", f_loss, [x1, mo, target], [wts["ln_post_mlp"]],
                                                  [(d, F32), (d, BF16)], [d, d])
    loss = 0.5 * jnp.sum(loss_cols) / d

    grads = {"ln_post_mlp": g_ln_post_mlp}
    grads["w_ff2"] = _mm_tn(act, d_mo, name="grad_ff2")
    def relu2_bwd(acc, ex):
        return [acc * (2.0 * jnp.maximum(ex[0], 0.0))]

    (d_u,) = _mm(d_mo, wts["w_ff2"], out_dtype=[BF16], name="ff2_bwd", transpose_b=True, extras=[u], epilogue=relu2_bwd)
    grads["w_ff1"] = _mm_tn(h2, d_u, name="grad_ff1")
    d_h2 = _mm(d_u, wts["w_ff1"], out_dtype=F32, name="ff1_bwd", transpose_b=True)

    def f_resid1_bwd(ti, pa):
        gres, dh2, x1v, mx = ti
        dx1n, dg_pre_mlp = _rms_bwd(x1v, pa[1], dh2)
        dx1 = gres + dx1n
        dmix, dg_post_mix = _rms_bwd(mx, pa[0], dx1)
        return [dx1, dmix], [dg_post_mix, dg_pre_mlp]

    d_x1, d_mix, grads["ln_post_mix"], grads["ln_pre_mlp"] = _rowwise(
        "resid_mix_bwd", f_resid1_bwd, [g2, d_h2, x1, mix], [wts["ln_post_mix"], wts["ln_pre_mlp"]],
        [(d, F32), (d, BF16)], [d, d])
    grads["w_out"] = _mm_tn(merged, d_mix, name="grad_out")
    d_merged = _mm(d_mix, wts["w_out"], out_dtype=F32, name="proj_out_bwd", transpose_b=True)

    def f_gate_bwd(ti, pa):
        dm, ga, gb, ya, yb = ti
        sa, sb = _sigmoid(ga), _sigmoid(gb)
        return [dm * sa, dm * sb, dm * ya * (sa * (1.0 - sa)), dm * yb * (sb * (1.0 - sb))], []

    d_ya, d_yb, d_ga, d_gb = _rowwise("gate_bwd", f_gate_bwd,
                                      [d_merged, zl.win(z, "ga"), zl.win(z, "gb"), y_mla, y_fox], [],
                                      [(d, BF16)] * 4)
    grads["w_o_mla"] = _mm_tn(om, d_ya, name="grad_o_mla")
    grads["w_o_fox"] = _mm_tn(ox, d_yb, name="grad_o_fox")
    dom = _mm(d_ya, wts["w_o_mla"], out_dtype=BF16, name="proj_o_mla_bwd", transpose_b=True)
    dox = _mm(d_yb, wts["w_o_fox"], out_dtype=BF16, name="proj_o_fox_bwd", transpose_b=True)
    do_mla = _heads(dom, MLA_HEADS)
    do_fox = _heads(dox, FOX_HEADS)

    def delta_rows(dm, om_, nh, name):
        return _head_dots(dm, om_, nh, name=name)[:, :nh].T.reshape(nh, s // t_att, 1, t_att)

    dk_mla, dv_mla, dq_mla = _flash_bwd(q_mla, k_mla, v_mla, do_mla, lse_mla,
                                        delta_rows(dom, om, MLA_HEADS, "mla_delta"), scale_mla, name="mla_bwd")
    dk_fox, dv_fox, dq_fox, dfs, dfq = _flash_bwd(q_fox, k_fox, v_fox, do_fox, lse_fox,
                                                  delta_rows(dox, ox, FOX_HEADS, "fox_delta"), scale_fox, fcol, frow,
                                                  name="fox_bwd")

    dfq_cols = jnp.pad(dfq.reshape(FOX_HEADS, s).T, ((0, 0), (0, LANES - FOX_HEADS)))
    dfs_cols = jnp.pad(dfs[:, :, 0, :].reshape(FOX_HEADS, s).T, ((0, 0), (0, LANES - FOX_HEADS)))
    d_fl = _forget_cumsum_bwd(dfq_cols, dfs_cols, z, zl.dst["fl"][0] // LANES)

    dq3 = dq_mla.transpose(1, 0, 2)
    dq1 = dq3[:, :, MLA_NOPE:MLA_NOPE + HALF_ROPE].reshape(s, rope_w)
    dq2 = dq3[:, :, MLA_NOPE + HALF_ROPE:MLA_NOPE + MLA_ROPE].reshape(s, rope_w)

    def f_rope_bwd(ti, pa):
        g1, g2_, c, sn = ti
        return [g1 * c + g2_ * sn, g2_ * c - g1 * sn], []

    dqr1, dqr2 = _rowwise("rope_q_bwd", f_rope_bwd, [dq1, dq2, cos_q, sin_q], [], [(LANES, BF16), (LANES, BF16)])
    d_qf = jnp.concatenate([dq3[:, :, :MLA_NOPE].reshape(s, nope_w).astype(BF16), dqr1, dqr2], axis=1)
    grads["w_uq"] = _uq_from_kernel(_mm_tn(cqn, d_qf, name="grad_uq"))
    d_cqn = _mm(d_qf, w_uq, out_dtype=F32, name="proj_uq_bwd", transpose_b=True)

    dk3 = dk_mla.transpose(1, 0, 2)

    def f_head_sum(ti, pa):
        a = ti[0]
        tot = a[:, 0:LANES]
        for hh in range(1, MLA_HEADS):
            tot = tot + a[:, hh * LANES:(hh + 1) * LANES]
        return [tot], []

    (dk_sum,) = _rowwise("rope_k_head_sum", f_head_sum, [dk3.reshape(s, MLA_HEADS * LANES)], [], [(LANES, F32)])
    dkr1 = jnp.pad(dk_sum[:, MLA_NOPE:MLA_NOPE + HALF_ROPE], ((0, 0), (0, LANES - HALF_ROPE)))
    dkr2 = jnp.pad(dk_sum[:, MLA_NOPE + HALF_ROPE:MLA_NOPE + MLA_ROPE], ((0, 0), (0, LANES - HALF_ROPE)))
    d_kv = jnp.concatenate([dk3[:, :, :MLA_NOPE].reshape(s, nope_w).astype(BF16), _merge(dv_mla).astype(BF16)], axis=1)
    g_ukv = _mm_tn(ckvn, d_kv, name="grad_ukv")
    grads["w_uk"], grads["w_uv"] = g_ukv[:, :nope_w], g_ukv[:, nope_w:]
    d_ckvn = _mm(d_kv, w_ukv, out_dtype=F32, name="proj_ukv_bwd", transpose_b=True)

    def f_mla_prep_bwd(ti, pa):
        cq, ckv, dcqn, dckvn, g1, g2_, ck, sk = ti
        dcq, dg_q = _rms_bwd(cq, pa[0], dcqn)
        dckv, dg_kv = _rms_bwd(ckv, pa[1], dckvn)
        return [dcq, dckv, g1 * ck + g2_ * sk, g2_ * ck - g1 * sk], [dg_q, dg_kv]

    d_cq, d_ckv, d_kr1, d_kr2, grads["q_a_norm"], grads["kv_a_norm"] = _rowwise(
        "mla_prep_bwd", f_mla_prep_bwd,
        [zl.win(z, "cq"), zl.win(z, "ckv"), d_cqn, d_ckvn, dkr1, dkr2, cos_k, sin_k],
        [wts["q_a_norm"], wts["kv_a_norm"]],
        [(MLA_Q_LORA, BF16), (MLA_KV_LORA, BF16), (LANES, BF16), (LANES, BF16)], [MLA_Q_LORA, MLA_KV_LORA])

    d_z = jnp.concatenate([d_ga, d_gb, _merge(dq_fox).astype(BF16), _merge(dk_fox).astype(BF16),
                           _merge(dv_fox).astype(BF16), d_cq, d_ckv, d_kr1, d_kr2, d_fl], axis=1)
    assert d_z.shape[1] == zl.width

    def f_bias(ti, pa):
        return [], [jnp.sum(ti[0].astype(F32), axis=0, keepdims=True)]

    (g_b_in,) = _rowwise("grad_b_in", f_bias, [d_z], [], [], [zl.width])
    grads["b_in"] = zl.from_kernel(g_b_in)
    grads["w_in"] = zl.from_kernel(_mm_tn(h, d_z, name="grad_in"))
    d_h = _mm(d_z, w_in, out_dtype=F32, name="proj_in_bwd", transpose_b=True)

    def f_norm_in_bwd(ti, pa):
        dx1v, dh, xa = ti
        dxn, dg = _rms_bwd(xa, pa[0], dh)
        return [dx1v + dxn], [dg]

    grad_x, grads["ln_pre_mix"] = _rowwise("norm_in_bwd", f_norm_in_bwd, [d_x1, d_h, x], [wts["ln_pre_mix"]],
                                           [(d, F32)], [d])
    return loss, grad_x, grads


def _round_up(v, mult):
    return -(-v // mult) * mult


class _PackLayout:
    def __init__(self, shapes):
        self.shapes = list(shapes)
        self.width = _round_up(max(b for _, b in shapes), LANES)
        self.bands = []
        row = 0
        shelf = []
        for idx, (a, b) in enumerate(shapes):
            if 2 * _round_up(b, LANES) > self.width:
                self.bands.append((row, _round_up(a, 32), [(idx, 0)]))
                row += _round_up(a, 32)
            else:
                shelf.append(idx)
        col, items = 0, []
        for idx in shelf:
            wb = _round_up(shapes[idx][1], LANES)
            if col + wb > self.width:
                hgt = max(_round_up(shapes[i][0], 32) for i, _ in items)
                self.bands.append((row, hgt, items))
                row += hgt
                col, items = 0, []
            items.append((idx, col))
            col += wb
        if items:
            hgt = max(_round_up(shapes[i][0], 32) for i, _ in items)
            self.bands.append((row, hgt, items))
            row += hgt
        self.rows = _round_up(row, 64)

    def pack(self, arrs):
        bands = []
        for _, hgt, items in self.bands:
            parts = []
            for k, (idx, col) in enumerate(items):
                a, b = self.shapes[idx]
                nxt = items[k + 1][1] if k + 1 < len(items) else self.width
                parts.append(jnp.pad(arrs[idx], ((0, hgt - a), (0, nxt - col - b))))
            bands.append(parts[0] if len(parts) == 1 else jnp.concatenate(parts, axis=1))
        used = sum(hgt for _, hgt, _ in self.bands)
        if used < self.rows:
            bands.append(jnp.zeros((self.rows - used, self.width), arrs[0].dtype))
        return jnp.concatenate(bands, axis=0)

    def unpack(self, packed):
        out = [None] * len(self.shapes)
        for row, _, items in self.bands:
            for idx, col in items:
                a, b = self.shapes[idx]
                out[idx] = packed[row:row + a, col:col + b]
        return out


ANY = pl.BlockSpec(memory_space=pl.ANY)


def _place():
    return lax.axis_index("x"), lax.axis_index("y"), lax.axis_index("c")


def _gather_weights(wpk):
    rows, wd = wpk.shape
    half = rows // 2

    def body(w_ref, out_ref, send_sems, recv_sems, local_sem):
        x, y, c = _place()
        sibling = (x, y, 1 - c)
        chips = [(1 - x, y), (x, 1 - y), (1 - x, 1 - y)]

        def slab(chip, hf):
            return out_ref.at[2 * chip[0] + chip[1], pl.ds(hf * half, half), :]

        def copy(k, chip, hf, to, src=None):
            return pltpu.make_async_remote_copy(
                src_ref=slab(chip, hf) if src is None else src, dst_ref=slab(chip, hf),
                send_sem=send_sems.at[k], recv_sem=recv_sems.at[k], device_id=to, device_id_type=MESH)

        mine = pltpu.make_async_copy(w_ref, out_ref.at[2 * x + y], local_sem)
        mine.start()
        first = [copy(j, (x, y), c, (*chip, c), src=w_ref.at[pl.ds(c * half, half), :]) for j, chip in enumerate(chips)]
        for cp in first:
            cp.start()
        passed = [copy(3 + j, chip, c, sibling) for j, chip in enumerate(chips)]
        for j, chip in enumerate(chips):
            copy(j, chip, c, (x, y, c)).wait_recv()
            passed[j].start()
        for j, chip in enumerate(chips):
            copy(3 + j, chip, 1 - c, (x, y, c)).wait_recv()
        for cp in first + passed:
            cp.wait_send()
        mine.wait()

    return pl.pallas_call(
        body, out_shape=jax.ShapeDtypeStruct((N_CHIPS, rows, wd), wpk.dtype),
        in_specs=[ANY], out_specs=ANY,
        scratch_shapes=[pltpu.SemaphoreType.DMA((6,)), pltpu.SemaphoreType.DMA((6,)), pltpu.SemaphoreType.DMA],
        name="gather_weights")(wpk)


def _sibling_exchange(g):
    n, rows, wd = g.shape
    half = rows // 2

    def body(g_ref, out_ref, send_sem, recv_sem):
        x, y, c = _place()
        cp = pltpu.make_async_remote_copy(
            src_ref=g_ref.at[:, pl.ds((1 - c) * half, half), :], dst_ref=out_ref,
            send_sem=send_sem, recv_sem=recv_sem, device_id=(x, y, 1 - c), device_id_type=MESH)
        cp.start()
        cp.wait()

    return pl.pallas_call(
        body, out_shape=jax.ShapeDtypeStruct((n, half, wd), g.dtype), in_specs=[ANY], out_specs=ANY,
        scratch_shapes=[pltpu.SemaphoreType.DMA, pltpu.SemaphoreType.DMA], name="grad_sibling_exchange")(g)


def _chip_exchange(part):
    n, half, wd = part.shape

    def body(p_ref, out_ref, send_sems, recv_sems):
        x, y, c = _place()
        chips = [(1 - x, y), (x, 1 - y), (1 - x, 1 - y)]

        def copy(j, chip):
            return pltpu.make_async_remote_copy(
                src_ref=p_ref.at[2 * chip[0] + chip[1]], dst_ref=out_ref.at[j],
                send_sem=send_sems.at[j], recv_sem=recv_sems.at[j], device_id=(*chip, c), device_id_type=MESH)

        cps = [copy(j, chip) for j, chip in enumerate(chips)]
        for cp in cps:
            cp.start()
        for cp in cps:
            cp.wait()

    return pl.pallas_call(
        body, out_shape=jax.ShapeDtypeStruct((3, half, wd), part.dtype), in_specs=[ANY], out_specs=ANY,
        scratch_shapes=[pltpu.SemaphoreType.DMA((3,)), pltpu.SemaphoreType.DMA((3,))], name="grad_chip_exchange")(part)


def _sibling_concat(mine):
    half, wd = mine.shape

    def body(m_ref, out_ref, send_sem, recv_sem, local_sem):
        x, y, c = _place()
        loc = pltpu.make_async_copy(m_ref, out_ref.at[c], local_sem)
        loc.start()
        cp = pltpu.make_async_remote_copy(
            src_ref=m_ref, dst_ref=out_ref.at[c], send_sem=send_sem, recv_sem=recv_sem,
            device_id=(x, y, 1 - c), device_id_type=MESH)
        cp.start()
        pltpu.make_async_remote_copy(
            src_ref=m_ref, dst_ref=out_ref.at[1 - c], send_sem=send_sem, recv_sem=recv_sem,
            device_id=(x, y, 1 - c), device_id_type=MESH).wait_recv()
        cp.wait_send()
        loc.wait()

    return pl.pallas_call(
        body, out_shape=jax.ShapeDtypeStruct((2, half, wd), mine.dtype), in_specs=[ANY], out_specs=ANY,
        scratch_shapes=[pltpu.SemaphoreType.DMA, pltpu.SemaphoreType.DMA, pltpu.SemaphoreType.DMA],
        name="grad_sibling_concat")(mine)


def _adamw(w, g, m, v):
    m = ADAM_B1 * m + (1.0 - ADAM_B1) * g
    v = ADAM_B2 * v + (1.0 - ADAM_B2) * (g * g)
    m_hat = m / (1.0 - ADAM_B1 ** ADAM_STEP)
    v_hat = v / (1.0 - ADAM_B2 ** ADAM_STEP)
    delta = -ADAM_LR * (m_hat / (jnp.sqrt(v_hat) + ADAM_EPS) + ADAM_WD * w)
    return delta, m, v


def _small_allreduce_adamw(gs, ws, ms, vs):
    n_dev = 8
    n_par = len(gs)
    wd = PACK_W
    chunks = []
    for p, g in enumerate(gs):
        for off in range(0, g.shape[1], wd):
            chunks.append((p, len(chunks), off, min(wd, g.shape[1] - off)))
    rows = _round_up(len(chunks), 8)

    def body(*refs):
        g_refs, w_refs, m_refs, v_refs = (refs[k * n_par:(k + 1) * n_par] for k in range(4))
        go_refs, d_refs, mo_refs, vo_refs = (refs[(4 + k) * n_par:(5 + k) * n_par] for k in range(4))
        mine_ref, all_ref, send_sems, recv_sems = refs[8 * n_par:]
        x, y, c = _place()
        me, sibling = (x, y, c), (x, y, 1 - c)
        chips = [(1 - x, y), (x, 1 - y), (1 - x, 1 - y)]

        def slot(px, py, pc):
            return all_ref.at[4 * px + 2 * py + pc]

        def copy(k, block, to, src=None):
            return pltpu.make_async_remote_copy(
                src_ref=slot(*block) if src is None else src, dst_ref=slot(*block),
                send_sem=send_sems.at[k], recv_sem=recv_sems.at[k], device_id=to, device_id_type=MESH)

        mine_ref[...] = jnp.zeros_like(mine_ref)
        for p, row, off, width in chunks:
            mine_ref[row:row + 1, 0:width] = g_refs[p][:, off:off + width]
        all_ref[4 * x + 2 * y + c] = mine_ref[...]
        first = [copy(0, me, sibling, src=mine_ref)]
        first += [copy(1 + j, me, (*chip, c), src=mine_ref) for j, chip in enumerate(chips)]
        for cp in first:
            cp.start()
        passed = [copy(4 + j, (*chip, c), sibling) for j, chip in enumerate(chips)]
        for j, chip in enumerate(chips):
            copy(1 + j, (*chip, c), me).wait_recv()
            passed[j].start()
        copy(0, sibling, me).wait_recv()
        for j, chip in enumerate(chips):
            copy(4 + j, (*chip, 1 - c), me).wait_recv()
        for cp in first + passed:
            cp.wait_send()
        tot = jnp.zeros((rows, wd), F32)
        for dev in range(n_dev):
            tot = tot + all_ref[dev]
        mine_ref[...] = tot
        for p, row, off, width in chunks:
            cols = slice(off, off + width)
            g = mine_ref[row:row + 1, 0:width]
            delta, m_new, v_new = _adamw(w_refs[p][:, cols], g, m_refs[p][:, cols], v_refs[p][:, cols])
            go_refs[p][:, cols] = g
            d_refs[p][:, cols] = delta
            mo_refs[p][:, cols] = m_new
            vo_refs[p][:, cols] = v_new

    vm = pl.BlockSpec(memory_space=pltpu.VMEM)
    shp = [jax.ShapeDtypeStruct(g.shape, F32) for g in gs]
    res = pl.pallas_call(
        body, out_shape=shp * 4, in_specs=[vm] * (4 * n_par), out_specs=[vm] * (4 * n_par),
        scratch_shapes=[pltpu.VMEM((rows, wd), F32), pltpu.VMEM((n_dev, rows, wd), F32),
                        pltpu.SemaphoreType.DMA((7,)), pltpu.SemaphoreType.DMA((7,))],
        name="small_allreduce_adamw")(*gs, *ws, *ms, *vs)
    return [res[k * n_par:(k + 1) * n_par] for k in range(4)]


def kernel(x, positions, ln_pre_mix, ln_post_mix, ln_pre_mlp, ln_post_mlp, w_in, b_in, q_a_norm, w_uq, kv_a_norm, w_uk, w_uv, w_o_mla, w_o_fox, w_out, w_ff1, w_ff2, loss_target, m_ln_pre_mix, m_ln_post_mix, m_ln_pre_mlp, m_ln_post_mlp, m_w_in, m_b_in, m_q_a_norm, m_w_uq, m_kv_a_norm, m_w_uk, m_w_uv, m_w_o_mla, m_w_o_fox, m_w_out, m_w_ff1, m_w_ff2, v_ln_pre_mix, v_ln_post_mix, v_ln_pre_mlp, v_ln_post_mlp, v_w_in, v_b_in, v_q_a_norm, v_w_uq, v_kv_a_norm, v_w_uk, v_w_uv, v_w_o_mla, v_w_o_fox, v_w_out, v_w_ff1, v_w_ff2):
    w = dict(ln_pre_mix=ln_pre_mix, ln_post_mix=ln_post_mix, ln_pre_mlp=ln_pre_mlp, ln_post_mlp=ln_post_mlp, w_in=w_in,
             b_in=b_in, q_a_norm=q_a_norm, w_uq=w_uq, kv_a_norm=kv_a_norm, w_uk=w_uk, w_uv=w_uv, w_o_mla=w_o_mla,
             w_o_fox=w_o_fox, w_out=w_out, w_ff1=w_ff1, w_ff2=w_ff2)
    mom = dict(ln_pre_mix=m_ln_pre_mix, ln_post_mix=m_ln_post_mix, ln_pre_mlp=m_ln_pre_mlp, ln_post_mlp=m_ln_post_mlp,
               w_in=m_w_in, b_in=m_b_in, q_a_norm=m_q_a_norm, w_uq=m_w_uq, kv_a_norm=m_kv_a_norm, w_uk=m_w_uk,
               w_uv=m_w_uv, w_o_mla=m_w_o_mla, w_o_fox=m_w_o_fox, w_out=m_w_out, w_ff1=m_w_ff1, w_ff2=m_w_ff2)
    var = dict(ln_pre_mix=v_ln_pre_mix, ln_post_mix=v_ln_post_mix, ln_pre_mlp=v_ln_pre_mlp, ln_post_mlp=v_ln_post_mlp,
               w_in=v_w_in, b_in=v_b_in, q_a_norm=v_q_a_norm, w_uq=v_w_uq, kv_a_norm=v_kv_a_norm, w_uk=v_w_uk,
               w_uv=v_w_uv, w_o_mla=v_w_o_mla, w_o_fox=v_w_o_fox, w_out=v_w_out, w_ff1=v_w_ff1, w_ff2=v_w_ff2)

    big_names = [nm for nm, _ in BIG]
    layout = _PackLayout([w[nm].shape[1:] for nm in big_names])
    rows, pack_w = layout.rows, layout.width
    half = rows // 2

    gathered = _gather_weights(layout.pack([w[nm][0].astype(BF16) for nm in big_names]))
    per_chip = [layout.unpack(gathered[ch]) for ch in range(N_CHIPS)]
    full = {nm: wv for nm, wv in w.items() if nm in SMALL}
    for k, (nm, axis) in enumerate(BIG):
        full[nm] = jnp.concatenate([per_chip[ch][k] for ch in range(N_CHIPS)], axis=axis - 1)

    loss_local, grad_x, grads = _local_step(x[0], positions[0], loss_target[0], full)
    loss = lax.psum(loss_local, ("x", "y", "c"))

    gpk = jnp.stack([
        layout.pack([jnp.split(grads[nm], N_CHIPS, axis=axis - 1)[ch] for nm, axis in BIG]) for ch in range(N_CHIPS)])
    c = lax.axis_index("c")
    chip = 2 * lax.axis_index("x") + lax.axis_index("y")
    theirs = _sibling_exchange(gpk)
    mine_half = lax.dynamic_slice_in_dim(gpk, c * half, half, axis=1)

    def f_add2(ti, pa):
        tot = ti[0] + ti[1]
        return [tot, tot], []

    chip_part, chip_part_bf = _rowwise("grad_add_sibling", f_add2, [mine_half.reshape(N_CHIPS * half, pack_w),
                                                                  theirs.reshape(N_CHIPS * half, pack_w)], [],
                                       [(pack_w, F32), (pack_w, BF16)])
    others = _chip_exchange(chip_part_bf.reshape(N_CHIPS, half, pack_w))
    own = lax.dynamic_index_in_dim(chip_part.reshape(N_CHIPS, half, pack_w), chip, axis=0, keepdims=False)

    def f_add4(ti, pa):
        return [((ti[0] + ti[1].astype(F32)) + ti[2].astype(F32)) + ti[3].astype(F32)], []

    (red_half,) = _rowwise("grad_add_chips", f_add4, [own, others[0], others[1], others[2]], [], [(pack_w, F32)])
    g_shards = layout.unpack(_sibling_concat(red_half).reshape(rows, pack_w))

    def f_adamw(ti, pa):
        wv, gv, mv, vv = ti
        return list(_adamw(wv, gv, mv, vv)), []

    out = {"grad": {}, "delta": {}, "m": {}, "v": {}}
    for nm, g_sh in zip(big_names, g_shards, strict=True):
        wd = g_sh.shape[1]
        d_sh, m_sh, v_sh = _rowwise("adamw_" + nm, f_adamw, [w[nm][0], g_sh, mom[nm][0], var[nm][0]], [], [(wd, F32)] * 3)
        out["grad"][nm], out["delta"][nm], out["m"][nm], out["v"][nm] = g_sh[None], d_sh[None], m_sh[None], v_sh[None]

    small = _small_allreduce_adamw([grads[nm] for nm in SMALL], [w[nm] for nm in SMALL],
                                   [mom[nm] for nm in SMALL], [var[nm] for nm in SMALL])
    for kind, arrs in zip(("grad", "delta", "m", "v"), small, strict=True):
        for nm, arr in zip(SMALL, arrs, strict=True):
            out[kind][nm] = arr

    return (loss, grad_x[None], *[out["grad"][nm] for nm in ALL_W], *[out["delta"][nm] for nm in ALL_W],
            *[out["m"][nm] for nm in ALL_W], *[out["v"][nm] for nm in ALL_W])
```

```python
import functools
import math

import jax
import jax.numpy as jnp
from jax import lax
from jax.experimental import pallas as pl
from jax.experimental.pallas import tpu as pltpu

F32 = jnp.float32
BF16 = jnp.bfloat16

MLA_HEADS = 8
MLA_Q_LORA = 256
MLA_KV_LORA = 128
MLA_NOPE = 64
MLA_ROPE = 32
MLA_V = 64
FOX_HEADS = 8
FOX_DIM = 64
ROPE_THETA = 10000.0
NORM_EPS = 1e-6
HALF_ROPE = MLA_ROPE // 2

ADAM_LR = 0.001
ADAM_B1 = 0.9
ADAM_B2 = 0.999
ADAM_EPS = 1e-08
ADAM_WD = 0.01
ADAM_STEP = 10

LANES = 128
VMEM_LIMIT = 56 * 1024 * 1024
ATT_TILE = 512
MM_LHS_BLOCK_BYTES = 9 * 1024 * 1024
NEG = -1e30
LOG2E = math.log2(math.e)
MESH = pl.DeviceIdType.MESH

V_ONES = 64
FOX_Q_F = 64
FOX_Q_L = 67
FOX_Q_ONES = 70
MLA_Q_L = 96

BIG = (("w_in", 2), ("w_uq", 2), ("w_uk", 2), ("w_uv", 2), ("w_o_mla", 2), ("w_o_fox", 2),
       ("w_out", 1), ("w_ff1", 2), ("w_ff2", 1))
SMALL = ("ln_pre_mix", "ln_post_mix", "ln_pre_mlp", "ln_post_mlp", "b_in", "q_a_norm", "kv_a_norm")
ALL_W = ("ln_pre_mix", "ln_post_mix", "ln_pre_mlp", "ln_post_mlp", "w_in", "b_in", "q_a_norm", "w_uq",
         "kv_a_norm", "w_uk", "w_uv", "w_o_mla", "w_o_fox", "w_out", "w_ff1", "w_ff2")
N_CHIPS = 4
PACK_W = 1024

_NT = (((1,), (1,)), ((), ()))
_TN = (((0,), (0,)), ((), ()))


def _cparams(sem=None):
    return pltpu.CompilerParams(dimension_semantics=sem, vmem_limit_bytes=VMEM_LIMIT)


def _divisor_tile(n, limit, mult):
    if n <= limit:
        return n
    best = None
    t = mult
    while t <= limit:
        if n % t == 0:
            best = t
        t += mult
    assert best is not None, (n, limit, mult)
    return best


def _round_up(v, mult):
    return -(-v // mult) * mult


def _mm(a, b, *, out_dtype, name, bias=None, transpose_b=False, extras=(), epilogue=None):
    m, k = a.shape
    n = b.shape[0] if transpose_b else b.shape[1]
    assert (b.shape[1] if transpose_b else b.shape[0]) == k and a.dtype == BF16 and b.dtype == BF16
    out_dtypes = list(out_dtype) if isinstance(out_dtype, (list, tuple)) else [out_dtype]
    tm = _divisor_tile(m, min(2048, max(512, MM_LHS_BLOCK_BYTES // (2 * k))), 16)
    tn = _divisor_tile(n, 512, LANES)
    n_ex = len(extras)

    def body(*refs):
        a_ref, b_ref = refs[:2]
        pos = 2
        bias_ref = None
        if bias is not None:
            bias_ref = refs[pos]
            pos += 1
        ex_refs = refs[pos:pos + n_ex]
        o_refs = refs[pos + n_ex:]
        if transpose_b:
            acc = lax.dot_general(a_ref[...], b_ref[...], _NT, preferred_element_type=F32)
        else:
            acc = jnp.dot(a_ref[...], b_ref[...], preferred_element_type=F32)
        if bias_ref is not None:
            acc = acc + bias_ref[...]
        vals = [acc] if epilogue is None else epilogue(acc, [r[...] for r in ex_refs])
        for ref, val in zip(o_refs, vals, strict=True):
            ref[...] = val.astype(ref.dtype)

    b_spec = pl.BlockSpec((tn, k), lambda i, j: (j, 0)) if transpose_b else pl.BlockSpec((k, tn), lambda i, j: (0, j))
    in_specs = [pl.BlockSpec((tm, k), lambda i, j: (i, 0)), b_spec]
    args = [a, b]
    if bias is not None:
        in_specs.append(pl.BlockSpec((1, tn), lambda i, j: (0, j)))
        args.append(bias)
    in_specs += [pl.BlockSpec((tm, tn), lambda i, j: (i, j)) for _ in extras]
    args += list(extras)
    res = pl.pallas_call(
        body, grid=(m // tm, n // tn), in_specs=in_specs,
        out_specs=[pl.BlockSpec((tm, tn), lambda i, j: (i, j)) for _ in out_dtypes],
        out_shape=[jax.ShapeDtypeStruct((m, n), dt) for dt in out_dtypes],
        compiler_params=_cparams(("parallel", "parallel")), name=name)(*args)
    return res if isinstance(out_dtype, (list, tuple)) else res[0]


def _mm_tn(a, b, *, name):
    s, m = a.shape
    s2, n = b.shape
    assert s == s2 and a.dtype == BF16 and b.dtype == BF16
    tm = _divisor_tile(m, 1024, LANES)
    tn = _divisor_tile(n, 2304, LANES)
    tk = _divisor_tile(s, 512, 16)

    def body(a_ref, b_ref, o_ref):
        @pl.when(pl.program_id(2) == 0)
        def _():
            o_ref[...] = jnp.zeros_like(o_ref)

        o_ref[...] += lax.dot_general(a_ref[...], b_ref[...], _TN, preferred_element_type=F32)

    return pl.pallas_call(
        body, grid=(m // tm, n // tn, s // tk),
        in_specs=[pl.BlockSpec((tk, tm), lambda i, j, k: (k, i)), pl.BlockSpec((tk, tn), lambda i, j, k: (k, j))],
        out_specs=pl.BlockSpec((tm, tn), lambda i, j, k: (i, j)),
        out_shape=jax.ShapeDtypeStruct((m, n), F32),
        compiler_params=_cparams(("parallel", "parallel", "arbitrary")), name=name)(a, b)


def _rowwise(name, fn, tiled, params, outs, reds=(), reverse=False):
    wins = [t if isinstance(t, tuple) else (t, 0, t.shape[1]) for t in tiled]
    s = wins[0][0].shape[0]
    row_bytes = sum(w * arr.dtype.itemsize for arr, _, w in wins) + sum(w * jnp.dtype(d).itemsize for w, d in outs)
    ts = _divisor_tile(s, max(16, min(1024, (6 * 1024 * 1024) // row_bytes)), 16)
    nt, npar, nout = len(wins), len(params), len(outs)
    n_tiles = s // ts

    def row(i):
        return n_tiles - 1 - i if reverse else i

    def body(*refs):
        tin = [r[...] for r in refs[:nt]]
        par = [r[...] for r in refs[nt:nt + npar]]
        out_refs = refs[nt + npar:nt + npar + nout]
        red_refs = refs[nt + npar + nout:]
        o, r = fn(tin, par)
        for ref, val in zip(out_refs, o, strict=True):
            ref[...] = val.astype(ref.dtype)
        if red_refs:
            @pl.when(pl.program_id(0) == 0)
            def _():
                for ref in red_refs:
                    ref[...] = jnp.zeros_like(ref)

            for ref, val in zip(red_refs, r, strict=True):
                ref[...] += val

    in_specs = [pl.BlockSpec((ts, w), functools.partial(lambda i, cb: (row(i), cb), cb=cb)) for _, cb, w in wins]
    in_specs += [pl.BlockSpec(p.shape, lambda i: (0, 0)) for p in params]
    out_specs = [pl.BlockSpec((ts, w), lambda i: (row(i), 0)) for w, _ in outs]
    out_specs += [pl.BlockSpec((1, w), lambda i: (0, 0)) for w in reds]
    out_shape = [jax.ShapeDtypeStruct((s, w), d) for w, d in outs]
    out_shape += [jax.ShapeDtypeStruct((1, w), F32) for w in reds]
    return pl.pallas_call(
        body, grid=(n_tiles,), in_specs=in_specs, out_specs=out_specs, out_shape=out_shape,
        compiler_params=_cparams(("arbitrary",)), name=name)(*[w[0] for w in wins], *params)


def _rms(x, g):
    r = lax.rsqrt(jnp.mean(x * x, axis=-1, keepdims=True) + NORM_EPS)
    return x * r * g, r


def _rms_bwd(x, g, dy):
    r = lax.rsqrt(jnp.mean(x * x, axis=-1, keepdims=True) + NORM_EPS)
    gy = dy * g
    dx = r * gy - x * (r * r * r) * jnp.mean(x * gy, axis=-1, keepdims=True)
    dg = jnp.sum(dy * (x * r), axis=0, keepdims=True)
    return dx, dg


def _sigmoid(x):
    return 1.0 / (1.0 + jnp.exp(-x))


def _split3(x):
    hi = x.astype(BF16).astype(F32)
    r = x - hi
    mid = r.astype(BF16).astype(F32)
    lo = (r - mid).astype(BF16).astype(F32)
    return hi, mid, lo


def _lane(shape):
    return lax.broadcasted_iota(jnp.int32, shape, 1)


def _put3(blk, lane, pos, pieces):
    for k, piece in enumerate(pieces):
        blk = jnp.where(lane == pos + k, piece, blk)
    return blk


def _lane_column(blk, lane, pos):
    return jnp.sum(jnp.where(lane == pos, blk, 0.0), axis=1, keepdims=True)


def _blocks(a, nh):
    return [a[:, h * LANES:(h + 1) * LANES] for h in range(nh)]


def _rope_block(x, c, sa, sb):
    return x * c + pltpu.roll(x, LANES - HALF_ROPE, 1) * sa + pltpu.roll(x, HALF_ROPE, 1) * sb


def _forget_cumsum(z, cb):
    s = z.shape[0]
    ts = _divisor_tile(s, 512, LANES)

    def body(x_ref, col_ref, carry):
        @pl.when(pl.program_id(0) == 0)
        def _():
            carry[...] = jnp.zeros_like(carry)

        x = x_ref[...]
        lf = jnp.minimum(x, 0.0) - jnp.log1p(jnp.exp(-jnp.abs(x)))
        r = lax.broadcasted_iota(jnp.int32, (ts, ts), 0)
        c = lax.broadcasted_iota(jnp.int32, (ts, ts), 1)
        tri = jnp.where(c <= r, 1.0, 0.0).astype(F32)
        col_ref[...] = jnp.dot(tri, lf, preferred_element_type=F32, precision=lax.Precision.HIGHEST) + carry[...]
        carry[...] += jnp.sum(lf, axis=0, keepdims=True)

    return pl.pallas_call(
        body, grid=(s // ts,),
        in_specs=[pl.BlockSpec((ts, LANES), lambda i: (i, cb))],
        out_specs=pl.BlockSpec((ts, LANES), lambda i: (i, 0)),
        out_shape=jax.ShapeDtypeStruct((s, LANES), F32),
        scratch_shapes=[pltpu.VMEM((1, LANES), F32)],
        compiler_params=_cparams(("arbitrary",)), name="forget_cumsum")(z)


def _forget_cumsum_bwd(dq, dk, z, cb, nh):
    s = z.shape[0]
    ts = _divisor_tile(s, 512, LANES)
    nt = s // ts
    wd = nh * LANES

    def body(dq_ref, dk_ref, x_ref, o_ref, carry):
        @pl.when(pl.program_id(0) == 0)
        def _():
            carry[...] = jnp.zeros_like(carry)

        lane = _lane((ts, LANES))
        df = jnp.zeros((ts, LANES), F32)
        for h in range(nh):
            cols = slice(h * LANES, (h + 1) * LANES)
            d_h = _lane_column(dq_ref[:, cols], lane, FOX_Q_F) - _lane_column(dk_ref[:, cols], lane, FOX_Q_ONES)
            df = jnp.where(lane == h, d_h, df)
        r = lax.broadcasted_iota(jnp.int32, (ts, ts), 0)
        c = lax.broadcasted_iota(jnp.int32, (ts, ts), 1)
        tri = jnp.where(c >= r, 1.0, 0.0).astype(F32)
        rc = jnp.dot(tri, df, preferred_element_type=F32, precision=lax.Precision.HIGHEST) + carry[...]
        carry[...] += jnp.sum(df, axis=0, keepdims=True)
        o_ref[...] = (rc * (1.0 / (1.0 + jnp.exp(x_ref[...])))).astype(o_ref.dtype)

    return pl.pallas_call(
        body, grid=(nt,),
        in_specs=[pl.BlockSpec((ts, wd), lambda i: (nt - 1 - i, 0)),
                  pl.BlockSpec((ts, wd), lambda i: (nt - 1 - i, 0)),
                  pl.BlockSpec((ts, LANES), lambda i: (nt - 1 - i, cb))],
        out_specs=pl.BlockSpec((ts, LANES), lambda i: (nt - 1 - i, 0)),
        out_shape=jax.ShapeDtypeStruct((s, LANES), BF16),
        scratch_shapes=[pltpu.VMEM((1, LANES), F32)],
        compiler_params=_cparams(("arbitrary",)), name="forget_cumsum_bwd")(dq, dk, z)


def _flash_fwd(q, k, v, scale, nh, l_lane, *, name):
    s = q.shape[0]
    t = min(ATT_TILE, s)
    nq = s // t
    c = scale * LOG2E

    def body(q_ref, k_ref, v_ref, o_ref, qb_ref):
        i = pl.program_id(1)
        qb = q_ref[...]

        def scores(j):
            kb = k_ref[pl.ds(pl.multiple_of(j * t, t), t), :]
            return lax.dot_general(qb, kb, _NT, preferred_element_type=F32)

        def update(j, sc, carry):
            m, acc = carry
            m_new = jnp.maximum(m, jnp.max(sc, axis=1, keepdims=True))
            p = jnp.exp2((sc - m_new) * c)
            alpha = jnp.exp2((m - m_new) * c)
            vb = v_ref[pl.ds(pl.multiple_of(j * t, t), t), :]
            acc = alpha * acc + jnp.dot(p.astype(BF16), vb, preferred_element_type=F32)
            return m_new, acc

        init = (jnp.full((t, 1), NEG, F32), jnp.zeros((t, LANES), F32))
        carry = lax.fori_loop(0, i, lambda j, cr: update(j, scores(j), cr), init)
        row = lax.broadcasted_iota(jnp.int32, (t, t), 0)
        col = lax.broadcasted_iota(jnp.int32, (t, t), 1)
        m, acc = update(i, jnp.where(col <= row, scores(i), NEG), carry)
        lane = _lane((t, LANES))
        l = _lane_column(acc, lane, V_ONES)
        o_ref[...] = (acc / l).astype(o_ref.dtype)
        big_l = m + jnp.log(l) / scale
        qb_ref[...] = _put3(qb.astype(F32), lane, l_lane, _split3(-big_l)).astype(qb_ref.dtype)

    head_rows = pl.BlockSpec((t, LANES), lambda h, i: (i, h))
    head_all = pl.BlockSpec((s, LANES), lambda h, i: (0, h))
    return pl.pallas_call(
        body, grid=(nh, nq), in_specs=[head_rows, head_all, head_all], out_specs=[head_rows, head_rows],
        out_shape=[jax.ShapeDtypeStruct(q.shape, BF16), jax.ShapeDtypeStruct(q.shape, BF16)],
        compiler_params=_cparams(("parallel", "arbitrary")), name=name)(q, k, v)


def _flash_bwd(qb, k, v, do, scale, nh, n_feat, *, name):
    s = qb.shape[0]
    t = min(ATT_TILE, s)
    nq = s // t
    c = scale * LOG2E

    def body(q_ref, k_ref, v_ref, do_ref, dk_ref, dv_ref, dq_ref):
        j = pl.program_id(1)
        kb = k_ref[...]
        vb = v_ref[...]

        @pl.when(j == 0)
        def _():
            dq_ref[...] = jnp.zeros_like(dq_ref)

        def block(i, carry, masked):
            dk_acc, dv_acc = carry
            rows = pl.ds(pl.multiple_of(i * t, t), t)
            qblk = q_ref[rows, :]
            dob = do_ref[rows, :]
            st = lax.dot_general(kb, qblk, _NT, preferred_element_type=F32)
            if masked:
                key = lax.broadcasted_iota(jnp.int32, (t, t), 0)
                qry = lax.broadcasted_iota(jnp.int32, (t, t), 1)
                st = jnp.where(key <= qry, st, NEG)
            pt = jnp.exp2(st * c)
            dv_acc = dv_acc + jnp.dot(pt.astype(BF16), dob, preferred_element_type=F32)
            dpt = lax.dot_general(vb, dob, _NT, preferred_element_type=F32)
            dsb = (pt * dpt).astype(BF16)
            dk_acc = dk_acc + jnp.dot(dsb, qblk, preferred_element_type=F32)
            dq_ref[rows, :] += lax.dot_general(dsb, kb, _TN, preferred_element_type=F32)
            return dk_acc, dv_acc

        init = (jnp.zeros((t, LANES), F32), jnp.zeros((t, LANES), F32))
        carry = block(j, init, True)
        dk_acc, dv_acc = lax.fori_loop(j + 1, nq, lambda i, cr: block(i, cr, False), carry)
        dk_ref[...] = dk_acc * jnp.where(_lane((t, LANES)) < n_feat, scale, 1.0)
        dv_ref[...] = dv_acc

        @pl.when(j == nq - 1)
        def _():
            dq_ref[...] = dq_ref[...] * jnp.where(_lane((s, LANES)) < n_feat, scale, 1.0)

    head_rows = pl.BlockSpec((t, LANES), lambda h, j: (j, h))
    head_all = pl.BlockSpec((s, LANES), lambda h, j: (0, h))
    shp = jax.ShapeDtypeStruct(qb.shape, F32)
    return pl.pallas_call(
        body, grid=(nh, nq), in_specs=[head_all, head_rows, head_rows, head_all],
        out_specs=[head_rows, head_rows, head_all], out_shape=[shp, shp, shp],
        compiler_params=_cparams(("parallel", "arbitrary")), name=name)(qb, k, v, do)


def _with_delta(do, o, nh, *, name):
    def fn(ti, pa):
        lane = _lane((ti[0].shape[0], LANES))
        out = []
        for d_blk, o_blk in zip(_blocks(ti[0], nh), _blocks(ti[1], nh), strict=True):
            delta = jnp.sum(d_blk * o_blk.astype(F32), axis=1, keepdims=True)
            out.append(_put3(d_blk, lane, V_ONES, _split3(-delta)))
        return [jnp.concatenate(out, axis=1)], []

    (res,) = _rowwise(name, fn, [do, o], [], [(do.shape[1], BF16)])
    return res


def _pad_heads(a, nh):
    d = a.shape[-1] // nh
    a = a.reshape(a.shape[:-1] + (nh, d))
    a = jnp.pad(a, [(0, 0)] * (a.ndim - 1) + [(0, LANES - d)])
    return a.reshape(a.shape[:-2] + (nh * LANES,))


def _unpad_heads(a, nh, d):
    a = a.reshape(a.shape[:-1] + (nh, LANES))[..., :d]
    return a.reshape(a.shape[:-2] + (nh * d,))


def _pad_head_rows(w, nh):
    return _pad_heads(w.T, nh).T


def _unpad_head_rows(g, nh, d):
    return _unpad_heads(g.T, nh, d).T


class _ZLayout:
    def __init__(self, d):
        fw = FOX_HEADS * FOX_DIM
        self.src = {}
        off = 0
        for nm, w in (("cq", MLA_Q_LORA), ("ckv", MLA_KV_LORA), ("kr", MLA_ROPE), ("fq", fw), ("fk", fw),
                      ("fv", fw), ("fl", FOX_HEADS), ("ga", d), ("gb", d)):
            self.src[nm] = (off, w)
            off += w
        self.dst = {}
        off = 0
        for nm, w in (("fq", FOX_HEADS * LANES), ("fk", FOX_HEADS * LANES), ("fv", FOX_HEADS * LANES), ("ga", d),
                      ("gb", d), ("cq", MLA_Q_LORA), ("ckv", MLA_KV_LORA), ("kr", LANES), ("fl", LANES)):
            assert off % w == 0
            self.dst[nm] = (off, w)
            off += w
        self.width = off

    def win(self, z, nm):
        off, w = self.dst[nm]
        return (z, off // w, w)

    def to_kernel(self, w):
        def seg(nm):
            off, wd = self.src[nm]
            return w[..., off:off + wd]

        def pad(a, left, total):
            return jnp.pad(a, [(0, 0)] * (a.ndim - 1) + [(left, total - left - a.shape[-1])])

        return jnp.concatenate([_pad_heads(seg("fq"), FOX_HEADS), _pad_heads(seg("fk"), FOX_HEADS),
                                _pad_heads(seg("fv"), FOX_HEADS), seg("ga"), seg("gb"), seg("cq"), seg("ckv"),
                                pad(seg("kr"), MLA_NOPE, LANES), pad(seg("fl"), 0, LANES)], axis=-1)

    def from_kernel(self, g):
        def seg(nm, lo=0, hi=None):
            off, wd = self.dst[nm]
            return g[..., off + lo:off + (wd if hi is None else hi)]

        return jnp.concatenate([seg("cq"), seg("ckv"), seg("kr", MLA_NOPE, MLA_NOPE + MLA_ROPE),
                                _unpad_heads(seg("fq"), FOX_HEADS, FOX_DIM), _unpad_heads(seg("fk"), FOX_HEADS, FOX_DIM),
                                _unpad_heads(seg("fv"), FOX_HEADS, FOX_DIM), seg("fl", 0, FOX_HEADS), seg("ga"),
                                seg("gb")], axis=-1)


def _local_step(x, positions, target, wts):
    s, d = x.shape
    zl = _ZLayout(d)
    hw = MLA_HEADS * LANES
    assert MLA_HEADS == FOX_HEADS
    scale_mla = (MLA_NOPE + MLA_ROPE) ** -0.5
    scale_fox = FOX_DIM ** -0.5

    inv_freq = ROPE_THETA ** (-jnp.arange(HALF_ROPE, dtype=F32) / HALF_ROPE)
    ang = positions.astype(F32)[:, None] * inv_freq
    cos, sin = jnp.cos(ang), jnp.sin(ang)
    tail = jnp.zeros((s, LANES - MLA_NOPE - MLA_ROPE), F32)
    rc = jnp.concatenate([jnp.ones((s, MLA_NOPE), F32), cos, cos, tail], axis=1)
    ra = jnp.concatenate([jnp.zeros((s, MLA_NOPE), F32), -sin, jnp.zeros((s, HALF_ROPE), F32), tail], axis=1)
    rb = jnp.concatenate([jnp.zeros((s, MLA_NOPE + HALF_ROPE), F32), sin, tail], axis=1)

    w_in = zl.to_kernel(wts["w_in"])
    b_in = zl.to_kernel(wts["b_in"])
    w_uq = _pad_heads(wts["w_uq"], MLA_HEADS)
    w_ukv = jnp.concatenate([_pad_heads(wts["w_uk"], MLA_HEADS), _pad_heads(wts["w_uv"], MLA_HEADS)], axis=1)
    w_o_mla = _pad_head_rows(wts["w_o_mla"], MLA_HEADS)
    w_o_fox = _pad_head_rows(wts["w_o_fox"], FOX_HEADS)

    def f_norm_in(ti, pa):
        y, _ = _rms(ti[0], pa[0])
        return [y], []

    (h,) = _rowwise("norm_in", f_norm_in, [x], [wts["ln_pre_mix"]], [(d, BF16)])
    z = _mm(h, w_in, bias=b_in, out_dtype=F32, name="proj_in")

    def f_mla_norms(ti, pa):
        cqn, _ = _rms(ti[0], pa[0])
        ckvn, _ = _rms(ti[1], pa[1])
        return [cqn, ckvn], []

    cqn, ckvn = _rowwise("mla_norms", f_mla_norms, [zl.win(z, "cq"), zl.win(z, "ckv")],
                         [wts["q_a_norm"], wts["kv_a_norm"]], [(MLA_Q_LORA, BF16), (MLA_KV_LORA, BF16)])
    qf = _mm(cqn, w_uq, out_dtype=F32, name="proj_uq")
    kv = _mm(ckvn, w_ukv, out_dtype=BF16, name="proj_ukv")

    def f_rope_q(ti, pa):
        xq, c_, a_, b_ = ti
        return [jnp.concatenate([_rope_block(blk, c_, a_, b_) for blk in _blocks(xq, MLA_HEADS)], axis=1)], []

    (q_mla,) = _rowwise("rope_q", f_rope_q, [qf, rc, ra, rb], [], [(hw, BF16)])

    def f_mla_kv(ti, pa):
        kn, vn, kr, c_, a_, b_ = ti
        lane = _lane(kr.shape)
        k_tail = jnp.where((lane >= MLA_Q_L) & (lane < MLA_Q_L + 3), 1.0, _rope_block(kr, c_, a_, b_))
        ones_v = (lane >= V_ONES) & (lane < V_ONES + 3)
        k_out = [jnp.where(lane < MLA_NOPE, blk.astype(F32), k_tail) for blk in _blocks(kn, MLA_HEADS)]
        v_out = [jnp.where(ones_v, 1.0, blk.astype(F32)) for blk in _blocks(vn, MLA_HEADS)]
        return [jnp.concatenate(k_out, axis=1), jnp.concatenate(v_out, axis=1)], []

    k_mla, v_mla = _rowwise("mla_kv", f_mla_kv, [(kv, 0, hw), (kv, 1, hw), zl.win(z, "kr"), rc, ra, rb], [],
                            [(hw, BF16), (hw, BF16)])
    o_mla, qb_mla = _flash_fwd(q_mla, k_mla, v_mla, scale_mla, MLA_HEADS, MLA_Q_L, name="mla_fwd")

    fl_cb = zl.dst["fl"][0] // LANES
    fcol = _forget_cumsum(z, fl_cb)

    def f_fox_qkv(ti, pa):
        fq, fk, fv, fc = ti
        lane = _lane(fc.shape)
        ones_q = (lane >= FOX_Q_ONES) & (lane < FOX_Q_ONES + 3)
        ones_k = (lane >= FOX_Q_F) & (lane < FOX_Q_ONES)
        ones_v = (lane >= V_ONES) & (lane < V_ONES + 3)
        q_out, k_out, v_out = [], [], []
        for hh, (qblk, kblk, vblk) in enumerate(zip(_blocks(fq, FOX_HEADS), _blocks(fk, FOX_HEADS),
                                                    _blocks(fv, FOX_HEADS), strict=True)):
            f_h = _lane_column(fc, lane, hh) * (1.0 / scale_fox)
            q_out.append(_put3(jnp.where(ones_q, 1.0, qblk), lane, FOX_Q_F, _split3(f_h)))
            k_out.append(_put3(jnp.where(ones_k, 1.0, kblk), lane, FOX_Q_ONES, _split3(-f_h)))
            v_out.append(jnp.where(ones_v, 1.0, vblk))
        return [jnp.concatenate(q_out, axis=1), jnp.concatenate(k_out, axis=1), jnp.concatenate(v_out, axis=1)], []

    q_fox, k_fox, v_fox = _rowwise("fox_qkv", f_fox_qkv, [zl.win(z, "fq"), zl.win(z, "fk"), zl.win(z, "fv"), fcol],
                                   [], [(hw, BF16)] * 3)
    o_fox, qb_fox = _flash_fwd(q_fox, k_fox, v_fox, scale_fox, FOX_HEADS, FOX_Q_L, name="fox_fwd")

    y_mla = _mm(o_mla, w_o_mla, out_dtype=F32, name="proj_o_mla")
    y_fox = _mm(o_fox, w_o_fox, out_dtype=F32, name="proj_o_fox")

    def f_gate(ti, pa):
        ga, gb, ya, yb = ti
        return [_sigmoid(ga) * ya + _sigmoid(gb) * yb], []

    (merged,) = _rowwise("gate", f_gate, [zl.win(z, "ga"), zl.win(z, "gb"), y_mla, y_fox], [], [(d, BF16)])
    mix = _mm(merged, wts["w_out"], out_dtype=F32, name="proj_out")

    def f_resid1(ti, pa):
        xa, mx = ti
        y, _ = _rms(mx, pa[0])
        x1 = xa + y
        h2, _ = _rms(x1, pa[1])
        return [x1, h2], []

    x1, h2 = _rowwise("resid_mix", f_resid1, [x, mix], [wts["ln_post_mix"], wts["ln_pre_mlp"]], [(d, F32), (d, BF16)])

    def relu2(acc, ex):
        r = jnp.maximum(acc, 0.0)
        return [acc, r * r]

    u, act = _mm(h2, wts["w_ff1"], out_dtype=[F32, BF16], name="ff1", epilogue=relu2)
    mo = _mm(act, wts["w_ff2"], out_dtype=F32, name="ff2")

    def f_loss(ti, pa):
        xa, mv, tg = ti
        y, _ = _rms(mv, pa[0])
        err = (xa + y) - tg
        g2 = err / d
        dmo, dg = _rms_bwd(mv, pa[0], g2)
        return [g2, dmo], [jnp.sum(err * err, axis=0, keepdims=True), dg]

    g2, d_mo, loss_cols, g_ln_post_mlp = _rowwise("loss", f_loss, [x1, mo, target], [wts["ln_post_mlp"]],
                                                  [(d, F32), (d, BF16)], [d, d])
    loss = 0.5 * jnp.sum(loss_cols) / d

    grads = {"ln_post_mlp": g_ln_post_mlp}
    grads["w_ff2"] = _mm_tn(act, d_mo, name="grad_ff2")

    def relu2_bwd(acc, ex):
        return [acc * (2.0 * jnp.maximum(ex[0], 0.0))]

    (d_u,) = _mm(d_mo, wts["w_ff2"], out_dtype=[BF16], name="ff2_bwd", transpose_b=True, extras=[u], epilogue=relu2_bwd)
    grads["w_ff1"] = _mm_tn(h2, d_u, name="grad_ff1")
    d_h2 = _mm(d_u, wts["w_ff1"], out_dtype=F32, name="ff1_bwd", transpose_b=True)

    def f_resid1_bwd(ti, pa):
        gres, dh2, x1v, mx = ti
        dx1n, dg_pre_mlp = _rms_bwd(x1v, pa[1], dh2)
        dx1 = gres + dx1n
        dmix, dg_post_mix = _rms_bwd(mx, pa[0], dx1)
        return [dx1, dmix], [dg_post_mix, dg_pre_mlp]

    d_x1, d_mix, grads["ln_post_mix"], grads["ln_pre_mlp"] = _rowwise(
        "resid_mix_bwd", f_resid1_bwd, [g2, d_h2, x1, mix], [wts["ln_post_mix"], wts["ln_pre_mlp"]],
        [(d, F32), (d, BF16)], [d, d])
    grads["w_out"] = _mm_tn(merged, d_mix, name="grad_out")
    d_merged = _mm(d_mix, wts["w_out"], out_dtype=F32, name="proj_out_bwd", transpose_b=True)

    def f_gate_bwd(ti, pa):
        dm, ga, gb, ya, yb = ti
        sa, sb = _sigmoid(ga), _sigmoid(gb)
        return [dm * sa, dm * sb, dm * ya * (sa * (1.0 - sa)), dm * yb * (sb * (1.0 - sb))], []

    d_ya, d_yb, d_ga, d_gb = _rowwise("gate_bwd", f_gate_bwd,
                                      [d_merged, zl.win(z, "ga"), zl.win(z, "gb"), y_mla, y_fox], [],
                                      [(d, BF16)] * 4)
    grads["w_o_mla"] = _unpad_head_rows(_mm_tn(o_mla, d_ya, name="grad_o_mla"), MLA_HEADS, MLA_V)
    grads["w_o_fox"] = _unpad_head_rows(_mm_tn(o_fox, d_yb, name="grad_o_fox"), FOX_HEADS, FOX_DIM)
    do_mla = _with_delta(_mm(d_ya, w_o_mla, out_dtype=F32, name="proj_o_mla_bwd", transpose_b=True), o_mla,
                         MLA_HEADS, name="mla_delta")
    do_fox = _with_delta(_mm(d_yb, w_o_fox, out_dtype=F32, name="proj_o_fox_bwd", transpose_b=True), o_fox,
                         FOX_HEADS, name="fox_delta")

    dk_mla, dv_mla, dq_mla = _flash_bwd(qb_mla, k_mla, v_mla, do_mla, scale_mla, MLA_HEADS, MLA_NOPE + MLA_ROPE,
                                        name="mla_bwd")
    dk_fox, dv_fox, dq_fox = _flash_bwd(qb_fox, k_fox, v_fox, do_fox, scale_fox, FOX_HEADS, FOX_DIM, name="fox_bwd")
    d_fl = _forget_cumsum_bwd(dq_fox, dk_fox, z, fl_cb, FOX_HEADS)

    def f_rope_q_bwd(ti, pa):
        g, c_, a_, b_ = ti
        return [jnp.concatenate([_rope_block(blk, c_, -a_, -b_) for blk in _blocks(g, MLA_HEADS)], axis=1)], []

    (d_qf,) = _rowwise("rope_q_bwd", f_rope_q_bwd, [dq_mla, rc, ra, rb], [], [(hw, BF16)])
    grads["w_uq"] = _unpad_heads(_mm_tn(cqn, d_qf, name="grad_uq"), MLA_HEADS, MLA_NOPE + MLA_ROPE)
    d_cqn = _mm(d_qf, w_uq, out_dtype=F32, name="proj_uq_bwd", transpose_b=True)

    def f_mla_kv_bwd(ti, pa):
        gk, gv, c_, a_, b_ = ti
        k_blocks = _blocks(gk, MLA_HEADS)
        tot = k_blocks[0]
        for blk in k_blocks[1:]:
            tot = tot + blk
        return [jnp.concatenate([gk, gv], axis=1), _rope_block(tot, c_, -a_, -b_)], []

    d_kv, d_kr = _rowwise("mla_kv_bwd", f_mla_kv_bwd, [dk_mla, dv_mla, rc, ra, rb], [], [(2 * hw, BF16), (LANES, BF16)])
    g_ukv = _mm_tn(ckvn, d_kv, name="grad_ukv")
    grads["w_uk"] = _unpad_heads(g_ukv[:, :hw], MLA_HEADS, MLA_NOPE)
    grads["w_uv"] = _unpad_heads(g_ukv[:, hw:], MLA_HEADS, MLA_V)
    d_ckvn = _mm(d_kv, w_ukv, out_dtype=F32, name="proj_ukv_bwd", transpose_b=True)

    def f_mla_norms_bwd(ti, pa):
        cq, ckv, dcqn, dckvn = ti
        dcq, dg_q = _rms_bwd(cq, pa[0], dcqn)
        dckv, dg_kv = _rms_bwd(ckv, pa[1], dckvn)
        return [dcq, dckv], [dg_q, dg_kv]

    d_cq, d_ckv, grads["q_a_norm"], grads["kv_a_norm"] = _rowwise(
        "mla_norms_bwd", f_mla_norms_bwd, [zl.win(z, "cq"), zl.win(z, "ckv"), d_cqn, d_ckvn],
        [wts["q_a_norm"], wts["kv_a_norm"]], [(MLA_Q_LORA, BF16), (MLA_KV_LORA, BF16)], [MLA_Q_LORA, MLA_KV_LORA])

    d_z = jnp.concatenate([dq_fox.astype(BF16), dk_fox.astype(BF16), dv_fox.astype(BF16), d_ga, d_gb, d_cq, d_ckv,
                           d_kr, d_fl], axis=1)
    assert d_z.shape[1] == zl.width

    def f_bias(ti, pa):
        return [], [jnp.sum(ti[0].astype(F32), axis=0, keepdims=True)]

    (g_b_in,) = _rowwise("grad_b_in", f_bias, [d_z], [], [], [zl.width])
    grads["b_in"] = zl.from_kernel(g_b_in)
    grads["w_in"] = zl.from_kernel(_mm_tn(h, d_z, name="grad_in"))
    d_h = _mm(d_z, w_in, out_dtype=F32, name="proj_in_bwd", transpose_b=True)

    def f_norm_in_bwd(ti, pa):
        dx1v, dh, xa = ti
        dxn, dg = _rms_bwd(xa, pa[0], dh)
        return [dx1v + dxn], [dg]

    grad_x, grads["ln_pre_mix"] = _rowwise("norm_in_bwd", f_norm_in_bwd, [d_x1, d_h, x], [wts["ln_pre_mix"]],
                                           [(d, F32)], [d])
    return loss, grad_x, grads


class _PackLayout:
    def __init__(self, shapes):
        self.shapes = list(shapes)
        self.width = _round_up(max(b for _, b in shapes), LANES)
        self.bands = []
        row = 0
        shelf = []
        for idx, (a, b) in enumerate(shapes):
            if 2 * _round_up(b, LANES) > self.width:
                self.bands.append((row, _round_up(a, 32), [(idx, 0)]))
                row += _round_up(a, 32)
            else:
                shelf.append(idx)
        col, items = 0, []
        for idx in shelf:
            wb = _round_up(shapes[idx][1], LANES)
            if col + wb > self.width:
                hgt = max(_round_up(shapes[i][0], 32) for i, _ in items)
                self.bands.append((row, hgt, items))
                row += hgt
                col, items = 0, []
            items.append((idx, col))
            col += wb
        if items:
            hgt = max(_round_up(shapes[i][0], 32) for i, _ in items)
            self.bands.append((row, hgt, items))
            row += hgt
        self.rows = _round_up(row, 64)

    def pack(self, arrs):
        bands = []
        for _, hgt, items in self.bands:
            parts = []
            for k, (idx, col) in enumerate(items):
                a, b = self.shapes[idx]
                nxt = items[k + 1][1] if k + 1 < len(items) else self.width
                parts.append(jnp.pad(arrs[idx], ((0, hgt - a), (0, nxt - col - b))))
            bands.append(parts[0] if len(parts) == 1 else jnp.concatenate(parts, axis=1))
        used = sum(hgt for _, hgt, _ in self.bands)
        if used < self.rows:
            bands.append(jnp.zeros((self.rows - used, self.width), arrs[0].dtype))
        return jnp.concatenate(bands, axis=0)

    def unpack(self, packed):
        out = [None] * len(self.shapes)
        for row, _, items in self.bands:
            for idx, col in items:
                a, b = self.shapes[idx]
                out[idx] = packed[row:row + a, col:col + b]
        return out


ANY = pl.BlockSpec(memory_space=pl.ANY)


def _place():
    return lax.axis_index("x"), lax.axis_index("y"), lax.axis_index("c")


def _gather_weights(wpk):
    rows, wd = wpk.shape
    half = rows // 2

    def body(w_ref, out_ref, send_sems, recv_sems, local_sem):
        x, y, c = _place()
        sibling = (x, y, 1 - c)
        chips = [(1 - x, y), (x, 1 - y), (1 - x, 1 - y)]

        def slab(chip, hf):
            return out_ref.at[2 * chip[0] + chip[1], pl.ds(hf * half, half), :]

        def copy(k, chip, hf, to, src=None):
            return pltpu.make_async_remote_copy(
                src_ref=slab(chip, hf) if src is None else src, dst_ref=slab(chip, hf),
                send_sem=send_sems.at[k], recv_sem=recv_sems.at[k], device_id=to, device_id_type=MESH)

        mine = pltpu.make_async_copy(w_ref, out_ref.at[2 * x + y], local_sem)
        mine.start()
        first = [copy(j, (x, y), c, (*chip, c), src=w_ref.at[pl.ds(c * half, half), :]) for j, chip in enumerate(chips)]
        for cp in first:
            cp.start()
        passed = [copy(3 + j, chip, c, sibling) for j, chip in enumerate(chips)]
        for j, chip in enumerate(chips):
            copy(j, chip, c, (x, y, c)).wait_recv()
            passed[j].start()
        for j, chip in enumerate(chips):
            copy(3 + j, chip, 1 - c, (x, y, c)).wait_recv()
        for cp in first + passed:
            cp.wait_send()
        mine.wait()

    return pl.pallas_call(
        body, out_shape=jax.ShapeDtypeStruct((N_CHIPS, rows, wd), wpk.dtype),
        in_specs=[ANY], out_specs=ANY,
        scratch_shapes=[pltpu.SemaphoreType.DMA((6,)), pltpu.SemaphoreType.DMA((6,)), pltpu.SemaphoreType.DMA],
        name="gather_weights")(wpk)


def _sibling_exchange(g):
    n, rows, wd = g.shape
    half = rows // 2

    def body(g_ref, out_ref, send_sem, recv_sem):
        x, y, c = _place()
        cp = pltpu.make_async_remote_copy(
            src_ref=g_ref.at[:, pl.ds((1 - c) * half, half), :], dst_ref=out_ref,
            send_sem=send_sem, recv_sem=recv_sem, device_id=(x, y, 1 - c), device_id_type=MESH)
        cp.start()
        cp.wait()

    return pl.pallas_call(
        body, out_shape=jax.ShapeDtypeStruct((n, half, wd), g.dtype), in_specs=[ANY], out_specs=ANY,
        scratch_shapes=[pltpu.SemaphoreType.DMA, pltpu.SemaphoreType.DMA], name="grad_sibling_exchange")(g)


def _chip_exchange(part):
    n, half, wd = part.shape

    def body(p_ref, out_ref, send_sems, recv_sems):
        x, y, c = _place()
        chips = [(1 - x, y), (x, 1 - y), (1 - x, 1 - y)]

        def copy(j, chip):
            return pltpu.make_async_remote_copy(
                src_ref=p_ref.at[2 * chip[0] + chip[1]], dst_ref=out_ref.at[j],
                send_sem=send_sems.at[j], recv_sem=recv_sems.at[j], device_id=(*chip, c), device_id_type=MESH)

        cps = [copy(j, chip) for j, chip in enumerate(chips)]
        for cp in cps:
            cp.start()
        for cp in cps:
            cp.wait()

    return pl.pallas_call(
        body, out_shape=jax.ShapeDtypeStruct((3, half, wd), part.dtype), in_specs=[ANY], out_specs=ANY,
        scratch_shapes=[pltpu.SemaphoreType.DMA((3,)), pltpu.SemaphoreType.DMA((3,))], name="grad_chip_exchange")(part)


def _sibling_concat(mine):
    half, wd = mine.shape

    def body(m_ref, out_ref, send_sem, recv_sem, local_sem):
        x, y, c = _place()
        loc = pltpu.make_async_copy(m_ref, out_ref.at[c], local_sem)
        loc.start()
        cp = pltpu.make_async_remote_copy(
            src_ref=m_ref, dst_ref=out_ref.at[c], send_sem=send_sem, recv_sem=recv_sem,
            device_id=(x, y, 1 - c), device_id_type=MESH)
        cp.start()
        pltpu.make_async_remote_copy(
            src_ref=m_ref, dst_ref=out_ref.at[1 - c], send_sem=send_sem, recv_sem=recv_sem,
            device_id=(x, y, 1 - c), device_id_type=MESH).wait_recv()
        cp.wait_send()
        loc.wait()

    return pl.pallas_call(
        body, out_shape=jax.ShapeDtypeStruct((2, half, wd), mine.dtype), in_specs=[ANY], out_specs=ANY,
        scratch_shapes=[pltpu.SemaphoreType.DMA, pltpu.SemaphoreType.DMA, pltpu.SemaphoreType.DMA],
        name="grad_sibling_concat")(mine)


def _adamw(w, g, m, v):
    m = ADAM_B1 * m + (1.0 - ADAM_B1) * g
    v = ADAM_B2 * v + (1.0 - ADAM_B2) * (g * g)
    m_hat = m / (1.0 - ADAM_B1 ** ADAM_STEP)
    v_hat = v / (1.0 - ADAM_B2 ** ADAM_STEP)
    delta = -ADAM_LR * (m_hat / (jnp.sqrt(v_hat) + ADAM_EPS) + ADAM_WD * w)
    return delta, m, v


def _small_allreduce_adamw(gs, ws, ms, vs):
    n_dev = 8
    n_par = len(gs)
    wd = PACK_W
    chunks = []
    for p, g in enumerate(gs):
        for off in range(0, g.shape[1], wd):
            chunks.append((p, len(chunks), off, min(wd, g.shape[1] - off)))
    rows = _round_up(len(chunks), 8)

    def body(*refs):
        g_refs, w_refs, m_refs, v_refs = (refs[k * n_par:(k + 1) * n_par] for k in range(4))
        go_refs, d_refs, mo_refs, vo_refs = (refs[(4 + k) * n_par:(5 + k) * n_par] for k in range(4))
        mine_ref, all_ref, send_sems, recv_sems = refs[8 * n_par:]
        x, y, c = _place()
        me, sibling = (x, y, c), (x, y, 1 - c)
        chips = [(1 - x, y), (x, 1 - y), (1 - x, 1 - y)]

        def slot(px, py, pc):
            return all_ref.at[4 * px + 2 * py + pc]

        def copy(k, block, to, src=None):
            return pltpu.make_async_remote_copy(
                src_ref=slot(*block) if src is None else src, dst_ref=slot(*block),
                send_sem=send_sems.at[k], recv_sem=recv_sems.at[k], device_id=to, device_id_type=MESH)

        mine_ref[...] = jnp.zeros_like(mine_ref)
        for p, row, off, width in chunks:
            mine_ref[row:row + 1, 0:width] = g_refs[p][:, off:off + width]
        all_ref[4 * x + 2 * y + c] = mine_ref[...]
        first = [copy(0, me, sibling, src=mine_ref)]
        first += [copy(1 + j, me, (*chip, c), src=mine_ref) for j, chip in enumerate(chips)]
        for cp in first:
            cp.start()
        passed = [copy(4 + j, (*chip, c), sibling) for j, chip in enumerate(chips)]
        for j, chip in enumerate(chips):
            copy(1 + j, (*chip, c), me).wait_recv()
            passed[j].start()
        copy(0, sibling, me).wait_recv()
        for j, chip in enumerate(chips):
            copy(4 + j, (*chip, 1 - c), me).wait_recv()
        for cp in first + passed:
            cp.wait_send()
        tot = jnp.zeros((rows, wd), F32)
        for dev in range(n_dev):
            tot = tot + all_ref[dev]
        mine_ref[...] = tot
        for p, row, off, width in chunks:
            cols = slice(off, off + width)
            g = mine_ref[row:row + 1, 0:width]
            delta, m_new, v_new = _adamw(w_refs[p][:, cols], g, m_refs[p][:, cols], v_refs[p][:, cols])
            go_refs[p][:, cols] = g
            d_refs[p][:, cols] = delta
            mo_refs[p][:, cols] = m_new
            vo_refs[p][:, cols] = v_new

    vm = pl.BlockSpec(memory_space=pltpu.VMEM)
    shp = [jax.ShapeDtypeStruct(g.shape, F32) for g in gs]
    res = pl.pallas_call(
        body, out_shape=shp * 4, in_specs=[vm] * (4 * n_par), out_specs=[vm] * (4 * n_par),
        scratch_shapes=[pltpu.VMEM((rows, wd), F32), pltpu.VMEM((n_dev, rows, wd), F32),
                        pltpu.SemaphoreType.DMA((7,)), pltpu.SemaphoreType.DMA((7,))],
        name="small_allreduce_adamw")(*gs, *ws, *ms, *vs)
    return [res[k * n_par:(k + 1) * n_par] for k in range(4)]


def kernel(x, positions, ln_pre_mix, ln_post_mix, ln_pre_mlp, ln_post_mlp, w_in, b_in, q_a_norm, w_uq, kv_a_norm, w_uk, w_uv, w_o_mla, w_o_fox, w_out, w_ff1, w_ff2, loss_target, m_ln_pre_mix, m_ln_post_mix, m_ln_pre_mlp, m_ln_post_mlp, m_w_in, m_b_in, m_q_a_norm, m_w_uq, m_kv_a_norm, m_w_uk, m_w_uv, m_w_o_mla, m_w_o_fox, m_w_out, m_w_ff1, m_w_ff2, v_ln_pre_mix, v_ln_post_mix, v_ln_pre_mlp, v_ln_post_mlp, v_w_in, v_b_in, v_q_a_norm, v_w_uq, v_kv_a_norm, v_w_uk, v_w_uv, v_w_o_mla, v_w_o_fox, v_w_out, v_w_ff1, v_w_ff2):
    w = dict(ln_pre_mix=ln_pre_mix, ln_post_mix=ln_post_mix, ln_pre_mlp=ln_pre_mlp, ln_post_mlp=ln_post_mlp, w_in=w_in,
             b_in=b_in, q_a_norm=q_a_norm, w_uq=w_uq, kv_a_norm=kv_a_norm, w_uk=w_uk, w_uv=w_uv, w_o_mla=w_o_mla,
             w_o_fox=w_o_fox, w_out=w_out, w_ff1=w_ff1, w_ff2=w_ff2)
    mom = dict(ln_pre_mix=m_ln_pre_mix, ln_post_mix=m_ln_post_mix, ln_pre_mlp=m_ln_pre_mlp, ln_post_mlp=m_ln_post_mlp,
               w_in=m_w_in, b_in=m_b_in, q_a_norm=m_q_a_norm, w_uq=m_w_uq, kv_a_norm=m_kv_a_norm, w_uk=m_w_uk,
               w_uv=m_w_uv, w_o_mla=m_w_o_mla, w_o_fox=m_w_o_fox, w_out=m_w_out, w_ff1=m_w_ff1, w_ff2=m_w_ff2)
    var = dict(ln_pre_mix=v_ln_pre_mix, ln_post_mix=v_ln_post_mix, ln_pre_mlp=v_ln_pre_mlp, ln_post_mlp=v_ln_post_mlp,
               w_in=v_w_in, b_in=v_b_in, q_a_norm=v_q_a_norm, w_uq=v_w_uq, kv_a_norm=v_kv_a_norm, w_uk=v_w_uk,
               w_uv=v_w_uv, w_o_mla=v_w_o_mla, w_o_fox=v_w_o_fox, w_out=v_w_out, w_ff1=v_w_ff1, w_ff2=v_w_ff2)

    big_names = [nm for nm, _ in BIG]
    layout = _PackLayout([w[nm].shape[1:] for nm in big_names])
    rows, pack_w = layout.rows, layout.width
    half = rows // 2

    gathered = _gather_weights(layout.pack([w[nm][0].astype(BF16) for nm in big_names]))
    per_chip = [layout.unpack(gathered[ch]) for ch in range(N_CHIPS)]
    full = {nm: wv for nm, wv in w.items() if nm in SMALL}
    for k, (nm, axis) in enumerate(BIG):
        full[nm] = jnp.concatenate([per_chip[ch][k] for ch in range(N_CHIPS)], axis=axis - 1)

    loss_local, grad_x, grads = _local_step(x[0], positions[0], loss_target[0], full)
    loss = lax.psum(loss_local, ("x", "y", "c"))

    gpk = jnp.stack([
        layout.pack([jnp.split(grads[nm], N_CHIPS, axis=axis - 1)[ch] for nm, axis in BIG]) for ch in range(N_CHIPS)])
    c = lax.axis_index("c")
    chip = 2 * lax.axis_index("x") + lax.axis_index("y")
    theirs = _sibling_exchange(gpk)
    mine_half = lax.dynamic_slice_in_dim(gpk, c * half, half, axis=1)

    def f_add2(ti, pa):
        tot = ti[0] + ti[1]
        return [tot, tot], []

    chip_part, chip_part_bf = _rowwise("grad_add_sibling", f_add2, [mine_half.reshape(N_CHIPS * half, pack_w),
                                                                  theirs.reshape(N_CHIPS * half, pack_w)], [],
                                       [(pack_w, F32), (pack_w, BF16)])
    others = _chip_exchange(chip_part_bf.reshape(N_CHIPS, half, pack_w))
    own = lax.dynamic_index_in_dim(chip_part.reshape(N_CHIPS, half, pack_w), chip, axis=0, keepdims=False)

    def f_add4(ti, pa):
        return [((ti[0] + ti[1].astype(F32)) + ti[2].astype(F32)) + ti[3].astype(F32)], []

    (red_half,) = _rowwise("grad_add_chips", f_add4, [own, others[0], others[1], others[2]], [], [(pack_w, F32)])
    g_shards = layout.unpack(_sibling_concat(red_half).reshape(rows, pack_w))

    def f_adamw(ti, pa):
        wv, gv, mv, vv = ti
        return list(_adamw(wv, gv, mv, vv)), []

    out = {"grad": {}, "delta": {}, "m": {}, "v": {}}
    for nm, g_sh in zip(big_names, g_shards, strict=True):
        wd = g_sh.shape[1]
        d_sh, m_sh, v_sh = _rowwise("adamw_" + nm, f_adamw, [w[nm][0], g_sh, mom[nm][0], var[nm][0]], [], [(wd, F32)] * 3)
        out["grad"][nm], out["delta"][nm], out["m"][nm], out["v"][nm] = g_sh[None], d_sh[None], m_sh[None], v_sh[None]

    small = _small_allreduce_adamw([grads[nm] for nm in SMALL], [w[nm] for nm in SMALL],
                                   [mom[nm] for nm in SMALL], [var[nm] for nm in SMALL])
    for kind, arrs in zip(("grad", "delta", "m", "v"), small, strict=True):
        for nm, arr in zip(SMALL, arrs, strict=True):
            out[kind][nm] = arr

    return (loss, grad_x[None], *[out["grad"][nm] for nm in ALL_W], *[out["delta"][nm] for nm in ALL_W],
            *[out["m"][nm] for nm in ALL_W], *[out["v"][nm] for nm in ALL_W])
```

```python
import functools
import math

import jax
import jax.numpy as jnp
from jax import lax
from jax.experimental import pallas as pl
from jax.experimental.pallas import tpu as pltpu

F32 = jnp.float32
BF16 = jnp.bfloat16

MLA_HEADS = 8
MLA_Q_LORA = 256
MLA_KV_LORA = 128
MLA_NOPE = 64
MLA_ROPE = 32
MLA_V = 64
FOX_HEADS = 8
FOX_DIM = 64
ROPE_THETA = 10000.0
NORM_EPS = 1e-6
HALF_ROPE = MLA_ROPE // 2

ADAM_LR = 0.001
ADAM_B1 = 0.9
ADAM_B2 = 0.999
ADAM_EPS = 1e-08
ADAM_WD = 0.01
ADAM_STEP = 10

LANES = 128
VMEM_LIMIT = 56 * 1024 * 1024
ATT_TILE = 512
MM_LHS_BLOCK_BYTES = 9 * 1024 * 1024
NEG = -1e30
LOG2E = math.log2(math.e)
MESH = pl.DeviceIdType.MESH

V_ONES = 64
FOX_Q_F = 64
FOX_Q_L = 67
FOX_Q_ONES = 70
MLA_Q_L = 96

BIG = (("w_in", 2), ("w_uq", 2), ("w_uk", 2), ("w_uv", 2), ("w_o_mla", 2), ("w_o_fox", 2),
       ("w_out", 1), ("w_ff1", 2), ("w_ff2", 1))
SMALL = ("ln_pre_mix", "ln_post_mix", "ln_pre_mlp", "ln_post_mlp", "b_in", "q_a_norm", "kv_a_norm")
ALL_W = ("ln_pre_mix", "ln_post_mix", "ln_pre_mlp", "ln_post_mlp", "w_in", "b_in", "q_a_norm", "w_uq",
         "kv_a_norm", "w_uk", "w_uv", "w_o_mla", "w_o_fox", "w_out", "w_ff1", "w_ff2")
N_CHIPS = 4
PACK_W = 1024
LOCAL_PIECES = 8

_NT = (((1,), (1,)), ((), ()))
_TN = (((0,), (0,)), ((), ()))


def _cparams(sem=None):
    return pltpu.CompilerParams(dimension_semantics=sem, vmem_limit_bytes=VMEM_LIMIT)


def _divisor_tile(n, limit, mult):
    if n <= limit:
        return n
    best = None
    t = mult
    while t <= limit:
        if n % t == 0:
            best = t
        t += mult
    assert best is not None, (n, limit, mult)
    return best


def _round_up(v, mult):
    return -(-v // mult) * mult


def _mm(a, b, *, out_dtype, name, bias=None, transpose_b=False, extras=(), epilogue=None):
    m, k = a.shape
    n = b.shape[0] if transpose_b else b.shape[1]
    assert (b.shape[1] if transpose_b else b.shape[0]) == k and a.dtype == BF16 and b.dtype == BF16
    out_dtypes = list(out_dtype) if isinstance(out_dtype, (list, tuple)) else [out_dtype]
    tm = _divisor_tile(m, min(2048, max(512, MM_LHS_BLOCK_BYTES // (2 * k))), 16)
    tn = _divisor_tile(n, 512, LANES)
    n_ex = len(extras)

    def body(*refs):
        a_ref, b_ref = refs[:2]
        pos = 2
        bias_ref = None
        if bias is not None:
            bias_ref = refs[pos]
            pos += 1
        ex_refs = refs[pos:pos + n_ex]
        o_refs = refs[pos + n_ex:]
        if transpose_b:
            acc = lax.dot_general(a_ref[...], b_ref[...], _NT, preferred_element_type=F32)
        else:
            acc = jnp.dot(a_ref[...], b_ref[...], preferred_element_type=F32)
        if bias_ref is not None:
            acc = acc + bias_ref[...]
        vals = [acc] if epilogue is None else epilogue(acc, [r[...] for r in ex_refs])
        for ref, val in zip(o_refs, vals, strict=True):
            ref[...] = val.astype(ref.dtype)

    b_spec = pl.BlockSpec((tn, k), lambda i, j: (j, 0)) if transpose_b else pl.BlockSpec((k, tn), lambda i, j: (0, j))
    in_specs = [pl.BlockSpec((tm, k), lambda i, j: (i, 0)), b_spec]
    args = [a, b]
    if bias is not None:
        in_specs.append(pl.BlockSpec((1, tn), lambda i, j: (0, j)))
        args.append(bias)
    in_specs += [pl.BlockSpec((tm, tn), lambda i, j: (i, j)) for _ in extras]
    args += list(extras)
    res = pl.pallas_call(
        body, grid=(m // tm, n // tn), in_specs=in_specs,
        out_specs=[pl.BlockSpec((tm, tn), lambda i, j: (i, j)) for _ in out_dtypes],
        out_shape=[jax.ShapeDtypeStruct((m, n), dt) for dt in out_dtypes],
        compiler_params=_cparams(("parallel", "parallel")), name=name)(*args)
    return res if isinstance(out_dtype, (list, tuple)) else res[0]


def _mm_tn(a, b, *, name):
    s, m = a.shape
    s2, n = b.shape
    assert s == s2 and a.dtype == BF16 and b.dtype == BF16
    tm = _divisor_tile(m, 1024, LANES)
    tn = _divisor_tile(n, 2304, LANES)
    tk = _divisor_tile(s, 512, 16)

    def body(a_ref, b_ref, o_ref):
        @pl.when(pl.program_id(2) == 0)
        def _():
            o_ref[...] = jnp.zeros_like(o_ref)

        o_ref[...] += lax.dot_general(a_ref[...], b_ref[...], _TN, preferred_element_type=F32)

    return pl.pallas_call(
        body, grid=(m // tm, n // tn, s // tk),
        in_specs=[pl.BlockSpec((tk, tm), lambda i, j, k: (k, i)), pl.BlockSpec((tk, tn), lambda i, j, k: (k, j))],
        out_specs=pl.BlockSpec((tm, tn), lambda i, j, k: (i, j)),
        out_shape=jax.ShapeDtypeStruct((m, n), F32),
        compiler_params=_cparams(("parallel", "parallel", "arbitrary")), name=name)(a, b)


def _rowwise(name, fn, tiled, params, outs, reds=(), reverse=False):
    wins = [t if isinstance(t, tuple) else (t, 0, t.shape[1]) for t in tiled]
    s = wins[0][0].shape[0]
    row_bytes = sum(w * arr.dtype.itemsize for arr, _, w in wins) + sum(w * jnp.dtype(d).itemsize for w, d in outs)
    ts = _divisor_tile(s, max(16, min(1024, (6 * 1024 * 1024) // row_bytes)), 16)
    nt, npar, nout = len(wins), len(params), len(outs)
    n_tiles = s // ts

    def row(i):
        return n_tiles - 1 - i if reverse else i

    def body(*refs):
        tin = [r[...] for r in refs[:nt]]
        par = [r[...] for r in refs[nt:nt + npar]]
        out_refs = refs[nt + npar:nt + npar + nout]
        red_refs = refs[nt + npar + nout:]
        o, r = fn(tin, par)
        for ref, val in zip(out_refs, o, strict=True):
            ref[...] = val.astype(ref.dtype)
        if red_refs:
            @pl.when(pl.program_id(0) == 0)
            def _():
                for ref in red_refs:
                    ref[...] = jnp.zeros_like(ref)

            for ref, val in zip(red_refs, r, strict=True):
                ref[...] += val

    in_specs = [pl.BlockSpec((ts, w), functools.partial(lambda i, cb: (row(i), cb), cb=cb)) for _, cb, w in wins]
    in_specs += [pl.BlockSpec(p.shape, lambda i: (0, 0)) for p in params]
    out_specs = [pl.BlockSpec((ts, w), lambda i: (row(i), 0)) for w, _ in outs]
    out_specs += [pl.BlockSpec((1, w), lambda i: (0, 0)) for w in reds]
    out_shape = [jax.ShapeDtypeStruct((s, w), d) for w, d in outs]
    out_shape += [jax.ShapeDtypeStruct((1, w), F32) for w in reds]
    return pl.pallas_call(
        body, grid=(n_tiles,), in_specs=in_specs, out_specs=out_specs, out_shape=out_shape,
        compiler_params=_cparams(("arbitrary",)), name=name)(*[w[0] for w in wins], *params)


def _rms(x, g):
    r = lax.rsqrt(jnp.mean(x * x, axis=-1, keepdims=True) + NORM_EPS)
    return x * r * g, r


def _rms_bwd(x, g, dy):
    r = lax.rsqrt(jnp.mean(x * x, axis=-1, keepdims=True) + NORM_EPS)
    gy = dy * g
    dx = r * gy - x * (r * r * r) * jnp.mean(x * gy, axis=-1, keepdims=True)
    dg = jnp.sum(dy * (x * r), axis=0, keepdims=True)
    return dx, dg


def _sigmoid(x):
    return 1.0 / (1.0 + jnp.exp(-x))


def _split3(x):
    hi = x.astype(BF16).astype(F32)
    r = x - hi
    mid = r.astype(BF16).astype(F32)
    lo = (r - mid).astype(BF16).astype(F32)
    return hi, mid, lo


def _lane(shape):
    return lax.broadcasted_iota(jnp.int32, shape, 1)


def _put3(blk, lane, pos, pieces):
    for k, piece in enumerate(pieces):
        blk = jnp.where(lane == pos + k, piece, blk)
    return blk


def _lane_column(blk, lane, pos):
    return jnp.sum(jnp.where(lane == pos, blk, 0.0), axis=1, keepdims=True)


def _blocks(a, nh):
    return [a[:, h * LANES:(h + 1) * LANES] for h in range(nh)]


def _rope_block(x, c, sa, sb):
    return x * c + pltpu.roll(x, LANES - HALF_ROPE, 1) * sa + pltpu.roll(x, HALF_ROPE, 1) * sb


def _forget_cumsum(z, cb):
    s = z.shape[0]
    ts = _divisor_tile(s, 512, LANES)

    def body(x_ref, col_ref, carry):
        @pl.when(pl.program_id(0) == 0)
        def _():
            carry[...] = jnp.zeros_like(carry)

        x = x_ref[...]
        lf = jnp.minimum(x, 0.0) - jnp.log1p(jnp.exp(-jnp.abs(x)))
        r = lax.broadcasted_iota(jnp.int32, (ts, ts), 0)
        c = lax.broadcasted_iota(jnp.int32, (ts, ts), 1)
        tri = jnp.where(c <= r, 1.0, 0.0).astype(F32)
        col_ref[...] = jnp.dot(tri, lf, preferred_element_type=F32, precision=lax.Precision.HIGHEST) + carry[...]
        carry[...] += jnp.sum(lf, axis=0, keepdims=True)

    return pl.pallas_call(
        body, grid=(s // ts,),
        in_specs=[pl.BlockSpec((ts, LANES), lambda i: (i, cb))],
        out_specs=pl.BlockSpec((ts, LANES), lambda i: (i, 0)),
        out_shape=jax.ShapeDtypeStruct((s, LANES), F32),
        scratch_shapes=[pltpu.VMEM((1, LANES), F32)],
        compiler_params=_cparams(("arbitrary",)), name="forget_cumsum")(z)


def _forget_cumsum_bwd(dq, dk, z, cb, nh):
    s = z.shape[0]
    ts = _divisor_tile(s, 512, LANES)
    nt = s // ts
    wd = nh * LANES

    def body(dq_ref, dk_ref, x_ref, o_ref, carry):
        @pl.when(pl.program_id(0) == 0)
        def _():
            carry[...] = jnp.zeros_like(carry)

        lane = _lane((ts, LANES))
        df = jnp.zeros((ts, LANES), F32)
        for h in range(nh):
            cols = slice(h * LANES, (h + 1) * LANES)
            d_h = _lane_column(dq_ref[:, cols], lane, FOX_Q_F) - _lane_column(dk_ref[:, cols], lane, FOX_Q_ONES)
            df = jnp.where(lane == h, d_h, df)
        r = lax.broadcasted_iota(jnp.int32, (ts, ts), 0)
        c = lax.broadcasted_iota(jnp.int32, (ts, ts), 1)
        tri = jnp.where(c >= r, 1.0, 0.0).astype(F32)
        rc = jnp.dot(tri, df, preferred_element_type=F32, precision=lax.Precision.HIGHEST) + carry[...]
        carry[...] += jnp.sum(df, axis=0, keepdims=True)
        o_ref[...] = (rc * (1.0 / (1.0 + jnp.exp(x_ref[...])))).astype(o_ref.dtype)

    return pl.pallas_call(
        body, grid=(nt,),
        in_specs=[pl.BlockSpec((ts, wd), lambda i: (nt - 1 - i, 0)),
                  pl.BlockSpec((ts, wd), lambda i: (nt - 1 - i, 0)),
                  pl.BlockSpec((ts, LANES), lambda i: (nt - 1 - i, cb))],
        out_specs=pl.BlockSpec((ts, LANES), lambda i: (nt - 1 - i, 0)),
        out_shape=jax.ShapeDtypeStruct((s, LANES), BF16),
        scratch_shapes=[pltpu.VMEM((1, LANES), F32)],
        compiler_params=_cparams(("arbitrary",)), name="forget_cumsum_bwd")(dq, dk, z)


def _flash_fwd(q, k, v, scale, nh, l_lane, *, name):
    s = q.shape[0]
    t = min(ATT_TILE, s)
    nq = s // t
    c = scale * LOG2E

    def body(q_ref, k_ref, v_ref, o_ref, qb_ref):
        i = pl.program_id(1)
        qb = q_ref[...]

        def scores(j):
            kb = k_ref[pl.ds(pl.multiple_of(j * t, t), t), :]
            return lax.dot_general(qb, kb, _NT, preferred_element_type=F32)

        def update(j, sc, carry):
            m, acc = carry
            m_new = jnp.maximum(m, jnp.max(sc, axis=1, keepdims=True))
            p = jnp.exp2((sc - m_new) * c)
            alpha = jnp.exp2((m - m_new) * c)
            vb = v_ref[pl.ds(pl.multiple_of(j * t, t), t), :]
            acc = alpha * acc + jnp.dot(p.astype(BF16), vb, preferred_element_type=F32)
            return m_new, acc

        def pair(jj, cr):
            s0 = scores(2 * jj)
            s1 = scores(2 * jj + 1)
            return update(2 * jj + 1, s1, update(2 * jj, s0, cr))

        init = (jnp.full((t, 1), NEG, F32), jnp.zeros((t, LANES), F32))
        carry = lax.fori_loop(0, i >> 1, pair, init)
        carry = lax.cond((i & 1) == 1, lambda cr: update(i - 1, scores(i - 1), cr), lambda cr: cr, carry)
        row = lax.broadcasted_iota(jnp.int32, (t, t), 0)
        col = lax.broadcasted_iota(jnp.int32, (t, t), 1)
        m, acc = update(i, jnp.where(col <= row, scores(i), NEG), carry)
        lane = _lane((t, LANES))
        l = _lane_column(acc, lane, V_ONES)
        o_ref[...] = (acc / l).astype(o_ref.dtype)
        big_l = m + jnp.log(l) / scale
        qb_ref[...] = _put3(qb.astype(F32), lane, l_lane, _split3(-big_l)).astype(qb_ref.dtype)

    head_rows = pl.BlockSpec((t, LANES), lambda h, i: (i, h))
    head_all = pl.BlockSpec((s, LANES), lambda h, i: (0, h))
    return pl.pallas_call(
        body, grid=(nh, nq), in_specs=[head_rows, head_all, head_all], out_specs=[head_rows, head_rows],
        out_shape=[jax.ShapeDtypeStruct(q.shape, BF16), jax.ShapeDtypeStruct(q.shape, BF16)],
        compiler_params=_cparams(("parallel", "arbitrary")), name=name)(q, k, v)


def _flash_bwd(qb, k, v, do, scale, nh, n_feat, *, name):
    s = qb.shape[0]
    t = min(ATT_TILE, s)
    nq = s // t
    c = scale * LOG2E

    def body(q_ref, k_ref, v_ref, do_ref, dk_ref, dv_ref, dq_ref):
        j = pl.program_id(1)
        kb = k_ref[...]
        vb = v_ref[...]

        @pl.when(j == 0)
        def _():
            dq_ref[...] = jnp.zeros_like(dq_ref)

        def block(i, carry, masked):
            dk_acc, dv_acc = carry
            rows = pl.ds(pl.multiple_of(i * t, t), t)
            qblk = q_ref[rows, :]
            dob = do_ref[rows, :]
            st = lax.dot_general(kb, qblk, _NT, preferred_element_type=F32)
            if masked:
                key = lax.broadcasted_iota(jnp.int32, (t, t), 0)
                qry = lax.broadcasted_iota(jnp.int32, (t, t), 1)
                st = jnp.where(key <= qry, st, NEG)
            pt = jnp.exp2(st * c)
            dv_acc = dv_acc + jnp.dot(pt.astype(BF16), dob, preferred_element_type=F32)
            dpt = lax.dot_general(vb, dob, _NT, preferred_element_type=F32)
            dsb = (pt * dpt).astype(BF16)
            dk_acc = dk_acc + jnp.dot(dsb, qblk, preferred_element_type=F32)
            dq_ref[rows, :] += lax.dot_general(dsb, kb, _TN, preferred_element_type=F32)
            return dk_acc, dv_acc

        init = (jnp.zeros((t, LANES), F32), jnp.zeros((t, LANES), F32))
        carry = block(j, init, True)
        rest = nq - 1 - j
        carry = lax.cond((rest & 1) == 1, lambda cr: block(j + 1, cr, False), lambda cr: cr, carry)
        first = j + 1 + (rest & 1)

        def pair(ii, cr):
            i0 = first + 2 * ii
            return block(i0 + 1, block(i0, cr, False), False)

        dk_acc, dv_acc = lax.fori_loop(0, rest >> 1, pair, carry)
        dk_ref[...] = dk_acc * jnp.where(_lane((t, LANES)) < n_feat, scale, 1.0)
        dv_ref[...] = dv_acc

        @pl.when(j == nq - 1)
        def _():
            dq_ref[...] = dq_ref[...] * jnp.where(_lane((s, LANES)) < n_feat, scale, 1.0)

    head_rows = pl.BlockSpec((t, LANES), lambda h, j: (j, h))
    head_all = pl.BlockSpec((s, LANES), lambda h, j: (0, h))
    shp = jax.ShapeDtypeStruct(qb.shape, F32)
    return pl.pallas_call(
        body, grid=(nh, nq), in_specs=[head_all, head_rows, head_rows, head_all],
        out_specs=[head_rows, head_rows, head_all], out_shape=[shp, shp, shp],
        compiler_params=_cparams(("parallel", "arbitrary")), name=name)(qb, k, v, do)


def _with_delta(do, o, nh, *, name):
    def fn(ti, pa):
        lane = _lane((ti[0].shape[0], LANES))
        out = []
        for d_blk, o_blk in zip(_blocks(ti[0], nh), _blocks(ti[1], nh), strict=True):
            delta = jnp.sum(d_blk * o_blk.astype(F32), axis=1, keepdims=True)
            out.append(_put3(d_blk, lane, V_ONES, _split3(-delta)))
        return [jnp.concatenate(out, axis=1)], []

    (res,) = _rowwise(name, fn, [do, o], [], [(do.shape[1], BF16)])
    return res


def _pad_heads(a, nh):
    d = a.shape[-1] // nh
    a = a.reshape(a.shape[:-1] + (nh, d))
    a = jnp.pad(a, [(0, 0)] * (a.ndim - 1) + [(0, LANES - d)])
    return a.reshape(a.shape[:-2] + (nh * LANES,))


def _unpad_heads(a, nh, d):
    a = a.reshape(a.shape[:-1] + (nh, LANES))[..., :d]
    return a.reshape(a.shape[:-2] + (nh * d,))


def _pad_head_rows(w, nh):
    return _pad_heads(w.T, nh).T


def _unpad_head_rows(g, nh, d):
    return _unpad_heads(g.T, nh, d).T


class _ZLayout:
    def __init__(self, d):
        fw = FOX_HEADS * FOX_DIM
        self.src = {}
        off = 0
        for nm, w in (("cq", MLA_Q_LORA), ("ckv", MLA_KV_LORA), ("kr", MLA_ROPE), ("fq", fw), ("fk", fw),
                      ("fv", fw), ("fl", FOX_HEADS), ("ga", d), ("gb", d)):
            self.src[nm] = (off, w)
            off += w
        self.dst = {}
        off = 0
        for nm, w in (("fq", FOX_HEADS * LANES), ("fk", FOX_HEADS * LANES), ("fv", FOX_HEADS * LANES), ("ga", d),
                      ("gb", d), ("cq", MLA_Q_LORA), ("ckv", MLA_KV_LORA), ("kr", LANES), ("fl", LANES)):
            assert off % w == 0
            self.dst[nm] = (off, w)
            off += w
        self.width = off

    def win(self, z, nm):
        off, w = self.dst[nm]
        return (z, off // w, w)

    def to_kernel(self, w):
        def seg(nm):
            off, wd = self.src[nm]
            return w[..., off:off + wd]

        def pad(a, left, total):
            return jnp.pad(a, [(0, 0)] * (a.ndim - 1) + [(left, total - left - a.shape[-1])])

        return jnp.concatenate([_pad_heads(seg("fq"), FOX_HEADS), _pad_heads(seg("fk"), FOX_HEADS),
                                _pad_heads(seg("fv"), FOX_HEADS), seg("ga"), seg("gb"), seg("cq"), seg("ckv"),
                                pad(seg("kr"), MLA_NOPE, LANES), pad(seg("fl"), 0, LANES)], axis=-1)

    def from_kernel(self, g):
        def seg(nm, lo=0, hi=None):
            off, wd = self.dst[nm]
            return g[..., off + lo:off + (wd if hi is None else hi)]

        return jnp.concatenate([seg("cq"), seg("ckv"), seg("kr", MLA_NOPE, MLA_NOPE + MLA_ROPE),
                                _unpad_heads(seg("fq"), FOX_HEADS, FOX_DIM), _unpad_heads(seg("fk"), FOX_HEADS, FOX_DIM),
                                _unpad_heads(seg("fv"), FOX_HEADS, FOX_DIM), seg("fl", 0, FOX_HEADS), seg("ga"),
                                seg("gb")], axis=-1)


def _local_step(x, positions, target, wts):
    s, d = x.shape
    zl = _ZLayout(d)
    hw = MLA_HEADS * LANES
    assert MLA_HEADS == FOX_HEADS
    scale_mla = (MLA_NOPE + MLA_ROPE) ** -0.5
    scale_fox = FOX_DIM ** -0.5

    inv_freq = ROPE_THETA ** (-jnp.arange(HALF_ROPE, dtype=F32) / HALF_ROPE)
    ang = positions.astype(F32)[:, None] * inv_freq
    cos, sin = jnp.cos(ang), jnp.sin(ang)
    tail = jnp.zeros((s, LANES - MLA_NOPE - MLA_ROPE), F32)
    rc = jnp.concatenate([jnp.ones((s, MLA_NOPE), F32), cos, cos, tail], axis=1)
    ra = jnp.concatenate([jnp.zeros((s, MLA_NOPE), F32), -sin, jnp.zeros((s, HALF_ROPE), F32), tail], axis=1)
    rb = jnp.concatenate([jnp.zeros((s, MLA_NOPE + HALF_ROPE), F32), sin, tail], axis=1)

    w_in = zl.to_kernel(wts["w_in"])
    b_in = zl.to_kernel(wts["b_in"])
    w_uq = _pad_heads(wts["w_uq"], MLA_HEADS)
    w_ukv = jnp.concatenate([_pad_heads(wts["w_uk"], MLA_HEADS), _pad_heads(wts["w_uv"], MLA_HEADS)], axis=1)
    w_o_mla = _pad_head_rows(wts["w_o_mla"], MLA_HEADS)
    w_o_fox = _pad_head_rows(wts["w_o_fox"], FOX_HEADS)

    def f_norm_in(ti, pa):
        y, _ = _rms(ti[0], pa[0])
        return [y], []

    (h,) = _rowwise("norm_in", f_norm_in, [x], [wts["ln_pre_mix"]], [(d, BF16)])
    z = _mm(h, w_in, bias=b_in, out_dtype=F32, name="proj_in")

    def f_mla_norms(ti, pa):
        cqn, _ = _rms(ti[0], pa[0])
        ckvn, _ = _rms(ti[1], pa[1])
        return [cqn, ckvn], []

    cqn, ckvn = _rowwise("mla_norms", f_mla_norms, [zl.win(z, "cq"), zl.win(z, "ckv")],
                         [wts["q_a_norm"], wts["kv_a_norm"]], [(MLA_Q_LORA, BF16), (MLA_KV_LORA, BF16)])
    qf = _mm(cqn, w_uq, out_dtype=F32, name="proj_uq")
    kv = _mm(ckvn, w_ukv, out_dtype=BF16, name="proj_ukv")

    def f_rope_q(ti, pa):
        xq, c_, a_, b_ = ti
        return [jnp.concatenate([_rope_block(blk, c_, a_, b_) for blk in _blocks(xq, MLA_HEADS)], axis=1)], []

    (q_mla,) = _rowwise("rope_q", f_rope_q, [qf, rc, ra, rb], [], [(hw, BF16)])

    def f_mla_kv(ti, pa):
        kn, vn, kr, c_, a_, b_ = ti
        lane = _lane(kr.shape)
        k_tail = jnp.where((lane >= MLA_Q_L) & (lane < MLA_Q_L + 3), 1.0, _rope_block(kr, c_, a_, b_))
        ones_v = (lane >= V_ONES) & (lane < V_ONES + 3)
        k_out = [jnp.where(lane < MLA_NOPE, blk.astype(F32), k_tail) for blk in _blocks(kn, MLA_HEADS)]
        v_out = [jnp.where(ones_v, 1.0, blk.astype(F32)) for blk in _blocks(vn, MLA_HEADS)]
        return [jnp.concatenate(k_out, axis=1), jnp.concatenate(v_out, axis=1)], []

    k_mla, v_mla = _rowwise("mla_kv", f_mla_kv, [(kv, 0, hw), (kv, 1, hw), zl.win(z, "kr"), rc, ra, rb], [],
                            [(hw, BF16), (hw, BF16)])
    o_mla, qb_mla = _flash_fwd(q_mla, k_mla, v_mla, scale_mla, MLA_HEADS, MLA_Q_L, name="mla_fwd")

    fl_cb = zl.dst["fl"][0] // LANES
    fcol = _forget_cumsum(z, fl_cb)

    def f_fox_qkv(ti, pa):
        fq, fk, fv, fc = ti
        lane = _lane(fc.shape)
        ones_q = (lane >= FOX_Q_ONES) & (lane < FOX_Q_ONES + 3)
        ones_k = (lane >= FOX_Q_F) & (lane < FOX_Q_ONES)
        ones_v = (lane >= V_ONES) & (lane < V_ONES + 3)
        q_out, k_out, v_out = [], [], []
        for hh, (qblk, kblk, vblk) in enumerate(zip(_blocks(fq, FOX_HEADS), _blocks(fk, FOX_HEADS),
                                                    _blocks(fv, FOX_HEADS), strict=True)):
            f_h = _lane_column(fc, lane, hh) * (1.0 / scale_fox)
            q_out.append(_put3(jnp.where(ones_q, 1.0, qblk), lane, FOX_Q_F, _split3(f_h)))
            k_out.append(_put3(jnp.where(ones_k, 1.0, kblk), lane, FOX_Q_ONES, _split3(-f_h)))
            v_out.append(jnp.where(ones_v, 1.0, vblk))
        return [jnp.concatenate(q_out, axis=1), jnp.concatenate(k_out, axis=1), jnp.concatenate(v_out, axis=1)], []

    q_fox, k_fox, v_fox = _rowwise("fox_qkv", f_fox_qkv, [zl.win(z, "fq"), zl.win(z, "fk"), zl.win(z, "fv"), fcol],
                                   [], [(hw, BF16)] * 3)
    o_fox, qb_fox = _flash_fwd(q_fox, k_fox, v_fox, scale_fox, FOX_HEADS, FOX_Q_L, name="fox_fwd")

    y_mla = _mm(o_mla, w_o_mla, out_dtype=F32, name="proj_o_mla")
    y_fox = _mm(o_fox, w_o_fox, out_dtype=F32, name="proj_o_fox")

    def f_gate(ti, pa):
        ga, gb, ya, yb = ti
        return [_sigmoid(ga) * ya + _sigmoid(gb) * yb], []

    (merged,) = _rowwise("gate", f_gate, [zl.win(z, "ga"), zl.win(z, "gb"), y_mla, y_fox], [], [(d, BF16)])
    mix = _mm(merged, wts["w_out"], out_dtype=F32, name="proj_out")

    def f_resid1(ti, pa):
        xa, mx = ti
        y, _ = _rms(mx, pa[0])
        x1 = xa + y
        h2, _ = _rms(x1, pa[1])
        return [x1, h2], []

    x1, h2 = _rowwise("resid_mix", f_resid1, [x, mix], [wts["ln_post_mix"], wts["ln_pre_mlp"]], [(d, F32), (d, BF16)])

    def relu2(acc, ex):
        r = jnp.maximum(acc, 0.0)
        return [acc, r * r]

    u, act = _mm(h2, wts["w_ff1"], out_dtype=[F32, BF16], name="ff1", epilogue=relu2)
    mo = _mm(act, wts["w_ff2"], out_dtype=F32, name="ff2")

    def f_loss(ti, pa):
        xa, mv, tg = ti
        y, _ = _rms(mv, pa[0])
        err = (xa + y) - tg
        g2 = err / d
        dmo, dg = _rms_bwd(mv, pa[0], g2)
        return [g2, dmo], [jnp.sum(err * err, axis=0, keepdims=True), dg]

    g2, d_mo, loss_cols, g_ln_post_mlp = _rowwise("loss", f_loss, [x1, mo, target], [wts["ln_post_mlp"]],
                                                  [(d, F32), (d, BF16)], [d, d])
    loss = 0.5 * jnp.sum(loss_cols) / d

    grads = {"ln_post_mlp": g_ln_post_mlp}
    grads["w_ff2"] = _mm_tn(act, d_mo, name="grad_ff2")

    def relu2_bwd(acc, ex):
        return [acc * (2.0 * jnp.maximum(ex[0], 0.0))]

    (d_u,) = _mm(d_mo, wts["w_ff2"], out_dtype=[BF16], name="ff2_bwd", transpose_b=True, extras=[u], epilogue=relu2_bwd)
    grads["w_ff1"] = _mm_tn(h2, d_u, name="grad_ff1")
    d_h2 = _mm(d_u, wts["w_ff1"], out_dtype=F32, name="ff1_bwd", transpose_b=True)

    def f_resid1_bwd(ti, pa):
        gres, dh2, x1v, mx = ti
        dx1n, dg_pre_mlp = _rms_bwd(x1v, pa[1], dh2)
        dx1 = gres + dx1n
        dmix, dg_post_mix = _rms_bwd(mx, pa[0], dx1)
        return [dx1, dmix], [dg_post_mix, dg_pre_mlp]

    d_x1, d_mix, grads["ln_post_mix"], grads["ln_pre_mlp"] = _rowwise(
        "resid_mix_bwd", f_resid1_bwd, [g2, d_h2, x1, mix], [wts["ln_post_mix"], wts["ln_pre_mlp"]],
        [(d, F32), (d, BF16)], [d, d])
    grads["w_out"] = _mm_tn(merged, d_mix, name="grad_out")
    d_merged = _mm(d_mix, wts["w_out"], out_dtype=F32, name="proj_out_bwd", transpose_b=True)

    def f_gate_bwd(ti, pa):
        dm, ga, gb, ya, yb = ti
        sa, sb = _sigmoid(ga), _sigmoid(gb)
        return [dm * sa, dm * sb, dm * ya * (sa * (1.0 - sa)), dm * yb * (sb * (1.0 - sb))], []

    d_ya, d_yb, d_ga, d_gb = _rowwise("gate_bwd", f_gate_bwd,
                                      [d_merged, zl.win(z, "ga"), zl.win(z, "gb"), y_mla, y_fox], [],
                                      [(d, BF16)] * 4)
    grads["w_o_mla"] = _unpad_head_rows(_mm_tn(o_mla, d_ya, name="grad_o_mla"), MLA_HEADS, MLA_V)
    grads["w_o_fox"] = _unpad_head_rows(_mm_tn(o_fox, d_yb, name="grad_o_fox"), FOX_HEADS, FOX_DIM)
    do_mla = _with_delta(_mm(d_ya, w_o_mla, out_dtype=F32, name="proj_o_mla_bwd", transpose_b=True), o_mla,
                         MLA_HEADS, name="mla_delta")
    do_fox = _with_delta(_mm(d_yb, w_o_fox, out_dtype=F32, name="proj_o_fox_bwd", transpose_b=True), o_fox,
                         FOX_HEADS, name="fox_delta")

    dk_mla, dv_mla, dq_mla = _flash_bwd(qb_mla, k_mla, v_mla, do_mla, scale_mla, MLA_HEADS, MLA_NOPE + MLA_ROPE,
                                        name="mla_bwd")
    dk_fox, dv_fox, dq_fox = _flash_bwd(qb_fox, k_fox, v_fox, do_fox, scale_fox, FOX_HEADS, FOX_DIM, name="fox_bwd")
    d_fl = _forget_cumsum_bwd(dq_fox, dk_fox, z, fl_cb, FOX_HEADS)

    def f_rope_q_bwd(ti, pa):
        g, c_, a_, b_ = ti
        return [jnp.concatenate([_rope_block(blk, c_, -a_, -b_) for blk in _blocks(g, MLA_HEADS)], axis=1)], []

    (d_qf,) = _rowwise("rope_q_bwd", f_rope_q_bwd, [dq_mla, rc, ra, rb], [], [(hw, BF16)])
    grads["w_uq"] = _unpad_heads(_mm_tn(cqn, d_qf, name="grad_uq"), MLA_HEADS, MLA_NOPE + MLA_ROPE)
    d_cqn = _mm(d_qf, w_uq, out_dtype=F32, name="proj_uq_bwd", transpose_b=True)

    def f_mla_kv_bwd(ti, pa):
        gk, gv, c_, a_, b_ = ti
        k_blocks = _blocks(gk, MLA_HEADS)
        tot = k_blocks[0]
        for blk in k_blocks[1:]:
            tot = tot + blk
        return [jnp.concatenate([gk, gv], axis=1), _rope_block(tot, c_, -a_, -b_)], []

    d_kv, d_kr = _rowwise("mla_kv_bwd", f_mla_kv_bwd, [dk_mla, dv_mla, rc, ra, rb], [], [(2 * hw, BF16), (LANES, BF16)])
    g_ukv = _mm_tn(ckvn, d_kv, name="grad_ukv")
    grads["w_uk"] = _unpad_heads(g_ukv[:, :hw], MLA_HEADS, MLA_NOPE)
    grads["w_uv"] = _unpad_heads(g_ukv[:, hw:], MLA_HEADS, MLA_V)
    d_ckvn = _mm(d_kv, w_ukv, out_dtype=F32, name="proj_ukv_bwd", transpose_b=True)

    def f_mla_norms_bwd(ti, pa):
        cq, ckv, dcqn, dckvn = ti
        dcq, dg_q = _rms_bwd(cq, pa[0], dcqn)
        dckv, dg_kv = _rms_bwd(ckv, pa[1], dckvn)
        return [dcq, dckv], [dg_q, dg_kv]

    d_cq, d_ckv, grads["q_a_norm"], grads["kv_a_norm"] = _rowwise(
        "mla_norms_bwd", f_mla_norms_bwd, [zl.win(z, "cq"), zl.win(z, "ckv"), d_cqn, d_ckvn],
        [wts["q_a_norm"], wts["kv_a_norm"]], [(MLA_Q_LORA, BF16), (MLA_KV_LORA, BF16)], [MLA_Q_LORA, MLA_KV_LORA])

    d_z = jnp.concatenate([dq_fox.astype(BF16), dk_fox.astype(BF16), dv_fox.astype(BF16), d_ga, d_gb, d_cq, d_ckv,
                           d_kr, d_fl], axis=1)
    assert d_z.shape[1] == zl.width

    def f_bias(ti, pa):
        return [], [jnp.sum(ti[0].astype(F32), axis=0, keepdims=True)]

    (g_b_in,) = _rowwise("grad_b_in", f_bias, [d_z], [], [], [zl.width])
    grads["b_in"] = zl.from_kernel(g_b_in)
    grads["w_in"] = zl.from_kernel(_mm_tn(h, d_z, name="grad_in"))
    d_h = _mm(d_z, w_in, out_dtype=F32, name="proj_in_bwd", transpose_b=True)

    def f_norm_in_bwd(ti, pa):
        dx1v, dh, xa = ti
        dxn, dg = _rms_bwd(xa, pa[0], dh)
        return [dx1v + dxn], [dg]

    grad_x, grads["ln_pre_mix"] = _rowwise("norm_in_bwd", f_norm_in_bwd, [d_x1, d_h, x], [wts["ln_pre_mix"]],
                                           [(d, F32)], [d])
    return loss, grad_x, grads


class _PackLayout:
    def __init__(self, shapes):
        self.shapes = list(shapes)
        self.width = _round_up(max(b for _, b in shapes), LANES)
        self.bands = []
        row = 0
        shelf = []
        for idx, (a, b) in enumerate(shapes):
            if 2 * _round_up(b, LANES) > self.width:
                self.bands.append((row, _round_up(a, 32), [(idx, 0)]))
                row += _round_up(a, 32)
            else:
                shelf.append(idx)
        col, items = 0, []
        for idx in shelf:
            wb = _round_up(shapes[idx][1], LANES)
            if col + wb > self.width:
                hgt = max(_round_up(shapes[i][0], 32) for i, _ in items)
                self.bands.append((row, hgt, items))
                row += hgt
                col, items = 0, []
            items.append((idx, col))
            col += wb
        if items:
            hgt = max(_round_up(shapes[i][0], 32) for i, _ in items)
            self.bands.append((row, hgt, items))
            row += hgt
        self.rows = _round_up(row, 16 * LOCAL_PIECES)

    def pack(self, arrs):
        bands = []
        for _, hgt, items in self.bands:
            parts = []
            for k, (idx, col) in enumerate(items):
                a, b = self.shapes[idx]
                nxt = items[k + 1][1] if k + 1 < len(items) else self.width
                parts.append(jnp.pad(arrs[idx], ((0, hgt - a), (0, nxt - col - b))))
            bands.append(parts[0] if len(parts) == 1 else jnp.concatenate(parts, axis=1))
        used = sum(hgt for _, hgt, _ in self.bands)
        if used < self.rows:
            bands.append(jnp.zeros((self.rows - used, self.width), arrs[0].dtype))
        return jnp.concatenate(bands, axis=0)

    def unpack(self, packed):
        out = [None] * len(self.shapes)
        for row, _, items in self.bands:
            for idx, col in items:
                a, b = self.shapes[idx]
                out[idx] = packed[row:row + a, col:col + b]
        return out


ANY = pl.BlockSpec(memory_space=pl.ANY)


def _place():
    return lax.axis_index("x"), lax.axis_index("y"), lax.axis_index("c")


def _gather_weights(wpk):
    rows, wd = wpk.shape
    half = rows // 2

    def body(w_ref, out_ref, send_sems, recv_sems, local_sems):
        x, y, c = _place()
        sibling = (x, y, 1 - c)
        chips = [(1 - x, y), (x, 1 - y), (1 - x, 1 - y)]

        def slab(chip, hf):
            return out_ref.at[2 * chip[0] + chip[1], pl.ds(hf * half, half), :]

        def copy(k, chip, hf, to, src=None):
            return pltpu.make_async_remote_copy(
                src_ref=slab(chip, hf) if src is None else src, dst_ref=slab(chip, hf),
                send_sem=send_sems.at[k], recv_sem=recv_sems.at[k], device_id=to, device_id_type=MESH)

        piece = rows // LOCAL_PIECES
        mine = [pltpu.make_async_copy(w_ref.at[pl.ds(n * piece, piece), :],
                                      out_ref.at[2 * x + y, pl.ds(n * piece, piece), :], local_sems.at[n])
                for n in range(LOCAL_PIECES)]
        for cp in mine:
            cp.start()
        first = [copy(j, (x, y), c, (*chip, c), src=w_ref.at[pl.ds(c * half, half), :]) for j, chip in enumerate(chips)]
        for cp in first:
            cp.start()
        passed = [copy(3 + j, chip, c, sibling) for j, chip in enumerate(chips)]
        for j, chip in enumerate(chips):
            copy(j, chip, c, (x, y, c)).wait_recv()
            passed[j].start()
        for j, chip in enumerate(chips):
            copy(3 + j, chip, 1 - c, (x, y, c)).wait_recv()
        for cp in first + passed:
            cp.wait_send()
        for cp in mine:
            cp.wait()

    assert rows % (16 * LOCAL_PIECES) == 0
    return pl.pallas_call(
        body, out_shape=jax.ShapeDtypeStruct((N_CHIPS, rows, wd), wpk.dtype),
        in_specs=[ANY], out_specs=ANY,
        scratch_shapes=[pltpu.SemaphoreType.DMA((6,)), pltpu.SemaphoreType.DMA((6,)),
                        pltpu.SemaphoreType.DMA((LOCAL_PIECES,))],
        name="gather_weights")(wpk)


def _sibling_exchange(g):
    n, rows, wd = g.shape
    half = rows // 2

    def body(g_ref, out_ref, send_sem, recv_sem):
        x, y, c = _place()
        cp = pltpu.make_async_remote_copy(
            src_ref=g_ref.at[:, pl.ds((1 - c) * half, half), :], dst_ref=out_ref,
            send_sem=send_sem, recv_sem=recv_sem, device_id=(x, y, 1 - c), device_id_type=MESH)
        cp.start()
        cp.wait()

    return pl.pallas_call(
        body, out_shape=jax.ShapeDtypeStruct((n, half, wd), g.dtype), in_specs=[ANY], out_specs=ANY,
        scratch_shapes=[pltpu.SemaphoreType.DMA, pltpu.SemaphoreType.DMA], name="grad_sibling_exchange")(g)


def _chip_exchange(part):
    n, half, wd = part.shape

    def body(p_ref, out_ref, send_sems, recv_sems):
        x, y, c = _place()
        chips = [(1 - x, y), (x, 1 - y), (1 - x, 1 - y)]

        def copy(j, chip):
            return pltpu.make_async_remote_copy(
                src_ref=p_ref.at[2 * chip[0] + chip[1]], dst_ref=out_ref.at[j],
                send_sem=send_sems.at[j], recv_sem=recv_sems.at[j], device_id=(*chip, c), device_id_type=MESH)

        cps = [copy(j, chip) for j, chip in enumerate(chips)]
        for cp in cps:
            cp.start()
        for cp in cps:
            cp.wait()

    return pl.pallas_call(
        body, out_shape=jax.ShapeDtypeStruct((3, half, wd), part.dtype), in_specs=[ANY], out_specs=ANY,
        scratch_shapes=[pltpu.SemaphoreType.DMA((3,)), pltpu.SemaphoreType.DMA((3,))], name="grad_chip_exchange")(part)


def _sibling_swap(mine):
    def body(m_ref, out_ref, send_sem, recv_sem):
        x, y, c = _place()
        cp = pltpu.make_async_remote_copy(
            src_ref=m_ref, dst_ref=out_ref, send_sem=send_sem, recv_sem=recv_sem,
            device_id=(x, y, 1 - c), device_id_type=MESH)
        cp.start()
        cp.wait()

    return pl.pallas_call(
        body, out_shape=jax.ShapeDtypeStruct(mine.shape, mine.dtype), in_specs=[ANY], out_specs=ANY,
        scratch_shapes=[pltpu.SemaphoreType.DMA, pltpu.SemaphoreType.DMA], name="grad_sibling_swap")(mine)


def _adamw(w, g, m, v):
    m = ADAM_B1 * m + (1.0 - ADAM_B1) * g
    v = ADAM_B2 * v + (1.0 - ADAM_B2) * (g * g)
    m_hat = m / (1.0 - ADAM_B1 ** ADAM_STEP)
    v_hat = v / (1.0 - ADAM_B2 ** ADAM_STEP)
    delta = -ADAM_LR * (m_hat / (jnp.sqrt(v_hat) + ADAM_EPS) + ADAM_WD * w)
    return delta, m, v


def _small_allreduce_adamw(gs, ws, ms, vs):
    n_dev = 8
    n_par = len(gs)
    wd = PACK_W
    chunks = []
    for p, g in enumerate(gs):
        for off in range(0, g.shape[1], wd):
            chunks.append((p, len(chunks), off, min(wd, g.shape[1] - off)))
    rows = _round_up(len(chunks), 8)

    def body(*refs):
        g_refs, w_refs, m_refs, v_refs = (refs[k * n_par:(k + 1) * n_par] for k in range(4))
        go_refs, d_refs, mo_refs, vo_refs = (refs[(4 + k) * n_par:(5 + k) * n_par] for k in range(4))
        mine_ref, all_ref, send_sems, recv_sems = refs[8 * n_par:]
        x, y, c = _place()
        me, sibling = (x, y, c), (x, y, 1 - c)
        chips = [(1 - x, y), (x, 1 - y), (1 - x, 1 - y)]

        def slot(px, py, pc):
            return all_ref.at[4 * px + 2 * py + pc]

        def copy(k, block, to, src=None):
            return pltpu.make_async_remote_copy(
                src_ref=slot(*block) if src is None else src, dst_ref=slot(*block),
                send_sem=send_sems.at[k], recv_sem=recv_sems.at[k], device_id=to, device_id_type=MESH)

        mine_ref[...] = jnp.zeros_like(mine_ref)
        for p, row, off, width in chunks:
            mine_ref[row:row + 1, 0:width] = g_refs[p][:, off:off + width]
        all_ref[4 * x + 2 * y + c] = mine_ref[...]
        first = [copy(0, me, sibling, src=mine_ref)]
        first += [copy(1 + j, me, (*chip, c), src=mine_ref) for j, chip in enumerate(chips)]
        for cp in first:
            cp.start()
        passed = [copy(4 + j, (*chip, c), sibling) for j, chip in enumerate(chips)]
        for j, chip in enumerate(chips):
            copy(1 + j, (*chip, c), me).wait_recv()
            passed[j].start()
        copy(0, sibling, me).wait_recv()
        for j, chip in enumerate(chips):
            copy(4 + j, (*chip, 1 - c), me).wait_recv()
        for cp in first + passed:
            cp.wait_send()
        tot = jnp.zeros((rows, wd), F32)
        for dev in range(n_dev):
            tot = tot + all_ref[dev]
        mine_ref[...] = tot
        for p, row, off, width in chunks:
            cols = slice(off, off + width)
            g = mine_ref[row:row + 1, 0:width]
            delta, m_new, v_new = _adamw(w_refs[p][:, cols], g, m_refs[p][:, cols], v_refs[p][:, cols])
            go_refs[p][:, cols] = g
            d_refs[p][:, cols] = delta
            mo_refs[p][:, cols] = m_new
            vo_refs[p][:, cols] = v_new

    vm = pl.BlockSpec(memory_space=pltpu.VMEM)
    shp = [jax.ShapeDtypeStruct(g.shape, F32) for g in gs]
    res = pl.pallas_call(
        body, out_shape=shp * 4, in_specs=[vm] * (4 * n_par), out_specs=[vm] * (4 * n_par),
        scratch_shapes=[pltpu.VMEM((rows, wd), F32), pltpu.VMEM((n_dev, rows, wd), F32),
                        pltpu.SemaphoreType.DMA((7,)), pltpu.SemaphoreType.DMA((7,))],
        name="small_allreduce_adamw")(*gs, *ws, *ms, *vs)
    return [res[k * n_par:(k + 1) * n_par] for k in range(4)]


def kernel(x, positions, ln_pre_mix, ln_post_mix, ln_pre_mlp, ln_post_mlp, w_in, b_in, q_a_norm, w_uq, kv_a_norm, w_uk, w_uv, w_o_mla, w_o_fox, w_out, w_ff1, w_ff2, loss_target, m_ln_pre_mix, m_ln_post_mix, m_ln_pre_mlp, m_ln_post_mlp, m_w_in, m_b_in, m_q_a_norm, m_w_uq, m_kv_a_norm, m_w_uk, m_w_uv, m_w_o_mla, m_w_o_fox, m_w_out, m_w_ff1, m_w_ff2, v_ln_pre_mix, v_ln_post_mix, v_ln_pre_mlp, v_ln_post_mlp, v_w_in, v_b_in, v_q_a_norm, v_w_uq, v_kv_a_norm, v_w_uk, v_w_uv, v_w_o_mla, v_w_o_fox, v_w_out, v_w_ff1, v_w_ff2):
    w = dict(ln_pre_mix=ln_pre_mix, ln_post_mix=ln_post_mix, ln_pre_mlp=ln_pre_mlp, ln_post_mlp=ln_post_mlp, w_in=w_in,
             b_in=b_in, q_a_norm=q_a_norm, w_uq=w_uq, kv_a_norm=kv_a_norm, w_uk=w_uk, w_uv=w_uv, w_o_mla=w_o_mla,
             w_o_fox=w_o_fox, w_out=w_out, w_ff1=w_ff1, w_ff2=w_ff2)
    mom = dict(ln_pre_mix=m_ln_pre_mix, ln_post_mix=m_ln_post_mix, ln_pre_mlp=m_ln_pre_mlp, ln_post_mlp=m_ln_post_mlp,
               w_in=m_w_in, b_in=m_b_in, q_a_norm=m_q_a_norm, w_uq=m_w_uq, kv_a_norm=m_kv_a_norm, w_uk=m_w_uk,
               w_uv=m_w_uv, w_o_mla=m_w_o_mla, w_o_fox=m_w_o_fox, w_out=m_w_out, w_ff1=m_w_ff1, w_ff2=m_w_ff2)
    var = dict(ln_pre_mix=v_ln_pre_mix, ln_post_mix=v_ln_post_mix, ln_pre_mlp=v_ln_pre_mlp, ln_post_mlp=v_ln_post_mlp,
               w_in=v_w_in, b_in=v_b_in, q_a_norm=v_q_a_norm, w_uq=v_w_uq, kv_a_norm=v_kv_a_norm, w_uk=v_w_uk,
               w_uv=v_w_uv, w_o_mla=v_w_o_mla, w_o_fox=v_w_o_fox, w_out=v_w_out, w_ff1=v_w_ff1, w_ff2=v_w_ff2)

    big_names = [nm for nm, _ in BIG]
    layout = _PackLayout([w[nm].shape[1:] for nm in big_names])
    rows, pack_w = layout.rows, layout.width
    half = rows // 2

    gathered = _gather_weights(layout.pack([w[nm][0].astype(BF16) for nm in big_names]))
    per_chip = [layout.unpack(gathered[ch]) for ch in range(N_CHIPS)]
    full = {nm: wv for nm, wv in w.items() if nm in SMALL}
    for k, (nm, axis) in enumerate(BIG):
        full[nm] = jnp.concatenate([per_chip[ch][k] for ch in range(N_CHIPS)], axis=axis - 1)

    loss_local, grad_x, grads = _local_step(x[0], positions[0], loss_target[0], full)
    loss = lax.psum(loss_local, ("x", "y", "c"))

    gpk = jnp.stack([
        layout.pack([jnp.split(grads[nm], N_CHIPS, axis=axis - 1)[ch] for nm, axis in BIG]) for ch in range(N_CHIPS)])
    c = lax.axis_index("c")
    chip = 2 * lax.axis_index("x") + lax.axis_index("y")
    theirs = _sibling_exchange(gpk)
    mine_half = lax.dynamic_slice_in_dim(gpk, c * half, half, axis=1)

    def f_add2(ti, pa):
        tot = ti[0] + ti[1]
        return [tot, tot], []

    chip_part, chip_part_bf = _rowwise("grad_add_sibling", f_add2, [mine_half.reshape(N_CHIPS * half, pack_w),
                                                                  theirs.reshape(N_CHIPS * half, pack_w)], [],
                                       [(pack_w, F32), (pack_w, BF16)])
    others = _chip_exchange(chip_part_bf.reshape(N_CHIPS, half, pack_w))
    own = lax.dynamic_index_in_dim(chip_part.reshape(N_CHIPS, half, pack_w), chip, axis=0, keepdims=False)

    def f_add4(ti, pa):
        return [((ti[0] + ti[1].astype(F32)) + ti[2].astype(F32)) + ti[3].astype(F32)], []

    (red_half,) = _rowwise("grad_add_chips", f_add4, [own, others[0], others[1], others[2]], [], [(pack_w, F32)])
    sib_half = _sibling_swap(red_half)
    g_shards = layout.unpack(jnp.concatenate([jnp.where(c == 0, red_half, sib_half),
                                              jnp.where(c == 0, sib_half, red_half)], axis=0))

    def f_adamw(ti, pa):
        wv, gv, mv, vv = ti
        return list(_adamw(wv, gv, mv, vv)), []

    out = {"grad": {}, "delta": {}, "m": {}, "v": {}}
    for nm, g_sh in zip(big_names, g_shards, strict=True):
        wd = g_sh.shape[1]
        d_sh, m_sh, v_sh = _rowwise("adamw_" + nm, f_adamw, [w[nm][0], g_sh, mom[nm][0], var[nm][0]], [], [(wd, F32)] * 3)
        out["grad"][nm], out["delta"][nm], out["m"][nm], out["v"][nm] = g_sh[None], d_sh[None], m_sh[None], v_sh[None]

    small = _small_allreduce_adamw([grads[nm] for nm in SMALL], [w[nm] for nm in SMALL],
                                   [mom[nm] for nm in SMALL], [var[nm] for nm in SMALL])
    for kind, arrs in zip(("grad", "delta", "m", "v"), small, strict=True):
        for nm, arr in zip(SMALL, arrs, strict=True):
            out[kind][nm] = arr

    return (loss, grad_x[None], *[out["grad"][nm] for nm in ALL_W], *[out["delta"][nm] for nm in ALL_W],
            *[out["m"][nm] for nm in ALL_W], *[out["v"][nm] for nm in ALL_W])
```

```python
import functools
import math

import jax
import jax.numpy as jnp
from jax import lax
from jax.experimental import pallas as pl
from jax.experimental.pallas import tpu as pltpu

F32 = jnp.float32
BF16 = jnp.bfloat16

MLA_HEADS = 8
MLA_Q_LORA = 256
MLA_KV_LORA = 128
MLA_NOPE = 64
MLA_ROPE = 32
MLA_V = 64
FOX_HEADS = 8
FOX_DIM = 64
ROPE_THETA = 10000.0
NORM_EPS = 1e-6
HALF_ROPE = MLA_ROPE // 2

ADAM_LR = 0.001
ADAM_B1 = 0.9
ADAM_B2 = 0.999
ADAM_EPS = 1e-08
ADAM_WD = 0.01
ADAM_STEP = 10

LANES = 128
VMEM_LIMIT = 56 * 1024 * 1024
ATT_TILE = 1024
FWD_GROUP_LOG2 = 1
FWD_GROUP = 1 << FWD_GROUP_LOG2
MM_VMEM_BUDGET = 40 * 1024 * 1024
MXU_WIDTH = 256
MXU_MACS_PER_S = 4.98e14
HBM_BYTES_PER_S = 3.2e12
STEP_OVERHEAD_S = 0.35e-6
NEG = -1e30
LOG2E = math.log2(math.e)
MESH = pl.DeviceIdType.MESH

V_ONES = 64
FOX_Q_F = 64
FOX_Q_L = 67
FOX_Q_ONES = 70
MLA_Q_L = 96

BIG = (("w_in", 2), ("w_uq", 2), ("w_uk", 2), ("w_uv", 2), ("w_o_mla", 2), ("w_o_fox", 2),
       ("w_out", 1), ("w_ff1", 2), ("w_ff2", 1))
SMALL = ("ln_pre_mix", "ln_post_mix", "ln_pre_mlp", "ln_post_mlp", "b_in", "q_a_norm", "kv_a_norm")
ALL_W = ("ln_pre_mix", "ln_post_mix", "ln_pre_mlp", "ln_post_mlp", "w_in", "b_in", "q_a_norm", "w_uq",
         "kv_a_norm", "w_uk", "w_uv", "w_o_mla", "w_o_fox", "w_out", "w_ff1", "w_ff2")
N_CHIPS = 4
PACK_W = 1024
LOCAL_PIECES = 8

_NT = (((1,), (1,)), ((), ()))
_TN = (((0,), (0,)), ((), ()))


def _cparams(sem=None):
    return pltpu.CompilerParams(dimension_semantics=sem, vmem_limit_bytes=VMEM_LIMIT)


def _divisor_tile(n, limit, mult):
    if n <= limit:
        return n
    best = None
    t = mult
    while t <= limit:
        if n % t == 0:
            best = t
        t += mult
    assert best is not None, (n, limit, mult)
    return best


def _round_up(v, mult):
    return -(-v // mult) * mult


def _mm_tiles(m, k, n, io_bytes):
    best = None
    for tm in (2048, 1024, 512, 256, 128):
        if m % tm:
            continue
        for tn in range(LANES, min(n, 2048) + 1, LANES):
            if n % tn:
                continue
            vmem = 2 * (tm * k * 2 + k * tn * 2 + tm * tn * io_bytes) + tm * tn * 4
            if vmem > MM_VMEM_BUDGET:
                continue
            mxu = m * k * n * (_round_up(tn, MXU_WIDTH) / tn) / MXU_MACS_PER_S
            hbm = (m * k * 2 + (m // tm) * k * n * 2 + m * n * io_bytes) / HBM_BYTES_PER_S
            cost = max(mxu, hbm) + (m // tm) * (n // tn) * STEP_OVERHEAD_S
            if best is None or cost < best[0]:
                best = (cost, tm, tn)
    assert best is not None, (m, k, n)
    return best[1], best[2]


def _mm(a, b, *, out_dtype, name, bias=None, transpose_b=False, extras=(), epilogue=None):
    m, k = a.shape
    n = b.shape[0] if transpose_b else b.shape[1]
    assert (b.shape[1] if transpose_b else b.shape[0]) == k and a.dtype == BF16 and b.dtype == BF16
    out_dtypes = list(out_dtype) if isinstance(out_dtype, (list, tuple)) else [out_dtype]
    n_ex = len(extras)
    tm, tn = _mm_tiles(m, k, n, sum(jnp.dtype(dt).itemsize for dt in out_dtypes) + 4 * n_ex)

    def body(*refs):
        a_ref, b_ref = refs[:2]
        pos = 2
        bias_ref = None
        if bias is not None:
            bias_ref = refs[pos]
            pos += 1
        ex_refs = refs[pos:pos + n_ex]
        o_refs = refs[pos + n_ex:]
        if transpose_b:
            acc = lax.dot_general(a_ref[...], b_ref[...], _NT, preferred_element_type=F32)
        else:
            acc = jnp.dot(a_ref[...], b_ref[...], preferred_element_type=F32)
        if bias_ref is not None:
            acc = acc + bias_ref[...]
        vals = [acc] if epilogue is None else epilogue(acc, [r[...] for r in ex_refs])
        for ref, val in zip(o_refs, vals, strict=True):
            ref[...] = val.astype(ref.dtype)

    b_spec = pl.BlockSpec((tn, k), lambda i, j: (j, 0)) if transpose_b else pl.BlockSpec((k, tn), lambda i, j: (0, j))
    in_specs = [pl.BlockSpec((tm, k), lambda i, j: (i, 0)), b_spec]
    args = [a, b]
    if bias is not None:
        in_specs.append(pl.BlockSpec((1, tn), lambda i, j: (0, j)))
        args.append(bias)
    in_specs += [pl.BlockSpec((tm, tn), lambda i, j: (i, j)) for _ in extras]
    args += list(extras)
    res = pl.pallas_call(
        body, grid=(m // tm, n // tn), in_specs=in_specs,
        out_specs=[pl.BlockSpec((tm, tn), lambda i, j: (i, j)) for _ in out_dtypes],
        out_shape=[jax.ShapeDtypeStruct((m, n), dt) for dt in out_dtypes],
        compiler_params=_cparams(("parallel", "parallel")), name=name)(*args)
    return res if isinstance(out_dtype, (list, tuple)) else res[0]


def _mm_tn(a, b, *, name):
    s, m = a.shape
    s2, n = b.shape
    assert s == s2 and a.dtype == BF16 and b.dtype == BF16
    tm = _divisor_tile(m, 1024, LANES)
    tn = _divisor_tile(n, 2304, LANES)
    tk = _divisor_tile(s, 512, 16)

    def body(a_ref, b_ref, o_ref):
        @pl.when(pl.program_id(2) == 0)
        def _():
            o_ref[...] = jnp.zeros_like(o_ref)

        o_ref[...] += lax.dot_general(a_ref[...], b_ref[...], _TN, preferred_element_type=F32)

    return pl.pallas_call(
        body, grid=(m // tm, n // tn, s // tk),
        in_specs=[pl.BlockSpec((tk, tm), lambda i, j, k: (k, i)), pl.BlockSpec((tk, tn), lambda i, j, k: (k, j))],
        out_specs=pl.BlockSpec((tm, tn), lambda i, j, k: (i, j)),
        out_shape=jax.ShapeDtypeStruct((m, n), F32),
        compiler_params=_cparams(("parallel", "parallel", "arbitrary")), name=name)(a, b)


def _rowwise(name, fn, tiled, params, outs, reds=(), reverse=False):
    wins = [t if isinstance(t, tuple) else (t, 0, t.shape[1]) for t in tiled]
    s = wins[0][0].shape[0]
    row_bytes = sum(w * arr.dtype.itemsize for arr, _, w in wins) + sum(w * jnp.dtype(d).itemsize for w, d in outs)
    ts = _divisor_tile(s, max(16, min(1024, (6 * 1024 * 1024) // row_bytes)), 16)
    nt, npar, nout = len(wins), len(params), len(outs)
    n_tiles = s // ts

    def row(i):
        return n_tiles - 1 - i if reverse else i

    def body(*refs):
        tin = [r[...] for r in refs[:nt]]
        par = [r[...] for r in refs[nt:nt + npar]]
        out_refs = refs[nt + npar:nt + npar + nout]
        red_refs = refs[nt + npar + nout:]
        o, r = fn(tin, par)
        for ref, val in zip(out_refs, o, strict=True):
            ref[...] = val.astype(ref.dtype)
        if red_refs:
            @pl.when(pl.program_id(0) == 0)
            def _():
                for ref in red_refs:
                    ref[...] = jnp.zeros_like(ref)

            for ref, val in zip(red_refs, r, strict=True):
                ref[...] += val

    in_specs = [pl.BlockSpec((ts, w), functools.partial(lambda i, cb: (row(i), cb), cb=cb)) for _, cb, w in wins]
    in_specs += [pl.BlockSpec(p.shape, lambda i: (0, 0)) for p in params]
    out_specs = [pl.BlockSpec((ts, w), lambda i: (row(i), 0)) for w, _ in outs]
    out_specs += [pl.BlockSpec((1, w), lambda i: (0, 0)) for w in reds]
    out_shape = [jax.ShapeDtypeStruct((s, w), d) for w, d in outs]
    out_shape += [jax.ShapeDtypeStruct((1, w), F32) for w in reds]
    return pl.pallas_call(
        body, grid=(n_tiles,), in_specs=in_specs, out_specs=out_specs, out_shape=out_shape,
        compiler_params=_cparams(("arbitrary",)), name=name)(*[w[0] for w in wins], *params)


def _rms(x, g):
    r = lax.rsqrt(jnp.mean(x * x, axis=-1, keepdims=True) + NORM_EPS)
    return x * r * g, r


def _rms_bwd(x, g, dy):
    r = lax.rsqrt(jnp.mean(x * x, axis=-1, keepdims=True) + NORM_EPS)
    gy = dy * g
    dx = r * gy - x * (r * r * r) * jnp.mean(x * gy, axis=-1, keepdims=True)
    dg = jnp.sum(dy * (x * r), axis=0, keepdims=True)
    return dx, dg


def _sigmoid(x):
    return 1.0 / (1.0 + jnp.exp(-x))


def _split3(x):
    hi = x.astype(BF16).astype(F32)
    r = x - hi
    mid = r.astype(BF16).astype(F32)
    lo = (r - mid).astype(BF16).astype(F32)
    return hi, mid, lo


def _lane(shape):
    return lax.broadcasted_iota(jnp.int32, shape, 1)


def _put3(blk, lane, pos, pieces):
    for k, piece in enumerate(pieces):
        blk = jnp.where(lane == pos + k, piece, blk)
    return blk


def _lane_column(blk, lane, pos):
    return jnp.sum(jnp.where(lane == pos, blk, 0.0), axis=1, keepdims=True)


def _blocks(a, nh):
    return [a[:, h * LANES:(h + 1) * LANES] for h in range(nh)]


def _rope_block(x, c, sa, sb):
    return x * c + pltpu.roll(x, LANES - HALF_ROPE, 1) * sa + pltpu.roll(x, HALF_ROPE, 1) * sb


def _forget_cumsum(z, cb):
    s = z.shape[0]
    ts = _divisor_tile(s, 512, LANES)

    def body(x_ref, col_ref, carry):
        @pl.when(pl.program_id(0) == 0)
        def _():
            carry[...] = jnp.zeros_like(carry)

        x = x_ref[...]
        lf = jnp.minimum(x, 0.0) - jnp.log1p(jnp.exp(-jnp.abs(x)))
        r = lax.broadcasted_iota(jnp.int32, (ts, ts), 0)
        c = lax.broadcasted_iota(jnp.int32, (ts, ts), 1)
        tri = jnp.where(c <= r, 1.0, 0.0).astype(F32)
        col_ref[...] = jnp.dot(tri, lf, preferred_element_type=F32, precision=lax.Precision.HIGHEST) + carry[...]
        carry[...] += jnp.sum(lf, axis=0, keepdims=True)

    return pl.pallas_call(
        body, grid=(s // ts,),
        in_specs=[pl.BlockSpec((ts, LANES), lambda i: (i, cb))],
        out_specs=pl.BlockSpec((ts, LANES), lambda i: (i, 0)),
        out_shape=jax.ShapeDtypeStruct((s, LANES), F32),
        scratch_shapes=[pltpu.VMEM((1, LANES), F32)],
        compiler_params=_cparams(("arbitrary",)), name="forget_cumsum")(z)


def _forget_cumsum_bwd(dq, dk, z, cb, nh):
    s = z.shape[0]
    ts = _divisor_tile(s, 512, LANES)
    nt = s // ts
    wd = nh * LANES

    def body(dq_ref, dk_ref, x_ref, o_ref, carry):
        @pl.when(pl.program_id(0) == 0)
        def _():
            carry[...] = jnp.zeros_like(carry)

        lane = _lane((ts, LANES))
        df = jnp.zeros((ts, LANES), F32)
        for h in range(nh):
            cols = slice(h * LANES, (h + 1) * LANES)
            d_h = _lane_column(dq_ref[:, cols], lane, FOX_Q_F) - _lane_column(dk_ref[:, cols], lane, FOX_Q_ONES)
            df = jnp.where(lane == h, d_h, df)
        r = lax.broadcasted_iota(jnp.int32, (ts, ts), 0)
        c = lax.broadcasted_iota(jnp.int32, (ts, ts), 1)
        tri = jnp.where(c >= r, 1.0, 0.0).astype(F32)
        rc = jnp.dot(tri, df, preferred_element_type=F32, precision=lax.Precision.HIGHEST) + carry[...]
        carry[...] += jnp.sum(df, axis=0, keepdims=True)
        o_ref[...] = (rc * (1.0 / (1.0 + jnp.exp(x_ref[...])))).astype(o_ref.dtype)

    return pl.pallas_call(
        body, grid=(nt,),
        in_specs=[pl.BlockSpec((ts, wd), lambda i: (nt - 1 - i, 0)),
                  pl.BlockSpec((ts, wd), lambda i: (nt - 1 - i, 0)),
                  pl.BlockSpec((ts, LANES), lambda i: (nt - 1 - i, cb))],
        out_specs=pl.BlockSpec((ts, LANES), lambda i: (nt - 1 - i, 0)),
        out_shape=jax.ShapeDtypeStruct((s, LANES), BF16),
        scratch_shapes=[pltpu.VMEM((1, LANES), F32)],
        compiler_params=_cparams(("arbitrary",)), name="forget_cumsum_bwd")(dq, dk, z)


def _flash_fwd(q, k, v, scale, nh, l_lane, *, name):
    s = q.shape[0]
    t = min(ATT_TILE, s)
    nq = s // t
    c = scale * LOG2E

    def body(q_ref, k_ref, v_ref, o_ref, qb_ref):
        i = pl.program_id(1)
        qb = q_ref[...]

        def scores(j):
            kb = k_ref[pl.ds(pl.multiple_of(j * t, t), t), :]
            return lax.dot_general(qb, kb, _NT, preferred_element_type=F32)

        def update(j, sc, carry):
            m, acc = carry
            m_new = jnp.maximum(m, jnp.max(sc, axis=1, keepdims=True))
            p = jnp.exp2((sc - m_new) * c)
            alpha = jnp.exp2((m - m_new) * c)
            vb = v_ref[pl.ds(pl.multiple_of(j * t, t), t), :]
            acc = alpha * acc + jnp.dot(p.astype(BF16), vb, preferred_element_type=F32)
            return m_new, acc

        def group(jj, cr):
            for n in range(FWD_GROUP):
                cr = update(FWD_GROUP * jj + n, scores(FWD_GROUP * jj + n), cr)
            return cr

        init = (jnp.full((t, 1), NEG, F32), jnp.zeros((t, LANES), F32))
        n_groups = i >> FWD_GROUP_LOG2
        carry = lax.fori_loop(0, n_groups, group, init)
        carry = lax.fori_loop(n_groups * FWD_GROUP, i, lambda j, cr: update(j, scores(j), cr), carry)
        row = lax.broadcasted_iota(jnp.int32, (t, t), 0)
        col = lax.broadcasted_iota(jnp.int32, (t, t), 1)
        m, acc = update(i, jnp.where(col <= row, scores(i), NEG), carry)
        lane = _lane((t, LANES))
        l = _lane_column(acc, lane, V_ONES)
        o_ref[...] = (acc / l).astype(o_ref.dtype)
        big_l = m + jnp.log(l) / scale
        qb_ref[...] = _put3(qb.astype(F32), lane, l_lane, _split3(-big_l)).astype(qb_ref.dtype)

    head_rows = pl.BlockSpec((t, LANES), lambda h, i: (i, h))
    head_all = pl.BlockSpec((s, LANES), lambda h, i: (0, h))
    return pl.pallas_call(
        body, grid=(nh, nq), in_specs=[head_rows, head_all, head_all], out_specs=[head_rows, head_rows],
        out_shape=[jax.ShapeDtypeStruct(q.shape, BF16), jax.ShapeDtypeStruct(q.shape, BF16)],
        compiler_params=_cparams(("parallel", "arbitrary")), name=name)(q, k, v)


def _flash_bwd(qb, k, v, do, scale, nh, n_feat, *, name):
    s = qb.shape[0]
    t = min(ATT_TILE, s)
    nq = s // t
    c = scale * LOG2E

    def body(q_ref, k_ref, v_ref, do_ref, dk_ref, dv_ref, dq_ref):
        j = pl.program_id(1)
        kb = k_ref[...]
        vb = v_ref[...]

        @pl.when(j == 0)
        def _():
            dq_ref[...] = jnp.zeros_like(dq_ref)

        def block(i, carry, masked):
            dk_acc, dv_acc = carry
            rows = pl.ds(pl.multiple_of(i * t, t), t)
            qblk = q_ref[rows, :]
            dob = do_ref[rows, :]
            st = lax.dot_general(kb, qblk, _NT, preferred_element_type=F32)
            if masked:
                key = lax.broadcasted_iota(jnp.int32, (t, t), 0)
                qry = lax.broadcasted_iota(jnp.int32, (t, t), 1)
                st = jnp.where(key <= qry, st, NEG)
            pt = jnp.exp2(st * c)
            dv_acc = dv_acc + jnp.dot(pt.astype(BF16), dob, preferred_element_type=F32)
            dpt = lax.dot_general(vb, dob, _NT, preferred_element_type=F32)
            dsb = (pt * dpt).astype(BF16)
            dk_acc = dk_acc + jnp.dot(dsb, qblk, preferred_element_type=F32)
            dq_ref[rows, :] += lax.dot_general(dsb, kb, _TN, preferred_element_type=F32)
            return dk_acc, dv_acc

        init = (jnp.zeros((t, LANES), F32), jnp.zeros((t, LANES), F32))
        carry = block(j, init, True)
        rest = nq - 1 - j
        carry = lax.cond((rest & 1) == 1, lambda cr: block(j + 1, cr, False), lambda cr: cr, carry)
        first = j + 1 + (rest & 1)

        def pair(ii, cr):
            i0 = first + 2 * ii
            return block(i0 + 1, block(i0, cr, False), False)

        dk_acc, dv_acc = lax.fori_loop(0, rest >> 1, pair, carry)
        dk_ref[...] = dk_acc * jnp.where(_lane((t, LANES)) < n_feat, scale, 1.0)
        dv_ref[...] = dv_acc

        @pl.when(j == nq - 1)
        def _():
            dq_ref[...] = dq_ref[...] * jnp.where(_lane((s, LANES)) < n_feat, scale, 1.0)

    head_rows = pl.BlockSpec((t, LANES), lambda h, j: (j, h))
    head_all = pl.BlockSpec((s, LANES), lambda h, j: (0, h))
    shp = jax.ShapeDtypeStruct(qb.shape, F32)
    return pl.pallas_call(
        body, grid=(nh, nq), in_specs=[head_all, head_rows, head_rows, head_all],
        out_specs=[head_rows, head_rows, head_all], out_shape=[shp, shp, shp],
        compiler_params=_cparams(("parallel", "arbitrary")), name=name)(qb, k, v, do)


def _with_delta(do, o, nh, *, name):
    def fn(ti, pa):
        lane = _lane((ti[0].shape[0], LANES))
        out = []
        for d_blk, o_blk in zip(_blocks(ti[0], nh), _blocks(ti[1], nh), strict=True):
            delta = jnp.sum(d_blk * o_blk.astype(F32), axis=1, keepdims=True)
            out.append(_put3(d_blk, lane, V_ONES, _split3(-delta)))
        return [jnp.concatenate(out, axis=1)], []

    (res,) = _rowwise(name, fn, [do, o], [], [(do.shape[1], BF16)])
    return res


def _pad_heads(a, nh):
    d = a.shape[-1] // nh
    a = a.reshape(a.shape[:-1] + (nh, d))
    a = jnp.pad(a, [(0, 0)] * (a.ndim - 1) + [(0, LANES - d)])
    return a.reshape(a.shape[:-2] + (nh * LANES,))


def _unpad_heads(a, nh, d):
    a = a.reshape(a.shape[:-1] + (nh, LANES))[..., :d]
    return a.reshape(a.shape[:-2] + (nh * d,))


def _pad_head_rows(w, nh):
    return _pad_heads(w.T, nh).T


def _unpad_head_rows(g, nh, d):
    return _unpad_heads(g.T, nh, d).T


class _ZLayout:
    def __init__(self, d):
        fw = FOX_HEADS * FOX_DIM
        self.src = {}
        off = 0
        for nm, w in (("cq", MLA_Q_LORA), ("ckv", MLA_KV_LORA), ("kr", MLA_ROPE), ("fq", fw), ("fk", fw),
                      ("fv", fw), ("fl", FOX_HEADS), ("ga", d), ("gb", d)):
            self.src[nm] = (off, w)
            off += w
        self.dst = {}
        off = 0
        for nm, w in (("fq", FOX_HEADS * LANES), ("fk", FOX_HEADS * LANES), ("fv", FOX_HEADS * LANES), ("ga", d),
                      ("gb", d), ("cq", MLA_Q_LORA), ("ckv", MLA_KV_LORA), ("kr", LANES), ("fl", LANES)):
            assert off % w == 0
            self.dst[nm] = (off, w)
            off += w
        self.width = off

    def win(self, z, nm):
        off, w = self.dst[nm]
        return (z, off // w, w)

    def to_kernel(self, w):
        def seg(nm):
            off, wd = self.src[nm]
            return w[..., off:off + wd]

        def pad(a, left, total):
            return jnp.pad(a, [(0, 0)] * (a.ndim - 1) + [(left, total - left - a.shape[-1])])

        return jnp.concatenate([_pad_heads(seg("fq"), FOX_HEADS), _pad_heads(seg("fk"), FOX_HEADS),
                                _pad_heads(seg("fv"), FOX_HEADS), seg("ga"), seg("gb"), seg("cq"), seg("ckv"),
                                pad(seg("kr"), MLA_NOPE, LANES), pad(seg("fl"), 0, LANES)], axis=-1)

    def from_kernel(self, g):
        def seg(nm, lo=0, hi=None):
            off, wd = self.dst[nm]
            return g[..., off + lo:off + (wd if hi is None else hi)]

        return jnp.concatenate([seg("cq"), seg("ckv"), seg("kr", MLA_NOPE, MLA_NOPE + MLA_ROPE),
                                _unpad_heads(seg("fq"), FOX_HEADS, FOX_DIM), _unpad_heads(seg("fk"), FOX_HEADS, FOX_DIM),
                                _unpad_heads(seg("fv"), FOX_HEADS, FOX_DIM), seg("fl", 0, FOX_HEADS), seg("ga"),
                                seg("gb")], axis=-1)


def _local_step(x, positions, target, wts):
    s, d = x.shape
    zl = _ZLayout(d)
    hw = MLA_HEADS * LANES
    assert MLA_HEADS == FOX_HEADS
    scale_mla = (MLA_NOPE + MLA_ROPE) ** -0.5
    scale_fox = FOX_DIM ** -0.5

    inv_freq = ROPE_THETA ** (-jnp.arange(HALF_ROPE, dtype=F32) / HALF_ROPE)
    ang = positions.astype(F32)[:, None] * inv_freq
    cos, sin = jnp.cos(ang), jnp.sin(ang)
    tail = jnp.zeros((s, LANES - MLA_NOPE - MLA_ROPE), F32)
    rc = jnp.concatenate([jnp.ones((s, MLA_NOPE), F32), cos, cos, tail], axis=1)
    ra = jnp.concatenate([jnp.zeros((s, MLA_NOPE), F32), -sin, jnp.zeros((s, HALF_ROPE), F32), tail], axis=1)
    rb = jnp.concatenate([jnp.zeros((s, MLA_NOPE + HALF_ROPE), F32), sin, tail], axis=1)

    w_in = zl.to_kernel(wts["w_in"])
    b_in = zl.to_kernel(wts["b_in"])
    w_uq = _pad_heads(wts["w_uq"], MLA_HEADS)
    w_ukv = jnp.concatenate([_pad_heads(wts["w_uk"], MLA_HEADS), _pad_heads(wts["w_uv"], MLA_HEADS)], axis=1)
    w_o_mla = _pad_head_rows(wts["w_o_mla"], MLA_HEADS)
    w_o_fox = _pad_head_rows(wts["w_o_fox"], FOX_HEADS)

    def f_norm_in(ti, pa):
        y, _ = _rms(ti[0], pa[0])
        return [y], []

    (h,) = _rowwise("norm_in", f_norm_in, [x], [wts["ln_pre_mix"]], [(d, BF16)])
    z = _mm(h, w_in, bias=b_in, out_dtype=F32, name="proj_in")

    def f_mla_norms(ti, pa):
        cqn, _ = _rms(ti[0], pa[0])
        ckvn, _ = _rms(ti[1], pa[1])
        return [cqn, ckvn], []

    cqn, ckvn = _rowwise("mla_norms", f_mla_norms, [zl.win(z, "cq"), zl.win(z, "ckv")],
                         [wts["q_a_norm"], wts["kv_a_norm"]], [(MLA_Q_LORA, BF16), (MLA_KV_LORA, BF16)])
    qf = _mm(cqn, w_uq, out_dtype=F32, name="proj_uq")
    kv = _mm(ckvn, w_ukv, out_dtype=BF16, name="proj_ukv")

    def f_rope_q(ti, pa):
        xq, c_, a_, b_ = ti
        return [jnp.concatenate([_rope_block(blk, c_, a_, b_) for blk in _blocks(xq, MLA_HEADS)], axis=1)], []

    (q_mla,) = _rowwise("rope_q", f_rope_q, [qf, rc, ra, rb], [], [(hw, BF16)])

    def f_mla_kv(ti, pa):
        kn, vn, kr, c_, a_, b_ = ti
        lane = _lane(kr.shape)
        k_tail = jnp.where((lane >= MLA_Q_L) & (lane < MLA_Q_L + 3), 1.0, _rope_block(kr, c_, a_, b_))
        ones_v = (lane >= V_ONES) & (lane < V_ONES + 3)
        k_out = [jnp.where(lane < MLA_NOPE, blk.astype(F32), k_tail) for blk in _blocks(kn, MLA_HEADS)]
        v_out = [jnp.where(ones_v, 1.0, blk.astype(F32)) for blk in _blocks(vn, MLA_HEADS)]
        return [jnp.concatenate(k_out, axis=1), jnp.concatenate(v_out, axis=1)], []

    k_mla, v_mla = _rowwise("mla_kv", f_mla_kv, [(kv, 0, hw), (kv, 1, hw), zl.win(z, "kr"), rc, ra, rb], [],
                            [(hw, BF16), (hw, BF16)])
    o_mla, qb_mla = _flash_fwd(q_mla, k_mla, v_mla, scale_mla, MLA_HEADS, MLA_Q_L, name="mla_fwd")

    fl_cb = zl.dst["fl"][0] // LANES
    fcol = _forget_cumsum(z, fl_cb)

    def f_fox_qkv(ti, pa):
        fq, fk, fv, fc = ti
        lane = _lane(fc.shape)
        ones_q = (lane >= FOX_Q_ONES) & (lane < FOX_Q_ONES + 3)
        ones_k = (lane >= FOX_Q_F) & (lane < FOX_Q_ONES)
        ones_v = (lane >= V_ONES) & (lane < V_ONES + 3)
        q_out, k_out, v_out = [], [], []
        for hh, (qblk, kblk, vblk) in enumerate(zip(_blocks(fq, FOX_HEADS), _blocks(fk, FOX_HEADS),
                                                    _blocks(fv, FOX_HEADS), strict=True)):
            f_h = _lane_column(fc, lane, hh) * (1.0 / scale_fox)
            q_out.append(_put3(jnp.where(ones_q, 1.0, qblk), lane, FOX_Q_F, _split3(f_h)))
            k_out.append(_put3(jnp.where(ones_k, 1.0, kblk), lane, FOX_Q_ONES, _split3(-f_h)))
            v_out.append(jnp.where(ones_v, 1.0, vblk))
        return [jnp.concatenate(q_out, axis=1), jnp.concatenate(k_out, axis=1), jnp.concatenate(v_out, axis=1)], []

    q_fox, k_fox, v_fox = _rowwise("fox_qkv", f_fox_qkv, [zl.win(z, "fq"), zl.win(z, "fk"), zl.win(z, "fv"), fcol],
                                   [], [(hw, BF16)] * 3)
    o_fox, qb_fox = _flash_fwd(q_fox, k_fox, v_fox, scale_fox, FOX_HEADS, FOX_Q_L, name="fox_fwd")

    y_mla = _mm(o_mla, w_o_mla, out_dtype=F32, name="proj_o_mla")
    y_fox = _mm(o_fox, w_o_fox, out_dtype=F32, name="proj_o_fox")

    def f_gate(ti, pa):
        ga, gb, ya, yb = ti
        return [_sigmoid(ga) * ya + _sigmoid(gb) * yb], []

    (merged,) = _rowwise("gate", f_gate, [zl.win(z, "ga"), zl.win(z, "gb"), y_mla, y_fox], [], [(d, BF16)])
    mix = _mm(merged, wts["w_out"], out_dtype=F32, name="proj_out")

    def f_resid1(ti, pa):
        xa, mx = ti
        y, _ = _rms(mx, pa[0])
        x1 = xa + y
        h2, _ = _rms(x1, pa[1])
        return [x1, h2], []

    x1, h2 = _rowwise("resid_mix", f_resid1, [x, mix], [wts["ln_post_mix"], wts["ln_pre_mlp"]], [(d, F32), (d, BF16)])

    def relu2(acc, ex):
        r = jnp.maximum(acc, 0.0)
        return [acc, r * r]

    u, act = _mm(h2, wts["w_ff1"], out_dtype=[F32, BF16], name="ff1", epilogue=relu2)
    mo = _mm(act, wts["w_ff2"], out_dtype=F32, name="ff2")

    def f_loss(ti, pa):
        xa, mv, tg = ti
        y, _ = _rms(mv, pa[0])
        err = (xa + y) - tg
        g2 = err / d
        dmo, dg = _rms_bwd(mv, pa[0], g2)
        return [g2, dmo], [jnp.sum(err * err, axis=0, keepdims=True), dg]

    g2, d_mo, loss_cols, g_ln_post_mlp = _rowwise("loss", f_loss, [x1, mo, target], [wts["ln_post_mlp"]],
                                                  [(d, F32), (d, BF16)], [d, d])
    loss = 0.5 * jnp.sum(loss_cols) / d

    grads = {"ln_post_mlp": g_ln_post_mlp}
    grads["w_ff2"] = _mm_tn(act, d_mo, name="grad_ff2")

    def relu2_bwd(acc, ex):
        return [acc * (2.0 * jnp.maximum(ex[0], 0.0))]

    (d_u,) = _mm(d_mo, wts["w_ff2"], out_dtype=[BF16], name="ff2_bwd", transpose_b=True, extras=[u], epilogue=relu2_bwd)
    grads["w_ff1"] = _mm_tn(h2, d_u, name="grad_ff1")
    d_h2 = _mm(d_u, wts["w_ff1"], out_dtype=F32, name="ff1_bwd", transpose_b=True)

    def f_resid1_bwd(ti, pa):
        gres, dh2, x1v, mx = ti
        dx1n, dg_pre_mlp = _rms_bwd(x1v, pa[1], dh2)
        dx1 = gres + dx1n
        dmix, dg_post_mix = _rms_bwd(mx, pa[0], dx1)
        return [dx1, dmix], [dg_post_mix, dg_pre_mlp]

    d_x1, d_mix, grads["ln_post_mix"], grads["ln_pre_mlp"] = _rowwise(
        "resid_mix_bwd", f_resid1_bwd, [g2, d_h2, x1, mix], [wts["ln_post_mix"], wts["ln_pre_mlp"]],
        [(d, F32), (d, BF16)], [d, d])
    grads["w_out"] = _mm_tn(merged, d_mix, name="grad_out")
    d_merged = _mm(d_mix, wts["w_out"], out_dtype=F32, name="proj_out_bwd", transpose_b=True)

    def f_gate_bwd(ti, pa):
        dm, ga, gb, ya, yb = ti
        sa, sb = _sigmoid(ga), _sigmoid(gb)
        return [dm * sa, dm * sb, dm * ya * (sa * (1.0 - sa)), dm * yb * (sb * (1.0 - sb))], []

    d_ya, d_yb, d_ga, d_gb = _rowwise("gate_bwd", f_gate_bwd,
                                      [d_merged, zl.win(z, "ga"), zl.win(z, "gb"), y_mla, y_fox], [],
                                      [(d, BF16)] * 4)
    grads["w_o_mla"] = _unpad_head_rows(_mm_tn(o_mla, d_ya, name="grad_o_mla"), MLA_HEADS, MLA_V)
    grads["w_o_fox"] = _unpad_head_rows(_mm_tn(o_fox, d_yb, name="grad_o_fox"), FOX_HEADS, FOX_DIM)
    do_mla = _with_delta(_mm(d_ya, w_o_mla, out_dtype=F32, name="proj_o_mla_bwd", transpose_b=True), o_mla,
                         MLA_HEADS, name="mla_delta")
    do_fox = _with_delta(_mm(d_yb, w_o_fox, out_dtype=F32, name="proj_o_fox_bwd", transpose_b=True), o_fox,
                         FOX_HEADS, name="fox_delta")

    dk_mla, dv_mla, dq_mla = _flash_bwd(qb_mla, k_mla, v_mla, do_mla, scale_mla, MLA_HEADS, MLA_NOPE + MLA_ROPE,
                                        name="mla_bwd")
    dk_fox, dv_fox, dq_fox = _flash_bwd(qb_fox, k_fox, v_fox, do_fox, scale_fox, FOX_HEADS, FOX_DIM, name="fox_bwd")
    d_fl = _forget_cumsum_bwd(dq_fox, dk_fox, z, fl_cb, FOX_HEADS)

    def f_rope_q_bwd(ti, pa):
        g, c_, a_, b_ = ti
        return [jnp.concatenate([_rope_block(blk, c_, -a_, -b_) for blk in _blocks(g, MLA_HEADS)], axis=1)], []

    (d_qf,) = _rowwise("rope_q_bwd", f_rope_q_bwd, [dq_mla, rc, ra, rb], [], [(hw, BF16)])
    grads["w_uq"] = _unpad_heads(_mm_tn(cqn, d_qf, name="grad_uq"), MLA_HEADS, MLA_NOPE + MLA_ROPE)
    d_cqn = _mm(d_qf, w_uq, out_dtype=F32, name="proj_uq_bwd", transpose_b=True)

    def f_mla_kv_bwd(ti, pa):
        gk, gv, c_, a_, b_ = ti
        k_blocks = _blocks(gk, MLA_HEADS)
        tot = k_blocks[0]
        for blk in k_blocks[1:]:
            tot = tot + blk
        return [jnp.concatenate([gk, gv], axis=1), _rope_block(tot, c_, -a_, -b_)], []

    d_kv, d_kr = _rowwise("mla_kv_bwd", f_mla_kv_bwd, [dk_mla, dv_mla, rc, ra, rb], [], [(2 * hw, BF16), (LANES, BF16)])
    g_ukv = _mm_tn(ckvn, d_kv, name="grad_ukv")
    grads["w_uk"] = _unpad_heads(g_ukv[:, :hw], MLA_HEADS, MLA_NOPE)
    grads["w_uv"] = _unpad_heads(g_ukv[:, hw:], MLA_HEADS, MLA_V)
    d_ckvn = _mm(d_kv, w_ukv, out_dtype=F32, name="proj_ukv_bwd", transpose_b=True)

    def f_mla_norms_bwd(ti, pa):
        cq, ckv, dcqn, dckvn = ti
        dcq, dg_q = _rms_bwd(cq, pa[0], dcqn)
        dckv, dg_kv = _rms_bwd(ckv, pa[1], dckvn)
        return [dcq, dckv], [dg_q, dg_kv]

    d_cq, d_ckv, grads["q_a_norm"], grads["kv_a_norm"] = _rowwise(
        "mla_norms_bwd", f_mla_norms_bwd, [zl.win(z, "cq"), zl.win(z, "ckv"), d_cqn, d_ckvn],
        [wts["q_a_norm"], wts["kv_a_norm"]], [(MLA_Q_LORA, BF16), (MLA_KV_LORA, BF16)], [MLA_Q_LORA, MLA_KV_LORA])

    d_z = jnp.concatenate([dq_fox.astype(BF16), dk_fox.astype(BF16), dv_fox.astype(BF16), d_ga, d_gb, d_cq, d_ckv,
                           d_kr, d_fl], axis=1)
    assert d_z.shape[1] == zl.width

    def f_bias(ti, pa):
        return [], [jnp.sum(ti[0].astype(F32), axis=0, keepdims=True)]

    (g_b_in,) = _rowwise("grad_b_in", f_bias, [d_z], [], [], [zl.width])
    grads["b_in"] = zl.from_kernel(g_b_in)
    grads["w_in"] = zl.from_kernel(_mm_tn(h, d_z, name="grad_in"))
    d_h = _mm(d_z, w_in, out_dtype=F32, name="proj_in_bwd", transpose_b=True)

    def f_norm_in_bwd(ti, pa):
        dx1v, dh, xa = ti
        dxn, dg = _rms_bwd(xa, pa[0], dh)
        return [dx1v + dxn], [dg]

    grad_x, grads["ln_pre_mix"] = _rowwise("norm_in_bwd", f_norm_in_bwd, [d_x1, d_h, x], [wts["ln_pre_mix"]],
                                           [(d, F32)], [d])
    return loss, grad_x, grads


class _PackLayout:
    def __init__(self, shapes):
        self.shapes = list(shapes)
        self.width = _round_up(max(b for _, b in shapes), LANES)
        self.bands = []
        row = 0
        shelf = []
        for idx, (a, b) in enumerate(shapes):
            if 2 * _round_up(b, LANES) > self.width:
                self.bands.append((row, _round_up(a, 32), [(idx, 0)]))
                row += _round_up(a, 32)
            else:
                shelf.append(idx)
        col, items = 0, []
        for idx in shelf:
            wb = _round_up(shapes[idx][1], LANES)
            if col + wb > self.width:
                hgt = max(_round_up(shapes[i][0], 32) for i, _ in items)
                self.bands.append((row, hgt, items))
                row += hgt
                col, items = 0, []
            items.append((idx, col))
            col += wb
        if items:
            hgt = max(_round_up(shapes[i][0], 32) for i, _ in items)
            self.bands.append((row, hgt, items))
            row += hgt
        self.rows = _round_up(row, 16 * LOCAL_PIECES)

    def pack(self, arrs):
        bands = []
        for _, hgt, items in self.bands:
            parts = []
            for k, (idx, col) in enumerate(items):
                a, b = self.shapes[idx]
                nxt = items[k + 1][1] if k + 1 < len(items) else self.width
                parts.append(jnp.pad(arrs[idx], ((0, hgt - a), (0, nxt - col - b))))
            bands.append(parts[0] if len(parts) == 1 else jnp.concatenate(parts, axis=1))
        used = sum(hgt for _, hgt, _ in self.bands)
        if used < self.rows:
            bands.append(jnp.zeros((self.rows - used, self.width), arrs[0].dtype))
        return jnp.concatenate(bands, axis=0)

    def unpack(self, packed):
        out = [None] * len(self.shapes)
        for row, _, items in self.bands:
            for idx, col in items:
                a, b = self.shapes[idx]
                out[idx] = packed[row:row + a, col:col + b]
        return out


ANY = pl.BlockSpec(memory_space=pl.ANY)


def _place():
    return lax.axis_index("x"), lax.axis_index("y"), lax.axis_index("c")


def _gather_weights(wpk):
    rows, wd = wpk.shape
    half = rows // 2

    def body(w_ref, out_ref, send_sems, recv_sems, local_sems):
        x, y, c = _place()
        sibling = (x, y, 1 - c)
        chips = [(1 - x, y), (x, 1 - y), (1 - x, 1 - y)]

        def slab(chip, hf):
            return out_ref.at[2 * chip[0] + chip[1], pl.ds(hf * half, half), :]

        def copy(k, chip, hf, to, src=None):
            return pltpu.make_async_remote_copy(
                src_ref=slab(chip, hf) if src is None else src, dst_ref=slab(chip, hf),
                send_sem=send_sems.at[k], recv_sem=recv_sems.at[k], device_id=to, device_id_type=MESH)

        piece = rows // LOCAL_PIECES
        mine = [pltpu.make_async_copy(w_ref.at[pl.ds(n * piece, piece), :],
                                      out_ref.at[2 * x + y, pl.ds(n * piece, piece), :], local_sems.at[n])
                for n in range(LOCAL_PIECES)]
        for cp in mine:
            cp.start()
        first = [copy(j, (x, y), c, (*chip, c), src=w_ref.at[pl.ds(c * half, half), :]) for j, chip in enumerate(chips)]
        for cp in first:
            cp.start()
        passed = [copy(3 + j, chip, c, sibling) for j, chip in enumerate(chips)]
        for j, chip in enumerate(chips):
            copy(j, chip, c, (x, y, c)).wait_recv()
            passed[j].start()
        for j, chip in enumerate(chips):
            copy(3 + j, chip, 1 - c, (x, y, c)).wait_recv()
        for cp in first + passed:
            cp.wait_send()
        for cp in mine:
            cp.wait()

    assert rows % (16 * LOCAL_PIECES) == 0
    return pl.pallas_call(
        body, out_shape=jax.ShapeDtypeStruct((N_CHIPS, rows, wd), wpk.dtype),
        in_specs=[ANY], out_specs=ANY,
        scratch_shapes=[pltpu.SemaphoreType.DMA((6,)), pltpu.SemaphoreType.DMA((6,)),
                        pltpu.SemaphoreType.DMA((LOCAL_PIECES,))],
        name="gather_weights")(wpk)


def _sibling_exchange(g):
    n, rows, wd = g.shape
    half = rows // 2

    def body(g_ref, out_ref, send_sem, recv_sem):
        x, y, c = _place()
        cp = pltpu.make_async_remote_copy(
            src_ref=g_ref.at[:, pl.ds((1 - c) * half, half), :], dst_ref=out_ref,
            send_sem=send_sem, recv_sem=recv_sem, device_id=(x, y, 1 - c), device_id_type=MESH)
        cp.start()
        cp.wait()

    return pl.pallas_call(
        body, out_shape=jax.ShapeDtypeStruct((n, half, wd), g.dtype), in_specs=[ANY], out_specs=ANY,
        scratch_shapes=[pltpu.SemaphoreType.DMA, pltpu.SemaphoreType.DMA], name="grad_sibling_exchange")(g)


def _chip_exchange(part):
    n, half, wd = part.shape

    def body(p_ref, out_ref, send_sems, recv_sems):
        x, y, c = _place()
        chips = [(1 - x, y), (x, 1 - y), (1 - x, 1 - y)]

        def copy(j, chip):
            return pltpu.make_async_remote_copy(
                src_ref=p_ref.at[2 * chip[0] + chip[1]], dst_ref=out_ref.at[j],
                send_sem=send_sems.at[j], recv_sem=recv_sems.at[j], device_id=(*chip, c), device_id_type=MESH)

        cps = [copy(j, chip) for j, chip in enumerate(chips)]
        for cp in cps:
            cp.start()
        for cp in cps:
            cp.wait()

    return pl.pallas_call(
        body, out_shape=jax.ShapeDtypeStruct((3, half, wd), part.dtype), in_specs=[ANY], out_specs=ANY,
        scratch_shapes=[pltpu.SemaphoreType.DMA((3,)), pltpu.SemaphoreType.DMA((3,))], name="grad_chip_exchange")(part)


def _sibling_swap(mine):
    def body(m_ref, out_ref, send_sem, recv_sem):
        x, y, c = _place()
        cp = pltpu.make_async_remote_copy(
            src_ref=m_ref, dst_ref=out_ref, send_sem=send_sem, recv_sem=recv_sem,
            device_id=(x, y, 1 - c), device_id_type=MESH)
        cp.start()
        cp.wait()

    return pl.pallas_call(
        body, out_shape=jax.ShapeDtypeStruct(mine.shape, mine.dtype), in_specs=[ANY], out_specs=ANY,
        scratch_shapes=[pltpu.SemaphoreType.DMA, pltpu.SemaphoreType.DMA], name="grad_sibling_swap")(mine)


def _adamw(w, g, m, v):
    m = ADAM_B1 * m + (1.0 - ADAM_B1) * g
    v = ADAM_B2 * v + (1.0 - ADAM_B2) * (g * g)
    m_hat = m / (1.0 - ADAM_B1 ** ADAM_STEP)
    v_hat = v / (1.0 - ADAM_B2 ** ADAM_STEP)
    delta = -ADAM_LR * (m_hat / (jnp.sqrt(v_hat) + ADAM_EPS) + ADAM_WD * w)
    return delta, m, v


def _small_allreduce_adamw(gs, ws, ms, vs):
    n_dev = 8
    n_par = len(gs)
    wd = PACK_W
    chunks = []
    for p, g in enumerate(gs):
        for off in range(0, g.shape[1], wd):
            chunks.append((p, len(chunks), off, min(wd, g.shape[1] - off)))
    rows = _round_up(len(chunks), 8)

    def body(*refs):
        g_refs, w_refs, m_refs, v_refs = (refs[k * n_par:(k + 1) * n_par] for k in range(4))
        go_refs, d_refs, mo_refs, vo_refs = (refs[(4 + k) * n_par:(5 + k) * n_par] for k in range(4))
        mine_ref, all_ref, send_sems, recv_sems = refs[8 * n_par:]
        x, y, c = _place()
        me, sibling = (x, y, c), (x, y, 1 - c)
        chips = [(1 - x, y), (x, 1 - y), (1 - x, 1 - y)]

        def slot(px, py, pc):
            return all_ref.at[4 * px + 2 * py + pc]

        def copy(k, block, to, src=None):
            return pltpu.make_async_remote_copy(
                src_ref=slot(*block) if src is None else src, dst_ref=slot(*block),
                send_sem=send_sems.at[k], recv_sem=recv_sems.at[k], device_id=to, device_id_type=MESH)

        mine_ref[...] = jnp.zeros_like(mine_ref)
        for p, row, off, width in chunks:
            mine_ref[row:row + 1, 0:width] = g_refs[p][:, off:off + width]
        all_ref[4 * x + 2 * y + c] = mine_ref[...]
        first = [copy(0, me, sibling, src=mine_ref)]
        first += [copy(1 + j, me, (*chip, c), src=mine_ref) for j, chip in enumerate(chips)]
        for cp in first:
            cp.start()
        passed = [copy(4 + j, (*chip, c), sibling) for j, chip in enumerate(chips)]
        for j, chip in enumerate(chips):
            copy(1 + j, (*chip, c), me).wait_recv()
            passed[j].start()
        copy(0, sibling, me).wait_recv()
        for j, chip in enumerate(chips):
            copy(4 + j, (*chip, 1 - c), me).wait_recv()
        for cp in first + passed:
            cp.wait_send()
        tot = jnp.zeros((rows, wd), F32)
        for dev in range(n_dev):
            tot = tot + all_ref[dev]
        mine_ref[...] = tot
        for p, row, off, width in chunks:
            cols = slice(off, off + width)
            g = mine_ref[row:row + 1, 0:width]
            delta, m_new, v_new = _adamw(w_refs[p][:, cols], g, m_refs[p][:, cols], v_refs[p][:, cols])
            go_refs[p][:, cols] = g
            d_refs[p][:, cols] = delta
            mo_refs[p][:, cols] = m_new
            vo_refs[p][:, cols] = v_new

    vm = pl.BlockSpec(memory_space=pltpu.VMEM)
    shp = [jax.ShapeDtypeStruct(g.shape, F32) for g in gs]
    res = pl.pallas_call(
        body, out_shape=shp * 4, in_specs=[vm] * (4 * n_par), out_specs=[vm] * (4 * n_par),
        scratch_shapes=[pltpu.VMEM((rows, wd), F32), pltpu.VMEM((n_dev, rows, wd), F32),
                        pltpu.SemaphoreType.DMA((7,)), pltpu.SemaphoreType.DMA((7,))],
        name="small_allreduce_adamw")(*gs, *ws, *ms, *vs)
    return [res[k * n_par:(k + 1) * n_par] for k in range(4)]


def kernel(x, positions, ln_pre_mix, ln_post_mix, ln_pre_mlp, ln_post_mlp, w_in, b_in, q_a_norm, w_uq, kv_a_norm, w_uk, w_uv, w_o_mla, w_o_fox, w_out, w_ff1, w_ff2, loss_target, m_ln_pre_mix, m_ln_post_mix, m_ln_pre_mlp, m_ln_post_mlp, m_w_in, m_b_in, m_q_a_norm, m_w_uq, m_kv_a_norm, m_w_uk, m_w_uv, m_w_o_mla, m_w_o_fox, m_w_out, m_w_ff1, m_w_ff2, v_ln_pre_mix, v_ln_post_mix, v_ln_pre_mlp, v_ln_post_mlp, v_w_in, v_b_in, v_q_a_norm, v_w_uq, v_kv_a_norm, v_w_uk, v_w_uv, v_w_o_mla, v_w_o_fox, v_w_out, v_w_ff1, v_w_ff2):
    w = dict(ln_pre_mix=ln_pre_mix, ln_post_mix=ln_post_mix, ln_pre_mlp=ln_pre_mlp, ln_post_mlp=ln_post_mlp, w_in=w_in,
             b_in=b_in, q_a_norm=q_a_norm, w_uq=w_uq, kv_a_norm=kv_a_norm, w_uk=w_uk, w_uv=w_uv, w_o_mla=w_o_mla,
             w_o_fox=w_o_fox, w_out=w_out, w_ff1=w_ff1, w_ff2=w_ff2)
    mom = dict(ln_pre_mix=m_ln_pre_mix, ln_post_mix=m_ln_post_mix, ln_pre_mlp=m_ln_pre_mlp, ln_post_mlp=m_ln_post_mlp,
               w_in=m_w_in, b_in=m_b_in, q_a_norm=m_q_a_norm, w_uq=m_w_uq, kv_a_norm=m_kv_a_norm, w_uk=m_w_uk,
               w_uv=m_w_uv, w_o_mla=m_w_o_mla, w_o_fox=m_w_o_fox, w_out=m_w_out, w_ff1=m_w_ff1, w_ff2=m_w_ff2)
    var = dict(ln_pre_mix=v_ln_pre_mix, ln_post_mix=v_ln_post_mix, ln_pre_mlp=v_ln_pre_mlp, ln_post_mlp=v_ln_post_mlp,
               w_in=v_w_in, b_in=v_b_in, q_a_norm=v_q_a_norm, w_uq=v_w_uq, kv_a_norm=v_kv_a_norm, w_uk=v_w_uk,
               w_uv=v_w_uv, w_o_mla=v_w_o_mla, w_o_fox=v_w_o_fox, w_out=v_w_out, w_ff1=v_w_ff1, w_ff2=v_w_ff2)

    big_names = [nm for nm, _ in BIG]
    layout = _PackLayout([w[nm].shape[1:] for nm in big_names])
    rows, pack_w = layout.rows, layout.width
    half = rows // 2

    gathered = _gather_weights(layout.pack([w[nm][0].astype(BF16) for nm in big_names]))
    per_chip = [layout.unpack(gathered[ch]) for ch in range(N_CHIPS)]
    full = {nm: wv for nm, wv in w.items() if nm in SMALL}
    for k, (nm, axis) in enumerate(BIG):
        full[nm] = jnp.concatenate([per_chip[ch][k] for ch in range(N_CHIPS)], axis=axis - 1)

    loss_local, grad_x, grads = _local_step(x[0], positions[0], loss_target[0], full)
    loss = lax.psum(loss_local, ("x", "y", "c"))

    gpk = jnp.stack([
        layout.pack([jnp.split(grads[nm], N_CHIPS, axis=axis - 1)[ch] for nm, axis in BIG]) for ch in range(N_CHIPS)])
    c = lax.axis_index("c")
    chip = 2 * lax.axis_index("x") + lax.axis_index("y")
    theirs = _sibling_exchange(gpk)
    mine_half = lax.dynamic_slice_in_dim(gpk, c * half, half, axis=1)

    def f_add2(ti, pa):
        tot = ti[0] + ti[1]
        return [tot, tot], []

    chip_part, chip_part_bf = _rowwise("grad_add_sibling", f_add2, [mine_half.reshape(N_CHIPS * half, pack_w),
                                                                  theirs.reshape(N_CHIPS * half, pack_w)], [],
                                       [(pack_w, F32), (pack_w, BF16)])
    others = _chip_exchange(chip_part_bf.reshape(N_CHIPS, half, pack_w))
    own = lax.dynamic_index_in_dim(chip_part.reshape(N_CHIPS, half, pack_w), chip, axis=0, keepdims=False)

    def f_add4(ti, pa):
        return [((ti[0] + ti[1].astype(F32)) + ti[2].astype(F32)) + ti[3].astype(F32)], []

    (red_half,) = _rowwise("grad_add_chips", f_add4, [own, others[0], others[1], others[2]], [], [(pack_w, F32)])
    sib_half = _sibling_swap(red_half)
    g_shards = layout.unpack(jnp.concatenate([jnp.where(c == 0, red_half, sib_half),
                                              jnp.where(c == 0, sib_half, red_half)], axis=0))

    def f_adamw(ti, pa):
        wv, gv, mv, vv = ti
        return list(_adamw(wv, gv, mv, vv)), []

    out = {"grad": {}, "delta": {}, "m": {}, "v": {}}
    for nm, g_sh in zip(big_names, g_shards, strict=True):
        wd = g_sh.shape[1]
        d_sh, m_sh, v_sh = _rowwise("adamw_" + nm, f_adamw, [w[nm][0], g_sh, mom[nm][0], var[nm][0]], [], [(wd, F32)] * 3)
        out["grad"][nm], out["delta"][nm], out["m"][nm], out["v"][nm] = g_sh[None], d_sh[None], m_sh[None], v_sh[None]

    small = _small_allreduce_adamw([grads[nm] for nm in SMALL], [w[nm] for nm in SMALL],
                                   [mom[nm] for nm in SMALL], [var[nm] for nm in SMALL])
    for kind, arrs in zip(("grad", "delta", "m", "v"), small, strict=True):
        for nm, arr in zip(SMALL, arrs, strict=True):
            out[kind][nm] = arr

    return (loss, grad_x[None], *[out["grad"][nm] for nm in ALL_W], *[out["delta"][nm] for nm in ALL_W],
            *[out["m"][nm] for nm in ALL_W], *[out["v"][nm] for nm in ALL_W])
```

```python
import functools
import math

import jax
import jax.numpy as jnp
from jax import lax
from jax.experimental import pallas as pl
from jax.experimental.pallas import tpu as pltpu

F32 = jnp.float32
BF16 = jnp.bfloat16

MLA_HEADS = 8
MLA_Q_LORA = 256
MLA_KV_LORA = 128
MLA_NOPE = 64
MLA_ROPE = 32
MLA_V = 64
FOX_HEADS = 8
FOX_DIM = 64
ROPE_THETA = 10000.0
NORM_EPS = 1e-6
HALF_ROPE = MLA_ROPE // 2

ADAM_LR = 0.001
ADAM_B1 = 0.9
ADAM_B2 = 0.999
ADAM_EPS = 1e-08
ADAM_WD = 0.01
ADAM_STEP = 10

LANES = 128
VMEM_LIMIT = 56 * 1024 * 1024
ATT_TILE = 1024
FWD_GROUP_LOG2 = 1
FWD_GROUP = 1 << FWD_GROUP_LOG2
MM_VMEM_BUDGET = 40 * 1024 * 1024
MXU_WIDTH = 256
MXU_MACS_PER_S = 4.98e14
HBM_BYTES_PER_S = 3.2e12
STEP_OVERHEAD_S = 0.35e-6
NEG = -1e30
LOG2E = math.log2(math.e)
MESH = pl.DeviceIdType.MESH

V_ONES = 64
FOX_Q_F = 64
FOX_Q_L = 67
FOX_Q_ONES = 70
MLA_Q_L = 96

BIG = (("w_in", 2), ("w_uq", 2), ("w_uk", 2), ("w_uv", 2), ("w_o_mla", 2), ("w_o_fox", 2),
       ("w_out", 1), ("w_ff1", 2), ("w_ff2", 1))
LATE = ("w_out", "w_ff1", "w_ff2")
SMALL = ("ln_pre_mix", "ln_post_mix", "ln_pre_mlp", "ln_post_mlp", "b_in", "q_a_norm", "kv_a_norm")
ALL_W = ("ln_pre_mix", "ln_post_mix", "ln_pre_mlp", "ln_post_mlp", "w_in", "b_in", "q_a_norm", "w_uq",
         "kv_a_norm", "w_uk", "w_uv", "w_o_mla", "w_o_fox", "w_out", "w_ff1", "w_ff2")
N_CHIPS = 4
PACK_W = 1024
LOCAL_PIECES = 8

_NT = (((1,), (1,)), ((), ()))
_TN = (((0,), (0,)), ((), ()))


def _cparams(sem=None):
    return pltpu.CompilerParams(dimension_semantics=sem, vmem_limit_bytes=VMEM_LIMIT)


def _divisor_tile(n, limit, mult):
    if n <= limit:
        return n
    best = None
    t = mult
    while t <= limit:
        if n % t == 0:
            best = t
        t += mult
    assert best is not None, (n, limit, mult)
    return best


def _round_up(v, mult):
    return -(-v // mult) * mult


def _mm_tiles(m, k, n, io_bytes):
    best = None
    for tm in (2048, 1024, 512, 256, 128):
        if m % tm:
            continue
        for tn in range(LANES, min(n, 2048) + 1, LANES):
            if n % tn:
                continue
            vmem = 2 * (tm * k * 2 + k * tn * 2 + tm * tn * io_bytes) + tm * tn * 4
            if vmem > MM_VMEM_BUDGET:
                continue
            mxu = m * k * n * (_round_up(tn, MXU_WIDTH) / tn) / MXU_MACS_PER_S
            hbm = (m * k * 2 + (m // tm) * k * n * 2 + m * n * io_bytes) / HBM_BYTES_PER_S
            cost = max(mxu, hbm) + (m // tm) * (n // tn) * STEP_OVERHEAD_S
            if best is None or cost < best[0]:
                best = (cost, tm, tn)
    assert best is not None, (m, k, n)
    return best[1], best[2]


def _mm(a, b, *, out_dtype, name, bias=None, transpose_b=False, extras=(), epilogue=None):
    m, k = a.shape
    n = b.shape[0] if transpose_b else b.shape[1]
    assert (b.shape[1] if transpose_b else b.shape[0]) == k and a.dtype == BF16 and b.dtype == BF16
    out_dtypes = list(out_dtype) if isinstance(out_dtype, (list, tuple)) else [out_dtype]
    n_ex = len(extras)
    tm, tn = _mm_tiles(m, k, n, sum(jnp.dtype(dt).itemsize for dt in out_dtypes) + 4 * n_ex)

    def body(*refs):
        a_ref, b_ref = refs[:2]
        pos = 2
        bias_ref = None
        if bias is not None:
            bias_ref = refs[pos]
            pos += 1
        ex_refs = refs[pos:pos + n_ex]
        o_refs = refs[pos + n_ex:]
        if transpose_b:
            acc = lax.dot_general(a_ref[...], b_ref[...], _NT, preferred_element_type=F32)
        else:
            acc = jnp.dot(a_ref[...], b_ref[...], preferred_element_type=F32)
        if bias_ref is not None:
            acc = acc + bias_ref[...]
        vals = [acc] if epilogue is None else epilogue(acc, [r[...] for r in ex_refs])
        for ref, val in zip(o_refs, vals, strict=True):
            ref[...] = val.astype(ref.dtype)

    b_spec = pl.BlockSpec((tn, k), lambda i, j: (j, 0)) if transpose_b else pl.BlockSpec((k, tn), lambda i, j: (0, j))
    in_specs = [pl.BlockSpec((tm, k), lambda i, j: (i, 0)), b_spec]
    args = [a, b]
    if bias is not None:
        in_specs.append(pl.BlockSpec((1, tn), lambda i, j: (0, j)))
        args.append(bias)
    in_specs += [pl.BlockSpec((tm, tn), lambda i, j: (i, j)) for _ in extras]
    args += list(extras)
    res = pl.pallas_call(
        body, grid=(m // tm, n // tn), in_specs=in_specs,
        out_specs=[pl.BlockSpec((tm, tn), lambda i, j: (i, j)) for _ in out_dtypes],
        out_shape=[jax.ShapeDtypeStruct((m, n), dt) for dt in out_dtypes],
        compiler_params=_cparams(("parallel", "parallel")), name=name)(*args)
    return res if isinstance(out_dtype, (list, tuple)) else res[0]


def _mm_tn(a, b, *, name):
    s, m = a.shape
    s2, n = b.shape
    assert s == s2 and a.dtype == BF16 and b.dtype == BF16
    tm = _divisor_tile(m, 1024, LANES)
    tn = _divisor_tile(n, 2304, LANES)
    tk = _divisor_tile(s, 512, 16)

    def body(a_ref, b_ref, o_ref):
        @pl.when(pl.program_id(2) == 0)
        def _():
            o_ref[...] = jnp.zeros_like(o_ref)

        o_ref[...] += lax.dot_general(a_ref[...], b_ref[...], _TN, preferred_element_type=F32)

    return pl.pallas_call(
        body, grid=(m // tm, n // tn, s // tk),
        in_specs=[pl.BlockSpec((tk, tm), lambda i, j, k: (k, i)), pl.BlockSpec((tk, tn), lambda i, j, k: (k, j))],
        out_specs=pl.BlockSpec((tm, tn), lambda i, j, k: (i, j)),
        out_shape=jax.ShapeDtypeStruct((m, n), F32),
        compiler_params=_cparams(("parallel", "parallel", "arbitrary")), name=name)(a, b)


def _rowwise(name, fn, tiled, params, outs, reds=(), reverse=False):
    wins = [t if isinstance(t, tuple) else (t, 0, t.shape[1]) for t in tiled]
    s = wins[0][0].shape[0]
    row_bytes = sum(w * arr.dtype.itemsize for arr, _, w in wins) + sum(w * jnp.dtype(d).itemsize for w, d in outs)
    ts = _divisor_tile(s, max(16, min(1024, (6 * 1024 * 1024) // row_bytes)), 16)
    nt, npar, nout = len(wins), len(params), len(outs)
    n_tiles = s // ts

    def row(i):
        return n_tiles - 1 - i if reverse else i

    def body(*refs):
        tin = [r[...] for r in refs[:nt]]
        par = [r[...] for r in refs[nt:nt + npar]]
        out_refs = refs[nt + npar:nt + npar + nout]
        red_refs = refs[nt + npar + nout:]
        o, r = fn(tin, par)
        for ref, val in zip(out_refs, o, strict=True):
            ref[...] = val.astype(ref.dtype)
        if red_refs:
            @pl.when(pl.program_id(0) == 0)
            def _():
                for ref in red_refs:
                    ref[...] = jnp.zeros_like(ref)

            for ref, val in zip(red_refs, r, strict=True):
                ref[...] += val

    in_specs = [pl.BlockSpec((ts, w), functools.partial(lambda i, cb: (row(i), cb), cb=cb)) for _, cb, w in wins]
    in_specs += [pl.BlockSpec(p.shape, lambda i: (0, 0)) for p in params]
    out_specs = [pl.BlockSpec((ts, w), lambda i: (row(i), 0)) for w, _ in outs]
    out_specs += [pl.BlockSpec((1, w), lambda i: (0, 0)) for w in reds]
    out_shape = [jax.ShapeDtypeStruct((s, w), d) for w, d in outs]
    out_shape += [jax.ShapeDtypeStruct((1, w), F32) for w in reds]
    return pl.pallas_call(
        body, grid=(n_tiles,), in_specs=in_specs, out_specs=out_specs, out_shape=out_shape,
        compiler_params=_cparams(("arbitrary",)), name=name)(*[w[0] for w in wins], *params)


def _rms(x, g):
    r = lax.rsqrt(jnp.mean(x * x, axis=-1, keepdims=True) + NORM_EPS)
    return x * r * g, r


def _rms_bwd(x, g, dy):
    r = lax.rsqrt(jnp.mean(x * x, axis=-1, keepdims=True) + NORM_EPS)
    gy = dy * g
    dx = r * gy - x * (r * r * r) * jnp.mean(x * gy, axis=-1, keepdims=True)
    dg = jnp.sum(dy * (x * r), axis=0, keepdims=True)
    return dx, dg


def _sigmoid(x):
    return 1.0 / (1.0 + jnp.exp(-x))


def _split3(x):
    hi = x.astype(BF16).astype(F32)
    r = x - hi
    mid = r.astype(BF16).astype(F32)
    lo = (r - mid).astype(BF16).astype(F32)
    return hi, mid, lo


def _lane(shape):
    return lax.broadcasted_iota(jnp.int32, shape, 1)


def _put3(blk, lane, pos, pieces):
    for k, piece in enumerate(pieces):
        blk = jnp.where(lane == pos + k, piece, blk)
    return blk


def _lane_column(blk, lane, pos):
    return jnp.sum(jnp.where(lane == pos, blk, 0.0), axis=1, keepdims=True)


def _blocks(a, nh):
    return [a[:, h * LANES:(h + 1) * LANES] for h in range(nh)]


def _rope_block(x, c, sa, sb):
    return x * c + pltpu.roll(x, LANES - HALF_ROPE, 1) * sa + pltpu.roll(x, HALF_ROPE, 1) * sb


def _forget_cumsum(z, cb):
    s = z.shape[0]
    ts = _divisor_tile(s, 512, LANES)

    def body(x_ref, col_ref, carry):
        @pl.when(pl.program_id(0) == 0)
        def _():
            carry[...] = jnp.zeros_like(carry)

        x = x_ref[...]
        lf = jnp.minimum(x, 0.0) - jnp.log1p(jnp.exp(-jnp.abs(x)))
        r = lax.broadcasted_iota(jnp.int32, (ts, ts), 0)
        c = lax.broadcasted_iota(jnp.int32, (ts, ts), 1)
        tri = jnp.where(c <= r, 1.0, 0.0).astype(F32)
        col_ref[...] = jnp.dot(tri, lf, preferred_element_type=F32, precision=lax.Precision.HIGHEST) + carry[...]
        carry[...] += jnp.sum(lf, axis=0, keepdims=True)

    return pl.pallas_call(
        body, grid=(s // ts,),
        in_specs=[pl.BlockSpec((ts, LANES), lambda i: (i, cb))],
        out_specs=pl.BlockSpec((ts, LANES), lambda i: (i, 0)),
        out_shape=jax.ShapeDtypeStruct((s, LANES), F32),
        scratch_shapes=[pltpu.VMEM((1, LANES), F32)],
        compiler_params=_cparams(("arbitrary",)), name="forget_cumsum")(z)


def _forget_cumsum_bwd(dq, dk, z, cb, nh):
    s = z.shape[0]
    ts = _divisor_tile(s, 512, LANES)
    nt = s // ts
    wd = nh * LANES

    def body(dq_ref, dk_ref, x_ref, o_ref, carry):
        @pl.when(pl.program_id(0) == 0)
        def _():
            carry[...] = jnp.zeros_like(carry)

        lane = _lane((ts, LANES))
        df = jnp.zeros((ts, LANES), F32)
        for h in range(nh):
            cols = slice(h * LANES, (h + 1) * LANES)
            d_h = _lane_column(dq_ref[:, cols], lane, FOX_Q_F) - _lane_column(dk_ref[:, cols], lane, FOX_Q_ONES)
            df = jnp.where(lane == h, d_h, df)
        r = lax.broadcasted_iota(jnp.int32, (ts, ts), 0)
        c = lax.broadcasted_iota(jnp.int32, (ts, ts), 1)
        tri = jnp.where(c >= r, 1.0, 0.0).astype(F32)
        rc = jnp.dot(tri, df, preferred_element_type=F32, precision=lax.Precision.HIGHEST) + carry[...]
        carry[...] += jnp.sum(df, axis=0, keepdims=True)
        o_ref[...] = (rc * (1.0 / (1.0 + jnp.exp(x_ref[...])))).astype(o_ref.dtype)

    return pl.pallas_call(
        body, grid=(nt,),
        in_specs=[pl.BlockSpec((ts, wd), lambda i: (nt - 1 - i, 0)),
                  pl.BlockSpec((ts, wd), lambda i: (nt - 1 - i, 0)),
                  pl.BlockSpec((ts, LANES), lambda i: (nt - 1 - i, cb))],
        out_specs=pl.BlockSpec((ts, LANES), lambda i: (nt - 1 - i, 0)),
        out_shape=jax.ShapeDtypeStruct((s, LANES), BF16),
        scratch_shapes=[pltpu.VMEM((1, LANES), F32)],
        compiler_params=_cparams(("arbitrary",)), name="forget_cumsum_bwd")(dq, dk, z)


def _flash_fwd(q, k, v, scale, nh, l_lane, *, name):
    s = q.shape[0]
    t = min(ATT_TILE, s)
    nq = s // t
    c = scale * LOG2E

    def body(q_ref, k_ref, v_ref, o_ref, qb_ref):
        i = pl.program_id(1)
        qb = q_ref[...]

        def scores(j):
            kb = k_ref[pl.ds(pl.multiple_of(j * t, t), t), :]
            return lax.dot_general(qb, kb, _NT, preferred_element_type=F32)

        def update(j, sc, carry):
            m, acc = carry
            m_new = jnp.maximum(m, jnp.max(sc, axis=1, keepdims=True))
            p = jnp.exp2((sc - m_new) * c)
            alpha = jnp.exp2((m - m_new) * c)
            vb = v_ref[pl.ds(pl.multiple_of(j * t, t), t), :]
            acc = alpha * acc + jnp.dot(p.astype(BF16), vb, preferred_element_type=F32)
            return m_new, acc

        def group(jj, cr):
            for n in range(FWD_GROUP):
                cr = update(FWD_GROUP * jj + n, scores(FWD_GROUP * jj + n), cr)
            return cr

        init = (jnp.full((t, 1), NEG, F32), jnp.zeros((t, LANES), F32))
        n_groups = i >> FWD_GROUP_LOG2
        carry = lax.fori_loop(0, n_groups, group, init)
        carry = lax.fori_loop(n_groups * FWD_GROUP, i, lambda j, cr: update(j, scores(j), cr), carry)
        row = lax.broadcasted_iota(jnp.int32, (t, t), 0)
        col = lax.broadcasted_iota(jnp.int32, (t, t), 1)
        m, acc = update(i, jnp.where(col <= row, scores(i), NEG), carry)
        lane = _lane((t, LANES))
        l = _lane_column(acc, lane, V_ONES)
        o_ref[...] = (acc / l).astype(o_ref.dtype)
        big_l = m + jnp.log(l) / scale
        qb_ref[...] = _put3(qb.astype(F32), lane, l_lane, _split3(-big_l)).astype(qb_ref.dtype)

    head_rows = pl.BlockSpec((t, LANES), lambda h, i: (i, h))
    head_all = pl.BlockSpec((s, LANES), lambda h, i: (0, h))
    return pl.pallas_call(
        body, grid=(nh, nq), in_specs=[head_rows, head_all, head_all], out_specs=[head_rows, head_rows],
        out_shape=[jax.ShapeDtypeStruct(q.shape, BF16), jax.ShapeDtypeStruct(q.shape, BF16)],
        compiler_params=_cparams(("parallel", "arbitrary")), name=name)(q, k, v)


def _flash_bwd(qb, k, v, do, scale, nh, n_feat, *, name):
    s = qb.shape[0]
    t = min(ATT_TILE, s)
    nq = s // t
    c = scale * LOG2E

    def body(q_ref, k_ref, v_ref, do_ref, dk_ref, dv_ref, dq_ref):
        j = pl.program_id(1)
        kb = k_ref[...]
        vb = v_ref[...]

        @pl.when(j == 0)
        def _():
            dq_ref[...] = jnp.zeros_like(dq_ref)

        def block(i, carry, masked):
            dk_acc, dv_acc = carry
            rows = pl.ds(pl.multiple_of(i * t, t), t)
            qblk = q_ref[rows, :]
            dob = do_ref[rows, :]
            st = lax.dot_general(kb, qblk, _NT, preferred_element_type=F32)
            if masked:
                key = lax.broadcasted_iota(jnp.int32, (t, t), 0)
                qry = lax.broadcasted_iota(jnp.int32, (t, t), 1)
                st = jnp.where(key <= qry, st, NEG)
            pt = jnp.exp2(st * c)
            dv_acc = dv_acc + jnp.dot(pt.astype(BF16), dob, preferred_element_type=F32)
            dpt = lax.dot_general(vb, dob, _NT, preferred_element_type=F32)
            dsb = (pt * dpt).astype(BF16)
            dk_acc = dk_acc + jnp.dot(dsb, qblk, preferred_element_type=F32)
            dq_ref[rows, :] += lax.dot_general(dsb, kb, _TN, preferred_element_type=F32)
            return dk_acc, dv_acc

        init = (jnp.zeros((t, LANES), F32), jnp.zeros((t, LANES), F32))
        carry = block(j, init, True)
        rest = nq - 1 - j
        carry = lax.cond((rest & 1) == 1, lambda cr: block(j + 1, cr, False), lambda cr: cr, carry)
        first = j + 1 + (rest & 1)

        def pair(ii, cr):
            i0 = first + 2 * ii
            return block(i0 + 1, block(i0, cr, False), False)

        dk_acc, dv_acc = lax.fori_loop(0, rest >> 1, pair, carry)
        dk_ref[...] = dk_acc * jnp.where(_lane((t, LANES)) < n_feat, scale, 1.0)
        dv_ref[...] = dv_acc

        @pl.when(j == nq - 1)
        def _():
            dq_ref[...] = dq_ref[...] * jnp.where(_lane((s, LANES)) < n_feat, scale, 1.0)

    head_rows = pl.BlockSpec((t, LANES), lambda h, j: (j, h))
    head_all = pl.BlockSpec((s, LANES), lambda h, j: (0, h))
    shp = jax.ShapeDtypeStruct(qb.shape, F32)
    return pl.pallas_call(
        body, grid=(nh, nq), in_specs=[head_all, head_rows, head_rows, head_all],
        out_specs=[head_rows, head_rows, head_all], out_shape=[shp, shp, shp],
        compiler_params=_cparams(("parallel", "arbitrary")), name=name)(qb, k, v, do)


def _with_delta(do, o, nh, *, name):
    def fn(ti, pa):
        lane = _lane((ti[0].shape[0], LANES))
        out = []
        for d_blk, o_blk in zip(_blocks(ti[0], nh), _blocks(ti[1], nh), strict=True):
            delta = jnp.sum(d_blk * o_blk.astype(F32), axis=1, keepdims=True)
            out.append(_put3(d_blk, lane, V_ONES, _split3(-delta)))
        return [jnp.concatenate(out, axis=1)], []

    (res,) = _rowwise(name, fn, [do, o], [], [(do.shape[1], BF16)])
    return res


def _pad_heads(a, nh):
    d = a.shape[-1] // nh
    a = a.reshape(a.shape[:-1] + (nh, d))
    a = jnp.pad(a, [(0, 0)] * (a.ndim - 1) + [(0, LANES - d)])
    return a.reshape(a.shape[:-2] + (nh * LANES,))


def _unpad_heads(a, nh, d):
    a = a.reshape(a.shape[:-1] + (nh, LANES))[..., :d]
    return a.reshape(a.shape[:-2] + (nh * d,))


def _pad_head_rows(w, nh):
    return _pad_heads(w.T, nh).T


def _unpad_head_rows(g, nh, d):
    return _unpad_heads(g.T, nh, d).T


class _ZLayout:
    def __init__(self, d):
        fw = FOX_HEADS * FOX_DIM
        self.src = {}
        off = 0
        for nm, w in (("cq", MLA_Q_LORA), ("ckv", MLA_KV_LORA), ("kr", MLA_ROPE), ("fq", fw), ("fk", fw),
                      ("fv", fw), ("fl", FOX_HEADS), ("ga", d), ("gb", d)):
            self.src[nm] = (off, w)
            off += w
        self.dst = {}
        off = 0
        for nm, w in (("fq", FOX_HEADS * LANES), ("fk", FOX_HEADS * LANES), ("fv", FOX_HEADS * LANES), ("ga", d),
                      ("gb", d), ("cq", MLA_Q_LORA), ("ckv", MLA_KV_LORA), ("kr", LANES), ("fl", LANES)):
            assert off % w == 0
            self.dst[nm] = (off, w)
            off += w
        self.width = off

    def win(self, z, nm):
        off, w = self.dst[nm]
        return (z, off // w, w)

    def to_kernel(self, w):
        def seg(nm):
            off, wd = self.src[nm]
            return w[..., off:off + wd]

        def pad(a, left, total):
            return jnp.pad(a, [(0, 0)] * (a.ndim - 1) + [(left, total - left - a.shape[-1])])

        return jnp.concatenate([_pad_heads(seg("fq"), FOX_HEADS), _pad_heads(seg("fk"), FOX_HEADS),
                                _pad_heads(seg("fv"), FOX_HEADS), seg("ga"), seg("gb"), seg("cq"), seg("ckv"),
                                pad(seg("kr"), MLA_NOPE, LANES), pad(seg("fl"), 0, LANES)], axis=-1)

    def from_kernel(self, g):
        def seg(nm, lo=0, hi=None):
            off, wd = self.dst[nm]
            return g[..., off + lo:off + (wd if hi is None else hi)]

        return jnp.concatenate([seg("cq"), seg("ckv"), seg("kr", MLA_NOPE, MLA_NOPE + MLA_ROPE),
                                _unpad_heads(seg("fq"), FOX_HEADS, FOX_DIM), _unpad_heads(seg("fk"), FOX_HEADS, FOX_DIM),
                                _unpad_heads(seg("fv"), FOX_HEADS, FOX_DIM), seg("fl", 0, FOX_HEADS), seg("ga"),
                                seg("gb")], axis=-1)


def _local_step(x, positions, target, wts, late_weights):
    s, d = x.shape
    zl = _ZLayout(d)
    hw = MLA_HEADS * LANES
    assert MLA_HEADS == FOX_HEADS
    scale_mla = (MLA_NOPE + MLA_ROPE) ** -0.5
    scale_fox = FOX_DIM ** -0.5

    inv_freq = ROPE_THETA ** (-jnp.arange(HALF_ROPE, dtype=F32) / HALF_ROPE)
    ang = positions.astype(F32)[:, None] * inv_freq
    cos, sin = jnp.cos(ang), jnp.sin(ang)
    tail = jnp.zeros((s, LANES - MLA_NOPE - MLA_ROPE), F32)
    rc = jnp.concatenate([jnp.ones((s, MLA_NOPE), F32), cos, cos, tail], axis=1)
    ra = jnp.concatenate([jnp.zeros((s, MLA_NOPE), F32), -sin, jnp.zeros((s, HALF_ROPE), F32), tail], axis=1)
    rb = jnp.concatenate([jnp.zeros((s, MLA_NOPE + HALF_ROPE), F32), sin, tail], axis=1)

    w_in = zl.to_kernel(wts["w_in"])
    b_in = zl.to_kernel(wts["b_in"])
    w_uq = _pad_heads(wts["w_uq"], MLA_HEADS)
    w_ukv = jnp.concatenate([_pad_heads(wts["w_uk"], MLA_HEADS), _pad_heads(wts["w_uv"], MLA_HEADS)], axis=1)
    w_o_mla = _pad_head_rows(wts["w_o_mla"], MLA_HEADS)
    w_o_fox = _pad_head_rows(wts["w_o_fox"], FOX_HEADS)

    def f_norm_in(ti, pa):
        y, _ = _rms(ti[0], pa[0])
        return [y], []

    (h,) = _rowwise("norm_in", f_norm_in, [x], [wts["ln_pre_mix"]], [(d, BF16)])
    z = _mm(h, w_in, bias=b_in, out_dtype=F32, name="proj_in")

    def f_mla_norms(ti, pa):
        cqn, _ = _rms(ti[0], pa[0])
        ckvn, _ = _rms(ti[1], pa[1])
        return [cqn, ckvn], []

    cqn, ckvn = _rowwise("mla_norms", f_mla_norms, [zl.win(z, "cq"), zl.win(z, "ckv")],
                         [wts["q_a_norm"], wts["kv_a_norm"]], [(MLA_Q_LORA, BF16), (MLA_KV_LORA, BF16)])
    qf = _mm(cqn, w_uq, out_dtype=F32, name="proj_uq")
    kv = _mm(ckvn, w_ukv, out_dtype=BF16, name="proj_ukv")

    def f_rope_q(ti, pa):
        xq, c_, a_, b_ = ti
        return [jnp.concatenate([_rope_block(blk, c_, a_, b_) for blk in _blocks(xq, MLA_HEADS)], axis=1)], []

    (q_mla,) = _rowwise("rope_q", f_rope_q, [qf, rc, ra, rb], [], [(hw, BF16)])

    def f_mla_kv(ti, pa):
        kn, vn, kr, c_, a_, b_ = ti
        lane = _lane(kr.shape)
        k_tail = jnp.where((lane >= MLA_Q_L) & (lane < MLA_Q_L + 3), 1.0, _rope_block(kr, c_, a_, b_))
        ones_v = (lane >= V_ONES) & (lane < V_ONES + 3)
        k_out = [jnp.where(lane < MLA_NOPE, blk.astype(F32), k_tail) for blk in _blocks(kn, MLA_HEADS)]
        v_out = [jnp.where(ones_v, 1.0, blk.astype(F32)) for blk in _blocks(vn, MLA_HEADS)]
        return [jnp.concatenate(k_out, axis=1), jnp.concatenate(v_out, axis=1)], []

    k_mla, v_mla = _rowwise("mla_kv", f_mla_kv, [(kv, 0, hw), (kv, 1, hw), zl.win(z, "kr"), rc, ra, rb], [],
                            [(hw, BF16), (hw, BF16)])
    o_mla, qb_mla = _flash_fwd(q_mla, k_mla, v_mla, scale_mla, MLA_HEADS, MLA_Q_L, name="mla_fwd")

    fl_cb = zl.dst["fl"][0] // LANES
    fcol = _forget_cumsum(z, fl_cb)

    def f_fox_qkv(ti, pa):
        fq, fk, fv, fc = ti
        lane = _lane(fc.shape)
        ones_q = (lane >= FOX_Q_ONES) & (lane < FOX_Q_ONES + 3)
        ones_k = (lane >= FOX_Q_F) & (lane < FOX_Q_ONES)
        ones_v = (lane >= V_ONES) & (lane < V_ONES + 3)
        q_out, k_out, v_out = [], [], []
        for hh, (qblk, kblk, vblk) in enumerate(zip(_blocks(fq, FOX_HEADS), _blocks(fk, FOX_HEADS),
                                                    _blocks(fv, FOX_HEADS), strict=True)):
            f_h = _lane_column(fc, lane, hh) * (1.0 / scale_fox)
            q_out.append(_put3(jnp.where(ones_q, 1.0, qblk), lane, FOX_Q_F, _split3(f_h)))
            k_out.append(_put3(jnp.where(ones_k, 1.0, kblk), lane, FOX_Q_ONES, _split3(-f_h)))
            v_out.append(jnp.where(ones_v, 1.0, vblk))
        return [jnp.concatenate(q_out, axis=1), jnp.concatenate(k_out, axis=1), jnp.concatenate(v_out, axis=1)], []

    q_fox, k_fox, v_fox = _rowwise("fox_qkv", f_fox_qkv, [zl.win(z, "fq"), zl.win(z, "fk"), zl.win(z, "fv"), fcol],
                                   [], [(hw, BF16)] * 3)
    o_fox, qb_fox = _flash_fwd(q_fox, k_fox, v_fox, scale_fox, FOX_HEADS, FOX_Q_L, name="fox_fwd")

    y_mla = _mm(o_mla, w_o_mla, out_dtype=BF16, name="proj_o_mla")
    y_fox = _mm(o_fox, w_o_fox, out_dtype=BF16, name="proj_o_fox")

    def f_gate(ti, pa):
        ga, gb, ya, yb = ti
        return [_sigmoid(ga) * ya + _sigmoid(gb) * yb], []

    (merged,) = _rowwise("gate", f_gate, [zl.win(z, "ga"), zl.win(z, "gb"), y_mla, y_fox], [], [(d, BF16)])
    wts = {**wts, **late_weights(merged)}
    mix = _mm(merged, wts["w_out"], out_dtype=F32, name="proj_out")

    def f_resid1(ti, pa):
        xa, mx = ti
        y, _ = _rms(mx, pa[0])
        x1 = xa + y
        h2, _ = _rms(x1, pa[1])
        return [x1, h2], []

    x1, h2 = _rowwise("resid_mix", f_resid1, [x, mix], [wts["ln_post_mix"], wts["ln_pre_mlp"]], [(d, F32), (d, BF16)])

    def relu2(acc, ex):
        r = jnp.maximum(acc, 0.0)
        return [acc, r * r]

    u, act = _mm(h2, wts["w_ff1"], out_dtype=[BF16, BF16], name="ff1", epilogue=relu2)
    mo = _mm(act, wts["w_ff2"], out_dtype=F32, name="ff2")

    def f_loss(ti, pa):
        xa, mv, tg = ti
        y, _ = _rms(mv, pa[0])
        err = (xa + y) - tg
        g2 = err / d
        dmo, dg = _rms_bwd(mv, pa[0], g2)
        return [g2, dmo], [jnp.sum(err * err, axis=0, keepdims=True), dg]

    g2, d_mo, loss_cols, g_ln_post_mlp = _rowwise("loss", f_loss, [x1, mo, target], [wts["ln_post_mlp"]],
                                                  [(d, F32), (d, BF16)], [d, d])
    loss = 0.5 * jnp.sum(loss_cols) / d

    grads = {"ln_post_mlp": g_ln_post_mlp}
    grads["w_ff2"] = _mm_tn(act, d_mo, name="grad_ff2")

    def relu2_bwd(acc, ex):
        return [acc * (2.0 * jnp.maximum(ex[0], 0.0))]

    (d_u,) = _mm(d_mo, wts["w_ff2"], out_dtype=[BF16], name="ff2_bwd", transpose_b=True, extras=[u], epilogue=relu2_bwd)
    grads["w_ff1"] = _mm_tn(h2, d_u, name="grad_ff1")
    d_h2 = _mm(d_u, wts["w_ff1"], out_dtype=F32, name="ff1_bwd", transpose_b=True)

    def f_resid1_bwd(ti, pa):
        gres, dh2, x1v, mx = ti
        dx1n, dg_pre_mlp = _rms_bwd(x1v, pa[1], dh2)
        dx1 = gres + dx1n
        dmix, dg_post_mix = _rms_bwd(mx, pa[0], dx1)
        return [dx1, dmix], [dg_post_mix, dg_pre_mlp]

    d_x1, d_mix, grads["ln_post_mix"], grads["ln_pre_mlp"] = _rowwise(
        "resid_mix_bwd", f_resid1_bwd, [g2, d_h2, x1, mix], [wts["ln_post_mix"], wts["ln_pre_mlp"]],
        [(d, F32), (d, BF16)], [d, d])
    grads["w_out"] = _mm_tn(merged, d_mix, name="grad_out")
    d_merged = _mm(d_mix, wts["w_out"], out_dtype=BF16, name="proj_out_bwd", transpose_b=True)

    def f_gate_bwd(ti, pa):
        dm, ga, gb, ya, yb = ti
        sa, sb = _sigmoid(ga), _sigmoid(gb)
        return [dm * sa, dm * sb, dm * ya * (sa * (1.0 - sa)), dm * yb * (sb * (1.0 - sb))], []

    d_ya, d_yb, d_ga, d_gb = _rowwise("gate_bwd", f_gate_bwd,
                                      [d_merged, zl.win(z, "ga"), zl.win(z, "gb"), y_mla, y_fox], [],
                                      [(d, BF16)] * 4)
    grads["w_o_mla"] = _unpad_head_rows(_mm_tn(o_mla, d_ya, name="grad_o_mla"), MLA_HEADS, MLA_V)
    grads["w_o_fox"] = _unpad_head_rows(_mm_tn(o_fox, d_yb, name="grad_o_fox"), FOX_HEADS, FOX_DIM)
    do_mla = _with_delta(_mm(d_ya, w_o_mla, out_dtype=F32, name="proj_o_mla_bwd", transpose_b=True), o_mla,
                         MLA_HEADS, name="mla_delta")
    do_fox = _with_delta(_mm(d_yb, w_o_fox, out_dtype=F32, name="proj_o_fox_bwd", transpose_b=True), o_fox,
                         FOX_HEADS, name="fox_delta")

    dk_mla, dv_mla, dq_mla = _flash_bwd(qb_mla, k_mla, v_mla, do_mla, scale_mla, MLA_HEADS, MLA_NOPE + MLA_ROPE,
                                        name="mla_bwd")
    dk_fox, dv_fox, dq_fox = _flash_bwd(qb_fox, k_fox, v_fox, do_fox, scale_fox, FOX_HEADS, FOX_DIM, name="fox_bwd")
    d_fl = _forget_cumsum_bwd(dq_fox, dk_fox, z, fl_cb, FOX_HEADS)

    def f_rope_q_bwd(ti, pa):
        g, c_, a_, b_ = ti
        return [jnp.concatenate([_rope_block(blk, c_, -a_, -b_) for blk in _blocks(g, MLA_HEADS)], axis=1)], []

    (d_qf,) = _rowwise("rope_q_bwd", f_rope_q_bwd, [dq_mla, rc, ra, rb], [], [(hw, BF16)])
    grads["w_uq"] = _unpad_heads(_mm_tn(cqn, d_qf, name="grad_uq"), MLA_HEADS, MLA_NOPE + MLA_ROPE)
    d_cqn = _mm(d_qf, w_uq, out_dtype=F32, name="proj_uq_bwd", transpose_b=True)

    def f_mla_kv_bwd(ti, pa):
        gk, gv, c_, a_, b_ = ti
        k_blocks = _blocks(gk, MLA_HEADS)
        tot = k_blocks[0]
        for blk in k_blocks[1:]:
            tot = tot + blk
        return [jnp.concatenate([gk, gv], axis=1), _rope_block(tot, c_, -a_, -b_)], []

    d_kv, d_kr = _rowwise("mla_kv_bwd", f_mla_kv_bwd, [dk_mla, dv_mla, rc, ra, rb], [], [(2 * hw, BF16), (LANES, BF16)])
    g_ukv = _mm_tn(ckvn, d_kv, name="grad_ukv")
    grads["w_uk"] = _unpad_heads(g_ukv[:, :hw], MLA_HEADS, MLA_NOPE)
    grads["w_uv"] = _unpad_heads(g_ukv[:, hw:], MLA_HEADS, MLA_V)
    d_ckvn = _mm(d_kv, w_ukv, out_dtype=F32, name="proj_ukv_bwd", transpose_b=True)

    def f_mla_norms_bwd(ti, pa):
        cq, ckv, dcqn, dckvn = ti
        dcq, dg_q = _rms_bwd(cq, pa[0], dcqn)
        dckv, dg_kv = _rms_bwd(ckv, pa[1], dckvn)
        return [dcq, dckv], [dg_q, dg_kv]

    d_cq, d_ckv, grads["q_a_norm"], grads["kv_a_norm"] = _rowwise(
        "mla_norms_bwd", f_mla_norms_bwd, [zl.win(z, "cq"), zl.win(z, "ckv"), d_cqn, d_ckvn],
        [wts["q_a_norm"], wts["kv_a_norm"]], [(MLA_Q_LORA, BF16), (MLA_KV_LORA, BF16)], [MLA_Q_LORA, MLA_KV_LORA])

    d_z = jnp.concatenate([dq_fox.astype(BF16), dk_fox.astype(BF16), dv_fox.astype(BF16), d_ga, d_gb, d_cq, d_ckv,
                           d_kr, d_fl], axis=1)
    assert d_z.shape[1] == zl.width

    def f_bias(ti, pa):
        return [], [jnp.sum(ti[0].astype(F32), axis=0, keepdims=True)]

    (g_b_in,) = _rowwise("grad_b_in", f_bias, [d_z], [], [], [zl.width])
    grads["b_in"] = zl.from_kernel(g_b_in)
    grads["w_in"] = zl.from_kernel(_mm_tn(h, d_z, name="grad_in"))
    d_h = _mm(d_z, w_in, out_dtype=F32, name="proj_in_bwd", transpose_b=True)

    def f_norm_in_bwd(ti, pa):
        dx1v, dh, xa = ti
        dxn, dg = _rms_bwd(xa, pa[0], dh)
        return [dx1v + dxn], [dg]

    grad_x, grads["ln_pre_mix"] = _rowwise("norm_in_bwd", f_norm_in_bwd, [d_x1, d_h, x], [wts["ln_pre_mix"]],
                                           [(d, F32)], [d])
    return loss, grad_x, grads


class _PackLayout:
    def __init__(self, shapes):
        self.shapes = list(shapes)
        self.width = _round_up(max(b for _, b in shapes), LANES)
        self.bands = []
        row = 0
        shelf = []
        for idx, (a, b) in enumerate(shapes):
            if 2 * _round_up(b, LANES) > self.width:
                self.bands.append((row, _round_up(a, 32), [(idx, 0)]))
                row += _round_up(a, 32)
            else:
                shelf.append(idx)
        col, items = 0, []
        for idx in shelf:
            wb = _round_up(shapes[idx][1], LANES)
            if col + wb > self.width:
                hgt = max(_round_up(shapes[i][0], 32) for i, _ in items)
                self.bands.append((row, hgt, items))
                row += hgt
                col, items = 0, []
            items.append((idx, col))
            col += wb
        if items:
            hgt = max(_round_up(shapes[i][0], 32) for i, _ in items)
            self.bands.append((row, hgt, items))
            row += hgt
        self.rows = _round_up(row, 16 * LOCAL_PIECES)

    def pack(self, arrs):
        bands = []
        for _, hgt, items in self.bands:
            parts = []
            for k, (idx, col) in enumerate(items):
                a, b = self.shapes[idx]
                nxt = items[k + 1][1] if k + 1 < len(items) else self.width
                parts.append(jnp.pad(arrs[idx], ((0, hgt - a), (0, nxt - col - b))))
            bands.append(parts[0] if len(parts) == 1 else jnp.concatenate(parts, axis=1))
        used = sum(hgt for _, hgt, _ in self.bands)
        if used < self.rows:
            bands.append(jnp.zeros((self.rows - used, self.width), arrs[0].dtype))
        return jnp.concatenate(bands, axis=0)

    def unpack(self, packed):
        out = [None] * len(self.shapes)
        for row, _, items in self.bands:
            for idx, col in items:
                a, b = self.shapes[idx]
                out[idx] = packed[row:row + a, col:col + b]
        return out


ANY = pl.BlockSpec(memory_space=pl.ANY)


def _place():
    return lax.axis_index("x"), lax.axis_index("y"), lax.axis_index("c")


def _gather_weights(wpk):
    rows, wd = wpk.shape
    half = rows // 2

    def body(w_ref, out_ref, send_sems, recv_sems, local_sems):
        x, y, c = _place()
        sibling = (x, y, 1 - c)
        chips = [(1 - x, y), (x, 1 - y), (1 - x, 1 - y)]

        def slab(chip, hf):
            return out_ref.at[2 * chip[0] + chip[1], pl.ds(hf * half, half), :]

        def copy(k, chip, hf, to, src=None):
            return pltpu.make_async_remote_copy(
                src_ref=slab(chip, hf) if src is None else src, dst_ref=slab(chip, hf),
                send_sem=send_sems.at[k], recv_sem=recv_sems.at[k], device_id=to, device_id_type=MESH)

        piece = rows // LOCAL_PIECES
        mine = [pltpu.make_async_copy(w_ref.at[pl.ds(n * piece, piece), :],
                                      out_ref.at[2 * x + y, pl.ds(n * piece, piece), :], local_sems.at[n])
                for n in range(LOCAL_PIECES)]
        for cp in mine:
            cp.start()
        first = [copy(j, (x, y), c, (*chip, c), src=w_ref.at[pl.ds(c * half, half), :]) for j, chip in enumerate(chips)]
        for cp in first:
            cp.start()
        passed = [copy(3 + j, chip, c, sibling) for j, chip in enumerate(chips)]
        for j, chip in enumerate(chips):
            copy(j, chip, c, (x, y, c)).wait_recv()
            passed[j].start()
        for j, chip in enumerate(chips):
            copy(3 + j, chip, 1 - c, (x, y, c)).wait_recv()
        for cp in first + passed:
            cp.wait_send()
        for cp in mine:
            cp.wait()

    assert rows % (16 * LOCAL_PIECES) == 0
    return pl.pallas_call(
        body, out_shape=jax.ShapeDtypeStruct((N_CHIPS, rows, wd), wpk.dtype),
        in_specs=[ANY], out_specs=ANY,
        scratch_shapes=[pltpu.SemaphoreType.DMA((6,)), pltpu.SemaphoreType.DMA((6,)),
                        pltpu.SemaphoreType.DMA((LOCAL_PIECES,))],
        name="gather_weights")(wpk)


HBM = pl.BlockSpec(memory_space=pltpu.HBM)
SEM = pl.BlockSpec(memory_space=pltpu.SEMAPHORE)
EFFECT = pltpu.SideEffectType.DATAFLOW_SIDE_EFFECTING
N_LATE = 6


def _gather_late_start(wpk):
    rows, wd = wpk.shape
    half = rows // 2

    def body(w_ref, land_ref, send_sems, recv_sems, w_thru, land_thru, token):
        x, y, c = _place()
        chips = [(1 - x, y), (x, 1 - y), (1 - x, 1 - y)]
        for j, chip in enumerate(chips):
            for to_core in range(2):
                pltpu.make_async_remote_copy(
                    src_ref=w_ref.at[pl.ds(c * half, half), :],
                    dst_ref=land_ref.at[2 * x + y, pl.ds(c * half, half), :],
                    send_sem=send_sems.at[2 * j + to_core], recv_sem=recv_sems.at[2 * j + c],
                    device_id=(*chip, to_core), device_id_type=MESH).start()
        token[...] = jnp.zeros_like(token)

    land = lax.empty((N_CHIPS, rows, wd), wpk.dtype)
    return pl.pallas_call(
        body, name="gather_late_start",
        out_shape=(pltpu.SemaphoreType.DMA((N_LATE,)), pltpu.SemaphoreType.DMA((N_LATE,)),
                   pltpu.HBM(wpk.shape, wpk.dtype), pltpu.HBM(land.shape, land.dtype),
                   jax.ShapeDtypeStruct((8, LANES), F32)),
        in_specs=(HBM, HBM), out_specs=(SEM, SEM, HBM, HBM, pl.BlockSpec(memory_space=pltpu.VMEM)),
        input_output_aliases={0: 2, 1: 3},
        compiler_params=pltpu.CompilerParams(has_side_effects=EFFECT),
    )(pltpu.with_memory_space_constraint(wpk, pltpu.HBM), pltpu.with_memory_space_constraint(land, pltpu.HBM))


def _gather_late_wait(send_sems, recv_sems, w_thru, land_thru, after):
    rows, wd = w_thru.shape
    half = rows // 2

    def body(w_ref, land_ref, send_sems, recv_sems, after_ref, w_dead, land_out):
        x, y, c = _place()
        for n in range(N_LATE):
            cp = pltpu.make_async_remote_copy(
                src_ref=w_ref.at[pl.ds(0, half), :], dst_ref=land_ref.at[0, pl.ds(0, half), :],
                send_sem=send_sems.at[n], recv_sem=recv_sems.at[n], device_id=(x, y, c), device_id_type=MESH)
            cp.wait_send()
            cp.wait_recv()

    return pl.pallas_call(
        body, name="gather_late_wait",
        out_shape=(pltpu.HBM(w_thru.shape, w_thru.dtype), pltpu.HBM(land_thru.shape, land_thru.dtype)),
        in_specs=(HBM, HBM, SEM, SEM, ANY), out_specs=(HBM, HBM), input_output_aliases={0: 0, 1: 1},
        compiler_params=pltpu.CompilerParams(has_side_effects=EFFECT),
    )(w_thru, land_thru, send_sems, recv_sems, after)[1]


def _sibling_exchange(g):
    n, rows, wd = g.shape
    half = rows // 2

    def body(g_ref, out_ref, send_sem, recv_sem):
        x, y, c = _place()
        cp = pltpu.make_async_remote_copy(
            src_ref=g_ref.at[:, pl.ds((1 - c) * half, half), :], dst_ref=out_ref,
            send_sem=send_sem, recv_sem=recv_sem, device_id=(x, y, 1 - c), device_id_type=MESH)
        cp.start()
        cp.wait()

    return pl.pallas_call(
        body, out_shape=jax.ShapeDtypeStruct((n, half, wd), g.dtype), in_specs=[ANY], out_specs=ANY,
        scratch_shapes=[pltpu.SemaphoreType.DMA, pltpu.SemaphoreType.DMA], name="grad_sibling_exchange")(g)


def _chip_exchange(part):
    n, half, wd = part.shape

    def body(p_ref, out_ref, send_sems, recv_sems):
        x, y, c = _place()
        chips = [(1 - x, y), (x, 1 - y), (1 - x, 1 - y)]

        def copy(j, chip):
            return pltpu.make_async_remote_copy(
                src_ref=p_ref.at[2 * chip[0] + chip[1]], dst_ref=out_ref.at[j],
                send_sem=send_sems.at[j], recv_sem=recv_sems.at[j], device_id=(*chip, c), device_id_type=MESH)

        cps = [copy(j, chip) for j, chip in enumerate(chips)]
        for cp in cps:
            cp.start()
        for cp in cps:
            cp.wait()

    return pl.pallas_call(
        body, out_shape=jax.ShapeDtypeStruct((3, half, wd), part.dtype), in_specs=[ANY], out_specs=ANY,
        scratch_shapes=[pltpu.SemaphoreType.DMA((3,)), pltpu.SemaphoreType.DMA((3,))], name="grad_chip_exchange")(part)


def _sibling_swap(mine):
    def body(m_ref, out_ref, send_sem, recv_sem):
        x, y, c = _place()
        cp = pltpu.make_async_remote_copy(
            src_ref=m_ref, dst_ref=out_ref, send_sem=send_sem, recv_sem=recv_sem,
            device_id=(x, y, 1 - c), device_id_type=MESH)
        cp.start()
        cp.wait()

    return pl.pallas_call(
        body, out_shape=jax.ShapeDtypeStruct(mine.shape, mine.dtype), in_specs=[ANY], out_specs=ANY,
        scratch_shapes=[pltpu.SemaphoreType.DMA, pltpu.SemaphoreType.DMA], name="grad_sibling_swap")(mine)


def _adamw(w, g, m, v):
    m = ADAM_B1 * m + (1.0 - ADAM_B1) * g
    v = ADAM_B2 * v + (1.0 - ADAM_B2) * (g * g)
    m_hat = m / (1.0 - ADAM_B1 ** ADAM_STEP)
    v_hat = v / (1.0 - ADAM_B2 ** ADAM_STEP)
    delta = -ADAM_LR * (m_hat / (jnp.sqrt(v_hat) + ADAM_EPS) + ADAM_WD * w)
    return delta, m, v


def _small_allreduce_adamw(gs, ws, ms, vs):
    n_dev = 8
    n_par = len(gs)
    wd = PACK_W
    chunks = []
    for p, g in enumerate(gs):
        for off in range(0, g.shape[1], wd):
            chunks.append((p, len(chunks), off, min(wd, g.shape[1] - off)))
    rows = _round_up(len(chunks), 8)

    def body(*refs):
        g_refs, w_refs, m_refs, v_refs = (refs[k * n_par:(k + 1) * n_par] for k in range(4))
        go_refs, d_refs, mo_refs, vo_refs = (refs[(4 + k) * n_par:(5 + k) * n_par] for k in range(4))
        mine_ref, all_ref, send_sems, recv_sems = refs[8 * n_par:]
        x, y, c = _place()
        me, sibling = (x, y, c), (x, y, 1 - c)
        chips = [(1 - x, y), (x, 1 - y), (1 - x, 1 - y)]

        def slot(px, py, pc):
            return all_ref.at[4 * px + 2 * py + pc]

        def copy(k, block, to, src=None):
            return pltpu.make_async_remote_copy(
                src_ref=slot(*block) if src is None else src, dst_ref=slot(*block),
                send_sem=send_sems.at[k], recv_sem=recv_sems.at[k], device_id=to, device_id_type=MESH)

        mine_ref[...] = jnp.zeros_like(mine_ref)
        for p, row, off, width in chunks:
            mine_ref[row:row + 1, 0:width] = g_refs[p][:, off:off + width]
        all_ref[4 * x + 2 * y + c] = mine_ref[...]
        first = [copy(0, me, sibling, src=mine_ref)]
        first += [copy(1 + j, me, (*chip, c), src=mine_ref) for j, chip in enumerate(chips)]
        for cp in first:
            cp.start()
        passed = [copy(4 + j, (*chip, c), sibling) for j, chip in enumerate(chips)]
        for j, chip in enumerate(chips):
            copy(1 + j, (*chip, c), me).wait_recv()
            passed[j].start()
        copy(0, sibling, me).wait_recv()
        for j, chip in enumerate(chips):
            copy(4 + j, (*chip, 1 - c), me).wait_recv()
        for cp in first + passed:
            cp.wait_send()
        tot = jnp.zeros((rows, wd), F32)
        for dev in range(n_dev):
            tot = tot + all_ref[dev]
        mine_ref[...] = tot
        for p, row, off, width in chunks:
            cols = slice(off, off + width)
            g = mine_ref[row:row + 1, 0:width]
            delta, m_new, v_new = _adamw(w_refs[p][:, cols], g, m_refs[p][:, cols], v_refs[p][:, cols])
            go_refs[p][:, cols] = g
            d_refs[p][:, cols] = delta
            mo_refs[p][:, cols] = m_new
            vo_refs[p][:, cols] = v_new

    vm = pl.BlockSpec(memory_space=pltpu.VMEM)
    shp = [jax.ShapeDtypeStruct(g.shape, F32) for g in gs]
    res = pl.pallas_call(
        body, out_shape=shp * 4, in_specs=[vm] * (4 * n_par), out_specs=[vm] * (4 * n_par),
        scratch_shapes=[pltpu.VMEM((rows, wd), F32), pltpu.VMEM((n_dev, rows, wd), F32),
                        pltpu.SemaphoreType.DMA((7,)), pltpu.SemaphoreType.DMA((7,))],
        name="small_allreduce_adamw")(*gs, *ws, *ms, *vs)
    return [res[k * n_par:(k + 1) * n_par] for k in range(4)]


def kernel(x, positions, ln_pre_mix, ln_post_mix, ln_pre_mlp, ln_post_mlp, w_in, b_in, q_a_norm, w_uq, kv_a_norm, w_uk, w_uv, w_o_mla, w_o_fox, w_out, w_ff1, w_ff2, loss_target, m_ln_pre_mix, m_ln_post_mix, m_ln_pre_mlp, m_ln_post_mlp, m_w_in, m_b_in, m_q_a_norm, m_w_uq, m_kv_a_norm, m_w_uk, m_w_uv, m_w_o_mla, m_w_o_fox, m_w_out, m_w_ff1, m_w_ff2, v_ln_pre_mix, v_ln_post_mix, v_ln_pre_mlp, v_ln_post_mlp, v_w_in, v_b_in, v_q_a_norm, v_w_uq, v_kv_a_norm, v_w_uk, v_w_uv, v_w_o_mla, v_w_o_fox, v_w_out, v_w_ff1, v_w_ff2):
    w = dict(ln_pre_mix=ln_pre_mix, ln_post_mix=ln_post_mix, ln_pre_mlp=ln_pre_mlp, ln_post_mlp=ln_post_mlp, w_in=w_in,
             b_in=b_in, q_a_norm=q_a_norm, w_uq=w_uq, kv_a_norm=kv_a_norm, w_uk=w_uk, w_uv=w_uv, w_o_mla=w_o_mla,
             w_o_fox=w_o_fox, w_out=w_out, w_ff1=w_ff1, w_ff2=w_ff2)
    mom = dict(ln_pre_mix=m_ln_pre_mix, ln_post_mix=m_ln_post_mix, ln_pre_mlp=m_ln_pre_mlp, ln_post_mlp=m_ln_post_mlp,
               w_in=m_w_in, b_in=m_b_in, q_a_norm=m_q_a_norm, w_uq=m_w_uq, kv_a_norm=m_kv_a_norm, w_uk=m_w_uk,
               w_uv=m_w_uv, w_o_mla=m_w_o_mla, w_o_fox=m_w_o_fox, w_out=m_w_out, w_ff1=m_w_ff1, w_ff2=m_w_ff2)
    var = dict(ln_pre_mix=v_ln_pre_mix, ln_post_mix=v_ln_post_mix, ln_pre_mlp=v_ln_pre_mlp, ln_post_mlp=v_ln_post_mlp,
               w_in=v_w_in, b_in=v_b_in, q_a_norm=v_q_a_norm, w_uq=v_w_uq, kv_a_norm=v_kv_a_norm, w_uk=v_w_uk,
               w_uv=v_w_uv, w_o_mla=v_w_o_mla, w_o_fox=v_w_o_fox, w_out=v_w_out, w_ff1=v_w_ff1, w_ff2=v_w_ff2)

    big_names = [nm for nm, _ in BIG]
    layout = _PackLayout([w[nm].shape[1:] for nm in big_names])
    rows, pack_w = layout.rows, layout.width
    half = rows // 2

    c = lax.axis_index("c")
    chip = 2 * lax.axis_index("x") + lax.axis_index("y")

    def assemble(names, lay, gathered):
        per_chip = [lay.unpack(gathered[ch]) for ch in range(N_CHIPS)]
        axes = dict(BIG)
        return {nm: jnp.concatenate([per_chip[ch][k] for ch in range(N_CHIPS)], axis=axes[nm] - 1)
                for k, nm in enumerate(names)}

    late_names = [nm for nm in big_names if nm in LATE]
    early_names = [nm for nm in big_names if nm not in LATE]
    lay_late = _PackLayout([w[nm].shape[1:] for nm in late_names])
    lay_early = _PackLayout([w[nm].shape[1:] for nm in early_names])
    late_shard = lay_late.pack([w[nm][0].astype(BF16) for nm in late_names])
    send_sems, recv_sems, late_thru, land_thru, token = _gather_late_start(late_shard)
    full = {nm: wv for nm, wv in w.items() if nm in SMALL}
    full["b_in"] = full["b_in"] + token[0, 0]
    full.update(assemble(early_names, lay_early,
                         _gather_weights(lay_early.pack([w[nm][0].astype(BF16) for nm in early_names]))))

    def late_weights(after):
        land = _gather_late_wait(send_sems, recv_sems, late_thru, land_thru, after)
        land = lax.dynamic_update_slice(land, late_shard[None], (chip, 0, 0))
        return assemble(late_names, lay_late, land)

    loss_local, grad_x, grads = _local_step(x[0], positions[0], loss_target[0], full, late_weights)
    loss = lax.psum(loss_local, ("x", "y", "c"))

    gpk = jnp.stack([
        layout.pack([jnp.split(grads[nm], N_CHIPS, axis=axis - 1)[ch] for nm, axis in BIG]) for ch in range(N_CHIPS)])
    theirs = _sibling_exchange(gpk)
    mine_half = lax.dynamic_slice_in_dim(gpk, c * half, half, axis=1)

    def f_add2(ti, pa):
        tot = ti[0] + ti[1]
        return [tot, tot], []

    chip_part, chip_part_bf = _rowwise("grad_add_sibling", f_add2, [mine_half.reshape(N_CHIPS * half, pack_w),
                                                                  theirs.reshape(N_CHIPS * half, pack_w)], [],
                                       [(pack_w, F32), (pack_w, BF16)])
    others = _chip_exchange(chip_part_bf.reshape(N_CHIPS, half, pack_w))
    own = lax.dynamic_index_in_dim(chip_part.reshape(N_CHIPS, half, pack_w), chip, axis=0, keepdims=False)

    def f_add4(ti, pa):
        return [((ti[0] + ti[1].astype(F32)) + ti[2].astype(F32)) + ti[3].astype(F32)], []

    (red_half,) = _rowwise("grad_add_chips", f_add4, [own, others[0], others[1], others[2]], [], [(pack_w, F32)])
    sib_half = _sibling_swap(red_half)
    g_shards = layout.unpack(jnp.concatenate([jnp.where(c == 0, red_half, sib_half),
                                              jnp.where(c == 0, sib_half, red_half)], axis=0))

    def f_adamw(ti, pa):
        wv, gv, mv, vv = ti
        return list(_adamw(wv, gv, mv, vv)), []

    out = {"grad": {}, "delta": {}, "m": {}, "v": {}}
    for nm, g_sh in zip(big_names, g_shards, strict=True):
        wd = g_sh.shape[1]
        d_sh, m_sh, v_sh = _rowwise("adamw_" + nm, f_adamw, [w[nm][0], g_sh, mom[nm][0], var[nm][0]], [], [(wd, F32)] * 3)
        out["grad"][nm], out["delta"][nm], out["m"][nm], out["v"][nm] = g_sh[None], d_sh[None], m_sh[None], v_sh[None]

    small = _small_allreduce_adamw([grads[nm] for nm in SMALL], [w[nm] for nm in SMALL],
                                   [mom[nm] for nm in SMALL], [var[nm] for nm in SMALL])
    for kind, arrs in zip(("grad", "delta", "m", "v"), small, strict=True):
        for nm, arr in zip(SMALL, arrs, strict=True):
            out[kind][nm] = arr

    return (loss, grad_x[None], *[out["grad"][nm] for nm in ALL_W], *[out["delta"][nm] for nm in ALL_W],
            *[out["m"][nm] for nm in ALL_W], *[out["v"][nm] for nm in ALL_W])
```

```python
import functools
import math

import jax
import jax.numpy as jnp
from jax import lax
from jax.experimental import pallas as pl
from jax.experimental.pallas import tpu as pltpu

F32 = jnp.float32
BF16 = jnp.bfloat16

MLA_HEADS = 8
MLA_Q_LORA = 256
MLA_KV_LORA = 128
MLA_NOPE = 64
MLA_ROPE = 32
MLA_V = 64
FOX_HEADS = 8
FOX_DIM = 64
ROPE_THETA = 10000.0
NORM_EPS = 1e-6
HALF_ROPE = MLA_ROPE // 2

ADAM_LR = 0.001
ADAM_B1 = 0.9
ADAM_B2 = 0.999
ADAM_EPS = 1e-08
ADAM_WD = 0.01
ADAM_STEP = 10

LANES = 128
VMEM_LIMIT = 56 * 1024 * 1024
ATT_TILE = 1024
FWD_GROUP_LOG2 = 1
FWD_GROUP = 1 << FWD_GROUP_LOG2
MM_VMEM_BUDGET = 40 * 1024 * 1024
MXU_WIDTH = 256
MXU_MACS_PER_S = 4.98e14
HBM_BYTES_PER_S = 3.2e12
STEP_OVERHEAD_S = 0.35e-6
NEG = -1e30
LOG2E = math.log2(math.e)
MESH = pl.DeviceIdType.MESH

V_ONES = 64
FOX_Q_F = 64
FOX_Q_L = 67
FOX_Q_ONES = 70
MLA_Q_L = 96

BIG = (("w_in", 2), ("w_uq", 2), ("w_uk", 2), ("w_uv", 2), ("w_o_mla", 2), ("w_o_fox", 2),
       ("w_out", 1), ("w_ff1", 2), ("w_ff2", 1))
LATE = ("w_out", "w_ff1", "w_ff2")
EARLY_GRADS = ("w_o_mla", "w_o_fox", "w_out", "w_ff1", "w_ff2")
SMALL = ("ln_pre_mix", "ln_post_mix", "ln_pre_mlp", "ln_post_mlp", "b_in", "q_a_norm", "kv_a_norm")
ALL_W = ("ln_pre_mix", "ln_post_mix", "ln_pre_mlp", "ln_post_mlp", "w_in", "b_in", "q_a_norm", "w_uq",
         "kv_a_norm", "w_uk", "w_uv", "w_o_mla", "w_o_fox", "w_out", "w_ff1", "w_ff2")
N_CHIPS = 4
PACK_W = 1024
LOCAL_PIECES = 8

_NT = (((1,), (1,)), ((), ()))
_TN = (((0,), (0,)), ((), ()))


def _cparams(sem=None):
    return pltpu.CompilerParams(dimension_semantics=sem, vmem_limit_bytes=VMEM_LIMIT)


def _divisor_tile(n, limit, mult):
    if n <= limit:
        return n
    best = None
    t = mult
    while t <= limit:
        if n % t == 0:
            best = t
        t += mult
    assert best is not None, (n, limit, mult)
    return best


def _round_up(v, mult):
    return -(-v // mult) * mult


def _mm_tiles(m, k, n, io_bytes):
    best = None
    for tm in (2048, 1024, 512, 256, 128):
        if m % tm:
            continue
        for tn in range(LANES, min(n, 2048) + 1, LANES):
            if n % tn:
                continue
            vmem = 2 * (tm * k * 2 + k * tn * 2 + tm * tn * io_bytes) + tm * tn * 4
            if vmem > MM_VMEM_BUDGET:
                continue
            mxu = m * k * n * (_round_up(tn, MXU_WIDTH) / tn) / MXU_MACS_PER_S
            hbm = (m * k * 2 + (m // tm) * k * n * 2 + m * n * io_bytes) / HBM_BYTES_PER_S
            cost = max(mxu, hbm) + (m // tm) * (n // tn) * STEP_OVERHEAD_S
            if best is None or cost < best[0]:
                best = (cost, tm, tn)
    assert best is not None, (m, k, n)
    return best[1], best[2]


def _mm(a, b, *, out_dtype, name, bias=None, transpose_b=False, extras=(), epilogue=None):
    m, k = a.shape
    n = b.shape[0] if transpose_b else b.shape[1]
    assert (b.shape[1] if transpose_b else b.shape[0]) == k and a.dtype == BF16 and b.dtype == BF16
    out_dtypes = list(out_dtype) if isinstance(out_dtype, (list, tuple)) else [out_dtype]
    n_ex = len(extras)
    tm, tn = _mm_tiles(m, k, n, sum(jnp.dtype(dt).itemsize for dt in out_dtypes) + 4 * n_ex)

    def body(*refs):
        a_ref, b_ref = refs[:2]
        pos = 2
        bias_ref = None
        if bias is not None:
            bias_ref = refs[pos]
            pos += 1
        ex_refs = refs[pos:pos + n_ex]
        o_refs = refs[pos + n_ex:]
        if transpose_b:
            acc = lax.dot_general(a_ref[...], b_ref[...], _NT, preferred_element_type=F32)
        else:
            acc = jnp.dot(a_ref[...], b_ref[...], preferred_element_type=F32)
        if bias_ref is not None:
            acc = acc + bias_ref[...]
        vals = [acc] if epilogue is None else epilogue(acc, [r[...] for r in ex_refs])
        for ref, val in zip(o_refs, vals, strict=True):
            ref[...] = val.astype(ref.dtype)

    b_spec = pl.BlockSpec((tn, k), lambda i, j: (j, 0)) if transpose_b else pl.BlockSpec((k, tn), lambda i, j: (0, j))
    in_specs = [pl.BlockSpec((tm, k), lambda i, j: (i, 0)), b_spec]
    args = [a, b]
    if bias is not None:
        in_specs.append(pl.BlockSpec((1, tn), lambda i, j: (0, j)))
        args.append(bias)
    in_specs += [pl.BlockSpec((tm, tn), lambda i, j: (i, j)) for _ in extras]
    args += list(extras)
    res = pl.pallas_call(
        body, grid=(m // tm, n // tn), in_specs=in_specs,
        out_specs=[pl.BlockSpec((tm, tn), lambda i, j: (i, j)) for _ in out_dtypes],
        out_shape=[jax.ShapeDtypeStruct((m, n), dt) for dt in out_dtypes],
        compiler_params=_cparams(("parallel", "parallel")), name=name)(*args)
    return res if isinstance(out_dtype, (list, tuple)) else res[0]


def _mm_tn(a, b, *, name):
    s, m = a.shape
    s2, n = b.shape
    assert s == s2 and a.dtype == BF16 and b.dtype == BF16
    tm = _divisor_tile(m, 1024, LANES)
    tn = _divisor_tile(n, 2304, LANES)
    tk = _divisor_tile(s, 512, 16)

    def body(a_ref, b_ref, o_ref):
        @pl.when(pl.program_id(2) == 0)
        def _():
            o_ref[...] = jnp.zeros_like(o_ref)

        o_ref[...] += lax.dot_general(a_ref[...], b_ref[...], _TN, preferred_element_type=F32)

    return pl.pallas_call(
        body, grid=(m // tm, n // tn, s // tk),
        in_specs=[pl.BlockSpec((tk, tm), lambda i, j, k: (k, i)), pl.BlockSpec((tk, tn), lambda i, j, k: (k, j))],
        out_specs=pl.BlockSpec((tm, tn), lambda i, j, k: (i, j)),
        out_shape=jax.ShapeDtypeStruct((m, n), F32),
        compiler_params=_cparams(("parallel", "parallel", "arbitrary")), name=name)(a, b)


def _rowwise(name, fn, tiled, params, outs, reds=(), reverse=False):
    wins = [t if isinstance(t, tuple) else (t, 0, t.shape[1]) for t in tiled]
    s = wins[0][0].shape[0]
    row_bytes = sum(w * arr.dtype.itemsize for arr, _, w in wins) + sum(w * jnp.dtype(d).itemsize for w, d in outs)
    ts = _divisor_tile(s, max(16, min(1024, (6 * 1024 * 1024) // row_bytes)), 16)
    nt, npar, nout = len(wins), len(params), len(outs)
    n_tiles = s // ts

    def row(i):
        return n_tiles - 1 - i if reverse else i

    def body(*refs):
        tin = [r[...] for r in refs[:nt]]
        par = [r[...] for r in refs[nt:nt + npar]]
        out_refs = refs[nt + npar:nt + npar + nout]
        red_refs = refs[nt + npar + nout:]
        o, r = fn(tin, par)
        for ref, val in zip(out_refs, o, strict=True):
            ref[...] = val.astype(ref.dtype)
        if red_refs:
            @pl.when(pl.program_id(0) == 0)
            def _():
                for ref in red_refs:
                    ref[...] = jnp.zeros_like(ref)

            for ref, val in zip(red_refs, r, strict=True):
                ref[...] += val

    in_specs = [pl.BlockSpec((ts, w), functools.partial(lambda i, cb: (row(i), cb), cb=cb)) for _, cb, w in wins]
    in_specs += [pl.BlockSpec(p.shape, lambda i: (0, 0)) for p in params]
    out_specs = [pl.BlockSpec((ts, w), lambda i: (row(i), 0)) for w, _ in outs]
    out_specs += [pl.BlockSpec((1, w), lambda i: (0, 0)) for w in reds]
    out_shape = [jax.ShapeDtypeStruct((s, w), d) for w, d in outs]
    out_shape += [jax.ShapeDtypeStruct((1, w), F32) for w in reds]
    return pl.pallas_call(
        body, grid=(n_tiles,), in_specs=in_specs, out_specs=out_specs, out_shape=out_shape,
        compiler_params=_cparams(("arbitrary",)), name=name)(*[w[0] for w in wins], *params)


def _rms(x, g):
    r = lax.rsqrt(jnp.mean(x * x, axis=-1, keepdims=True) + NORM_EPS)
    return x * r * g, r


def _rms_bwd(x, g, dy):
    r = lax.rsqrt(jnp.mean(x * x, axis=-1, keepdims=True) + NORM_EPS)
    gy = dy * g
    dx = r * gy - x * (r * r * r) * jnp.mean(x * gy, axis=-1, keepdims=True)
    dg = jnp.sum(dy * (x * r), axis=0, keepdims=True)
    return dx, dg


def _sigmoid(x):
    return 1.0 / (1.0 + jnp.exp(-x))


def _split3(x):
    hi = x.astype(BF16).astype(F32)
    r = x - hi
    mid = r.astype(BF16).astype(F32)
    lo = (r - mid).astype(BF16).astype(F32)
    return hi, mid, lo


def _lane(shape):
    return lax.broadcasted_iota(jnp.int32, shape, 1)


def _put3(blk, lane, pos, pieces):
    for k, piece in enumerate(pieces):
        blk = jnp.where(lane == pos + k, piece, blk)
    return blk


def _lane_column(blk, lane, pos):
    return jnp.sum(jnp.where(lane == pos, blk, 0.0), axis=1, keepdims=True)


def _blocks(a, nh):
    return [a[:, h * LANES:(h + 1) * LANES] for h in range(nh)]


def _rope_block(x, c, sa, sb):
    return x * c + pltpu.roll(x, LANES - HALF_ROPE, 1) * sa + pltpu.roll(x, HALF_ROPE, 1) * sb


def _forget_cumsum(z, cb):
    s = z.shape[0]
    ts = _divisor_tile(s, 512, LANES)

    def body(x_ref, col_ref, carry):
        @pl.when(pl.program_id(0) == 0)
        def _():
            carry[...] = jnp.zeros_like(carry)

        x = x_ref[...]
        lf = jnp.minimum(x, 0.0) - jnp.log1p(jnp.exp(-jnp.abs(x)))
        r = lax.broadcasted_iota(jnp.int32, (ts, ts), 0)
        c = lax.broadcasted_iota(jnp.int32, (ts, ts), 1)
        tri = jnp.where(c <= r, 1.0, 0.0).astype(F32)
        col_ref[...] = jnp.dot(tri, lf, preferred_element_type=F32, precision=lax.Precision.HIGHEST) + carry[...]
        carry[...] += jnp.sum(lf, axis=0, keepdims=True)

    return pl.pallas_call(
        body, grid=(s // ts,),
        in_specs=[pl.BlockSpec((ts, LANES), lambda i: (i, cb))],
        out_specs=pl.BlockSpec((ts, LANES), lambda i: (i, 0)),
        out_shape=jax.ShapeDtypeStruct((s, LANES), F32),
        scratch_shapes=[pltpu.VMEM((1, LANES), F32)],
        compiler_params=_cparams(("arbitrary",)), name="forget_cumsum")(z)


def _forget_cumsum_bwd(dq, dk, z, cb, nh):
    s = z.shape[0]
    ts = _divisor_tile(s, 512, LANES)
    nt = s // ts
    wd = nh * LANES

    def body(dq_ref, dk_ref, x_ref, o_ref, carry):
        @pl.when(pl.program_id(0) == 0)
        def _():
            carry[...] = jnp.zeros_like(carry)

        lane = _lane((ts, LANES))
        df = jnp.zeros((ts, LANES), F32)
        for h in range(nh):
            cols = slice(h * LANES, (h + 1) * LANES)
            d_h = _lane_column(dq_ref[:, cols], lane, FOX_Q_F) - _lane_column(dk_ref[:, cols], lane, FOX_Q_ONES)
            df = jnp.where(lane == h, d_h, df)
        r = lax.broadcasted_iota(jnp.int32, (ts, ts), 0)
        c = lax.broadcasted_iota(jnp.int32, (ts, ts), 1)
        tri = jnp.where(c >= r, 1.0, 0.0).astype(F32)
        rc = jnp.dot(tri, df, preferred_element_type=F32, precision=lax.Precision.HIGHEST) + carry[...]
        carry[...] += jnp.sum(df, axis=0, keepdims=True)
        o_ref[...] = (rc * (1.0 / (1.0 + jnp.exp(x_ref[...])))).astype(o_ref.dtype)

    return pl.pallas_call(
        body, grid=(nt,),
        in_specs=[pl.BlockSpec((ts, wd), lambda i: (nt - 1 - i, 0)),
                  pl.BlockSpec((ts, wd), lambda i: (nt - 1 - i, 0)),
                  pl.BlockSpec((ts, LANES), lambda i: (nt - 1 - i, cb))],
        out_specs=pl.BlockSpec((ts, LANES), lambda i: (nt - 1 - i, 0)),
        out_shape=jax.ShapeDtypeStruct((s, LANES), BF16),
        scratch_shapes=[pltpu.VMEM((1, LANES), F32)],
        compiler_params=_cparams(("arbitrary",)), name="forget_cumsum_bwd")(dq, dk, z)


def _flash_fwd(q, k, v, scale, nh, l_lane, *, name):
    s = q.shape[0]
    t = min(ATT_TILE, s)
    nq = s // t
    c = scale * LOG2E

    def body(q_ref, k_ref, v_ref, o_ref, qb_ref):
        i = pl.program_id(1)
        qb = q_ref[...]

        def scores(j):
            kb = k_ref[pl.ds(pl.multiple_of(j * t, t), t), :]
            return lax.dot_general(qb, kb, _NT, preferred_element_type=F32)

        def update(j, sc, carry):
            m, acc = carry
            m_new = jnp.maximum(m, jnp.max(sc, axis=1, keepdims=True))
            p = jnp.exp2((sc - m_new) * c)
            alpha = jnp.exp2((m - m_new) * c)
            vb = v_ref[pl.ds(pl.multiple_of(j * t, t), t), :]
            acc = alpha * acc + jnp.dot(p.astype(BF16), vb, preferred_element_type=F32)
            return m_new, acc

        def group(jj, cr):
            for n in range(FWD_GROUP):
                cr = update(FWD_GROUP * jj + n, scores(FWD_GROUP * jj + n), cr)
            return cr

        init = (jnp.full((t, 1), NEG, F32), jnp.zeros((t, LANES), F32))
        n_groups = i >> FWD_GROUP_LOG2
        carry = lax.fori_loop(0, n_groups, group, init)
        carry = lax.fori_loop(n_groups * FWD_GROUP, i, lambda j, cr: update(j, scores(j), cr), carry)
        row = lax.broadcasted_iota(jnp.int32, (t, t), 0)
        col = lax.broadcasted_iota(jnp.int32, (t, t), 1)
        m, acc = update(i, jnp.where(col <= row, scores(i), NEG), carry)
        lane = _lane((t, LANES))
        l = _lane_column(acc, lane, V_ONES)
        o_ref[...] = (acc / l).astype(o_ref.dtype)
        big_l = m + jnp.log(l) / scale
        qb_ref[...] = _put3(qb.astype(F32), lane, l_lane, _split3(-big_l)).astype(qb_ref.dtype)

    head_rows = pl.BlockSpec((t, LANES), lambda h, i: (i, h))
    head_all = pl.BlockSpec((s, LANES), lambda h, i: (0, h))
    return pl.pallas_call(
        body, grid=(nh, nq), in_specs=[head_rows, head_all, head_all], out_specs=[head_rows, head_rows],
        out_shape=[jax.ShapeDtypeStruct(q.shape, BF16), jax.ShapeDtypeStruct(q.shape, BF16)],
        compiler_params=_cparams(("parallel", "arbitrary")), name=name)(q, k, v)


def _flash_bwd(qb, k, v, do, scale, nh, n_feat, *, name):
    s = qb.shape[0]
    t = min(ATT_TILE, s)
    nq = s // t
    c = scale * LOG2E

    def body(q_ref, k_ref, v_ref, do_ref, dk_ref, dv_ref, dq_ref):
        j = pl.program_id(1)
        kb = k_ref[...]
        vb = v_ref[...]

        @pl.when(j == 0)
        def _():
            dq_ref[...] = jnp.zeros_like(dq_ref)

        def block(i, carry, masked):
            dk_acc, dv_acc = carry
            rows = pl.ds(pl.multiple_of(i * t, t), t)
            qblk = q_ref[rows, :]
            dob = do_ref[rows, :]
            st = lax.dot_general(kb, qblk, _NT, preferred_element_type=F32)
            if masked:
                key = lax.broadcasted_iota(jnp.int32, (t, t), 0)
                qry = lax.broadcasted_iota(jnp.int32, (t, t), 1)
                st = jnp.where(key <= qry, st, NEG)
            pt = jnp.exp2(st * c)
            dv_acc = dv_acc + jnp.dot(pt.astype(BF16), dob, preferred_element_type=F32)
            dpt = lax.dot_general(vb, dob, _NT, preferred_element_type=F32)
            dsb = (pt * dpt).astype(BF16)
            dk_acc = dk_acc + jnp.dot(dsb, qblk, preferred_element_type=F32)
            dq_ref[rows, :] += lax.dot_general(dsb, kb, _TN, preferred_element_type=F32)
            return dk_acc, dv_acc

        init = (jnp.zeros((t, LANES), F32), jnp.zeros((t, LANES), F32))
        carry = block(j, init, True)
        rest = nq - 1 - j
        carry = lax.cond((rest & 1) == 1, lambda cr: block(j + 1, cr, False), lambda cr: cr, carry)
        first = j + 1 + (rest & 1)

        def pair(ii, cr):
            i0 = first + 2 * ii
            return block(i0 + 1, block(i0, cr, False), False)

        dk_acc, dv_acc = lax.fori_loop(0, rest >> 1, pair, carry)
        dk_ref[...] = dk_acc * jnp.where(_lane((t, LANES)) < n_feat, scale, 1.0)
        dv_ref[...] = dv_acc

        @pl.when(j == nq - 1)
        def _():
            dq_ref[...] = dq_ref[...] * jnp.where(_lane((s, LANES)) < n_feat, scale, 1.0)

    head_rows = pl.BlockSpec((t, LANES), lambda h, j: (j, h))
    head_all = pl.BlockSpec((s, LANES), lambda h, j: (0, h))
    shp = jax.ShapeDtypeStruct(qb.shape, F32)
    return pl.pallas_call(
        body, grid=(nh, nq), in_specs=[head_all, head_rows, head_rows, head_all],
        out_specs=[head_rows, head_rows, head_all], out_shape=[shp, shp, shp],
        compiler_params=_cparams(("parallel", "arbitrary")), name=name)(qb, k, v, do)


def _with_delta(do, o, nh, *, name):
    def fn(ti, pa):
        lane = _lane((ti[0].shape[0], LANES))
        out = []
        for d_blk, o_blk in zip(_blocks(ti[0], nh), _blocks(ti[1], nh), strict=True):
            delta = jnp.sum(d_blk * o_blk.astype(F32), axis=1, keepdims=True)
            out.append(_put3(d_blk, lane, V_ONES, _split3(-delta)))
        return [jnp.concatenate(out, axis=1)], []

    (res,) = _rowwise(name, fn, [do, o], [], [(do.shape[1], BF16)])
    return res


def _pad_heads(a, nh):
    d = a.shape[-1] // nh
    a = a.reshape(a.shape[:-1] + (nh, d))
    a = jnp.pad(a, [(0, 0)] * (a.ndim - 1) + [(0, LANES - d)])
    return a.reshape(a.shape[:-2] + (nh * LANES,))


def _unpad_heads(a, nh, d):
    a = a.reshape(a.shape[:-1] + (nh, LANES))[..., :d]
    return a.reshape(a.shape[:-2] + (nh * d,))


def _pad_head_rows(w, nh):
    return _pad_heads(w.T, nh).T


def _unpad_head_rows(g, nh, d):
    return _unpad_heads(g.T, nh, d).T


class _ZLayout:
    def __init__(self, d):
        fw = FOX_HEADS * FOX_DIM
        self.src = {}
        off = 0
        for nm, w in (("cq", MLA_Q_LORA), ("ckv", MLA_KV_LORA), ("kr", MLA_ROPE), ("fq", fw), ("fk", fw),
                      ("fv", fw), ("fl", FOX_HEADS), ("ga", d), ("gb", d)):
            self.src[nm] = (off, w)
            off += w
        self.dst = {}
        off = 0
        for nm, w in (("fq", FOX_HEADS * LANES), ("fk", FOX_HEADS * LANES), ("fv", FOX_HEADS * LANES), ("ga", d),
                      ("gb", d), ("cq", MLA_Q_LORA), ("ckv", MLA_KV_LORA), ("kr", LANES), ("fl", LANES)):
            assert off % w == 0
            self.dst[nm] = (off, w)
            off += w
        self.width = off

    def win(self, z, nm):
        off, w = self.dst[nm]
        return (z, off // w, w)

    def to_kernel(self, w):
        def seg(nm):
            off, wd = self.src[nm]
            return w[..., off:off + wd]

        def pad(a, left, total):
            return jnp.pad(a, [(0, 0)] * (a.ndim - 1) + [(left, total - left - a.shape[-1])])

        return jnp.concatenate([_pad_heads(seg("fq"), FOX_HEADS), _pad_heads(seg("fk"), FOX_HEADS),
                                _pad_heads(seg("fv"), FOX_HEADS), seg("ga"), seg("gb"), seg("cq"), seg("ckv"),
                                pad(seg("kr"), MLA_NOPE, LANES), pad(seg("fl"), 0, LANES)], axis=-1)

    def from_kernel(self, g):
        def seg(nm, lo=0, hi=None):
            off, wd = self.dst[nm]
            return g[..., off + lo:off + (wd if hi is None else hi)]

        return jnp.concatenate([seg("cq"), seg("ckv"), seg("kr", MLA_NOPE, MLA_NOPE + MLA_ROPE),
                                _unpad_heads(seg("fq"), FOX_HEADS, FOX_DIM), _unpad_heads(seg("fk"), FOX_HEADS, FOX_DIM),
                                _unpad_heads(seg("fv"), FOX_HEADS, FOX_DIM), seg("fl", 0, FOX_HEADS), seg("ga"),
                                seg("gb")], axis=-1)


def _local_step(x, positions, target, wts, late_weights, early_grads):
    s, d = x.shape
    zl = _ZLayout(d)
    hw = MLA_HEADS * LANES
    assert MLA_HEADS == FOX_HEADS
    scale_mla = (MLA_NOPE + MLA_ROPE) ** -0.5
    scale_fox = FOX_DIM ** -0.5

    inv_freq = ROPE_THETA ** (-jnp.arange(HALF_ROPE, dtype=F32) / HALF_ROPE)
    ang = positions.astype(F32)[:, None] * inv_freq
    cos, sin = jnp.cos(ang), jnp.sin(ang)
    tail = jnp.zeros((s, LANES - MLA_NOPE - MLA_ROPE), F32)
    rc = jnp.concatenate([jnp.ones((s, MLA_NOPE), F32), cos, cos, tail], axis=1)
    ra = jnp.concatenate([jnp.zeros((s, MLA_NOPE), F32), -sin, jnp.zeros((s, HALF_ROPE), F32), tail], axis=1)
    rb = jnp.concatenate([jnp.zeros((s, MLA_NOPE + HALF_ROPE), F32), sin, tail], axis=1)

    w_in = zl.to_kernel(wts["w_in"])
    b_in = zl.to_kernel(wts["b_in"])
    w_uq = _pad_heads(wts["w_uq"], MLA_HEADS)
    w_ukv = jnp.concatenate([_pad_heads(wts["w_uk"], MLA_HEADS), _pad_heads(wts["w_uv"], MLA_HEADS)], axis=1)
    w_o_mla = _pad_head_rows(wts["w_o_mla"], MLA_HEADS)
    w_o_fox = _pad_head_rows(wts["w_o_fox"], FOX_HEADS)

    def f_norm_in(ti, pa):
        y, _ = _rms(ti[0], pa[0])
        return [y], []

    (h,) = _rowwise("norm_in", f_norm_in, [x], [wts["ln_pre_mix"]], [(d, BF16)])
    z = _mm(h, w_in, bias=b_in, out_dtype=F32, name="proj_in")

    def f_mla_norms(ti, pa):
        cqn, _ = _rms(ti[0], pa[0])
        ckvn, _ = _rms(ti[1], pa[1])
        return [cqn, ckvn], []

    cqn, ckvn = _rowwise("mla_norms", f_mla_norms, [zl.win(z, "cq"), zl.win(z, "ckv")],
                         [wts["q_a_norm"], wts["kv_a_norm"]], [(MLA_Q_LORA, BF16), (MLA_KV_LORA, BF16)])
    qf = _mm(cqn, w_uq, out_dtype=F32, name="proj_uq")
    kv = _mm(ckvn, w_ukv, out_dtype=BF16, name="proj_ukv")

    def f_rope_q(ti, pa):
        xq, c_, a_, b_ = ti
        return [jnp.concatenate([_rope_block(blk, c_, a_, b_) for blk in _blocks(xq, MLA_HEADS)], axis=1)], []

    (q_mla,) = _rowwise("rope_q", f_rope_q, [qf, rc, ra, rb], [], [(hw, BF16)])

    def f_mla_kv(ti, pa):
        kn, vn, kr, c_, a_, b_ = ti
        lane = _lane(kr.shape)
        k_tail = jnp.where((lane >= MLA_Q_L) & (lane < MLA_Q_L + 3), 1.0, _rope_block(kr, c_, a_, b_))
        ones_v = (lane >= V_ONES) & (lane < V_ONES + 3)
        k_out = [jnp.where(lane < MLA_NOPE, blk.astype(F32), k_tail) for blk in _blocks(kn, MLA_HEADS)]
        v_out = [jnp.where(ones_v, 1.0, blk.astype(F32)) for blk in _blocks(vn, MLA_HEADS)]
        return [jnp.concatenate(k_out, axis=1), jnp.concatenate(v_out, axis=1)], []

    k_mla, v_mla = _rowwise("mla_kv", f_mla_kv, [(kv, 0, hw), (kv, 1, hw), zl.win(z, "kr"), rc, ra, rb], [],
                            [(hw, BF16), (hw, BF16)])
    o_mla, qb_mla = _flash_fwd(q_mla, k_mla, v_mla, scale_mla, MLA_HEADS, MLA_Q_L, name="mla_fwd")

    fl_cb = zl.dst["fl"][0] // LANES
    fcol = _forget_cumsum(z, fl_cb)

    def f_fox_qkv(ti, pa):
        fq, fk, fv, fc = ti
        lane = _lane(fc.shape)
        ones_q = (lane >= FOX_Q_ONES) & (lane < FOX_Q_ONES + 3)
        ones_k = (lane >= FOX_Q_F) & (lane < FOX_Q_ONES)
        ones_v = (lane >= V_ONES) & (lane < V_ONES + 3)
        q_out, k_out, v_out = [], [], []
        for hh, (qblk, kblk, vblk) in enumerate(zip(_blocks(fq, FOX_HEADS), _blocks(fk, FOX_HEADS),
                                                    _blocks(fv, FOX_HEADS), strict=True)):
            f_h = _lane_column(fc, lane, hh) * (1.0 / scale_fox)
            q_out.append(_put3(jnp.where(ones_q, 1.0, qblk), lane, FOX_Q_F, _split3(f_h)))
            k_out.append(_put3(jnp.where(ones_k, 1.0, kblk), lane, FOX_Q_ONES, _split3(-f_h)))
            v_out.append(jnp.where(ones_v, 1.0, vblk))
        return [jnp.concatenate(q_out, axis=1), jnp.concatenate(k_out, axis=1), jnp.concatenate(v_out, axis=1)], []

    q_fox, k_fox, v_fox = _rowwise("fox_qkv", f_fox_qkv, [zl.win(z, "fq"), zl.win(z, "fk"), zl.win(z, "fv"), fcol],
                                   [], [(hw, BF16)] * 3)
    o_fox, qb_fox = _flash_fwd(q_fox, k_fox, v_fox, scale_fox, FOX_HEADS, FOX_Q_L, name="fox_fwd")

    y_mla = _mm(o_mla, w_o_mla, out_dtype=BF16, name="proj_o_mla")
    y_fox = _mm(o_fox, w_o_fox, out_dtype=BF16, name="proj_o_fox")

    def f_gate(ti, pa):
        ga, gb, ya, yb = ti
        return [_sigmoid(ga) * ya + _sigmoid(gb) * yb], []

    (merged,) = _rowwise("gate", f_gate, [zl.win(z, "ga"), zl.win(z, "gb"), y_mla, y_fox], [], [(d, BF16)])
    wts = {**wts, **late_weights(merged)}
    mix = _mm(merged, wts["w_out"], out_dtype=F32, name="proj_out")

    def f_resid1(ti, pa):
        xa, mx = ti
        y, _ = _rms(mx, pa[0])
        x1 = xa + y
        h2, _ = _rms(x1, pa[1])
        return [x1, h2], []

    x1, h2 = _rowwise("resid_mix", f_resid1, [x, mix], [wts["ln_post_mix"], wts["ln_pre_mlp"]], [(d, F32), (d, BF16)])

    def relu2(acc, ex):
        r = jnp.maximum(acc, 0.0)
        return [acc, r * r]

    u, act = _mm(h2, wts["w_ff1"], out_dtype=[BF16, BF16], name="ff1", epilogue=relu2)
    mo = _mm(act, wts["w_ff2"], out_dtype=F32, name="ff2")

    def f_loss(ti, pa):
        xa, mv, tg = ti
        y, _ = _rms(mv, pa[0])
        err = (xa + y) - tg
        g2 = err / d
        dmo, dg = _rms_bwd(mv, pa[0], g2)
        return [g2, dmo], [jnp.sum(err * err, axis=0, keepdims=True), dg]

    g2, d_mo, loss_cols, g_ln_post_mlp = _rowwise("loss", f_loss, [x1, mo, target], [wts["ln_post_mlp"]],
                                                  [(d, F32), (d, BF16)], [d, d])
    loss = 0.5 * jnp.sum(loss_cols) / d

    grads = {"ln_post_mlp": g_ln_post_mlp}
    grads["w_ff2"] = _mm_tn(act, d_mo, name="grad_ff2")

    def relu2_bwd(acc, ex):
        return [acc * (2.0 * jnp.maximum(ex[0], 0.0))]

    (d_u,) = _mm(d_mo, wts["w_ff2"], out_dtype=[BF16], name="ff2_bwd", transpose_b=True, extras=[u], epilogue=relu2_bwd)
    grads["w_ff1"] = _mm_tn(h2, d_u, name="grad_ff1")
    d_h2 = _mm(d_u, wts["w_ff1"], out_dtype=F32, name="ff1_bwd", transpose_b=True)

    def f_resid1_bwd(ti, pa):
        gres, dh2, x1v, mx = ti
        dx1n, dg_pre_mlp = _rms_bwd(x1v, pa[1], dh2)
        dx1 = gres + dx1n
        dmix, dg_post_mix = _rms_bwd(mx, pa[0], dx1)
        return [dx1, dmix], [dg_post_mix, dg_pre_mlp]

    d_x1, d_mix, grads["ln_post_mix"], grads["ln_pre_mlp"] = _rowwise(
        "resid_mix_bwd", f_resid1_bwd, [g2, d_h2, x1, mix], [wts["ln_post_mix"], wts["ln_pre_mlp"]],
        [(d, F32), (d, BF16)], [d, d])
    grads["w_out"] = _mm_tn(merged, d_mix, name="grad_out")
    d_merged = _mm(d_mix, wts["w_out"], out_dtype=BF16, name="proj_out_bwd", transpose_b=True)

    def f_gate_bwd(ti, pa):
        dm, ga, gb, ya, yb = ti
        sa, sb = _sigmoid(ga), _sigmoid(gb)
        return [dm * sa, dm * sb, dm * ya * (sa * (1.0 - sa)), dm * yb * (sb * (1.0 - sb))], []

    d_ya, d_yb, d_ga, d_gb = _rowwise("gate_bwd", f_gate_bwd,
                                      [d_merged, zl.win(z, "ga"), zl.win(z, "gb"), y_mla, y_fox], [],
                                      [(d, BF16)] * 4)
    grads["w_o_mla"] = _unpad_head_rows(_mm_tn(o_mla, d_ya, name="grad_o_mla"), MLA_HEADS, MLA_V)
    grads["w_o_fox"] = _unpad_head_rows(_mm_tn(o_fox, d_yb, name="grad_o_fox"), FOX_HEADS, FOX_DIM)
    w_o_mla = w_o_mla + early_grads({nm: grads[nm] for nm in EARLY_GRADS}).astype(BF16)
    do_mla = _with_delta(_mm(d_ya, w_o_mla, out_dtype=F32, name="proj_o_mla_bwd", transpose_b=True), o_mla,
                         MLA_HEADS, name="mla_delta")
    do_fox = _with_delta(_mm(d_yb, w_o_fox, out_dtype=F32, name="proj_o_fox_bwd", transpose_b=True), o_fox,
                         FOX_HEADS, name="fox_delta")

    dk_mla, dv_mla, dq_mla = _flash_bwd(qb_mla, k_mla, v_mla, do_mla, scale_mla, MLA_HEADS, MLA_NOPE + MLA_ROPE,
                                        name="mla_bwd")
    dk_fox, dv_fox, dq_fox = _flash_bwd(qb_fox, k_fox, v_fox, do_fox, scale_fox, FOX_HEADS, FOX_DIM, name="fox_bwd")
    d_fl = _forget_cumsum_bwd(dq_fox, dk_fox, z, fl_cb, FOX_HEADS)

    def f_rope_q_bwd(ti, pa):
        g, c_, a_, b_ = ti
        return [jnp.concatenate([_rope_block(blk, c_, -a_, -b_) for blk in _blocks(g, MLA_HEADS)], axis=1)], []

    (d_qf,) = _rowwise("rope_q_bwd", f_rope_q_bwd, [dq_mla, rc, ra, rb], [], [(hw, BF16)])
    grads["w_uq"] = _unpad_heads(_mm_tn(cqn, d_qf, name="grad_uq"), MLA_HEADS, MLA_NOPE + MLA_ROPE)
    d_cqn = _mm(d_qf, w_uq, out_dtype=F32, name="proj_uq_bwd", transpose_b=True)

    def f_mla_kv_bwd(ti, pa):
        gk, gv, c_, a_, b_ = ti
        k_blocks = _blocks(gk, MLA_HEADS)
        tot = k_blocks[0]
        for blk in k_blocks[1:]:
            tot = tot + blk
        return [jnp.concatenate([gk, gv], axis=1), _rope_block(tot, c_, -a_, -b_)], []

    d_kv, d_kr = _rowwise("mla_kv_bwd", f_mla_kv_bwd, [dk_mla, dv_mla, rc, ra, rb], [], [(2 * hw, BF16), (LANES, BF16)])
    g_ukv = _mm_tn(ckvn, d_kv, name="grad_ukv")
    grads["w_uk"] = _unpad_heads(g_ukv[:, :hw], MLA_HEADS, MLA_NOPE)
    grads["w_uv"] = _unpad_heads(g_ukv[:, hw:], MLA_HEADS, MLA_V)
    d_ckvn = _mm(d_kv, w_ukv, out_dtype=F32, name="proj_ukv_bwd", transpose_b=True)

    def f_mla_norms_bwd(ti, pa):
        cq, ckv, dcqn, dckvn = ti
        dcq, dg_q = _rms_bwd(cq, pa[0], dcqn)
        dckv, dg_kv = _rms_bwd(ckv, pa[1], dckvn)
        return [dcq, dckv], [dg_q, dg_kv]

    d_cq, d_ckv, grads["q_a_norm"], grads["kv_a_norm"] = _rowwise(
        "mla_norms_bwd", f_mla_norms_bwd, [zl.win(z, "cq"), zl.win(z, "ckv"), d_cqn, d_ckvn],
        [wts["q_a_norm"], wts["kv_a_norm"]], [(MLA_Q_LORA, BF16), (MLA_KV_LORA, BF16)], [MLA_Q_LORA, MLA_KV_LORA])

    d_z = jnp.concatenate([dq_fox.astype(BF16), dk_fox.astype(BF16), dv_fox.astype(BF16), d_ga, d_gb, d_cq, d_ckv,
                           d_kr, d_fl], axis=1)
    assert d_z.shape[1] == zl.width

    def f_bias(ti, pa):
        return [], [jnp.sum(ti[0].astype(F32), axis=0, keepdims=True)]

    (g_b_in,) = _rowwise("grad_b_in", f_bias, [d_z], [], [], [zl.width])
    grads["b_in"] = zl.from_kernel(g_b_in)
    grads["w_in"] = zl.from_kernel(_mm_tn(h, d_z, name="grad_in"))
    d_h = _mm(d_z, w_in, out_dtype=F32, name="proj_in_bwd", transpose_b=True)

    def f_norm_in_bwd(ti, pa):
        dx1v, dh, xa = ti
        dxn, dg = _rms_bwd(xa, pa[0], dh)
        return [dx1v + dxn], [dg]

    grad_x, grads["ln_pre_mix"] = _rowwise("norm_in_bwd", f_norm_in_bwd, [d_x1, d_h, x], [wts["ln_pre_mix"]],
                                           [(d, F32)], [d])
    return loss, grad_x, grads


class _PackLayout:
    def __init__(self, shapes):
        self.shapes = list(shapes)
        self.width = _round_up(max(b for _, b in shapes), LANES)
        self.bands = []
        row = 0
        shelf = []
        for idx, (a, b) in enumerate(shapes):
            if 2 * _round_up(b, LANES) > self.width:
                self.bands.append((row, _round_up(a, 32), [(idx, 0)]))
                row += _round_up(a, 32)
            else:
                shelf.append(idx)
        col, items = 0, []
        for idx in shelf:
            wb = _round_up(shapes[idx][1], LANES)
            if col + wb > self.width:
                hgt = max(_round_up(shapes[i][0], 32) for i, _ in items)
                self.bands.append((row, hgt, items))
                row += hgt
                col, items = 0, []
            items.append((idx, col))
            col += wb
        if items:
            hgt = max(_round_up(shapes[i][0], 32) for i, _ in items)
            self.bands.append((row, hgt, items))
            row += hgt
        self.rows = _round_up(row, 16 * LOCAL_PIECES)

    def pack(self, arrs):
        bands = []
        for _, hgt, items in self.bands:
            parts = []
            for k, (idx, col) in enumerate(items):
                a, b = self.shapes[idx]
                nxt = items[k + 1][1] if k + 1 < len(items) else self.width
                parts.append(jnp.pad(arrs[idx], ((0, hgt - a), (0, nxt - col - b))))
            bands.append(parts[0] if len(parts) == 1 else jnp.concatenate(parts, axis=1))
        used = sum(hgt for _, hgt, _ in self.bands)
        if used < self.rows:
            bands.append(jnp.zeros((self.rows - used, self.width), arrs[0].dtype))
        return jnp.concatenate(bands, axis=0)

    def unpack(self, packed):
        out = [None] * len(self.shapes)
        for row, _, items in self.bands:
            for idx, col in items:
                a, b = self.shapes[idx]
                out[idx] = packed[row:row + a, col:col + b]
        return out


ANY = pl.BlockSpec(memory_space=pl.ANY)


def _place():
    return lax.axis_index("x"), lax.axis_index("y"), lax.axis_index("c")


def _gather_weights(wpk):
    rows, wd = wpk.shape
    half = rows // 2

    def body(w_ref, out_ref, send_sems, recv_sems, local_sems):
        x, y, c = _place()
        sibling = (x, y, 1 - c)
        chips = [(1 - x, y), (x, 1 - y), (1 - x, 1 - y)]

        def slab(chip, hf):
            return out_ref.at[2 * chip[0] + chip[1], pl.ds(hf * half, half), :]

        def copy(k, chip, hf, to, src=None):
            return pltpu.make_async_remote_copy(
                src_ref=slab(chip, hf) if src is None else src, dst_ref=slab(chip, hf),
                send_sem=send_sems.at[k], recv_sem=recv_sems.at[k], device_id=to, device_id_type=MESH)

        piece = rows // LOCAL_PIECES
        mine = [pltpu.make_async_copy(w_ref.at[pl.ds(n * piece, piece), :],
                                      out_ref.at[2 * x + y, pl.ds(n * piece, piece), :], local_sems.at[n])
                for n in range(LOCAL_PIECES)]
        for cp in mine:
            cp.start()
        first = [copy(j, (x, y), c, (*chip, c), src=w_ref.at[pl.ds(c * half, half), :]) for j, chip in enumerate(chips)]
        for cp in first:
            cp.start()
        passed = [copy(3 + j, chip, c, sibling) for j, chip in enumerate(chips)]
        for j, chip in enumerate(chips):
            copy(j, chip, c, (x, y, c)).wait_recv()
            passed[j].start()
        for j, chip in enumerate(chips):
            copy(3 + j, chip, 1 - c, (x, y, c)).wait_recv()
        for cp in first + passed:
            cp.wait_send()
        for cp in mine:
            cp.wait()

    assert rows % (16 * LOCAL_PIECES) == 0
    return pl.pallas_call(
        body, out_shape=jax.ShapeDtypeStruct((N_CHIPS, rows, wd), wpk.dtype),
        in_specs=[ANY], out_specs=ANY,
        scratch_shapes=[pltpu.SemaphoreType.DMA((6,)), pltpu.SemaphoreType.DMA((6,)),
                        pltpu.SemaphoreType.DMA((LOCAL_PIECES,))],
        name="gather_weights")(wpk)


HBM = pl.BlockSpec(memory_space=pltpu.HBM)
SEM = pl.BlockSpec(memory_space=pltpu.SEMAPHORE)
EFFECT = pltpu.SideEffectType.DATAFLOW_SIDE_EFFECTING
N_LATE = 6


def _gather_late_start(wpk):
    rows, wd = wpk.shape
    half = rows // 2

    def body(w_ref, land_ref, send_sems, recv_sems, w_thru, land_thru, token):
        x, y, c = _place()
        chips = [(1 - x, y), (x, 1 - y), (1 - x, 1 - y)]
        for j, chip in enumerate(chips):
            for to_core in range(2):
                pltpu.make_async_remote_copy(
                    src_ref=w_ref.at[pl.ds(c * half, half), :],
                    dst_ref=land_ref.at[2 * x + y, pl.ds(c * half, half), :],
                    send_sem=send_sems.at[2 * j + to_core], recv_sem=recv_sems.at[2 * j + c],
                    device_id=(*chip, to_core), device_id_type=MESH).start()
        token[...] = jnp.zeros_like(token)

    land = lax.empty((N_CHIPS, rows, wd), wpk.dtype)
    return pl.pallas_call(
        body, name="gather_late_start",
        out_shape=(pltpu.SemaphoreType.DMA((N_LATE,)), pltpu.SemaphoreType.DMA((N_LATE,)),
                   pltpu.HBM(wpk.shape, wpk.dtype), pltpu.HBM(land.shape, land.dtype),
                   jax.ShapeDtypeStruct((8, LANES), F32)),
        in_specs=(HBM, HBM), out_specs=(SEM, SEM, HBM, HBM, pl.BlockSpec(memory_space=pltpu.VMEM)),
        input_output_aliases={0: 2, 1: 3},
        compiler_params=pltpu.CompilerParams(has_side_effects=EFFECT),
    )(pltpu.with_memory_space_constraint(wpk, pltpu.HBM), pltpu.with_memory_space_constraint(land, pltpu.HBM))


def _gather_late_wait(send_sems, recv_sems, w_thru, land_thru, after):
    rows, wd = w_thru.shape
    half = rows // 2

    def body(w_ref, land_ref, send_sems, recv_sems, after_ref, w_dead, land_out):
        x, y, c = _place()
        for n in range(N_LATE):
            cp = pltpu.make_async_remote_copy(
                src_ref=w_ref.at[pl.ds(0, half), :], dst_ref=land_ref.at[0, pl.ds(0, half), :],
                send_sem=send_sems.at[n], recv_sem=recv_sems.at[n], device_id=(x, y, c), device_id_type=MESH)
            cp.wait_send()
            cp.wait_recv()

    return pl.pallas_call(
        body, name="gather_late_wait",
        out_shape=(pltpu.HBM(w_thru.shape, w_thru.dtype), pltpu.HBM(land_thru.shape, land_thru.dtype)),
        in_specs=(HBM, HBM, SEM, SEM, ANY), out_specs=(HBM, HBM), input_output_aliases={0: 0, 1: 1},
        compiler_params=pltpu.CompilerParams(has_side_effects=EFFECT),
    )(w_thru, land_thru, send_sems, recv_sems, after)[1]


N_PART = 7


def _reduce_early_start(gbf):
    _, _, hrows, wd = gbf.shape

    def body(g_ref, land_ref, send_sems, recv_sems, g_thru, land_thru, token):
        x, y, c = _place()
        chips = [(1 - x, y), (x, 1 - y), (1 - x, 1 - y)]
        for j, chip in enumerate(chips):
            for to_core in range(2):
                pltpu.make_async_remote_copy(
                    src_ref=g_ref.at[2 * chip[0] + chip[1], to_core], dst_ref=land_ref.at[2 * j + c],
                    send_sem=send_sems.at[2 * j + to_core], recv_sem=recv_sems.at[2 * j + c],
                    device_id=(*chip, to_core), device_id_type=MESH).start()
        pltpu.make_async_remote_copy(
            src_ref=g_ref.at[2 * x + y, 1 - c], dst_ref=land_ref.at[N_PART - 1],
            send_sem=send_sems.at[N_PART - 1], recv_sem=recv_sems.at[N_PART - 1],
            device_id=(x, y, 1 - c), device_id_type=MESH).start()
        token[...] = jnp.zeros_like(token)

    land = lax.empty((N_PART, hrows, wd), gbf.dtype)
    return pl.pallas_call(
        body, name="reduce_early_start",
        out_shape=(pltpu.SemaphoreType.DMA((N_PART,)), pltpu.SemaphoreType.DMA((N_PART,)),
                   pltpu.HBM(gbf.shape, gbf.dtype), pltpu.HBM(land.shape, land.dtype),
                   jax.ShapeDtypeStruct((8, LANES), F32)),
        in_specs=(HBM, HBM), out_specs=(SEM, SEM, HBM, HBM, pl.BlockSpec(memory_space=pltpu.VMEM)),
        input_output_aliases={0: 2, 1: 3},
        compiler_params=pltpu.CompilerParams(has_side_effects=EFFECT),
    )(pltpu.with_memory_space_constraint(gbf, pltpu.HBM), pltpu.with_memory_space_constraint(land, pltpu.HBM))


def _reduce_early_wait(send_sems, recv_sems, g_thru, land_thru, after):
    def body(g_ref, land_ref, send_sems, recv_sems, after_ref, g_dead, land_out):
        x, y, c = _place()
        for n in range(N_PART):
            cp = pltpu.make_async_remote_copy(
                src_ref=g_ref.at[0, 0], dst_ref=land_ref.at[0], send_sem=send_sems.at[n], recv_sem=recv_sems.at[n],
                device_id=(x, y, c), device_id_type=MESH)
            cp.wait_send()
            cp.wait_recv()

    return pl.pallas_call(
        body, name="reduce_early_wait",
        out_shape=(pltpu.HBM(g_thru.shape, g_thru.dtype), pltpu.HBM(land_thru.shape, land_thru.dtype)),
        in_specs=(HBM, HBM, SEM, SEM, ANY), out_specs=(HBM, HBM), input_output_aliases={0: 0, 1: 1},
        compiler_params=pltpu.CompilerParams(has_side_effects=EFFECT),
    )(g_thru, land_thru, send_sems, recv_sems, after)[1]


def _sibling_exchange(g):
    n, rows, wd = g.shape
    half = rows // 2

    def body(g_ref, out_ref, send_sem, recv_sem):
        x, y, c = _place()
        cp = pltpu.make_async_remote_copy(
            src_ref=g_ref.at[:, pl.ds((1 - c) * half, half), :], dst_ref=out_ref,
            send_sem=send_sem, recv_sem=recv_sem, device_id=(x, y, 1 - c), device_id_type=MESH)
        cp.start()
        cp.wait()

    return pl.pallas_call(
        body, out_shape=jax.ShapeDtypeStruct((n, half, wd), g.dtype), in_specs=[ANY], out_specs=ANY,
        scratch_shapes=[pltpu.SemaphoreType.DMA, pltpu.SemaphoreType.DMA], name="grad_sibling_exchange")(g)


def _chip_exchange(part):
    n, half, wd = part.shape

    def body(p_ref, out_ref, send_sems, recv_sems):
        x, y, c = _place()
        chips = [(1 - x, y), (x, 1 - y), (1 - x, 1 - y)]

        def copy(j, chip):
            return pltpu.make_async_remote_copy(
                src_ref=p_ref.at[2 * chip[0] + chip[1]], dst_ref=out_ref.at[j],
                send_sem=send_sems.at[j], recv_sem=recv_sems.at[j], device_id=(*chip, c), device_id_type=MESH)

        cps = [copy(j, chip) for j, chip in enumerate(chips)]
        for cp in cps:
            cp.start()
        for cp in cps:
            cp.wait()

    return pl.pallas_call(
        body, out_shape=jax.ShapeDtypeStruct((3, half, wd), part.dtype), in_specs=[ANY], out_specs=ANY,
        scratch_shapes=[pltpu.SemaphoreType.DMA((3,)), pltpu.SemaphoreType.DMA((3,))], name="grad_chip_exchange")(part)


def _sibling_swap(mine):
    def body(m_ref, out_ref, send_sem, recv_sem):
        x, y, c = _place()
        cp = pltpu.make_async_remote_copy(
            src_ref=m_ref, dst_ref=out_ref, send_sem=send_sem, recv_sem=recv_sem,
            device_id=(x, y, 1 - c), device_id_type=MESH)
        cp.start()
        cp.wait()

    return pl.pallas_call(
        body, out_shape=jax.ShapeDtypeStruct(mine.shape, mine.dtype), in_specs=[ANY], out_specs=ANY,
        scratch_shapes=[pltpu.SemaphoreType.DMA, pltpu.SemaphoreType.DMA], name="grad_sibling_swap")(mine)


def _adamw(w, g, m, v):
    m = ADAM_B1 * m + (1.0 - ADAM_B1) * g
    v = ADAM_B2 * v + (1.0 - ADAM_B2) * (g * g)
    m_hat = m / (1.0 - ADAM_B1 ** ADAM_STEP)
    v_hat = v / (1.0 - ADAM_B2 ** ADAM_STEP)
    delta = -ADAM_LR * (m_hat / (jnp.sqrt(v_hat) + ADAM_EPS) + ADAM_WD * w)
    return delta, m, v


def _small_allreduce_adamw(gs, ws, ms, vs):
    n_dev = 8
    n_par = len(gs)
    wd = PACK_W
    chunks = []
    for p, g in enumerate(gs):
        for off in range(0, g.shape[1], wd):
            chunks.append((p, len(chunks), off, min(wd, g.shape[1] - off)))
    rows = _round_up(len(chunks), 8)

    def body(*refs):
        g_refs, w_refs, m_refs, v_refs = (refs[k * n_par:(k + 1) * n_par] for k in range(4))
        go_refs, d_refs, mo_refs, vo_refs = (refs[(4 + k) * n_par:(5 + k) * n_par] for k in range(4))
        mine_ref, all_ref, send_sems, recv_sems = refs[8 * n_par:]
        x, y, c = _place()
        me, sibling = (x, y, c), (x, y, 1 - c)
        chips = [(1 - x, y), (x, 1 - y), (1 - x, 1 - y)]

        def slot(px, py, pc):
            return all_ref.at[4 * px + 2 * py + pc]

        def copy(k, block, to, src=None):
            return pltpu.make_async_remote_copy(
                src_ref=slot(*block) if src is None else src, dst_ref=slot(*block),
                send_sem=send_sems.at[k], recv_sem=recv_sems.at[k], device_id=to, device_id_type=MESH)

        mine_ref[...] = jnp.zeros_like(mine_ref)
        for p, row, off, width in chunks:
            mine_ref[row:row + 1, 0:width] = g_refs[p][:, off:off + width]
        all_ref[4 * x + 2 * y + c] = mine_ref[...]
        first = [copy(0, me, sibling, src=mine_ref)]
        first += [copy(1 + j, me, (*chip, c), src=mine_ref) for j, chip in enumerate(chips)]
        for cp in first:
            cp.start()
        passed = [copy(4 + j, (*chip, c), sibling) for j, chip in enumerate(chips)]
        for j, chip in enumerate(chips):
            copy(1 + j, (*chip, c), me).wait_recv()
            passed[j].start()
        copy(0, sibling, me).wait_recv()
        for j, chip in enumerate(chips):
            copy(4 + j, (*chip, 1 - c), me).wait_recv()
        for cp in first + passed:
            cp.wait_send()
        tot = jnp.zeros((rows, wd), F32)
        for dev in range(n_dev):
            tot = tot + all_ref[dev]
        mine_ref[...] = tot
        for p, row, off, width in chunks:
            cols = slice(off, off + width)
            g = mine_ref[row:row + 1, 0:width]
            delta, m_new, v_new = _adamw(w_refs[p][:, cols], g, m_refs[p][:, cols], v_refs[p][:, cols])
            go_refs[p][:, cols] = g
            d_refs[p][:, cols] = delta
            mo_refs[p][:, cols] = m_new
            vo_refs[p][:, cols] = v_new

    vm = pl.BlockSpec(memory_space=pltpu.VMEM)
    shp = [jax.ShapeDtypeStruct(g.shape, F32) for g in gs]
    res = pl.pallas_call(
        body, out_shape=shp * 4, in_specs=[vm] * (4 * n_par), out_specs=[vm] * (4 * n_par),
        scratch_shapes=[pltpu.VMEM((rows, wd), F32), pltpu.VMEM((n_dev, rows, wd), F32),
                        pltpu.SemaphoreType.DMA((7,)), pltpu.SemaphoreType.DMA((7,))],
        name="small_allreduce_adamw")(*gs, *ws, *ms, *vs)
    return [res[k * n_par:(k + 1) * n_par] for k in range(4)]


def kernel(x, positions, ln_pre_mix, ln_post_mix, ln_pre_mlp, ln_post_mlp, w_in, b_in, q_a_norm, w_uq, kv_a_norm, w_uk, w_uv, w_o_mla, w_o_fox, w_out, w_ff1, w_ff2, loss_target, m_ln_pre_mix, m_ln_post_mix, m_ln_pre_mlp, m_ln_post_mlp, m_w_in, m_b_in, m_q_a_norm, m_w_uq, m_kv_a_norm, m_w_uk, m_w_uv, m_w_o_mla, m_w_o_fox, m_w_out, m_w_ff1, m_w_ff2, v_ln_pre_mix, v_ln_post_mix, v_ln_pre_mlp, v_ln_post_mlp, v_w_in, v_b_in, v_q_a_norm, v_w_uq, v_kv_a_norm, v_w_uk, v_w_uv, v_w_o_mla, v_w_o_fox, v_w_out, v_w_ff1, v_w_ff2):
    w = dict(ln_pre_mix=ln_pre_mix, ln_post_mix=ln_post_mix, ln_pre_mlp=ln_pre_mlp, ln_post_mlp=ln_post_mlp, w_in=w_in,
             b_in=b_in, q_a_norm=q_a_norm, w_uq=w_uq, kv_a_norm=kv_a_norm, w_uk=w_uk, w_uv=w_uv, w_o_mla=w_o_mla,
             w_o_fox=w_o_fox, w_out=w_out, w_ff1=w_ff1, w_ff2=w_ff2)
    mom = dict(ln_pre_mix=m_ln_pre_mix, ln_post_mix=m_ln_post_mix, ln_pre_mlp=m_ln_pre_mlp, ln_post_mlp=m_ln_post_mlp,
               w_in=m_w_in, b_in=m_b_in, q_a_norm=m_q_a_norm, w_uq=m_w_uq, kv_a_norm=m_kv_a_norm, w_uk=m_w_uk,
               w_uv=m_w_uv, w_o_mla=m_w_o_mla, w_o_fox=m_w_o_fox, w_out=m_w_out, w_ff1=m_w_ff1, w_ff2=m_w_ff2)
    var = dict(ln_pre_mix=v_ln_pre_mix, ln_post_mix=v_ln_post_mix, ln_pre_mlp=v_ln_pre_mlp, ln_post_mlp=v_ln_post_mlp,
               w_in=v_w_in, b_in=v_b_in, q_a_norm=v_q_a_norm, w_uq=v_w_uq, kv_a_norm=v_kv_a_norm, w_uk=v_w_uk,
               w_uv=v_w_uv, w_o_mla=v_w_o_mla, w_o_fox=v_w_o_fox, w_out=v_w_out, w_ff1=v_w_ff1, w_ff2=v_w_ff2)

    big_names = [nm for nm, _ in BIG]
    c = lax.axis_index("c")
    chip = 2 * lax.axis_index("x") + lax.axis_index("y")

    def assemble(names, lay, gathered):
        per_chip = [lay.unpack(gathered[ch]) for ch in range(N_CHIPS)]
        axes = dict(BIG)
        return {nm: jnp.concatenate([per_chip[ch][k] for ch in range(N_CHIPS)], axis=axes[nm] - 1)
                for k, nm in enumerate(names)}

    late_names = [nm for nm in big_names if nm in LATE]
    early_names = [nm for nm in big_names if nm not in LATE]
    lay_late = _PackLayout([w[nm].shape[1:] for nm in late_names])
    lay_early = _PackLayout([w[nm].shape[1:] for nm in early_names])
    late_shard = lay_late.pack([w[nm][0].astype(BF16) for nm in late_names])
    send_sems, recv_sems, late_thru, land_thru, token = _gather_late_start(late_shard)
    full = {nm: wv for nm, wv in w.items() if nm in SMALL}
    full["b_in"] = full["b_in"] + token[0, 0]
    full.update(assemble(early_names, lay_early,
                         _gather_weights(lay_early.pack([w[nm][0].astype(BF16) for nm in early_names]))))

    def late_weights(after):
        land = _gather_late_wait(send_sems, recv_sems, late_thru, land_thru, after)
        land = lax.dynamic_update_slice(land, late_shard[None], (chip, 0, 0))
        return assemble(late_names, lay_late, land)

    axes = dict(BIG)
    eg_names = [nm for nm in big_names if nm in EARLY_GRADS]
    lg_names = [nm for nm in big_names if nm not in EARLY_GRADS]
    lay_eg = _PackLayout([w[nm].shape[1:] for nm in eg_names])
    lay_lg = _PackLayout([w[nm].shape[1:] for nm in lg_names])
    pending = {}

    def shards_of(g, names, lay, dtype):
        return jnp.stack([lay.pack([jnp.split(g[nm], N_CHIPS, axis=axes[nm] - 1)[ch].astype(dtype) for nm in names])
                          for ch in range(N_CHIPS)])

    def early_grads(g):
        gbf = shards_of(g, eg_names, lay_eg, BF16).reshape(N_CHIPS, 2, lay_eg.rows // 2, lay_eg.width)
        started = _reduce_early_start(gbf)
        pending["handles"], tok = started[:4], started[4]
        own = lay_eg.pack([lax.dynamic_slice_in_dim(g[nm], chip * w[nm].shape[axes[nm]], w[nm].shape[axes[nm]],
                                                    axis=axes[nm] - 1) for nm in eg_names])
        pending["own"] = lax.dynamic_slice_in_dim(own, c * (lay_eg.rows // 2), lay_eg.rows // 2, axis=0)
        return tok[0, 0]

    loss_local, grad_x, grads = _local_step(x[0], positions[0], loss_target[0], full, late_weights, early_grads)
    loss = lax.psum(loss_local, ("x", "y", "c"))
    parts = _reduce_early_wait(*pending["handles"], grad_x)

    def f_add8(ti, pa):
        tot = ti[0]
        for part in ti[1:]:
            tot = tot + part.astype(F32)
        return [tot], []

    (red_early,) = _rowwise("grad_add_parts", f_add8, [pending["own"]] + [parts[n] for n in range(N_PART)], [],
                            [(lay_eg.width, F32)])

    rows, pack_w = lay_lg.rows, lay_lg.width
    half = rows // 2
    gpk = shards_of(grads, lg_names, lay_lg, F32)
    theirs = _sibling_exchange(gpk)
    mine_half = lax.dynamic_slice_in_dim(gpk, c * half, half, axis=1)

    def f_add2(ti, pa):
        tot = ti[0] + ti[1]
        return [tot, tot], []

    chip_part, chip_part_bf = _rowwise("grad_add_sibling", f_add2, [mine_half.reshape(N_CHIPS * half, pack_w),
                                                                  theirs.reshape(N_CHIPS * half, pack_w)], [],
                                       [(pack_w, F32), (pack_w, BF16)])
    others = _chip_exchange(chip_part_bf.reshape(N_CHIPS, half, pack_w))
    own = lax.dynamic_index_in_dim(chip_part.reshape(N_CHIPS, half, pack_w), chip, axis=0, keepdims=False)

    def f_add4(ti, pa):
        return [((ti[0] + ti[1].astype(F32)) + ti[2].astype(F32)) + ti[3].astype(F32)], []

    (red_late,) = _rowwise("grad_add_chips", f_add4, [own, others[0], others[1], others[2]], [], [(pack_w, F32)])

    assert lay_eg.width == lay_lg.width
    red = jnp.concatenate([red_early, red_late], axis=0)
    sib = _sibling_swap(red)
    lower, upper = jnp.where(c == 0, red, sib), jnp.where(c == 0, sib, red)
    n_e = lay_eg.rows // 2
    g_by_name = dict(zip(eg_names, lay_eg.unpack(jnp.concatenate([lower[:n_e], upper[:n_e]], axis=0)), strict=True))
    g_by_name.update(zip(lg_names, lay_lg.unpack(jnp.concatenate([lower[n_e:], upper[n_e:]], axis=0)), strict=True))
    g_shards = [g_by_name[nm] for nm in big_names]

    def f_adamw(ti, pa):
        wv, gv, mv, vv = ti
        return list(_adamw(wv, gv, mv, vv)), []

    out = {"grad": {}, "delta": {}, "m": {}, "v": {}}
    for nm, g_sh in zip(big_names, g_shards, strict=True):
        wd = g_sh.shape[1]
        d_sh, m_sh, v_sh = _rowwise("adamw_" + nm, f_adamw, [w[nm][0], g_sh, mom[nm][0], var[nm][0]], [], [(wd, F32)] * 3)
        out["grad"][nm], out["delta"][nm], out["m"][nm], out["v"][nm] = g_sh[None], d_sh[None], m_sh[None], v_sh[None]

    small = _small_allreduce_adamw([grads[nm] for nm in SMALL], [w[nm] for nm in SMALL],
                                   [mom[nm] for nm in SMALL], [var[nm] for nm in SMALL])
    for kind, arrs in zip(("grad", "delta", "m", "v"), small, strict=True):
        for nm, arr in zip(SMALL, arrs, strict=True):
            out[kind][nm] = arr

    return (loss, grad_x[None], *[out["grad"][nm] for nm in ALL_W], *[out["delta"][nm] for nm in ALL_W],
            *[out["m"][nm] for nm in ALL_W], *[out["v"][nm] for nm in ALL_W])
```

```python
import functools
import math

import jax
import jax.numpy as jnp
from jax import lax
from jax.experimental import pallas as pl
from jax.experimental.pallas import tpu as pltpu

F32 = jnp.float32
BF16 = jnp.bfloat16

MLA_HEADS = 8
MLA_Q_LORA = 256
MLA_KV_LORA = 128
MLA_NOPE = 64
MLA_ROPE = 32
MLA_V = 64
FOX_HEADS = 8
FOX_DIM = 64
ROPE_THETA = 10000.0
NORM_EPS = 1e-6
HALF_ROPE = MLA_ROPE // 2

ADAM_LR = 0.001
ADAM_B1 = 0.9
ADAM_B2 = 0.999
ADAM_EPS = 1e-08
ADAM_WD = 0.01
ADAM_STEP = 10

LANES = 128
VMEM_LIMIT = 56 * 1024 * 1024
ATT_TILE = 1024
FWD_GROUP_LOG2 = 1
FWD_GROUP = 1 << FWD_GROUP_LOG2
MM_VMEM_BUDGET = 40 * 1024 * 1024
MXU_WIDTH = 256
MXU_MACS_PER_S = 4.98e14
HBM_BYTES_PER_S = 3.2e12
STEP_OVERHEAD_S = 0.35e-6
NEG = -1e30
LOG2E = math.log2(math.e)
MESH = pl.DeviceIdType.MESH

V_ONES = 64
FOX_Q_F = 64
FOX_Q_L = 67
FOX_Q_ONES = 70
MLA_Q_L = 96

BIG = (("w_in", 2), ("w_uq", 2), ("w_uk", 2), ("w_uv", 2), ("w_o_mla", 2), ("w_o_fox", 2),
       ("w_out", 1), ("w_ff1", 2), ("w_ff2", 1))
MIXER = ("w_uq", "w_uk", "w_uv", "w_o_mla", "w_o_fox")
LATE = ("w_out", "w_ff1", "w_ff2")
EARLY_GRADS = ("w_o_mla", "w_o_fox", "w_out", "w_ff1", "w_ff2")
SMALL = ("ln_pre_mix", "ln_post_mix", "ln_pre_mlp", "ln_post_mlp", "b_in", "q_a_norm", "kv_a_norm")
ALL_W = ("ln_pre_mix", "ln_post_mix", "ln_pre_mlp", "ln_post_mlp", "w_in", "b_in", "q_a_norm", "w_uq",
         "kv_a_norm", "w_uk", "w_uv", "w_o_mla", "w_o_fox", "w_out", "w_ff1", "w_ff2")
N_CHIPS = 4
PACK_W = 1024
LOCAL_PIECES = 8

_NT = (((1,), (1,)), ((), ()))
_TN = (((0,), (0,)), ((), ()))


def _cparams(sem=None):
    return pltpu.CompilerParams(dimension_semantics=sem, vmem_limit_bytes=VMEM_LIMIT)


def _divisor_tile(n, limit, mult):
    if n <= limit:
        return n
    best = None
    t = mult
    while t <= limit:
        if n % t == 0:
            best = t
        t += mult
    assert best is not None, (n, limit, mult)
    return best


def _round_up(v, mult):
    return -(-v // mult) * mult


def _mm_tiles(m, k, n, io_bytes):
    best = None
    for tm in (2048, 1024, 512, 256, 128):
        if m % tm:
            continue
        for tn in range(LANES, min(n, 2048) + 1, LANES):
            if n % tn:
                continue
            vmem = 2 * (tm * k * 2 + k * tn * 2 + tm * tn * io_bytes) + tm * tn * 4
            if vmem > MM_VMEM_BUDGET:
                continue
            mxu = m * k * n * (_round_up(tn, MXU_WIDTH) / tn) / MXU_MACS_PER_S
            hbm = (m * k * 2 + (m // tm) * k * n * 2 + m * n * io_bytes) / HBM_BYTES_PER_S
            cost = max(mxu, hbm) + (m // tm) * (n // tn) * STEP_OVERHEAD_S
            if best is None or cost < best[0]:
                best = (cost, tm, tn)
    assert best is not None, (m, k, n)
    return best[1], best[2]


def _mm(a, b, *, out_dtype, name, bias=None, transpose_b=False, extras=(), epilogue=None):
    m, k = a.shape
    n = b.shape[0] if transpose_b else b.shape[1]
    assert (b.shape[1] if transpose_b else b.shape[0]) == k and a.dtype == BF16 and b.dtype == BF16
    out_dtypes = list(out_dtype) if isinstance(out_dtype, (list, tuple)) else [out_dtype]
    n_ex = len(extras)
    tm, tn = _mm_tiles(m, k, n, sum(jnp.dtype(dt).itemsize for dt in out_dtypes) + 4 * n_ex)

    def body(*refs):
        a_ref, b_ref = refs[:2]
        pos = 2
        bias_ref = None
        if bias is not None:
            bias_ref = refs[pos]
            pos += 1
        ex_refs = refs[pos:pos + n_ex]
        o_refs = refs[pos + n_ex:]
        if transpose_b:
            acc = lax.dot_general(a_ref[...], b_ref[...], _NT, preferred_element_type=F32)
        else:
            acc = jnp.dot(a_ref[...], b_ref[...], preferred_element_type=F32)
        if bias_ref is not None:
            acc = acc + bias_ref[...]
        vals = [acc] if epilogue is None else epilogue(acc, [r[...] for r in ex_refs])
        for ref, val in zip(o_refs, vals, strict=True):
            ref[...] = val.astype(ref.dtype)

    b_spec = pl.BlockSpec((tn, k), lambda i, j: (j, 0)) if transpose_b else pl.BlockSpec((k, tn), lambda i, j: (0, j))
    in_specs = [pl.BlockSpec((tm, k), lambda i, j: (i, 0)), b_spec]
    args = [a, b]
    if bias is not None:
        in_specs.append(pl.BlockSpec((1, tn), lambda i, j: (0, j)))
        args.append(bias)
    in_specs += [pl.BlockSpec((tm, tn), lambda i, j: (i, j)) for _ in extras]
    args += list(extras)
    res = pl.pallas_call(
        body, grid=(m // tm, n // tn), in_specs=in_specs,
        out_specs=[pl.BlockSpec((tm, tn), lambda i, j: (i, j)) for _ in out_dtypes],
        out_shape=[jax.ShapeDtypeStruct((m, n), dt) for dt in out_dtypes],
        compiler_params=_cparams(("parallel", "parallel")), name=name)(*args)
    return res if isinstance(out_dtype, (list, tuple)) else res[0]


def _mm_tn(a, b, *, name):
    s, m = a.shape
    s2, n = b.shape
    assert s == s2 and a.dtype == BF16 and b.dtype == BF16
    tm = _divisor_tile(m, 1024, LANES)
    tn = _divisor_tile(n, 2304, LANES)
    tk = _divisor_tile(s, 512, 16)

    def body(a_ref, b_ref, o_ref):
        @pl.when(pl.program_id(2) == 0)
        def _():
            o_ref[...] = jnp.zeros_like(o_ref)

        o_ref[...] += lax.dot_general(a_ref[...], b_ref[...], _TN, preferred_element_type=F32)

    return pl.pallas_call(
        body, grid=(m // tm, n // tn, s // tk),
        in_specs=[pl.BlockSpec((tk, tm), lambda i, j, k: (k, i)), pl.BlockSpec((tk, tn), lambda i, j, k: (k, j))],
        out_specs=pl.BlockSpec((tm, tn), lambda i, j, k: (i, j)),
        out_shape=jax.ShapeDtypeStruct((m, n), F32),
        compiler_params=_cparams(("parallel", "parallel", "arbitrary")), name=name)(a, b)


def _rowwise(name, fn, tiled, params, outs, reds=(), reverse=False):
    wins = [t if isinstance(t, tuple) else (t, 0, t.shape[1]) for t in tiled]
    s = wins[0][0].shape[0]
    row_bytes = sum(w * arr.dtype.itemsize for arr, _, w in wins) + sum(w * jnp.dtype(d).itemsize for w, d in outs)
    ts = _divisor_tile(s, max(16, min(1024, (6 * 1024 * 1024) // row_bytes)), 16)
    nt, npar, nout = len(wins), len(params), len(outs)
    n_tiles = s // ts

    def row(i):
        return n_tiles - 1 - i if reverse else i

    def body(*refs):
        tin = [r[...] for r in refs[:nt]]
        par = [r[...] for r in refs[nt:nt + npar]]
        out_refs = refs[nt + npar:nt + npar + nout]
        red_refs = refs[nt + npar + nout:]
        o, r = fn(tin, par)
        for ref, val in zip(out_refs, o, strict=True):
            ref[...] = val.astype(ref.dtype)
        if red_refs:
            @pl.when(pl.program_id(0) == 0)
            def _():
                for ref in red_refs:
                    ref[...] = jnp.zeros_like(ref)

            for ref, val in zip(red_refs, r, strict=True):
                ref[...] += val

    in_specs = [pl.BlockSpec((ts, w), functools.partial(lambda i, cb: (row(i), cb), cb=cb)) for _, cb, w in wins]
    in_specs += [pl.BlockSpec(p.shape, lambda i: (0, 0)) for p in params]
    out_specs = [pl.BlockSpec((ts, w), lambda i: (row(i), 0)) for w, _ in outs]
    out_specs += [pl.BlockSpec((1, w), lambda i: (0, 0)) for w in reds]
    out_shape = [jax.ShapeDtypeStruct((s, w), d) for w, d in outs]
    out_shape += [jax.ShapeDtypeStruct((1, w), F32) for w in reds]
    return pl.pallas_call(
        body, grid=(n_tiles,), in_specs=in_specs, out_specs=out_specs, out_shape=out_shape,
        compiler_params=_cparams(("arbitrary",)), name=name)(*[w[0] for w in wins], *params)


def _rms(x, g):
    r = lax.rsqrt(jnp.mean(x * x, axis=-1, keepdims=True) + NORM_EPS)
    return x * r * g, r


def _rms_bwd(x, g, dy):
    r = lax.rsqrt(jnp.mean(x * x, axis=-1, keepdims=True) + NORM_EPS)
    gy = dy * g
    dx = r * gy - x * (r * r * r) * jnp.mean(x * gy, axis=-1, keepdims=True)
    dg = jnp.sum(dy * (x * r), axis=0, keepdims=True)
    return dx, dg


def _sigmoid(x):
    return 1.0 / (1.0 + jnp.exp(-x))


def _split3(x):
    hi = x.astype(BF16).astype(F32)
    r = x - hi
    mid = r.astype(BF16).astype(F32)
    lo = (r - mid).astype(BF16).astype(F32)
    return hi, mid, lo


def _lane(shape):
    return lax.broadcasted_iota(jnp.int32, shape, 1)


def _put3(blk, lane, pos, pieces):
    for k, piece in enumerate(pieces):
        blk = jnp.where(lane == pos + k, piece, blk)
    return blk


def _lane_column(blk, lane, pos):
    return jnp.sum(jnp.where(lane == pos, blk, 0.0), axis=1, keepdims=True)


def _blocks(a, nh):
    return [a[:, h * LANES:(h + 1) * LANES] for h in range(nh)]


def _rope_block(x, c, sa, sb):
    return x * c + pltpu.roll(x, LANES - HALF_ROPE, 1) * sa + pltpu.roll(x, HALF_ROPE, 1) * sb


def _forget_cumsum(z, cb):
    s = z.shape[0]
    ts = _divisor_tile(s, 512, LANES)

    def body(x_ref, col_ref, carry):
        @pl.when(pl.program_id(0) == 0)
        def _():
            carry[...] = jnp.zeros_like(carry)

        x = x_ref[...]
        lf = jnp.minimum(x, 0.0) - jnp.log1p(jnp.exp(-jnp.abs(x)))
        r = lax.broadcasted_iota(jnp.int32, (ts, ts), 0)
        c = lax.broadcasted_iota(jnp.int32, (ts, ts), 1)
        tri = jnp.where(c <= r, 1.0, 0.0).astype(F32)
        col_ref[...] = jnp.dot(tri, lf, preferred_element_type=F32, precision=lax.Precision.HIGHEST) + carry[...]
        carry[...] += jnp.sum(lf, axis=0, keepdims=True)

    return pl.pallas_call(
        body, grid=(s // ts,),
        in_specs=[pl.BlockSpec((ts, LANES), lambda i: (i, cb))],
        out_specs=pl.BlockSpec((ts, LANES), lambda i: (i, 0)),
        out_shape=jax.ShapeDtypeStruct((s, LANES), F32),
        scratch_shapes=[pltpu.VMEM((1, LANES), F32)],
        compiler_params=_cparams(("arbitrary",)), name="forget_cumsum")(z)


def _forget_cumsum_bwd(dq, dk, z, cb, nh):
    s = z.shape[0]
    ts = _divisor_tile(s, 512, LANES)
    nt = s // ts
    wd = nh * LANES

    def body(dq_ref, dk_ref, x_ref, o_ref, carry):
        @pl.when(pl.program_id(0) == 0)
        def _():
            carry[...] = jnp.zeros_like(carry)

        lane = _lane((ts, LANES))
        df = jnp.zeros((ts, LANES), F32)
        for h in range(nh):
            cols = slice(h * LANES, (h + 1) * LANES)
            d_h = _lane_column(dq_ref[:, cols], lane, FOX_Q_F) - _lane_column(dk_ref[:, cols], lane, FOX_Q_ONES)
            df = jnp.where(lane == h, d_h, df)
        r = lax.broadcasted_iota(jnp.int32, (ts, ts), 0)
        c = lax.broadcasted_iota(jnp.int32, (ts, ts), 1)
        tri = jnp.where(c >= r, 1.0, 0.0).astype(F32)
        rc = jnp.dot(tri, df, preferred_element_type=F32, precision=lax.Precision.HIGHEST) + carry[...]
        carry[...] += jnp.sum(df, axis=0, keepdims=True)
        o_ref[...] = (rc * (1.0 / (1.0 + jnp.exp(x_ref[...])))).astype(o_ref.dtype)

    return pl.pallas_call(
        body, grid=(nt,),
        in_specs=[pl.BlockSpec((ts, wd), lambda i: (nt - 1 - i, 0)),
                  pl.BlockSpec((ts, wd), lambda i: (nt - 1 - i, 0)),
                  pl.BlockSpec((ts, LANES), lambda i: (nt - 1 - i, cb))],
        out_specs=pl.BlockSpec((ts, LANES), lambda i: (nt - 1 - i, 0)),
        out_shape=jax.ShapeDtypeStruct((s, LANES), BF16),
        scratch_shapes=[pltpu.VMEM((1, LANES), F32)],
        compiler_params=_cparams(("arbitrary",)), name="forget_cumsum_bwd")(dq, dk, z)


def _flash_fwd(q, k, v, scale, nh, l_lane, *, name):
    s = q.shape[0]
    t = min(ATT_TILE, s)
    nq = s // t
    c = scale * LOG2E

    def body(q_ref, k_ref, v_ref, o_ref, qb_ref):
        i = pl.program_id(1)
        qb = q_ref[...]

        def scores(j):
            kb = k_ref[pl.ds(pl.multiple_of(j * t, t), t), :]
            return lax.dot_general(qb, kb, _NT, preferred_element_type=F32)

        def update(j, sc, carry):
            m, acc = carry
            m_new = jnp.maximum(m, jnp.max(sc, axis=1, keepdims=True))
            p = jnp.exp2((sc - m_new) * c)
            alpha = jnp.exp2((m - m_new) * c)
            vb = v_ref[pl.ds(pl.multiple_of(j * t, t), t), :]
            acc = alpha * acc + jnp.dot(p.astype(BF16), vb, preferred_element_type=F32)
            return m_new, acc

        def group(jj, cr):
            for n in range(FWD_GROUP):
                cr = update(FWD_GROUP * jj + n, scores(FWD_GROUP * jj + n), cr)
            return cr

        init = (jnp.full((t, 1), NEG, F32), jnp.zeros((t, LANES), F32))
        n_groups = i >> FWD_GROUP_LOG2
        carry = lax.fori_loop(0, n_groups, group, init)
        carry = lax.fori_loop(n_groups * FWD_GROUP, i, lambda j, cr: update(j, scores(j), cr), carry)
        row = lax.broadcasted_iota(jnp.int32, (t, t), 0)
        col = lax.broadcasted_iota(jnp.int32, (t, t), 1)
        m, acc = update(i, jnp.where(col <= row, scores(i), NEG), carry)
        lane = _lane((t, LANES))
        l = _lane_column(acc, lane, V_ONES)
        o_ref[...] = (acc / l).astype(o_ref.dtype)
        big_l = m + jnp.log(l) / scale
        qb_ref[...] = _put3(qb.astype(F32), lane, l_lane, _split3(-big_l)).astype(qb_ref.dtype)

    head_rows = pl.BlockSpec((t, LANES), lambda h, i: (i, h))
    head_all = pl.BlockSpec((s, LANES), lambda h, i: (0, h))
    return pl.pallas_call(
        body, grid=(nh, nq), in_specs=[head_rows, head_all, head_all], out_specs=[head_rows, head_rows],
        out_shape=[jax.ShapeDtypeStruct(q.shape, BF16), jax.ShapeDtypeStruct(q.shape, BF16)],
        compiler_params=_cparams(("parallel", "arbitrary")), name=name)(q, k, v)


def _flash_bwd(qb, k, v, do, scale, nh, n_feat, *, name):
    s = qb.shape[0]
    t = min(ATT_TILE, s)
    nq = s // t
    c = scale * LOG2E

    def body(q_ref, k_ref, v_ref, do_ref, dk_ref, dv_ref, dq_ref):
        j = pl.program_id(1)
        kb = k_ref[...]
        vb = v_ref[...]

        @pl.when(j == 0)
        def _():
            dq_ref[...] = jnp.zeros_like(dq_ref)

        def block(i, carry, masked):
            dk_acc, dv_acc = carry
            rows = pl.ds(pl.multiple_of(i * t, t), t)
            qblk = q_ref[rows, :]
            dob = do_ref[rows, :]
            st = lax.dot_general(kb, qblk, _NT, preferred_element_type=F32)
            if masked:
                key = lax.broadcasted_iota(jnp.int32, (t, t), 0)
                qry = lax.broadcasted_iota(jnp.int32, (t, t), 1)
                st = jnp.where(key <= qry, st, NEG)
            pt = jnp.exp2(st * c)
            dv_acc = dv_acc + jnp.dot(pt.astype(BF16), dob, preferred_element_type=F32)
            dpt = lax.dot_general(vb, dob, _NT, preferred_element_type=F32)
            dsb = (pt * dpt).astype(BF16)
            dk_acc = dk_acc + jnp.dot(dsb, qblk, preferred_element_type=F32)
            dq_ref[rows, :] += lax.dot_general(dsb, kb, _TN, preferred_element_type=F32)
            return dk_acc, dv_acc

        init = (jnp.zeros((t, LANES), F32), jnp.zeros((t, LANES), F32))
        carry = block(j, init, True)
        rest = nq - 1 - j
        carry = lax.cond((rest & 1) == 1, lambda cr: block(j + 1, cr, False), lambda cr: cr, carry)
        first = j + 1 + (rest & 1)

        def pair(ii, cr):
            i0 = first + 2 * ii
            return block(i0 + 1, block(i0, cr, False), False)

        dk_acc, dv_acc = lax.fori_loop(0, rest >> 1, pair, carry)
        dk_ref[...] = dk_acc * jnp.where(_lane((t, LANES)) < n_feat, scale, 1.0)
        dv_ref[...] = dv_acc

        @pl.when(j == nq - 1)
        def _():
            dq_ref[...] = dq_ref[...] * jnp.where(_lane((s, LANES)) < n_feat, scale, 1.0)

    head_rows = pl.BlockSpec((t, LANES), lambda h, j: (j, h))
    head_all = pl.BlockSpec((s, LANES), lambda h, j: (0, h))
    shp = jax.ShapeDtypeStruct(qb.shape, F32)
    return pl.pallas_call(
        body, grid=(nh, nq), in_specs=[head_all, head_rows, head_rows, head_all],
        out_specs=[head_rows, head_rows, head_all], out_shape=[shp, shp, shp],
        compiler_params=_cparams(("parallel", "arbitrary")), name=name)(qb, k, v, do)


def _with_delta(do, o, nh, *, name):
    def fn(ti, pa):
        lane = _lane((ti[0].shape[0], LANES))
        out = []
        for d_blk, o_blk in zip(_blocks(ti[0], nh), _blocks(ti[1], nh), strict=True):
            delta = jnp.sum(d_blk * o_blk.astype(F32), axis=1, keepdims=True)
            out.append(_put3(d_blk, lane, V_ONES, _split3(-delta)))
        return [jnp.concatenate(out, axis=1)], []

    (res,) = _rowwise(name, fn, [do, o], [], [(do.shape[1], BF16)])
    return res


def _pad_heads(a, nh):
    d = a.shape[-1] // nh
    a = a.reshape(a.shape[:-1] + (nh, d))
    a = jnp.pad(a, [(0, 0)] * (a.ndim - 1) + [(0, LANES - d)])
    return a.reshape(a.shape[:-2] + (nh * LANES,))


def _unpad_heads(a, nh, d):
    a = a.reshape(a.shape[:-1] + (nh, LANES))[..., :d]
    return a.reshape(a.shape[:-2] + (nh * d,))


def _pad_head_rows(w, nh):
    return _pad_heads(w.T, nh).T


def _unpad_head_rows(g, nh, d):
    return _unpad_heads(g.T, nh, d).T


class _ZLayout:
    def __init__(self, d):
        fw = FOX_HEADS * FOX_DIM
        self.src = {}
        off = 0
        for nm, w in (("cq", MLA_Q_LORA), ("ckv", MLA_KV_LORA), ("kr", MLA_ROPE), ("fq", fw), ("fk", fw),
                      ("fv", fw), ("fl", FOX_HEADS), ("ga", d), ("gb", d)):
            self.src[nm] = (off, w)
            off += w
        self.dst = {}
        off = 0
        for nm, w in (("fq", FOX_HEADS * LANES), ("fk", FOX_HEADS * LANES), ("fv", FOX_HEADS * LANES), ("ga", d),
                      ("gb", d), ("cq", MLA_Q_LORA), ("ckv", MLA_KV_LORA), ("kr", LANES), ("fl", LANES)):
            assert off % w == 0
            self.dst[nm] = (off, w)
            off += w
        self.width = off

    def win(self, z, nm):
        off, w = self.dst[nm]
        return (z, off // w, w)

    def to_kernel(self, w):
        def seg(nm):
            off, wd = self.src[nm]
            return w[..., off:off + wd]

        def pad(a, left, total):
            return jnp.pad(a, [(0, 0)] * (a.ndim - 1) + [(left, total - left - a.shape[-1])])

        return jnp.concatenate([_pad_heads(seg("fq"), FOX_HEADS), _pad_heads(seg("fk"), FOX_HEADS),
                                _pad_heads(seg("fv"), FOX_HEADS), seg("ga"), seg("gb"), seg("cq"), seg("ckv"),
                                pad(seg("kr"), MLA_NOPE, LANES), pad(seg("fl"), 0, LANES)], axis=-1)

    def from_kernel(self, g):
        def seg(nm, lo=0, hi=None):
            off, wd = self.dst[nm]
            return g[..., off + lo:off + (wd if hi is None else hi)]

        return jnp.concatenate([seg("cq"), seg("ckv"), seg("kr", MLA_NOPE, MLA_NOPE + MLA_ROPE),
                                _unpad_heads(seg("fq"), FOX_HEADS, FOX_DIM), _unpad_heads(seg("fk"), FOX_HEADS, FOX_DIM),
                                _unpad_heads(seg("fv"), FOX_HEADS, FOX_DIM), seg("fl", 0, FOX_HEADS), seg("ga"),
                                seg("gb")], axis=-1)


def _local_step(x, positions, target, wts, late_weights, early_grads):
    s, d = x.shape
    zl = _ZLayout(d)
    hw = MLA_HEADS * LANES
    assert MLA_HEADS == FOX_HEADS
    scale_mla = (MLA_NOPE + MLA_ROPE) ** -0.5
    scale_fox = FOX_DIM ** -0.5

    inv_freq = ROPE_THETA ** (-jnp.arange(HALF_ROPE, dtype=F32) / HALF_ROPE)
    ang = positions.astype(F32)[:, None] * inv_freq
    cos, sin = jnp.cos(ang), jnp.sin(ang)
    tail = jnp.zeros((s, LANES - MLA_NOPE - MLA_ROPE), F32)
    rc = jnp.concatenate([jnp.ones((s, MLA_NOPE), F32), cos, cos, tail], axis=1)
    ra = jnp.concatenate([jnp.zeros((s, MLA_NOPE), F32), -sin, jnp.zeros((s, HALF_ROPE), F32), tail], axis=1)
    rb = jnp.concatenate([jnp.zeros((s, MLA_NOPE + HALF_ROPE), F32), sin, tail], axis=1)

    w_in = zl.to_kernel(wts["w_in"])
    b_in = zl.to_kernel(wts["b_in"])

    def f_norm_in(ti, pa):
        y, _ = _rms(ti[0], pa[0])
        return [y], []

    (h,) = _rowwise("norm_in", f_norm_in, [x], [wts["ln_pre_mix"]], [(d, BF16)])
    z = _mm(h, w_in, bias=b_in, out_dtype=F32, name="proj_in")
    wts = {**wts, **late_weights("mixer", z)}
    w_uq = _pad_heads(wts["w_uq"], MLA_HEADS)
    w_ukv = jnp.concatenate([_pad_heads(wts["w_uk"], MLA_HEADS), _pad_heads(wts["w_uv"], MLA_HEADS)], axis=1)
    w_o_mla = _pad_head_rows(wts["w_o_mla"], MLA_HEADS)
    w_o_fox = _pad_head_rows(wts["w_o_fox"], FOX_HEADS)

    def f_mla_norms(ti, pa):
        cqn, _ = _rms(ti[0], pa[0])
        ckvn, _ = _rms(ti[1], pa[1])
        return [cqn, ckvn], []

    cqn, ckvn = _rowwise("mla_norms", f_mla_norms, [zl.win(z, "cq"), zl.win(z, "ckv")],
                         [wts["q_a_norm"], wts["kv_a_norm"]], [(MLA_Q_LORA, BF16), (MLA_KV_LORA, BF16)])
    qf = _mm(cqn, w_uq, out_dtype=F32, name="proj_uq")
    kv = _mm(ckvn, w_ukv, out_dtype=BF16, name="proj_ukv")

    def f_rope_q(ti, pa):
        xq, c_, a_, b_ = ti
        return [jnp.concatenate([_rope_block(blk, c_, a_, b_) for blk in _blocks(xq, MLA_HEADS)], axis=1)], []

    (q_mla,) = _rowwise("rope_q", f_rope_q, [qf, rc, ra, rb], [], [(hw, BF16)])

    def f_mla_kv(ti, pa):
        kn, vn, kr, c_, a_, b_ = ti
        lane = _lane(kr.shape)
        k_tail = jnp.where((lane >= MLA_Q_L) & (lane < MLA_Q_L + 3), 1.0, _rope_block(kr, c_, a_, b_))
        ones_v = (lane >= V_ONES) & (lane < V_ONES + 3)
        k_out = [jnp.where(lane < MLA_NOPE, blk.astype(F32), k_tail) for blk in _blocks(kn, MLA_HEADS)]
        v_out = [jnp.where(ones_v, 1.0, blk.astype(F32)) for blk in _blocks(vn, MLA_HEADS)]
        return [jnp.concatenate(k_out, axis=1), jnp.concatenate(v_out, axis=1)], []

    k_mla, v_mla = _rowwise("mla_kv", f_mla_kv, [(kv, 0, hw), (kv, 1, hw), zl.win(z, "kr"), rc, ra, rb], [],
                            [(hw, BF16), (hw, BF16)])
    o_mla, qb_mla = _flash_fwd(q_mla, k_mla, v_mla, scale_mla, MLA_HEADS, MLA_Q_L, name="mla_fwd")

    fl_cb = zl.dst["fl"][0] // LANES
    fcol = _forget_cumsum(z, fl_cb)

    def f_fox_qkv(ti, pa):
        fq, fk, fv, fc = ti
        lane = _lane(fc.shape)
        ones_q = (lane >= FOX_Q_ONES) & (lane < FOX_Q_ONES + 3)
        ones_k = (lane >= FOX_Q_F) & (lane < FOX_Q_ONES)
        ones_v = (lane >= V_ONES) & (lane < V_ONES + 3)
        q_out, k_out, v_out = [], [], []
        for hh, (qblk, kblk, vblk) in enumerate(zip(_blocks(fq, FOX_HEADS), _blocks(fk, FOX_HEADS),
                                                    _blocks(fv, FOX_HEADS), strict=True)):
            f_h = _lane_column(fc, lane, hh) * (1.0 / scale_fox)
            q_out.append(_put3(jnp.where(ones_q, 1.0, qblk), lane, FOX_Q_F, _split3(f_h)))
            k_out.append(_put3(jnp.where(ones_k, 1.0, kblk), lane, FOX_Q_ONES, _split3(-f_h)))
            v_out.append(jnp.where(ones_v, 1.0, vblk))
        return [jnp.concatenate(q_out, axis=1), jnp.concatenate(k_out, axis=1), jnp.concatenate(v_out, axis=1)], []

    q_fox, k_fox, v_fox = _rowwise("fox_qkv", f_fox_qkv, [zl.win(z, "fq"), zl.win(z, "fk"), zl.win(z, "fv"), fcol],
                                   [], [(hw, BF16)] * 3)
    o_fox, qb_fox = _flash_fwd(q_fox, k_fox, v_fox, scale_fox, FOX_HEADS, FOX_Q_L, name="fox_fwd")

    y_mla = _mm(o_mla, w_o_mla, out_dtype=BF16, name="proj_o_mla")
    y_fox = _mm(o_fox, w_o_fox, out_dtype=BF16, name="proj_o_fox")

    def f_gate(ti, pa):
        ga, gb, ya, yb = ti
        return [_sigmoid(ga) * ya + _sigmoid(gb) * yb], []

    (merged,) = _rowwise("gate", f_gate, [zl.win(z, "ga"), zl.win(z, "gb"), y_mla, y_fox], [], [(d, BF16)])
    wts = {**wts, **late_weights("mlp", merged)}
    mix = _mm(merged, wts["w_out"], out_dtype=F32, name="proj_out")

    def f_resid1(ti, pa):
        xa, mx = ti
        y, _ = _rms(mx, pa[0])
        x1 = xa + y
        h2, _ = _rms(x1, pa[1])
        return [x1, h2], []

    x1, h2 = _rowwise("resid_mix", f_resid1, [x, mix], [wts["ln_post_mix"], wts["ln_pre_mlp"]], [(d, F32), (d, BF16)])

    def relu2(acc, ex):
        r = jnp.maximum(acc, 0.0)
        return [acc, r * r]

    u, act = _mm(h2, wts["w_ff1"], out_dtype=[BF16, BF16], name="ff1", epilogue=relu2)
    mo = _mm(act, wts["w_ff2"], out_dtype=F32, name="ff2")

    def f_loss(ti, pa):
        xa, mv, tg = ti
        y, _ = _rms(mv, pa[0])
        err = (xa + y) - tg
        g2 = err / d
        dmo, dg = _rms_bwd(mv, pa[0], g2)
        return [g2, dmo], [jnp.sum(err * err, axis=0, keepdims=True), dg]

    g2, d_mo, loss_cols, g_ln_post_mlp = _rowwise("loss", f_loss, [x1, mo, target], [wts["ln_post_mlp"]],
                                                  [(d, F32), (d, BF16)], [d, d])
    loss = 0.5 * jnp.sum(loss_cols) / d

    grads = {"ln_post_mlp": g_ln_post_mlp}
    grads["w_ff2"] = _mm_tn(act, d_mo, name="grad_ff2")

    def relu2_bwd(acc, ex):
        return [acc * (2.0 * jnp.maximum(ex[0], 0.0))]

    (d_u,) = _mm(d_mo, wts["w_ff2"], out_dtype=[BF16], name="ff2_bwd", transpose_b=True, extras=[u], epilogue=relu2_bwd)
    grads["w_ff1"] = _mm_tn(h2, d_u, name="grad_ff1")
    d_h2 = _mm(d_u, wts["w_ff1"], out_dtype=F32, name="ff1_bwd", transpose_b=True)

    def f_resid1_bwd(ti, pa):
        gres, dh2, x1v, mx = ti
        dx1n, dg_pre_mlp = _rms_bwd(x1v, pa[1], dh2)
        dx1 = gres + dx1n
        dmix, dg_post_mix = _rms_bwd(mx, pa[0], dx1)
        return [dx1, dmix], [dg_post_mix, dg_pre_mlp]

    d_x1, d_mix, grads["ln_post_mix"], grads["ln_pre_mlp"] = _rowwise(
        "resid_mix_bwd", f_resid1_bwd, [g2, d_h2, x1, mix], [wts["ln_post_mix"], wts["ln_pre_mlp"]],
        [(d, F32), (d, BF16)], [d, d])
    grads["w_out"] = _mm_tn(merged, d_mix, name="grad_out")
    d_merged = _mm(d_mix, wts["w_out"], out_dtype=BF16, name="proj_out_bwd", transpose_b=True)

    def f_gate_bwd(ti, pa):
        dm, ga, gb, ya, yb = ti
        sa, sb = _sigmoid(ga), _sigmoid(gb)
        return [dm * sa, dm * sb, dm * ya * (sa * (1.0 - sa)), dm * yb * (sb * (1.0 - sb))], []

    d_ya, d_yb, d_ga, d_gb = _rowwise("gate_bwd", f_gate_bwd,
                                      [d_merged, zl.win(z, "ga"), zl.win(z, "gb"), y_mla, y_fox], [],
                                      [(d, BF16)] * 4)
    grads["w_o_mla"] = _unpad_head_rows(_mm_tn(o_mla, d_ya, name="grad_o_mla"), MLA_HEADS, MLA_V)
    grads["w_o_fox"] = _unpad_head_rows(_mm_tn(o_fox, d_yb, name="grad_o_fox"), FOX_HEADS, FOX_DIM)
    w_o_mla = w_o_mla + early_grads({nm: grads[nm] for nm in EARLY_GRADS}).astype(BF16)
    do_mla = _with_delta(_mm(d_ya, w_o_mla, out_dtype=F32, name="proj_o_mla_bwd", transpose_b=True), o_mla,
                         MLA_HEADS, name="mla_delta")
    do_fox = _with_delta(_mm(d_yb, w_o_fox, out_dtype=F32, name="proj_o_fox_bwd", transpose_b=True), o_fox,
                         FOX_HEADS, name="fox_delta")

    dk_mla, dv_mla, dq_mla = _flash_bwd(qb_mla, k_mla, v_mla, do_mla, scale_mla, MLA_HEADS, MLA_NOPE + MLA_ROPE,
                                        name="mla_bwd")
    dk_fox, dv_fox, dq_fox = _flash_bwd(qb_fox, k_fox, v_fox, do_fox, scale_fox, FOX_HEADS, FOX_DIM, name="fox_bwd")
    d_fl = _forget_cumsum_bwd(dq_fox, dk_fox, z, fl_cb, FOX_HEADS)

    def f_rope_q_bwd(ti, pa):
        g, c_, a_, b_ = ti
        return [jnp.concatenate([_rope_block(blk, c_, -a_, -b_) for blk in _blocks(g, MLA_HEADS)], axis=1)], []

    (d_qf,) = _rowwise("rope_q_bwd", f_rope_q_bwd, [dq_mla, rc, ra, rb], [], [(hw, BF16)])
    grads["w_uq"] = _unpad_heads(_mm_tn(cqn, d_qf, name="grad_uq"), MLA_HEADS, MLA_NOPE + MLA_ROPE)
    d_cqn = _mm(d_qf, w_uq, out_dtype=F32, name="proj_uq_bwd", transpose_b=True)

    def f_mla_kv_bwd(ti, pa):
        gk, gv, c_, a_, b_ = ti
        k_blocks = _blocks(gk, MLA_HEADS)
        tot = k_blocks[0]
        for blk in k_blocks[1:]:
            tot = tot + blk
        return [jnp.concatenate([gk, gv], axis=1), _rope_block(tot, c_, -a_, -b_)], []

    d_kv, d_kr = _rowwise("mla_kv_bwd", f_mla_kv_bwd, [dk_mla, dv_mla, rc, ra, rb], [], [(2 * hw, BF16), (LANES, BF16)])
    g_ukv = _mm_tn(ckvn, d_kv, name="grad_ukv")
    grads["w_uk"] = _unpad_heads(g_ukv[:, :hw], MLA_HEADS, MLA_NOPE)
    grads["w_uv"] = _unpad_heads(g_ukv[:, hw:], MLA_HEADS, MLA_V)
    d_ckvn = _mm(d_kv, w_ukv, out_dtype=F32, name="proj_ukv_bwd", transpose_b=True)

    def f_mla_norms_bwd(ti, pa):
        cq, ckv, dcqn, dckvn = ti
        dcq, dg_q = _rms_bwd(cq, pa[0], dcqn)
        dckv, dg_kv = _rms_bwd(ckv, pa[1], dckvn)
        return [dcq, dckv], [dg_q, dg_kv]

    d_cq, d_ckv, grads["q_a_norm"], grads["kv_a_norm"] = _rowwise(
        "mla_norms_bwd", f_mla_norms_bwd, [zl.win(z, "cq"), zl.win(z, "ckv"), d_cqn, d_ckvn],
        [wts["q_a_norm"], wts["kv_a_norm"]], [(MLA_Q_LORA, BF16), (MLA_KV_LORA, BF16)], [MLA_Q_LORA, MLA_KV_LORA])

    d_z = jnp.concatenate([dq_fox.astype(BF16), dk_fox.astype(BF16), dv_fox.astype(BF16), d_ga, d_gb, d_cq, d_ckv,
                           d_kr, d_fl], axis=1)
    assert d_z.shape[1] == zl.width

    def f_bias(ti, pa):
        return [], [jnp.sum(ti[0].astype(F32), axis=0, keepdims=True)]

    (g_b_in,) = _rowwise("grad_b_in", f_bias, [d_z], [], [], [zl.width])
    grads["b_in"] = zl.from_kernel(g_b_in)
    grads["w_in"] = zl.from_kernel(_mm_tn(h, d_z, name="grad_in"))
    d_h = _mm(d_z, w_in, out_dtype=F32, name="proj_in_bwd", transpose_b=True)

    def f_norm_in_bwd(ti, pa):
        dx1v, dh, xa = ti
        dxn, dg = _rms_bwd(xa, pa[0], dh)
        return [dx1v + dxn], [dg]

    grad_x, grads["ln_pre_mix"] = _rowwise("norm_in_bwd", f_norm_in_bwd, [d_x1, d_h, x], [wts["ln_pre_mix"]],
                                           [(d, F32)], [d])
    return loss, grad_x, grads


class _PackLayout:
    def __init__(self, shapes):
        self.shapes = list(shapes)
        self.width = _round_up(max(b for _, b in shapes), LANES)
        self.bands = []
        row = 0
        shelf = []
        for idx, (a, b) in enumerate(shapes):
            if 2 * _round_up(b, LANES) > self.width:
                self.bands.append((row, _round_up(a, 32), [(idx, 0)]))
                row += _round_up(a, 32)
            else:
                shelf.append(idx)
        col, items = 0, []
        for idx in shelf:
            wb = _round_up(shapes[idx][1], LANES)
            if col + wb > self.width:
                hgt = max(_round_up(shapes[i][0], 32) for i, _ in items)
                self.bands.append((row, hgt, items))
                row += hgt
                col, items = 0, []
            items.append((idx, col))
            col += wb
        if items:
            hgt = max(_round_up(shapes[i][0], 32) for i, _ in items)
            self.bands.append((row, hgt, items))
            row += hgt
        self.rows = _round_up(row, 16 * LOCAL_PIECES)

    def pack(self, arrs):
        lead = arrs[0].shape[:-2]
        no_pad = [(0, 0)] * len(lead)
        bands = []
        for _, hgt, items in self.bands:
            parts = []
            for k, (idx, col) in enumerate(items):
                a, b = self.shapes[idx]
                nxt = items[k + 1][1] if k + 1 < len(items) else self.width
                parts.append(jnp.pad(arrs[idx], no_pad + [(0, hgt - a), (0, nxt - col - b)]))
            bands.append(parts[0] if len(parts) == 1 else jnp.concatenate(parts, axis=-1))
        used = sum(hgt for _, hgt, _ in self.bands)
        if used < self.rows:
            bands.append(jnp.zeros(lead + (self.rows - used, self.width), arrs[0].dtype))
        return jnp.concatenate(bands, axis=-2)

    def unpack(self, packed):
        out = [None] * len(self.shapes)
        for row, _, items in self.bands:
            for idx, col in items:
                a, b = self.shapes[idx]
                out[idx] = packed[..., row:row + a, col:col + b]
        return out


def _to_shards(g, axis):
    if axis == 0:
        return g.reshape(N_CHIPS, g.shape[0] // N_CHIPS, g.shape[1])
    return jnp.stack(jnp.split(g, N_CHIPS, axis=1))


def _from_shards(s4, axis):
    n, a, b = s4.shape
    if axis == 0:
        return s4.reshape(n * a, b)
    return jnp.concatenate([s4[ch] for ch in range(n)], axis=1)


ANY = pl.BlockSpec(memory_space=pl.ANY)


def _place():
    return lax.axis_index("x"), lax.axis_index("y"), lax.axis_index("c")


def _gather_weights(wpk):
    rows, wd = wpk.shape
    half = rows // 2

    def body(w_ref, out_ref, send_sems, recv_sems, local_sems):
        x, y, c = _place()
        sibling = (x, y, 1 - c)
        chips = [(1 - x, y), (x, 1 - y), (1 - x, 1 - y)]

        def slab(chip, hf):
            return out_ref.at[2 * chip[0] + chip[1], pl.ds(hf * half, half), :]

        def copy(k, chip, hf, to, src=None):
            return pltpu.make_async_remote_copy(
                src_ref=slab(chip, hf) if src is None else src, dst_ref=slab(chip, hf),
                send_sem=send_sems.at[k], recv_sem=recv_sems.at[k], device_id=to, device_id_type=MESH)

        piece = rows // LOCAL_PIECES
        mine = [pltpu.make_async_copy(w_ref.at[pl.ds(n * piece, piece), :],
                                      out_ref.at[2 * x + y, pl.ds(n * piece, piece), :], local_sems.at[n])
                for n in range(LOCAL_PIECES)]
        for cp in mine:
            cp.start()
        first = [copy(j, (x, y), c, (*chip, c), src=w_ref.at[pl.ds(c * half, half), :]) for j, chip in enumerate(chips)]
        for cp in first:
            cp.start()
        passed = [copy(3 + j, chip, c, sibling) for j, chip in enumerate(chips)]
        for j, chip in enumerate(chips):
            copy(j, chip, c, (x, y, c)).wait_recv()
            passed[j].start()
        for j, chip in enumerate(chips):
            copy(3 + j, chip, 1 - c, (x, y, c)).wait_recv()
        for cp in first + passed:
            cp.wait_send()
        for cp in mine:
            cp.wait()

    assert rows % (16 * LOCAL_PIECES) == 0
    return pl.pallas_call(
        body, out_shape=jax.ShapeDtypeStruct((N_CHIPS, rows, wd), wpk.dtype),
        in_specs=[ANY], out_specs=ANY,
        scratch_shapes=[pltpu.SemaphoreType.DMA((6,)), pltpu.SemaphoreType.DMA((6,)),
                        pltpu.SemaphoreType.DMA((LOCAL_PIECES,))],
        name="gather_weights")(wpk)


HBM = pl.BlockSpec(memory_space=pltpu.HBM)
SEM = pl.BlockSpec(memory_space=pltpu.SEMAPHORE)
EFFECT = pltpu.SideEffectType.DATAFLOW_SIDE_EFFECTING
N_LATE = 6


def _gather_late_start(wpk, tag):
    rows, wd = wpk.shape
    half = rows // 2

    def body(w_ref, land_ref, send_sems, recv_sems, w_thru, land_thru, token):
        x, y, c = _place()
        chips = [(1 - x, y), (x, 1 - y), (1 - x, 1 - y)]
        for j, chip in enumerate(chips):
            for to_core in range(2):
                pltpu.make_async_remote_copy(
                    src_ref=w_ref.at[pl.ds(c * half, half), :],
                    dst_ref=land_ref.at[2 * x + y, pl.ds(c * half, half), :],
                    send_sem=send_sems.at[2 * j + to_core], recv_sem=recv_sems.at[2 * j + c],
                    device_id=(*chip, to_core), device_id_type=MESH).start()
        token[...] = jnp.zeros_like(token)

    land = lax.empty((N_CHIPS, rows, wd), wpk.dtype)
    return pl.pallas_call(
        body, name="gather_" + tag + "_start",
        out_shape=(pltpu.SemaphoreType.DMA((N_LATE,)), pltpu.SemaphoreType.DMA((N_LATE,)),
                   pltpu.HBM(wpk.shape, wpk.dtype), pltpu.HBM(land.shape, land.dtype),
                   jax.ShapeDtypeStruct((8, LANES), F32)),
        in_specs=(HBM, HBM), out_specs=(SEM, SEM, HBM, HBM, pl.BlockSpec(memory_space=pltpu.VMEM)),
        input_output_aliases={0: 2, 1: 3},
        compiler_params=pltpu.CompilerParams(has_side_effects=EFFECT),
    )(pltpu.with_memory_space_constraint(wpk, pltpu.HBM), pltpu.with_memory_space_constraint(land, pltpu.HBM))


def _gather_late_wait(send_sems, recv_sems, w_thru, land_thru, after, tag):
    rows, wd = w_thru.shape
    half = rows // 2

    def body(w_ref, land_ref, send_sems, recv_sems, after_ref, w_dead, land_out):
        x, y, c = _place()
        for n in range(N_LATE):
            cp = pltpu.make_async_remote_copy(
                src_ref=w_ref.at[pl.ds(0, half), :], dst_ref=land_ref.at[0, pl.ds(0, half), :],
                send_sem=send_sems.at[n], recv_sem=recv_sems.at[n], device_id=(x, y, c), device_id_type=MESH)
            cp.wait_send()
            cp.wait_recv()

    return pl.pallas_call(
        body, name="gather_" + tag + "_wait",
        out_shape=(pltpu.HBM(w_thru.shape, w_thru.dtype), pltpu.HBM(land_thru.shape, land_thru.dtype)),
        in_specs=(HBM, HBM, SEM, SEM, ANY), out_specs=(HBM, HBM), input_output_aliases={0: 0, 1: 1},
        compiler_params=pltpu.CompilerParams(has_side_effects=EFFECT),
    )(w_thru, land_thru, send_sems, recv_sems, after)[1]


N_PART = 7


def _reduce_early_start(gbf):
    _, _, hrows, wd = gbf.shape

    def body(g_ref, land_ref, send_sems, recv_sems, g_thru, land_thru, token):
        x, y, c = _place()
        chips = [(1 - x, y), (x, 1 - y), (1 - x, 1 - y)]
        for j, chip in enumerate(chips):
            for to_core in range(2):
                pltpu.make_async_remote_copy(
                    src_ref=g_ref.at[2 * chip[0] + chip[1], to_core], dst_ref=land_ref.at[2 * j + c],
                    send_sem=send_sems.at[2 * j + to_core], recv_sem=recv_sems.at[2 * j + c],
                    device_id=(*chip, to_core), device_id_type=MESH).start()
        pltpu.make_async_remote_copy(
            src_ref=g_ref.at[2 * x + y, 1 - c], dst_ref=land_ref.at[N_PART - 1],
            send_sem=send_sems.at[N_PART - 1], recv_sem=recv_sems.at[N_PART - 1],
            device_id=(x, y, 1 - c), device_id_type=MESH).start()
        token[...] = jnp.zeros_like(token)

    land = lax.empty((N_PART, hrows, wd), gbf.dtype)
    return pl.pallas_call(
        body, name="reduce_early_start",
        out_shape=(pltpu.SemaphoreType.DMA((N_PART,)), pltpu.SemaphoreType.DMA((N_PART,)),
                   pltpu.HBM(gbf.shape, gbf.dtype), pltpu.HBM(land.shape, land.dtype),
                   jax.ShapeDtypeStruct((8, LANES), F32)),
        in_specs=(HBM, HBM), out_specs=(SEM, SEM, HBM, HBM, pl.BlockSpec(memory_space=pltpu.VMEM)),
        input_output_aliases={0: 2, 1: 3},
        compiler_params=pltpu.CompilerParams(has_side_effects=EFFECT),
    )(pltpu.with_memory_space_constraint(gbf, pltpu.HBM), pltpu.with_memory_space_constraint(land, pltpu.HBM))


def _reduce_early_wait(send_sems, recv_sems, g_thru, land_thru, after):
    def body(g_ref, land_ref, send_sems, recv_sems, after_ref, g_dead, land_out):
        x, y, c = _place()
        for n in range(N_PART):
            cp = pltpu.make_async_remote_copy(
                src_ref=g_ref.at[0, 0], dst_ref=land_ref.at[0], send_sem=send_sems.at[n], recv_sem=recv_sems.at[n],
                device_id=(x, y, c), device_id_type=MESH)
            cp.wait_send()
            cp.wait_recv()

    return pl.pallas_call(
        body, name="reduce_early_wait",
        out_shape=(pltpu.HBM(g_thru.shape, g_thru.dtype), pltpu.HBM(land_thru.shape, land_thru.dtype)),
        in_specs=(HBM, HBM, SEM, SEM, ANY), out_specs=(HBM, HBM), input_output_aliases={0: 0, 1: 1},
        compiler_params=pltpu.CompilerParams(has_side_effects=EFFECT),
    )(g_thru, land_thru, send_sems, recv_sems, after)[1]


def _sibling_exchange(g):
    n, rows, wd = g.shape
    half = rows // 2

    def body(g_ref, out_ref, send_sem, recv_sem):
        x, y, c = _place()
        cp = pltpu.make_async_remote_copy(
            src_ref=g_ref.at[:, pl.ds((1 - c) * half, half), :], dst_ref=out_ref,
            send_sem=send_sem, recv_sem=recv_sem, device_id=(x, y, 1 - c), device_id_type=MESH)
        cp.start()
        cp.wait()

    return pl.pallas_call(
        body, out_shape=jax.ShapeDtypeStruct((n, half, wd), g.dtype), in_specs=[ANY], out_specs=ANY,
        scratch_shapes=[pltpu.SemaphoreType.DMA, pltpu.SemaphoreType.DMA], name="grad_sibling_exchange")(g)


def _chip_exchange(part):
    n, half, wd = part.shape

    def body(p_ref, out_ref, send_sems, recv_sems):
        x, y, c = _place()
        chips = [(1 - x, y), (x, 1 - y), (1 - x, 1 - y)]

        def copy(j, chip):
            return pltpu.make_async_remote_copy(
                src_ref=p_ref.at[2 * chip[0] + chip[1]], dst_ref=out_ref.at[j],
                send_sem=send_sems.at[j], recv_sem=recv_sems.at[j], device_id=(*chip, c), device_id_type=MESH)

        cps = [copy(j, chip) for j, chip in enumerate(chips)]
        for cp in cps:
            cp.start()
        for cp in cps:
            cp.wait()

    return pl.pallas_call(
        body, out_shape=jax.ShapeDtypeStruct((3, half, wd), part.dtype), in_specs=[ANY], out_specs=ANY,
        scratch_shapes=[pltpu.SemaphoreType.DMA((3,)), pltpu.SemaphoreType.DMA((3,))], name="grad_chip_exchange")(part)


def _sibling_swap(mine):
    def body(m_ref, out_ref, send_sem, recv_sem):
        x, y, c = _place()
        cp = pltpu.make_async_remote_copy(
            src_ref=m_ref, dst_ref=out_ref, send_sem=send_sem, recv_sem=recv_sem,
            device_id=(x, y, 1 - c), device_id_type=MESH)
        cp.start()
        cp.wait()

    return pl.pallas_call(
        body, out_shape=jax.ShapeDtypeStruct(mine.shape, mine.dtype), in_specs=[ANY], out_specs=ANY,
        scratch_shapes=[pltpu.SemaphoreType.DMA, pltpu.SemaphoreType.DMA], name="grad_sibling_swap")(mine)


def _adamw(w, g, m, v):
    m = ADAM_B1 * m + (1.0 - ADAM_B1) * g
    v = ADAM_B2 * v + (1.0 - ADAM_B2) * (g * g)
    m_hat = m / (1.0 - ADAM_B1 ** ADAM_STEP)
    v_hat = v / (1.0 - ADAM_B2 ** ADAM_STEP)
    delta = -ADAM_LR * (m_hat / (jnp.sqrt(v_hat) + ADAM_EPS) + ADAM_WD * w)
    return delta, m, v


def _small_allreduce_adamw(gs, ws, ms, vs):
    n_dev = 8
    n_par = len(gs)
    wd = PACK_W
    chunks = []
    for p, g in enumerate(gs):
        for off in range(0, g.shape[1], wd):
            chunks.append((p, len(chunks), off, min(wd, g.shape[1] - off)))
    rows = _round_up(len(chunks), 8)

    def body(*refs):
        g_refs, w_refs, m_refs, v_refs = (refs[k * n_par:(k + 1) * n_par] for k in range(4))
        go_refs, d_refs, mo_refs, vo_refs = (refs[(4 + k) * n_par:(5 + k) * n_par] for k in range(4))
        mine_ref, all_ref, send_sems, recv_sems = refs[8 * n_par:]
        x, y, c = _place()
        me, sibling = (x, y, c), (x, y, 1 - c)
        chips = [(1 - x, y), (x, 1 - y), (1 - x, 1 - y)]

        def slot(px, py, pc):
            return all_ref.at[4 * px + 2 * py + pc]

        def copy(k, block, to, src=None):
            return pltpu.make_async_remote_copy(
                src_ref=slot(*block) if src is None else src, dst_ref=slot(*block),
                send_sem=send_sems.at[k], recv_sem=recv_sems.at[k], device_id=to, device_id_type=MESH)

        mine_ref[...] = jnp.zeros_like(mine_ref)
        for p, row, off, width in chunks:
            mine_ref[row:row + 1, 0:width] = g_refs[p][:, off:off + width]
        all_ref[4 * x + 2 * y + c] = mine_ref[...]
        first = [copy(0, me, sibling, src=mine_ref)]
        first += [copy(1 + j, me, (*chip, c), src=mine_ref) for j, chip in enumerate(chips)]
        for cp in first:
            cp.start()
        passed = [copy(4 + j, (*chip, c), sibling) for j, chip in enumerate(chips)]
        for j, chip in enumerate(chips):
            copy(1 + j, (*chip, c), me).wait_recv()
            passed[j].start()
        copy(0, sibling, me).wait_recv()
        for j, chip in enumerate(chips):
            copy(4 + j, (*chip, 1 - c), me).wait_recv()
        for cp in first + passed:
            cp.wait_send()
        tot = jnp.zeros((rows, wd), F32)
        for dev in range(n_dev):
            tot = tot + all_ref[dev]
        mine_ref[...] = tot
        for p, row, off, width in chunks:
            cols = slice(off, off + width)
            g = mine_ref[row:row + 1, 0:width]
            delta, m_new, v_new = _adamw(w_refs[p][:, cols], g, m_refs[p][:, cols], v_refs[p][:, cols])
            go_refs[p][:, cols] = g
            d_refs[p][:, cols] = delta
            mo_refs[p][:, cols] = m_new
            vo_refs[p][:, cols] = v_new

    vm = pl.BlockSpec(memory_space=pltpu.VMEM)
    shp = [jax.ShapeDtypeStruct(g.shape, F32) for g in gs]
    res = pl.pallas_call(
        body, out_shape=shp * 4, in_specs=[vm] * (4 * n_par), out_specs=[vm] * (4 * n_par),
        scratch_shapes=[pltpu.VMEM((rows, wd), F32), pltpu.VMEM((n_dev, rows, wd), F32),
                        pltpu.SemaphoreType.DMA((7,)), pltpu.SemaphoreType.DMA((7,))],
        name="small_allreduce_adamw")(*gs, *ws, *ms, *vs)
    return [res[k * n_par:(k + 1) * n_par] for k in range(4)]


def kernel(x, positions, ln_pre_mix, ln_post_mix, ln_pre_mlp, ln_post_mlp, w_in, b_in, q_a_norm, w_uq, kv_a_norm, w_uk, w_uv, w_o_mla, w_o_fox, w_out, w_ff1, w_ff2, loss_target, m_ln_pre_mix, m_ln_post_mix, m_ln_pre_mlp, m_ln_post_mlp, m_w_in, m_b_in, m_q_a_norm, m_w_uq, m_kv_a_norm, m_w_uk, m_w_uv, m_w_o_mla, m_w_o_fox, m_w_out, m_w_ff1, m_w_ff2, v_ln_pre_mix, v_ln_post_mix, v_ln_pre_mlp, v_ln_post_mlp, v_w_in, v_b_in, v_q_a_norm, v_w_uq, v_kv_a_norm, v_w_uk, v_w_uv, v_w_o_mla, v_w_o_fox, v_w_out, v_w_ff1, v_w_ff2):
    w = dict(ln_pre_mix=ln_pre_mix, ln_post_mix=ln_post_mix, ln_pre_mlp=ln_pre_mlp, ln_post_mlp=ln_post_mlp, w_in=w_in,
             b_in=b_in, q_a_norm=q_a_norm, w_uq=w_uq, kv_a_norm=kv_a_norm, w_uk=w_uk, w_uv=w_uv, w_o_mla=w_o_mla,
             w_o_fox=w_o_fox, w_out=w_out, w_ff1=w_ff1, w_ff2=w_ff2)
    mom = dict(ln_pre_mix=m_ln_pre_mix, ln_post_mix=m_ln_post_mix, ln_pre_mlp=m_ln_pre_mlp, ln_post_mlp=m_ln_post_mlp,
               w_in=m_w_in, b_in=m_b_in, q_a_norm=m_q_a_norm, w_uq=m_w_uq, kv_a_norm=m_kv_a_norm, w_uk=m_w_uk,
               w_uv=m_w_uv, w_o_mla=m_w_o_mla, w_o_fox=m_w_o_fox, w_out=m_w_out, w_ff1=m_w_ff1, w_ff2=m_w_ff2)
    var = dict(ln_pre_mix=v_ln_pre_mix, ln_post_mix=v_ln_post_mix, ln_pre_mlp=v_ln_pre_mlp, ln_post_mlp=v_ln_post_mlp,
               w_in=v_w_in, b_in=v_b_in, q_a_norm=v_q_a_norm, w_uq=v_w_uq, kv_a_norm=v_kv_a_norm, w_uk=v_w_uk,
               w_uv=v_w_uv, w_o_mla=v_w_o_mla, w_o_fox=v_w_o_fox, w_out=v_w_out, w_ff1=v_w_ff1, w_ff2=v_w_ff2)

    big_names = [nm for nm, _ in BIG]
    c = lax.axis_index("c")
    chip = 2 * lax.axis_index("x") + lax.axis_index("y")

    axes = dict(BIG)

    def assemble(names, lay, gathered):
        return {nm: _from_shards(s4, axes[nm] - 1) for nm, s4 in zip(names, lay.unpack(gathered), strict=True)}

    groups = {"mixer": [nm for nm in big_names if nm in MIXER], "mlp": [nm for nm in big_names if nm in LATE]}
    first_names = [nm for nm in big_names if nm not in MIXER and nm not in LATE]
    full = {nm: wv for nm, wv in w.items() if nm in SMALL}
    travelling = {}
    for tag, names in groups.items():
        lay = _PackLayout([w[nm].shape[1:] for nm in names])
        shard = lay.pack([w[nm][0].astype(BF16) for nm in names])
        started = _gather_late_start(shard, tag)
        travelling[tag] = (names, lay, shard, started[:4])
        full["b_in"] = full["b_in"] + started[4][0, 0]
    lay_first = _PackLayout([w[nm].shape[1:] for nm in first_names])
    full.update(assemble(first_names, lay_first,
                         _gather_weights(lay_first.pack([w[nm][0].astype(BF16) for nm in first_names]))))

    def late_weights(tag, after):
        names, lay, shard, handles = travelling[tag]
        land = _gather_late_wait(*handles, after, tag)
        land = lax.dynamic_update_slice(land, shard[None], (chip, 0, 0))
        return assemble(names, lay, land)

    eg_names = [nm for nm in big_names if nm in EARLY_GRADS]
    lg_names = [nm for nm in big_names if nm not in EARLY_GRADS]
    lay_eg = _PackLayout([w[nm].shape[1:] for nm in eg_names])
    lay_lg = _PackLayout([w[nm].shape[1:] for nm in lg_names])
    pending = {}

    def shards_of(g, names, lay, dtype):
        return lay.pack([_to_shards(g[nm], axes[nm] - 1).astype(dtype) for nm in names])

    def early_grads(g):
        gbf = shards_of(g, eg_names, lay_eg, BF16).reshape(N_CHIPS, 2, lay_eg.rows // 2, lay_eg.width)
        started = _reduce_early_start(gbf)
        pending["handles"], tok = started[:4], started[4]
        own = lay_eg.pack([lax.dynamic_slice_in_dim(g[nm], chip * w[nm].shape[axes[nm]], w[nm].shape[axes[nm]],
                                                    axis=axes[nm] - 1) for nm in eg_names])
        pending["own"] = lax.dynamic_slice_in_dim(own, c * (lay_eg.rows // 2), lay_eg.rows // 2, axis=0)
        return tok[0, 0]

    loss_local, grad_x, grads = _local_step(x[0], positions[0], loss_target[0], full, late_weights, early_grads)
    loss = lax.psum(loss_local, ("x", "y", "c"))
    parts = _reduce_early_wait(*pending["handles"], grad_x)

    def f_add8(ti, pa):
        tot = ti[0]
        for part in ti[1:]:
            tot = tot + part.astype(F32)
        return [tot], []

    (red_early,) = _rowwise("grad_add_parts", f_add8, [pending["own"]] + [parts[n] for n in range(N_PART)], [],
                            [(lay_eg.width, F32)])

    rows, pack_w = lay_lg.rows, lay_lg.width
    half = rows // 2
    gpk = shards_of(grads, lg_names, lay_lg, F32)
    theirs = _sibling_exchange(gpk)
    mine_half = lax.dynamic_slice_in_dim(gpk, c * half, half, axis=1)

    def f_add2(ti, pa):
        tot = ti[0] + ti[1]
        return [tot, tot], []

    chip_part, chip_part_bf = _rowwise("grad_add_sibling", f_add2, [mine_half.reshape(N_CHIPS * half, pack_w),
                                                                  theirs.reshape(N_CHIPS * half, pack_w)], [],
                                       [(pack_w, F32), (pack_w, BF16)])
    others = _chip_exchange(chip_part_bf.reshape(N_CHIPS, half, pack_w))
    own = lax.dynamic_index_in_dim(chip_part.reshape(N_CHIPS, half, pack_w), chip, axis=0, keepdims=False)

    def f_add4(ti, pa):
        return [((ti[0] + ti[1].astype(F32)) + ti[2].astype(F32)) + ti[3].astype(F32)], []

    (red_late,) = _rowwise("grad_add_chips", f_add4, [own, others[0], others[1], others[2]], [], [(pack_w, F32)])

    assert lay_eg.width == lay_lg.width
    red = jnp.concatenate([red_early, red_late], axis=0)
    sib = _sibling_swap(red)
    lower, upper = jnp.where(c == 0, red, sib), jnp.where(c == 0, sib, red)
    n_e = lay_eg.rows // 2
    g_by_name = dict(zip(eg_names, lay_eg.unpack(jnp.concatenate([lower[:n_e], upper[:n_e]], axis=0)), strict=True))
    g_by_name.update(zip(lg_names, lay_lg.unpack(jnp.concatenate([lower[n_e:], upper[n_e:]], axis=0)), strict=True))
    g_shards = [g_by_name[nm] for nm in big_names]

    def f_adamw(ti, pa):
        wv, gv, mv, vv = ti
        return list(_adamw(wv, gv, mv, vv)), []

    out = {"grad": {}, "delta": {}, "m": {}, "v": {}}
    for nm, g_sh in zip(big_names, g_shards, strict=True):
        wd = g_sh.shape[1]
        d_sh, m_sh, v_sh = _rowwise("adamw_" + nm, f_adamw, [w[nm][0], g_sh, mom[nm][0], var[nm][0]], [], [(wd, F32)] * 3)
        out["grad"][nm], out["delta"][nm], out["m"][nm], out["v"][nm] = g_sh[None], d_sh[None], m_sh[None], v_sh[None]

    small = _small_allreduce_adamw([grads[nm] for nm in SMALL], [w[nm] for nm in SMALL],
                                   [mom[nm] for nm in SMALL], [var[nm] for nm in SMALL])
    for kind, arrs in zip(("grad", "delta", "m", "v"), small, strict=True):
        for nm, arr in zip(SMALL, arrs, strict=True):
            out[kind][nm] = arr

    return (loss, grad_x[None], *[out["grad"][nm] for nm in ALL_W], *[out["delta"][nm] for nm in ALL_W],
            *[out["m"][nm] for nm in ALL_W], *[out["v"][nm] for nm in ALL_W])
```

```python
import functools
import math

import jax
import jax.numpy as jnp
from jax import lax
from jax.experimental import pallas as pl
from jax.experimental.pallas import tpu as pltpu

F32 = jnp.float32
BF16 = jnp.bfloat16

MLA_HEADS = 8
MLA_Q_LORA = 256
MLA_KV_LORA = 128
MLA_NOPE = 64
MLA_ROPE = 32
MLA_V = 64
FOX_HEADS = 8
FOX_DIM = 64
ROPE_THETA = 10000.0
NORM_EPS = 1e-6
HALF_ROPE = MLA_ROPE // 2

ADAM_LR = 0.001
ADAM_B1 = 0.9
ADAM_B2 = 0.999
ADAM_EPS = 1e-08
ADAM_WD = 0.01
ADAM_STEP = 10

LANES = 128
VMEM_LIMIT = 56 * 1024 * 1024
ATT_TILE = 1024
FWD_GROUP_LOG2 = 1
FWD_GROUP = 1 << FWD_GROUP_LOG2
MM_VMEM_BUDGET = 40 * 1024 * 1024
MXU_WIDTH = 256
MXU_MACS_PER_S = 4.98e14
HBM_BYTES_PER_S = 3.2e12
STEP_OVERHEAD_S = 0.35e-6
NEG = -1e30
LOG2E = math.log2(math.e)
MESH = pl.DeviceIdType.MESH

V_ONES = 64
FOX_Q_F = 64
FOX_Q_L = 67
FOX_Q_ONES = 70
MLA_Q_L = 96

BIG = (("w_in", 2), ("w_uq", 2), ("w_uk", 2), ("w_uv", 2), ("w_o_mla", 2), ("w_o_fox", 2),
       ("w_out", 1), ("w_ff1", 2), ("w_ff2", 1))
MIXER = ("w_uq", "w_uk", "w_uv", "w_o_mla", "w_o_fox")
LATE = ("w_out", "w_ff1", "w_ff2")
EARLY_GRADS = ("w_o_mla", "w_o_fox", "w_out", "w_ff1", "w_ff2")
SMALL = ("ln_pre_mix", "ln_post_mix", "ln_pre_mlp", "ln_post_mlp", "b_in", "q_a_norm", "kv_a_norm")
ALL_W = ("ln_pre_mix", "ln_post_mix", "ln_pre_mlp", "ln_post_mlp", "w_in", "b_in", "q_a_norm", "w_uq",
         "kv_a_norm", "w_uk", "w_uv", "w_o_mla", "w_o_fox", "w_out", "w_ff1", "w_ff2")
N_CHIPS = 4
PACK_W = 1024
LOCAL_PIECES = 8

_NT = (((1,), (1,)), ((), ()))
_TN = (((0,), (0,)), ((), ()))


def _cparams(sem=None):
    return pltpu.CompilerParams(dimension_semantics=sem, vmem_limit_bytes=VMEM_LIMIT)


def _divisor_tile(n, limit, mult):
    if n <= limit:
        return n
    best = None
    t = mult
    while t <= limit:
        if n % t == 0:
            best = t
        t += mult
    assert best is not None, (n, limit, mult)
    return best


def _round_up(v, mult):
    return -(-v // mult) * mult


def _mm_tiles(m, k, n, io_bytes):
    best = None
    for tm in (2048, 1024, 512, 256, 128):
        if m % tm:
            continue
        for tn in range(LANES, min(n, 2048) + 1, LANES):
            if n % tn:
                continue
            vmem = 2 * (tm * k * 2 + k * tn * 2 + tm * tn * io_bytes) + tm * tn * 4
            if vmem > MM_VMEM_BUDGET:
                continue
            mxu = m * k * n * (_round_up(tn, MXU_WIDTH) / tn) / MXU_MACS_PER_S
            hbm = (m * k * 2 + (m // tm) * k * n * 2 + m * n * io_bytes) / HBM_BYTES_PER_S
            cost = max(mxu, hbm) + (m // tm) * (n // tn) * STEP_OVERHEAD_S
            if best is None or cost < best[0]:
                best = (cost, tm, tn)
    assert best is not None, (m, k, n)
    return best[1], best[2]


def _mm(a, b, *, out_dtype, name, bias=None, transpose_b=False, extras=(), epilogue=None):
    m, k = a.shape
    n = b.shape[0] if transpose_b else b.shape[1]
    assert (b.shape[1] if transpose_b else b.shape[0]) == k and a.dtype == BF16 and b.dtype == BF16
    out_dtypes = list(out_dtype) if isinstance(out_dtype, (list, tuple)) else [out_dtype]
    n_ex = len(extras)
    tm, tn = _mm_tiles(m, k, n, sum(jnp.dtype(dt).itemsize for dt in out_dtypes) + 4 * n_ex)

    def body(*refs):
        a_ref, b_ref = refs[:2]
        pos = 2
        bias_ref = None
        if bias is not None:
            bias_ref = refs[pos]
            pos += 1
        ex_refs = refs[pos:pos + n_ex]
        o_refs = refs[pos + n_ex:]
        if transpose_b:
            acc = lax.dot_general(a_ref[...], b_ref[...], _NT, preferred_element_type=F32)
        else:
            acc = jnp.dot(a_ref[...], b_ref[...], preferred_element_type=F32)
        if bias_ref is not None:
            acc = acc + bias_ref[...]
        vals = [acc] if epilogue is None else epilogue(acc, [r[...] for r in ex_refs])
        for ref, val in zip(o_refs, vals, strict=True):
            ref[...] = val.astype(ref.dtype)

    b_spec = pl.BlockSpec((tn, k), lambda i, j: (j, 0)) if transpose_b else pl.BlockSpec((k, tn), lambda i, j: (0, j))
    in_specs = [pl.BlockSpec((tm, k), lambda i, j: (i, 0)), b_spec]
    args = [a, b]
    if bias is not None:
        in_specs.append(pl.BlockSpec((1, tn), lambda i, j: (0, j)))
        args.append(bias)
    in_specs += [pl.BlockSpec((tm, tn), lambda i, j: (i, j)) for _ in extras]
    args += list(extras)
    res = pl.pallas_call(
        body, grid=(m // tm, n // tn), in_specs=in_specs,
        out_specs=[pl.BlockSpec((tm, tn), lambda i, j: (i, j)) for _ in out_dtypes],
        out_shape=[jax.ShapeDtypeStruct((m, n), dt) for dt in out_dtypes],
        compiler_params=_cparams(("parallel", "parallel")), name=name)(*args)
    return res if isinstance(out_dtype, (list, tuple)) else res[0]


def _mm_tn(a, b, *, name):
    s, m = a.shape
    s2, n = b.shape
    assert s == s2 and a.dtype == BF16 and b.dtype == BF16
    tm = _divisor_tile(m, 1024, LANES)
    tn = _divisor_tile(n, 2304, LANES)
    tk = _divisor_tile(s, 512, 16)

    def body(a_ref, b_ref, o_ref):
        @pl.when(pl.program_id(2) == 0)
        def _():
            o_ref[...] = jnp.zeros_like(o_ref)

        o_ref[...] += lax.dot_general(a_ref[...], b_ref[...], _TN, preferred_element_type=F32)

    return pl.pallas_call(
        body, grid=(m // tm, n // tn, s // tk),
        in_specs=[pl.BlockSpec((tk, tm), lambda i, j, k: (k, i)), pl.BlockSpec((tk, tn), lambda i, j, k: (k, j))],
        out_specs=pl.BlockSpec((tm, tn), lambda i, j, k: (i, j)),
        out_shape=jax.ShapeDtypeStruct((m, n), F32),
        compiler_params=_cparams(("parallel", "parallel", "arbitrary")), name=name)(a, b)


def _rowwise(name, fn, tiled, params, outs, reds=(), reverse=False):
    wins = [t if isinstance(t, tuple) else (t, 0, t.shape[1]) for t in tiled]
    s = wins[0][0].shape[0]
    row_bytes = sum(w * arr.dtype.itemsize for arr, _, w in wins) + sum(w * jnp.dtype(d).itemsize for w, d in outs)
    ts = _divisor_tile(s, max(16, min(1024, (6 * 1024 * 1024) // row_bytes)), 16)
    nt, npar, nout = len(wins), len(params), len(outs)
    n_tiles = s // ts

    def row(i):
        return n_tiles - 1 - i if reverse else i

    def body(*refs):
        tin = [r[...] for r in refs[:nt]]
        par = [r[...] for r in refs[nt:nt + npar]]
        out_refs = refs[nt + npar:nt + npar + nout]
        red_refs = refs[nt + npar + nout:]
        o, r = fn(tin, par)
        for ref, val in zip(out_refs, o, strict=True):
            ref[...] = val.astype(ref.dtype)
        if red_refs:
            @pl.when(pl.program_id(0) == 0)
            def _():
                for ref in red_refs:
                    ref[...] = jnp.zeros_like(ref)

            for ref, val in zip(red_refs, r, strict=True):
                ref[...] += val

    in_specs = [pl.BlockSpec((ts, w), functools.partial(lambda i, cb: (row(i), cb), cb=cb)) for _, cb, w in wins]
    in_specs += [pl.BlockSpec(p.shape, lambda i: (0, 0)) for p in params]
    out_specs = [pl.BlockSpec((ts, w), lambda i: (row(i), 0)) for w, _ in outs]
    out_specs += [pl.BlockSpec((1, w), lambda i: (0, 0)) for w in reds]
    out_shape = [jax.ShapeDtypeStruct((s, w), d) for w, d in outs]
    out_shape += [jax.ShapeDtypeStruct((1, w), F32) for w in reds]
    return pl.pallas_call(
        body, grid=(n_tiles,), in_specs=in_specs, out_specs=out_specs, out_shape=out_shape,
        compiler_params=_cparams(("arbitrary",)), name=name)(*[w[0] for w in wins], *params)


def _rms(x, g):
    r = lax.rsqrt(jnp.mean(x * x, axis=-1, keepdims=True) + NORM_EPS)
    return x * r * g, r


def _rms_bwd(x, g, dy):
    r = lax.rsqrt(jnp.mean(x * x, axis=-1, keepdims=True) + NORM_EPS)
    gy = dy * g
    dx = r * gy - x * (r * r * r) * jnp.mean(x * gy, axis=-1, keepdims=True)
    dg = jnp.sum(dy * (x * r), axis=0, keepdims=True)
    return dx, dg


def _sigmoid(x):
    return 1.0 / (1.0 + jnp.exp(-x))


def _split3(x):
    hi = x.astype(BF16).astype(F32)
    r = x - hi
    mid = r.astype(BF16).astype(F32)
    lo = (r - mid).astype(BF16).astype(F32)
    return hi, mid, lo


def _lane(shape):
    return lax.broadcasted_iota(jnp.int32, shape, 1)


def _put3(blk, lane, pos, pieces):
    for k, piece in enumerate(pieces):
        blk = jnp.where(lane == pos + k, piece, blk)
    return blk


def _lane_column(blk, lane, pos):
    return jnp.sum(jnp.where(lane == pos, blk, 0.0), axis=1, keepdims=True)


def _blocks(a, nh):
    return [a[:, h * LANES:(h + 1) * LANES] for h in range(nh)]


def _rope_block(x, c, sa, sb):
    return x * c + pltpu.roll(x, LANES - HALF_ROPE, 1) * sa + pltpu.roll(x, HALF_ROPE, 1) * sb


def _forget_cumsum(z, cb):
    s = z.shape[0]
    ts = _divisor_tile(s, 512, LANES)

    def body(x_ref, col_ref, carry):
        @pl.when(pl.program_id(0) == 0)
        def _():
            carry[...] = jnp.zeros_like(carry)

        x = x_ref[...]
        lf = jnp.minimum(x, 0.0) - jnp.log1p(jnp.exp(-jnp.abs(x)))
        r = lax.broadcasted_iota(jnp.int32, (ts, ts), 0)
        c = lax.broadcasted_iota(jnp.int32, (ts, ts), 1)
        tri = jnp.where(c <= r, 1.0, 0.0).astype(F32)
        col_ref[...] = jnp.dot(tri, lf, preferred_element_type=F32, precision=lax.Precision.HIGHEST) + carry[...]
        carry[...] += jnp.sum(lf, axis=0, keepdims=True)

    return pl.pallas_call(
        body, grid=(s // ts,),
        in_specs=[pl.BlockSpec((ts, LANES), lambda i: (i, cb))],
        out_specs=pl.BlockSpec((ts, LANES), lambda i: (i, 0)),
        out_shape=jax.ShapeDtypeStruct((s, LANES), F32),
        scratch_shapes=[pltpu.VMEM((1, LANES), F32)],
        compiler_params=_cparams(("arbitrary",)), name="forget_cumsum")(z)


def _forget_cumsum_bwd(dq, dk, z, cb, nh):
    s = z.shape[0]
    ts = _divisor_tile(s, 512, LANES)
    nt = s // ts
    wd = nh * LANES

    def body(dq_ref, dk_ref, x_ref, o_ref, carry):
        @pl.when(pl.program_id(0) == 0)
        def _():
            carry[...] = jnp.zeros_like(carry)

        lane = _lane((ts, LANES))
        df = jnp.zeros((ts, LANES), F32)
        for h in range(nh):
            cols = slice(h * LANES, (h + 1) * LANES)
            d_h = _lane_column(dq_ref[:, cols], lane, FOX_Q_F) - _lane_column(dk_ref[:, cols], lane, FOX_Q_ONES)
            df = jnp.where(lane == h, d_h, df)
        r = lax.broadcasted_iota(jnp.int32, (ts, ts), 0)
        c = lax.broadcasted_iota(jnp.int32, (ts, ts), 1)
        tri = jnp.where(c >= r, 1.0, 0.0).astype(F32)
        rc = jnp.dot(tri, df, preferred_element_type=F32, precision=lax.Precision.HIGHEST) + carry[...]
        carry[...] += jnp.sum(df, axis=0, keepdims=True)
        o_ref[...] = (rc * (1.0 / (1.0 + jnp.exp(x_ref[...])))).astype(o_ref.dtype)

    return pl.pallas_call(
        body, grid=(nt,),
        in_specs=[pl.BlockSpec((ts, wd), lambda i: (nt - 1 - i, 0)),
                  pl.BlockSpec((ts, wd), lambda i: (nt - 1 - i, 0)),
                  pl.BlockSpec((ts, LANES), lambda i: (nt - 1 - i, cb))],
        out_specs=pl.BlockSpec((ts, LANES), lambda i: (nt - 1 - i, 0)),
        out_shape=jax.ShapeDtypeStruct((s, LANES), BF16),
        scratch_shapes=[pltpu.VMEM((1, LANES), F32)],
        compiler_params=_cparams(("arbitrary",)), name="forget_cumsum_bwd")(dq, dk, z)


def _flash_fwd(q, k, v, scale, nh, l_lane, *, name):
    s = q.shape[0]
    t = min(ATT_TILE, s)
    half = t // 2 if t % (2 * LANES) == 0 else t
    nq = s // t
    c = scale * LOG2E

    def body(q_ref, k_ref, v_ref, o_ref, qb_ref):
        i = pl.program_id(1)
        qb = q_ref[...]

        def scores(q_rows, k0, nk):
            kb = k_ref[pl.ds(pl.multiple_of(k0, half), nk), :]
            return lax.dot_general(q_rows, kb, _NT, preferred_element_type=F32)

        def update(sc, k0, nk, carry):
            m, acc = carry
            m_new = jnp.maximum(m, jnp.max(sc, axis=1, keepdims=True))
            p = jnp.exp2((sc - m_new) * c)
            alpha = jnp.exp2((m - m_new) * c)
            vb = v_ref[pl.ds(pl.multiple_of(k0, half), nk), :]
            acc = alpha * acc + jnp.dot(p.astype(BF16), vb, preferred_element_type=F32)
            return m_new, acc

        def full_block(j, cr):
            return update(scores(qb, j * t, t), j * t, t, cr)

        def group(jj, cr):
            for n in range(FWD_GROUP):
                cr = full_block(FWD_GROUP * jj + n, cr)
            return cr

        def causal(sc):
            row = lax.broadcasted_iota(jnp.int32, sc.shape, 0)
            col = lax.broadcasted_iota(jnp.int32, sc.shape, 1)
            return jnp.where(col <= row, sc, NEG)

        init = (jnp.full((t, 1), NEG, F32), jnp.zeros((t, LANES), F32))
        n_groups = i >> FWD_GROUP_LOG2
        carry = lax.fori_loop(0, n_groups, group, init)
        carry = lax.fori_loop(n_groups * FWD_GROUP, i, full_block, carry)
        m, acc = update(causal(scores(qb, i * t, t)), i * t, t, carry)
        lane = _lane((t, LANES))
        l = _lane_column(acc, lane, V_ONES)
        o_ref[...] = (acc / l).astype(o_ref.dtype)
        big_l = m + jnp.log(l) / scale
        qb_ref[...] = _put3(qb.astype(F32), lane, l_lane, _split3(-big_l)).astype(qb_ref.dtype)

    head_rows = pl.BlockSpec((t, LANES), lambda h, i: (i, h))
    head_all = pl.BlockSpec((s, LANES), lambda h, i: (0, h))
    return pl.pallas_call(
        body, grid=(nh, nq), in_specs=[head_rows, head_all, head_all], out_specs=[head_rows, head_rows],
        out_shape=[jax.ShapeDtypeStruct(q.shape, BF16), jax.ShapeDtypeStruct(q.shape, BF16)],
        compiler_params=_cparams(("parallel", "arbitrary")), name=name)(q, k, v)


def _flash_bwd(qb, k, v, do, scale, nh, n_feat, *, name):
    s = qb.shape[0]
    t = min(ATT_TILE, s)
    half = t // 2 if t % (2 * LANES) == 0 else t
    nq = s // t
    c = scale * LOG2E

    def body(q_ref, k_ref, v_ref, do_ref, dk_ref, dv_ref, dq_ref):
        j = pl.program_id(1)
        kb = k_ref[...]
        vb = v_ref[...]

        @pl.when(j == 0)
        def _():
            dq_ref[...] = jnp.zeros_like(dq_ref)

        def part(q0, n_q, n_k, carry, q_off):
            dk_acc, dv_acc = carry
            rows = pl.ds(pl.multiple_of(q0, half), n_q)
            qblk = q_ref[rows, :]
            dob = do_ref[rows, :]
            kbb, vbb = kb[:n_k], vb[:n_k]
            st = lax.dot_general(kbb, qblk, _NT, preferred_element_type=F32)
            if q_off is not None:
                key = lax.broadcasted_iota(jnp.int32, st.shape, 0)
                qry = lax.broadcasted_iota(jnp.int32, st.shape, 1) + q_off
                st = jnp.where(key <= qry, st, NEG)
            pt = jnp.exp2(st * c)
            dv_new = jnp.dot(pt.astype(BF16), dob, preferred_element_type=F32)
            dpt = lax.dot_general(vbb, dob, _NT, preferred_element_type=F32)
            dsb = (pt * dpt).astype(BF16)
            dk_new = jnp.dot(dsb, qblk, preferred_element_type=F32)
            dq_ref[rows, :] += lax.dot_general(dsb, kbb, _TN, preferred_element_type=F32)
            if n_k == t:
                return dk_acc + dk_new, dv_acc + dv_new
            return (jnp.concatenate([dk_acc[:n_k] + dk_new, dk_acc[n_k:]], axis=0),
                    jnp.concatenate([dv_acc[:n_k] + dv_new, dv_acc[n_k:]], axis=0))

        def block(i, carry, masked):
            return part(i * t, t, t, carry, None)

        init = (jnp.zeros((t, LANES), F32), jnp.zeros((t, LANES), F32))
        carry = part(j * t + half, t - half, t, init, half) if half < t else part(j * t, t, t, init, 0)
        if half < t:
            carry = part(j * t, half, half, carry, 0)
        rest = nq - 1 - j
        carry = lax.cond((rest & 1) == 1, lambda cr: block(j + 1, cr, False), lambda cr: cr, carry)
        first = j + 1 + (rest & 1)

        def pair(ii, cr):
            i0 = first + 2 * ii
            return block(i0 + 1, block(i0, cr, False), False)

        dk_acc, dv_acc = lax.fori_loop(0, rest >> 1, pair, carry)
        dk_ref[...] = dk_acc * jnp.where(_lane((t, LANES)) < n_feat, scale, 1.0)
        dv_ref[...] = dv_acc

        @pl.when(j == nq - 1)
        def _():
            dq_ref[...] = dq_ref[...] * jnp.where(_lane((s, LANES)) < n_feat, scale, 1.0)

    head_rows = pl.BlockSpec((t, LANES), lambda h, j: (j, h))
    head_all = pl.BlockSpec((s, LANES), lambda h, j: (0, h))
    shp = jax.ShapeDtypeStruct(qb.shape, F32)
    return pl.pallas_call(
        body, grid=(nh, nq), in_specs=[head_all, head_rows, head_rows, head_all],
        out_specs=[head_rows, head_rows, head_all], out_shape=[shp, shp, shp],
        compiler_params=_cparams(("parallel", "arbitrary")), name=name)(qb, k, v, do)


def _with_delta(do, o, nh, *, name):
    def fn(ti, pa):
        lane = _lane((ti[0].shape[0], LANES))
        out = []
        for d_blk, o_blk in zip(_blocks(ti[0], nh), _blocks(ti[1], nh), strict=True):
            delta = jnp.sum(d_blk * o_blk.astype(F32), axis=1, keepdims=True)
            out.append(_put3(d_blk, lane, V_ONES, _split3(-delta)))
        return [jnp.concatenate(out, axis=1)], []

    (res,) = _rowwise(name, fn, [do, o], [], [(do.shape[1], BF16)])
    return res


def _pad_heads(a, nh):
    d = a.shape[-1] // nh
    a = a.reshape(a.shape[:-1] + (nh, d))
    a = jnp.pad(a, [(0, 0)] * (a.ndim - 1) + [(0, LANES - d)])
    return a.reshape(a.shape[:-2] + (nh * LANES,))


def _unpad_heads(a, nh, d):
    a = a.reshape(a.shape[:-1] + (nh, LANES))[..., :d]
    return a.reshape(a.shape[:-2] + (nh * d,))


def _pad_head_rows(w, nh):
    return _pad_heads(w.T, nh).T


def _unpad_head_rows(g, nh, d):
    return _unpad_heads(g.T, nh, d).T


class _ZLayout:
    def __init__(self, d):
        fw = FOX_HEADS * FOX_DIM
        self.src = {}
        off = 0
        for nm, w in (("cq", MLA_Q_LORA), ("ckv", MLA_KV_LORA), ("kr", MLA_ROPE), ("fq", fw), ("fk", fw),
                      ("fv", fw), ("fl", FOX_HEADS), ("ga", d), ("gb", d)):
            self.src[nm] = (off, w)
            off += w
        self.dst = {}
        off = 0
        for nm, w in (("fq", FOX_HEADS * LANES), ("fk", FOX_HEADS * LANES), ("fv", FOX_HEADS * LANES), ("ga", d),
                      ("gb", d), ("cq", MLA_Q_LORA), ("ckv", MLA_KV_LORA), ("kr", LANES), ("fl", LANES)):
            assert off % w == 0
            self.dst[nm] = (off, w)
            off += w
        self.width = off

    def win(self, z, nm):
        off, w = self.dst[nm]
        return (z, off // w, w)

    def to_kernel(self, w):
        def seg(nm):
            off, wd = self.src[nm]
            return w[..., off:off + wd]

        def pad(a, left, total):
            return jnp.pad(a, [(0, 0)] * (a.ndim - 1) + [(left, total - left - a.shape[-1])])

        return jnp.concatenate([_pad_heads(seg("fq"), FOX_HEADS), _pad_heads(seg("fk"), FOX_HEADS),
                                _pad_heads(seg("fv"), FOX_HEADS), seg("ga"), seg("gb"), seg("cq"), seg("ckv"),
                                pad(seg("kr"), MLA_NOPE, LANES), pad(seg("fl"), 0, LANES)], axis=-1)

    def from_kernel(self, g):
        def seg(nm, lo=0, hi=None):
            off, wd = self.dst[nm]
            return g[..., off + lo:off + (wd if hi is None else hi)]

        return jnp.concatenate([seg("cq"), seg("ckv"), seg("kr", MLA_NOPE, MLA_NOPE + MLA_ROPE),
                                _unpad_heads(seg("fq"), FOX_HEADS, FOX_DIM), _unpad_heads(seg("fk"), FOX_HEADS, FOX_DIM),
                                _unpad_heads(seg("fv"), FOX_HEADS, FOX_DIM), seg("fl", 0, FOX_HEADS), seg("ga"),
                                seg("gb")], axis=-1)


def _local_step(x, positions, target, wts, late_weights, send_grads):
    s, d = x.shape
    zl = _ZLayout(d)
    hw = MLA_HEADS * LANES
    assert MLA_HEADS == FOX_HEADS
    scale_mla = (MLA_NOPE + MLA_ROPE) ** -0.5
    scale_fox = FOX_DIM ** -0.5

    inv_freq = ROPE_THETA ** (-jnp.arange(HALF_ROPE, dtype=F32) / HALF_ROPE)
    ang = positions.astype(F32)[:, None] * inv_freq
    cos, sin = jnp.cos(ang), jnp.sin(ang)
    tail = jnp.zeros((s, LANES - MLA_NOPE - MLA_ROPE), F32)
    rc = jnp.concatenate([jnp.ones((s, MLA_NOPE), F32), cos, cos, tail], axis=1)
    ra = jnp.concatenate([jnp.zeros((s, MLA_NOPE), F32), -sin, jnp.zeros((s, HALF_ROPE), F32), tail], axis=1)
    rb = jnp.concatenate([jnp.zeros((s, MLA_NOPE + HALF_ROPE), F32), sin, tail], axis=1)

    w_in = zl.to_kernel(wts["w_in"])
    b_in = zl.to_kernel(wts["b_in"])

    def f_norm_in(ti, pa):
        y, _ = _rms(ti[0], pa[0])
        return [y], []

    (h,) = _rowwise("norm_in", f_norm_in, [x], [wts["ln_pre_mix"]], [(d, BF16)])
    z = _mm(h, w_in, bias=b_in, out_dtype=F32, name="proj_in")
    wts = {**wts, **late_weights("mixer", z)}
    w_uq = _pad_heads(wts["w_uq"], MLA_HEADS)
    w_ukv = jnp.concatenate([_pad_heads(wts["w_uk"], MLA_HEADS), _pad_heads(wts["w_uv"], MLA_HEADS)], axis=1)
    w_o_mla = _pad_head_rows(wts["w_o_mla"], MLA_HEADS)
    w_o_fox = _pad_head_rows(wts["w_o_fox"], FOX_HEADS)

    def f_mla_norms(ti, pa):
        cqn, _ = _rms(ti[0], pa[0])
        ckvn, _ = _rms(ti[1], pa[1])
        return [cqn, ckvn], []

    cqn, ckvn = _rowwise("mla_norms", f_mla_norms, [zl.win(z, "cq"), zl.win(z, "ckv")],
                         [wts["q_a_norm"], wts["kv_a_norm"]], [(MLA_Q_LORA, BF16), (MLA_KV_LORA, BF16)])
    qf = _mm(cqn, w_uq, out_dtype=F32, name="proj_uq")
    kv = _mm(ckvn, w_ukv, out_dtype=BF16, name="proj_ukv")

    def f_rope_q(ti, pa):
        xq, c_, a_, b_ = ti
        return [jnp.concatenate([_rope_block(blk, c_, a_, b_) for blk in _blocks(xq, MLA_HEADS)], axis=1)], []

    (q_mla,) = _rowwise("rope_q", f_rope_q, [qf, rc, ra, rb], [], [(hw, BF16)])

    def f_mla_kv(ti, pa):
        kn, vn, kr, c_, a_, b_ = ti
        lane = _lane(kr.shape)
        k_tail = jnp.where((lane >= MLA_Q_L) & (lane < MLA_Q_L + 3), 1.0, _rope_block(kr, c_, a_, b_))
        ones_v = (lane >= V_ONES) & (lane < V_ONES + 3)
        k_out = [jnp.where(lane < MLA_NOPE, blk.astype(F32), k_tail) for blk in _blocks(kn, MLA_HEADS)]
        v_out = [jnp.where(ones_v, 1.0, blk.astype(F32)) for blk in _blocks(vn, MLA_HEADS)]
        return [jnp.concatenate(k_out, axis=1), jnp.concatenate(v_out, axis=1)], []

    k_mla, v_mla = _rowwise("mla_kv", f_mla_kv, [(kv, 0, hw), (kv, 1, hw), zl.win(z, "kr"), rc, ra, rb], [],
                            [(hw, BF16), (hw, BF16)])
    o_mla, qb_mla = _flash_fwd(q_mla, k_mla, v_mla, scale_mla, MLA_HEADS, MLA_Q_L, name="mla_fwd")

    fl_cb = zl.dst["fl"][0] // LANES
    fcol = _forget_cumsum(z, fl_cb)

    def f_fox_qkv(ti, pa):
        fq, fk, fv, fc = ti
        lane = _lane(fc.shape)
        ones_q = (lane >= FOX_Q_ONES) & (lane < FOX_Q_ONES + 3)
        ones_k = (lane >= FOX_Q_F) & (lane < FOX_Q_ONES)
        ones_v = (lane >= V_ONES) & (lane < V_ONES + 3)
        q_out, k_out, v_out = [], [], []
        for hh, (qblk, kblk, vblk) in enumerate(zip(_blocks(fq, FOX_HEADS), _blocks(fk, FOX_HEADS),
                                                    _blocks(fv, FOX_HEADS), strict=True)):
            f_h = _lane_column(fc, lane, hh) * (1.0 / scale_fox)
            q_out.append(_put3(jnp.where(ones_q, 1.0, qblk), lane, FOX_Q_F, _split3(f_h)))
            k_out.append(_put3(jnp.where(ones_k, 1.0, kblk), lane, FOX_Q_ONES, _split3(-f_h)))
            v_out.append(jnp.where(ones_v, 1.0, vblk))
        return [jnp.concatenate(q_out, axis=1), jnp.concatenate(k_out, axis=1), jnp.concatenate(v_out, axis=1)], []

    q_fox, k_fox, v_fox = _rowwise("fox_qkv", f_fox_qkv, [zl.win(z, "fq"), zl.win(z, "fk"), zl.win(z, "fv"), fcol],
                                   [], [(hw, BF16)] * 3)
    o_fox, qb_fox = _flash_fwd(q_fox, k_fox, v_fox, scale_fox, FOX_HEADS, FOX_Q_L, name="fox_fwd")

    y_mla = _mm(o_mla, w_o_mla, out_dtype=BF16, name="proj_o_mla")
    y_fox = _mm(o_fox, w_o_fox, out_dtype=BF16, name="proj_o_fox")

    def f_gate(ti, pa):
        ga, gb, ya, yb = ti
        return [_sigmoid(ga) * ya + _sigmoid(gb) * yb], []

    (merged,) = _rowwise("gate", f_gate, [zl.win(z, "ga"), zl.win(z, "gb"), y_mla, y_fox], [], [(d, BF16)])
    wts = {**wts, **late_weights("mlp", merged)}
    mix = _mm(merged, wts["w_out"], out_dtype=F32, name="proj_out")

    def f_resid1(ti, pa):
        xa, mx = ti
        y, _ = _rms(mx, pa[0])
        x1 = xa + y
        h2, _ = _rms(x1, pa[1])
        return [x1, h2], []

    x1, h2 = _rowwise("resid_mix", f_resid1, [x, mix], [wts["ln_post_mix"], wts["ln_pre_mlp"]], [(d, F32), (d, BF16)])

    def relu2(acc, ex):
        r = jnp.maximum(acc, 0.0)
        return [acc, r * r]

    u, act = _mm(h2, wts["w_ff1"], out_dtype=[BF16, BF16], name="ff1", epilogue=relu2)
    mo = _mm(act, wts["w_ff2"], out_dtype=F32, name="ff2")

    def f_loss(ti, pa):
        xa, mv, tg = ti
        y, _ = _rms(mv, pa[0])
        err = (xa + y) - tg
        g2 = err / d
        dmo, dg = _rms_bwd(mv, pa[0], g2)
        return [g2, dmo], [jnp.sum(err * err, axis=0, keepdims=True), dg]

    g2, d_mo, loss_cols, g_ln_post_mlp = _rowwise("loss", f_loss, [x1, mo, target], [wts["ln_post_mlp"]],
                                                  [(d, F32), (d, BF16)], [d, d])
    loss = 0.5 * jnp.sum(loss_cols) / d

    grads = {"ln_post_mlp": g_ln_post_mlp}
    grads["w_ff2"] = _mm_tn(act, d_mo, name="grad_ff2")

    def relu2_bwd(acc, ex):
        return [acc * (2.0 * jnp.maximum(ex[0], 0.0))]

    (d_u,) = _mm(d_mo, wts["w_ff2"], out_dtype=[BF16], name="ff2_bwd", transpose_b=True, extras=[u], epilogue=relu2_bwd)
    grads["w_ff1"] = _mm_tn(h2, d_u, name="grad_ff1")
    d_h2 = _mm(d_u, wts["w_ff1"], out_dtype=F32, name="ff1_bwd", transpose_b=True)

    def f_resid1_bwd(ti, pa):
        gres, dh2, x1v, mx = ti
        dx1n, dg_pre_mlp = _rms_bwd(x1v, pa[1], dh2)
        dx1 = gres + dx1n
        dmix, dg_post_mix = _rms_bwd(mx, pa[0], dx1)
        return [dx1, dmix], [dg_post_mix, dg_pre_mlp]

    d_x1, d_mix, grads["ln_post_mix"], grads["ln_pre_mlp"] = _rowwise(
        "resid_mix_bwd", f_resid1_bwd, [g2, d_h2, x1, mix], [wts["ln_post_mix"], wts["ln_pre_mlp"]],
        [(d, F32), (d, BF16)], [d, d])
    grads["w_out"] = _mm_tn(merged, d_mix, name="grad_out")
    d_merged = _mm(d_mix, wts["w_out"], out_dtype=BF16, name="proj_out_bwd", transpose_b=True)

    def f_gate_bwd(ti, pa):
        dm, ga, gb, ya, yb = ti
        sa, sb = _sigmoid(ga), _sigmoid(gb)
        return [dm * sa, dm * sb, dm * ya * (sa * (1.0 - sa)), dm * yb * (sb * (1.0 - sb))], []

    d_ya, d_yb, d_ga, d_gb = _rowwise("gate_bwd", f_gate_bwd,
                                      [d_merged, zl.win(z, "ga"), zl.win(z, "gb"), y_mla, y_fox], [],
                                      [(d, BF16)] * 4)
    grads["w_o_mla"] = _unpad_head_rows(_mm_tn(o_mla, d_ya, name="grad_o_mla"), MLA_HEADS, MLA_V)
    grads["w_o_fox"] = _unpad_head_rows(_mm_tn(o_fox, d_yb, name="grad_o_fox"), FOX_HEADS, FOX_DIM)
    w_o_mla = w_o_mla + send_grads("early", {nm: grads[nm] for nm in EARLY_GRADS}).astype(BF16)
    do_mla = _with_delta(_mm(d_ya, w_o_mla, out_dtype=F32, name="proj_o_mla_bwd", transpose_b=True), o_mla,
                         MLA_HEADS, name="mla_delta")
    do_fox = _with_delta(_mm(d_yb, w_o_fox, out_dtype=F32, name="proj_o_fox_bwd", transpose_b=True), o_fox,
                         FOX_HEADS, name="fox_delta")

    dk_mla, dv_mla, dq_mla = _flash_bwd(qb_mla, k_mla, v_mla, do_mla, scale_mla, MLA_HEADS, MLA_NOPE + MLA_ROPE,
                                        name="mla_bwd")
    dk_fox, dv_fox, dq_fox = _flash_bwd(qb_fox, k_fox, v_fox, do_fox, scale_fox, FOX_HEADS, FOX_DIM, name="fox_bwd")
    d_fl = _forget_cumsum_bwd(dq_fox, dk_fox, z, fl_cb, FOX_HEADS)

    def f_rope_q_bwd(ti, pa):
        g, c_, a_, b_ = ti
        return [jnp.concatenate([_rope_block(blk, c_, -a_, -b_) for blk in _blocks(g, MLA_HEADS)], axis=1)], []

    (d_qf,) = _rowwise("rope_q_bwd", f_rope_q_bwd, [dq_mla, rc, ra, rb], [], [(hw, BF16)])
    grads["w_uq"] = _unpad_heads(_mm_tn(cqn, d_qf, name="grad_uq"), MLA_HEADS, MLA_NOPE + MLA_ROPE)
    d_cqn = _mm(d_qf, w_uq, out_dtype=F32, name="proj_uq_bwd", transpose_b=True)

    def f_mla_kv_bwd(ti, pa):
        gk, gv, c_, a_, b_ = ti
        k_blocks = _blocks(gk, MLA_HEADS)
        tot = k_blocks[0]
        for blk in k_blocks[1:]:
            tot = tot + blk
        return [jnp.concatenate([gk, gv], axis=1), _rope_block(tot, c_, -a_, -b_)], []

    d_kv, d_kr = _rowwise("mla_kv_bwd", f_mla_kv_bwd, [dk_mla, dv_mla, rc, ra, rb], [], [(2 * hw, BF16), (LANES, BF16)])
    g_ukv = _mm_tn(ckvn, d_kv, name="grad_ukv")
    grads["w_uk"] = _unpad_heads(g_ukv[:, :hw], MLA_HEADS, MLA_NOPE)
    grads["w_uv"] = _unpad_heads(g_ukv[:, hw:], MLA_HEADS, MLA_V)
    d_ckvn = _mm(d_kv, w_ukv, out_dtype=F32, name="proj_ukv_bwd", transpose_b=True)

    def f_mla_norms_bwd(ti, pa):
        cq, ckv, dcqn, dckvn = ti
        dcq, dg_q = _rms_bwd(cq, pa[0], dcqn)
        dckv, dg_kv = _rms_bwd(ckv, pa[1], dckvn)
        return [dcq, dckv], [dg_q, dg_kv]

    d_cq, d_ckv, grads["q_a_norm"], grads["kv_a_norm"] = _rowwise(
        "mla_norms_bwd", f_mla_norms_bwd, [zl.win(z, "cq"), zl.win(z, "ckv"), d_cqn, d_ckvn],
        [wts["q_a_norm"], wts["kv_a_norm"]], [(MLA_Q_LORA, BF16), (MLA_KV_LORA, BF16)], [MLA_Q_LORA, MLA_KV_LORA])

    d_z = jnp.concatenate([dq_fox.astype(BF16), dk_fox.astype(BF16), dv_fox.astype(BF16), d_ga, d_gb, d_cq, d_ckv,
                           d_kr, d_fl], axis=1)
    assert d_z.shape[1] == zl.width

    def f_bias(ti, pa):
        return [], [jnp.sum(ti[0].astype(F32), axis=0, keepdims=True)]

    (g_b_in,) = _rowwise("grad_b_in", f_bias, [d_z], [], [], [zl.width])
    grads["b_in"] = zl.from_kernel(g_b_in)
    grads["w_in"] = zl.from_kernel(_mm_tn(h, d_z, name="grad_in"))
    tok = send_grads("late", {nm: grads[nm] for nm, _ in BIG if nm not in EARLY_GRADS})
    d_h = _mm(d_z, w_in, bias=jnp.zeros((1, d), F32) + tok, out_dtype=F32, name="proj_in_bwd", transpose_b=True)

    def f_norm_in_bwd(ti, pa):
        dx1v, dh, xa = ti
        dxn, dg = _rms_bwd(xa, pa[0], dh)
        return [dx1v + dxn], [dg]

    grad_x, grads["ln_pre_mix"] = _rowwise("norm_in_bwd", f_norm_in_bwd, [d_x1, d_h, x], [wts["ln_pre_mix"]],
                                           [(d, F32)], [d])
    return loss, grad_x, grads


class _PackLayout:
    def __init__(self, shapes):
        self.shapes = list(shapes)
        self.width = _round_up(max(b for _, b in shapes), LANES)
        self.bands = []
        row = 0
        shelf = []
        for idx, (a, b) in enumerate(shapes):
            if 2 * _round_up(b, LANES) > self.width:
                self.bands.append((row, _round_up(a, 32), [(idx, 0)]))
                row += _round_up(a, 32)
            else:
                shelf.append(idx)
        col, items = 0, []
        for idx in shelf:
            wb = _round_up(shapes[idx][1], LANES)
            if col + wb > self.width:
                hgt = max(_round_up(shapes[i][0], 32) for i, _ in items)
                self.bands.append((row, hgt, items))
                row += hgt
                col, items = 0, []
            items.append((idx, col))
            col += wb
        if items:
            hgt = max(_round_up(shapes[i][0], 32) for i, _ in items)
            self.bands.append((row, hgt, items))
            row += hgt
        self.rows = _round_up(row, 16 * LOCAL_PIECES)

    def pack(self, arrs):
        lead = arrs[0].shape[:-2]
        no_pad = [(0, 0)] * len(lead)
        bands = []
        for _, hgt, items in self.bands:
            parts = []
            for k, (idx, col) in enumerate(items):
                a, b = self.shapes[idx]
                nxt = items[k + 1][1] if k + 1 < len(items) else self.width
                parts.append(jnp.pad(arrs[idx], no_pad + [(0, hgt - a), (0, nxt - col - b)]))
            bands.append(parts[0] if len(parts) == 1 else jnp.concatenate(parts, axis=-1))
        used = sum(hgt for _, hgt, _ in self.bands)
        if used < self.rows:
            bands.append(jnp.zeros(lead + (self.rows - used, self.width), arrs[0].dtype))
        return jnp.concatenate(bands, axis=-2)

    def unpack(self, packed):
        out = [None] * len(self.shapes)
        for row, _, items in self.bands:
            for idx, col in items:
                a, b = self.shapes[idx]
                out[idx] = packed[..., row:row + a, col:col + b]
        return out


def _to_shards(g, axis):
    if axis == 0:
        return g.reshape(N_CHIPS, g.shape[0] // N_CHIPS, g.shape[1])
    return jnp.stack(jnp.split(g, N_CHIPS, axis=1))


def _from_shards(s4, axis):
    n, a, b = s4.shape
    if axis == 0:
        return s4.reshape(n * a, b)
    return jnp.concatenate([s4[ch] for ch in range(n)], axis=1)


ANY = pl.BlockSpec(memory_space=pl.ANY)


def _place():
    return lax.axis_index("x"), lax.axis_index("y"), lax.axis_index("c")


def _gather_weights(wpk):
    rows, wd = wpk.shape
    half = rows // 2

    def body(w_ref, out_ref, send_sems, recv_sems, local_sems):
        x, y, c = _place()
        sibling = (x, y, 1 - c)
        chips = [(1 - x, y), (x, 1 - y), (1 - x, 1 - y)]

        def slab(chip, hf):
            return out_ref.at[2 * chip[0] + chip[1], pl.ds(hf * half, half), :]

        def copy(k, chip, hf, to, src=None):
            return pltpu.make_async_remote_copy(
                src_ref=slab(chip, hf) if src is None else src, dst_ref=slab(chip, hf),
                send_sem=send_sems.at[k], recv_sem=recv_sems.at[k], device_id=to, device_id_type=MESH)

        piece = rows // LOCAL_PIECES
        mine = [pltpu.make_async_copy(w_ref.at[pl.ds(n * piece, piece), :],
                                      out_ref.at[2 * x + y, pl.ds(n * piece, piece), :], local_sems.at[n])
                for n in range(LOCAL_PIECES)]
        for cp in mine:
            cp.start()
        first = [copy(j, (x, y), c, (*chip, c), src=w_ref.at[pl.ds(c * half, half), :]) for j, chip in enumerate(chips)]
        for cp in first:
            cp.start()
        passed = [copy(3 + j, chip, c, sibling) for j, chip in enumerate(chips)]
        for j, chip in enumerate(chips):
            copy(j, chip, c, (x, y, c)).wait_recv()
            passed[j].start()
        for j, chip in enumerate(chips):
            copy(3 + j, chip, 1 - c, (x, y, c)).wait_recv()
        for cp in first + passed:
            cp.wait_send()
        for cp in mine:
            cp.wait()

    assert rows % (16 * LOCAL_PIECES) == 0
    return pl.pallas_call(
        body, out_shape=jax.ShapeDtypeStruct((N_CHIPS, rows, wd), wpk.dtype),
        in_specs=[ANY], out_specs=ANY,
        scratch_shapes=[pltpu.SemaphoreType.DMA((6,)), pltpu.SemaphoreType.DMA((6,)),
                        pltpu.SemaphoreType.DMA((LOCAL_PIECES,))],
        name="gather_weights")(wpk)


HBM = pl.BlockSpec(memory_space=pltpu.HBM)
SEM = pl.BlockSpec(memory_space=pltpu.SEMAPHORE)
EFFECT = pltpu.SideEffectType.DATAFLOW_SIDE_EFFECTING
N_LATE = 6


def _gather_late_start(wpk, tag):
    rows, wd = wpk.shape
    half = rows // 2

    def body(w_ref, land_ref, send_sems, recv_sems, w_thru, land_thru, token):
        x, y, c = _place()
        chips = [(1 - x, y), (x, 1 - y), (1 - x, 1 - y)]
        for j, chip in enumerate(chips):
            for to_core in range(2):
                pltpu.make_async_remote_copy(
                    src_ref=w_ref.at[pl.ds(c * half, half), :],
                    dst_ref=land_ref.at[2 * x + y, pl.ds(c * half, half), :],
                    send_sem=send_sems.at[2 * j + to_core], recv_sem=recv_sems.at[2 * j + c],
                    device_id=(*chip, to_core), device_id_type=MESH).start()
        token[...] = jnp.zeros_like(token)

    land = lax.empty((N_CHIPS, rows, wd), wpk.dtype)
    return pl.pallas_call(
        body, name="gather_" + tag + "_start",
        out_shape=(pltpu.SemaphoreType.DMA((N_LATE,)), pltpu.SemaphoreType.DMA((N_LATE,)),
                   pltpu.HBM(wpk.shape, wpk.dtype), pltpu.HBM(land.shape, land.dtype),
                   jax.ShapeDtypeStruct((8, LANES), F32)),
        in_specs=(HBM, HBM), out_specs=(SEM, SEM, HBM, HBM, pl.BlockSpec(memory_space=pltpu.VMEM)),
        input_output_aliases={0: 2, 1: 3},
        compiler_params=pltpu.CompilerParams(has_side_effects=EFFECT),
    )(pltpu.with_memory_space_constraint(wpk, pltpu.HBM), pltpu.with_memory_space_constraint(land, pltpu.HBM))


def _gather_late_wait(send_sems, recv_sems, w_thru, land_thru, after, tag):
    rows, wd = w_thru.shape
    half = rows // 2

    def body(w_ref, land_ref, send_sems, recv_sems, after_ref, w_dead, land_out):
        x, y, c = _place()
        for n in range(N_LATE):
            cp = pltpu.make_async_remote_copy(
                src_ref=w_ref.at[pl.ds(0, half), :], dst_ref=land_ref.at[0, pl.ds(0, half), :],
                send_sem=send_sems.at[n], recv_sem=recv_sems.at[n], device_id=(x, y, c), device_id_type=MESH)
            cp.wait_send()
            cp.wait_recv()

    return pl.pallas_call(
        body, name="gather_" + tag + "_wait",
        out_shape=(pltpu.HBM(w_thru.shape, w_thru.dtype), pltpu.HBM(land_thru.shape, land_thru.dtype)),
        in_specs=(HBM, HBM, SEM, SEM, ANY), out_specs=(HBM, HBM), input_output_aliases={0: 0, 1: 1},
        compiler_params=pltpu.CompilerParams(has_side_effects=EFFECT),
    )(w_thru, land_thru, send_sems, recv_sems, after)[1]


N_PART = 7


def _reduce_start(gbf, tag):
    _, _, hrows, wd = gbf.shape

    def body(g_ref, land_ref, send_sems, recv_sems, g_thru, land_thru, token):
        x, y, c = _place()
        chips = [(1 - x, y), (x, 1 - y), (1 - x, 1 - y)]
        for j, chip in enumerate(chips):
            for to_core in range(2):
                pltpu.make_async_remote_copy(
                    src_ref=g_ref.at[2 * chip[0] + chip[1], to_core], dst_ref=land_ref.at[2 * j + c],
                    send_sem=send_sems.at[2 * j + to_core], recv_sem=recv_sems.at[2 * j + c],
                    device_id=(*chip, to_core), device_id_type=MESH).start()
        pltpu.make_async_remote_copy(
            src_ref=g_ref.at[2 * x + y, 1 - c], dst_ref=land_ref.at[N_PART - 1],
            send_sem=send_sems.at[N_PART - 1], recv_sem=recv_sems.at[N_PART - 1],
            device_id=(x, y, 1 - c), device_id_type=MESH).start()
        token[...] = jnp.zeros_like(token)

    land = lax.empty((N_PART, hrows, wd), gbf.dtype)
    return pl.pallas_call(
        body, name="reduce_" + tag + "_start",
        out_shape=(pltpu.SemaphoreType.DMA((N_PART,)), pltpu.SemaphoreType.DMA((N_PART,)),
                   pltpu.HBM(gbf.shape, gbf.dtype), pltpu.HBM(land.shape, land.dtype),
                   jax.ShapeDtypeStruct((8, LANES), F32)),
        in_specs=(HBM, HBM), out_specs=(SEM, SEM, HBM, HBM, pl.BlockSpec(memory_space=pltpu.VMEM)),
        input_output_aliases={0: 2, 1: 3},
        compiler_params=pltpu.CompilerParams(has_side_effects=EFFECT),
    )(pltpu.with_memory_space_constraint(gbf, pltpu.HBM), pltpu.with_memory_space_constraint(land, pltpu.HBM))


def _reduce_wait(send_sems, recv_sems, g_thru, land_thru, after, tag):
    def body(g_ref, land_ref, send_sems, recv_sems, after_ref, g_dead, land_out):
        x, y, c = _place()
        for n in range(N_PART):
            cp = pltpu.make_async_remote_copy(
                src_ref=g_ref.at[0, 0], dst_ref=land_ref.at[0], send_sem=send_sems.at[n], recv_sem=recv_sems.at[n],
                device_id=(x, y, c), device_id_type=MESH)
            cp.wait_send()
            cp.wait_recv()

    return pl.pallas_call(
        body, name="reduce_" + tag + "_wait",
        out_shape=(pltpu.HBM(g_thru.shape, g_thru.dtype), pltpu.HBM(land_thru.shape, land_thru.dtype)),
        in_specs=(HBM, HBM, SEM, SEM, ANY), out_specs=(HBM, HBM), input_output_aliases={0: 0, 1: 1},
        compiler_params=pltpu.CompilerParams(has_side_effects=EFFECT),
    )(g_thru, land_thru, send_sems, recv_sems, after)[1]


def _sibling_swap(mine):
    def body(m_ref, out_ref, send_sem, recv_sem):
        x, y, c = _place()
        cp = pltpu.make_async_remote_copy(
            src_ref=m_ref, dst_ref=out_ref, send_sem=send_sem, recv_sem=recv_sem,
            device_id=(x, y, 1 - c), device_id_type=MESH)
        cp.start()
        cp.wait()

    return pl.pallas_call(
        body, out_shape=jax.ShapeDtypeStruct(mine.shape, mine.dtype), in_specs=[ANY], out_specs=ANY,
        scratch_shapes=[pltpu.SemaphoreType.DMA, pltpu.SemaphoreType.DMA], name="grad_sibling_swap")(mine)


def _adamw(w, g, m, v):
    m = ADAM_B1 * m + (1.0 - ADAM_B1) * g
    v = ADAM_B2 * v + (1.0 - ADAM_B2) * (g * g)
    m_hat = m / (1.0 - ADAM_B1 ** ADAM_STEP)
    v_hat = v / (1.0 - ADAM_B2 ** ADAM_STEP)
    delta = -ADAM_LR * (m_hat / (jnp.sqrt(v_hat) + ADAM_EPS) + ADAM_WD * w)
    return delta, m, v


def _small_allreduce_adamw(gs, ws, ms, vs):
    n_dev = 8
    n_par = len(gs)
    wd = PACK_W
    chunks = []
    for p, g in enumerate(gs):
        for off in range(0, g.shape[1], wd):
            chunks.append((p, len(chunks), off, min(wd, g.shape[1] - off)))
    rows = _round_up(len(chunks), 8)

    def body(*refs):
        g_refs, w_refs, m_refs, v_refs = (refs[k * n_par:(k + 1) * n_par] for k in range(4))
        go_refs, d_refs, mo_refs, vo_refs = (refs[(4 + k) * n_par:(5 + k) * n_par] for k in range(4))
        mine_ref, all_ref, send_sems, recv_sems = refs[8 * n_par:]
        x, y, c = _place()
        me, sibling = (x, y, c), (x, y, 1 - c)
        chips = [(1 - x, y), (x, 1 - y), (1 - x, 1 - y)]

        def slot(px, py, pc):
            return all_ref.at[4 * px + 2 * py + pc]

        def copy(k, block, to, src=None):
            return pltpu.make_async_remote_copy(
                src_ref=slot(*block) if src is None else src, dst_ref=slot(*block),
                send_sem=send_sems.at[k], recv_sem=recv_sems.at[k], device_id=to, device_id_type=MESH)

        mine_ref[...] = jnp.zeros_like(mine_ref)
        for p, row, off, width in chunks:
            mine_ref[row:row + 1, 0:width] = g_refs[p][:, off:off + width]
        all_ref[4 * x + 2 * y + c] = mine_ref[...]
        first = [copy(0, me, sibling, src=mine_ref)]
        first += [copy(1 + j, me, (*chip, c), src=mine_ref) for j, chip in enumerate(chips)]
        for cp in first:
            cp.start()
        passed = [copy(4 + j, (*chip, c), sibling) for j, chip in enumerate(chips)]
        for j, chip in enumerate(chips):
            copy(1 + j, (*chip, c), me).wait_recv()
            passed[j].start()
        copy(0, sibling, me).wait_recv()
        for j, chip in enumerate(chips):
            copy(4 + j, (*chip, 1 - c), me).wait_recv()
        for cp in first + passed:
            cp.wait_send()
        tot = jnp.zeros((rows, wd), F32)
        for dev in range(n_dev):
            tot = tot + all_ref[dev]
        mine_ref[...] = tot
        for p, row, off, width in chunks:
            cols = slice(off, off + width)
            g = mine_ref[row:row + 1, 0:width]
            delta, m_new, v_new = _adamw(w_refs[p][:, cols], g, m_refs[p][:, cols], v_refs[p][:, cols])
            go_refs[p][:, cols] = g
            d_refs[p][:, cols] = delta
            mo_refs[p][:, cols] = m_new
            vo_refs[p][:, cols] = v_new

    vm = pl.BlockSpec(memory_space=pltpu.VMEM)
    shp = [jax.ShapeDtypeStruct(g.shape, F32) for g in gs]
    res = pl.pallas_call(
        body, out_shape=shp * 4, in_specs=[vm] * (4 * n_par), out_specs=[vm] * (4 * n_par),
        scratch_shapes=[pltpu.VMEM((rows, wd), F32), pltpu.VMEM((n_dev, rows, wd), F32),
                        pltpu.SemaphoreType.DMA((7,)), pltpu.SemaphoreType.DMA((7,))],
        name="small_allreduce_adamw")(*gs, *ws, *ms, *vs)
    return [res[k * n_par:(k + 1) * n_par] for k in range(4)]


def kernel(x, positions, ln_pre_mix, ln_post_mix, ln_pre_mlp, ln_post_mlp, w_in, b_in, q_a_norm, w_uq, kv_a_norm, w_uk, w_uv, w_o_mla, w_o_fox, w_out, w_ff1, w_ff2, loss_target, m_ln_pre_mix, m_ln_post_mix, m_ln_pre_mlp, m_ln_post_mlp, m_w_in, m_b_in, m_q_a_norm, m_w_uq, m_kv_a_norm, m_w_uk, m_w_uv, m_w_o_mla, m_w_o_fox, m_w_out, m_w_ff1, m_w_ff2, v_ln_pre_mix, v_ln_post_mix, v_ln_pre_mlp, v_ln_post_mlp, v_w_in, v_b_in, v_q_a_norm, v_w_uq, v_kv_a_norm, v_w_uk, v_w_uv, v_w_o_mla, v_w_o_fox, v_w_out, v_w_ff1, v_w_ff2):
    w = dict(ln_pre_mix=ln_pre_mix, ln_post_mix=ln_post_mix, ln_pre_mlp=ln_pre_mlp, ln_post_mlp=ln_post_mlp, w_in=w_in,
             b_in=b_in, q_a_norm=q_a_norm, w_uq=w_uq, kv_a_norm=kv_a_norm, w_uk=w_uk, w_uv=w_uv, w_o_mla=w_o_mla,
             w_o_fox=w_o_fox, w_out=w_out, w_ff1=w_ff1, w_ff2=w_ff2)
    mom = dict(ln_pre_mix=m_ln_pre_mix, ln_post_mix=m_ln_post_mix, ln_pre_mlp=m_ln_pre_mlp, ln_post_mlp=m_ln_post_mlp,
               w_in=m_w_in, b_in=m_b_in, q_a_norm=m_q_a_norm, w_uq=m_w_uq, kv_a_norm=m_kv_a_norm, w_uk=m_w_uk,
               w_uv=m_w_uv, w_o_mla=m_w_o_mla, w_o_fox=m_w_o_fox, w_out=m_w_out, w_ff1=m_w_ff1, w_ff2=m_w_ff2)
    var = dict(ln_pre_mix=v_ln_pre_mix, ln_post_mix=v_ln_post_mix, ln_pre_mlp=v_ln_pre_mlp, ln_post_mlp=v_ln_post_mlp,
               w_in=v_w_in, b_in=v_b_in, q_a_norm=v_q_a_norm, w_uq=v_w_uq, kv_a_norm=v_kv_a_norm, w_uk=v_w_uk,
               w_uv=v_w_uv, w_o_mla=v_w_o_mla, w_o_fox=v_w_o_fox, w_out=v_w_out, w_ff1=v_w_ff1, w_ff2=v_w_ff2)

    big_names = [nm for nm, _ in BIG]
    c = lax.axis_index("c")
    chip = 2 * lax.axis_index("x") + lax.axis_index("y")

    axes = dict(BIG)

    def assemble(names, lay, gathered):
        return {nm: _from_shards(s4, axes[nm] - 1) for nm, s4 in zip(names, lay.unpack(gathered), strict=True)}

    groups = {"mixer": [nm for nm in big_names if nm in MIXER], "mlp": [nm for nm in big_names if nm in LATE]}
    first_names = [nm for nm in big_names if nm not in MIXER and nm not in LATE]
    full = {nm: wv for nm, wv in w.items() if nm in SMALL}
    travelling = {}
    for tag, names in groups.items():
        lay = _PackLayout([w[nm].shape[1:] for nm in names])
        shard = lay.pack([w[nm][0].astype(BF16) for nm in names])
        started = _gather_late_start(shard, tag)
        travelling[tag] = (names, lay, shard, started[:4])
        full["b_in"] = full["b_in"] + started[4][0, 0]
    lay_first = _PackLayout([w[nm].shape[1:] for nm in first_names])
    full.update(assemble(first_names, lay_first,
                         _gather_weights(lay_first.pack([w[nm][0].astype(BF16) for nm in first_names]))))

    def late_weights(tag, after):
        names, lay, shard, handles = travelling[tag]
        land = _gather_late_wait(*handles, after, tag)
        land = lax.dynamic_update_slice(land, shard[None], (chip, 0, 0))
        return assemble(names, lay, land)

    grad_groups = {"early": [nm for nm in big_names if nm in EARLY_GRADS],
                   "late": [nm for nm in big_names if nm not in EARLY_GRADS]}
    sent = {}

    def send_grads(tag, g):
        names = grad_groups[tag]
        lay = _PackLayout([w[nm].shape[1:] for nm in names])
        hrows = lay.rows // 2
        gbf = lay.pack([_to_shards(g[nm], axes[nm] - 1).astype(BF16) for nm in names])
        started = _reduce_start(gbf.reshape(N_CHIPS, 2, hrows, lay.width), tag)
        own = lay.pack([lax.dynamic_slice_in_dim(g[nm], chip * w[nm].shape[axes[nm]], w[nm].shape[axes[nm]],
                                                 axis=axes[nm] - 1) for nm in names])
        sent[tag] = (names, lay, started[:4], lax.dynamic_slice_in_dim(own, c * hrows, hrows, axis=0))
        return started[4][0, 0]

    loss_local, grad_x, grads = _local_step(x[0], positions[0], loss_target[0], full, late_weights, send_grads)
    loss = lax.psum(loss_local, ("x", "y", "c"))

    def f_add8(ti, pa):
        tot = ti[0]
        for part in ti[1:]:
            tot = tot + part.astype(F32)
        return [tot], []

    reduced = []
    for tag, (names, lay, handles, own) in sent.items():
        parts = _reduce_wait(*handles, grad_x, tag)
        reduced.append(_rowwise("grad_add_" + tag, f_add8, [own] + [parts[n] for n in range(N_PART)], [],
                                [(lay.width, F32)])[0])

    assert len({lay.width for _, lay, _, _ in sent.values()}) == 1
    red = jnp.concatenate(reduced, axis=0)
    sib = _sibling_swap(red)
    lower, upper = jnp.where(c == 0, red, sib), jnp.where(c == 0, sib, red)
    g_by_name, row = {}, 0
    for names, lay, _, _ in sent.values():
        hrows = lay.rows // 2
        both = jnp.concatenate([lower[row:row + hrows], upper[row:row + hrows]], axis=0)
        g_by_name.update(zip(names, lay.unpack(both), strict=True))
        row += hrows
    g_shards = [g_by_name[nm] for nm in big_names]

    def f_adamw(ti, pa):
        wv, gv, mv, vv = ti
        return list(_adamw(wv, gv, mv, vv)), []

    out = {"grad": {}, "delta": {}, "m": {}, "v": {}}
    for nm, g_sh in zip(big_names, g_shards, strict=True):
        wd = g_sh.shape[1]
        d_sh, m_sh, v_sh = _rowwise("adamw_" + nm, f_adamw, [w[nm][0], g_sh, mom[nm][0], var[nm][0]], [], [(wd, F32)] * 3)
        out["grad"][nm], out["delta"][nm], out["m"][nm], out["v"][nm] = g_sh[None], d_sh[None], m_sh[None], v_sh[None]

    small = _small_allreduce_adamw([grads[nm] for nm in SMALL], [w[nm] for nm in SMALL],
                                   [mom[nm] for nm in SMALL], [var[nm] for nm in SMALL])
    for kind, arrs in zip(("grad", "delta", "m", "v"), small, strict=True):
        for nm, arr in zip(SMALL, arrs, strict=True):
            out[kind][nm] = arr

    return (loss, grad_x[None], *[out["grad"][nm] for nm in ALL_W], *[out["delta"][nm] for nm in ALL_W],
            *[out["m"][nm] for nm in ALL_W], *[out["v"][nm] for nm in ALL_W])
```

```python
import functools
import math

import jax
import jax.numpy as jnp
from jax import lax
from jax.experimental import pallas as pl
from jax.experimental.pallas import tpu as pltpu

F32 = jnp.float32
BF16 = jnp.bfloat16

MLA_HEADS = 8
MLA_Q_LORA = 256
MLA_KV_LORA = 128
MLA_NOPE = 64
MLA_ROPE = 32
MLA_V = 64
FOX_HEADS = 8
FOX_DIM = 64
ROPE_THETA = 10000.0
NORM_EPS = 1e-6
HALF_ROPE = MLA_ROPE // 2

ADAM_LR = 0.001
ADAM_B1 = 0.9
ADAM_B2 = 0.999
ADAM_EPS = 1e-08
ADAM_WD = 0.01
ADAM_STEP = 10

LANES = 128
VMEM_LIMIT = 56 * 1024 * 1024
ATT_TILE = 1024
FWD_GROUP_LOG2 = 1
FWD_GROUP = 1 << FWD_GROUP_LOG2
MM_VMEM_BUDGET = 40 * 1024 * 1024
MXU_WIDTH = 256
MXU_MACS_PER_S = 4.98e14
HBM_BYTES_PER_S = 3.2e12
STEP_OVERHEAD_S = 0.35e-6
NEG = -1e30
LOG2E = math.log2(math.e)
MESH = pl.DeviceIdType.MESH

V_ONES = 64
FOX_Q_F = 64
FOX_Q_L = 67
FOX_Q_ONES = 70
MLA_Q_L = 96

BIG = (("w_in", 2), ("w_uq", 2), ("w_uk", 2), ("w_uv", 2), ("w_o_mla", 2), ("w_o_fox", 2),
       ("w_out", 1), ("w_ff1", 2), ("w_ff2", 1))
MIXER = ("w_uq", "w_uk", "w_uv", "w_o_mla", "w_o_fox")
LATE = ("w_out", "w_ff1", "w_ff2")
EARLY_GRADS = ("w_o_mla", "w_o_fox", "w_out", "w_ff1", "w_ff2")
SMALL = ("ln_pre_mix", "ln_post_mix", "ln_pre_mlp", "ln_post_mlp", "b_in", "q_a_norm", "kv_a_norm")
ALL_W = ("ln_pre_mix", "ln_post_mix", "ln_pre_mlp", "ln_post_mlp", "w_in", "b_in", "q_a_norm", "w_uq",
         "kv_a_norm", "w_uk", "w_uv", "w_o_mla", "w_o_fox", "w_out", "w_ff1", "w_ff2")
N_CHIPS = 4
PACK_W = 1024
LOCAL_PIECES = 8

_NT = (((1,), (1,)), ((), ()))
_TN = (((0,), (0,)), ((), ()))


def _cparams(sem=None):
    return pltpu.CompilerParams(dimension_semantics=sem, vmem_limit_bytes=VMEM_LIMIT)


def _divisor_tile(n, limit, mult):
    if n <= limit:
        return n
    best = None
    t = mult
    while t <= limit:
        if n % t == 0:
            best = t
        t += mult
    assert best is not None, (n, limit, mult)
    return best


def _round_up(v, mult):
    return -(-v // mult) * mult


def _mm_tiles(m, k, n, io_bytes):
    best = None
    for tm in (2048, 1024, 512, 256, 128):
        if m % tm:
            continue
        for tn in range(LANES, min(n, 2048) + 1, LANES):
            if n % tn:
                continue
            vmem = 2 * (tm * k * 2 + k * tn * 2 + tm * tn * io_bytes) + tm * tn * 4
            if vmem > MM_VMEM_BUDGET:
                continue
            mxu = m * k * n * (_round_up(tn, MXU_WIDTH) / tn) / MXU_MACS_PER_S
            hbm = (m * k * 2 + (m // tm) * k * n * 2 + m * n * io_bytes) / HBM_BYTES_PER_S
            cost = max(mxu, hbm) + (m // tm) * (n // tn) * STEP_OVERHEAD_S
            if best is None or cost < best[0]:
                best = (cost, tm, tn)
    assert best is not None, (m, k, n)
    return best[1], best[2]


def _mm(a, b, *, out_dtype, name, bias=None, transpose_b=False, extras=(), epilogue=None):
    m, k = a.shape
    n = b.shape[0] if transpose_b else b.shape[1]
    assert (b.shape[1] if transpose_b else b.shape[0]) == k and a.dtype == BF16 and b.dtype == BF16
    out_dtypes = list(out_dtype) if isinstance(out_dtype, (list, tuple)) else [out_dtype]
    n_ex = len(extras)
    tm, tn = _mm_tiles(m, k, n, sum(jnp.dtype(dt).itemsize for dt in out_dtypes) + 4 * n_ex)

    def body(*refs):
        a_ref, b_ref = refs[:2]
        pos = 2
        bias_ref = None
        if bias is not None:
            bias_ref = refs[pos]
            pos += 1
        ex_refs = refs[pos:pos + n_ex]
        o_refs = refs[pos + n_ex:]
        if transpose_b:
            acc = lax.dot_general(a_ref[...], b_ref[...], _NT, preferred_element_type=F32)
        else:
            acc = jnp.dot(a_ref[...], b_ref[...], preferred_element_type=F32)
        if bias_ref is not None:
            acc = acc + bias_ref[...]
        vals = [acc] if epilogue is None else epilogue(acc, [r[...] for r in ex_refs])
        for ref, val in zip(o_refs, vals, strict=True):
            ref[...] = val.astype(ref.dtype)

    b_spec = pl.BlockSpec((tn, k), lambda i, j: (j, 0)) if transpose_b else pl.BlockSpec((k, tn), lambda i, j: (0, j))
    in_specs = [pl.BlockSpec((tm, k), lambda i, j: (i, 0)), b_spec]
    args = [a, b]
    if bias is not None:
        in_specs.append(pl.BlockSpec((1, tn), lambda i, j: (0, j)))
        args.append(bias)
    in_specs += [pl.BlockSpec((tm, tn), lambda i, j: (i, j)) for _ in extras]
    args += list(extras)
    res = pl.pallas_call(
        body, grid=(m // tm, n // tn), in_specs=in_specs,
        out_specs=[pl.BlockSpec((tm, tn), lambda i, j: (i, j)) for _ in out_dtypes],
        out_shape=[jax.ShapeDtypeStruct((m, n), dt) for dt in out_dtypes],
        compiler_params=_cparams(("parallel", "parallel")), name=name)(*args)
    return res if isinstance(out_dtype, (list, tuple)) else res[0]


def _mm_tn(a, b, *, name):
    s, m = a.shape
    s2, n = b.shape
    assert s == s2 and a.dtype == BF16 and b.dtype == BF16
    tm = _divisor_tile(m, 1024, LANES)
    tn = _divisor_tile(n, 2304, LANES)
    tk = _divisor_tile(s, 512, 16)

    def body(a_ref, b_ref, o_ref):
        @pl.when(pl.program_id(2) == 0)
        def _():
            o_ref[...] = jnp.zeros_like(o_ref)

        o_ref[...] += lax.dot_general(a_ref[...], b_ref[...], _TN, preferred_element_type=F32)

    return pl.pallas_call(
        body, grid=(m // tm, n // tn, s // tk),
        in_specs=[pl.BlockSpec((tk, tm), lambda i, j, k: (k, i)), pl.BlockSpec((tk, tn), lambda i, j, k: (k, j))],
        out_specs=pl.BlockSpec((tm, tn), lambda i, j, k: (i, j)),
        out_shape=jax.ShapeDtypeStruct((m, n), F32),
        compiler_params=_cparams(("parallel", "parallel", "arbitrary")), name=name)(a, b)


def _rowwise(name, fn, tiled, params, outs, reds=(), reverse=False):
    wins = [t if isinstance(t, tuple) else (t, 0, t.shape[1]) for t in tiled]
    s = wins[0][0].shape[0]
    row_bytes = sum(w * arr.dtype.itemsize for arr, _, w in wins) + sum(w * jnp.dtype(d).itemsize for w, d in outs)
    ts = _divisor_tile(s, max(16, min(1024, (6 * 1024 * 1024) // row_bytes)), 16)
    nt, npar, nout = len(wins), len(params), len(outs)
    n_tiles = s // ts

    def row(i):
        return n_tiles - 1 - i if reverse else i

    def body(*refs):
        tin = [r[...] for r in refs[:nt]]
        par = [r[...] for r in refs[nt:nt + npar]]
        out_refs = refs[nt + npar:nt + npar + nout]
        red_refs = refs[nt + npar + nout:]
        o, r = fn(tin, par)
        for ref, val in zip(out_refs, o, strict=True):
            ref[...] = val.astype(ref.dtype)
        if red_refs:
            @pl.when(pl.program_id(0) == 0)
            def _():
                for ref in red_refs:
                    ref[...] = jnp.zeros_like(ref)

            for ref, val in zip(red_refs, r, strict=True):
                ref[...] += val

    in_specs = [pl.BlockSpec((ts, w), functools.partial(lambda i, cb: (row(i), cb), cb=cb)) for _, cb, w in wins]
    in_specs += [pl.BlockSpec(p.shape, lambda i: (0, 0)) for p in params]
    out_specs = [pl.BlockSpec((ts, w), lambda i: (row(i), 0)) for w, _ in outs]
    out_specs += [pl.BlockSpec((1, w), lambda i: (0, 0)) for w in reds]
    out_shape = [jax.ShapeDtypeStruct((s, w), d) for w, d in outs]
    out_shape += [jax.ShapeDtypeStruct((1, w), F32) for w in reds]
    return pl.pallas_call(
        body, grid=(n_tiles,), in_specs=in_specs, out_specs=out_specs, out_shape=out_shape,
        compiler_params=_cparams(("arbitrary",)), name=name)(*[w[0] for w in wins], *params)


def _rms(x, g):
    r = lax.rsqrt(jnp.mean(x * x, axis=-1, keepdims=True) + NORM_EPS)
    return x * r * g, r


def _rms_bwd(x, g, dy):
    r = lax.rsqrt(jnp.mean(x * x, axis=-1, keepdims=True) + NORM_EPS)
    gy = dy * g
    dx = r * gy - x * (r * r * r) * jnp.mean(x * gy, axis=-1, keepdims=True)
    dg = jnp.sum(dy * (x * r), axis=0, keepdims=True)
    return dx, dg


def _sigmoid(x):
    return 1.0 / (1.0 + jnp.exp(-x))


def _split3(x):
    hi = x.astype(BF16).astype(F32)
    r = x - hi
    mid = r.astype(BF16).astype(F32)
    lo = (r - mid).astype(BF16).astype(F32)
    return hi, mid, lo


def _lane(shape):
    return lax.broadcasted_iota(jnp.int32, shape, 1)


def _put3(blk, lane, pos, pieces):
    for k, piece in enumerate(pieces):
        blk = jnp.where(lane == pos + k, piece, blk)
    return blk


def _lane_column(blk, lane, pos):
    return jnp.sum(jnp.where(lane == pos, blk, 0.0), axis=1, keepdims=True)


def _blocks(a, nh):
    return [a[:, h * LANES:(h + 1) * LANES] for h in range(nh)]


def _rope_block(x, c, sa, sb):
    return x * c + pltpu.roll(x, LANES - HALF_ROPE, 1) * sa + pltpu.roll(x, HALF_ROPE, 1) * sb


def _forget_cumsum(z, cb):
    s = z.shape[0]
    ts = _divisor_tile(s, 512, LANES)

    def body(x_ref, col_ref, carry):
        @pl.when(pl.program_id(0) == 0)
        def _():
            carry[...] = jnp.zeros_like(carry)

        x = x_ref[...]
        lf = jnp.minimum(x, 0.0) - jnp.log1p(jnp.exp(-jnp.abs(x)))
        r = lax.broadcasted_iota(jnp.int32, (ts, ts), 0)
        c = lax.broadcasted_iota(jnp.int32, (ts, ts), 1)
        tri = jnp.where(c <= r, 1.0, 0.0).astype(F32)
        col_ref[...] = jnp.dot(tri, lf, preferred_element_type=F32, precision=lax.Precision.HIGHEST) + carry[...]
        carry[...] += jnp.sum(lf, axis=0, keepdims=True)

    return pl.pallas_call(
        body, grid=(s // ts,),
        in_specs=[pl.BlockSpec((ts, LANES), lambda i: (i, cb))],
        out_specs=pl.BlockSpec((ts, LANES), lambda i: (i, 0)),
        out_shape=jax.ShapeDtypeStruct((s, LANES), F32),
        scratch_shapes=[pltpu.VMEM((1, LANES), F32)],
        compiler_params=_cparams(("arbitrary",)), name="forget_cumsum")(z)


def _forget_cumsum_bwd(dq, dk, z, cb, nh):
    s = z.shape[0]
    ts = _divisor_tile(s, 512, LANES)
    nt = s // ts
    wd = nh * LANES

    def body(dq_ref, dk_ref, x_ref, o_ref, carry):
        @pl.when(pl.program_id(0) == 0)
        def _():
            carry[...] = jnp.zeros_like(carry)

        lane = _lane((ts, LANES))
        df = jnp.zeros((ts, LANES), F32)
        for h in range(nh):
            cols = slice(h * LANES, (h + 1) * LANES)
            d_h = _lane_column(dq_ref[:, cols], lane, FOX_Q_F) - _lane_column(dk_ref[:, cols], lane, FOX_Q_ONES)
            df = jnp.where(lane == h, d_h, df)
        r = lax.broadcasted_iota(jnp.int32, (ts, ts), 0)
        c = lax.broadcasted_iota(jnp.int32, (ts, ts), 1)
        tri = jnp.where(c >= r, 1.0, 0.0).astype(F32)
        rc = jnp.dot(tri, df, preferred_element_type=F32, precision=lax.Precision.HIGHEST) + carry[...]
        carry[...] += jnp.sum(df, axis=0, keepdims=True)
        o_ref[...] = (rc * (1.0 / (1.0 + jnp.exp(x_ref[...])))).astype(o_ref.dtype)

    return pl.pallas_call(
        body, grid=(nt,),
        in_specs=[pl.BlockSpec((ts, wd), lambda i: (nt - 1 - i, 0)),
                  pl.BlockSpec((ts, wd), lambda i: (nt - 1 - i, 0)),
                  pl.BlockSpec((ts, LANES), lambda i: (nt - 1 - i, cb))],
        out_specs=pl.BlockSpec((ts, LANES), lambda i: (nt - 1 - i, 0)),
        out_shape=jax.ShapeDtypeStruct((s, LANES), BF16),
        scratch_shapes=[pltpu.VMEM((1, LANES), F32)],
        compiler_params=_cparams(("arbitrary",)), name="forget_cumsum_bwd")(dq, dk, z)


def _flash_fwd(q, k, v, scale, nh, l_lane, *, name):
    s = q.shape[0]
    t = min(ATT_TILE, s)
    half = t // 2 if t % (2 * LANES) == 0 else t
    nq = s // t
    c = scale * LOG2E

    def body(q_ref, k_ref, v_ref, o_ref, qb_ref):
        i = pl.program_id(1)
        qb = q_ref[...]

        def scores(q_rows, k0, nk):
            kb = k_ref[pl.ds(pl.multiple_of(k0, half), nk), :]
            return lax.dot_general(q_rows, kb, _NT, preferred_element_type=F32)

        def update(sc, k0, nk, carry):
            m, acc = carry
            m_new = jnp.maximum(m, jnp.max(sc, axis=1, keepdims=True))
            p = jnp.exp2((sc - m_new) * c)
            alpha = jnp.exp2((m - m_new) * c)
            vb = v_ref[pl.ds(pl.multiple_of(k0, half), nk), :]
            acc = alpha * acc + jnp.dot(p.astype(BF16), vb, preferred_element_type=F32)
            return m_new, acc

        def full_block(j, cr):
            return update(scores(qb, j * t, t), j * t, t, cr)

        def group(jj, cr):
            for n in range(FWD_GROUP):
                cr = full_block(FWD_GROUP * jj + n, cr)
            return cr

        def causal(sc):
            row = lax.broadcasted_iota(jnp.int32, sc.shape, 0)
            col = lax.broadcasted_iota(jnp.int32, sc.shape, 1)
            return jnp.where(col <= row, sc, NEG)

        init = (jnp.full((t, 1), NEG, F32), jnp.zeros((t, LANES), F32))
        n_groups = i >> FWD_GROUP_LOG2
        carry = lax.fori_loop(0, n_groups, group, init)
        carry = lax.fori_loop(n_groups * FWD_GROUP, i, full_block, carry)
        m, acc = update(causal(scores(qb, i * t, t)), i * t, t, carry)
        lane = _lane((t, LANES))
        l = _lane_column(acc, lane, V_ONES)
        o_ref[...] = (acc / l).astype(o_ref.dtype)
        big_l = m + jnp.log(l) / scale
        qb_ref[...] = _put3(qb.astype(F32), lane, l_lane, _split3(-big_l)).astype(qb_ref.dtype)

    head_rows = pl.BlockSpec((t, LANES), lambda h, i: (i, h))
    head_all = pl.BlockSpec((s, LANES), lambda h, i: (0, h))
    return pl.pallas_call(
        body, grid=(nh, nq), in_specs=[head_rows, head_all, head_all], out_specs=[head_rows, head_rows],
        out_shape=[jax.ShapeDtypeStruct(q.shape, BF16), jax.ShapeDtypeStruct(q.shape, BF16)],
        compiler_params=_cparams(("parallel", "arbitrary")), name=name)(q, k, v)


def _flash_bwd(qb, k, v, do, scale, nh, n_feat, *, name):
    s = qb.shape[0]
    t = min(ATT_TILE, s)
    half = t // 2 if t % (2 * LANES) == 0 else t
    nq = s // t
    c = scale * LOG2E

    def body(q_ref, k_ref, v_ref, do_ref, dk_ref, dv_ref, dq_ref):
        j = pl.program_id(1)
        kb = k_ref[...]
        vb = v_ref[...]

        @pl.when(j == 0)
        def _():
            dq_ref[...] = jnp.zeros_like(dq_ref)

        def part(q0, n_q, n_k, carry, q_off):
            dk_acc, dv_acc = carry
            rows = pl.ds(pl.multiple_of(q0, half), n_q)
            qblk = q_ref[rows, :]
            dob = do_ref[rows, :]
            kbb, vbb = kb[:n_k], vb[:n_k]
            st = lax.dot_general(kbb, qblk, _NT, preferred_element_type=F32)
            if q_off is not None:
                key = lax.broadcasted_iota(jnp.int32, st.shape, 0)
                qry = lax.broadcasted_iota(jnp.int32, st.shape, 1) + q_off
                st = jnp.where(key <= qry, st, NEG)
            pt = jnp.exp2(st * c)
            dv_new = jnp.dot(pt.astype(BF16), dob, preferred_element_type=F32)
            dpt = lax.dot_general(vbb, dob, _NT, preferred_element_type=F32)
            dsb = (pt * dpt).astype(BF16)
            dk_new = jnp.dot(dsb, qblk, preferred_element_type=F32)
            dq_ref[rows, :] += lax.dot_general(dsb, kbb, _TN, preferred_element_type=F32)
            if n_k == t:
                return dk_acc + dk_new, dv_acc + dv_new
            return (jnp.concatenate([dk_acc[:n_k] + dk_new, dk_acc[n_k:]], axis=0),
                    jnp.concatenate([dv_acc[:n_k] + dv_new, dv_acc[n_k:]], axis=0))

        def block(i, carry, masked):
            return part(i * t, t, t, carry, None)

        init = (jnp.zeros((t, LANES), F32), jnp.zeros((t, LANES), F32))
        carry = part(j * t + half, t - half, t, init, half) if half < t else part(j * t, t, t, init, 0)
        if half < t:
            carry = part(j * t, half, half, carry, 0)
        rest = nq - 1 - j
        carry = lax.cond((rest & 1) == 1, lambda cr: block(j + 1, cr, False), lambda cr: cr, carry)
        first = j + 1 + (rest & 1)

        def pair(ii, cr):
            i0 = first + 2 * ii
            return block(i0 + 1, block(i0, cr, False), False)

        dk_acc, dv_acc = lax.fori_loop(0, rest >> 1, pair, carry)
        dk_ref[...] = dk_acc * jnp.where(_lane((t, LANES)) < n_feat, scale, 1.0)
        dv_ref[...] = dv_acc

        @pl.when(j == nq - 1)
        def _():
            dq_ref[...] = dq_ref[...] * jnp.where(_lane((s, LANES)) < n_feat, scale, 1.0)

    head_rows = pl.BlockSpec((t, LANES), lambda h, j: (j, h))
    head_all = pl.BlockSpec((s, LANES), lambda h, j: (0, h))
    shp = jax.ShapeDtypeStruct(qb.shape, F32)
    return pl.pallas_call(
        body, grid=(nh, nq), in_specs=[head_all, head_rows, head_rows, head_all],
        out_specs=[head_rows, head_rows, head_all], out_shape=[shp, shp, shp],
        compiler_params=_cparams(("parallel", "arbitrary")), name=name)(qb, k, v, do)


def _with_delta(do, o, nh, *, name):
    def fn(ti, pa):
        lane = _lane((ti[0].shape[0], LANES))
        out = []
        for d_blk, o_blk in zip(_blocks(ti[0], nh), _blocks(ti[1], nh), strict=True):
            delta = jnp.sum(d_blk * o_blk.astype(F32), axis=1, keepdims=True)
            out.append(_put3(d_blk, lane, V_ONES, _split3(-delta)))
        return [jnp.concatenate(out, axis=1)], []

    (res,) = _rowwise(name, fn, [do, o], [], [(do.shape[1], BF16)])
    return res


def _pad_heads(a, nh):
    d = a.shape[-1] // nh
    a = a.reshape(a.shape[:-1] + (nh, d))
    a = jnp.pad(a, [(0, 0)] * (a.ndim - 1) + [(0, LANES - d)])
    return a.reshape(a.shape[:-2] + (nh * LANES,))


def _unpad_heads(a, nh, d):
    a = a.reshape(a.shape[:-1] + (nh, LANES))[..., :d]
    return a.reshape(a.shape[:-2] + (nh * d,))


def _pad_head_rows(w, nh):
    return _pad_heads(w.T, nh).T


def _unpad_head_rows(g, nh, d):
    return _unpad_heads(g.T, nh, d).T


class _ZLayout:
    def __init__(self, d):
        fw = FOX_HEADS * FOX_DIM
        self.src = {}
        off = 0
        for nm, w in (("cq", MLA_Q_LORA), ("ckv", MLA_KV_LORA), ("kr", MLA_ROPE), ("fq", fw), ("fk", fw),
                      ("fv", fw), ("fl", FOX_HEADS), ("ga", d), ("gb", d)):
            self.src[nm] = (off, w)
            off += w
        self.dst = {}
        off = 0
        for nm, w in (("fq", FOX_HEADS * LANES), ("fk", FOX_HEADS * LANES), ("fv", FOX_HEADS * LANES), ("ga", d),
                      ("gb", d), ("cq", MLA_Q_LORA), ("ckv", MLA_KV_LORA), ("kr", LANES), ("fl", LANES)):
            assert off % w == 0
            self.dst[nm] = (off, w)
            off += w
        self.width = off
        self.split = self.dst["cq"][0]
        assert all((o - self.split) % w == 0 for o, w in self.dst.values() if o >= self.split)

    def to_kernel(self, w):
        def seg(nm):
            off, wd = self.src[nm]
            return w[..., off:off + wd]

        def pad(a, left, total):
            return jnp.pad(a, [(0, 0)] * (a.ndim - 1) + [(left, total - left - a.shape[-1])])

        return jnp.concatenate([_pad_heads(seg("fq"), FOX_HEADS), _pad_heads(seg("fk"), FOX_HEADS),
                                _pad_heads(seg("fv"), FOX_HEADS), seg("ga"), seg("gb"), seg("cq"), seg("ckv"),
                                pad(seg("kr"), MLA_NOPE, LANES), pad(seg("fl"), 0, LANES)], axis=-1)

    def from_kernel(self, g):
        def seg(nm, lo=0, hi=None):
            off, wd = self.dst[nm]
            return g[..., off + lo:off + (wd if hi is None else hi)]

        return jnp.concatenate([seg("cq"), seg("ckv"), seg("kr", MLA_NOPE, MLA_NOPE + MLA_ROPE),
                                _unpad_heads(seg("fq"), FOX_HEADS, FOX_DIM), _unpad_heads(seg("fk"), FOX_HEADS, FOX_DIM),
                                _unpad_heads(seg("fv"), FOX_HEADS, FOX_DIM), seg("fl", 0, FOX_HEADS), seg("ga"),
                                seg("gb")], axis=-1)


def _local_step(x, positions, target, wts, late_weights, send_grads):
    s, d = x.shape
    zl = _ZLayout(d)
    hw = MLA_HEADS * LANES
    assert MLA_HEADS == FOX_HEADS
    scale_mla = (MLA_NOPE + MLA_ROPE) ** -0.5
    scale_fox = FOX_DIM ** -0.5

    inv_freq = ROPE_THETA ** (-jnp.arange(HALF_ROPE, dtype=F32) / HALF_ROPE)
    ang = positions.astype(F32)[:, None] * inv_freq
    cos, sin = jnp.cos(ang), jnp.sin(ang)
    tail = jnp.zeros((s, LANES - MLA_NOPE - MLA_ROPE), F32)
    rc = jnp.concatenate([jnp.ones((s, MLA_NOPE), F32), cos, cos, tail], axis=1)
    ra = jnp.concatenate([jnp.zeros((s, MLA_NOPE), F32), -sin, jnp.zeros((s, HALF_ROPE), F32), tail], axis=1)
    rb = jnp.concatenate([jnp.zeros((s, MLA_NOPE + HALF_ROPE), F32), sin, tail], axis=1)

    w_in = zl.to_kernel(wts["w_in"])
    b_in = zl.to_kernel(wts["b_in"])

    def f_norm_in(ti, pa):
        y, _ = _rms(ti[0], pa[0])
        return [y], []

    (h,) = _rowwise("norm_in", f_norm_in, [x], [wts["ln_pre_mix"]], [(d, BF16)])
    z_lo = _mm(h, w_in[:, :zl.split], bias=b_in[:, :zl.split], out_dtype=BF16, name="proj_in_lo")
    z = _mm(h, w_in[:, zl.split:], bias=b_in[:, zl.split:], out_dtype=F32, name="proj_in_hi")

    def zwin(nm):
        off, wd = zl.dst[nm]
        return (z_lo, off // wd, wd) if off < zl.split else (z, (off - zl.split) // wd, wd)

    wts = {**wts, **late_weights("mixer", z)}
    w_uq = _pad_heads(wts["w_uq"], MLA_HEADS)
    w_ukv = jnp.concatenate([_pad_heads(wts["w_uk"], MLA_HEADS), _pad_heads(wts["w_uv"], MLA_HEADS)], axis=1)
    w_o_mla = _pad_head_rows(wts["w_o_mla"], MLA_HEADS)
    w_o_fox = _pad_head_rows(wts["w_o_fox"], FOX_HEADS)

    def f_mla_norms(ti, pa):
        cqn, _ = _rms(ti[0], pa[0])
        ckvn, _ = _rms(ti[1], pa[1])
        return [cqn, ckvn], []

    cqn, ckvn = _rowwise("mla_norms", f_mla_norms, [zwin("cq"), zwin("ckv")],
                         [wts["q_a_norm"], wts["kv_a_norm"]], [(MLA_Q_LORA, BF16), (MLA_KV_LORA, BF16)])
    qf = _mm(cqn, w_uq, out_dtype=F32, name="proj_uq")
    kv = _mm(ckvn, w_ukv, out_dtype=BF16, name="proj_ukv")

    def f_rope_q(ti, pa):
        xq, c_, a_, b_ = ti
        return [jnp.concatenate([_rope_block(blk, c_, a_, b_) for blk in _blocks(xq, MLA_HEADS)], axis=1)], []

    (q_mla,) = _rowwise("rope_q", f_rope_q, [qf, rc, ra, rb], [], [(hw, BF16)])

    def f_mla_kv(ti, pa):
        kn, vn, kr, c_, a_, b_ = ti
        lane = _lane(kr.shape)
        k_tail = jnp.where((lane >= MLA_Q_L) & (lane < MLA_Q_L + 3), 1.0, _rope_block(kr, c_, a_, b_))
        ones_v = (lane >= V_ONES) & (lane < V_ONES + 3)
        k_out = [jnp.where(lane < MLA_NOPE, blk.astype(F32), k_tail) for blk in _blocks(kn, MLA_HEADS)]
        v_out = [jnp.where(ones_v, 1.0, blk.astype(F32)) for blk in _blocks(vn, MLA_HEADS)]
        return [jnp.concatenate(k_out, axis=1), jnp.concatenate(v_out, axis=1)], []

    k_mla, v_mla = _rowwise("mla_kv", f_mla_kv, [(kv, 0, hw), (kv, 1, hw), zwin("kr"), rc, ra, rb], [],
                            [(hw, BF16), (hw, BF16)])
    o_mla, qb_mla = _flash_fwd(q_mla, k_mla, v_mla, scale_mla, MLA_HEADS, MLA_Q_L, name="mla_fwd")

    fl_cb = zwin("fl")[1]
    fcol = _forget_cumsum(z, fl_cb)

    def f_fox_qkv(ti, pa):
        fq, fk, fv, fc = (a.astype(F32) for a in ti)
        lane = _lane(fc.shape)
        ones_q = (lane >= FOX_Q_ONES) & (lane < FOX_Q_ONES + 3)
        ones_k = (lane >= FOX_Q_F) & (lane < FOX_Q_ONES)
        ones_v = (lane >= V_ONES) & (lane < V_ONES + 3)
        q_out, k_out, v_out = [], [], []
        for hh, (qblk, kblk, vblk) in enumerate(zip(_blocks(fq, FOX_HEADS), _blocks(fk, FOX_HEADS),
                                                    _blocks(fv, FOX_HEADS), strict=True)):
            f_h = _lane_column(fc, lane, hh) * (1.0 / scale_fox)
            q_out.append(_put3(jnp.where(ones_q, 1.0, qblk), lane, FOX_Q_F, _split3(f_h)))
            k_out.append(_put3(jnp.where(ones_k, 1.0, kblk), lane, FOX_Q_ONES, _split3(-f_h)))
            v_out.append(jnp.where(ones_v, 1.0, vblk))
        return [jnp.concatenate(q_out, axis=1), jnp.concatenate(k_out, axis=1), jnp.concatenate(v_out, axis=1)], []

    q_fox, k_fox, v_fox = _rowwise("fox_qkv", f_fox_qkv, [zwin("fq"), zwin("fk"), zwin("fv"), fcol],
                                   [], [(hw, BF16)] * 3)
    o_fox, qb_fox = _flash_fwd(q_fox, k_fox, v_fox, scale_fox, FOX_HEADS, FOX_Q_L, name="fox_fwd")

    y_mla = _mm(o_mla, w_o_mla, out_dtype=BF16, name="proj_o_mla")
    y_fox = _mm(o_fox, w_o_fox, out_dtype=BF16, name="proj_o_fox")

    def f_gate(ti, pa):
        ga, gb, ya, yb = (a.astype(F32) for a in ti)
        return [_sigmoid(ga) * ya + _sigmoid(gb) * yb], []

    (merged,) = _rowwise("gate", f_gate, [zwin("ga"), zwin("gb"), y_mla, y_fox], [], [(d, BF16)])
    wts = {**wts, **late_weights("mlp", merged)}
    mix = _mm(merged, wts["w_out"], out_dtype=F32, name="proj_out")

    def f_resid1(ti, pa):
        xa, mx = ti
        y, _ = _rms(mx, pa[0])
        x1 = xa + y
        h2, _ = _rms(x1, pa[1])
        return [x1, h2], []

    x1, h2 = _rowwise("resid_mix", f_resid1, [x, mix], [wts["ln_post_mix"], wts["ln_pre_mlp"]], [(d, F32), (d, BF16)])

    def relu2(acc, ex):
        r = jnp.maximum(acc, 0.0)
        return [acc, r * r]

    u, act = _mm(h2, wts["w_ff1"], out_dtype=[BF16, BF16], name="ff1", epilogue=relu2)
    mo = _mm(act, wts["w_ff2"], out_dtype=F32, name="ff2")

    def f_loss(ti, pa):
        xa, mv, tg = ti
        y, _ = _rms(mv, pa[0])
        err = (xa + y) - tg
        g2 = err / d
        dmo, dg = _rms_bwd(mv, pa[0], g2)
        return [g2, dmo], [jnp.sum(err * err, axis=0, keepdims=True), dg]

    g2, d_mo, loss_cols, g_ln_post_mlp = _rowwise("loss", f_loss, [x1, mo, target], [wts["ln_post_mlp"]],
                                                  [(d, F32), (d, BF16)], [d, d])
    loss = 0.5 * jnp.sum(loss_cols) / d

    grads = {"ln_post_mlp": g_ln_post_mlp}
    grads["w_ff2"] = _mm_tn(act, d_mo, name="grad_ff2")

    def relu2_bwd(acc, ex):
        return [acc * (2.0 * jnp.maximum(ex[0], 0.0))]

    (d_u,) = _mm(d_mo, wts["w_ff2"], out_dtype=[BF16], name="ff2_bwd", transpose_b=True, extras=[u], epilogue=relu2_bwd)
    grads["w_ff1"] = _mm_tn(h2, d_u, name="grad_ff1")
    d_h2 = _mm(d_u, wts["w_ff1"], out_dtype=F32, name="ff1_bwd", transpose_b=True)

    def f_resid1_bwd(ti, pa):
        gres, dh2, x1v, mx = ti
        dx1n, dg_pre_mlp = _rms_bwd(x1v, pa[1], dh2)
        dx1 = gres + dx1n
        dmix, dg_post_mix = _rms_bwd(mx, pa[0], dx1)
        return [dx1, dmix], [dg_post_mix, dg_pre_mlp]

    d_x1, d_mix, grads["ln_post_mix"], grads["ln_pre_mlp"] = _rowwise(
        "resid_mix_bwd", f_resid1_bwd, [g2, d_h2, x1, mix], [wts["ln_post_mix"], wts["ln_pre_mlp"]],
        [(d, F32), (d, BF16)], [d, d])
    grads["w_out"] = _mm_tn(merged, d_mix, name="grad_out")
    d_merged = _mm(d_mix, wts["w_out"], out_dtype=BF16, name="proj_out_bwd", transpose_b=True)

    def f_gate_bwd(ti, pa):
        dm, ga, gb, ya, yb = (a.astype(F32) for a in ti)
        sa, sb = _sigmoid(ga), _sigmoid(gb)
        return [dm * sa, dm * sb, dm * ya * (sa * (1.0 - sa)), dm * yb * (sb * (1.0 - sb))], []

    d_ya, d_yb, d_ga, d_gb = _rowwise("gate_bwd", f_gate_bwd,
                                      [d_merged, zwin("ga"), zwin("gb"), y_mla, y_fox], [],
                                      [(d, BF16)] * 4)
    grads["w_o_mla"] = _unpad_head_rows(_mm_tn(o_mla, d_ya, name="grad_o_mla"), MLA_HEADS, MLA_V)
    grads["w_o_fox"] = _unpad_head_rows(_mm_tn(o_fox, d_yb, name="grad_o_fox"), FOX_HEADS, FOX_DIM)
    w_o_mla = w_o_mla + send_grads("early", {nm: grads[nm] for nm in EARLY_GRADS}).astype(BF16)
    do_mla = _with_delta(_mm(d_ya, w_o_mla, out_dtype=F32, name="proj_o_mla_bwd", transpose_b=True), o_mla,
                         MLA_HEADS, name="mla_delta")
    do_fox = _with_delta(_mm(d_yb, w_o_fox, out_dtype=F32, name="proj_o_fox_bwd", transpose_b=True), o_fox,
                         FOX_HEADS, name="fox_delta")

    dk_mla, dv_mla, dq_mla = _flash_bwd(qb_mla, k_mla, v_mla, do_mla, scale_mla, MLA_HEADS, MLA_NOPE + MLA_ROPE,
                                        name="mla_bwd")
    dk_fox, dv_fox, dq_fox = _flash_bwd(qb_fox, k_fox, v_fox, do_fox, scale_fox, FOX_HEADS, FOX_DIM, name="fox_bwd")
    d_fl = _forget_cumsum_bwd(dq_fox, dk_fox, z, fl_cb, FOX_HEADS)

    def f_rope_q_bwd(ti, pa):
        g, c_, a_, b_ = ti
        return [jnp.concatenate([_rope_block(blk, c_, -a_, -b_) for blk in _blocks(g, MLA_HEADS)], axis=1)], []

    (d_qf,) = _rowwise("rope_q_bwd", f_rope_q_bwd, [dq_mla, rc, ra, rb], [], [(hw, BF16)])
    grads["w_uq"] = _unpad_heads(_mm_tn(cqn, d_qf, name="grad_uq"), MLA_HEADS, MLA_NOPE + MLA_ROPE)
    d_cqn = _mm(d_qf, w_uq, out_dtype=F32, name="proj_uq_bwd", transpose_b=True)

    def f_mla_kv_bwd(ti, pa):
        gk, gv, c_, a_, b_ = ti
        k_blocks = _blocks(gk, MLA_HEADS)
        tot = k_blocks[0]
        for blk in k_blocks[1:]:
            tot = tot + blk
        return [jnp.concatenate([gk, gv], axis=1), _rope_block(tot, c_, -a_, -b_)], []

    d_kv, d_kr = _rowwise("mla_kv_bwd", f_mla_kv_bwd, [dk_mla, dv_mla, rc, ra, rb], [], [(2 * hw, BF16), (LANES, BF16)])
    g_ukv = _mm_tn(ckvn, d_kv, name="grad_ukv")
    grads["w_uk"] = _unpad_heads(g_ukv[:, :hw], MLA_HEADS, MLA_NOPE)
    grads["w_uv"] = _unpad_heads(g_ukv[:, hw:], MLA_HEADS, MLA_V)
    d_ckvn = _mm(d_kv, w_ukv, out_dtype=F32, name="proj_ukv_bwd", transpose_b=True)

    def f_mla_norms_bwd(ti, pa):
        cq, ckv, dcqn, dckvn = ti
        dcq, dg_q = _rms_bwd(cq, pa[0], dcqn)
        dckv, dg_kv = _rms_bwd(ckv, pa[1], dckvn)
        return [dcq, dckv], [dg_q, dg_kv]

    d_cq, d_ckv, grads["q_a_norm"], grads["kv_a_norm"] = _rowwise(
        "mla_norms_bwd", f_mla_norms_bwd, [zwin("cq"), zwin("ckv"), d_cqn, d_ckvn],
        [wts["q_a_norm"], wts["kv_a_norm"]], [(MLA_Q_LORA, BF16), (MLA_KV_LORA, BF16)], [MLA_Q_LORA, MLA_KV_LORA])

    d_z = jnp.concatenate([dq_fox.astype(BF16), dk_fox.astype(BF16), dv_fox.astype(BF16), d_ga, d_gb, d_cq, d_ckv,
                           d_kr, d_fl], axis=1)
    assert d_z.shape[1] == zl.width

    def f_bias(ti, pa):
        return [], [jnp.sum(ti[0].astype(F32), axis=0, keepdims=True)]

    (g_b_in,) = _rowwise("grad_b_in", f_bias, [d_z], [], [], [zl.width])
    grads["b_in"] = zl.from_kernel(g_b_in)
    grads["w_in"] = zl.from_kernel(_mm_tn(h, d_z, name="grad_in"))
    tok = send_grads("late", {nm: grads[nm] for nm, _ in BIG if nm not in EARLY_GRADS})
    d_h = _mm(d_z, w_in, bias=jnp.zeros((1, d), F32) + tok, out_dtype=F32, name="proj_in_bwd", transpose_b=True)

    def f_norm_in_bwd(ti, pa):
        dx1v, dh, xa = ti
        dxn, dg = _rms_bwd(xa, pa[0], dh)
        return [dx1v + dxn], [dg]

    grad_x, grads["ln_pre_mix"] = _rowwise("norm_in_bwd", f_norm_in_bwd, [d_x1, d_h, x], [wts["ln_pre_mix"]],
                                           [(d, F32)], [d])
    return loss, grad_x, grads


class _PackLayout:
    def __init__(self, shapes):
        self.shapes = list(shapes)
        self.width = _round_up(max(b for _, b in shapes), LANES)
        self.bands = []
        row = 0
        shelf = []
        for idx, (a, b) in enumerate(shapes):
            if 2 * _round_up(b, LANES) > self.width:
                self.bands.append((row, _round_up(a, 32), [(idx, 0)]))
                row += _round_up(a, 32)
            else:
                shelf.append(idx)
        col, items = 0, []
        for idx in shelf:
            wb = _round_up(shapes[idx][1], LANES)
            if col + wb > self.width:
                hgt = max(_round_up(shapes[i][0], 32) for i, _ in items)
                self.bands.append((row, hgt, items))
                row += hgt
                col, items = 0, []
            items.append((idx, col))
            col += wb
        if items:
            hgt = max(_round_up(shapes[i][0], 32) for i, _ in items)
            self.bands.append((row, hgt, items))
            row += hgt
        self.rows = _round_up(row, 16 * LOCAL_PIECES)

    def pack(self, arrs):
        lead = arrs[0].shape[:-2]
        no_pad = [(0, 0)] * len(lead)
        bands = []
        for _, hgt, items in self.bands:
            parts = []
            for k, (idx, col) in enumerate(items):
                a, b = self.shapes[idx]
                nxt = items[k + 1][1] if k + 1 < len(items) else self.width
                parts.append(jnp.pad(arrs[idx], no_pad + [(0, hgt - a), (0, nxt - col - b)]))
            bands.append(parts[0] if len(parts) == 1 else jnp.concatenate(parts, axis=-1))
        used = sum(hgt for _, hgt, _ in self.bands)
        if used < self.rows:
            bands.append(jnp.zeros(lead + (self.rows - used, self.width), arrs[0].dtype))
        return jnp.concatenate(bands, axis=-2)

    def unpack(self, packed):
        out = [None] * len(self.shapes)
        for row, _, items in self.bands:
            for idx, col in items:
                a, b = self.shapes[idx]
                out[idx] = packed[..., row:row + a, col:col + b]
        return out


def _to_shards(g, axis):
    if axis == 0:
        return g.reshape(N_CHIPS, g.shape[0] // N_CHIPS, g.shape[1])
    return jnp.stack(jnp.split(g, N_CHIPS, axis=1))


def _from_shards(s4, axis):
    n, a, b = s4.shape
    if axis == 0:
        return s4.reshape(n * a, b)
    return jnp.concatenate([s4[ch] for ch in range(n)], axis=1)


ANY = pl.BlockSpec(memory_space=pl.ANY)


def _place():
    return lax.axis_index("x"), lax.axis_index("y"), lax.axis_index("c")


def _gather_weights(wpk):
    rows, wd = wpk.shape
    half = rows // 2

    def body(w_ref, out_ref, send_sems, recv_sems, local_sems):
        x, y, c = _place()
        sibling = (x, y, 1 - c)
        chips = [(1 - x, y), (x, 1 - y), (1 - x, 1 - y)]

        def slab(chip, hf):
            return out_ref.at[2 * chip[0] + chip[1], pl.ds(hf * half, half), :]

        def copy(k, chip, hf, to, src=None):
            return pltpu.make_async_remote_copy(
                src_ref=slab(chip, hf) if src is None else src, dst_ref=slab(chip, hf),
                send_sem=send_sems.at[k], recv_sem=recv_sems.at[k], device_id=to, device_id_type=MESH)

        piece = rows // LOCAL_PIECES
        mine = [pltpu.make_async_copy(w_ref.at[pl.ds(n * piece, piece), :],
                                      out_ref.at[2 * x + y, pl.ds(n * piece, piece), :], local_sems.at[n])
                for n in range(LOCAL_PIECES)]
        for cp in mine:
            cp.start()
        first = [copy(j, (x, y), c, (*chip, c), src=w_ref.at[pl.ds(c * half, half), :]) for j, chip in enumerate(chips)]
        for cp in first:
            cp.start()
        passed = [copy(3 + j, chip, c, sibling) for j, chip in enumerate(chips)]
        for j, chip in enumerate(chips):
            copy(j, chip, c, (x, y, c)).wait_recv()
            passed[j].start()
        for j, chip in enumerate(chips):
            copy(3 + j, chip, 1 - c, (x, y, c)).wait_recv()
        for cp in first + passed:
            cp.wait_send()
        for cp in mine:
            cp.wait()

    assert rows % (16 * LOCAL_PIECES) == 0
    return pl.pallas_call(
        body, out_shape=jax.ShapeDtypeStruct((N_CHIPS, rows, wd), wpk.dtype),
        in_specs=[ANY], out_specs=ANY,
        scratch_shapes=[pltpu.SemaphoreType.DMA((6,)), pltpu.SemaphoreType.DMA((6,)),
                        pltpu.SemaphoreType.DMA((LOCAL_PIECES,))],
        name="gather_weights")(wpk)


HBM = pl.BlockSpec(memory_space=pltpu.HBM)
SEM = pl.BlockSpec(memory_space=pltpu.SEMAPHORE)
EFFECT = pltpu.SideEffectType.DATAFLOW_SIDE_EFFECTING
N_LATE = 6


def _gather_late_start(wpk, tag):
    rows, wd = wpk.shape
    half = rows // 2

    def body(w_ref, land_ref, send_sems, recv_sems, w_thru, land_thru, token):
        x, y, c = _place()
        chips = [(1 - x, y), (x, 1 - y), (1 - x, 1 - y)]
        for j, chip in enumerate(chips):
            for to_core in range(2):
                pltpu.make_async_remote_copy(
                    src_ref=w_ref.at[pl.ds(c * half, half), :],
                    dst_ref=land_ref.at[2 * x + y, pl.ds(c * half, half), :],
                    send_sem=send_sems.at[2 * j + to_core], recv_sem=recv_sems.at[2 * j + c],
                    device_id=(*chip, to_core), device_id_type=MESH).start()
        token[...] = jnp.zeros_like(token)

    land = lax.empty((N_CHIPS, rows, wd), wpk.dtype)
    return pl.pallas_call(
        body, name="gather_" + tag + "_start",
        out_shape=(pltpu.SemaphoreType.DMA((N_LATE,)), pltpu.SemaphoreType.DMA((N_LATE,)),
                   pltpu.HBM(wpk.shape, wpk.dtype), pltpu.HBM(land.shape, land.dtype),
                   jax.ShapeDtypeStruct((8, LANES), F32)),
        in_specs=(HBM, HBM), out_specs=(SEM, SEM, HBM, HBM, pl.BlockSpec(memory_space=pltpu.VMEM)),
        input_output_aliases={0: 2, 1: 3},
        compiler_params=pltpu.CompilerParams(has_side_effects=EFFECT),
    )(pltpu.with_memory_space_constraint(wpk, pltpu.HBM), pltpu.with_memory_space_constraint(land, pltpu.HBM))


def _gather_late_wait(send_sems, recv_sems, w_thru, land_thru, after, tag):
    rows, wd = w_thru.shape
    half = rows // 2

    def body(w_ref, land_ref, send_sems, recv_sems, after_ref, w_dead, land_out):
        x, y, c = _place()
        for n in range(N_LATE):
            cp = pltpu.make_async_remote_copy(
                src_ref=w_ref.at[pl.ds(0, half), :], dst_ref=land_ref.at[0, pl.ds(0, half), :],
                send_sem=send_sems.at[n], recv_sem=recv_sems.at[n], device_id=(x, y, c), device_id_type=MESH)
            cp.wait_send()
            cp.wait_recv()

    return pl.pallas_call(
        body, name="gather_" + tag + "_wait",
        out_shape=(pltpu.HBM(w_thru.shape, w_thru.dtype), pltpu.HBM(land_thru.shape, land_thru.dtype)),
        in_specs=(HBM, HBM, SEM, SEM, ANY), out_specs=(HBM, HBM), input_output_aliases={0: 0, 1: 1},
        compiler_params=pltpu.CompilerParams(has_side_effects=EFFECT),
    )(w_thru, land_thru, send_sems, recv_sems, after)[1]


N_PART = 7


def _reduce_start(gbf, tag):
    _, _, hrows, wd = gbf.shape

    def body(g_ref, land_ref, send_sems, recv_sems, g_thru, land_thru, token):
        x, y, c = _place()
        chips = [(1 - x, y), (x, 1 - y), (1 - x, 1 - y)]
        for j, chip in enumerate(chips):
            for to_core in range(2):
                pltpu.make_async_remote_copy(
                    src_ref=g_ref.at[2 * chip[0] + chip[1], to_core], dst_ref=land_ref.at[2 * j + c],
                    send_sem=send_sems.at[2 * j + to_core], recv_sem=recv_sems.at[2 * j + c],
                    device_id=(*chip, to_core), device_id_type=MESH).start()
        pltpu.make_async_remote_copy(
            src_ref=g_ref.at[2 * x + y, 1 - c], dst_ref=land_ref.at[N_PART - 1],
            send_sem=send_sems.at[N_PART - 1], recv_sem=recv_sems.at[N_PART - 1],
            device_id=(x, y, 1 - c), device_id_type=MESH).start()
        token[...] = jnp.zeros_like(token)

    land = lax.empty((N_PART, hrows, wd), gbf.dtype)
    return pl.pallas_call(
        body, name="reduce_" + tag + "_start",
        out_shape=(pltpu.SemaphoreType.DMA((N_PART,)), pltpu.SemaphoreType.DMA((N_PART,)),
                   pltpu.HBM(gbf.shape, gbf.dtype), pltpu.HBM(land.shape, land.dtype),
                   jax.ShapeDtypeStruct((8, LANES), F32)),
        in_specs=(HBM, HBM), out_specs=(SEM, SEM, HBM, HBM, pl.BlockSpec(memory_space=pltpu.VMEM)),
        input_output_aliases={0: 2, 1: 3},
        compiler_params=pltpu.CompilerParams(has_side_effects=EFFECT),
    )(pltpu.with_memory_space_constraint(gbf, pltpu.HBM), pltpu.with_memory_space_constraint(land, pltpu.HBM))


def _reduce_wait(send_sems, recv_sems, g_thru, land_thru, after, tag):
    def body(g_ref, land_ref, send_sems, recv_sems, after_ref, g_dead, land_out):
        x, y, c = _place()
        for n in range(N_PART):
            cp = pltpu.make_async_remote_copy(
                src_ref=g_ref.at[0, 0], dst_ref=land_ref.at[0], send_sem=send_sems.at[n], recv_sem=recv_sems.at[n],
                device_id=(x, y, c), device_id_type=MESH)
            cp.wait_send()
            cp.wait_recv()

    return pl.pallas_call(
        body, name="reduce_" + tag + "_wait",
        out_shape=(pltpu.HBM(g_thru.shape, g_thru.dtype), pltpu.HBM(land_thru.shape, land_thru.dtype)),
        in_specs=(HBM, HBM, SEM, SEM, ANY), out_specs=(HBM, HBM), input_output_aliases={0: 0, 1: 1},
        compiler_params=pltpu.CompilerParams(has_side_effects=EFFECT),
    )(g_thru, land_thru, send_sems, recv_sems, after)[1]


def _sibling_swap(mine):
    def body(m_ref, out_ref, send_sem, recv_sem):
        x, y, c = _place()
        cp = pltpu.make_async_remote_copy(
            src_ref=m_ref, dst_ref=out_ref, send_sem=send_sem, recv_sem=recv_sem,
            device_id=(x, y, 1 - c), device_id_type=MESH)
        cp.start()
        cp.wait()

    return pl.pallas_call(
        body, out_shape=jax.ShapeDtypeStruct(mine.shape, mine.dtype), in_specs=[ANY], out_specs=ANY,
        scratch_shapes=[pltpu.SemaphoreType.DMA, pltpu.SemaphoreType.DMA], name="grad_sibling_swap")(mine)


def _adamw(w, g, m, v):
    m = ADAM_B1 * m + (1.0 - ADAM_B1) * g
    v = ADAM_B2 * v + (1.0 - ADAM_B2) * (g * g)
    m_hat = m / (1.0 - ADAM_B1 ** ADAM_STEP)
    v_hat = v / (1.0 - ADAM_B2 ** ADAM_STEP)
    delta = -ADAM_LR * (m_hat / (jnp.sqrt(v_hat) + ADAM_EPS) + ADAM_WD * w)
    return delta, m, v


def _small_allreduce_adamw(gs, ws, ms, vs):
    n_dev = 8
    n_par = len(gs)
    wd = PACK_W
    chunks = []
    for p, g in enumerate(gs):
        for off in range(0, g.shape[1], wd):
            chunks.append((p, len(chunks), off, min(wd, g.shape[1] - off)))
    rows = _round_up(len(chunks), 8)

    def body(*refs):
        g_refs, w_refs, m_refs, v_refs = (refs[k * n_par:(k + 1) * n_par] for k in range(4))
        go_refs, d_refs, mo_refs, vo_refs = (refs[(4 + k) * n_par:(5 + k) * n_par] for k in range(4))
        mine_ref, all_ref, send_sems, recv_sems = refs[8 * n_par:]
        x, y, c = _place()
        me, sibling = (x, y, c), (x, y, 1 - c)
        chips = [(1 - x, y), (x, 1 - y), (1 - x, 1 - y)]

        def slot(px, py, pc):
            return all_ref.at[4 * px + 2 * py + pc]

        def copy(k, block, to, src=None):
            return pltpu.make_async_remote_copy(
                src_ref=slot(*block) if src is None else src, dst_ref=slot(*block),
                send_sem=send_sems.at[k], recv_sem=recv_sems.at[k], device_id=to, device_id_type=MESH)

        mine_ref[...] = jnp.zeros_like(mine_ref)
        for p, row, off, width in chunks:
            mine_ref[row:row + 1, 0:width] = g_refs[p][:, off:off + width]
        all_ref[4 * x + 2 * y + c] = mine_ref[...]
        first = [copy(0, me, sibling, src=mine_ref)]
        first += [copy(1 + j, me, (*chip, c), src=mine_ref) for j, chip in enumerate(chips)]
        for cp in first:
            cp.start()
        passed = [copy(4 + j, (*chip, c), sibling) for j, chip in enumerate(chips)]
        for j, chip in enumerate(chips):
            copy(1 + j, (*chip, c), me).wait_recv()
            passed[j].start()
        copy(0, sibling, me).wait_recv()
        for j, chip in enumerate(chips):
            copy(4 + j, (*chip, 1 - c), me).wait_recv()
        for cp in first + passed:
            cp.wait_send()
        tot = jnp.zeros((rows, wd), F32)
        for dev in range(n_dev):
            tot = tot + all_ref[dev]
        mine_ref[...] = tot
        for p, row, off, width in chunks:
            cols = slice(off, off + width)
            g = mine_ref[row:row + 1, 0:width]
            delta, m_new, v_new = _adamw(w_refs[p][:, cols], g, m_refs[p][:, cols], v_refs[p][:, cols])
            go_refs[p][:, cols] = g
            d_refs[p][:, cols] = delta
            mo_refs[p][:, cols] = m_new
            vo_refs[p][:, cols] = v_new

    vm = pl.BlockSpec(memory_space=pltpu.VMEM)
    shp = [jax.ShapeDtypeStruct(g.shape, F32) for g in gs]
    res = pl.pallas_call(
        body, out_shape=shp * 4, in_specs=[vm] * (4 * n_par), out_specs=[vm] * (4 * n_par),
        scratch_shapes=[pltpu.VMEM((rows, wd), F32), pltpu.VMEM((n_dev, rows, wd), F32),
                        pltpu.SemaphoreType.DMA((7,)), pltpu.SemaphoreType.DMA((7,))],
        name="small_allreduce_adamw")(*gs, *ws, *ms, *vs)
    return [res[k * n_par:(k + 1) * n_par] for k in range(4)]


def kernel(x, positions, ln_pre_mix, ln_post_mix, ln_pre_mlp, ln_post_mlp, w_in, b_in, q_a_norm, w_uq, kv_a_norm, w_uk, w_uv, w_o_mla, w_o_fox, w_out, w_ff1, w_ff2, loss_target, m_ln_pre_mix, m_ln_post_mix, m_ln_pre_mlp, m_ln_post_mlp, m_w_in, m_b_in, m_q_a_norm, m_w_uq, m_kv_a_norm, m_w_uk, m_w_uv, m_w_o_mla, m_w_o_fox, m_w_out, m_w_ff1, m_w_ff2, v_ln_pre_mix, v_ln_post_mix, v_ln_pre_mlp, v_ln_post_mlp, v_w_in, v_b_in, v_q_a_norm, v_w_uq, v_kv_a_norm, v_w_uk, v_w_uv, v_w_o_mla, v_w_o_fox, v_w_out, v_w_ff1, v_w_ff2):
    w = dict(ln_pre_mix=ln_pre_mix, ln_post_mix=ln_post_mix, ln_pre_mlp=ln_pre_mlp, ln_post_mlp=ln_post_mlp, w_in=w_in,
             b_in=b_in, q_a_norm=q_a_norm, w_uq=w_uq, kv_a_norm=kv_a_norm, w_uk=w_uk, w_uv=w_uv, w_o_mla=w_o_mla,
             w_o_fox=w_o_fox, w_out=w_out, w_ff1=w_ff1, w_ff2=w_ff2)
    mom = dict(ln_pre_mix=m_ln_pre_mix, ln_post_mix=m_ln_post_mix, ln_pre_mlp=m_ln_pre_mlp, ln_post_mlp=m_ln_post_mlp,
               w_in=m_w_in, b_in=m_b_in, q_a_norm=m_q_a_norm, w_uq=m_w_uq, kv_a_norm=m_kv_a_norm, w_uk=m_w_uk,
               w_uv=m_w_uv, w_o_mla=m_w_o_mla, w_o_fox=m_w_o_fox, w_out=m_w_out, w_ff1=m_w_ff1, w_ff2=m_w_ff2)
    var = dict(ln_pre_mix=v_ln_pre_mix, ln_post_mix=v_ln_post_mix, ln_pre_mlp=v_ln_pre_mlp, ln_post_mlp=v_ln_post_mlp,
               w_in=v_w_in, b_in=v_b_in, q_a_norm=v_q_a_norm, w_uq=v_w_uq, kv_a_norm=v_kv_a_norm, w_uk=v_w_uk,
               w_uv=v_w_uv, w_o_mla=v_w_o_mla, w_o_fox=v_w_o_fox, w_out=v_w_out, w_ff1=v_w_ff1, w_ff2=v_w_ff2)

    big_names = [nm for nm, _ in BIG]
    c = lax.axis_index("c")
    chip = 2 * lax.axis_index("x") + lax.axis_index("y")

    axes = dict(BIG)

    def assemble(names, lay, gathered):
        return {nm: _from_shards(s4, axes[nm] - 1) for nm, s4 in zip(names, lay.unpack(gathered), strict=True)}

    groups = {"mixer": [nm for nm in big_names if nm in MIXER], "mlp": [nm for nm in big_names if nm in LATE]}
    first_names = [nm for nm in big_names if nm not in MIXER and nm not in LATE]
    full = {nm: wv for nm, wv in w.items() if nm in SMALL}
    travelling = {}
    for tag, names in groups.items():
        lay = _PackLayout([w[nm].shape[1:] for nm in names])
        shard = lay.pack([w[nm][0].astype(BF16) for nm in names])
        started = _gather_late_start(shard, tag)
        travelling[tag] = (names, lay, shard, started[:4])
        full["b_in"] = full["b_in"] + started[4][0, 0]
    lay_first = _PackLayout([w[nm].shape[1:] for nm in first_names])
    full.update(assemble(first_names, lay_first,
                         _gather_weights(lay_first.pack([w[nm][0].astype(BF16) for nm in first_names]))))

    def late_weights(tag, after):
        names, lay, shard, handles = travelling[tag]
        land = _gather_late_wait(*handles, after, tag)
        land = lax.dynamic_update_slice(land, shard[None], (chip, 0, 0))
        return assemble(names, lay, land)

    grad_groups = {"early": [nm for nm in big_names if nm in EARLY_GRADS],
                   "late": [nm for nm in big_names if nm not in EARLY_GRADS]}
    sent = {}

    def send_grads(tag, g):
        names = grad_groups[tag]
        lay = _PackLayout([w[nm].shape[1:] for nm in names])
        hrows = lay.rows // 2
        gbf = lay.pack([_to_shards(g[nm], axes[nm] - 1).astype(BF16) for nm in names])
        started = _reduce_start(gbf.reshape(N_CHIPS, 2, hrows, lay.width), tag)
        own = lay.pack([lax.dynamic_slice_in_dim(g[nm], chip * w[nm].shape[axes[nm]], w[nm].shape[axes[nm]],
                                                 axis=axes[nm] - 1) for nm in names])
        sent[tag] = (names, lay, started[:4], lax.dynamic_slice_in_dim(own, c * hrows, hrows, axis=0))
        return started[4][0, 0]

    loss_local, grad_x, grads = _local_step(x[0], positions[0], loss_target[0], full, late_weights, send_grads)
    loss = lax.psum(loss_local, ("x", "y", "c"))

    def f_add8(ti, pa):
        tot = ti[0]
        for part in ti[1:]:
            tot = tot + part.astype(F32)
        return [tot], []

    reduced = []
    for tag, (names, lay, handles, own) in sent.items():
        parts = _reduce_wait(*handles, grad_x, tag)
        reduced.append(_rowwise("grad_add_" + tag, f_add8, [own] + [parts[n] for n in range(N_PART)], [],
                                [(lay.width, F32)])[0])

    assert len({lay.width for _, lay, _, _ in sent.values()}) == 1
    red = jnp.concatenate(reduced, axis=0)
    sib = _sibling_swap(red)
    lower, upper = jnp.where(c == 0, red, sib), jnp.where(c == 0, sib, red)
    g_by_name, row = {}, 0
    for names, lay, _, _ in sent.values():
        hrows = lay.rows // 2
        both = jnp.concatenate([lower[row:row + hrows], upper[row:row + hrows]], axis=0)
        g_by_name.update(zip(names, lay.unpack(both), strict=True))
        row += hrows
    g_shards = [g_by_name[nm] for nm in big_names]

    def f_adamw(ti, pa):
        wv, gv, mv, vv = ti
        return list(_adamw(wv, gv, mv, vv)), []

    out = {"grad": {}, "delta": {}, "m": {}, "v": {}}
    for nm, g_sh in zip(big_names, g_shards, strict=True):
        wd = g_sh.shape[1]
        d_sh, m_sh, v_sh = _rowwise("adamw_" + nm, f_adamw, [w[nm][0], g_sh, mom[nm][0], var[nm][0]], [], [(wd, F32)] * 3)
        out["grad"][nm], out["delta"][nm], out["m"][nm], out["v"][nm] = g_sh[None], d_sh[None], m_sh[None], v_sh[None]

    small = _small_allreduce_adamw([grads[nm] for nm in SMALL], [w[nm] for nm in SMALL],
                                   [mom[nm] for nm in SMALL], [var[nm] for nm in SMALL])
    for kind, arrs in zip(("grad", "delta", "m", "v"), small, strict=True):
        for nm, arr in zip(SMALL, arrs, strict=True):
            out[kind][nm] = arr

    return (loss, grad_x[None], *[out["grad"][nm] for nm in ALL_W], *[out["delta"][nm] for nm in ALL_W],
            *[out["m"][nm] for nm in ALL_W], *[out["v"][nm] for nm in ALL_W])
```

```python
import functools
import math

import jax
import jax.numpy as jnp
from jax import lax
from jax.experimental import pallas as pl
from jax.experimental.pallas import tpu as pltpu

F32 = jnp.float32
BF16 = jnp.bfloat16

MLA_HEADS = 8
MLA_Q_LORA = 256
MLA_KV_LORA = 128
MLA_NOPE = 64
MLA_ROPE = 32
MLA_V = 64
FOX_HEADS = 8
FOX_DIM = 64
ROPE_THETA = 10000.0
NORM_EPS = 1e-6
HALF_ROPE = MLA_ROPE // 2

ADAM_LR = 0.001
ADAM_B1 = 0.9
ADAM_B2 = 0.999
ADAM_EPS = 1e-08
ADAM_WD = 0.01
ADAM_STEP = 10

LANES = 128
VMEM_LIMIT = 56 * 1024 * 1024
ATT_TILE = 1024
FWD_GROUP_LOG2 = 1
FWD_GROUP = 1 << FWD_GROUP_LOG2
MM_VMEM_BUDGET = 40 * 1024 * 1024
MXU_WIDTH = 256
MXU_MACS_PER_S = 4.98e14
HBM_BYTES_PER_S = 3.2e12
STEP_OVERHEAD_S = 0.35e-6
NEG = -1e30
LOG2E = math.log2(math.e)
MESH = pl.DeviceIdType.MESH

V_ONES = 64
FOX_Q_F = 64
FOX_Q_L = 67
FOX_Q_ONES = 70
MLA_Q_L = 96

BIG = (("w_in", 2), ("w_uq", 2), ("w_uk", 2), ("w_uv", 2), ("w_o_mla", 2), ("w_o_fox", 2),
       ("w_out", 1), ("w_ff1", 2), ("w_ff2", 1))
MIXER = ("w_uq", "w_uk", "w_uv", "w_o_mla", "w_o_fox")
LATE = ("w_out", "w_ff1", "w_ff2")
EARLY_GRADS = ("w_o_mla", "w_o_fox", "w_out", "w_ff1", "w_ff2")
SMALL = ("ln_pre_mix", "ln_post_mix", "ln_pre_mlp", "ln_post_mlp", "b_in", "q_a_norm", "kv_a_norm")
ALL_W = ("ln_pre_mix", "ln_post_mix", "ln_pre_mlp", "ln_post_mlp", "w_in", "b_in", "q_a_norm", "w_uq",
         "kv_a_norm", "w_uk", "w_uv", "w_o_mla", "w_o_fox", "w_out", "w_ff1", "w_ff2")
N_CHIPS = 4
PACK_W = 1024
LOCAL_PIECES = 8

_NT = (((1,), (1,)), ((), ()))
_TN = (((0,), (0,)), ((), ()))


def _cparams(sem=None):
    return pltpu.CompilerParams(dimension_semantics=sem, vmem_limit_bytes=VMEM_LIMIT)


def _divisor_tile(n, limit, mult):
    if n <= limit:
        return n
    best = None
    t = mult
    while t <= limit:
        if n % t == 0:
            best = t
        t += mult
    assert best is not None, (n, limit, mult)
    return best


def _round_up(v, mult):
    return -(-v // mult) * mult


def _mm_tiles(m, k, n, io_bytes):
    best = None
    for tm in (2048, 1024, 512, 256, 128):
        if m % tm:
            continue
        for tn in range(LANES, min(n, 2048) + 1, LANES):
            if n % tn:
                continue
            vmem = 2 * (tm * k * 2 + k * tn * 2 + tm * tn * io_bytes) + tm * tn * 4
            if vmem > MM_VMEM_BUDGET:
                continue
            mxu = m * k * n * (_round_up(tn, MXU_WIDTH) / tn) / MXU_MACS_PER_S
            hbm = (m * k * 2 + (m // tm) * k * n * 2 + m * n * io_bytes) / HBM_BYTES_PER_S
            cost = max(mxu, hbm) + (m // tm) * (n // tn) * STEP_OVERHEAD_S
            if best is None or cost < best[0]:
                best = (cost, tm, tn)
    assert best is not None, (m, k, n)
    return best[1], best[2]


def _mm(a, b, *, out_dtype, name, bias=None, transpose_b=False, extras=(), epilogue=None):
    m, k = a.shape
    n = b.shape[0] if transpose_b else b.shape[1]
    assert (b.shape[1] if transpose_b else b.shape[0]) == k and a.dtype == BF16 and b.dtype == BF16
    out_dtypes = list(out_dtype) if isinstance(out_dtype, (list, tuple)) else [out_dtype]
    n_ex = len(extras)
    tm, tn = _mm_tiles(m, k, n, sum(jnp.dtype(dt).itemsize for dt in out_dtypes) + 4 * n_ex)

    def body(*refs):
        a_ref, b_ref = refs[:2]
        pos = 2
        bias_ref = None
        if bias is not None:
            bias_ref = refs[pos]
            pos += 1
        ex_refs = refs[pos:pos + n_ex]
        o_refs = refs[pos + n_ex:]
        if transpose_b:
            acc = lax.dot_general(a_ref[...], b_ref[...], _NT, preferred_element_type=F32)
        else:
            acc = jnp.dot(a_ref[...], b_ref[...], preferred_element_type=F32)
        if bias_ref is not None:
            acc = acc + bias_ref[...]
        vals = [acc] if epilogue is None else epilogue(acc, [r[...] for r in ex_refs])
        for ref, val in zip(o_refs, vals, strict=True):
            ref[...] = val.astype(ref.dtype)

    b_spec = pl.BlockSpec((tn, k), lambda i, j: (j, 0)) if transpose_b else pl.BlockSpec((k, tn), lambda i, j: (0, j))
    in_specs = [pl.BlockSpec((tm, k), lambda i, j: (i, 0)), b_spec]
    args = [a, b]
    if bias is not None:
        in_specs.append(pl.BlockSpec((1, tn), lambda i, j: (0, j)))
        args.append(bias)
    in_specs += [pl.BlockSpec((tm, tn), lambda i, j: (i, j)) for _ in extras]
    args += list(extras)
    res = pl.pallas_call(
        body, grid=(m // tm, n // tn), in_specs=in_specs,
        out_specs=[pl.BlockSpec((tm, tn), lambda i, j: (i, j)) for _ in out_dtypes],
        out_shape=[jax.ShapeDtypeStruct((m, n), dt) for dt in out_dtypes],
        compiler_params=_cparams(("parallel", "parallel")), name=name)(*args)
    return res if isinstance(out_dtype, (list, tuple)) else res[0]


def _mm_tn(a, b, *, name):
    s, m = a.shape
    s2, n = b.shape
    assert s == s2 and a.dtype == BF16 and b.dtype == BF16
    tm = _divisor_tile(m, 1024, LANES)
    tn = _divisor_tile(n, 2304, LANES)
    tk = _divisor_tile(s, 512, 16)

    def body(a_ref, b_ref, o_ref):
        @pl.when(pl.program_id(2) == 0)
        def _():
            o_ref[...] = jnp.zeros_like(o_ref)

        o_ref[...] += lax.dot_general(a_ref[...], b_ref[...], _TN, preferred_element_type=F32)

    return pl.pallas_call(
        body, grid=(m // tm, n // tn, s // tk),
        in_specs=[pl.BlockSpec((tk, tm), lambda i, j, k: (k, i)), pl.BlockSpec((tk, tn), lambda i, j, k: (k, j))],
        out_specs=pl.BlockSpec((tm, tn), lambda i, j, k: (i, j)),
        out_shape=jax.ShapeDtypeStruct((m, n), F32),
        compiler_params=_cparams(("parallel", "parallel", "arbitrary")), name=name)(a, b)


def _rowwise(name, fn, tiled, params, outs, reds=(), reverse=False):
    wins = [t if isinstance(t, tuple) else (t, 0, t.shape[1]) for t in tiled]
    s = wins[0][0].shape[0]
    row_bytes = sum(w * arr.dtype.itemsize for arr, _, w in wins) + sum(w * jnp.dtype(d).itemsize for w, d in outs)
    ts = _divisor_tile(s, max(16, min(1024, (6 * 1024 * 1024) // row_bytes)), 16)
    nt, npar, nout = len(wins), len(params), len(outs)
    n_tiles = s // ts

    def row(i):
        return n_tiles - 1 - i if reverse else i

    def body(*refs):
        tin = [r[...] for r in refs[:nt]]
        par = [r[...] for r in refs[nt:nt + npar]]
        out_refs = refs[nt + npar:nt + npar + nout]
        red_refs = refs[nt + npar + nout:]
        o, r = fn(tin, par)
        for ref, val in zip(out_refs, o, strict=True):
            ref[...] = val.astype(ref.dtype)
        if red_refs:
            @pl.when(pl.program_id(0) == 0)
            def _():
                for ref in red_refs:
                    ref[...] = jnp.zeros_like(ref)

            for ref, val in zip(red_refs, r, strict=True):
                ref[...] += val

    in_specs = [pl.BlockSpec((ts, w), functools.partial(lambda i, cb: (row(i), cb), cb=cb)) for _, cb, w in wins]
    in_specs += [pl.BlockSpec(p.shape, lambda i: (0, 0)) for p in params]
    out_specs = [pl.BlockSpec((ts, w), lambda i: (row(i), 0)) for w, _ in outs]
    out_specs += [pl.BlockSpec((1, w), lambda i: (0, 0)) for w in reds]
    out_shape = [jax.ShapeDtypeStruct((s, w), d) for w, d in outs]
    out_shape += [jax.ShapeDtypeStruct((1, w), F32) for w in reds]
    return pl.pallas_call(
        body, grid=(n_tiles,), in_specs=in_specs, out_specs=out_specs, out_shape=out_shape,
        compiler_params=_cparams(("arbitrary",)), name=name)(*[w[0] for w in wins], *params)


def _rms(x, g):
    r = lax.rsqrt(jnp.mean(x * x, axis=-1, keepdims=True) + NORM_EPS)
    return x * r * g, r


def _rms_bwd(x, g, dy):
    r = lax.rsqrt(jnp.mean(x * x, axis=-1, keepdims=True) + NORM_EPS)
    gy = dy * g
    dx = r * gy - x * (r * r * r) * jnp.mean(x * gy, axis=-1, keepdims=True)
    dg = jnp.sum(dy * (x * r), axis=0, keepdims=True)
    return dx, dg


def _sigmoid(x):
    return 1.0 / (1.0 + jnp.exp(-x))


def _split3(x):
    hi = x.astype(BF16).astype(F32)
    r = x - hi
    mid = r.astype(BF16).astype(F32)
    lo = (r - mid).astype(BF16).astype(F32)
    return hi, mid, lo


def _lane(shape):
    return lax.broadcasted_iota(jnp.int32, shape, 1)


def _put3(blk, lane, pos, pieces):
    for k, piece in enumerate(pieces):
        blk = jnp.where(lane == pos + k, piece, blk)
    return blk


def _lane_column(blk, lane, pos):
    return jnp.sum(jnp.where(lane == pos, blk, 0.0), axis=1, keepdims=True)


def _blocks(a, nh):
    return [a[:, h * LANES:(h + 1) * LANES] for h in range(nh)]


def _rope_block(x, c, sa, sb):
    return x * c + pltpu.roll(x, LANES - HALF_ROPE, 1) * sa + pltpu.roll(x, HALF_ROPE, 1) * sb


def _forget_cumsum(z, cb):
    s = z.shape[0]
    ts = _divisor_tile(s, 512, LANES)

    def body(x_ref, col_ref, carry):
        @pl.when(pl.program_id(0) == 0)
        def _():
            carry[...] = jnp.zeros_like(carry)

        x = x_ref[...]
        lf = jnp.minimum(x, 0.0) - jnp.log1p(jnp.exp(-jnp.abs(x)))
        r = lax.broadcasted_iota(jnp.int32, (ts, ts), 0)
        c = lax.broadcasted_iota(jnp.int32, (ts, ts), 1)
        tri = jnp.where(c <= r, 1.0, 0.0).astype(F32)
        col_ref[...] = jnp.dot(tri, lf, preferred_element_type=F32, precision=lax.Precision.HIGHEST) + carry[...]
        carry[...] += jnp.sum(lf, axis=0, keepdims=True)

    return pl.pallas_call(
        body, grid=(s // ts,),
        in_specs=[pl.BlockSpec((ts, LANES), lambda i: (i, cb))],
        out_specs=pl.BlockSpec((ts, LANES), lambda i: (i, 0)),
        out_shape=jax.ShapeDtypeStruct((s, LANES), F32),
        scratch_shapes=[pltpu.VMEM((1, LANES), F32)],
        compiler_params=_cparams(("arbitrary",)), name="forget_cumsum")(z)


def _forget_cumsum_bwd(dq, dk, z, cb, nh):
    s = z.shape[0]
    ts = _divisor_tile(s, 512, LANES)
    nt = s // ts
    wd = nh * LANES

    def body(dq_ref, dk_ref, x_ref, o_ref, carry):
        @pl.when(pl.program_id(0) == 0)
        def _():
            carry[...] = jnp.zeros_like(carry)

        lane = _lane((ts, LANES))
        df = jnp.zeros((ts, LANES), F32)
        for h in range(nh):
            cols = slice(h * LANES, (h + 1) * LANES)
            d_h = _lane_column(dq_ref[:, cols], lane, FOX_Q_F) - _lane_column(dk_ref[:, cols], lane, FOX_Q_ONES)
            df = jnp.where(lane == h, d_h, df)
        r = lax.broadcasted_iota(jnp.int32, (ts, ts), 0)
        c = lax.broadcasted_iota(jnp.int32, (ts, ts), 1)
        tri = jnp.where(c >= r, 1.0, 0.0).astype(F32)
        rc = jnp.dot(tri, df, preferred_element_type=F32, precision=lax.Precision.HIGHEST) + carry[...]
        carry[...] += jnp.sum(df, axis=0, keepdims=True)
        o_ref[...] = (rc * (1.0 / (1.0 + jnp.exp(x_ref[...])))).astype(o_ref.dtype)

    return pl.pallas_call(
        body, grid=(nt,),
        in_specs=[pl.BlockSpec((ts, wd), lambda i: (nt - 1 - i, 0)),
                  pl.BlockSpec((ts, wd), lambda i: (nt - 1 - i, 0)),
                  pl.BlockSpec((ts, LANES), lambda i: (nt - 1 - i, cb))],
        out_specs=pl.BlockSpec((ts, LANES), lambda i: (nt - 1 - i, 0)),
        out_shape=jax.ShapeDtypeStruct((s, LANES), BF16),
        scratch_shapes=[pltpu.VMEM((1, LANES), F32)],
        compiler_params=_cparams(("arbitrary",)), name="forget_cumsum_bwd")(dq, dk, z)


def _flash_fwd(q, k, v, scale, nh, l_lane, *, name):
    s = q.shape[0]
    t = min(ATT_TILE, s)
    half = t // 2 if t % (2 * LANES) == 0 else t
    nq = s // t
    c = scale * LOG2E

    def body(q_ref, k_ref, v_ref, o_ref, qb_ref):
        i = pl.program_id(1)
        qb = q_ref[...]

        def scores(q_rows, k0, nk):
            kb = k_ref[pl.ds(pl.multiple_of(k0, half), nk), :]
            return lax.dot_general(q_rows, kb, _NT, preferred_element_type=F32)

        def update(sc, k0, nk, carry):
            m, acc = carry
            m_new = jnp.maximum(m, jnp.max(sc, axis=1, keepdims=True))
            p = jnp.exp2((sc - m_new) * c)
            alpha = jnp.exp2((m - m_new) * c)
            vb = v_ref[pl.ds(pl.multiple_of(k0, half), nk), :]
            acc = alpha * acc + jnp.dot(p.astype(BF16), vb, preferred_element_type=F32)
            return m_new, acc

        def full_block(j, cr):
            return update(scores(qb, j * t, t), j * t, t, cr)

        def group(jj, cr):
            for n in range(FWD_GROUP):
                cr = full_block(FWD_GROUP * jj + n, cr)
            return cr

        def causal(sc):
            row = lax.broadcasted_iota(jnp.int32, sc.shape, 0)
            col = lax.broadcasted_iota(jnp.int32, sc.shape, 1)
            return jnp.where(col <= row, sc, NEG)

        init = (jnp.full((t, 1), NEG, F32), jnp.zeros((t, LANES), F32))
        n_groups = i >> FWD_GROUP_LOG2
        carry = lax.fori_loop(0, n_groups, group, init)
        carry = lax.fori_loop(n_groups * FWD_GROUP, i, full_block, carry)
        m, acc = update(causal(scores(qb, i * t, t)), i * t, t, carry)
        lane = _lane((t, LANES))
        l = _lane_column(acc, lane, V_ONES)
        o_ref[...] = (acc / l).astype(o_ref.dtype)
        big_l = m + jnp.log(l) / scale
        qb_ref[...] = _put3(qb.astype(F32), lane, l_lane, _split3(-big_l)).astype(qb_ref.dtype)

    head_rows = pl.BlockSpec((t, LANES), lambda h, i: (i, h))
    head_all = pl.BlockSpec((s, LANES), lambda h, i: (0, h))
    return pl.pallas_call(
        body, grid=(nh, nq), in_specs=[head_rows, head_all, head_all], out_specs=[head_rows, head_rows],
        out_shape=[jax.ShapeDtypeStruct(q.shape, BF16), jax.ShapeDtypeStruct(q.shape, BF16)],
        compiler_params=_cparams(("parallel", "arbitrary")), name=name)(q, k, v)


def _flash_bwd(qb, k, v, do, scale, nh, n_feat, *, name):
    s = qb.shape[0]
    t = min(ATT_TILE, s)
    half = t // 2 if t % (2 * LANES) == 0 else t
    nq = s // t
    c = scale * LOG2E

    def body(q_ref, k_ref, v_ref, do_ref, dk_ref, dv_ref, dq_ref):
        j = pl.program_id(1)
        kb = k_ref[...]
        vb = v_ref[...]

        @pl.when(j == 0)
        def _():
            dq_ref[...] = jnp.zeros_like(dq_ref)

        def part(q0, n_q, n_k, carry, q_off):
            dk_acc, dv_acc = carry
            rows = pl.ds(pl.multiple_of(q0, half), n_q)
            qblk = q_ref[rows, :]
            dob = do_ref[rows, :]
            kbb, vbb = kb[:n_k], vb[:n_k]
            st = lax.dot_general(kbb, qblk, _NT, preferred_element_type=F32)
            if q_off is not None:
                key = lax.broadcasted_iota(jnp.int32, st.shape, 0)
                qry = lax.broadcasted_iota(jnp.int32, st.shape, 1) + q_off
                st = jnp.where(key <= qry, st, NEG)
            pt = jnp.exp2(st * c)
            dv_new = jnp.dot(pt.astype(BF16), dob, preferred_element_type=F32)
            dpt = lax.dot_general(vbb, dob, _NT, preferred_element_type=F32)
            dsb = (pt * dpt).astype(BF16)
            dk_new = jnp.dot(dsb, qblk, preferred_element_type=F32)
            dq_ref[rows, :] += lax.dot_general(dsb, kbb, _TN, preferred_element_type=F32)
            if n_k == t:
                return dk_acc + dk_new, dv_acc + dv_new
            return (jnp.concatenate([dk_acc[:n_k] + dk_new, dk_acc[n_k:]], axis=0),
                    jnp.concatenate([dv_acc[:n_k] + dv_new, dv_acc[n_k:]], axis=0))

        def block(i, carry, masked):
            return part(i * t, t, t, carry, None)

        init = (jnp.zeros((t, LANES), F32), jnp.zeros((t, LANES), F32))
        carry = part(j * t + half, t - half, t, init, half) if half < t else part(j * t, t, t, init, 0)
        if half < t:
            carry = part(j * t, half, half, carry, 0)
        rest = nq - 1 - j
        carry = lax.cond((rest & 1) == 1, lambda cr: block(j + 1, cr, False), lambda cr: cr, carry)
        first = j + 1 + (rest & 1)

        def pair(ii, cr):
            i0 = first + 2 * ii
            return block(i0 + 1, block(i0, cr, False), False)

        dk_acc, dv_acc = lax.fori_loop(0, rest >> 1, pair, carry)
        dk_ref[...] = dk_acc * jnp.where(_lane((t, LANES)) < n_feat, scale, 1.0)
        dv_ref[...] = dv_acc

        @pl.when(j == nq - 1)
        def _():
            dq_ref[...] = dq_ref[...] * jnp.where(_lane((s, LANES)) < n_feat, scale, 1.0)

    head_rows = pl.BlockSpec((t, LANES), lambda h, j: (j, h))
    head_all = pl.BlockSpec((s, LANES), lambda h, j: (0, h))
    shp = jax.ShapeDtypeStruct(qb.shape, F32)
    return pl.pallas_call(
        body, grid=(nh, nq), in_specs=[head_all, head_rows, head_rows, head_all],
        out_specs=[head_rows, head_rows, head_all], out_shape=[shp, shp, shp],
        compiler_params=_cparams(("parallel", "arbitrary")), name=name)(qb, k, v, do)


def _with_delta(do, o, nh, *, name):
    def fn(ti, pa):
        lane = _lane((ti[0].shape[0], LANES))
        out = []
        for d_blk, o_blk in zip(_blocks(ti[0], nh), _blocks(ti[1], nh), strict=True):
            delta = jnp.sum(d_blk * o_blk.astype(F32), axis=1, keepdims=True)
            out.append(_put3(d_blk, lane, V_ONES, _split3(-delta)))
        return [jnp.concatenate(out, axis=1)], []

    (res,) = _rowwise(name, fn, [do, o], [], [(do.shape[1], BF16)])
    return res


def _pad_heads(a, nh):
    d = a.shape[-1] // nh
    a = a.reshape(a.shape[:-1] + (nh, d))
    a = jnp.pad(a, [(0, 0)] * (a.ndim - 1) + [(0, LANES - d)])
    return a.reshape(a.shape[:-2] + (nh * LANES,))


def _unpad_heads(a, nh, d):
    a = a.reshape(a.shape[:-1] + (nh, LANES))[..., :d]
    return a.reshape(a.shape[:-2] + (nh * d,))


def _pad_head_rows(w, nh):
    return _pad_heads(w.T, nh).T


def _unpad_head_rows(g, nh, d):
    return _unpad_heads(g.T, nh, d).T


class _ZLayout:
    def __init__(self, d):
        fw = FOX_HEADS * FOX_DIM
        self.src = {}
        off = 0
        for nm, w in (("cq", MLA_Q_LORA), ("ckv", MLA_KV_LORA), ("kr", MLA_ROPE), ("fq", fw), ("fk", fw),
                      ("fv", fw), ("fl", FOX_HEADS), ("ga", d), ("gb", d)):
            self.src[nm] = (off, w)
            off += w
        self.dst = {}
        off = 0
        for nm, w in (("fq", FOX_HEADS * LANES), ("fk", FOX_HEADS * LANES), ("fv", FOX_HEADS * LANES), ("ga", d),
                      ("gb", d), ("cq", MLA_Q_LORA), ("ckv", MLA_KV_LORA), ("kr", LANES), ("fl", LANES)):
            assert off % w == 0
            self.dst[nm] = (off, w)
            off += w
        self.width = off
        self.split = self.dst["cq"][0]
        assert all((o - self.split) % w == 0 for o, w in self.dst.values() if o >= self.split)

    def to_kernel(self, w):
        def seg(nm):
            off, wd = self.src[nm]
            return w[..., off:off + wd]

        def pad(a, left, total):
            return jnp.pad(a, [(0, 0)] * (a.ndim - 1) + [(left, total - left - a.shape[-1])])

        return jnp.concatenate([_pad_heads(seg("fq"), FOX_HEADS), _pad_heads(seg("fk"), FOX_HEADS),
                                _pad_heads(seg("fv"), FOX_HEADS), seg("ga"), seg("gb"), seg("cq"), seg("ckv"),
                                pad(seg("kr"), MLA_NOPE, LANES), pad(seg("fl"), 0, LANES)], axis=-1)

    def from_kernel(self, g):
        def seg(nm, lo=0, hi=None):
            off, wd = self.dst[nm]
            return g[..., off + lo:off + (wd if hi is None else hi)]

        return jnp.concatenate([seg("cq"), seg("ckv"), seg("kr", MLA_NOPE, MLA_NOPE + MLA_ROPE),
                                _unpad_heads(seg("fq"), FOX_HEADS, FOX_DIM), _unpad_heads(seg("fk"), FOX_HEADS, FOX_DIM),
                                _unpad_heads(seg("fv"), FOX_HEADS, FOX_DIM), seg("fl", 0, FOX_HEADS), seg("ga"),
                                seg("gb")], axis=-1)


def _local_step(x, positions, target, wts, late_weights, send_grads):
    s, d = x.shape
    zl = _ZLayout(d)
    hw = MLA_HEADS * LANES
    assert MLA_HEADS == FOX_HEADS
    scale_mla = (MLA_NOPE + MLA_ROPE) ** -0.5
    scale_fox = FOX_DIM ** -0.5

    inv_freq = ROPE_THETA ** (-jnp.arange(HALF_ROPE, dtype=F32) / HALF_ROPE)
    ang = positions.astype(F32)[:, None] * inv_freq
    cos, sin = jnp.cos(ang), jnp.sin(ang)
    tail = jnp.zeros((s, LANES - MLA_NOPE - MLA_ROPE), F32)
    rc = jnp.concatenate([jnp.ones((s, MLA_NOPE), F32), cos, cos, tail], axis=1)
    ra = jnp.concatenate([jnp.zeros((s, MLA_NOPE), F32), -sin, jnp.zeros((s, HALF_ROPE), F32), tail], axis=1)
    rb = jnp.concatenate([jnp.zeros((s, MLA_NOPE + HALF_ROPE), F32), sin, tail], axis=1)

    w_in = zl.to_kernel(wts["w_in"])
    b_in = zl.to_kernel(wts["b_in"])

    def f_norm_in(ti, pa):
        y, _ = _rms(ti[0], pa[0])
        return [y], []

    (h,) = _rowwise("norm_in", f_norm_in, [x], [wts["ln_pre_mix"]], [(d, BF16)])
    z_lo = _mm(h, w_in[:, :zl.split], bias=b_in[:, :zl.split], out_dtype=BF16, name="proj_in_lo")
    z = _mm(h, w_in[:, zl.split:], bias=b_in[:, zl.split:], out_dtype=F32, name="proj_in_hi")

    def zwin(nm):
        off, wd = zl.dst[nm]
        return (z_lo, off // wd, wd) if off < zl.split else (z, (off - zl.split) // wd, wd)

    wts = {**wts, **late_weights("mixer", z)}
    w_uq = _pad_heads(wts["w_uq"], MLA_HEADS)
    w_ukv = jnp.concatenate([_pad_heads(wts["w_uk"], MLA_HEADS), _pad_heads(wts["w_uv"], MLA_HEADS)], axis=1)
    w_o_mla = _pad_head_rows(wts["w_o_mla"], MLA_HEADS)
    w_o_fox = _pad_head_rows(wts["w_o_fox"], FOX_HEADS)

    def f_mla_norms(ti, pa):
        cqn, _ = _rms(ti[0], pa[0])
        ckvn, _ = _rms(ti[1], pa[1])
        return [cqn, ckvn], []

    cqn, ckvn = _rowwise("mla_norms", f_mla_norms, [zwin("cq"), zwin("ckv")],
                         [wts["q_a_norm"], wts["kv_a_norm"]], [(MLA_Q_LORA, BF16), (MLA_KV_LORA, BF16)])
    qf = _mm(cqn, w_uq, out_dtype=F32, name="proj_uq")
    kv = _mm(ckvn, w_ukv, out_dtype=BF16, name="proj_ukv")

    def f_rope_q(ti, pa):
        xq, c_, a_, b_ = ti
        return [jnp.concatenate([_rope_block(blk, c_, a_, b_) for blk in _blocks(xq, MLA_HEADS)], axis=1)], []

    (q_mla,) = _rowwise("rope_q", f_rope_q, [qf, rc, ra, rb], [], [(hw, BF16)])

    def f_mla_kv(ti, pa):
        kn, vn, kr, c_, a_, b_ = ti
        lane = _lane(kr.shape)
        k_tail = jnp.where((lane >= MLA_Q_L) & (lane < MLA_Q_L + 3), 1.0, _rope_block(kr, c_, a_, b_))
        ones_v = (lane >= V_ONES) & (lane < V_ONES + 3)
        k_out = [jnp.where(lane < MLA_NOPE, blk.astype(F32), k_tail) for blk in _blocks(kn, MLA_HEADS)]
        v_out = [jnp.where(ones_v, 1.0, blk.astype(F32)) for blk in _blocks(vn, MLA_HEADS)]
        return [jnp.concatenate(k_out, axis=1), jnp.concatenate(v_out, axis=1)], []

    k_mla, v_mla = _rowwise("mla_kv", f_mla_kv, [(kv, 0, hw), (kv, 1, hw), zwin("kr"), rc, ra, rb], [],
                            [(hw, BF16), (hw, BF16)])
    o_mla, qb_mla = _flash_fwd(q_mla, k_mla, v_mla, scale_mla, MLA_HEADS, MLA_Q_L, name="mla_fwd")

    fl_cb = zwin("fl")[1]
    fcol = _forget_cumsum(z, fl_cb)

    def f_fox_qkv(ti, pa):
        fq, fk, fv, fc = (a.astype(F32) for a in ti)
        lane = _lane(fc.shape)
        ones_q = (lane >= FOX_Q_ONES) & (lane < FOX_Q_ONES + 3)
        ones_k = (lane >= FOX_Q_F) & (lane < FOX_Q_ONES)
        ones_v = (lane >= V_ONES) & (lane < V_ONES + 3)
        q_out, k_out, v_out = [], [], []
        for hh, (qblk, kblk, vblk) in enumerate(zip(_blocks(fq, FOX_HEADS), _blocks(fk, FOX_HEADS),
                                                    _blocks(fv, FOX_HEADS), strict=True)):
            f_h = _lane_column(fc, lane, hh) * (1.0 / scale_fox)
            q_out.append(_put3(jnp.where(ones_q, 1.0, qblk), lane, FOX_Q_F, _split3(f_h)))
            k_out.append(_put3(jnp.where(ones_k, 1.0, kblk), lane, FOX_Q_ONES, _split3(-f_h)))
            v_out.append(jnp.where(ones_v, 1.0, vblk))
        return [jnp.concatenate(q_out, axis=1), jnp.concatenate(k_out, axis=1), jnp.concatenate(v_out, axis=1)], []

    q_fox, k_fox, v_fox = _rowwise("fox_qkv", f_fox_qkv, [zwin("fq"), zwin("fk"), zwin("fv"), fcol],
                                   [], [(hw, BF16)] * 3)
    o_fox, qb_fox = _flash_fwd(q_fox, k_fox, v_fox, scale_fox, FOX_HEADS, FOX_Q_L, name="fox_fwd")

    y_mla = _mm(o_mla, w_o_mla, out_dtype=BF16, name="proj_o_mla")
    y_fox = _mm(o_fox, w_o_fox, out_dtype=BF16, name="proj_o_fox")

    def f_gate(ti, pa):
        ga, gb, ya, yb = (a.astype(F32) for a in ti)
        return [_sigmoid(ga) * ya + _sigmoid(gb) * yb], []

    (merged,) = _rowwise("gate", f_gate, [zwin("ga"), zwin("gb"), y_mla, y_fox], [], [(d, BF16)])
    wts = {**wts, **late_weights("mlp", merged)}
    mix = _mm(merged, wts["w_out"], out_dtype=F32, name="proj_out")

    def f_resid1(ti, pa):
        xa, mx = ti
        y, _ = _rms(mx, pa[0])
        x1 = xa + y
        h2, _ = _rms(x1, pa[1])
        return [x1, h2], []

    x1, h2 = _rowwise("resid_mix", f_resid1, [x, mix], [wts["ln_post_mix"], wts["ln_pre_mlp"]], [(d, F32), (d, BF16)])

    def relu2(acc, ex):
        r = jnp.maximum(acc, 0.0)
        return [acc, r * r]

    u, act = _mm(h2, wts["w_ff1"], out_dtype=[BF16, BF16], name="ff1", epilogue=relu2)
    mo = _mm(act, wts["w_ff2"], out_dtype=F32, name="ff2")

    def f_loss(ti, pa):
        xa, mv, tg = ti
        y, _ = _rms(mv, pa[0])
        err = (xa + y) - tg
        g2 = err / d
        dmo, dg = _rms_bwd(mv, pa[0], g2)
        return [g2, dmo], [jnp.sum(err * err, axis=0, keepdims=True), dg]

    g2, d_mo, loss_cols, g_ln_post_mlp = _rowwise("loss", f_loss, [x1, mo, target], [wts["ln_post_mlp"]],
                                                  [(d, F32), (d, BF16)], [d, d])
    loss = 0.5 * jnp.sum(loss_cols) / d

    grads = {"ln_post_mlp": g_ln_post_mlp}
    grads["w_ff2"] = _mm_tn(act, d_mo, name="grad_ff2")

    def relu2_bwd(acc, ex):
        return [acc * (2.0 * jnp.maximum(ex[0], 0.0))]

    (d_u,) = _mm(d_mo, wts["w_ff2"], out_dtype=[BF16], name="ff2_bwd", transpose_b=True, extras=[u], epilogue=relu2_bwd)
    grads["w_ff1"] = _mm_tn(h2, d_u, name="grad_ff1")
    d_h2 = _mm(d_u, wts["w_ff1"], out_dtype=F32, name="ff1_bwd", transpose_b=True)

    def f_resid1_bwd(ti, pa):
        gres, dh2, x1v, mx = ti
        dx1n, dg_pre_mlp = _rms_bwd(x1v, pa[1], dh2)
        dx1 = gres + dx1n
        dmix, dg_post_mix = _rms_bwd(mx, pa[0], dx1)
        return [dx1, dmix], [dg_post_mix, dg_pre_mlp]

    d_x1, d_mix, grads["ln_post_mix"], grads["ln_pre_mlp"] = _rowwise(
        "resid_mix_bwd", f_resid1_bwd, [g2, d_h2, x1, mix], [wts["ln_post_mix"], wts["ln_pre_mlp"]],
        [(d, F32), (d, BF16)], [d, d])
    grads["w_out"] = _mm_tn(merged, d_mix, name="grad_out")
    d_merged = _mm(d_mix, wts["w_out"], out_dtype=BF16, name="proj_out_bwd", transpose_b=True)

    def f_gate_bwd(ti, pa):
        dm, ga, gb, ya, yb = (a.astype(F32) for a in ti)
        sa, sb = _sigmoid(ga), _sigmoid(gb)
        return [dm * sa, dm * sb, dm * ya * (sa * (1.0 - sa)), dm * yb * (sb * (1.0 - sb))], []

    d_ya, d_yb, d_ga, d_gb = _rowwise("gate_bwd", f_gate_bwd,
                                      [d_merged, zwin("ga"), zwin("gb"), y_mla, y_fox], [],
                                      [(d, BF16)] * 4)
    grads["w_o_mla"] = _unpad_head_rows(_mm_tn(o_mla, d_ya, name="grad_o_mla"), MLA_HEADS, MLA_V)
    grads["w_o_fox"] = _unpad_head_rows(_mm_tn(o_fox, d_yb, name="grad_o_fox"), FOX_HEADS, FOX_DIM)
    w_o_mla = w_o_mla + send_grads("early", {nm: grads[nm] for nm in EARLY_GRADS}).astype(BF16)
    do_mla = _with_delta(_mm(d_ya, w_o_mla, out_dtype=F32, name="proj_o_mla_bwd", transpose_b=True), o_mla,
                         MLA_HEADS, name="mla_delta")
    do_fox = _with_delta(_mm(d_yb, w_o_fox, out_dtype=F32, name="proj_o_fox_bwd", transpose_b=True), o_fox,
                         FOX_HEADS, name="fox_delta")

    dk_mla, dv_mla, dq_mla = _flash_bwd(qb_mla, k_mla, v_mla, do_mla, scale_mla, MLA_HEADS, MLA_NOPE + MLA_ROPE,
                                        name="mla_bwd")
    dk_fox, dv_fox, dq_fox = _flash_bwd(qb_fox, k_fox, v_fox, do_fox, scale_fox, FOX_HEADS, FOX_DIM, name="fox_bwd")
    d_fl = _forget_cumsum_bwd(dq_fox, dk_fox, z, fl_cb, FOX_HEADS)

    def f_rope_q_bwd(ti, pa):
        g, c_, a_, b_ = ti
        return [jnp.concatenate([_rope_block(blk, c_, -a_, -b_) for blk in _blocks(g, MLA_HEADS)], axis=1)], []

    (d_qf,) = _rowwise("rope_q_bwd", f_rope_q_bwd, [dq_mla, rc, ra, rb], [], [(hw, BF16)])
    grads["w_uq"] = _unpad_heads(_mm_tn(cqn, d_qf, name="grad_uq"), MLA_HEADS, MLA_NOPE + MLA_ROPE)
    d_cqn = _mm(d_qf, w_uq, out_dtype=F32, name="proj_uq_bwd", transpose_b=True)

    def f_mla_kv_bwd(ti, pa):
        gk, gv, c_, a_, b_ = ti
        k_blocks = _blocks(gk, MLA_HEADS)
        tot = k_blocks[0]
        for blk in k_blocks[1:]:
            tot = tot + blk
        return [jnp.concatenate([gk, gv], axis=1), _rope_block(tot, c_, -a_, -b_)], []

    d_kv, d_kr = _rowwise("mla_kv_bwd", f_mla_kv_bwd, [dk_mla, dv_mla, rc, ra, rb], [], [(2 * hw, BF16), (LANES, BF16)])
    g_ukv = _mm_tn(ckvn, d_kv, name="grad_ukv")
    grads["w_uk"] = _unpad_heads(g_ukv[:, :hw], MLA_HEADS, MLA_NOPE)
    grads["w_uv"] = _unpad_heads(g_ukv[:, hw:], MLA_HEADS, MLA_V)
    d_ckvn = _mm(d_kv, w_ukv, out_dtype=F32, name="proj_ukv_bwd", transpose_b=True)

    def f_mla_norms_bwd(ti, pa):
        cq, ckv, dcqn, dckvn = ti
        dcq, dg_q = _rms_bwd(cq, pa[0], dcqn)
        dckv, dg_kv = _rms_bwd(ckv, pa[1], dckvn)
        return [dcq, dckv], [dg_q, dg_kv]

    d_cq, d_ckv, grads["q_a_norm"], grads["kv_a_norm"] = _rowwise(
        "mla_norms_bwd", f_mla_norms_bwd, [zwin("cq"), zwin("ckv"), d_cqn, d_ckvn],
        [wts["q_a_norm"], wts["kv_a_norm"]], [(MLA_Q_LORA, BF16), (MLA_KV_LORA, BF16)], [MLA_Q_LORA, MLA_KV_LORA])

    d_z = jnp.concatenate([dq_fox.astype(BF16), dk_fox.astype(BF16), dv_fox.astype(BF16), d_ga, d_gb, d_cq, d_ckv,
                           d_kr, d_fl], axis=1)
    assert d_z.shape[1] == zl.width

    def f_bias(ti, pa):
        return [], [jnp.sum(ti[0].astype(F32), axis=0, keepdims=True)]

    (g_b_in,) = _rowwise("grad_b_in", f_bias, [d_z], [], [], [zl.width])
    grads["b_in"] = zl.from_kernel(g_b_in)
    grads["w_in"] = zl.from_kernel(_mm_tn(h, d_z, name="grad_in"))
    tok = send_grads("late", {nm: grads[nm] for nm, _ in BIG if nm not in EARLY_GRADS})
    d_h = _mm(d_z, w_in, bias=jnp.zeros((1, d), F32) + tok, out_dtype=F32, name="proj_in_bwd", transpose_b=True)

    def f_norm_in_bwd(ti, pa):
        dx1v, dh, xa = ti
        dxn, dg = _rms_bwd(xa, pa[0], dh)
        return [dx1v + dxn], [dg]

    grad_x, grads["ln_pre_mix"] = _rowwise("norm_in_bwd", f_norm_in_bwd, [d_x1, d_h, x], [wts["ln_pre_mix"]],
                                           [(d, F32)], [d])
    return loss, grad_x, grads


class _PackLayout:
    def __init__(self, shapes):
        self.shapes = list(shapes)
        self.width = _round_up(max(b for _, b in shapes), LANES)
        self.bands = []
        row = 0
        shelf = []
        for idx, (a, b) in enumerate(shapes):
            if 2 * _round_up(b, LANES) > self.width:
                self.bands.append((row, _round_up(a, 32), [(idx, 0)]))
                row += _round_up(a, 32)
            else:
                shelf.append(idx)
        col, items = 0, []
        for idx in shelf:
            wb = _round_up(shapes[idx][1], LANES)
            if col + wb > self.width:
                hgt = max(_round_up(shapes[i][0], 32) for i, _ in items)
                self.bands.append((row, hgt, items))
                row += hgt
                col, items = 0, []
            items.append((idx, col))
            col += wb
        if items:
            hgt = max(_round_up(shapes[i][0], 32) for i, _ in items)
            self.bands.append((row, hgt, items))
            row += hgt
        self.rows = _round_up(row, 16 * LOCAL_PIECES)

    def pack(self, arrs):
        lead = arrs[0].shape[:-2]
        no_pad = [(0, 0)] * len(lead)
        bands = []
        for _, hgt, items in self.bands:
            parts = []
            for k, (idx, col) in enumerate(items):
                a, b = self.shapes[idx]
                nxt = items[k + 1][1] if k + 1 < len(items) else self.width
                parts.append(jnp.pad(arrs[idx], no_pad + [(0, hgt - a), (0, nxt - col - b)]))
            bands.append(parts[0] if len(parts) == 1 else jnp.concatenate(parts, axis=-1))
        used = sum(hgt for _, hgt, _ in self.bands)
        if used < self.rows:
            bands.append(jnp.zeros(lead + (self.rows - used, self.width), arrs[0].dtype))
        return jnp.concatenate(bands, axis=-2)

    def unpack(self, packed):
        out = [None] * len(self.shapes)
        for row, _, items in self.bands:
            for idx, col in items:
                a, b = self.shapes[idx]
                out[idx] = packed[..., row:row + a, col:col + b]
        return out


def _to_shards(g, axis):
    if axis == 0:
        return g.reshape(N_CHIPS, g.shape[0] // N_CHIPS, g.shape[1])
    return jnp.stack(jnp.split(g, N_CHIPS, axis=1))


def _from_shards(s4, axis):
    n, a, b = s4.shape
    if axis == 0:
        return s4.reshape(n * a, b)
    return jnp.concatenate([s4[ch] for ch in range(n)], axis=1)


ANY = pl.BlockSpec(memory_space=pl.ANY)


def _place():
    return lax.axis_index("x"), lax.axis_index("y"), lax.axis_index("c")


def _gather_weights(wpk):
    rows, wd = wpk.shape
    half = rows // 2

    def body(w_ref, out_ref, send_sems, recv_sems, local_sems):
        x, y, c = _place()
        sibling = (x, y, 1 - c)
        chips = [(1 - x, y), (x, 1 - y), (1 - x, 1 - y)]

        def slab(chip, hf):
            return out_ref.at[2 * chip[0] + chip[1], pl.ds(hf * half, half), :]

        def copy(k, chip, hf, to, src=None):
            return pltpu.make_async_remote_copy(
                src_ref=slab(chip, hf) if src is None else src, dst_ref=slab(chip, hf),
                send_sem=send_sems.at[k], recv_sem=recv_sems.at[k], device_id=to, device_id_type=MESH)

        piece = rows // LOCAL_PIECES
        mine = [pltpu.make_async_copy(w_ref.at[pl.ds(n * piece, piece), :],
                                      out_ref.at[2 * x + y, pl.ds(n * piece, piece), :], local_sems.at[n])
                for n in range(LOCAL_PIECES)]
        for cp in mine:
            cp.start()
        first = [copy(j, (x, y), c, (*chip, c), src=w_ref.at[pl.ds(c * half, half), :]) for j, chip in enumerate(chips)]
        for cp in first:
            cp.start()
        passed = [copy(3 + j, chip, c, sibling) for j, chip in enumerate(chips)]
        for j, chip in enumerate(chips):
            copy(j, chip, c, (x, y, c)).wait_recv()
            passed[j].start()
        for j, chip in enumerate(chips):
            copy(3 + j, chip, 1 - c, (x, y, c)).wait_recv()
        for cp in first + passed:
            cp.wait_send()
        for cp in mine:
            cp.wait()

    assert rows % (16 * LOCAL_PIECES) == 0
    return pl.pallas_call(
        body, out_shape=jax.ShapeDtypeStruct((N_CHIPS, rows, wd), wpk.dtype),
        in_specs=[ANY], out_specs=ANY,
        scratch_shapes=[pltpu.SemaphoreType.DMA((6,)), pltpu.SemaphoreType.DMA((6,)),
                        pltpu.SemaphoreType.DMA((LOCAL_PIECES,))],
        name="gather_weights")(wpk)


HBM = pl.BlockSpec(memory_space=pltpu.HBM)
SEM = pl.BlockSpec(memory_space=pltpu.SEMAPHORE)
EFFECT = pltpu.SideEffectType.DATAFLOW_SIDE_EFFECTING
N_LATE = 6


def _gather_late_start(wpk, tag):
    rows, wd = wpk.shape
    half = rows // 2

    def body(w_ref, land_ref, send_sems, recv_sems, w_thru, land_thru, token):
        x, y, c = _place()
        chips = [(1 - x, y), (x, 1 - y), (1 - x, 1 - y)]
        for j, chip in enumerate(chips):
            for to_core in range(2):
                pltpu.make_async_remote_copy(
                    src_ref=w_ref.at[pl.ds(c * half, half), :],
                    dst_ref=land_ref.at[2 * x + y, pl.ds(c * half, half), :],
                    send_sem=send_sems.at[2 * j + to_core], recv_sem=recv_sems.at[2 * j + c],
                    device_id=(*chip, to_core), device_id_type=MESH).start()
        token[...] = jnp.zeros_like(token)

    land = lax.empty((N_CHIPS, rows, wd), wpk.dtype)
    return pl.pallas_call(
        body, name="gather_" + tag + "_start",
        out_shape=(pltpu.SemaphoreType.DMA((N_LATE,)), pltpu.SemaphoreType.DMA((N_LATE,)),
                   pltpu.HBM(wpk.shape, wpk.dtype), pltpu.HBM(land.shape, land.dtype),
                   jax.ShapeDtypeStruct((8, LANES), F32)),
        in_specs=(HBM, HBM), out_specs=(SEM, SEM, HBM, HBM, pl.BlockSpec(memory_space=pltpu.VMEM)),
        input_output_aliases={0: 2, 1: 3},
        compiler_params=pltpu.CompilerParams(has_side_effects=EFFECT),
    )(pltpu.with_memory_space_constraint(wpk, pltpu.HBM), pltpu.with_memory_space_constraint(land, pltpu.HBM))


def _gather_late_wait(send_sems, recv_sems, w_thru, land_thru, after, tag):
    rows, wd = w_thru.shape
    half = rows // 2

    def body(w_ref, land_ref, send_sems, recv_sems, after_ref, w_dead, land_out):
        x, y, c = _place()
        for n in range(N_LATE):
            cp = pltpu.make_async_remote_copy(
                src_ref=w_ref.at[pl.ds(0, half), :], dst_ref=land_ref.at[0, pl.ds(0, half), :],
                send_sem=send_sems.at[n], recv_sem=recv_sems.at[n], device_id=(x, y, c), device_id_type=MESH)
            cp.wait_send()
            cp.wait_recv()

    return pl.pallas_call(
        body, name="gather_" + tag + "_wait",
        out_shape=(pltpu.HBM(w_thru.shape, w_thru.dtype), pltpu.HBM(land_thru.shape, land_thru.dtype)),
        in_specs=(HBM, HBM, SEM, SEM, ANY), out_specs=(HBM, HBM), input_output_aliases={0: 0, 1: 1},
        compiler_params=pltpu.CompilerParams(has_side_effects=EFFECT),
    )(w_thru, land_thru, send_sems, recv_sems, after)[1]


N_PART = 7


def _reduce_start(gbf, tag):
    _, _, hrows, wd = gbf.shape

    def body(g_ref, land_ref, send_sems, recv_sems, g_thru, land_thru, token):
        x, y, c = _place()
        chips = [(1 - x, y), (x, 1 - y), (1 - x, 1 - y)]
        for j, chip in enumerate(chips):
            for to_core in range(2):
                pltpu.make_async_remote_copy(
                    src_ref=g_ref.at[2 * chip[0] + chip[1], to_core], dst_ref=land_ref.at[2 * j + c],
                    send_sem=send_sems.at[2 * j + to_core], recv_sem=recv_sems.at[2 * j + c],
                    device_id=(*chip, to_core), device_id_type=MESH).start()
        pltpu.make_async_remote_copy(
            src_ref=g_ref.at[2 * x + y, 1 - c], dst_ref=land_ref.at[N_PART - 1],
            send_sem=send_sems.at[N_PART - 1], recv_sem=recv_sems.at[N_PART - 1],
            device_id=(x, y, 1 - c), device_id_type=MESH).start()
        token[...] = jnp.zeros_like(token)

    land = lax.empty((N_PART, hrows, wd), gbf.dtype)
    return pl.pallas_call(
        body, name="reduce_" + tag + "_start",
        out_shape=(pltpu.SemaphoreType.DMA((N_PART,)), pltpu.SemaphoreType.DMA((N_PART,)),
                   pltpu.HBM(gbf.shape, gbf.dtype), pltpu.HBM(land.shape, land.dtype),
                   jax.ShapeDtypeStruct((8, LANES), F32)),
        in_specs=(HBM, HBM), out_specs=(SEM, SEM, HBM, HBM, pl.BlockSpec(memory_space=pltpu.VMEM)),
        input_output_aliases={0: 2, 1: 3},
        compiler_params=pltpu.CompilerParams(has_side_effects=EFFECT),
    )(pltpu.with_memory_space_constraint(gbf, pltpu.HBM), pltpu.with_memory_space_constraint(land, pltpu.HBM))


def _reduce_wait(send_sems, recv_sems, g_thru, land_thru, after, tag):
    def body(g_ref, land_ref, send_sems, recv_sems, after_ref, g_dead, land_out):
        x, y, c = _place()
        for n in range(N_PART):
            cp = pltpu.make_async_remote_copy(
                src_ref=g_ref.at[0, 0], dst_ref=land_ref.at[0], send_sem=send_sems.at[n], recv_sem=recv_sems.at[n],
                device_id=(x, y, c), device_id_type=MESH)
            cp.wait_send()
            cp.wait_recv()

    return pl.pallas_call(
        body, name="reduce_" + tag + "_wait",
        out_shape=(pltpu.HBM(g_thru.shape, g_thru.dtype), pltpu.HBM(land_thru.shape, land_thru.dtype)),
        in_specs=(HBM, HBM, SEM, SEM, ANY), out_specs=(HBM, HBM), input_output_aliases={0: 0, 1: 1},
        compiler_params=pltpu.CompilerParams(has_side_effects=EFFECT),
    )(g_thru, land_thru, send_sems, recv_sems, after)[1]


def _sibling_swap(mine):
    def body(m_ref, out_ref, send_sem, recv_sem):
        x, y, c = _place()
        cp = pltpu.make_async_remote_copy(
            src_ref=m_ref, dst_ref=out_ref, send_sem=send_sem, recv_sem=recv_sem,
            device_id=(x, y, 1 - c), device_id_type=MESH)
        cp.start()
        cp.wait()

    return pl.pallas_call(
        body, out_shape=jax.ShapeDtypeStruct(mine.shape, mine.dtype), in_specs=[ANY], out_specs=ANY,
        scratch_shapes=[pltpu.SemaphoreType.DMA, pltpu.SemaphoreType.DMA], name="grad_sibling_swap")(mine)


def _adamw(w, g, m, v):
    m = ADAM_B1 * m + (1.0 - ADAM_B1) * g
    v = ADAM_B2 * v + (1.0 - ADAM_B2) * (g * g)
    m_hat = m / (1.0 - ADAM_B1 ** ADAM_STEP)
    v_hat = v / (1.0 - ADAM_B2 ** ADAM_STEP)
    delta = -ADAM_LR * (m_hat / (jnp.sqrt(v_hat) + ADAM_EPS) + ADAM_WD * w)
    return delta, m, v


def _small_allreduce_adamw(gs, ws, ms, vs):
    n_dev = 8
    n_par = len(gs)
    wd = PACK_W
    chunks = []
    for p, g in enumerate(gs):
        for off in range(0, g.shape[1], wd):
            chunks.append((p, len(chunks), off, min(wd, g.shape[1] - off)))
    rows = _round_up(len(chunks), 8)

    def body(*refs):
        g_refs, w_refs, m_refs, v_refs = (refs[k * n_par:(k + 1) * n_par] for k in range(4))
        go_refs, d_refs, mo_refs, vo_refs = (refs[(4 + k) * n_par:(5 + k) * n_par] for k in range(4))
        mine_ref, all_ref, send_sems, recv_sems = refs[8 * n_par:]
        x, y, c = _place()
        me, sibling = (x, y, c), (x, y, 1 - c)
        chips = [(1 - x, y), (x, 1 - y), (1 - x, 1 - y)]

        def slot(px, py, pc):
            return all_ref.at[4 * px + 2 * py + pc]

        def copy(k, block, to, src=None):
            return pltpu.make_async_remote_copy(
                src_ref=slot(*block) if src is None else src, dst_ref=slot(*block),
                send_sem=send_sems.at[k], recv_sem=recv_sems.at[k], device_id=to, device_id_type=MESH)

        mine_ref[...] = jnp.zeros_like(mine_ref)
        for p, row, off, width in chunks:
            mine_ref[row:row + 1, 0:width] = g_refs[p][:, off:off + width]
        all_ref[4 * x + 2 * y + c] = mine_ref[...]
        first = [copy(0, me, sibling, src=mine_ref)]
        first += [copy(1 + j, me, (*chip, c), src=mine_ref) for j, chip in enumerate(chips)]
        for cp in first:
            cp.start()
        passed = [copy(4 + j, (*chip, c), sibling) for j, chip in enumerate(chips)]
        for j, chip in enumerate(chips):
            copy(1 + j, (*chip, c), me).wait_recv()
            passed[j].start()
        copy(0, sibling, me).wait_recv()
        for j, chip in enumerate(chips):
            copy(4 + j, (*chip, 1 - c), me).wait_recv()
        for cp in first + passed:
            cp.wait_send()
        tot = jnp.zeros((rows, wd), F32)
        for dev in range(n_dev):
            tot = tot + all_ref[dev]
        mine_ref[...] = tot
        for p, row, off, width in chunks:
            cols = slice(off, off + width)
            g = mine_ref[row:row + 1, 0:width]
            delta, m_new, v_new = _adamw(w_refs[p][:, cols], g, m_refs[p][:, cols], v_refs[p][:, cols])
            go_refs[p][:, cols] = g
            d_refs[p][:, cols] = delta
            mo_refs[p][:, cols] = m_new
            vo_refs[p][:, cols] = v_new

    vm = pl.BlockSpec(memory_space=pltpu.VMEM)
    shp = [jax.ShapeDtypeStruct(g.shape, F32) for g in gs]
    res = pl.pallas_call(
        body, out_shape=shp * 4, in_specs=[vm] * (4 * n_par), out_specs=[vm] * (4 * n_par),
        scratch_shapes=[pltpu.VMEM((rows, wd), F32), pltpu.VMEM((n_dev, rows, wd), F32),
                        pltpu.SemaphoreType.DMA((7,)), pltpu.SemaphoreType.DMA((7,))],
        name="small_allreduce_adamw")(*gs, *ws, *ms, *vs)
    return [res[k * n_par:(k + 1) * n_par] for k in range(4)]


def kernel(x, positions, ln_pre_mix, ln_post_mix, ln_pre_mlp, ln_post_mlp, w_in, b_in, q_a_norm, w_uq, kv_a_norm, w_uk, w_uv, w_o_mla, w_o_fox, w_out, w_ff1, w_ff2, loss_target, m_ln_pre_mix, m_ln_post_mix, m_ln_pre_mlp, m_ln_post_mlp, m_w_in, m_b_in, m_q_a_norm, m_w_uq, m_kv_a_norm, m_w_uk, m_w_uv, m_w_o_mla, m_w_o_fox, m_w_out, m_w_ff1, m_w_ff2, v_ln_pre_mix, v_ln_post_mix, v_ln_pre_mlp, v_ln_post_mlp, v_w_in, v_b_in, v_q_a_norm, v_w_uq, v_kv_a_norm, v_w_uk, v_w_uv, v_w_o_mla, v_w_o_fox, v_w_out, v_w_ff1, v_w_ff2):
    w = dict(ln_pre_mix=ln_pre_mix, ln_post_mix=ln_post_mix, ln_pre_mlp=ln_pre_mlp, ln_post_mlp=ln_post_mlp, w_in=w_in,
             b_in=b_in, q_a_norm=q_a_norm, w_uq=w_uq, kv_a_norm=kv_a_norm, w_uk=w_uk, w_uv=w_uv, w_o_mla=w_o_mla,
             w_o_fox=w_o_fox, w_out=w_out, w_ff1=w_ff1, w_ff2=w_ff2)
    mom = dict(ln_pre_mix=m_ln_pre_mix, ln_post_mix=m_ln_post_mix, ln_pre_mlp=m_ln_pre_mlp, ln_post_mlp=m_ln_post_mlp,
               w_in=m_w_in, b_in=m_b_in, q_a_norm=m_q_a_norm, w_uq=m_w_uq, kv_a_norm=m_kv_a_norm, w_uk=m_w_uk,
               w_uv=m_w_uv, w_o_mla=m_w_o_mla, w_o_fox=m_w_o_fox, w_out=m_w_out, w_ff1=m_w_ff1, w_ff2=m_w_ff2)
    var = dict(ln_pre_mix=v_ln_pre_mix, ln_post_mix=v_ln_post_mix, ln_pre_mlp=v_ln_pre_mlp, ln_post_mlp=v_ln_post_mlp,
               w_in=v_w_in, b_in=v_b_in, q_a_norm=v_q_a_norm, w_uq=v_w_uq, kv_a_norm=v_kv_a_norm, w_uk=v_w_uk,
               w_uv=v_w_uv, w_o_mla=v_w_o_mla, w_o_fox=v_w_o_fox, w_out=v_w_out, w_ff1=v_w_ff1, w_ff2=v_w_ff2)

    big_names = [nm for nm, _ in BIG]
    c = lax.axis_index("c")
    chip = 2 * lax.axis_index("x") + lax.axis_index("y")

    axes = dict(BIG)

    def assemble(names, lay, gathered):
        return {nm: _from_shards(s4, axes[nm] - 1) for nm, s4 in zip(names, lay.unpack(gathered), strict=True)}

    groups = {"mixer": [nm for nm in big_names if nm in MIXER], "mlp": [nm for nm in big_names if nm in LATE]}
    first_names = [nm for nm in big_names if nm not in MIXER and nm not in LATE]
    full = {nm: wv for nm, wv in w.items() if nm in SMALL}
    travelling = {}
    for tag, names in groups.items():
        lay = _PackLayout([w[nm].shape[1:] for nm in names])
        shard = lay.pack([w[nm][0].astype(BF16) for nm in names])
        started = _gather_late_start(shard, tag)
        travelling[tag] = (names, lay, shard, started[:4])
        full["b_in"] = full["b_in"] + started[4][0, 0]
    lay_first = _PackLayout([w[nm].shape[1:] for nm in first_names])
    full.update(assemble(first_names, lay_first,
                         _gather_weights(lay_first.pack([w[nm][0].astype(BF16) for nm in first_names]))))

    def late_weights(tag, after):
        names, lay, shard, handles = travelling[tag]
        land = _gather_late_wait(*handles, after, tag)
        land = lax.dynamic_update_slice(land, shard[None], (chip, 0, 0))
        return assemble(names, lay, land)

    grad_groups = {"early": [nm for nm in big_names if nm in EARLY_GRADS],
                   "late": [nm for nm in big_names if nm not in EARLY_GRADS]}
    sent = {}

    def send_grads(tag, g):
        names = grad_groups[tag]
        lay = _PackLayout([w[nm].shape[1:] for nm in names])
        hrows = lay.rows // 2
        gbf = lay.pack([_to_shards(g[nm], axes[nm] - 1).astype(BF16) for nm in names])
        gbf = gbf.reshape(N_CHIPS, 2, hrows, lay.width)
        started = _reduce_start(gbf, tag)
        own = lax.dynamic_index_in_dim(lax.dynamic_index_in_dim(started[2], chip, axis=0, keepdims=False), c, axis=0,
                                       keepdims=False)
        sent[tag] = (names, lay, started[:4], own)
        return started[4][0, 0]

    loss_local, grad_x, grads = _local_step(x[0], positions[0], loss_target[0], full, late_weights, send_grads)

    def f_add8(ti, pa):
        tot = ti[0].astype(F32)
        for part in ti[1:]:
            tot = tot + part.astype(F32)
        return [tot], []

    reduced = []
    for tag, (names, lay, handles, own) in sent.items():
        parts = _reduce_wait(*handles, grad_x, tag)
        reduced.append(_rowwise("grad_add_" + tag, f_add8, [own] + [parts[n] for n in range(N_PART)], [],
                                [(lay.width, F32)])[0])

    assert len({lay.width for _, lay, _, _ in sent.values()}) == 1
    red = jnp.concatenate(reduced, axis=0)
    sib = _sibling_swap(red)
    lower, upper = jnp.where(c == 0, red, sib), jnp.where(c == 0, sib, red)
    g_by_name, row = {}, 0
    for names, lay, _, _ in sent.values():
        hrows = lay.rows // 2
        both = jnp.concatenate([lower[row:row + hrows], upper[row:row + hrows]], axis=0)
        g_by_name.update(zip(names, lay.unpack(both), strict=True))
        row += hrows
    g_shards = [g_by_name[nm] for nm in big_names]

    def f_adamw(ti, pa):
        wv, gv, mv, vv = ti
        return list(_adamw(wv, gv, mv, vv)), []

    out = {"grad": {}, "delta": {}, "m": {}, "v": {}}
    for nm, g_sh in zip(big_names, g_shards, strict=True):
        wd = g_sh.shape[1]
        d_sh, m_sh, v_sh = _rowwise("adamw_" + nm, f_adamw, [w[nm][0], g_sh, mom[nm][0], var[nm][0]], [], [(wd, F32)] * 3)
        out["grad"][nm], out["delta"][nm], out["m"][nm], out["v"][nm] = g_sh[None], d_sh[None], m_sh[None], v_sh[None]

    loss_row = jnp.zeros((1, LANES), F32) + loss_local
    blank = jnp.zeros((1, LANES), F32)
    small = _small_allreduce_adamw([grads[nm] for nm in SMALL] + [loss_row], [w[nm] for nm in SMALL] + [blank],
                                   [mom[nm] for nm in SMALL] + [blank], [var[nm] for nm in SMALL] + [blank])
    for kind, arrs in zip(("grad", "delta", "m", "v"), small, strict=True):
        for nm, arr in zip(SMALL, arrs[:len(SMALL)], strict=True):
            out[kind][nm] = arr
    loss = small[0][len(SMALL)][0, 0]

    return (loss, grad_x[None], *[out["grad"][nm] for nm in ALL_W], *[out["delta"][nm] for nm in ALL_W],
            *[out["m"][nm] for nm in ALL_W], *[out["v"][nm] for nm in ALL_W])
```

```python
import functools
import math

import jax
import jax.numpy as jnp
from jax import lax
from jax.experimental import pallas as pl
from jax.experimental.pallas import tpu as pltpu

F32 = jnp.float32
BF16 = jnp.bfloat16

MLA_HEADS = 8
MLA_Q_LORA = 256
MLA_KV_LORA = 128
MLA_NOPE = 64
MLA_ROPE = 32
MLA_V = 64
FOX_HEADS = 8
FOX_DIM = 64
ROPE_THETA = 10000.0
NORM_EPS = 1e-6
HALF_ROPE = MLA_ROPE // 2

ADAM_LR = 0.001
ADAM_B1 = 0.9
ADAM_B2 = 0.999
ADAM_EPS = 1e-08
ADAM_WD = 0.01
ADAM_STEP = 10

LANES = 128
VMEM_LIMIT = 56 * 1024 * 1024
ATT_TILE = 1024
FWD_GROUP_LOG2 = 1
FWD_GROUP = 1 << FWD_GROUP_LOG2
MM_VMEM_BUDGET = 40 * 1024 * 1024
MXU_WIDTH = 256
MXU_MACS_PER_S = 4.98e14
HBM_BYTES_PER_S = 3.2e12
STEP_OVERHEAD_S = 0.35e-6
NEG = -1e30
LOG2E = math.log2(math.e)
MESH = pl.DeviceIdType.MESH

V_ONES = 64
FOX_Q_F = 64
FOX_Q_L = 67
FOX_Q_ONES = 70
MLA_Q_L = 96

BIG = (("w_in", 2), ("w_uq", 2), ("w_uk", 2), ("w_uv", 2), ("w_o_mla", 2), ("w_o_fox", 2),
       ("w_out", 1), ("w_ff1", 2), ("w_ff2", 1))
MIXER = ("w_uq", "w_uk", "w_uv", "w_o_mla", "w_o_fox")
LATE = ("w_out", "w_ff1", "w_ff2")
EARLY_GRADS = ("w_o_mla", "w_o_fox", "w_out", "w_ff1", "w_ff2")
SMALL = ("ln_pre_mix", "ln_post_mix", "ln_pre_mlp", "ln_post_mlp", "b_in", "q_a_norm", "kv_a_norm")
ALL_W = ("ln_pre_mix", "ln_post_mix", "ln_pre_mlp", "ln_post_mlp", "w_in", "b_in", "q_a_norm", "w_uq",
         "kv_a_norm", "w_uk", "w_uv", "w_o_mla", "w_o_fox", "w_out", "w_ff1", "w_ff2")
N_CHIPS = 4
PACK_W = 1024
LOCAL_PIECES = 8

_NT = (((1,), (1,)), ((), ()))
_TN = (((0,), (0,)), ((), ()))


def _cparams(sem=None):
    return pltpu.CompilerParams(dimension_semantics=sem, vmem_limit_bytes=VMEM_LIMIT)


def _divisor_tile(n, limit, mult):
    if n <= limit:
        return n
    best = None
    t = mult
    while t <= limit:
        if n % t == 0:
            best = t
        t += mult
    assert best is not None, (n, limit, mult)
    return best


def _round_up(v, mult):
    return -(-v // mult) * mult


def _mm_tiles(m, k, n, io_bytes):
    best = None
    for tm in (2048, 1024, 512, 256, 128):
        if m % tm:
            continue
        for tn in range(LANES, min(n, 2048) + 1, LANES):
            if n % tn:
                continue
            vmem = 2 * (tm * k * 2 + k * tn * 2 + tm * tn * io_bytes) + tm * tn * 4
            if vmem > MM_VMEM_BUDGET:
                continue
            mxu = m * k * n * (_round_up(tn, MXU_WIDTH) / tn) / MXU_MACS_PER_S
            hbm = (m * k * 2 + (m // tm) * k * n * 2 + m * n * io_bytes) / HBM_BYTES_PER_S
            cost = max(mxu, hbm) + (m // tm) * (n // tn) * STEP_OVERHEAD_S
            if best is None or cost < best[0]:
                best = (cost, tm, tn)
    assert best is not None, (m, k, n)
    return best[1], best[2]


def _mm(a, b, *, out_dtype, name, bias=None, transpose_b=False, extras=(), epilogue=None):
    m, k = a.shape
    n = b.shape[0] if transpose_b else b.shape[1]
    assert (b.shape[1] if transpose_b else b.shape[0]) == k and a.dtype == BF16 and b.dtype == BF16
    out_dtypes = list(out_dtype) if isinstance(out_dtype, (list, tuple)) else [out_dtype]
    n_ex = len(extras)
    tm, tn = _mm_tiles(m, k, n, sum(jnp.dtype(dt).itemsize for dt in out_dtypes) + 4 * n_ex)

    def body(*refs):
        a_ref, b_ref = refs[:2]
        pos = 2
        bias_ref = None
        if bias is not None:
            bias_ref = refs[pos]
            pos += 1
        ex_refs = refs[pos:pos + n_ex]
        o_refs = refs[pos + n_ex:]
        if transpose_b:
            acc = lax.dot_general(a_ref[...], b_ref[...], _NT, preferred_element_type=F32)
        else:
            acc = jnp.dot(a_ref[...], b_ref[...], preferred_element_type=F32)
        if bias_ref is not None:
            acc = acc + bias_ref[...]
        vals = [acc] if epilogue is None else epilogue(acc, [r[...] for r in ex_refs])
        for ref, val in zip(o_refs, vals, strict=True):
            ref[...] = val.astype(ref.dtype)

    b_spec = pl.BlockSpec((tn, k), lambda i, j: (j, 0)) if transpose_b else pl.BlockSpec((k, tn), lambda i, j: (0, j))
    in_specs = [pl.BlockSpec((tm, k), lambda i, j: (i, 0)), b_spec]
    args = [a, b]
    if bias is not None:
        in_specs.append(pl.BlockSpec((1, tn), lambda i, j: (0, j)))
        args.append(bias)
    in_specs += [pl.BlockSpec((tm, tn), lambda i, j: (i, j)) for _ in extras]
    args += list(extras)
    res = pl.pallas_call(
        body, grid=(m // tm, n // tn), in_specs=in_specs,
        out_specs=[pl.BlockSpec((tm, tn), lambda i, j: (i, j)) for _ in out_dtypes],
        out_shape=[jax.ShapeDtypeStruct((m, n), dt) for dt in out_dtypes],
        compiler_params=_cparams(("parallel", "parallel")), name=name)(*args)
    return res if isinstance(out_dtype, (list, tuple)) else res[0]


def _mm_tn(a, b, *, name):
    s, m = a.shape
    s2, n = b.shape
    assert s == s2 and a.dtype == BF16 and b.dtype == BF16
    tm = _divisor_tile(m, 1024, LANES)
    tn = _divisor_tile(n, 2304, LANES)
    tk = _divisor_tile(s, 512, 16)

    def body(a_ref, b_ref, o_ref):
        @pl.when(pl.program_id(2) == 0)
        def _():
            o_ref[...] = jnp.zeros_like(o_ref)

        o_ref[...] += lax.dot_general(a_ref[...], b_ref[...], _TN, preferred_element_type=F32)

    return pl.pallas_call(
        body, grid=(m // tm, n // tn, s // tk),
        in_specs=[pl.BlockSpec((tk, tm), lambda i, j, k: (k, i)), pl.BlockSpec((tk, tn), lambda i, j, k: (k, j))],
        out_specs=pl.BlockSpec((tm, tn), lambda i, j, k: (i, j)),
        out_shape=jax.ShapeDtypeStruct((m, n), F32),
        compiler_params=_cparams(("parallel", "parallel", "arbitrary")), name=name)(a, b)


def _rowwise(name, fn, tiled, params, outs, reds=(), reverse=False):
    wins = [t if isinstance(t, tuple) else (t, 0, t.shape[1]) for t in tiled]
    s = wins[0][0].shape[0]
    row_bytes = sum(w * arr.dtype.itemsize for arr, _, w in wins) + sum(w * jnp.dtype(d).itemsize for w, d in outs)
    ts = _divisor_tile(s, max(16, min(1024, (6 * 1024 * 1024) // row_bytes)), 16)
    nt, npar, nout = len(wins), len(params), len(outs)
    n_tiles = s // ts

    def row(i):
        return n_tiles - 1 - i if reverse else i

    def body(*refs):
        tin = [r[...] for r in refs[:nt]]
        par = [r[...] for r in refs[nt:nt + npar]]
        out_refs = refs[nt + npar:nt + npar + nout]
        red_refs = refs[nt + npar + nout:]
        o, r = fn(tin, par)
        for ref, val in zip(out_refs, o, strict=True):
            ref[...] = val.astype(ref.dtype)
        if red_refs:
            @pl.when(pl.program_id(0) == 0)
            def _():
                for ref in red_refs:
                    ref[...] = jnp.zeros_like(ref)

            for ref, val in zip(red_refs, r, strict=True):
                ref[...] += val

    in_specs = [pl.BlockSpec((ts, w), functools.partial(lambda i, cb: (row(i), cb), cb=cb)) for _, cb, w in wins]
    in_specs += [pl.BlockSpec(p.shape, lambda i: (0, 0)) for p in params]
    out_specs = [pl.BlockSpec((ts, w), lambda i: (row(i), 0)) for w, _ in outs]
    out_specs += [pl.BlockSpec((1, w), lambda i: (0, 0)) for w in reds]
    out_shape = [jax.ShapeDtypeStruct((s, w), d) for w, d in outs]
    out_shape += [jax.ShapeDtypeStruct((1, w), F32) for w in reds]
    return pl.pallas_call(
        body, grid=(n_tiles,), in_specs=in_specs, out_specs=out_specs, out_shape=out_shape,
        compiler_params=_cparams(("arbitrary",)), name=name)(*[w[0] for w in wins], *params)


def _rms(x, g):
    r = lax.rsqrt(jnp.mean(x * x, axis=-1, keepdims=True) + NORM_EPS)
    return x * r * g, r


def _rms_bwd(x, g, dy):
    r = lax.rsqrt(jnp.mean(x * x, axis=-1, keepdims=True) + NORM_EPS)
    gy = dy * g
    dx = r * gy - x * (r * r * r) * jnp.mean(x * gy, axis=-1, keepdims=True)
    dg = jnp.sum(dy * (x * r), axis=0, keepdims=True)
    return dx, dg


def _sigmoid(x):
    return 1.0 / (1.0 + jnp.exp(-x))


def _split3(x):
    hi = x.astype(BF16).astype(F32)
    r = x - hi
    mid = r.astype(BF16).astype(F32)
    lo = (r - mid).astype(BF16).astype(F32)
    return hi, mid, lo


def _lane(shape):
    return lax.broadcasted_iota(jnp.int32, shape, 1)


def _put3(blk, lane, pos, pieces):
    for k, piece in enumerate(pieces):
        blk = jnp.where(lane == pos + k, piece, blk)
    return blk


def _lane_column(blk, lane, pos):
    return jnp.sum(jnp.where(lane == pos, blk, 0.0), axis=1, keepdims=True)


def _blocks(a, nh):
    return [a[:, h * LANES:(h + 1) * LANES] for h in range(nh)]


def _head_spread(nh, d):
    shift = d.bit_length() - 1
    assert 1 << shift == d
    r = lax.broadcasted_iota(jnp.int32, (nh * d, nh * LANES), 0)
    c = lax.broadcasted_iota(jnp.int32, (nh * d, nh * LANES), 1)
    return jnp.where(c == (r >> shift) * LANES + (r & (d - 1)), 1.0, 0.0).astype(BF16)


def _rope_block(x, c, sa, sb):
    return x * c + pltpu.roll(x, LANES - HALF_ROPE, 1) * sa + pltpu.roll(x, HALF_ROPE, 1) * sb


def _forget_cumsum(z, cb):
    s = z.shape[0]
    ts = _divisor_tile(s, 512, LANES)

    def body(x_ref, col_ref, carry):
        @pl.when(pl.program_id(0) == 0)
        def _():
            carry[...] = jnp.zeros_like(carry)

        x = x_ref[...]
        lf = jnp.minimum(x, 0.0) - jnp.log1p(jnp.exp(-jnp.abs(x)))
        r = lax.broadcasted_iota(jnp.int32, (ts, ts), 0)
        c = lax.broadcasted_iota(jnp.int32, (ts, ts), 1)
        tri = jnp.where(c <= r, 1.0, 0.0).astype(F32)
        col_ref[...] = jnp.dot(tri, lf, preferred_element_type=F32, precision=lax.Precision.HIGHEST) + carry[...]
        carry[...] += jnp.sum(lf, axis=0, keepdims=True)

    return pl.pallas_call(
        body, grid=(s // ts,),
        in_specs=[pl.BlockSpec((ts, LANES), lambda i: (i, cb))],
        out_specs=pl.BlockSpec((ts, LANES), lambda i: (i, 0)),
        out_shape=jax.ShapeDtypeStruct((s, LANES), F32),
        scratch_shapes=[pltpu.VMEM((1, LANES), F32)],
        compiler_params=_cparams(("arbitrary",)), name="forget_cumsum")(z)


def _forget_cumsum_bwd(dq, dk, z, cb, nh):
    s = z.shape[0]
    ts = _divisor_tile(s, 512, LANES)
    nt = s // ts
    wd = nh * LANES

    def body(dq_ref, dk_ref, x_ref, o_ref, carry):
        @pl.when(pl.program_id(0) == 0)
        def _():
            carry[...] = jnp.zeros_like(carry)

        lane = _lane((ts, LANES))
        df = jnp.zeros((ts, LANES), F32)
        for h in range(nh):
            cols = slice(h * LANES, (h + 1) * LANES)
            d_h = _lane_column(dq_ref[:, cols], lane, FOX_Q_F) - _lane_column(dk_ref[:, cols], lane, FOX_Q_ONES)
            df = jnp.where(lane == h, d_h, df)
        r = lax.broadcasted_iota(jnp.int32, (ts, ts), 0)
        c = lax.broadcasted_iota(jnp.int32, (ts, ts), 1)
        tri = jnp.where(c >= r, 1.0, 0.0).astype(F32)
        rc = jnp.dot(tri, df, preferred_element_type=F32, precision=lax.Precision.HIGHEST) + carry[...]
        carry[...] += jnp.sum(df, axis=0, keepdims=True)
        o_ref[...] = (rc * (1.0 / (1.0 + jnp.exp(x_ref[...])))).astype(o_ref.dtype)

    return pl.pallas_call(
        body, grid=(nt,),
        in_specs=[pl.BlockSpec((ts, wd), lambda i: (nt - 1 - i, 0)),
                  pl.BlockSpec((ts, wd), lambda i: (nt - 1 - i, 0)),
                  pl.BlockSpec((ts, LANES), lambda i: (nt - 1 - i, cb))],
        out_specs=pl.BlockSpec((ts, LANES), lambda i: (nt - 1 - i, 0)),
        out_shape=jax.ShapeDtypeStruct((s, LANES), BF16),
        scratch_shapes=[pltpu.VMEM((1, LANES), F32)],
        compiler_params=_cparams(("arbitrary",)), name="forget_cumsum_bwd")(dq, dk, z)


def _flash_fwd(q, k, v, scale, nh, l_lane, *, name):
    s = q.shape[0]
    t = min(ATT_TILE, s)
    half = t // 2 if t % (2 * LANES) == 0 else t
    nq = s // t
    c = scale * LOG2E

    def body(q_ref, k_ref, v_ref, o_ref, qb_ref):
        i = pl.program_id(1)
        qb = q_ref[...]

        def scores(q_rows, k0, nk):
            kb = k_ref[pl.ds(pl.multiple_of(k0, half), nk), :]
            return lax.dot_general(q_rows, kb, _NT, preferred_element_type=F32)

        def update(sc, k0, nk, carry):
            m, acc = carry
            m_new = jnp.maximum(m, jnp.max(sc, axis=1, keepdims=True))
            p = jnp.exp2((sc - m_new) * c)
            alpha = jnp.exp2((m - m_new) * c)
            vb = v_ref[pl.ds(pl.multiple_of(k0, half), nk), :]
            acc = alpha * acc + jnp.dot(p.astype(BF16), vb, preferred_element_type=F32)
            return m_new, acc

        def full_block(j, cr):
            return update(scores(qb, j * t, t), j * t, t, cr)

        def group(jj, cr):
            for n in range(FWD_GROUP):
                cr = full_block(FWD_GROUP * jj + n, cr)
            return cr

        def causal(sc):
            row = lax.broadcasted_iota(jnp.int32, sc.shape, 0)
            col = lax.broadcasted_iota(jnp.int32, sc.shape, 1)
            return jnp.where(col <= row, sc, NEG)

        init = (jnp.full((t, 1), NEG, F32), jnp.zeros((t, LANES), F32))
        n_groups = i >> FWD_GROUP_LOG2
        carry = lax.fori_loop(0, n_groups, group, init)
        carry = lax.fori_loop(n_groups * FWD_GROUP, i, full_block, carry)
        m, acc = update(causal(scores(qb, i * t, t)), i * t, t, carry)
        lane = _lane((t, LANES))
        l = _lane_column(acc, lane, V_ONES)
        o_ref[...] = (acc / l).astype(o_ref.dtype)
        big_l = m + jnp.log(l) / scale
        qb_ref[...] = _put3(qb.astype(F32), lane, l_lane, _split3(-big_l)).astype(qb_ref.dtype)

    head_rows = pl.BlockSpec((t, LANES), lambda h, i: (i, h))
    head_all = pl.BlockSpec((s, LANES), lambda h, i: (0, h))
    return pl.pallas_call(
        body, grid=(nh, nq), in_specs=[head_rows, head_all, head_all], out_specs=[head_rows, head_rows],
        out_shape=[jax.ShapeDtypeStruct(q.shape, BF16), jax.ShapeDtypeStruct(q.shape, BF16)],
        compiler_params=_cparams(("parallel", "arbitrary")), name=name)(q, k, v)


def _flash_bwd(qb, k, v, do, scale, nh, n_feat, *, name):
    s = qb.shape[0]
    t = min(ATT_TILE, s)
    half = t // 2 if t % (2 * LANES) == 0 else t
    nq = s // t
    c = scale * LOG2E

    def body(q_ref, k_ref, v_ref, do_ref, dk_ref, dv_ref, dq_ref):
        j = pl.program_id(1)
        kb = k_ref[...]
        vb = v_ref[...]

        @pl.when(j == 0)
        def _():
            dq_ref[...] = jnp.zeros_like(dq_ref)

        def part(q0, n_q, n_k, carry, q_off):
            dk_acc, dv_acc = carry
            rows = pl.ds(pl.multiple_of(q0, half), n_q)
            qblk = q_ref[rows, :]
            dob = do_ref[rows, :]
            kbb, vbb = kb[:n_k], vb[:n_k]
            st = lax.dot_general(kbb, qblk, _NT, preferred_element_type=F32)
            if q_off is not None:
                key = lax.broadcasted_iota(jnp.int32, st.shape, 0)
                qry = lax.broadcasted_iota(jnp.int32, st.shape, 1) + q_off
                st = jnp.where(key <= qry, st, NEG)
            pt = jnp.exp2(st * c)
            dv_new = jnp.dot(pt.astype(BF16), dob, preferred_element_type=F32)
            dpt = lax.dot_general(vbb, dob, _NT, preferred_element_type=F32)
            dsb = (pt * dpt).astype(BF16)
            dk_new = jnp.dot(dsb, qblk, preferred_element_type=F32)
            dq_ref[rows, :] += lax.dot_general(dsb, kbb, _TN, preferred_element_type=F32)
            if n_k == t:
                return dk_acc + dk_new, dv_acc + dv_new
            return (jnp.concatenate([dk_acc[:n_k] + dk_new, dk_acc[n_k:]], axis=0),
                    jnp.concatenate([dv_acc[:n_k] + dv_new, dv_acc[n_k:]], axis=0))

        def block(i, carry):
            return part(i * t, t, t, carry, None)

        init = (jnp.zeros((t, LANES), F32), jnp.zeros((t, LANES), F32))
        carry = part(j * t + half, t - half, t, init, half) if half < t else part(j * t, t, t, init, 0)
        if half < t:
            carry = part(j * t, half, half, carry, 0)
        rest = nq - 1 - j
        carry = lax.cond((rest & 1) == 1, lambda cr: block(j + 1, cr), lambda cr: cr, carry)
        first = j + 1 + (rest & 1)

        def pair(ii, cr):
            i0 = first + 2 * ii
            return block(i0 + 1, block(i0, cr))

        dk_acc, dv_acc = lax.fori_loop(0, rest >> 1, pair, carry)
        dk_ref[...] = dk_acc * jnp.where(_lane((t, LANES)) < n_feat, scale, 1.0)
        dv_ref[...] = dv_acc

        @pl.when(j == nq - 1)
        def _():
            dq_ref[...] = dq_ref[...] * jnp.where(_lane((s, LANES)) < n_feat, scale, 1.0)

    head_rows = pl.BlockSpec((t, LANES), lambda h, j: (j, h))
    head_all = pl.BlockSpec((s, LANES), lambda h, j: (0, h))
    shp = jax.ShapeDtypeStruct(qb.shape, F32)
    return pl.pallas_call(
        body, grid=(nh, nq), in_specs=[head_all, head_rows, head_rows, head_all],
        out_specs=[head_rows, head_rows, head_all], out_shape=[shp, shp, shp],
        compiler_params=_cparams(("parallel", "arbitrary")), name=name)(qb, k, v, do)


def _with_delta(do, o, nh, *, name):
    def fn(ti, pa):
        lane = _lane((ti[0].shape[0], LANES))
        out = []
        for d_blk, o_blk in zip(_blocks(ti[0], nh), _blocks(ti[1], nh), strict=True):
            delta = jnp.sum(d_blk * o_blk.astype(F32), axis=1, keepdims=True)
            out.append(_put3(d_blk, lane, V_ONES, _split3(-delta)))
        return [jnp.concatenate(out, axis=1)], []

    (res,) = _rowwise(name, fn, [do, o], [], [(do.shape[1], BF16)])
    return res


def _pad_heads(a, nh):
    d = a.shape[-1] // nh
    a = a.reshape(a.shape[:-1] + (nh, d))
    a = jnp.pad(a, [(0, 0)] * (a.ndim - 1) + [(0, LANES - d)])
    return a.reshape(a.shape[:-2] + (nh * LANES,))


def _unpad_heads(a, nh, d):
    a = a.reshape(a.shape[:-1] + (nh, LANES))[..., :d]
    return a.reshape(a.shape[:-2] + (nh * d,))


def _pad_head_rows(w, nh):
    return _pad_heads(w.T, nh).T


def _unpad_head_rows(g, nh, d):
    return _unpad_heads(g.T, nh, d).T


class _ZLayout:
    def __init__(self, d):
        fw = FOX_HEADS * FOX_DIM
        self.src = {}
        off = 0
        for nm, w in (("cq", MLA_Q_LORA), ("ckv", MLA_KV_LORA), ("kr", MLA_ROPE), ("fq", fw), ("fk", fw),
                      ("fv", fw), ("fl", FOX_HEADS), ("ga", d), ("gb", d)):
            self.src[nm] = (off, w)
            off += w
        self.dst = {}
        off = 0
        for nm, w in (("ga", d), ("gb", d), ("fq", fw), ("fk", fw), ("fv", fw), ("cq", MLA_Q_LORA),
                      ("ckv", MLA_KV_LORA), ("kr", LANES), ("fl", LANES)):
            assert off % w == 0
            self.dst[nm] = (off, w)
            off += w
        self.width = off
        self.split = self.dst["cq"][0]
        assert all((o - self.split) % w == 0 for o, w in self.dst.values() if o >= self.split)

    def to_kernel(self, w):
        def seg(nm):
            off, wd = self.src[nm]
            return w[..., off:off + wd]

        def pad(a, left, total):
            return jnp.pad(a, [(0, 0)] * (a.ndim - 1) + [(left, total - left - a.shape[-1])])

        return jnp.concatenate([seg("ga"), seg("gb"), seg("fq"), seg("fk"), seg("fv"), seg("cq"), seg("ckv"),
                                pad(seg("kr"), MLA_NOPE, LANES), pad(seg("fl"), 0, LANES)], axis=-1)

    def from_kernel(self, g):
        def seg(nm, lo=0, hi=None):
            off, wd = self.dst[nm]
            return g[..., off + lo:off + (wd if hi is None else hi)]

        return jnp.concatenate([seg("cq"), seg("ckv"), seg("kr", MLA_NOPE, MLA_NOPE + MLA_ROPE), seg("fq"), seg("fk"),
                                seg("fv"), seg("fl", 0, FOX_HEADS), seg("ga"), seg("gb")], axis=-1)


def _local_step(x, positions, target, wts, late_weights, send_grads):
    s, d = x.shape
    zl = _ZLayout(d)
    hw = MLA_HEADS * LANES
    assert MLA_HEADS == FOX_HEADS
    scale_mla = (MLA_NOPE + MLA_ROPE) ** -0.5
    scale_fox = FOX_DIM ** -0.5

    inv_freq = ROPE_THETA ** (-jnp.arange(HALF_ROPE, dtype=F32) / HALF_ROPE)
    ang = positions.astype(F32)[:, None] * inv_freq
    cos, sin = jnp.cos(ang), jnp.sin(ang)
    tail = jnp.zeros((s, LANES - MLA_NOPE - MLA_ROPE), F32)
    rc = jnp.concatenate([jnp.ones((s, MLA_NOPE), F32), cos, cos, tail], axis=1)
    ra = jnp.concatenate([jnp.zeros((s, MLA_NOPE), F32), -sin, jnp.zeros((s, HALF_ROPE), F32), tail], axis=1)
    rb = jnp.concatenate([jnp.zeros((s, MLA_NOPE + HALF_ROPE), F32), sin, tail], axis=1)

    w_in = zl.to_kernel(wts["w_in"])
    b_in = zl.to_kernel(wts["b_in"])

    def f_norm_in(ti, pa):
        y, _ = _rms(ti[0], pa[0])
        return [y], []

    (h,) = _rowwise("norm_in", f_norm_in, [x], [wts["ln_pre_mix"]], [(d, BF16)])
    z_lo = _mm(h, w_in[:, :zl.split], bias=b_in[:, :zl.split], out_dtype=BF16, name="proj_in_lo")
    z = _mm(h, w_in[:, zl.split:], bias=b_in[:, zl.split:], out_dtype=F32, name="proj_in_hi")

    def zwin(nm):
        off, wd = zl.dst[nm]
        return (z_lo, off // wd, wd) if off < zl.split else (z, (off - zl.split) // wd, wd)

    wts = {**wts, **late_weights("mixer", z)}
    w_uq = _pad_heads(wts["w_uq"], MLA_HEADS)
    w_ukv = jnp.concatenate([_pad_heads(wts["w_uk"], MLA_HEADS), _pad_heads(wts["w_uv"], MLA_HEADS)], axis=1)
    w_o_mla = _pad_head_rows(wts["w_o_mla"], MLA_HEADS)
    w_o_fox = _pad_head_rows(wts["w_o_fox"], FOX_HEADS)

    def f_mla_norms(ti, pa):
        cqn, _ = _rms(ti[0], pa[0])
        ckvn, _ = _rms(ti[1], pa[1])
        return [cqn, ckvn], []

    cqn, ckvn = _rowwise("mla_norms", f_mla_norms, [zwin("cq"), zwin("ckv")],
                         [wts["q_a_norm"], wts["kv_a_norm"]], [(MLA_Q_LORA, BF16), (MLA_KV_LORA, BF16)])
    qf = _mm(cqn, w_uq, out_dtype=F32, name="proj_uq")
    kv = _mm(ckvn, w_ukv, out_dtype=BF16, name="proj_ukv")

    def f_rope_q(ti, pa):
        xq, c_, a_, b_ = ti
        return [jnp.concatenate([_rope_block(blk, c_, a_, b_) for blk in _blocks(xq, MLA_HEADS)], axis=1)], []

    (q_mla,) = _rowwise("rope_q", f_rope_q, [qf, rc, ra, rb], [], [(hw, BF16)])

    def f_mla_kv(ti, pa):
        kn, vn, kr, c_, a_, b_ = ti
        lane = _lane(kr.shape)
        k_tail = jnp.where((lane >= MLA_Q_L) & (lane < MLA_Q_L + 3), 1.0, _rope_block(kr, c_, a_, b_))
        ones_v = (lane >= V_ONES) & (lane < V_ONES + 3)
        k_out = [jnp.where(lane < MLA_NOPE, blk.astype(F32), k_tail) for blk in _blocks(kn, MLA_HEADS)]
        v_out = [jnp.where(ones_v, 1.0, blk.astype(F32)) for blk in _blocks(vn, MLA_HEADS)]
        return [jnp.concatenate(k_out, axis=1), jnp.concatenate(v_out, axis=1)], []

    k_mla, v_mla = _rowwise("mla_kv", f_mla_kv, [(kv, 0, hw), (kv, 1, hw), zwin("kr"), rc, ra, rb], [],
                            [(hw, BF16), (hw, BF16)])
    o_mla, qb_mla = _flash_fwd(q_mla, k_mla, v_mla, scale_mla, MLA_HEADS, MLA_Q_L, name="mla_fwd")

    fl_cb = zwin("fl")[1]
    fcol = _forget_cumsum(z, fl_cb)

    def f_fox_qkv(ti, pa):
        spread = _head_spread(FOX_HEADS, FOX_DIM)
        fq, fk, fv = (jnp.dot(a, spread, preferred_element_type=F32) for a in ti[:3])
        fc = ti[3]
        lane = _lane(fc.shape)
        ones_q = (lane >= FOX_Q_ONES) & (lane < FOX_Q_ONES + 3)
        ones_k = (lane >= FOX_Q_F) & (lane < FOX_Q_ONES)
        ones_v = (lane >= V_ONES) & (lane < V_ONES + 3)
        q_out, k_out, v_out = [], [], []
        for hh, (qblk, kblk, vblk) in enumerate(zip(_blocks(fq, FOX_HEADS), _blocks(fk, FOX_HEADS),
                                                    _blocks(fv, FOX_HEADS), strict=True)):
            f_h = _lane_column(fc, lane, hh) * (1.0 / scale_fox)
            q_out.append(_put3(jnp.where(ones_q, 1.0, qblk), lane, FOX_Q_F, _split3(f_h)))
            k_out.append(_put3(jnp.where(ones_k, 1.0, kblk), lane, FOX_Q_ONES, _split3(-f_h)))
            v_out.append(jnp.where(ones_v, 1.0, vblk))
        return [jnp.concatenate(q_out, axis=1), jnp.concatenate(k_out, axis=1), jnp.concatenate(v_out, axis=1)], []

    q_fox, k_fox, v_fox = _rowwise("fox_qkv", f_fox_qkv, [zwin("fq"), zwin("fk"), zwin("fv"), fcol],
                                   [], [(hw, BF16)] * 3)
    o_fox, qb_fox = _flash_fwd(q_fox, k_fox, v_fox, scale_fox, FOX_HEADS, FOX_Q_L, name="fox_fwd")

    y_mla = _mm(o_mla, w_o_mla, out_dtype=BF16, name="proj_o_mla")
    y_fox = _mm(o_fox, w_o_fox, out_dtype=BF16, name="proj_o_fox")

    def f_gate(ti, pa):
        ga, gb, ya, yb = (a.astype(F32) for a in ti)
        return [_sigmoid(ga) * ya + _sigmoid(gb) * yb], []

    (merged,) = _rowwise("gate", f_gate, [zwin("ga"), zwin("gb"), y_mla, y_fox], [], [(d, BF16)])
    wts = {**wts, **late_weights("mlp", merged)}
    mix = _mm(merged, wts["w_out"], out_dtype=F32, name="proj_out")

    def f_resid1(ti, pa):
        xa, mx = ti
        y, _ = _rms(mx, pa[0])
        x1 = xa + y
        h2, _ = _rms(x1, pa[1])
        return [x1, h2], []

    x1, h2 = _rowwise("resid_mix", f_resid1, [x, mix], [wts["ln_post_mix"], wts["ln_pre_mlp"]], [(d, F32), (d, BF16)])

    def relu2(acc, ex):
        r = jnp.maximum(acc, 0.0)
        return [acc, r * r]

    u, act = _mm(h2, wts["w_ff1"], out_dtype=[BF16, BF16], name="ff1", epilogue=relu2)
    mo = _mm(act, wts["w_ff2"], out_dtype=F32, name="ff2")

    def f_loss(ti, pa):
        xa, mv, tg = ti
        y, _ = _rms(mv, pa[0])
        err = (xa + y) - tg
        g2 = err / d
        dmo, dg = _rms_bwd(mv, pa[0], g2)
        return [g2, dmo], [jnp.sum(err * err, axis=0, keepdims=True), dg]

    g2, d_mo, loss_cols, g_ln_post_mlp = _rowwise("loss", f_loss, [x1, mo, target], [wts["ln_post_mlp"]],
                                                  [(d, F32), (d, BF16)], [d, d])
    loss = 0.5 * jnp.sum(loss_cols) / d

    grads = {"ln_post_mlp": g_ln_post_mlp}
    grads["w_ff2"] = _mm_tn(act, d_mo, name="grad_ff2")

    def relu2_bwd(acc, ex):
        return [acc * (2.0 * jnp.maximum(ex[0], 0.0))]

    (d_u,) = _mm(d_mo, wts["w_ff2"], out_dtype=[BF16], name="ff2_bwd", transpose_b=True, extras=[u], epilogue=relu2_bwd)
    grads["w_ff1"] = _mm_tn(h2, d_u, name="grad_ff1")
    d_h2 = _mm(d_u, wts["w_ff1"], out_dtype=F32, name="ff1_bwd", transpose_b=True)

    def f_resid1_bwd(ti, pa):
        gres, dh2, x1v, mx = ti
        dx1n, dg_pre_mlp = _rms_bwd(x1v, pa[1], dh2)
        dx1 = gres + dx1n
        dmix, dg_post_mix = _rms_bwd(mx, pa[0], dx1)
        return [dx1, dmix], [dg_post_mix, dg_pre_mlp]

    d_x1, d_mix, grads["ln_post_mix"], grads["ln_pre_mlp"] = _rowwise(
        "resid_mix_bwd", f_resid1_bwd, [g2, d_h2, x1, mix], [wts["ln_post_mix"], wts["ln_pre_mlp"]],
        [(d, F32), (d, BF16)], [d, d])
    grads["w_out"] = _mm_tn(merged, d_mix, name="grad_out")
    d_merged = _mm(d_mix, wts["w_out"], out_dtype=BF16, name="proj_out_bwd", transpose_b=True)

    def f_gate_bwd(ti, pa):
        dm, ga, gb, ya, yb = (a.astype(F32) for a in ti)
        sa, sb = _sigmoid(ga), _sigmoid(gb)
        return [dm * sa, dm * sb, dm * ya * (sa * (1.0 - sa)), dm * yb * (sb * (1.0 - sb))], []

    d_ya, d_yb, d_ga, d_gb = _rowwise("gate_bwd", f_gate_bwd,
                                      [d_merged, zwin("ga"), zwin("gb"), y_mla, y_fox], [],
                                      [(d, BF16)] * 4)
    grads["w_o_mla"] = _unpad_head_rows(_mm_tn(o_mla, d_ya, name="grad_o_mla"), MLA_HEADS, MLA_V)
    grads["w_o_fox"] = _unpad_head_rows(_mm_tn(o_fox, d_yb, name="grad_o_fox"), FOX_HEADS, FOX_DIM)
    w_o_mla = w_o_mla + send_grads("early", {nm: grads[nm] for nm in EARLY_GRADS}).astype(BF16)
    do_mla = _with_delta(_mm(d_ya, w_o_mla, out_dtype=F32, name="proj_o_mla_bwd", transpose_b=True), o_mla,
                         MLA_HEADS, name="mla_delta")
    do_fox = _with_delta(_mm(d_yb, w_o_fox, out_dtype=F32, name="proj_o_fox_bwd", transpose_b=True), o_fox,
                         FOX_HEADS, name="fox_delta")

    dk_mla, dv_mla, dq_mla = _flash_bwd(qb_mla, k_mla, v_mla, do_mla, scale_mla, MLA_HEADS, MLA_NOPE + MLA_ROPE,
                                        name="mla_bwd")
    dk_fox, dv_fox, dq_fox = _flash_bwd(qb_fox, k_fox, v_fox, do_fox, scale_fox, FOX_HEADS, FOX_DIM, name="fox_bwd")
    d_fl = _forget_cumsum_bwd(dq_fox, dk_fox, z, fl_cb, FOX_HEADS)

    def f_rope_q_bwd(ti, pa):
        g, c_, a_, b_ = ti
        return [jnp.concatenate([_rope_block(blk, c_, -a_, -b_) for blk in _blocks(g, MLA_HEADS)], axis=1)], []

    (d_qf,) = _rowwise("rope_q_bwd", f_rope_q_bwd, [dq_mla, rc, ra, rb], [], [(hw, BF16)])
    grads["w_uq"] = _unpad_heads(_mm_tn(cqn, d_qf, name="grad_uq"), MLA_HEADS, MLA_NOPE + MLA_ROPE)
    d_cqn = _mm(d_qf, w_uq, out_dtype=F32, name="proj_uq_bwd", transpose_b=True)

    def f_mla_kv_bwd(ti, pa):
        gk, gv, c_, a_, b_ = ti
        k_blocks = _blocks(gk, MLA_HEADS)
        tot = k_blocks[0]
        for blk in k_blocks[1:]:
            tot = tot + blk
        return [jnp.concatenate([gk, gv], axis=1), _rope_block(tot, c_, -a_, -b_)], []

    d_kv, d_kr = _rowwise("mla_kv_bwd", f_mla_kv_bwd, [dk_mla, dv_mla, rc, ra, rb], [], [(2 * hw, BF16), (LANES, BF16)])
    g_ukv = _mm_tn(ckvn, d_kv, name="grad_ukv")
    grads["w_uk"] = _unpad_heads(g_ukv[:, :hw], MLA_HEADS, MLA_NOPE)
    grads["w_uv"] = _unpad_heads(g_ukv[:, hw:], MLA_HEADS, MLA_V)
    d_ckvn = _mm(d_kv, w_ukv, out_dtype=F32, name="proj_ukv_bwd", transpose_b=True)

    def f_mla_norms_bwd(ti, pa):
        cq, ckv, dcqn, dckvn = ti
        dcq, dg_q = _rms_bwd(cq, pa[0], dcqn)
        dckv, dg_kv = _rms_bwd(ckv, pa[1], dckvn)
        return [dcq, dckv], [dg_q, dg_kv]

    d_cq, d_ckv, grads["q_a_norm"], grads["kv_a_norm"] = _rowwise(
        "mla_norms_bwd", f_mla_norms_bwd, [zwin("cq"), zwin("ckv"), d_cqn, d_ckvn],
        [wts["q_a_norm"], wts["kv_a_norm"]], [(MLA_Q_LORA, BF16), (MLA_KV_LORA, BF16)], [MLA_Q_LORA, MLA_KV_LORA])

    def f_fox_compact(ti, pa):
        gather = _head_spread(FOX_HEADS, FOX_DIM)
        return [lax.dot_general(a.astype(BF16), gather, _NT, preferred_element_type=F32) for a in ti], []

    d_fq, d_fk, d_fv = _rowwise("fox_compact", f_fox_compact, [dq_fox, dk_fox, dv_fox], [],
                                [(FOX_HEADS * FOX_DIM, BF16)] * 3)
    d_z = jnp.concatenate([d_ga, d_gb, d_fq, d_fk, d_fv, d_cq, d_ckv, d_kr, d_fl], axis=1)
    assert d_z.shape[1] == zl.width

    def f_bias(ti, pa):
        return [], [jnp.sum(ti[0].astype(F32), axis=0, keepdims=True)]

    (g_b_in,) = _rowwise("grad_b_in", f_bias, [d_z], [], [], [zl.width])
    grads["b_in"] = zl.from_kernel(g_b_in)
    grads["w_in"] = zl.from_kernel(_mm_tn(h, d_z, name="grad_in"))
    tok = send_grads("late", {nm: grads[nm] for nm, _ in BIG if nm not in EARLY_GRADS})
    d_h = _mm(d_z, w_in, bias=jnp.zeros((1, d), F32) + tok, out_dtype=F32, name="proj_in_bwd", transpose_b=True)

    def f_norm_in_bwd(ti, pa):
        dx1v, dh, xa = ti
        dxn, dg = _rms_bwd(xa, pa[0], dh)
        return [dx1v + dxn], [dg]

    grad_x, grads["ln_pre_mix"] = _rowwise("norm_in_bwd", f_norm_in_bwd, [d_x1, d_h, x], [wts["ln_pre_mix"]],
                                           [(d, F32)], [d])
    return loss, grad_x, grads


class _PackLayout:
    def __init__(self, shapes):
        self.shapes = list(shapes)
        self.width = _round_up(max(b for _, b in shapes), LANES)
        self.bands = []
        row = 0
        shelf = []
        for idx, (a, b) in enumerate(shapes):
            if 2 * _round_up(b, LANES) > self.width:
                self.bands.append((row, _round_up(a, 32), [(idx, 0)]))
                row += _round_up(a, 32)
            else:
                shelf.append(idx)
        col, items = 0, []
        for idx in shelf:
            wb = _round_up(shapes[idx][1], LANES)
            if col + wb > self.width:
                hgt = max(_round_up(shapes[i][0], 32) for i, _ in items)
                self.bands.append((row, hgt, items))
                row += hgt
                col, items = 0, []
            items.append((idx, col))
            col += wb
        if items:
            hgt = max(_round_up(shapes[i][0], 32) for i, _ in items)
            self.bands.append((row, hgt, items))
            row += hgt
        self.rows = _round_up(row, 16 * LOCAL_PIECES)

    def pack(self, arrs):
        lead = arrs[0].shape[:-2]
        no_pad = [(0, 0)] * len(lead)
        bands = []
        for _, hgt, items in self.bands:
            parts = []
            for k, (idx, col) in enumerate(items):
                a, b = self.shapes[idx]
                nxt = items[k + 1][1] if k + 1 < len(items) else self.width
                parts.append(jnp.pad(arrs[idx], no_pad + [(0, hgt - a), (0, nxt - col - b)]))
            bands.append(parts[0] if len(parts) == 1 else jnp.concatenate(parts, axis=-1))
        used = sum(hgt for _, hgt, _ in self.bands)
        if used < self.rows:
            bands.append(jnp.zeros(lead + (self.rows - used, self.width), arrs[0].dtype))
        return jnp.concatenate(bands, axis=-2)

    def unpack(self, packed):
        out = [None] * len(self.shapes)
        for row, _, items in self.bands:
            for idx, col in items:
                a, b = self.shapes[idx]
                out[idx] = packed[..., row:row + a, col:col + b]
        return out


def _to_shards(g, axis):
    if axis == 0:
        return g.reshape(N_CHIPS, g.shape[0] // N_CHIPS, g.shape[1])
    return jnp.stack(jnp.split(g, N_CHIPS, axis=1))


def _from_shards(s4, axis):
    n, a, b = s4.shape
    if axis == 0:
        return s4.reshape(n * a, b)
    return jnp.concatenate([s4[ch] for ch in range(n)], axis=1)


ANY = pl.BlockSpec(memory_space=pl.ANY)


def _place():
    return lax.axis_index("x"), lax.axis_index("y"), lax.axis_index("c")


def _gather_weights(wpk):
    rows, wd = wpk.shape
    half = rows // 2

    def body(w_ref, out_ref, send_sems, recv_sems, local_sems):
        x, y, c = _place()
        sibling = (x, y, 1 - c)
        chips = [(1 - x, y), (x, 1 - y), (1 - x, 1 - y)]

        def slab(chip, hf):
            return out_ref.at[2 * chip[0] + chip[1], pl.ds(hf * half, half), :]

        def copy(k, chip, hf, to, src=None):
            return pltpu.make_async_remote_copy(
                src_ref=slab(chip, hf) if src is None else src, dst_ref=slab(chip, hf),
                send_sem=send_sems.at[k], recv_sem=recv_sems.at[k], device_id=to, device_id_type=MESH)

        piece = rows // LOCAL_PIECES
        mine = [pltpu.make_async_copy(w_ref.at[pl.ds(n * piece, piece), :],
                                      out_ref.at[2 * x + y, pl.ds(n * piece, piece), :], local_sems.at[n])
                for n in range(LOCAL_PIECES)]
        for cp in mine:
            cp.start()
        first = [copy(j, (x, y), c, (*chip, c), src=w_ref.at[pl.ds(c * half, half), :]) for j, chip in enumerate(chips)]
        for cp in first:
            cp.start()
        passed = [copy(3 + j, chip, c, sibling) for j, chip in enumerate(chips)]
        for j, chip in enumerate(chips):
            copy(j, chip, c, (x, y, c)).wait_recv()
            passed[j].start()
        for j, chip in enumerate(chips):
            copy(3 + j, chip, 1 - c, (x, y, c)).wait_recv()
        for cp in first + passed:
            cp.wait_send()
        for cp in mine:
            cp.wait()

    assert rows % (16 * LOCAL_PIECES) == 0
    return pl.pallas_call(
        body, out_shape=jax.ShapeDtypeStruct((N_CHIPS, rows, wd), wpk.dtype),
        in_specs=[ANY], out_specs=ANY,
        scratch_shapes=[pltpu.SemaphoreType.DMA((6,)), pltpu.SemaphoreType.DMA((6,)),
                        pltpu.SemaphoreType.DMA((LOCAL_PIECES,))],
        name="gather_weights")(wpk)


HBM = pl.BlockSpec(memory_space=pltpu.HBM)
SEM = pl.BlockSpec(memory_space=pltpu.SEMAPHORE)
EFFECT = pltpu.SideEffectType.DATAFLOW_SIDE_EFFECTING
N_LATE = 6


def _gather_late_start(wpk, tag):
    rows, wd = wpk.shape
    half = rows // 2

    def body(w_ref, land_ref, send_sems, recv_sems, w_thru, land_thru, token):
        x, y, c = _place()
        chips = [(1 - x, y), (x, 1 - y), (1 - x, 1 - y)]
        for j, chip in enumerate(chips):
            for to_core in range(2):
                pltpu.make_async_remote_copy(
                    src_ref=w_ref.at[pl.ds(c * half, half), :],
                    dst_ref=land_ref.at[2 * x + y, pl.ds(c * half, half), :],
                    send_sem=send_sems.at[2 * j + to_core], recv_sem=recv_sems.at[2 * j + c],
                    device_id=(*chip, to_core), device_id_type=MESH).start()
        token[...] = jnp.zeros_like(token)

    land = lax.empty((N_CHIPS, rows, wd), wpk.dtype)
    return pl.pallas_call(
        body, name="gather_" + tag + "_start",
        out_shape=(pltpu.SemaphoreType.DMA((N_LATE,)), pltpu.SemaphoreType.DMA((N_LATE,)),
                   pltpu.HBM(wpk.shape, wpk.dtype), pltpu.HBM(land.shape, land.dtype),
                   jax.ShapeDtypeStruct((8, LANES), F32)),
        in_specs=(HBM, HBM), out_specs=(SEM, SEM, HBM, HBM, pl.BlockSpec(memory_space=pltpu.VMEM)),
        input_output_aliases={0: 2, 1: 3},
        compiler_params=pltpu.CompilerParams(has_side_effects=EFFECT),
    )(pltpu.with_memory_space_constraint(wpk, pltpu.HBM), pltpu.with_memory_space_constraint(land, pltpu.HBM))


def _gather_late_wait(send_sems, recv_sems, w_thru, land_thru, after, tag):
    rows, wd = w_thru.shape
    half = rows // 2

    def body(w_ref, land_ref, send_sems, recv_sems, after_ref, w_dead, land_out):
        x, y, c = _place()
        for n in range(N_LATE):
            cp = pltpu.make_async_remote_copy(
                src_ref=w_ref.at[pl.ds(0, half), :], dst_ref=land_ref.at[0, pl.ds(0, half), :],
                send_sem=send_sems.at[n], recv_sem=recv_sems.at[n], device_id=(x, y, c), device_id_type=MESH)
            cp.wait_send()
            cp.wait_recv()

    return pl.pallas_call(
        body, name="gather_" + tag + "_wait",
        out_shape=(pltpu.HBM(w_thru.shape, w_thru.dtype), pltpu.HBM(land_thru.shape, land_thru.dtype)),
        in_specs=(HBM, HBM, SEM, SEM, ANY), out_specs=(HBM, HBM), input_output_aliases={0: 0, 1: 1},
        compiler_params=pltpu.CompilerParams(has_side_effects=EFFECT),
    )(w_thru, land_thru, send_sems, recv_sems, after)[1]


N_PART = 7


def _reduce_start(gbf, tag):
    _, _, hrows, wd = gbf.shape

    def body(g_ref, land_ref, send_sems, recv_sems, g_thru, land_thru, token):
        x, y, c = _place()
        chips = [(1 - x, y), (x, 1 - y), (1 - x, 1 - y)]
        for j, chip in enumerate(chips):
            for to_core in range(2):
                pltpu.make_async_remote_copy(
                    src_ref=g_ref.at[2 * chip[0] + chip[1], to_core], dst_ref=land_ref.at[2 * j + c],
                    send_sem=send_sems.at[2 * j + to_core], recv_sem=recv_sems.at[2 * j + c],
                    device_id=(*chip, to_core), device_id_type=MESH).start()
        pltpu.make_async_remote_copy(
            src_ref=g_ref.at[2 * x + y, 1 - c], dst_ref=land_ref.at[N_PART - 1],
            send_sem=send_sems.at[N_PART - 1], recv_sem=recv_sems.at[N_PART - 1],
            device_id=(x, y, 1 - c), device_id_type=MESH).start()
        token[...] = jnp.zeros_like(token)

    land = lax.empty((N_PART, hrows, wd), gbf.dtype)
    return pl.pallas_call(
        body, name="reduce_" + tag + "_start",
        out_shape=(pltpu.SemaphoreType.DMA((N_PART,)), pltpu.SemaphoreType.DMA((N_PART,)),
                   pltpu.HBM(gbf.shape, gbf.dtype), pltpu.HBM(land.shape, land.dtype),
                   jax.ShapeDtypeStruct((8, LANES), F32)),
        in_specs=(HBM, HBM), out_specs=(SEM, SEM, HBM, HBM, pl.BlockSpec(memory_space=pltpu.VMEM)),
        input_output_aliases={0: 2, 1: 3},
        compiler_params=pltpu.CompilerParams(has_side_effects=EFFECT),
    )(pltpu.with_memory_space_constraint(gbf, pltpu.HBM), pltpu.with_memory_space_constraint(land, pltpu.HBM))


def _reduce_wait(send_sems, recv_sems, g_thru, land_thru, after, tag):
    def body(g_ref, land_ref, send_sems, recv_sems, after_ref, g_dead, land_out):
        x, y, c = _place()
        for n in range(N_PART):
            cp = pltpu.make_async_remote_copy(
                src_ref=g_ref.at[0, 0], dst_ref=land_ref.at[0], send_sem=send_sems.at[n], recv_sem=recv_sems.at[n],
                device_id=(x, y, c), device_id_type=MESH)
            cp.wait_send()
            cp.wait_recv()

    return pl.pallas_call(
        body, name="reduce_" + tag + "_wait",
        out_shape=(pltpu.HBM(g_thru.shape, g_thru.dtype), pltpu.HBM(land_thru.shape, land_thru.dtype)),
        in_specs=(HBM, HBM, SEM, SEM, ANY), out_specs=(HBM, HBM), input_output_aliases={0: 0, 1: 1},
        compiler_params=pltpu.CompilerParams(has_side_effects=EFFECT),
    )(g_thru, land_thru, send_sems, recv_sems, after)[1]


def _sibling_swap(mine):
    def body(m_ref, out_ref, send_sem, recv_sem):
        x, y, c = _place()
        cp = pltpu.make_async_remote_copy(
            src_ref=m_ref, dst_ref=out_ref, send_sem=send_sem, recv_sem=recv_sem,
            device_id=(x, y, 1 - c), device_id_type=MESH)
        cp.start()
        cp.wait()

    return pl.pallas_call(
        body, out_shape=jax.ShapeDtypeStruct(mine.shape, mine.dtype), in_specs=[ANY], out_specs=ANY,
        scratch_shapes=[pltpu.SemaphoreType.DMA, pltpu.SemaphoreType.DMA], name="grad_sibling_swap")(mine)


def _adamw(w, g, m, v):
    m = ADAM_B1 * m + (1.0 - ADAM_B1) * g
    v = ADAM_B2 * v + (1.0 - ADAM_B2) * (g * g)
    m_hat = m / (1.0 - ADAM_B1 ** ADAM_STEP)
    v_hat = v / (1.0 - ADAM_B2 ** ADAM_STEP)
    delta = -ADAM_LR * (m_hat / (jnp.sqrt(v_hat) + ADAM_EPS) + ADAM_WD * w)
    return delta, m, v


def _small_allreduce_adamw(gs, ws, ms, vs):
    n_dev = 8
    n_par = len(gs)
    wd = PACK_W
    chunks = []
    for p, g in enumerate(gs):
        for off in range(0, g.shape[1], wd):
            chunks.append((p, len(chunks), off, min(wd, g.shape[1] - off)))
    rows = _round_up(len(chunks), 8)

    def body(*refs):
        g_refs, w_refs, m_refs, v_refs = (refs[k * n_par:(k + 1) * n_par] for k in range(4))
        go_refs, d_refs, mo_refs, vo_refs = (refs[(4 + k) * n_par:(5 + k) * n_par] for k in range(4))
        mine_ref, all_ref, send_sems, recv_sems = refs[8 * n_par:]
        x, y, c = _place()
        me, sibling = (x, y, c), (x, y, 1 - c)
        chips = [(1 - x, y), (x, 1 - y), (1 - x, 1 - y)]

        def slot(px, py, pc):
            return all_ref.at[4 * px + 2 * py + pc]

        def copy(k, block, to, src=None):
            return pltpu.make_async_remote_copy(
                src_ref=slot(*block) if src is None else src, dst_ref=slot(*block),
                send_sem=send_sems.at[k], recv_sem=recv_sems.at[k], device_id=to, device_id_type=MESH)

        mine_ref[...] = jnp.zeros_like(mine_ref)
        for p, row, off, width in chunks:
            mine_ref[row:row + 1, 0:width] = g_refs[p][:, off:off + width]
        all_ref[4 * x + 2 * y + c] = mine_ref[...]
        first = [copy(0, me, sibling, src=mine_ref)]
        first += [copy(1 + j, me, (*chip, c), src=mine_ref) for j, chip in enumerate(chips)]
        for cp in first:
            cp.start()
        passed = [copy(4 + j, (*chip, c), sibling) for j, chip in enumerate(chips)]
        for j, chip in enumerate(chips):
            copy(1 + j, (*chip, c), me).wait_recv()
            passed[j].start()
        copy(0, sibling, me).wait_recv()
        for j, chip in enumerate(chips):
            copy(4 + j, (*chip, 1 - c), me).wait_recv()
        for cp in first + passed:
            cp.wait_send()
        tot = jnp.zeros((rows, wd), F32)
        for dev in range(n_dev):
            tot = tot + all_ref[dev]
        mine_ref[...] = tot
        for p, row, off, width in chunks:
            cols = slice(off, off + width)
            g = mine_ref[row:row + 1, 0:width]
            delta, m_new, v_new = _adamw(w_refs[p][:, cols], g, m_refs[p][:, cols], v_refs[p][:, cols])
            go_refs[p][:, cols] = g
            d_refs[p][:, cols] = delta
            mo_refs[p][:, cols] = m_new
            vo_refs[p][:, cols] = v_new

    vm = pl.BlockSpec(memory_space=pltpu.VMEM)
    shp = [jax.ShapeDtypeStruct(g.shape, F32) for g in gs]
    res = pl.pallas_call(
        body, out_shape=shp * 4, in_specs=[vm] * (4 * n_par), out_specs=[vm] * (4 * n_par),
        scratch_shapes=[pltpu.VMEM((rows, wd), F32), pltpu.VMEM((n_dev, rows, wd), F32),
                        pltpu.SemaphoreType.DMA((7,)), pltpu.SemaphoreType.DMA((7,))],
        name="small_allreduce_adamw")(*gs, *ws, *ms, *vs)
    return [res[k * n_par:(k + 1) * n_par] for k in range(4)]


def kernel(x, positions, ln_pre_mix, ln_post_mix, ln_pre_mlp, ln_post_mlp, w_in, b_in, q_a_norm, w_uq, kv_a_norm, w_uk, w_uv, w_o_mla, w_o_fox, w_out, w_ff1, w_ff2, loss_target, m_ln_pre_mix, m_ln_post_mix, m_ln_pre_mlp, m_ln_post_mlp, m_w_in, m_b_in, m_q_a_norm, m_w_uq, m_kv_a_norm, m_w_uk, m_w_uv, m_w_o_mla, m_w_o_fox, m_w_out, m_w_ff1, m_w_ff2, v_ln_pre_mix, v_ln_post_mix, v_ln_pre_mlp, v_ln_post_mlp, v_w_in, v_b_in, v_q_a_norm, v_w_uq, v_kv_a_norm, v_w_uk, v_w_uv, v_w_o_mla, v_w_o_fox, v_w_out, v_w_ff1, v_w_ff2):
    w = dict(ln_pre_mix=ln_pre_mix, ln_post_mix=ln_post_mix, ln_pre_mlp=ln_pre_mlp, ln_post_mlp=ln_post_mlp, w_in=w_in,
             b_in=b_in, q_a_norm=q_a_norm, w_uq=w_uq, kv_a_norm=kv_a_norm, w_uk=w_uk, w_uv=w_uv, w_o_mla=w_o_mla,
             w_o_fox=w_o_fox, w_out=w_out, w_ff1=w_ff1, w_ff2=w_ff2)
    mom = dict(ln_pre_mix=m_ln_pre_mix, ln_post_mix=m_ln_post_mix, ln_pre_mlp=m_ln_pre_mlp, ln_post_mlp=m_ln_post_mlp,
               w_in=m_w_in, b_in=m_b_in, q_a_norm=m_q_a_norm, w_uq=m_w_uq, kv_a_norm=m_kv_a_norm, w_uk=m_w_uk,
               w_uv=m_w_uv, w_o_mla=m_w_o_mla, w_o_fox=m_w_o_fox, w_out=m_w_out, w_ff1=m_w_ff1, w_ff2=m_w_ff2)
    var = dict(ln_pre_mix=v_ln_pre_mix, ln_post_mix=v_ln_post_mix, ln_pre_mlp=v_ln_pre_mlp, ln_post_mlp=v_ln_post_mlp,
               w_in=v_w_in, b_in=v_b_in, q_a_norm=v_q_a_norm, w_uq=v_w_uq, kv_a_norm=v_kv_a_norm, w_uk=v_w_uk,
               w_uv=v_w_uv, w_o_mla=v_w_o_mla, w_o_fox=v_w_o_fox, w_out=v_w_out, w_ff1=v_w_ff1, w_ff2=v_w_ff2)

    big_names = [nm for nm, _ in BIG]
    c = lax.axis_index("c")
    chip = 2 * lax.axis_index("x") + lax.axis_index("y")

    axes = dict(BIG)

    def assemble(names, lay, gathered):
        return {nm: _from_shards(s4, axes[nm] - 1) for nm, s4 in zip(names, lay.unpack(gathered), strict=True)}

    groups = {"mixer": [nm for nm in big_names if nm in MIXER], "mlp": [nm for nm in big_names if nm in LATE]}
    first_names = [nm for nm in big_names if nm not in MIXER and nm not in LATE]
    full = {nm: wv for nm, wv in w.items() if nm in SMALL}
    travelling = {}
    for tag, names in groups.items():
        lay = _PackLayout([w[nm].shape[1:] for nm in names])
        shard = lay.pack([w[nm][0].astype(BF16) for nm in names])
        started = _gather_late_start(shard, tag)
        travelling[tag] = (names, lay, shard, started[:4])
        full["b_in"] = full["b_in"] + started[4][0, 0]
    lay_first = _PackLayout([w[nm].shape[1:] for nm in first_names])
    full.update(assemble(first_names, lay_first,
                         _gather_weights(lay_first.pack([w[nm][0].astype(BF16) for nm in first_names]))))

    def late_weights(tag, after):
        names, lay, shard, handles = travelling[tag]
        land = _gather_late_wait(*handles, after, tag)
        land = lax.dynamic_update_slice(land, shard[None], (chip, 0, 0))
        return assemble(names, lay, land)

    grad_groups = {"early": [nm for nm in big_names if nm in EARLY_GRADS],
                   "late": [nm for nm in big_names if nm not in EARLY_GRADS]}
    sent = {}

    def send_grads(tag, g):
        names = grad_groups[tag]
        lay = _PackLayout([w[nm].shape[1:] for nm in names])
        hrows = lay.rows // 2
        gbf = lay.pack([_to_shards(g[nm], axes[nm] - 1).astype(BF16) for nm in names])
        gbf = gbf.reshape(N_CHIPS, 2, hrows, lay.width)
        started = _reduce_start(gbf, tag)
        own = lax.dynamic_index_in_dim(lax.dynamic_index_in_dim(started[2], chip, axis=0, keepdims=False), c, axis=0,
                                       keepdims=False)
        sent[tag] = (names, lay, started[:4], own)
        return started[4][0, 0]

    loss_local, grad_x, grads = _local_step(x[0], positions[0], loss_target[0], full, late_weights, send_grads)

    def f_add8(ti, pa):
        tot = ti[0].astype(F32)
        for part in ti[1:]:
            tot = tot + part.astype(F32)
        return [tot], []

    reduced = []
    for tag, (names, lay, handles, own) in sent.items():
        parts = _reduce_wait(*handles, grad_x, tag)
        reduced.append(_rowwise("grad_add_" + tag, f_add8, [own] + [parts[n] for n in range(N_PART)], [],
                                [(lay.width, F32)])[0])

    assert len({lay.width for _, lay, _, _ in sent.values()}) == 1
    red = jnp.concatenate(reduced, axis=0)
    sib = _sibling_swap(red)
    lower, upper = jnp.where(c == 0, red, sib), jnp.where(c == 0, sib, red)
    g_by_name, row = {}, 0
    for names, lay, _, _ in sent.values():
        hrows = lay.rows // 2
        both = jnp.concatenate([lower[row:row + hrows], upper[row:row + hrows]], axis=0)
        g_by_name.update(zip(names, lay.unpack(both), strict=True))
        row += hrows
    g_shards = [g_by_name[nm] for nm in big_names]

    def f_adamw(ti, pa):
        wv, gv, mv, vv = ti
        return list(_adamw(wv, gv, mv, vv)), []

    out = {"grad": {}, "delta": {}, "m": {}, "v": {}}
    for nm, g_sh in zip(big_names, g_shards, strict=True):
        wd = g_sh.shape[1]
        d_sh, m_sh, v_sh = _rowwise("adamw_" + nm, f_adamw, [w[nm][0], g_sh, mom[nm][0], var[nm][0]], [], [(wd, F32)] * 3)
        out["grad"][nm], out["delta"][nm], out["m"][nm], out["v"][nm] = g_sh[None], d_sh[None], m_sh[None], v_sh[None]

    loss_row = jnp.zeros((1, LANES), F32) + loss_local
    blank = jnp.zeros((1, LANES), F32)
    small = _small_allreduce_adamw([grads[nm] for nm in SMALL] + [loss_row], [w[nm] for nm in SMALL] + [blank],
                                   [mom[nm] for nm in SMALL] + [blank], [var[nm] for nm in SMALL] + [blank])
    for kind, arrs in zip(("grad", "delta", "m", "v"), small, strict=True):
        for nm, arr in zip(SMALL, arrs[:len(SMALL)], strict=True):
            out[kind][nm] = arr
    loss = small[0][len(SMALL)][0, 0]

    return (loss, grad_x[None], *[out["grad"][nm] for nm in ALL_W], *[out["delta"][nm] for nm in ALL_W],
            *[out["m"][nm] for nm in ALL_W], *[out["v"][nm] for nm in ALL_W])
```

```python
import functools
import math

import jax
import jax.numpy as jnp
from jax import lax
from jax.experimental import pallas as pl
from jax.experimental.pallas import tpu as pltpu

F32 = jnp.float32
BF16 = jnp.bfloat16

MLA_HEADS = 8
MLA_Q_LORA = 256
MLA_KV_LORA = 128
MLA_NOPE = 64
MLA_ROPE = 32
MLA_V = 64
FOX_HEADS = 8
FOX_DIM = 64
ROPE_THETA = 10000.0
NORM_EPS = 1e-6
HALF_ROPE = MLA_ROPE // 2

ADAM_LR = 0.001
ADAM_B1 = 0.9
ADAM_B2 = 0.999
ADAM_EPS = 1e-08
ADAM_WD = 0.01
ADAM_STEP = 10

LANES = 128
VMEM_LIMIT = 56 * 1024 * 1024
ATT_TILE = 1024
FWD_GROUP_LOG2 = 1
FWD_GROUP = 1 << FWD_GROUP_LOG2
MM_VMEM_BUDGET = 40 * 1024 * 1024
MXU_WIDTH = 256
MXU_MACS_PER_S = 4.98e14
HBM_BYTES_PER_S = 3.2e12
STEP_OVERHEAD_S = 0.35e-6
NEG = -1e30
LOG2E = math.log2(math.e)
MESH = pl.DeviceIdType.MESH

V_ONES = 64
FOX_Q_F = 64
FOX_Q_L = 67
FOX_Q_ONES = 70
MLA_Q_L = 96

BIG = (("w_in", 2), ("w_uq", 2), ("w_uk", 2), ("w_uv", 2), ("w_o_mla", 2), ("w_o_fox", 2),
       ("w_out", 1), ("w_ff1", 2), ("w_ff2", 1))
MIXER = ("w_uq", "w_uk", "w_uv", "w_o_mla", "w_o_fox")
LATE = ("w_out", "w_ff1", "w_ff2")
EARLY_GRADS = ("w_o_mla", "w_o_fox", "w_out", "w_ff1", "w_ff2")
SMALL = ("ln_pre_mix", "ln_post_mix", "ln_pre_mlp", "ln_post_mlp", "b_in", "q_a_norm", "kv_a_norm")
ALL_W = ("ln_pre_mix", "ln_post_mix", "ln_pre_mlp", "ln_post_mlp", "w_in", "b_in", "q_a_norm", "w_uq",
         "kv_a_norm", "w_uk", "w_uv", "w_o_mla", "w_o_fox", "w_out", "w_ff1", "w_ff2")
N_CHIPS = 4
PACK_W = 1024
LOCAL_PIECES = 8

_NT = (((1,), (1,)), ((), ()))
_TN = (((0,), (0,)), ((), ()))


def _cparams(sem=None):
    return pltpu.CompilerParams(dimension_semantics=sem, vmem_limit_bytes=VMEM_LIMIT)


def _divisor_tile(n, limit, mult):
    if n <= limit:
        return n
    best = None
    t = mult
    while t <= limit:
        if n % t == 0:
            best = t
        t += mult
    assert best is not None, (n, limit, mult)
    return best


def _round_up(v, mult):
    return -(-v // mult) * mult


def _mm_tiles(m, k, n, io_bytes):
    best = None
    for tm in (2048, 1024, 512, 256, 128):
        if m % tm:
            continue
        for tn in range(LANES, min(n, 2048) + 1, LANES):
            if n % tn:
                continue
            vmem = 2 * (tm * k * 2 + k * tn * 2 + tm * tn * io_bytes) + tm * tn * 4
            if vmem > MM_VMEM_BUDGET:
                continue
            mxu = m * k * n * (_round_up(tn, MXU_WIDTH) / tn) / MXU_MACS_PER_S
            hbm = (m * k * 2 + (m // tm) * k * n * 2 + m * n * io_bytes) / HBM_BYTES_PER_S
            cost = max(mxu, hbm) + (m // tm) * (n // tn) * STEP_OVERHEAD_S
            if best is None or cost < best[0]:
                best = (cost, tm, tn)
    assert best is not None, (m, k, n)
    return best[1], best[2]


def _mm(a, b, *, out_dtype, name, bias=None, transpose_b=False, extras=(), epilogue=None):
    m, k = a.shape
    n = b.shape[0] if transpose_b else b.shape[1]
    assert (b.shape[1] if transpose_b else b.shape[0]) == k and a.dtype == BF16 and b.dtype == BF16
    out_dtypes = list(out_dtype) if isinstance(out_dtype, (list, tuple)) else [out_dtype]
    n_ex = len(extras)
    tm, tn = _mm_tiles(m, k, n, sum(jnp.dtype(dt).itemsize for dt in out_dtypes) + 4 * n_ex)

    def body(*refs):
        a_ref, b_ref = refs[:2]
        pos = 2
        bias_ref = None
        if bias is not None:
            bias_ref = refs[pos]
            pos += 1
        ex_refs = refs[pos:pos + n_ex]
        o_refs = refs[pos + n_ex:]
        if transpose_b:
            acc = lax.dot_general(a_ref[...], b_ref[...], _NT, preferred_element_type=F32)
        else:
            acc = jnp.dot(a_ref[...], b_ref[...], preferred_element_type=F32)
        if bias_ref is not None:
            acc = acc + bias_ref[...]
        vals = [acc] if epilogue is None else epilogue(acc, [r[...] for r in ex_refs])
        for ref, val in zip(o_refs, vals, strict=True):
            ref[...] = val.astype(ref.dtype)

    b_spec = pl.BlockSpec((tn, k), lambda i, j: (j, 0)) if transpose_b else pl.BlockSpec((k, tn), lambda i, j: (0, j))
    in_specs = [pl.BlockSpec((tm, k), lambda i, j: (i, 0)), b_spec]
    args = [a, b]
    if bias is not None:
        in_specs.append(pl.BlockSpec((1, tn), lambda i, j: (0, j)))
        args.append(bias)
    in_specs += [pl.BlockSpec((tm, tn), lambda i, j: (i, j)) for _ in extras]
    args += list(extras)
    res = pl.pallas_call(
        body, grid=(m // tm, n // tn), in_specs=in_specs,
        out_specs=[pl.BlockSpec((tm, tn), lambda i, j: (i, j)) for _ in out_dtypes],
        out_shape=[jax.ShapeDtypeStruct((m, n), dt) for dt in out_dtypes],
        compiler_params=_cparams(("parallel", "parallel")), name=name)(*args)
    return res if isinstance(out_dtype, (list, tuple)) else res[0]


def _mm_tn(a, b, *, name):
    s, m = a.shape
    s2, n = b.shape
    assert s == s2 and a.dtype == BF16 and b.dtype == BF16
    tm = _divisor_tile(m, 1024, LANES)
    tn = _divisor_tile(n, 2304, LANES)
    tk = _divisor_tile(s, 512, 16)

    def body(a_ref, b_ref, o_ref):
        @pl.when(pl.program_id(2) == 0)
        def _():
            o_ref[...] = jnp.zeros_like(o_ref)

        o_ref[...] += lax.dot_general(a_ref[...], b_ref[...], _TN, preferred_element_type=F32)

    return pl.pallas_call(
        body, grid=(m // tm, n // tn, s // tk),
        in_specs=[pl.BlockSpec((tk, tm), lambda i, j, k: (k, i)), pl.BlockSpec((tk, tn), lambda i, j, k: (k, j))],
        out_specs=pl.BlockSpec((tm, tn), lambda i, j, k: (i, j)),
        out_shape=jax.ShapeDtypeStruct((m, n), F32),
        compiler_params=_cparams(("parallel", "parallel", "arbitrary")), name=name)(a, b)


def _rowwise(name, fn, tiled, params, outs, reds=(), reverse=False):
    wins = [t if isinstance(t, tuple) else (t, 0, t.shape[1]) for t in tiled]
    s = wins[0][0].shape[0]
    row_bytes = sum(w * arr.dtype.itemsize for arr, _, w in wins) + sum(w * jnp.dtype(d).itemsize for w, d in outs)
    ts = _divisor_tile(s, max(16, min(1024, (6 * 1024 * 1024) // row_bytes)), 16)
    nt, npar, nout = len(wins), len(params), len(outs)
    n_tiles = s // ts

    def row(i):
        return n_tiles - 1 - i if reverse else i

    def body(*refs):
        tin = [r[...] for r in refs[:nt]]
        par = [r[...] for r in refs[nt:nt + npar]]
        out_refs = refs[nt + npar:nt + npar + nout]
        red_refs = refs[nt + npar + nout:]
        o, r = fn(tin, par)
        for ref, val in zip(out_refs, o, strict=True):
            ref[...] = val.astype(ref.dtype)
        if red_refs:
            @pl.when(pl.program_id(0) == 0)
            def _():
                for ref in red_refs:
                    ref[...] = jnp.zeros_like(ref)

            for ref, val in zip(red_refs, r, strict=True):
                ref[...] += val

    in_specs = [pl.BlockSpec((ts, w), functools.partial(lambda i, cb: (row(i), cb), cb=cb)) for _, cb, w in wins]
    in_specs += [pl.BlockSpec(p.shape, lambda i: (0, 0)) for p in params]
    out_specs = [pl.BlockSpec((ts, w), lambda i: (row(i), 0)) for w, _ in outs]
    out_specs += [pl.BlockSpec((1, w), lambda i: (0, 0)) for w in reds]
    out_shape = [jax.ShapeDtypeStruct((s, w), d) for w, d in outs]
    out_shape += [jax.ShapeDtypeStruct((1, w), F32) for w in reds]
    return pl.pallas_call(
        body, grid=(n_tiles,), in_specs=in_specs, out_specs=out_specs, out_shape=out_shape,
        compiler_params=_cparams(("arbitrary",)), name=name)(*[w[0] for w in wins], *params)


def _rms(x, g):
    r = lax.rsqrt(jnp.mean(x * x, axis=-1, keepdims=True) + NORM_EPS)
    return x * r * g, r


def _rms_bwd(x, g, dy):
    r = lax.rsqrt(jnp.mean(x * x, axis=-1, keepdims=True) + NORM_EPS)
    gy = dy * g
    dx = r * gy - x * (r * r * r) * jnp.mean(x * gy, axis=-1, keepdims=True)
    dg = jnp.sum(dy * (x * r), axis=0, keepdims=True)
    return dx, dg


def _sigmoid(x):
    return 1.0 / (1.0 + jnp.exp(-x))


def _split3(x):
    hi = x.astype(BF16).astype(F32)
    r = x - hi
    mid = r.astype(BF16).astype(F32)
    lo = (r - mid).astype(BF16).astype(F32)
    return hi, mid, lo


def _lane(shape):
    return lax.broadcasted_iota(jnp.int32, shape, 1)


def _put3(blk, lane, pos, pieces):
    for k, piece in enumerate(pieces):
        blk = jnp.where(lane == pos + k, piece, blk)
    return blk


def _lane_column(blk, lane, pos):
    return jnp.sum(jnp.where(lane == pos, blk, 0.0), axis=1, keepdims=True)


def _blocks(a, nh):
    return [a[:, h * LANES:(h + 1) * LANES] for h in range(nh)]


def _head_spread(nh, d):
    shift = d.bit_length() - 1
    assert 1 << shift == d
    r = lax.broadcasted_iota(jnp.int32, (nh * d, nh * LANES), 0)
    c = lax.broadcasted_iota(jnp.int32, (nh * d, nh * LANES), 1)
    return jnp.where(c == (r >> shift) * LANES + (r & (d - 1)), 1.0, 0.0).astype(BF16)


def _rope_block(x, c, sa, sb):
    return x * c + pltpu.roll(x, LANES - HALF_ROPE, 1) * sa + pltpu.roll(x, HALF_ROPE, 1) * sb


def _forget_cumsum(z, cb):
    s = z.shape[0]
    ts = _divisor_tile(s, 512, LANES)

    def body(x_ref, col_ref, carry):
        @pl.when(pl.program_id(0) == 0)
        def _():
            carry[...] = jnp.zeros_like(carry)

        x = x_ref[...]
        lf = jnp.minimum(x, 0.0) - jnp.log1p(jnp.exp(-jnp.abs(x)))
        r = lax.broadcasted_iota(jnp.int32, (ts, ts), 0)
        c = lax.broadcasted_iota(jnp.int32, (ts, ts), 1)
        tri = jnp.where(c <= r, 1.0, 0.0).astype(F32)
        col_ref[...] = jnp.dot(tri, lf, preferred_element_type=F32, precision=lax.Precision.HIGHEST) + carry[...]
        carry[...] += jnp.sum(lf, axis=0, keepdims=True)

    return pl.pallas_call(
        body, grid=(s // ts,),
        in_specs=[pl.BlockSpec((ts, LANES), lambda i: (i, cb))],
        out_specs=pl.BlockSpec((ts, LANES), lambda i: (i, 0)),
        out_shape=jax.ShapeDtypeStruct((s, LANES), F32),
        scratch_shapes=[pltpu.VMEM((1, LANES), F32)],
        compiler_params=_cparams(("arbitrary",)), name="forget_cumsum")(z)


def _forget_cumsum_bwd(dq, dk, z, cb, nh):
    s = z.shape[0]
    ts = _divisor_tile(s, 512, LANES)
    nt = s // ts
    wd = nh * LANES

    def body(dq_ref, dk_ref, x_ref, o_ref, carry):
        @pl.when(pl.program_id(0) == 0)
        def _():
            carry[...] = jnp.zeros_like(carry)

        lane = _lane((ts, LANES))
        df = jnp.zeros((ts, LANES), F32)
        for h in range(nh):
            cols = slice(h * LANES, (h + 1) * LANES)
            d_h = _lane_column(dq_ref[:, cols], lane, FOX_Q_F) - _lane_column(dk_ref[:, cols], lane, FOX_Q_ONES)
            df = jnp.where(lane == h, d_h, df)
        r = lax.broadcasted_iota(jnp.int32, (ts, ts), 0)
        c = lax.broadcasted_iota(jnp.int32, (ts, ts), 1)
        tri = jnp.where(c >= r, 1.0, 0.0).astype(F32)
        rc = jnp.dot(tri, df, preferred_element_type=F32, precision=lax.Precision.HIGHEST) + carry[...]
        carry[...] += jnp.sum(df, axis=0, keepdims=True)
        o_ref[...] = (rc * (1.0 / (1.0 + jnp.exp(x_ref[...])))).astype(o_ref.dtype)

    return pl.pallas_call(
        body, grid=(nt,),
        in_specs=[pl.BlockSpec((ts, wd), lambda i: (nt - 1 - i, 0)),
                  pl.BlockSpec((ts, wd), lambda i: (nt - 1 - i, 0)),
                  pl.BlockSpec((ts, LANES), lambda i: (nt - 1 - i, cb))],
        out_specs=pl.BlockSpec((ts, LANES), lambda i: (nt - 1 - i, 0)),
        out_shape=jax.ShapeDtypeStruct((s, LANES), BF16),
        scratch_shapes=[pltpu.VMEM((1, LANES), F32)],
        compiler_params=_cparams(("arbitrary",)), name="forget_cumsum_bwd")(dq, dk, z)


def _flash_fwd(q, k, v, scale, nh, l_lane, *, name):
    s = q.shape[0]
    t = min(ATT_TILE, s)
    half = t // 2 if t % (2 * LANES) == 0 else t
    nq = s // t
    c = scale * LOG2E

    def body(q_ref, k_ref, v_ref, o_ref, qb_ref):
        i = pl.program_id(1)
        qb = q_ref[...]

        def scores(q_rows, k0, nk):
            kb = k_ref[pl.ds(pl.multiple_of(k0, half), nk), :]
            return lax.dot_general(q_rows, kb, _NT, preferred_element_type=F32)

        def update(sc, k0, nk, carry):
            m, acc = carry
            m_new = jnp.maximum(m, jnp.max(sc, axis=1, keepdims=True))
            p = jnp.exp2((sc - m_new) * c)
            alpha = jnp.exp2((m - m_new) * c)
            vb = v_ref[pl.ds(pl.multiple_of(k0, half), nk), :]
            acc = alpha * acc + jnp.dot(p.astype(BF16), vb, preferred_element_type=F32)
            return m_new, acc

        def full_block(j, cr):
            return update(scores(qb, j * t, t), j * t, t, cr)

        def group(jj, cr):
            for n in range(FWD_GROUP):
                cr = full_block(FWD_GROUP * jj + n, cr)
            return cr

        def causal(sc):
            row = lax.broadcasted_iota(jnp.int32, sc.shape, 0)
            col = lax.broadcasted_iota(jnp.int32, sc.shape, 1)
            return jnp.where(col <= row, sc, NEG)

        init = (jnp.full((t, 1), NEG, F32), jnp.zeros((t, LANES), F32))
        n_groups = i >> FWD_GROUP_LOG2
        carry = lax.fori_loop(0, n_groups, group, init)
        carry = lax.fori_loop(n_groups * FWD_GROUP, i, full_block, carry)
        m, acc = update(causal(scores(qb, i * t, t)), i * t, t, carry)
        lane = _lane((t, LANES))
        l = _lane_column(acc, lane, V_ONES)
        o_ref[...] = (acc / l).astype(o_ref.dtype)
        big_l = m + jnp.log(l) / scale
        qb_ref[...] = _put3(qb.astype(F32), lane, l_lane, _split3(-big_l)).astype(qb_ref.dtype)

    head_rows = pl.BlockSpec((t, LANES), lambda h, i: (i, h))
    head_all = pl.BlockSpec((s, LANES), lambda h, i: (0, h))
    return pl.pallas_call(
        body, grid=(nh, nq), in_specs=[head_rows, head_all, head_all], out_specs=[head_rows, head_rows],
        out_shape=[jax.ShapeDtypeStruct(q.shape, BF16), jax.ShapeDtypeStruct(q.shape, BF16)],
        compiler_params=_cparams(("parallel", "arbitrary")), name=name)(q, k, v)


def _flash_bwd(qb, k, v, do, scale, nh, n_feat, *, name):
    s = qb.shape[0]
    t = min(ATT_TILE, s)
    half = t // 2 if t % (2 * LANES) == 0 else t
    nq = s // t
    c = scale * LOG2E

    def body(q_ref, k_ref, v_ref, do_ref, dk_ref, dv_ref, dq_ref):
        j = pl.program_id(1)
        kb = k_ref[...]
        vb = v_ref[...]

        @pl.when(j == 0)
        def _():
            dq_ref[...] = jnp.zeros_like(dq_ref)

        def part(q0, n_q, n_k, carry, q_off):
            dk_acc, dv_acc = carry
            rows = pl.ds(pl.multiple_of(q0, half), n_q)
            qblk = q_ref[rows, :]
            dob = do_ref[rows, :]
            kbb, vbb = kb[:n_k], vb[:n_k]
            st = lax.dot_general(kbb, qblk, _NT, preferred_element_type=F32)
            if q_off is not None:
                key = lax.broadcasted_iota(jnp.int32, st.shape, 0)
                qry = lax.broadcasted_iota(jnp.int32, st.shape, 1) + q_off
                st = jnp.where(key <= qry, st, NEG)
            pt = jnp.exp2(st * c)
            dv_new = jnp.dot(pt.astype(BF16), dob, preferred_element_type=F32)
            dpt = lax.dot_general(vbb, dob, _NT, preferred_element_type=F32)
            dsb = (pt * dpt).astype(BF16)
            dk_new = jnp.dot(dsb, qblk, preferred_element_type=F32)
            dq_ref[rows, :] += lax.dot_general(dsb, kbb, _TN, preferred_element_type=F32)
            if n_k == t:
                return dk_acc + dk_new, dv_acc + dv_new
            return (jnp.concatenate([dk_acc[:n_k] + dk_new, dk_acc[n_k:]], axis=0),
                    jnp.concatenate([dv_acc[:n_k] + dv_new, dv_acc[n_k:]], axis=0))

        def block(i, carry):
            return part(i * t, t, t, carry, None)

        init = (jnp.zeros((t, LANES), F32), jnp.zeros((t, LANES), F32))
        carry = part(j * t + half, t - half, t, init, half) if half < t else part(j * t, t, t, init, 0)
        if half < t:
            carry = part(j * t, half, half, carry, 0)
        rest = nq - 1 - j
        carry = lax.cond((rest & 1) == 1, lambda cr: block(j + 1, cr), lambda cr: cr, carry)
        first = j + 1 + (rest & 1)

        def pair(ii, cr):
            i0 = first + 2 * ii
            return block(i0 + 1, block(i0, cr))

        dk_acc, dv_acc = lax.fori_loop(0, rest >> 1, pair, carry)
        dk_ref[...] = dk_acc * jnp.where(_lane((t, LANES)) < n_feat, scale, 1.0)
        dv_ref[...] = dv_acc.astype(dv_ref.dtype)

        @pl.when(j == nq - 1)
        def _():
            dq_ref[...] = dq_ref[...] * jnp.where(_lane((s, LANES)) < n_feat, scale, 1.0)

    head_rows = pl.BlockSpec((t, LANES), lambda h, j: (j, h))
    head_all = pl.BlockSpec((s, LANES), lambda h, j: (0, h))
    shp = jax.ShapeDtypeStruct(qb.shape, F32)
    return pl.pallas_call(
        body, grid=(nh, nq), in_specs=[head_all, head_rows, head_rows, head_all],
        out_specs=[head_rows, head_rows, head_all],
        out_shape=[shp, jax.ShapeDtypeStruct(qb.shape, BF16), shp],
        compiler_params=_cparams(("parallel", "arbitrary")), name=name)(qb, k, v, do)


def _with_delta(do, o, nh, *, name):
    def fn(ti, pa):
        lane = _lane((ti[0].shape[0], LANES))
        out = []
        for d_blk, o_blk in zip(_blocks(ti[0], nh), _blocks(ti[1], nh), strict=True):
            delta = jnp.sum(d_blk * o_blk.astype(F32), axis=1, keepdims=True)
            out.append(_put3(d_blk, lane, V_ONES, _split3(-delta)))
        return [jnp.concatenate(out, axis=1)], []

    (res,) = _rowwise(name, fn, [do, o], [], [(do.shape[1], BF16)])
    return res


def _pad_heads(a, nh):
    d = a.shape[-1] // nh
    a = a.reshape(a.shape[:-1] + (nh, d))
    a = jnp.pad(a, [(0, 0)] * (a.ndim - 1) + [(0, LANES - d)])
    return a.reshape(a.shape[:-2] + (nh * LANES,))


def _unpad_heads(a, nh, d):
    a = a.reshape(a.shape[:-1] + (nh, LANES))[..., :d]
    return a.reshape(a.shape[:-2] + (nh * d,))


def _pad_head_rows(w, nh):
    return _pad_heads(w.T, nh).T


def _unpad_head_rows(g, nh, d):
    return _unpad_heads(g.T, nh, d).T


class _ZLayout:
    def __init__(self, d):
        fw = FOX_HEADS * FOX_DIM
        self.src = {}
        off = 0
        for nm, w in (("cq", MLA_Q_LORA), ("ckv", MLA_KV_LORA), ("kr", MLA_ROPE), ("fq", fw), ("fk", fw),
                      ("fv", fw), ("fl", FOX_HEADS), ("ga", d), ("gb", d)):
            self.src[nm] = (off, w)
            off += w
        self.dst = {}
        off = 0
        for nm, w in (("ga", d), ("gb", d), ("fq", fw), ("fk", fw), ("fv", fw), ("cq", MLA_Q_LORA),
                      ("ckv", MLA_KV_LORA), ("kr", LANES), ("fl", LANES)):
            assert off % w == 0
            self.dst[nm] = (off, w)
            off += w
        self.width = off
        self.split = self.dst["cq"][0]
        assert all((o - self.split) % w == 0 for o, w in self.dst.values() if o >= self.split)

    def to_kernel(self, w):
        def seg(nm):
            off, wd = self.src[nm]
            return w[..., off:off + wd]

        def pad(a, left, total):
            return jnp.pad(a, [(0, 0)] * (a.ndim - 1) + [(left, total - left - a.shape[-1])])

        return jnp.concatenate([seg("ga"), seg("gb"), seg("fq"), seg("fk"), seg("fv"), seg("cq"), seg("ckv"),
                                pad(seg("kr"), MLA_NOPE, LANES), pad(seg("fl"), 0, LANES)], axis=-1)

    def from_kernel(self, g):
        def seg(nm, lo=0, hi=None):
            off, wd = self.dst[nm]
            return g[..., off + lo:off + (wd if hi is None else hi)]

        return jnp.concatenate([seg("cq"), seg("ckv"), seg("kr", MLA_NOPE, MLA_NOPE + MLA_ROPE), seg("fq"), seg("fk"),
                                seg("fv"), seg("fl", 0, FOX_HEADS), seg("ga"), seg("gb")], axis=-1)


def _local_step(x, positions, target, wts, late_weights, send_grads):
    s, d = x.shape
    zl = _ZLayout(d)
    hw = MLA_HEADS * LANES
    assert MLA_HEADS == FOX_HEADS
    scale_mla = (MLA_NOPE + MLA_ROPE) ** -0.5
    scale_fox = FOX_DIM ** -0.5

    inv_freq = ROPE_THETA ** (-jnp.arange(HALF_ROPE, dtype=F32) / HALF_ROPE)
    ang = positions.astype(F32)[:, None] * inv_freq
    cos, sin = jnp.cos(ang), jnp.sin(ang)
    tail = jnp.zeros((s, LANES - MLA_NOPE - MLA_ROPE), F32)
    rc = jnp.concatenate([jnp.ones((s, MLA_NOPE), F32), cos, cos, tail], axis=1)
    ra = jnp.concatenate([jnp.zeros((s, MLA_NOPE), F32), -sin, jnp.zeros((s, HALF_ROPE), F32), tail], axis=1)
    rb = jnp.concatenate([jnp.zeros((s, MLA_NOPE + HALF_ROPE), F32), sin, tail], axis=1)

    w_in = zl.to_kernel(wts["w_in"])
    b_in = zl.to_kernel(wts["b_in"])

    def f_norm_in(ti, pa):
        y, _ = _rms(ti[0], pa[0])
        return [y], []

    (h,) = _rowwise("norm_in", f_norm_in, [x], [wts["ln_pre_mix"]], [(d, BF16)])
    z_lo = _mm(h, w_in[:, :zl.split], bias=b_in[:, :zl.split], out_dtype=BF16, name="proj_in_lo")
    z = _mm(h, w_in[:, zl.split:], bias=b_in[:, zl.split:], out_dtype=F32, name="proj_in_hi")

    def zwin(nm):
        off, wd = zl.dst[nm]
        return (z_lo, off // wd, wd) if off < zl.split else (z, (off - zl.split) // wd, wd)

    wts = {**wts, **late_weights("mixer", z)}
    w_uq = _pad_heads(wts["w_uq"], MLA_HEADS)
    w_ukv = jnp.concatenate([_pad_heads(wts["w_uk"], MLA_HEADS), _pad_heads(wts["w_uv"], MLA_HEADS)], axis=1)
    w_o_mla = _pad_head_rows(wts["w_o_mla"], MLA_HEADS)
    w_o_fox = _pad_head_rows(wts["w_o_fox"], FOX_HEADS)

    def f_mla_norms(ti, pa):
        cqn, _ = _rms(ti[0], pa[0])
        ckvn, _ = _rms(ti[1], pa[1])
        return [cqn, ckvn], []

    cqn, ckvn = _rowwise("mla_norms", f_mla_norms, [zwin("cq"), zwin("ckv")],
                         [wts["q_a_norm"], wts["kv_a_norm"]], [(MLA_Q_LORA, BF16), (MLA_KV_LORA, BF16)])
    qf = _mm(cqn, w_uq, out_dtype=F32, name="proj_uq")
    kv = _mm(ckvn, w_ukv, out_dtype=BF16, name="proj_ukv")

    def f_rope_q(ti, pa):
        xq, c_, a_, b_ = ti
        return [jnp.concatenate([_rope_block(blk, c_, a_, b_) for blk in _blocks(xq, MLA_HEADS)], axis=1)], []

    (q_mla,) = _rowwise("rope_q", f_rope_q, [qf, rc, ra, rb], [], [(hw, BF16)])

    def f_mla_kv(ti, pa):
        kn, vn, kr, c_, a_, b_ = ti
        lane = _lane(kr.shape)
        k_tail = jnp.where((lane >= MLA_Q_L) & (lane < MLA_Q_L + 3), 1.0, _rope_block(kr, c_, a_, b_))
        ones_v = (lane >= V_ONES) & (lane < V_ONES + 3)
        k_out = [jnp.where(lane < MLA_NOPE, blk.astype(F32), k_tail) for blk in _blocks(kn, MLA_HEADS)]
        v_out = [jnp.where(ones_v, 1.0, blk.astype(F32)) for blk in _blocks(vn, MLA_HEADS)]
        return [jnp.concatenate(k_out, axis=1), jnp.concatenate(v_out, axis=1)], []

    k_mla, v_mla = _rowwise("mla_kv", f_mla_kv, [(kv, 0, hw), (kv, 1, hw), zwin("kr"), rc, ra, rb], [],
                            [(hw, BF16), (hw, BF16)])
    o_mla, qb_mla = _flash_fwd(q_mla, k_mla, v_mla, scale_mla, MLA_HEADS, MLA_Q_L, name="mla_fwd")

    fl_cb = zwin("fl")[1]
    fcol = _forget_cumsum(z, fl_cb)

    def f_fox_qkv(ti, pa):
        spread = _head_spread(FOX_HEADS, FOX_DIM)
        fq, fk, fv = (jnp.dot(a, spread, preferred_element_type=F32) for a in ti[:3])
        fc = ti[3]
        lane = _lane(fc.shape)
        ones_q = (lane >= FOX_Q_ONES) & (lane < FOX_Q_ONES + 3)
        ones_k = (lane >= FOX_Q_F) & (lane < FOX_Q_ONES)
        ones_v = (lane >= V_ONES) & (lane < V_ONES + 3)
        q_out, k_out, v_out = [], [], []
        for hh, (qblk, kblk, vblk) in enumerate(zip(_blocks(fq, FOX_HEADS), _blocks(fk, FOX_HEADS),
                                                    _blocks(fv, FOX_HEADS), strict=True)):
            f_h = _lane_column(fc, lane, hh) * (1.0 / scale_fox)
            q_out.append(_put3(jnp.where(ones_q, 1.0, qblk), lane, FOX_Q_F, _split3(f_h)))
            k_out.append(_put3(jnp.where(ones_k, 1.0, kblk), lane, FOX_Q_ONES, _split3(-f_h)))
            v_out.append(jnp.where(ones_v, 1.0, vblk))
        return [jnp.concatenate(q_out, axis=1), jnp.concatenate(k_out, axis=1), jnp.concatenate(v_out, axis=1)], []

    q_fox, k_fox, v_fox = _rowwise("fox_qkv", f_fox_qkv, [zwin("fq"), zwin("fk"), zwin("fv"), fcol],
                                   [], [(hw, BF16)] * 3)
    o_fox, qb_fox = _flash_fwd(q_fox, k_fox, v_fox, scale_fox, FOX_HEADS, FOX_Q_L, name="fox_fwd")

    y_mla = _mm(o_mla, w_o_mla, out_dtype=BF16, name="proj_o_mla")
    y_fox = _mm(o_fox, w_o_fox, out_dtype=BF16, name="proj_o_fox")

    def f_gate(ti, pa):
        ga, gb, ya, yb = (a.astype(F32) for a in ti)
        return [_sigmoid(ga) * ya + _sigmoid(gb) * yb], []

    (merged,) = _rowwise("gate", f_gate, [zwin("ga"), zwin("gb"), y_mla, y_fox], [], [(d, BF16)])
    wts = {**wts, **late_weights("mlp", merged)}
    mix = _mm(merged, wts["w_out"], out_dtype=F32, name="proj_out")

    def f_resid1(ti, pa):
        xa, mx = ti
        y, _ = _rms(mx, pa[0])
        x1 = xa + y
        h2, _ = _rms(x1, pa[1])
        return [x1, h2], []

    x1, h2 = _rowwise("resid_mix", f_resid1, [x, mix], [wts["ln_post_mix"], wts["ln_pre_mlp"]], [(d, F32), (d, BF16)])

    def relu2(acc, ex):
        r = jnp.maximum(acc, 0.0)
        return [acc, r * r]

    u, act = _mm(h2, wts["w_ff1"], out_dtype=[BF16, BF16], name="ff1", epilogue=relu2)
    mo = _mm(act, wts["w_ff2"], out_dtype=F32, name="ff2")

    def f_loss(ti, pa):
        xa, mv, tg = ti
        y, _ = _rms(mv, pa[0])
        err = (xa + y) - tg
        g2 = err / d
        dmo, dg = _rms_bwd(mv, pa[0], g2)
        return [g2, dmo], [jnp.sum(err * err, axis=0, keepdims=True), dg]

    g2, d_mo, loss_cols, g_ln_post_mlp = _rowwise("loss", f_loss, [x1, mo, target], [wts["ln_post_mlp"]],
                                                  [(d, F32), (d, BF16)], [d, d])
    loss = 0.5 * jnp.sum(loss_cols) / d

    grads = {"ln_post_mlp": g_ln_post_mlp}
    grads["w_ff2"] = _mm_tn(act, d_mo, name="grad_ff2")

    def relu2_bwd(acc, ex):
        return [acc * (2.0 * jnp.maximum(ex[0], 0.0))]

    (d_u,) = _mm(d_mo, wts["w_ff2"], out_dtype=[BF16], name="ff2_bwd", transpose_b=True, extras=[u], epilogue=relu2_bwd)
    grads["w_ff1"] = _mm_tn(h2, d_u, name="grad_ff1")
    d_h2 = _mm(d_u, wts["w_ff1"], out_dtype=F32, name="ff1_bwd", transpose_b=True)

    def f_resid1_bwd(ti, pa):
        gres, dh2, x1v, mx = ti
        dx1n, dg_pre_mlp = _rms_bwd(x1v, pa[1], dh2)
        dx1 = gres + dx1n
        dmix, dg_post_mix = _rms_bwd(mx, pa[0], dx1)
        return [dx1, dmix], [dg_post_mix, dg_pre_mlp]

    d_x1, d_mix, grads["ln_post_mix"], grads["ln_pre_mlp"] = _rowwise(
        "resid_mix_bwd", f_resid1_bwd, [g2, d_h2, x1, mix], [wts["ln_post_mix"], wts["ln_pre_mlp"]],
        [(d, F32), (d, BF16)], [d, d])
    grads["w_out"] = _mm_tn(merged, d_mix, name="grad_out")
    d_merged = _mm(d_mix, wts["w_out"], out_dtype=BF16, name="proj_out_bwd", transpose_b=True)

    def f_gate_bwd(ti, pa):
        dm, ga, gb, ya, yb = (a.astype(F32) for a in ti)
        sa, sb = _sigmoid(ga), _sigmoid(gb)
        return [dm * sa, dm * sb, dm * ya * (sa * (1.0 - sa)), dm * yb * (sb * (1.0 - sb))], []

    d_ya, d_yb, d_ga, d_gb = _rowwise("gate_bwd", f_gate_bwd,
                                      [d_merged, zwin("ga"), zwin("gb"), y_mla, y_fox], [],
                                      [(d, BF16)] * 4)
    grads["w_o_mla"] = _unpad_head_rows(_mm_tn(o_mla, d_ya, name="grad_o_mla"), MLA_HEADS, MLA_V)
    grads["w_o_fox"] = _unpad_head_rows(_mm_tn(o_fox, d_yb, name="grad_o_fox"), FOX_HEADS, FOX_DIM)
    w_o_mla = w_o_mla + send_grads("early", {nm: grads[nm] for nm in EARLY_GRADS}).astype(BF16)
    do_mla = _with_delta(_mm(d_ya, w_o_mla, out_dtype=F32, name="proj_o_mla_bwd", transpose_b=True), o_mla,
                         MLA_HEADS, name="mla_delta")
    do_fox = _with_delta(_mm(d_yb, w_o_fox, out_dtype=F32, name="proj_o_fox_bwd", transpose_b=True), o_fox,
                         FOX_HEADS, name="fox_delta")

    dk_mla, dv_mla, dq_mla = _flash_bwd(qb_mla, k_mla, v_mla, do_mla, scale_mla, MLA_HEADS, MLA_NOPE + MLA_ROPE,
                                        name="mla_bwd")
    dk_fox, dv_fox, dq_fox = _flash_bwd(qb_fox, k_fox, v_fox, do_fox, scale_fox, FOX_HEADS, FOX_DIM, name="fox_bwd")
    d_fl = _forget_cumsum_bwd(dq_fox, dk_fox, z, fl_cb, FOX_HEADS)

    def f_rope_q_bwd(ti, pa):
        g, c_, a_, b_ = ti
        return [jnp.concatenate([_rope_block(blk, c_, -a_, -b_) for blk in _blocks(g, MLA_HEADS)], axis=1)], []

    (d_qf,) = _rowwise("rope_q_bwd", f_rope_q_bwd, [dq_mla, rc, ra, rb], [], [(hw, BF16)])
    grads["w_uq"] = _unpad_heads(_mm_tn(cqn, d_qf, name="grad_uq"), MLA_HEADS, MLA_NOPE + MLA_ROPE)
    d_cqn = _mm(d_qf, w_uq, out_dtype=F32, name="proj_uq_bwd", transpose_b=True)

    def f_mla_kv_bwd(ti, pa):
        gk, gv, c_, a_, b_ = ti
        k_blocks = _blocks(gk, MLA_HEADS)
        tot = k_blocks[0]
        for blk in k_blocks[1:]:
            tot = tot + blk
        return [jnp.concatenate([gk, gv], axis=1), _rope_block(tot, c_, -a_, -b_)], []

    d_kv, d_kr = _rowwise("mla_kv_bwd", f_mla_kv_bwd, [dk_mla, dv_mla, rc, ra, rb], [], [(2 * hw, BF16), (LANES, BF16)])
    g_ukv = _mm_tn(ckvn, d_kv, name="grad_ukv")
    grads["w_uk"] = _unpad_heads(g_ukv[:, :hw], MLA_HEADS, MLA_NOPE)
    grads["w_uv"] = _unpad_heads(g_ukv[:, hw:], MLA_HEADS, MLA_V)
    d_ckvn = _mm(d_kv, w_ukv, out_dtype=F32, name="proj_ukv_bwd", transpose_b=True)

    def f_mla_norms_bwd(ti, pa):
        cq, ckv, dcqn, dckvn = ti
        dcq, dg_q = _rms_bwd(cq, pa[0], dcqn)
        dckv, dg_kv = _rms_bwd(ckv, pa[1], dckvn)
        return [dcq, dckv], [dg_q, dg_kv]

    d_cq, d_ckv, grads["q_a_norm"], grads["kv_a_norm"] = _rowwise(
        "mla_norms_bwd", f_mla_norms_bwd, [zwin("cq"), zwin("ckv"), d_cqn, d_ckvn],
        [wts["q_a_norm"], wts["kv_a_norm"]], [(MLA_Q_LORA, BF16), (MLA_KV_LORA, BF16)], [MLA_Q_LORA, MLA_KV_LORA])

    def f_fox_compact(ti, pa):
        gather = _head_spread(FOX_HEADS, FOX_DIM)
        return [lax.dot_general(a.astype(BF16), gather, _NT, preferred_element_type=F32) for a in ti], []

    d_fq, d_fk, d_fv = _rowwise("fox_compact", f_fox_compact, [dq_fox, dk_fox, dv_fox], [],
                                [(FOX_HEADS * FOX_DIM, BF16)] * 3)
    d_z = jnp.concatenate([d_ga, d_gb, d_fq, d_fk, d_fv, d_cq, d_ckv, d_kr, d_fl], axis=1)
    assert d_z.shape[1] == zl.width

    def f_bias(ti, pa):
        return [], [jnp.sum(ti[0].astype(F32), axis=0, keepdims=True)]

    (g_b_in,) = _rowwise("grad_b_in", f_bias, [d_z], [], [], [zl.width])
    grads["b_in"] = zl.from_kernel(g_b_in)
    grads["w_in"] = zl.from_kernel(_mm_tn(h, d_z, name="grad_in"))
    tok = send_grads("late", {nm: grads[nm] for nm, _ in BIG if nm not in EARLY_GRADS})
    d_h = _mm(d_z, w_in, bias=jnp.zeros((1, d), F32) + tok, out_dtype=F32, name="proj_in_bwd", transpose_b=True)

    def f_norm_in_bwd(ti, pa):
        dx1v, dh, xa = ti
        dxn, dg = _rms_bwd(xa, pa[0], dh)
        return [dx1v + dxn], [dg]

    grad_x, grads["ln_pre_mix"] = _rowwise("norm_in_bwd", f_norm_in_bwd, [d_x1, d_h, x], [wts["ln_pre_mix"]],
                                           [(d, F32)], [d])
    return loss, grad_x, grads


class _PackLayout:
    def __init__(self, shapes):
        self.shapes = list(shapes)
        self.width = _round_up(max(b for _, b in shapes), LANES)
        self.bands = []
        row = 0
        shelf = []
        for idx, (a, b) in enumerate(shapes):
            if 2 * _round_up(b, LANES) > self.width:
                self.bands.append((row, _round_up(a, 32), [(idx, 0)]))
                row += _round_up(a, 32)
            else:
                shelf.append(idx)
        col, items = 0, []
        for idx in shelf:
            wb = _round_up(shapes[idx][1], LANES)
            if col + wb > self.width:
                hgt = max(_round_up(shapes[i][0], 32) for i, _ in items)
                self.bands.append((row, hgt, items))
                row += hgt
                col, items = 0, []
            items.append((idx, col))
            col += wb
        if items:
            hgt = max(_round_up(shapes[i][0], 32) for i, _ in items)
            self.bands.append((row, hgt, items))
            row += hgt
        self.rows = _round_up(row, 16 * LOCAL_PIECES)

    def pack(self, arrs):
        lead = arrs[0].shape[:-2]
        no_pad = [(0, 0)] * len(lead)
        bands = []
        for _, hgt, items in self.bands:
            parts = []
            for k, (idx, col) in enumerate(items):
                a, b = self.shapes[idx]
                nxt = items[k + 1][1] if k + 1 < len(items) else self.width
                parts.append(jnp.pad(arrs[idx], no_pad + [(0, hgt - a), (0, nxt - col - b)]))
            bands.append(parts[0] if len(parts) == 1 else jnp.concatenate(parts, axis=-1))
        used = sum(hgt for _, hgt, _ in self.bands)
        if used < self.rows:
            bands.append(jnp.zeros(lead + (self.rows - used, self.width), arrs[0].dtype))
        return jnp.concatenate(bands, axis=-2)

    def unpack(self, packed):
        out = [None] * len(self.shapes)
        for row, _, items in self.bands:
            for idx, col in items:
                a, b = self.shapes[idx]
                out[idx] = packed[..., row:row + a, col:col + b]
        return out


def _to_shards(g, axis):
    if axis == 0:
        return g.reshape(N_CHIPS, g.shape[0] // N_CHIPS, g.shape[1])
    return jnp.stack(jnp.split(g, N_CHIPS, axis=1))


def _from_shards(s4, axis):
    n, a, b = s4.shape
    if axis == 0:
        return s4.reshape(n * a, b)
    return jnp.concatenate([s4[ch] for ch in range(n)], axis=1)


ANY = pl.BlockSpec(memory_space=pl.ANY)


def _place():
    return lax.axis_index("x"), lax.axis_index("y"), lax.axis_index("c")


def _gather_weights(wpk):
    rows, wd = wpk.shape
    half = rows // 2

    def body(w_ref, out_ref, send_sems, recv_sems, local_sems):
        x, y, c = _place()
        sibling = (x, y, 1 - c)
        chips = [(1 - x, y), (x, 1 - y), (1 - x, 1 - y)]

        def slab(chip, hf):
            return out_ref.at[2 * chip[0] + chip[1], pl.ds(hf * half, half), :]

        def copy(k, chip, hf, to, src=None):
            return pltpu.make_async_remote_copy(
                src_ref=slab(chip, hf) if src is None else src, dst_ref=slab(chip, hf),
                send_sem=send_sems.at[k], recv_sem=recv_sems.at[k], device_id=to, device_id_type=MESH)

        piece = rows // LOCAL_PIECES
        mine = [pltpu.make_async_copy(w_ref.at[pl.ds(n * piece, piece), :],
                                      out_ref.at[2 * x + y, pl.ds(n * piece, piece), :], local_sems.at[n])
                for n in range(LOCAL_PIECES)]
        for cp in mine:
            cp.start()
        first = [copy(j, (x, y), c, (*chip, c), src=w_ref.at[pl.ds(c * half, half), :]) for j, chip in enumerate(chips)]
        for cp in first:
            cp.start()
        passed = [copy(3 + j, chip, c, sibling) for j, chip in enumerate(chips)]
        for j, chip in enumerate(chips):
            copy(j, chip, c, (x, y, c)).wait_recv()
            passed[j].start()
        for j, chip in enumerate(chips):
            copy(3 + j, chip, 1 - c, (x, y, c)).wait_recv()
        for cp in first + passed:
            cp.wait_send()
        for cp in mine:
            cp.wait()

    assert rows % (16 * LOCAL_PIECES) == 0
    return pl.pallas_call(
        body, out_shape=jax.ShapeDtypeStruct((N_CHIPS, rows, wd), wpk.dtype),
        in_specs=[ANY], out_specs=ANY,
        scratch_shapes=[pltpu.SemaphoreType.DMA((6,)), pltpu.SemaphoreType.DMA((6,)),
                        pltpu.SemaphoreType.DMA((LOCAL_PIECES,))],
        name="gather_weights")(wpk)


HBM = pl.BlockSpec(memory_space=pltpu.HBM)
SEM = pl.BlockSpec(memory_space=pltpu.SEMAPHORE)
EFFECT = pltpu.SideEffectType.DATAFLOW_SIDE_EFFECTING
N_LATE = 6


def _gather_late_start(wpk, tag):
    rows, wd = wpk.shape
    half = rows // 2

    def body(w_ref, land_ref, send_sems, recv_sems, w_thru, land_thru, token):
        x, y, c = _place()
        chips = [(1 - x, y), (x, 1 - y), (1 - x, 1 - y)]
        for j, chip in enumerate(chips):
            for to_core in range(2):
                pltpu.make_async_remote_copy(
                    src_ref=w_ref.at[pl.ds(c * half, half), :],
                    dst_ref=land_ref.at[2 * x + y, pl.ds(c * half, half), :],
                    send_sem=send_sems.at[2 * j + to_core], recv_sem=recv_sems.at[2 * j + c],
                    device_id=(*chip, to_core), device_id_type=MESH).start()
        token[...] = jnp.zeros_like(token)

    land = lax.empty((N_CHIPS, rows, wd), wpk.dtype)
    return pl.pallas_call(
        body, name="gather_" + tag + "_start",
        out_shape=(pltpu.SemaphoreType.DMA((N_LATE,)), pltpu.SemaphoreType.DMA((N_LATE,)),
                   pltpu.HBM(wpk.shape, wpk.dtype), pltpu.HBM(land.shape, land.dtype),
                   jax.ShapeDtypeStruct((8, LANES), F32)),
        in_specs=(HBM, HBM), out_specs=(SEM, SEM, HBM, HBM, pl.BlockSpec(memory_space=pltpu.VMEM)),
        input_output_aliases={0: 2, 1: 3},
        compiler_params=pltpu.CompilerParams(has_side_effects=EFFECT),
    )(pltpu.with_memory_space_constraint(wpk, pltpu.HBM), pltpu.with_memory_space_constraint(land, pltpu.HBM))


def _gather_late_wait(send_sems, recv_sems, w_thru, land_thru, after, tag):
    rows, wd = w_thru.shape
    half = rows // 2

    def body(w_ref, land_ref, send_sems, recv_sems, after_ref, w_dead, land_out):
        x, y, c = _place()
        for n in range(N_LATE):
            cp = pltpu.make_async_remote_copy(
                src_ref=w_ref.at[pl.ds(0, half), :], dst_ref=land_ref.at[0, pl.ds(0, half), :],
                send_sem=send_sems.at[n], recv_sem=recv_sems.at[n], device_id=(x, y, c), device_id_type=MESH)
            cp.wait_send()
            cp.wait_recv()

    return pl.pallas_call(
        body, name="gather_" + tag + "_wait",
        out_shape=(pltpu.HBM(w_thru.shape, w_thru.dtype), pltpu.HBM(land_thru.shape, land_thru.dtype)),
        in_specs=(HBM, HBM, SEM, SEM, ANY), out_specs=(HBM, HBM), input_output_aliases={0: 0, 1: 1},
        compiler_params=pltpu.CompilerParams(has_side_effects=EFFECT),
    )(w_thru, land_thru, send_sems, recv_sems, after)[1]


N_PART = 7


def _reduce_start(gbf, tag):
    _, _, hrows, wd = gbf.shape

    def body(g_ref, land_ref, send_sems, recv_sems, g_thru, land_thru, token):
        x, y, c = _place()
        chips = [(1 - x, y), (x, 1 - y), (1 - x, 1 - y)]
        for j, chip in enumerate(chips):
            for to_core in range(2):
                pltpu.make_async_remote_copy(
                    src_ref=g_ref.at[2 * chip[0] + chip[1], to_core], dst_ref=land_ref.at[2 * j + c],
                    send_sem=send_sems.at[2 * j + to_core], recv_sem=recv_sems.at[2 * j + c],
                    device_id=(*chip, to_core), device_id_type=MESH).start()
        pltpu.make_async_remote_copy(
            src_ref=g_ref.at[2 * x + y, 1 - c], dst_ref=land_ref.at[N_PART - 1],
            send_sem=send_sems.at[N_PART - 1], recv_sem=recv_sems.at[N_PART - 1],
            device_id=(x, y, 1 - c), device_id_type=MESH).start()
        token[...] = jnp.zeros_like(token)

    land = lax.empty((N_PART, hrows, wd), gbf.dtype)
    return pl.pallas_call(
        body, name="reduce_" + tag + "_start",
        out_shape=(pltpu.SemaphoreType.DMA((N_PART,)), pltpu.SemaphoreType.DMA((N_PART,)),
                   pltpu.HBM(gbf.shape, gbf.dtype), pltpu.HBM(land.shape, land.dtype),
                   jax.ShapeDtypeStruct((8, LANES), F32)),
        in_specs=(HBM, HBM), out_specs=(SEM, SEM, HBM, HBM, pl.BlockSpec(memory_space=pltpu.VMEM)),
        input_output_aliases={0: 2, 1: 3},
        compiler_params=pltpu.CompilerParams(has_side_effects=EFFECT),
    )(pltpu.with_memory_space_constraint(gbf, pltpu.HBM), pltpu.with_memory_space_constraint(land, pltpu.HBM))


def _reduce_wait(send_sems, recv_sems, g_thru, land_thru, after, tag):
    def body(g_ref, land_ref, send_sems, recv_sems, after_ref, g_dead, land_out):
        x, y, c = _place()
        for n in range(N_PART):
            cp = pltpu.make_async_remote_copy(
                src_ref=g_ref.at[0, 0], dst_ref=land_ref.at[0], send_sem=send_sems.at[n], recv_sem=recv_sems.at[n],
                device_id=(x, y, c), device_id_type=MESH)
            cp.wait_send()
            cp.wait_recv()

    return pl.pallas_call(
        body, name="reduce_" + tag + "_wait",
        out_shape=(pltpu.HBM(g_thru.shape, g_thru.dtype), pltpu.HBM(land_thru.shape, land_thru.dtype)),
        in_specs=(HBM, HBM, SEM, SEM, ANY), out_specs=(HBM, HBM), input_output_aliases={0: 0, 1: 1},
        compiler_params=pltpu.CompilerParams(has_side_effects=EFFECT),
    )(g_thru, land_thru, send_sems, recv_sems, after)[1]


def _sibling_swap(mine):
    def body(m_ref, out_ref, send_sem, recv_sem):
        x, y, c = _place()
        cp = pltpu.make_async_remote_copy(
            src_ref=m_ref, dst_ref=out_ref, send_sem=send_sem, recv_sem=recv_sem,
            device_id=(x, y, 1 - c), device_id_type=MESH)
        cp.start()
        cp.wait()

    return pl.pallas_call(
        body, out_shape=jax.ShapeDtypeStruct(mine.shape, mine.dtype), in_specs=[ANY], out_specs=ANY,
        scratch_shapes=[pltpu.SemaphoreType.DMA, pltpu.SemaphoreType.DMA], name="grad_sibling_swap")(mine)


def _adamw(w, g, m, v):
    m = ADAM_B1 * m + (1.0 - ADAM_B1) * g
    v = ADAM_B2 * v + (1.0 - ADAM_B2) * (g * g)
    m_hat = m / (1.0 - ADAM_B1 ** ADAM_STEP)
    v_hat = v / (1.0 - ADAM_B2 ** ADAM_STEP)
    delta = -ADAM_LR * (m_hat / (jnp.sqrt(v_hat) + ADAM_EPS) + ADAM_WD * w)
    return delta, m, v


def _small_allreduce_adamw(gs, ws, ms, vs):
    n_dev = 8
    n_par = len(gs)
    wd = PACK_W
    chunks = []
    for p, g in enumerate(gs):
        for off in range(0, g.shape[1], wd):
            chunks.append((p, len(chunks), off, min(wd, g.shape[1] - off)))
    rows = _round_up(len(chunks), 8)

    def body(*refs):
        g_refs, w_refs, m_refs, v_refs = (refs[k * n_par:(k + 1) * n_par] for k in range(4))
        go_refs, d_refs, mo_refs, vo_refs = (refs[(4 + k) * n_par:(5 + k) * n_par] for k in range(4))
        mine_ref, all_ref, send_sems, recv_sems = refs[8 * n_par:]
        x, y, c = _place()
        me, sibling = (x, y, c), (x, y, 1 - c)
        chips = [(1 - x, y), (x, 1 - y), (1 - x, 1 - y)]

        def slot(px, py, pc):
            return all_ref.at[4 * px + 2 * py + pc]

        def copy(k, block, to, src=None):
            return pltpu.make_async_remote_copy(
                src_ref=slot(*block) if src is None else src, dst_ref=slot(*block),
                send_sem=send_sems.at[k], recv_sem=recv_sems.at[k], device_id=to, device_id_type=MESH)

        mine_ref[...] = jnp.zeros_like(mine_ref)
        for p, row, off, width in chunks:
            mine_ref[row:row + 1, 0:width] = g_refs[p][:, off:off + width]
        all_ref[4 * x + 2 * y + c] = mine_ref[...]
        first = [copy(0, me, sibling, src=mine_ref)]
        first += [copy(1 + j, me, (*chip, c), src=mine_ref) for j, chip in enumerate(chips)]
        for cp in first:
            cp.start()
        passed = [copy(4 + j, (*chip, c), sibling) for j, chip in enumerate(chips)]
        for j, chip in enumerate(chips):
            copy(1 + j, (*chip, c), me).wait_recv()
            passed[j].start()
        copy(0, sibling, me).wait_recv()
        for j, chip in enumerate(chips):
            copy(4 + j, (*chip, 1 - c), me).wait_recv()
        for cp in first + passed:
            cp.wait_send()
        tot = jnp.zeros((rows, wd), F32)
        for dev in range(n_dev):
            tot = tot + all_ref[dev]
        mine_ref[...] = tot
        for p, row, off, width in chunks:
            cols = slice(off, off + width)
            g = mine_ref[row:row + 1, 0:width]
            delta, m_new, v_new = _adamw(w_refs[p][:, cols], g, m_refs[p][:, cols], v_refs[p][:, cols])
            go_refs[p][:, cols] = g
            d_refs[p][:, cols] = delta
            mo_refs[p][:, cols] = m_new
            vo_refs[p][:, cols] = v_new

    vm = pl.BlockSpec(memory_space=pltpu.VMEM)
    shp = [jax.ShapeDtypeStruct(g.shape, F32) for g in gs]
    res = pl.pallas_call(
        body, out_shape=shp * 4, in_specs=[vm] * (4 * n_par), out_specs=[vm] * (4 * n_par),
        scratch_shapes=[pltpu.VMEM((rows, wd), F32), pltpu.VMEM((n_dev, rows, wd), F32),
                        pltpu.SemaphoreType.DMA((7,)), pltpu.SemaphoreType.DMA((7,))],
        name="small_allreduce_adamw")(*gs, *ws, *ms, *vs)
    return [res[k * n_par:(k + 1) * n_par] for k in range(4)]


def kernel(x, positions, ln_pre_mix, ln_post_mix, ln_pre_mlp, ln_post_mlp, w_in, b_in, q_a_norm, w_uq, kv_a_norm, w_uk, w_uv, w_o_mla, w_o_fox, w_out, w_ff1, w_ff2, loss_target, m_ln_pre_mix, m_ln_post_mix, m_ln_pre_mlp, m_ln_post_mlp, m_w_in, m_b_in, m_q_a_norm, m_w_uq, m_kv_a_norm, m_w_uk, m_w_uv, m_w_o_mla, m_w_o_fox, m_w_out, m_w_ff1, m_w_ff2, v_ln_pre_mix, v_ln_post_mix, v_ln_pre_mlp, v_ln_post_mlp, v_w_in, v_b_in, v_q_a_norm, v_w_uq, v_kv_a_norm, v_w_uk, v_w_uv, v_w_o_mla, v_w_o_fox, v_w_out, v_w_ff1, v_w_ff2):
    w = dict(ln_pre_mix=ln_pre_mix, ln_post_mix=ln_post_mix, ln_pre_mlp=ln_pre_mlp, ln_post_mlp=ln_post_mlp, w_in=w_in,
             b_in=b_in, q_a_norm=q_a_norm, w_uq=w_uq, kv_a_norm=kv_a_norm, w_uk=w_uk, w_uv=w_uv, w_o_mla=w_o_mla,
             w_o_fox=w_o_fox, w_out=w_out, w_ff1=w_ff1, w_ff2=w_ff2)
    mom = dict(ln_pre_mix=m_ln_pre_mix, ln_post_mix=m_ln_post_mix, ln_pre_mlp=m_ln_pre_mlp, ln_post_mlp=m_ln_post_mlp,
               w_in=m_w_in, b_in=m_b_in, q_a_norm=m_q_a_norm, w_uq=m_w_uq, kv_a_norm=m_kv_a_norm, w_uk=m_w_uk,
               w_uv=m_w_uv, w_o_mla=m_w_o_mla, w_o_fox=m_w_o_fox, w_out=m_w_out, w_ff1=m_w_ff1, w_ff2=m_w_ff2)
    var = dict(ln_pre_mix=v_ln_pre_mix, ln_post_mix=v_ln_post_mix, ln_pre_mlp=v_ln_pre_mlp, ln_post_mlp=v_ln_post_mlp,
               w_in=v_w_in, b_in=v_b_in, q_a_norm=v_q_a_norm, w_uq=v_w_uq, kv_a_norm=v_kv_a_norm, w_uk=v_w_uk,
               w_uv=v_w_uv, w_o_mla=v_w_o_mla, w_o_fox=v_w_o_fox, w_out=v_w_out, w_ff1=v_w_ff1, w_ff2=v_w_ff2)

    big_names = [nm for nm, _ in BIG]
    c = lax.axis_index("c")
    chip = 2 * lax.axis_index("x") + lax.axis_index("y")

    axes = dict(BIG)

    def assemble(names, lay, gathered):
        return {nm: _from_shards(s4, axes[nm] - 1) for nm, s4 in zip(names, lay.unpack(gathered), strict=True)}

    groups = {"mixer": [nm for nm in big_names if nm in MIXER], "mlp": [nm for nm in big_names if nm in LATE]}
    first_names = [nm for nm in big_names if nm not in MIXER and nm not in LATE]
    full = {nm: wv for nm, wv in w.items() if nm in SMALL}
    lay_first = _PackLayout([w[nm].shape[1:] for nm in first_names])
    first = _gather_weights(lay_first.pack([w[nm][0].astype(BF16) for nm in first_names]))
    full.update(assemble(first_names, lay_first, first))
    travelling = {}
    for tag, names in groups.items():
        lay = _PackLayout([w[nm].shape[1:] for nm in names])
        shard = lay.pack([w[nm][0].astype(BF16) for nm in names])
        started = _gather_late_start(lax.optimization_barrier((shard, first))[0], tag)
        travelling[tag] = (names, lay, shard, started[:4])
        full["b_in"] = full["b_in"] + started[4][0, 0]

    def late_weights(tag, after):
        names, lay, shard, handles = travelling[tag]
        land = _gather_late_wait(*handles, after, tag)
        land = lax.dynamic_update_slice(land, shard[None], (chip, 0, 0))
        return assemble(names, lay, land)

    grad_groups = {"early": [nm for nm in big_names if nm in EARLY_GRADS],
                   "late": [nm for nm in big_names if nm not in EARLY_GRADS]}
    sent = {}

    def send_grads(tag, g):
        names = grad_groups[tag]
        lay = _PackLayout([w[nm].shape[1:] for nm in names])
        hrows = lay.rows // 2
        gbf = lay.pack([_to_shards(g[nm], axes[nm] - 1).astype(BF16) for nm in names])
        gbf = gbf.reshape(N_CHIPS, 2, hrows, lay.width)
        started = _reduce_start(gbf, tag)
        own = lax.dynamic_index_in_dim(lax.dynamic_index_in_dim(started[2], chip, axis=0, keepdims=False), c, axis=0,
                                       keepdims=False)
        sent[tag] = (names, lay, started[:4], own)
        return started[4][0, 0]

    loss_local, grad_x, grads = _local_step(x[0], positions[0], loss_target[0], full, late_weights, send_grads)

    def f_add8(ti, pa):
        tot = ti[0].astype(F32)
        for part in ti[1:]:
            tot = tot + part.astype(F32)
        return [tot], []

    reduced = []
    for tag, (names, lay, handles, own) in sent.items():
        parts = _reduce_wait(*handles, grad_x, tag)
        reduced.append(_rowwise("grad_add_" + tag, f_add8, [own] + [parts[n] for n in range(N_PART)], [],
                                [(lay.width, F32)])[0])

    assert len({lay.width for _, lay, _, _ in sent.values()}) == 1
    red = jnp.concatenate(reduced, axis=0)
    sib = _sibling_swap(red)
    lower, upper = jnp.where(c == 0, red, sib), jnp.where(c == 0, sib, red)
    g_by_name, row = {}, 0
    for names, lay, _, _ in sent.values():
        hrows = lay.rows // 2
        both = jnp.concatenate([lower[row:row + hrows], upper[row:row + hrows]], axis=0)
        g_by_name.update(zip(names, lay.unpack(both), strict=True))
        row += hrows
    g_shards = [g_by_name[nm] for nm in big_names]

    def f_adamw(ti, pa):
        wv, gv, mv, vv = ti
        return list(_adamw(wv, gv, mv, vv)), []

    out = {"grad": {}, "delta": {}, "m": {}, "v": {}}
    for nm, g_sh in zip(big_names, g_shards, strict=True):
        wd = g_sh.shape[1]
        d_sh, m_sh, v_sh = _rowwise("adamw_" + nm, f_adamw, [w[nm][0], g_sh, mom[nm][0], var[nm][0]], [], [(wd, F32)] * 3)
        out["grad"][nm], out["delta"][nm], out["m"][nm], out["v"][nm] = g_sh[None], d_sh[None], m_sh[None], v_sh[None]

    loss_row = jnp.zeros((1, LANES), F32) + loss_local
    blank = jnp.zeros((1, LANES), F32)
    small = _small_allreduce_adamw([grads[nm] for nm in SMALL] + [loss_row], [w[nm] for nm in SMALL] + [blank],
                                   [mom[nm] for nm in SMALL] + [blank], [var[nm] for nm in SMALL] + [blank])
    for kind, arrs in zip(("grad", "delta", "m", "v"), small, strict=True):
        for nm, arr in zip(SMALL, arrs[:len(SMALL)], strict=True):
            out[kind][nm] = arr
    loss = small[0][len(SMALL)][0, 0]

    return (loss, grad_x[None], *[out["grad"][nm] for nm in ALL_W], *[out["delta"][nm] for nm in ALL_W],
            *[out["m"][nm] for nm in ALL_W], *[out["v"][nm] for nm in ALL_W])
```

```python
import functools
import math

import jax
import jax.numpy as jnp
from jax import lax
from jax.experimental import pallas as pl
from jax.experimental.pallas import tpu as pltpu

F32 = jnp.float32
BF16 = jnp.bfloat16

MLA_HEADS = 8
MLA_Q_LORA = 256
MLA_KV_LORA = 128
MLA_NOPE = 64
MLA_ROPE = 32
MLA_V = 64
FOX_HEADS = 8
FOX_DIM = 64
ROPE_THETA = 10000.0
NORM_EPS = 1e-6
HALF_ROPE = MLA_ROPE // 2

ADAM_LR = 0.001
ADAM_B1 = 0.9
ADAM_B2 = 0.999
ADAM_EPS = 1e-08
ADAM_WD = 0.01
ADAM_STEP = 10

LANES = 128
VMEM_LIMIT = 56 * 1024 * 1024
ATT_TILE = 1024
FWD_GROUP_LOG2 = 1
FWD_GROUP = 1 << FWD_GROUP_LOG2
MM_VMEM_BUDGET = 40 * 1024 * 1024
MXU_WIDTH = 256
MXU_MACS_PER_S = 4.98e14
HBM_BYTES_PER_S = 3.2e12
STEP_OVERHEAD_S = 0.35e-6
NEG = -1e30
LOG2E = math.log2(math.e)
MESH = pl.DeviceIdType.MESH

V_ONES = 64
FOX_Q_F = 64
FOX_Q_L = 67
FOX_Q_ONES = 70
MLA_Q_L = 96

BIG = (("w_in", 2), ("w_uq", 2), ("w_uk", 2), ("w_uv", 2), ("w_o_mla", 2), ("w_o_fox", 2),
       ("w_out", 1), ("w_ff1", 2), ("w_ff2", 1))
MIXER = ("w_uq", "w_uk", "w_uv", "w_o_mla", "w_o_fox")
LATE = ("w_out", "w_ff1", "w_ff2")
EARLY_GRADS = ("w_o_mla", "w_o_fox", "w_out", "w_ff1", "w_ff2")
SMALL = ("ln_pre_mix", "ln_post_mix", "ln_pre_mlp", "ln_post_mlp", "b_in", "q_a_norm", "kv_a_norm")
ALL_W = ("ln_pre_mix", "ln_post_mix", "ln_pre_mlp", "ln_post_mlp", "w_in", "b_in", "q_a_norm", "w_uq",
         "kv_a_norm", "w_uk", "w_uv", "w_o_mla", "w_o_fox", "w_out", "w_ff1", "w_ff2")
N_CHIPS = 4
PACK_W = 1024
LOCAL_PIECES = 8

_NT = (((1,), (1,)), ((), ()))
_TN = (((0,), (0,)), ((), ()))


def _cparams(sem=None):
    return pltpu.CompilerParams(dimension_semantics=sem, vmem_limit_bytes=VMEM_LIMIT)


def _divisor_tile(n, limit, mult):
    if n <= limit:
        return n
    best = None
    t = mult
    while t <= limit:
        if n % t == 0:
            best = t
        t += mult
    assert best is not None, (n, limit, mult)
    return best


def _round_up(v, mult):
    return -(-v // mult) * mult


def _mm_tiles(m, k, n, io_bytes):
    best = None
    for tm in (2048, 1024, 512, 256, 128):
        if m % tm:
            continue
        for tn in range(LANES, min(n, 2048) + 1, LANES):
            if n % tn:
                continue
            vmem = 2 * (tm * k * 2 + k * tn * 2 + tm * tn * io_bytes) + tm * tn * 4
            if vmem > MM_VMEM_BUDGET:
                continue
            mxu = m * k * n * (_round_up(tn, MXU_WIDTH) / tn) / MXU_MACS_PER_S
            hbm = (m * k * 2 + (m // tm) * k * n * 2 + m * n * io_bytes) / HBM_BYTES_PER_S
            cost = max(mxu, hbm) + (m // tm) * (n // tn) * STEP_OVERHEAD_S
            if best is None or cost < best[0]:
                best = (cost, tm, tn)
    assert best is not None, (m, k, n)
    return best[1], best[2]


def _mm(a, b, *, out_dtype, name, bias=None, transpose_b=False, extras=(), epilogue=None):
    m, k = a.shape
    n = b.shape[0] if transpose_b else b.shape[1]
    assert (b.shape[1] if transpose_b else b.shape[0]) == k and a.dtype == BF16 and b.dtype == BF16
    out_dtypes = list(out_dtype) if isinstance(out_dtype, (list, tuple)) else [out_dtype]
    n_ex = len(extras)
    tm, tn = _mm_tiles(m, k, n, sum(jnp.dtype(dt).itemsize for dt in out_dtypes) + 4 * n_ex)

    def body(*refs):
        a_ref, b_ref = refs[:2]
        pos = 2
        bias_ref = None
        if bias is not None:
            bias_ref = refs[pos]
            pos += 1
        ex_refs = refs[pos:pos + n_ex]
        o_refs = refs[pos + n_ex:]
        if transpose_b:
            acc = lax.dot_general(a_ref[...], b_ref[...], _NT, preferred_element_type=F32)
        else:
            acc = jnp.dot(a_ref[...], b_ref[...], preferred_element_type=F32)
        if bias_ref is not None:
            acc = acc + bias_ref[...]
        vals = [acc] if epilogue is None else epilogue(acc, [r[...] for r in ex_refs])
        for ref, val in zip(o_refs, vals, strict=True):
            ref[...] = val.astype(ref.dtype)

    b_spec = pl.BlockSpec((tn, k), lambda i, j: (j, 0)) if transpose_b else pl.BlockSpec((k, tn), lambda i, j: (0, j))
    in_specs = [pl.BlockSpec((tm, k), lambda i, j: (i, 0)), b_spec]
    args = [a, b]
    if bias is not None:
        in_specs.append(pl.BlockSpec((1, tn), lambda i, j: (0, j)))
        args.append(bias)
    in_specs += [pl.BlockSpec((tm, tn), lambda i, j: (i, j)) for _ in extras]
    args += list(extras)
    res = pl.pallas_call(
        body, grid=(m // tm, n // tn), in_specs=in_specs,
        out_specs=[pl.BlockSpec((tm, tn), lambda i, j: (i, j)) for _ in out_dtypes],
        out_shape=[jax.ShapeDtypeStruct((m, n), dt) for dt in out_dtypes],
        compiler_params=_cparams(("parallel", "parallel")), name=name)(*args)
    return res if isinstance(out_dtype, (list, tuple)) else res[0]


def _mm_tn(a, b, *, name):
    s, m = a.shape
    s2, n = b.shape
    assert s == s2 and a.dtype == BF16 and b.dtype == BF16
    tm = _divisor_tile(m, 1024, LANES)
    tn = _divisor_tile(n, 2304, LANES)
    tk = _divisor_tile(s, 512, 16)
    nk = s // tk

    def body(a_ref, b_ref, o_ref, acc_ref):
        @pl.when(pl.program_id(2) == 0)
        def _():
            acc_ref[...] = jnp.zeros_like(acc_ref)

        acc_ref[...] += lax.dot_general(a_ref[...], b_ref[...], _TN, preferred_element_type=F32)

        @pl.when(pl.program_id(2) == nk - 1)
        def _():
            o_ref[...] = acc_ref[...].astype(o_ref.dtype)

    return pl.pallas_call(
        body, grid=(m // tm, n // tn, nk),
        in_specs=[pl.BlockSpec((tk, tm), lambda i, j, k: (k, i)), pl.BlockSpec((tk, tn), lambda i, j, k: (k, j))],
        out_specs=pl.BlockSpec((tm, tn), lambda i, j, k: (i, j)),
        out_shape=jax.ShapeDtypeStruct((m, n), BF16),
        scratch_shapes=[pltpu.VMEM((tm, tn), F32)],
        compiler_params=_cparams(("parallel", "parallel", "arbitrary")), name=name)(a, b)


def _rowwise(name, fn, tiled, params, outs, reds=(), reverse=False):
    wins = [t if isinstance(t, tuple) else (t, 0, t.shape[1]) for t in tiled]
    s = wins[0][0].shape[0]
    row_bytes = sum(w * arr.dtype.itemsize for arr, _, w in wins) + sum(w * jnp.dtype(d).itemsize for w, d in outs)
    ts = _divisor_tile(s, max(16, min(1024, (6 * 1024 * 1024) // row_bytes)), 16)
    nt, npar, nout = len(wins), len(params), len(outs)
    n_tiles = s // ts

    def row(i):
        return n_tiles - 1 - i if reverse else i

    def body(*refs):
        tin = [r[...] for r in refs[:nt]]
        par = [r[...] for r in refs[nt:nt + npar]]
        out_refs = refs[nt + npar:nt + npar + nout]
        red_refs = refs[nt + npar + nout:]
        o, r = fn(tin, par)
        for ref, val in zip(out_refs, o, strict=True):
            ref[...] = val.astype(ref.dtype)
        if red_refs:
            @pl.when(pl.program_id(0) == 0)
            def _():
                for ref in red_refs:
                    ref[...] = jnp.zeros_like(ref)

            for ref, val in zip(red_refs, r, strict=True):
                ref[...] += val

    in_specs = [pl.BlockSpec((ts, w), functools.partial(lambda i, cb: (row(i), cb), cb=cb)) for _, cb, w in wins]
    in_specs += [pl.BlockSpec(p.shape, lambda i: (0, 0)) for p in params]
    out_specs = [pl.BlockSpec((ts, w), lambda i: (row(i), 0)) for w, _ in outs]
    out_specs += [pl.BlockSpec((1, w), lambda i: (0, 0)) for w in reds]
    out_shape = [jax.ShapeDtypeStruct((s, w), d) for w, d in outs]
    out_shape += [jax.ShapeDtypeStruct((1, w), F32) for w in reds]
    return pl.pallas_call(
        body, grid=(n_tiles,), in_specs=in_specs, out_specs=out_specs, out_shape=out_shape,
        compiler_params=_cparams(("arbitrary",)), name=name)(*[w[0] for w in wins], *params)


def _rms(x, g):
    r = lax.rsqrt(jnp.mean(x * x, axis=-1, keepdims=True) + NORM_EPS)
    return x * r * g, r


def _rms_bwd(x, g, dy):
    r = lax.rsqrt(jnp.mean(x * x, axis=-1, keepdims=True) + NORM_EPS)
    gy = dy * g
    dx = r * gy - x * (r * r * r) * jnp.mean(x * gy, axis=-1, keepdims=True)
    dg = jnp.sum(dy * (x * r), axis=0, keepdims=True)
    return dx, dg


def _sigmoid(x):
    return 1.0 / (1.0 + jnp.exp(-x))


def _split3(x):
    hi = x.astype(BF16).astype(F32)
    r = x - hi
    mid = r.astype(BF16).astype(F32)
    lo = (r - mid).astype(BF16).astype(F32)
    return hi, mid, lo


def _lane(shape):
    return lax.broadcasted_iota(jnp.int32, shape, 1)


def _put3(blk, lane, pos, pieces):
    for k, piece in enumerate(pieces):
        blk = jnp.where(lane == pos + k, piece, blk)
    return blk


def _lane_column(blk, lane, pos):
    return jnp.sum(jnp.where(lane == pos, blk, 0.0), axis=1, keepdims=True)


def _blocks(a, nh):
    return [a[:, h * LANES:(h + 1) * LANES] for h in range(nh)]


def _head_spread(nh, d):
    shift = d.bit_length() - 1
    assert 1 << shift == d
    r = lax.broadcasted_iota(jnp.int32, (nh * d, nh * LANES), 0)
    c = lax.broadcasted_iota(jnp.int32, (nh * d, nh * LANES), 1)
    return jnp.where(c == (r >> shift) * LANES + (r & (d - 1)), 1.0, 0.0).astype(BF16)


def _rope_block(x, c, sa, sb):
    return x * c + pltpu.roll(x, LANES - HALF_ROPE, 1) * sa + pltpu.roll(x, HALF_ROPE, 1) * sb


def _forget_cumsum(z, cb):
    s = z.shape[0]
    ts = _divisor_tile(s, 512, LANES)

    def body(x_ref, col_ref, carry):
        @pl.when(pl.program_id(0) == 0)
        def _():
            carry[...] = jnp.zeros_like(carry)

        x = x_ref[...]
        lf = jnp.minimum(x, 0.0) - jnp.log1p(jnp.exp(-jnp.abs(x)))
        r = lax.broadcasted_iota(jnp.int32, (ts, ts), 0)
        c = lax.broadcasted_iota(jnp.int32, (ts, ts), 1)
        tri = jnp.where(c <= r, 1.0, 0.0).astype(F32)
        col_ref[...] = jnp.dot(tri, lf, preferred_element_type=F32, precision=lax.Precision.HIGHEST) + carry[...]
        carry[...] += jnp.sum(lf, axis=0, keepdims=True)

    return pl.pallas_call(
        body, grid=(s // ts,),
        in_specs=[pl.BlockSpec((ts, LANES), lambda i: (i, cb))],
        out_specs=pl.BlockSpec((ts, LANES), lambda i: (i, 0)),
        out_shape=jax.ShapeDtypeStruct((s, LANES), F32),
        scratch_shapes=[pltpu.VMEM((1, LANES), F32)],
        compiler_params=_cparams(("arbitrary",)), name="forget_cumsum")(z)


def _forget_cumsum_bwd(dq, dk, z, cb, nh):
    s = z.shape[0]
    ts = _divisor_tile(s, 512, LANES)
    nt = s // ts
    wd = nh * LANES

    def body(dq_ref, dk_ref, x_ref, o_ref, carry):
        @pl.when(pl.program_id(0) == 0)
        def _():
            carry[...] = jnp.zeros_like(carry)

        lane = _lane((ts, LANES))
        df = jnp.zeros((ts, LANES), F32)
        for h in range(nh):
            cols = slice(h * LANES, (h + 1) * LANES)
            d_h = _lane_column(dq_ref[:, cols], lane, FOX_Q_F) - _lane_column(dk_ref[:, cols], lane, FOX_Q_ONES)
            df = jnp.where(lane == h, d_h, df)
        r = lax.broadcasted_iota(jnp.int32, (ts, ts), 0)
        c = lax.broadcasted_iota(jnp.int32, (ts, ts), 1)
        tri = jnp.where(c >= r, 1.0, 0.0).astype(F32)
        rc = jnp.dot(tri, df, preferred_element_type=F32, precision=lax.Precision.HIGHEST) + carry[...]
        carry[...] += jnp.sum(df, axis=0, keepdims=True)
        o_ref[...] = (rc * (1.0 / (1.0 + jnp.exp(x_ref[...])))).astype(o_ref.dtype)

    return pl.pallas_call(
        body, grid=(nt,),
        in_specs=[pl.BlockSpec((ts, wd), lambda i: (nt - 1 - i, 0)),
                  pl.BlockSpec((ts, wd), lambda i: (nt - 1 - i, 0)),
                  pl.BlockSpec((ts, LANES), lambda i: (nt - 1 - i, cb))],
        out_specs=pl.BlockSpec((ts, LANES), lambda i: (nt - 1 - i, 0)),
        out_shape=jax.ShapeDtypeStruct((s, LANES), BF16),
        scratch_shapes=[pltpu.VMEM((1, LANES), F32)],
        compiler_params=_cparams(("arbitrary",)), name="forget_cumsum_bwd")(dq, dk, z)


def _flash_fwd(q, k, v, scale, nh, l_lane, *, name):
    s = q.shape[0]
    t = min(ATT_TILE, s)
    half = t // 2 if t % (2 * LANES) == 0 else t
    nq = s // t
    c = scale * LOG2E

    def body(q_ref, k_ref, v_ref, o_ref, qb_ref):
        i = pl.program_id(1)
        qb = q_ref[...]

        def scores(q_rows, k0, nk):
            kb = k_ref[pl.ds(pl.multiple_of(k0, half), nk), :]
            return lax.dot_general(q_rows, kb, _NT, preferred_element_type=F32)

        def update(sc, k0, nk, carry):
            m, acc = carry
            m_new = jnp.maximum(m, jnp.max(sc, axis=1, keepdims=True))
            p = jnp.exp2((sc - m_new) * c)
            alpha = jnp.exp2((m - m_new) * c)
            vb = v_ref[pl.ds(pl.multiple_of(k0, half), nk), :]
            acc = alpha * acc + jnp.dot(p.astype(BF16), vb, preferred_element_type=F32)
            return m_new, acc

        def full_block(j, cr):
            return update(scores(qb, j * t, t), j * t, t, cr)

        def group(jj, cr):
            for n in range(FWD_GROUP):
                cr = full_block(FWD_GROUP * jj + n, cr)
            return cr

        def causal(sc):
            row = lax.broadcasted_iota(jnp.int32, sc.shape, 0)
            col = lax.broadcasted_iota(jnp.int32, sc.shape, 1)
            return jnp.where(col <= row, sc, NEG)

        init = (jnp.full((t, 1), NEG, F32), jnp.zeros((t, LANES), F32))
        n_groups = i >> FWD_GROUP_LOG2
        carry = lax.fori_loop(0, n_groups, group, init)
        carry = lax.fori_loop(n_groups * FWD_GROUP, i, full_block, carry)
        m, acc = update(causal(scores(qb, i * t, t)), i * t, t, carry)
        lane = _lane((t, LANES))
        l = _lane_column(acc, lane, V_ONES)
        o_ref[...] = (acc / l).astype(o_ref.dtype)
        big_l = m + jnp.log(l) / scale
        qb_ref[...] = _put3(qb.astype(F32), lane, l_lane, _split3(-big_l)).astype(qb_ref.dtype)

    head_rows = pl.BlockSpec((t, LANES), lambda h, i: (i, h))
    head_all = pl.BlockSpec((s, LANES), lambda h, i: (0, h))
    return pl.pallas_call(
        body, grid=(nh, nq), in_specs=[head_rows, head_all, head_all], out_specs=[head_rows, head_rows],
        out_shape=[jax.ShapeDtypeStruct(q.shape, BF16), jax.ShapeDtypeStruct(q.shape, BF16)],
        compiler_params=_cparams(("parallel", "arbitrary")), name=name)(q, k, v)


def _flash_bwd(qb, k, v, do, scale, nh, n_feat, *, name):
    s = qb.shape[0]
    t = min(ATT_TILE, s)
    half = t // 2 if t % (2 * LANES) == 0 else t
    nq = s // t
    c = scale * LOG2E

    def body(q_ref, k_ref, v_ref, do_ref, dk_ref, dv_ref, dq_ref):
        j = pl.program_id(1)
        kb = k_ref[...]
        vb = v_ref[...]

        @pl.when(j == 0)
        def _():
            dq_ref[...] = jnp.zeros_like(dq_ref)

        def part(q0, n_q, n_k, carry, q_off):
            dk_acc, dv_acc = carry
            rows = pl.ds(pl.multiple_of(q0, half), n_q)
            qblk = q_ref[rows, :]
            dob = do_ref[rows, :]
            kbb, vbb = kb[:n_k], vb[:n_k]
            st = lax.dot_general(kbb, qblk, _NT, preferred_element_type=F32)
            if q_off is not None:
                key = lax.broadcasted_iota(jnp.int32, st.shape, 0)
                qry = lax.broadcasted_iota(jnp.int32, st.shape, 1) + q_off
                st = jnp.where(key <= qry, st, NEG)
            pt = jnp.exp2(st * c)
            dv_new = jnp.dot(pt.astype(BF16), dob, preferred_element_type=F32)
            dpt = lax.dot_general(vbb, dob, _NT, preferred_element_type=F32)
            dsb = (pt * dpt).astype(BF16)
            dk_new = jnp.dot(dsb, qblk, preferred_element_type=F32)
            dq_ref[rows, :] += lax.dot_general(dsb, kbb, _TN, preferred_element_type=F32)
            if n_k == t:
                return dk_acc + dk_new, dv_acc + dv_new
            return (jnp.concatenate([dk_acc[:n_k] + dk_new, dk_acc[n_k:]], axis=0),
                    jnp.concatenate([dv_acc[:n_k] + dv_new, dv_acc[n_k:]], axis=0))

        def block(i, carry):
            return part(i * t, t, t, carry, None)

        init = (jnp.zeros((t, LANES), F32), jnp.zeros((t, LANES), F32))
        carry = part(j * t + half, t - half, t, init, half) if half < t else part(j * t, t, t, init, 0)
        if half < t:
            carry = part(j * t, half, half, carry, 0)
        rest = nq - 1 - j
        carry = lax.cond((rest & 1) == 1, lambda cr: block(j + 1, cr), lambda cr: cr, carry)
        first = j + 1 + (rest & 1)

        def pair(ii, cr):
            i0 = first + 2 * ii
            return block(i0 + 1, block(i0, cr))

        dk_acc, dv_acc = lax.fori_loop(0, rest >> 1, pair, carry)
        dk_ref[...] = dk_acc * jnp.where(_lane((t, LANES)) < n_feat, scale, 1.0)
        dv_ref[...] = dv_acc.astype(dv_ref.dtype)

        @pl.when(j == nq - 1)
        def _():
            dq_ref[...] = dq_ref[...] * jnp.where(_lane((s, LANES)) < n_feat, scale, 1.0)

    head_rows = pl.BlockSpec((t, LANES), lambda h, j: (j, h))
    head_all = pl.BlockSpec((s, LANES), lambda h, j: (0, h))
    shp = jax.ShapeDtypeStruct(qb.shape, F32)
    return pl.pallas_call(
        body, grid=(nh, nq), in_specs=[head_all, head_rows, head_rows, head_all],
        out_specs=[head_rows, head_rows, head_all],
        out_shape=[shp, jax.ShapeDtypeStruct(qb.shape, BF16), shp],
        compiler_params=_cparams(("parallel", "arbitrary")), name=name)(qb, k, v, do)


def _with_delta(do, o, nh, *, name):
    def fn(ti, pa):
        lane = _lane((ti[0].shape[0], LANES))
        out = []
        for d_blk, o_blk in zip(_blocks(ti[0], nh), _blocks(ti[1], nh), strict=True):
            delta = jnp.sum(d_blk * o_blk.astype(F32), axis=1, keepdims=True)
            out.append(_put3(d_blk, lane, V_ONES, _split3(-delta)))
        return [jnp.concatenate(out, axis=1)], []

    (res,) = _rowwise(name, fn, [do, o], [], [(do.shape[1], BF16)])
    return res


def _pad_heads(a, nh):
    d = a.shape[-1] // nh
    a = a.reshape(a.shape[:-1] + (nh, d))
    a = jnp.pad(a, [(0, 0)] * (a.ndim - 1) + [(0, LANES - d)])
    return a.reshape(a.shape[:-2] + (nh * LANES,))


def _unpad_heads(a, nh, d):
    a = a.reshape(a.shape[:-1] + (nh, LANES))[..., :d]
    return a.reshape(a.shape[:-2] + (nh * d,))


def _pad_head_rows(w, nh):
    return _pad_heads(w.T, nh).T


def _unpad_head_rows(g, nh, d):
    return _unpad_heads(g.T, nh, d).T


class _ZLayout:
    def __init__(self, d):
        fw = FOX_HEADS * FOX_DIM
        self.src = {}
        off = 0
        for nm, w in (("cq", MLA_Q_LORA), ("ckv", MLA_KV_LORA), ("kr", MLA_ROPE), ("fq", fw), ("fk", fw),
                      ("fv", fw), ("fl", FOX_HEADS), ("ga", d), ("gb", d)):
            self.src[nm] = (off, w)
            off += w
        self.dst = {}
        off = 0
        for nm, w in (("ga", d), ("gb", d), ("fq", fw), ("fk", fw), ("fv", fw), ("cq", MLA_Q_LORA),
                      ("ckv", MLA_KV_LORA), ("kr", LANES), ("fl", LANES)):
            assert off % w == 0
            self.dst[nm] = (off, w)
            off += w
        self.width = off
        self.split = self.dst["cq"][0]
        assert all((o - self.split) % w == 0 for o, w in self.dst.values() if o >= self.split)

    def to_kernel(self, w):
        def seg(nm):
            off, wd = self.src[nm]
            return w[..., off:off + wd]

        def pad(a, left, total):
            return jnp.pad(a, [(0, 0)] * (a.ndim - 1) + [(left, total - left - a.shape[-1])])

        return jnp.concatenate([seg("ga"), seg("gb"), seg("fq"), seg("fk"), seg("fv"), seg("cq"), seg("ckv"),
                                pad(seg("kr"), MLA_NOPE, LANES), pad(seg("fl"), 0, LANES)], axis=-1)

    def from_kernel(self, g):
        def seg(nm, lo=0, hi=None):
            off, wd = self.dst[nm]
            return g[..., off + lo:off + (wd if hi is None else hi)]

        return jnp.concatenate([seg("cq"), seg("ckv"), seg("kr", MLA_NOPE, MLA_NOPE + MLA_ROPE), seg("fq"), seg("fk"),
                                seg("fv"), seg("fl", 0, FOX_HEADS), seg("ga"), seg("gb")], axis=-1)


def _local_step(x, positions, target, wts, late_weights, send_grads):
    s, d = x.shape
    zl = _ZLayout(d)
    hw = MLA_HEADS * LANES
    assert MLA_HEADS == FOX_HEADS
    scale_mla = (MLA_NOPE + MLA_ROPE) ** -0.5
    scale_fox = FOX_DIM ** -0.5

    inv_freq = ROPE_THETA ** (-jnp.arange(HALF_ROPE, dtype=F32) / HALF_ROPE)
    ang = positions.astype(F32)[:, None] * inv_freq
    cos, sin = jnp.cos(ang), jnp.sin(ang)
    tail = jnp.zeros((s, LANES - MLA_NOPE - MLA_ROPE), F32)
    rc = jnp.concatenate([jnp.ones((s, MLA_NOPE), F32), cos, cos, tail], axis=1)
    ra = jnp.concatenate([jnp.zeros((s, MLA_NOPE), F32), -sin, jnp.zeros((s, HALF_ROPE), F32), tail], axis=1)
    rb = jnp.concatenate([jnp.zeros((s, MLA_NOPE + HALF_ROPE), F32), sin, tail], axis=1)

    w_in = zl.to_kernel(wts["w_in"])
    b_in = zl.to_kernel(wts["b_in"])

    def f_norm_in(ti, pa):
        y, _ = _rms(ti[0], pa[0])
        return [y], []

    (h,) = _rowwise("norm_in", f_norm_in, [x], [wts["ln_pre_mix"]], [(d, BF16)])
    z_lo = _mm(h, w_in[:, :zl.split], bias=b_in[:, :zl.split], out_dtype=BF16, name="proj_in_lo")
    z = _mm(h, w_in[:, zl.split:], bias=b_in[:, zl.split:], out_dtype=F32, name="proj_in_hi")

    def zwin(nm):
        off, wd = zl.dst[nm]
        return (z_lo, off // wd, wd) if off < zl.split else (z, (off - zl.split) // wd, wd)

    wts = {**wts, **late_weights("mixer", z)}
    w_uq = _pad_heads(wts["w_uq"], MLA_HEADS)
    w_ukv = jnp.concatenate([_pad_heads(wts["w_uk"], MLA_HEADS), _pad_heads(wts["w_uv"], MLA_HEADS)], axis=1)
    w_o_mla = _pad_head_rows(wts["w_o_mla"], MLA_HEADS)
    w_o_fox = _pad_head_rows(wts["w_o_fox"], FOX_HEADS)

    def f_mla_norms(ti, pa):
        cqn, _ = _rms(ti[0], pa[0])
        ckvn, _ = _rms(ti[1], pa[1])
        return [cqn, ckvn], []

    cqn, ckvn = _rowwise("mla_norms", f_mla_norms, [zwin("cq"), zwin("ckv")],
                         [wts["q_a_norm"], wts["kv_a_norm"]], [(MLA_Q_LORA, BF16), (MLA_KV_LORA, BF16)])
    qf = _mm(cqn, w_uq, out_dtype=F32, name="proj_uq")
    kv = _mm(ckvn, w_ukv, out_dtype=BF16, name="proj_ukv")

    def f_rope_q(ti, pa):
        xq, c_, a_, b_ = ti
        return [jnp.concatenate([_rope_block(blk, c_, a_, b_) for blk in _blocks(xq, MLA_HEADS)], axis=1)], []

    (q_mla,) = _rowwise("rope_q", f_rope_q, [qf, rc, ra, rb], [], [(hw, BF16)])

    def f_mla_kv(ti, pa):
        kn, vn, kr, c_, a_, b_ = ti
        lane = _lane(kr.shape)
        k_tail = jnp.where((lane >= MLA_Q_L) & (lane < MLA_Q_L + 3), 1.0, _rope_block(kr, c_, a_, b_))
        ones_v = (lane >= V_ONES) & (lane < V_ONES + 3)
        k_out = [jnp.where(lane < MLA_NOPE, blk.astype(F32), k_tail) for blk in _blocks(kn, MLA_HEADS)]
        v_out = [jnp.where(ones_v, 1.0, blk.astype(F32)) for blk in _blocks(vn, MLA_HEADS)]
        return [jnp.concatenate(k_out, axis=1), jnp.concatenate(v_out, axis=1)], []

    k_mla, v_mla = _rowwise("mla_kv", f_mla_kv, [(kv, 0, hw), (kv, 1, hw), zwin("kr"), rc, ra, rb], [],
                            [(hw, BF16), (hw, BF16)])
    o_mla, qb_mla = _flash_fwd(q_mla, k_mla, v_mla, scale_mla, MLA_HEADS, MLA_Q_L, name="mla_fwd")

    fl_cb = zwin("fl")[1]
    fcol = _forget_cumsum(z, fl_cb)

    def f_fox_qkv(ti, pa):
        spread = _head_spread(FOX_HEADS, FOX_DIM)
        fq, fk, fv = (jnp.dot(a, spread, preferred_element_type=F32) for a in ti[:3])
        fc = ti[3]
        lane = _lane(fc.shape)
        ones_q = (lane >= FOX_Q_ONES) & (lane < FOX_Q_ONES + 3)
        ones_k = (lane >= FOX_Q_F) & (lane < FOX_Q_ONES)
        ones_v = (lane >= V_ONES) & (lane < V_ONES + 3)
        q_out, k_out, v_out = [], [], []
        for hh, (qblk, kblk, vblk) in enumerate(zip(_blocks(fq, FOX_HEADS), _blocks(fk, FOX_HEADS),
                                                    _blocks(fv, FOX_HEADS), strict=True)):
            f_h = _lane_column(fc, lane, hh) * (1.0 / scale_fox)
            q_out.append(_put3(jnp.where(ones_q, 1.0, qblk), lane, FOX_Q_F, _split3(f_h)))
            k_out.append(_put3(jnp.where(ones_k, 1.0, kblk), lane, FOX_Q_ONES, _split3(-f_h)))
            v_out.append(jnp.where(ones_v, 1.0, vblk))
        return [jnp.concatenate(q_out, axis=1), jnp.concatenate(k_out, axis=1), jnp.concatenate(v_out, axis=1)], []

    q_fox, k_fox, v_fox = _rowwise("fox_qkv", f_fox_qkv, [zwin("fq"), zwin("fk"), zwin("fv"), fcol],
                                   [], [(hw, BF16)] * 3)
    o_fox, qb_fox = _flash_fwd(q_fox, k_fox, v_fox, scale_fox, FOX_HEADS, FOX_Q_L, name="fox_fwd")

    y_mla = _mm(o_mla, w_o_mla, out_dtype=BF16, name="proj_o_mla")
    y_fox = _mm(o_fox, w_o_fox, out_dtype=BF16, name="proj_o_fox")

    def f_gate(ti, pa):
        ga, gb, ya, yb = (a.astype(F32) for a in ti)
        return [_sigmoid(ga) * ya + _sigmoid(gb) * yb], []

    (merged,) = _rowwise("gate", f_gate, [zwin("ga"), zwin("gb"), y_mla, y_fox], [], [(d, BF16)])
    wts = {**wts, **late_weights("mlp", merged)}
    mix = _mm(merged, wts["w_out"], out_dtype=F32, name="proj_out")

    def f_resid1(ti, pa):
        xa, mx = ti
        y, _ = _rms(mx, pa[0])
        x1 = xa + y
        h2, _ = _rms(x1, pa[1])
        return [x1, h2], []

    x1, h2 = _rowwise("resid_mix", f_resid1, [x, mix], [wts["ln_post_mix"], wts["ln_pre_mlp"]], [(d, F32), (d, BF16)])

    def relu2(acc, ex):
        r = jnp.maximum(acc, 0.0)
        return [acc, r * r]

    u, act = _mm(h2, wts["w_ff1"], out_dtype=[BF16, BF16], name="ff1", epilogue=relu2)
    mo = _mm(act, wts["w_ff2"], out_dtype=F32, name="ff2")

    def f_loss(ti, pa):
        xa, mv, tg = ti
        y, _ = _rms(mv, pa[0])
        err = (xa + y) - tg
        g2 = err / d
        dmo, dg = _rms_bwd(mv, pa[0], g2)
        return [g2, dmo], [jnp.sum(err * err, axis=0, keepdims=True), dg]

    g2, d_mo, loss_cols, g_ln_post_mlp = _rowwise("loss", f_loss, [x1, mo, target], [wts["ln_post_mlp"]],
                                                  [(d, F32), (d, BF16)], [d, d])
    loss = 0.5 * jnp.sum(loss_cols) / d

    grads = {"ln_post_mlp": g_ln_post_mlp}
    grads["w_ff2"] = _mm_tn(act, d_mo, name="grad_ff2")

    def relu2_bwd(acc, ex):
        return [acc * (2.0 * jnp.maximum(ex[0], 0.0))]

    (d_u,) = _mm(d_mo, wts["w_ff2"], out_dtype=[BF16], name="ff2_bwd", transpose_b=True, extras=[u], epilogue=relu2_bwd)
    grads["w_ff1"] = _mm_tn(h2, d_u, name="grad_ff1")
    d_h2 = _mm(d_u, wts["w_ff1"], out_dtype=F32, name="ff1_bwd", transpose_b=True)

    def f_resid1_bwd(ti, pa):
        gres, dh2, x1v, mx = ti
        dx1n, dg_pre_mlp = _rms_bwd(x1v, pa[1], dh2)
        dx1 = gres + dx1n
        dmix, dg_post_mix = _rms_bwd(mx, pa[0], dx1)
        return [dx1, dmix], [dg_post_mix, dg_pre_mlp]

    d_x1, d_mix, grads["ln_post_mix"], grads["ln_pre_mlp"] = _rowwise(
        "resid_mix_bwd", f_resid1_bwd, [g2, d_h2, x1, mix], [wts["ln_post_mix"], wts["ln_pre_mlp"]],
        [(d, F32), (d, BF16)], [d, d])
    grads["w_out"] = _mm_tn(merged, d_mix, name="grad_out")
    d_merged = _mm(d_mix, wts["w_out"], out_dtype=BF16, name="proj_out_bwd", transpose_b=True)

    def f_gate_bwd(ti, pa):
        dm, ga, gb, ya, yb = (a.astype(F32) for a in ti)
        sa, sb = _sigmoid(ga), _sigmoid(gb)
        return [dm * sa, dm * sb, dm * ya * (sa * (1.0 - sa)), dm * yb * (sb * (1.0 - sb))], []

    d_ya, d_yb, d_ga, d_gb = _rowwise("gate_bwd", f_gate_bwd,
                                      [d_merged, zwin("ga"), zwin("gb"), y_mla, y_fox], [],
                                      [(d, BF16)] * 4)
    grads["w_o_mla"] = _unpad_head_rows(_mm_tn(o_mla, d_ya, name="grad_o_mla"), MLA_HEADS, MLA_V)
    grads["w_o_fox"] = _unpad_head_rows(_mm_tn(o_fox, d_yb, name="grad_o_fox"), FOX_HEADS, FOX_DIM)
    w_o_mla = w_o_mla + send_grads("early", {nm: grads[nm] for nm in EARLY_GRADS}).astype(BF16)
    do_mla = _with_delta(_mm(d_ya, w_o_mla, out_dtype=F32, name="proj_o_mla_bwd", transpose_b=True), o_mla,
                         MLA_HEADS, name="mla_delta")
    do_fox = _with_delta(_mm(d_yb, w_o_fox, out_dtype=F32, name="proj_o_fox_bwd", transpose_b=True), o_fox,
                         FOX_HEADS, name="fox_delta")

    dk_mla, dv_mla, dq_mla = _flash_bwd(qb_mla, k_mla, v_mla, do_mla, scale_mla, MLA_HEADS, MLA_NOPE + MLA_ROPE,
                                        name="mla_bwd")
    dk_fox, dv_fox, dq_fox = _flash_bwd(qb_fox, k_fox, v_fox, do_fox, scale_fox, FOX_HEADS, FOX_DIM, name="fox_bwd")
    d_fl = _forget_cumsum_bwd(dq_fox, dk_fox, z, fl_cb, FOX_HEADS)

    def f_rope_q_bwd(ti, pa):
        g, c_, a_, b_ = ti
        return [jnp.concatenate([_rope_block(blk, c_, -a_, -b_) for blk in _blocks(g, MLA_HEADS)], axis=1)], []

    (d_qf,) = _rowwise("rope_q_bwd", f_rope_q_bwd, [dq_mla, rc, ra, rb], [], [(hw, BF16)])
    grads["w_uq"] = _unpad_heads(_mm_tn(cqn, d_qf, name="grad_uq"), MLA_HEADS, MLA_NOPE + MLA_ROPE)
    d_cqn = _mm(d_qf, w_uq, out_dtype=F32, name="proj_uq_bwd", transpose_b=True)

    def f_mla_kv_bwd(ti, pa):
        gk, gv, c_, a_, b_ = ti
        k_blocks = _blocks(gk, MLA_HEADS)
        tot = k_blocks[0]
        for blk in k_blocks[1:]:
            tot = tot + blk
        return [jnp.concatenate([gk, gv], axis=1), _rope_block(tot, c_, -a_, -b_)], []

    d_kv, d_kr = _rowwise("mla_kv_bwd", f_mla_kv_bwd, [dk_mla, dv_mla, rc, ra, rb], [], [(2 * hw, BF16), (LANES, BF16)])
    g_ukv = _mm_tn(ckvn, d_kv, name="grad_ukv")
    grads["w_uk"] = _unpad_heads(g_ukv[:, :hw], MLA_HEADS, MLA_NOPE)
    grads["w_uv"] = _unpad_heads(g_ukv[:, hw:], MLA_HEADS, MLA_V)
    d_ckvn = _mm(d_kv, w_ukv, out_dtype=F32, name="proj_ukv_bwd", transpose_b=True)

    def f_mla_norms_bwd(ti, pa):
        cq, ckv, dcqn, dckvn = ti
        dcq, dg_q = _rms_bwd(cq, pa[0], dcqn)
        dckv, dg_kv = _rms_bwd(ckv, pa[1], dckvn)
        return [dcq, dckv], [dg_q, dg_kv]

    d_cq, d_ckv, grads["q_a_norm"], grads["kv_a_norm"] = _rowwise(
        "mla_norms_bwd", f_mla_norms_bwd, [zwin("cq"), zwin("ckv"), d_cqn, d_ckvn],
        [wts["q_a_norm"], wts["kv_a_norm"]], [(MLA_Q_LORA, BF16), (MLA_KV_LORA, BF16)], [MLA_Q_LORA, MLA_KV_LORA])

    def f_fox_compact(ti, pa):
        gather = _head_spread(FOX_HEADS, FOX_DIM)
        return [lax.dot_general(a.astype(BF16), gather, _NT, preferred_element_type=F32) for a in ti], []

    d_fq, d_fk, d_fv = _rowwise("fox_compact", f_fox_compact, [dq_fox, dk_fox, dv_fox], [],
                                [(FOX_HEADS * FOX_DIM, BF16)] * 3)
    d_z = jnp.concatenate([d_ga, d_gb, d_fq, d_fk, d_fv, d_cq, d_ckv, d_kr, d_fl], axis=1)
    assert d_z.shape[1] == zl.width

    def f_bias(ti, pa):
        return [], [jnp.sum(ti[0].astype(F32), axis=0, keepdims=True)]

    (g_b_in,) = _rowwise("grad_b_in", f_bias, [d_z], [], [], [zl.width])
    grads["b_in"] = zl.from_kernel(g_b_in)
    grads["w_in"] = zl.from_kernel(_mm_tn(h, d_z, name="grad_in"))
    tok = send_grads("late", {nm: grads[nm] for nm, _ in BIG if nm not in EARLY_GRADS})
    d_h = _mm(d_z, w_in, bias=jnp.zeros((1, d), F32) + tok, out_dtype=F32, name="proj_in_bwd", transpose_b=True)

    def f_norm_in_bwd(ti, pa):
        dx1v, dh, xa = ti
        dxn, dg = _rms_bwd(xa, pa[0], dh)
        return [dx1v + dxn], [dg]

    grad_x, grads["ln_pre_mix"] = _rowwise("norm_in_bwd", f_norm_in_bwd, [d_x1, d_h, x], [wts["ln_pre_mix"]],
                                           [(d, F32)], [d])
    return loss, grad_x, grads


class _PackLayout:
    def __init__(self, shapes):
        self.shapes = list(shapes)
        self.width = _round_up(max(b for _, b in shapes), LANES)
        self.bands = []
        row = 0
        shelf = []
        for idx, (a, b) in enumerate(shapes):
            if 2 * _round_up(b, LANES) > self.width:
                self.bands.append((row, _round_up(a, 32), [(idx, 0)]))
                row += _round_up(a, 32)
            else:
                shelf.append(idx)
        col, items = 0, []
        for idx in shelf:
            wb = _round_up(shapes[idx][1], LANES)
            if col + wb > self.width:
                hgt = max(_round_up(shapes[i][0], 32) for i, _ in items)
                self.bands.append((row, hgt, items))
                row += hgt
                col, items = 0, []
            items.append((idx, col))
            col += wb
        if items:
            hgt = max(_round_up(shapes[i][0], 32) for i, _ in items)
            self.bands.append((row, hgt, items))
            row += hgt
        self.rows = _round_up(row, 16 * LOCAL_PIECES)

    def pack(self, arrs):
        lead = arrs[0].shape[:-2]
        no_pad = [(0, 0)] * len(lead)
        bands = []
        for _, hgt, items in self.bands:
            parts = []
            for k, (idx, col) in enumerate(items):
                a, b = self.shapes[idx]
                nxt = items[k + 1][1] if k + 1 < len(items) else self.width
                parts.append(jnp.pad(arrs[idx], no_pad + [(0, hgt - a), (0, nxt - col - b)]))
            bands.append(parts[0] if len(parts) == 1 else jnp.concatenate(parts, axis=-1))
        used = sum(hgt for _, hgt, _ in self.bands)
        if used < self.rows:
            bands.append(jnp.zeros(lead + (self.rows - used, self.width), arrs[0].dtype))
        return jnp.concatenate(bands, axis=-2)

    def unpack(self, packed):
        out = [None] * len(self.shapes)
        for row, _, items in self.bands:
            for idx, col in items:
                a, b = self.shapes[idx]
                out[idx] = packed[..., row:row + a, col:col + b]
        return out


def _to_shards(g, axis):
    if axis == 0:
        return g.reshape(N_CHIPS, g.shape[0] // N_CHIPS, g.shape[1])
    return jnp.stack(jnp.split(g, N_CHIPS, axis=1))


def _from_shards(s4, axis):
    n, a, b = s4.shape
    if axis == 0:
        return s4.reshape(n * a, b)
    return jnp.concatenate([s4[ch] for ch in range(n)], axis=1)


ANY = pl.BlockSpec(memory_space=pl.ANY)


def _place():
    return lax.axis_index("x"), lax.axis_index("y"), lax.axis_index("c")


def _gather_weights(wpk):
    rows, wd = wpk.shape
    half = rows // 2

    def body(w_ref, out_ref, send_sems, recv_sems, local_sems):
        x, y, c = _place()
        sibling = (x, y, 1 - c)
        chips = [(1 - x, y), (x, 1 - y), (1 - x, 1 - y)]

        def slab(chip, hf):
            return out_ref.at[2 * chip[0] + chip[1], pl.ds(hf * half, half), :]

        def copy(k, chip, hf, to, src=None):
            return pltpu.make_async_remote_copy(
                src_ref=slab(chip, hf) if src is None else src, dst_ref=slab(chip, hf),
                send_sem=send_sems.at[k], recv_sem=recv_sems.at[k], device_id=to, device_id_type=MESH)

        piece = rows // LOCAL_PIECES
        mine = [pltpu.make_async_copy(w_ref.at[pl.ds(n * piece, piece), :],
                                      out_ref.at[2 * x + y, pl.ds(n * piece, piece), :], local_sems.at[n])
                for n in range(LOCAL_PIECES)]
        for cp in mine:
            cp.start()
        first = [copy(j, (x, y), c, (*chip, c), src=w_ref.at[pl.ds(c * half, half), :]) for j, chip in enumerate(chips)]
        for cp in first:
            cp.start()
        passed = [copy(3 + j, chip, c, sibling) for j, chip in enumerate(chips)]
        for j, chip in enumerate(chips):
            copy(j, chip, c, (x, y, c)).wait_recv()
            passed[j].start()
        for j, chip in enumerate(chips):
            copy(3 + j, chip, 1 - c, (x, y, c)).wait_recv()
        for cp in first + passed:
            cp.wait_send()
        for cp in mine:
            cp.wait()

    assert rows % (16 * LOCAL_PIECES) == 0
    return pl.pallas_call(
        body, out_shape=jax.ShapeDtypeStruct((N_CHIPS, rows, wd), wpk.dtype),
        in_specs=[ANY], out_specs=ANY,
        scratch_shapes=[pltpu.SemaphoreType.DMA((6,)), pltpu.SemaphoreType.DMA((6,)),
                        pltpu.SemaphoreType.DMA((LOCAL_PIECES,))],
        name="gather_weights")(wpk)


HBM = pl.BlockSpec(memory_space=pltpu.HBM)
SEM = pl.BlockSpec(memory_space=pltpu.SEMAPHORE)
EFFECT = pltpu.SideEffectType.DATAFLOW_SIDE_EFFECTING
N_LATE = 6


def _gather_late_start(wpk, tag):
    rows, wd = wpk.shape
    half = rows // 2

    def body(w_ref, land_ref, send_sems, recv_sems, w_thru, land_thru, token):
        x, y, c = _place()
        chips = [(1 - x, y), (x, 1 - y), (1 - x, 1 - y)]
        for j, chip in enumerate(chips):
            for to_core in range(2):
                pltpu.make_async_remote_copy(
                    src_ref=w_ref.at[pl.ds(c * half, half), :],
                    dst_ref=land_ref.at[2 * x + y, pl.ds(c * half, half), :],
                    send_sem=send_sems.at[2 * j + to_core], recv_sem=recv_sems.at[2 * j + c],
                    device_id=(*chip, to_core), device_id_type=MESH).start()
        token[...] = jnp.zeros_like(token)

    land = lax.empty((N_CHIPS, rows, wd), wpk.dtype)
    return pl.pallas_call(
        body, name="gather_" + tag + "_start",
        out_shape=(pltpu.SemaphoreType.DMA((N_LATE,)), pltpu.SemaphoreType.DMA((N_LATE,)),
                   pltpu.HBM(wpk.shape, wpk.dtype), pltpu.HBM(land.shape, land.dtype),
                   jax.ShapeDtypeStruct((8, LANES), F32)),
        in_specs=(HBM, HBM), out_specs=(SEM, SEM, HBM, HBM, pl.BlockSpec(memory_space=pltpu.VMEM)),
        input_output_aliases={0: 2, 1: 3},
        compiler_params=pltpu.CompilerParams(has_side_effects=EFFECT),
    )(pltpu.with_memory_space_constraint(wpk, pltpu.HBM), pltpu.with_memory_space_constraint(land, pltpu.HBM))


def _gather_late_wait(send_sems, recv_sems, w_thru, land_thru, after, tag):
    rows, wd = w_thru.shape
    half = rows // 2

    def body(w_ref, land_ref, send_sems, recv_sems, after_ref, w_dead, land_out):
        x, y, c = _place()
        for n in range(N_LATE):
            cp = pltpu.make_async_remote_copy(
                src_ref=w_ref.at[pl.ds(0, half), :], dst_ref=land_ref.at[0, pl.ds(0, half), :],
                send_sem=send_sems.at[n], recv_sem=recv_sems.at[n], device_id=(x, y, c), device_id_type=MESH)
            cp.wait_send()
            cp.wait_recv()

    return pl.pallas_call(
        body, name="gather_" + tag + "_wait",
        out_shape=(pltpu.HBM(w_thru.shape, w_thru.dtype), pltpu.HBM(land_thru.shape, land_thru.dtype)),
        in_specs=(HBM, HBM, SEM, SEM, ANY), out_specs=(HBM, HBM), input_output_aliases={0: 0, 1: 1},
        compiler_params=pltpu.CompilerParams(has_side_effects=EFFECT),
    )(w_thru, land_thru, send_sems, recv_sems, after)[1]


N_PART = 7


def _reduce_start(gbf, tag):
    _, _, hrows, wd = gbf.shape

    def body(g_ref, land_ref, send_sems, recv_sems, g_thru, land_thru, token):
        x, y, c = _place()
        chips = [(1 - x, y), (x, 1 - y), (1 - x, 1 - y)]
        for j, chip in enumerate(chips):
            for to_core in range(2):
                pltpu.make_async_remote_copy(
                    src_ref=g_ref.at[2 * chip[0] + chip[1], to_core], dst_ref=land_ref.at[2 * j + c],
                    send_sem=send_sems.at[2 * j + to_core], recv_sem=recv_sems.at[2 * j + c],
                    device_id=(*chip, to_core), device_id_type=MESH).start()
        pltpu.make_async_remote_copy(
            src_ref=g_ref.at[2 * x + y, 1 - c], dst_ref=land_ref.at[N_PART - 1],
            send_sem=send_sems.at[N_PART - 1], recv_sem=recv_sems.at[N_PART - 1],
            device_id=(x, y, 1 - c), device_id_type=MESH).start()
        token[...] = jnp.zeros_like(token)

    land = lax.empty((N_PART, hrows, wd), gbf.dtype)
    return pl.pallas_call(
        body, name="reduce_" + tag + "_start",
        out_shape=(pltpu.SemaphoreType.DMA((N_PART,)), pltpu.SemaphoreType.DMA((N_PART,)),
                   pltpu.HBM(gbf.shape, gbf.dtype), pltpu.HBM(land.shape, land.dtype),
                   jax.ShapeDtypeStruct((8, LANES), F32)),
        in_specs=(HBM, HBM), out_specs=(SEM, SEM, HBM, HBM, pl.BlockSpec(memory_space=pltpu.VMEM)),
        input_output_aliases={0: 2, 1: 3},
        compiler_params=pltpu.CompilerParams(has_side_effects=EFFECT),
    )(pltpu.with_memory_space_constraint(gbf, pltpu.HBM), pltpu.with_memory_space_constraint(land, pltpu.HBM))


def _reduce_wait(send_sems, recv_sems, g_thru, land_thru, after, tag):
    def body(g_ref, land_ref, send_sems, recv_sems, after_ref, g_dead, land_out):
        x, y, c = _place()
        for n in range(N_PART):
            cp = pltpu.make_async_remote_copy(
                src_ref=g_ref.at[0, 0], dst_ref=land_ref.at[0], send_sem=send_sems.at[n], recv_sem=recv_sems.at[n],
                device_id=(x, y, c), device_id_type=MESH)
            cp.wait_send()
            cp.wait_recv()

    return pl.pallas_call(
        body, name="reduce_" + tag + "_wait",
        out_shape=(pltpu.HBM(g_thru.shape, g_thru.dtype), pltpu.HBM(land_thru.shape, land_thru.dtype)),
        in_specs=(HBM, HBM, SEM, SEM, ANY), out_specs=(HBM, HBM), input_output_aliases={0: 0, 1: 1},
        compiler_params=pltpu.CompilerParams(has_side_effects=EFFECT),
    )(g_thru, land_thru, send_sems, recv_sems, after)[1]


def _sibling_swap(mine):
    def body(m_ref, out_ref, send_sem, recv_sem):
        x, y, c = _place()
        cp = pltpu.make_async_remote_copy(
            src_ref=m_ref, dst_ref=out_ref, send_sem=send_sem, recv_sem=recv_sem,
            device_id=(x, y, 1 - c), device_id_type=MESH)
        cp.start()
        cp.wait()

    return pl.pallas_call(
        body, out_shape=jax.ShapeDtypeStruct(mine.shape, mine.dtype), in_specs=[ANY], out_specs=ANY,
        scratch_shapes=[pltpu.SemaphoreType.DMA, pltpu.SemaphoreType.DMA], name="grad_sibling_swap")(mine)


def _adamw(w, g, m, v):
    m = ADAM_B1 * m + (1.0 - ADAM_B1) * g
    v = ADAM_B2 * v + (1.0 - ADAM_B2) * (g * g)
    m_hat = m / (1.0 - ADAM_B1 ** ADAM_STEP)
    v_hat = v / (1.0 - ADAM_B2 ** ADAM_STEP)
    delta = -ADAM_LR * (m_hat / (jnp.sqrt(v_hat) + ADAM_EPS) + ADAM_WD * w)
    return delta, m, v


def _small_allreduce_adamw(gs, ws, ms, vs):
    n_dev = 8
    n_par = len(gs)
    wd = PACK_W
    chunks = []
    for p, g in enumerate(gs):
        for off in range(0, g.shape[1], wd):
            chunks.append((p, len(chunks), off, min(wd, g.shape[1] - off)))
    rows = _round_up(len(chunks), 8)

    def body(*refs):
        g_refs, w_refs, m_refs, v_refs = (refs[k * n_par:(k + 1) * n_par] for k in range(4))
        go_refs, d_refs, mo_refs, vo_refs = (refs[(4 + k) * n_par:(5 + k) * n_par] for k in range(4))
        mine_ref, all_ref, send_sems, recv_sems = refs[8 * n_par:]
        x, y, c = _place()
        me, sibling = (x, y, c), (x, y, 1 - c)
        chips = [(1 - x, y), (x, 1 - y), (1 - x, 1 - y)]

        def slot(px, py, pc):
            return all_ref.at[4 * px + 2 * py + pc]

        def copy(k, block, to, src=None):
            return pltpu.make_async_remote_copy(
                src_ref=slot(*block) if src is None else src, dst_ref=slot(*block),
                send_sem=send_sems.at[k], recv_sem=recv_sems.at[k], device_id=to, device_id_type=MESH)

        mine_ref[...] = jnp.zeros_like(mine_ref)
        for p, row, off, width in chunks:
            mine_ref[row:row + 1, 0:width] = g_refs[p][:, off:off + width]
        all_ref[4 * x + 2 * y + c] = mine_ref[...]
        first = [copy(0, me, sibling, src=mine_ref)]
        first += [copy(1 + j, me, (*chip, c), src=mine_ref) for j, chip in enumerate(chips)]
        for cp in first:
            cp.start()
        passed = [copy(4 + j, (*chip, c), sibling) for j, chip in enumerate(chips)]
        for j, chip in enumerate(chips):
            copy(1 + j, (*chip, c), me).wait_recv()
            passed[j].start()
        copy(0, sibling, me).wait_recv()
        for j, chip in enumerate(chips):
            copy(4 + j, (*chip, 1 - c), me).wait_recv()
        for cp in first + passed:
            cp.wait_send()
        tot = jnp.zeros((rows, wd), F32)
        for dev in range(n_dev):
            tot = tot + all_ref[dev]
        mine_ref[...] = tot
        for p, row, off, width in chunks:
            cols = slice(off, off + width)
            g = mine_ref[row:row + 1, 0:width]
            delta, m_new, v_new = _adamw(w_refs[p][:, cols], g, m_refs[p][:, cols], v_refs[p][:, cols])
            go_refs[p][:, cols] = g
            d_refs[p][:, cols] = delta
            mo_refs[p][:, cols] = m_new
            vo_refs[p][:, cols] = v_new

    vm = pl.BlockSpec(memory_space=pltpu.VMEM)
    shp = [jax.ShapeDtypeStruct(g.shape, F32) for g in gs]
    res = pl.pallas_call(
        body, out_shape=shp * 4, in_specs=[vm] * (4 * n_par), out_specs=[vm] * (4 * n_par),
        scratch_shapes=[pltpu.VMEM((rows, wd), F32), pltpu.VMEM((n_dev, rows, wd), F32),
                        pltpu.SemaphoreType.DMA((7,)), pltpu.SemaphoreType.DMA((7,))],
        name="small_allreduce_adamw")(*gs, *ws, *ms, *vs)
    return [res[k * n_par:(k + 1) * n_par] for k in range(4)]


def kernel(x, positions, ln_pre_mix, ln_post_mix, ln_pre_mlp, ln_post_mlp, w_in, b_in, q_a_norm, w_uq, kv_a_norm, w_uk, w_uv, w_o_mla, w_o_fox, w_out, w_ff1, w_ff2, loss_target, m_ln_pre_mix, m_ln_post_mix, m_ln_pre_mlp, m_ln_post_mlp, m_w_in, m_b_in, m_q_a_norm, m_w_uq, m_kv_a_norm, m_w_uk, m_w_uv, m_w_o_mla, m_w_o_fox, m_w_out, m_w_ff1, m_w_ff2, v_ln_pre_mix, v_ln_post_mix, v_ln_pre_mlp, v_ln_post_mlp, v_w_in, v_b_in, v_q_a_norm, v_w_uq, v_kv_a_norm, v_w_uk, v_w_uv, v_w_o_mla, v_w_o_fox, v_w_out, v_w_ff1, v_w_ff2):
    w = dict(ln_pre_mix=ln_pre_mix, ln_post_mix=ln_post_mix, ln_pre_mlp=ln_pre_mlp, ln_post_mlp=ln_post_mlp, w_in=w_in,
             b_in=b_in, q_a_norm=q_a_norm, w_uq=w_uq, kv_a_norm=kv_a_norm, w_uk=w_uk, w_uv=w_uv, w_o_mla=w_o_mla,
             w_o_fox=w_o_fox, w_out=w_out, w_ff1=w_ff1, w_ff2=w_ff2)
    mom = dict(ln_pre_mix=m_ln_pre_mix, ln_post_mix=m_ln_post_mix, ln_pre_mlp=m_ln_pre_mlp, ln_post_mlp=m_ln_post_mlp,
               w_in=m_w_in, b_in=m_b_in, q_a_norm=m_q_a_norm, w_uq=m_w_uq, kv_a_norm=m_kv_a_norm, w_uk=m_w_uk,
               w_uv=m_w_uv, w_o_mla=m_w_o_mla, w_o_fox=m_w_o_fox, w_out=m_w_out, w_ff1=m_w_ff1, w_ff2=m_w_ff2)
    var = dict(ln_pre_mix=v_ln_pre_mix, ln_post_mix=v_ln_post_mix, ln_pre_mlp=v_ln_pre_mlp, ln_post_mlp=v_ln_post_mlp,
               w_in=v_w_in, b_in=v_b_in, q_a_norm=v_q_a_norm, w_uq=v_w_uq, kv_a_norm=v_kv_a_norm, w_uk=v_w_uk,
               w_uv=v_w_uv, w_o_mla=v_w_o_mla, w_o_fox=v_w_o_fox, w_out=v_w_out, w_ff1=v_w_ff1, w_ff2=v_w_ff2)

    big_names = [nm for nm, _ in BIG]
    c = lax.axis_index("c")
    chip = 2 * lax.axis_index("x") + lax.axis_index("y")

    axes = dict(BIG)

    def assemble(names, lay, gathered):
        return {nm: _from_shards(s4, axes[nm] - 1) for nm, s4 in zip(names, lay.unpack(gathered), strict=True)}

    groups = {"mixer": [nm for nm in big_names if nm in MIXER], "mlp": [nm for nm in big_names if nm in LATE]}
    first_names = [nm for nm in big_names if nm not in MIXER and nm not in LATE]
    full = {nm: wv for nm, wv in w.items() if nm in SMALL}
    lay_first = _PackLayout([w[nm].shape[1:] for nm in first_names])
    first = _gather_weights(lay_first.pack([w[nm][0].astype(BF16) for nm in first_names]))
    full.update(assemble(first_names, lay_first, first))
    travelling = {}
    for tag, names in groups.items():
        lay = _PackLayout([w[nm].shape[1:] for nm in names])
        shard = lay.pack([w[nm][0].astype(BF16) for nm in names])
        started = _gather_late_start(lax.optimization_barrier((shard, first))[0], tag)
        travelling[tag] = (names, lay, shard, started[:4])
        full["b_in"] = full["b_in"] + started[4][0, 0]

    def late_weights(tag, after):
        names, lay, shard, handles = travelling[tag]
        land = _gather_late_wait(*handles, after, tag)
        land = lax.dynamic_update_slice(land, shard[None], (chip, 0, 0))
        return assemble(names, lay, land)

    grad_groups = {"early": [nm for nm in big_names if nm in EARLY_GRADS],
                   "late": [nm for nm in big_names if nm not in EARLY_GRADS]}
    sent = {}

    def send_grads(tag, g):
        names = grad_groups[tag]
        lay = _PackLayout([w[nm].shape[1:] for nm in names])
        hrows = lay.rows // 2
        gbf = lay.pack([_to_shards(g[nm], axes[nm] - 1).astype(BF16) for nm in names])
        gbf = gbf.reshape(N_CHIPS, 2, hrows, lay.width)
        started = _reduce_start(gbf, tag)
        own = lax.dynamic_index_in_dim(lax.dynamic_index_in_dim(started[2], chip, axis=0, keepdims=False), c, axis=0,
                                       keepdims=False)
        sent[tag] = (names, lay, started[:4], own)
        return started[4][0, 0]

    loss_local, grad_x, grads = _local_step(x[0], positions[0], loss_target[0], full, late_weights, send_grads)

    def f_add8(ti, pa):
        tot = ti[0].astype(F32)
        for part in ti[1:]:
            tot = tot + part.astype(F32)
        return [tot], []

    reduced = []
    for tag, (names, lay, handles, own) in sent.items():
        parts = _reduce_wait(*handles, grad_x, tag)
        reduced.append(_rowwise("grad_add_" + tag, f_add8, [own] + [parts[n] for n in range(N_PART)], [],
                                [(lay.width, F32)])[0])

    assert len({lay.width for _, lay, _, _ in sent.values()}) == 1
    red = jnp.concatenate(reduced, axis=0)
    sib = _sibling_swap(red)
    lower, upper = jnp.where(c == 0, red, sib), jnp.where(c == 0, sib, red)
    g_by_name, row = {}, 0
    for names, lay, _, _ in sent.values():
        hrows = lay.rows // 2
        both = jnp.concatenate([lower[row:row + hrows], upper[row:row + hrows]], axis=0)
        g_by_name.update(zip(names, lay.unpack(both), strict=True))
        row += hrows
    g_shards = [g_by_name[nm] for nm in big_names]

    def f_adamw(ti, pa):
        wv, gv, mv, vv = ti
        return list(_adamw(wv, gv, mv, vv)), []

    out = {"grad": {}, "delta": {}, "m": {}, "v": {}}
    for nm, g_sh in zip(big_names, g_shards, strict=True):
        wd = g_sh.shape[1]
        d_sh, m_sh, v_sh = _rowwise("adamw_" + nm, f_adamw, [w[nm][0], g_sh, mom[nm][0], var[nm][0]], [], [(wd, F32)] * 3)
        out["grad"][nm], out["delta"][nm], out["m"][nm], out["v"][nm] = g_sh[None], d_sh[None], m_sh[None], v_sh[None]

    loss_row = jnp.zeros((1, LANES), F32) + loss_local
    blank = jnp.zeros((1, LANES), F32)
    small = _small_allreduce_adamw([grads[nm] for nm in SMALL] + [loss_row], [w[nm] for nm in SMALL] + [blank],
                                   [mom[nm] for nm in SMALL] + [blank], [var[nm] for nm in SMALL] + [blank])
    for kind, arrs in zip(("grad", "delta", "m", "v"), small, strict=True):
        for nm, arr in zip(SMALL, arrs[:len(SMALL)], strict=True):
            out[kind][nm] = arr
    loss = small[0][len(SMALL)][0, 0]

    return (loss, grad_x[None], *[out["grad"][nm] for nm in ALL_W], *[out["delta"][nm] for nm in ALL_W],
            *[out["m"][nm] for nm in ALL_W], *[out["v"][nm] for nm in ALL_W])
```

```python
import functools
import math

import jax
import jax.numpy as jnp
from jax import lax
from jax.experimental import pallas as pl
from jax.experimental.pallas import tpu as pltpu

F32 = jnp.float32
BF16 = jnp.bfloat16

MLA_HEADS = 8
MLA_Q_LORA = 256
MLA_KV_LORA = 128
MLA_NOPE = 64
MLA_ROPE = 32
MLA_V = 64
FOX_HEADS = 8
FOX_DIM = 64
ROPE_THETA = 10000.0
NORM_EPS = 1e-6
HALF_ROPE = MLA_ROPE // 2

ADAM_LR = 0.001
ADAM_B1 = 0.9
ADAM_B2 = 0.999
ADAM_EPS = 1e-08
ADAM_WD = 0.01
ADAM_STEP = 10

LANES = 128
VMEM_LIMIT = 56 * 1024 * 1024
ATT_TILE = 1024
FWD_GROUP_LOG2 = 1
FWD_GROUP = 1 << FWD_GROUP_LOG2
MM_VMEM_BUDGET = 40 * 1024 * 1024
ROW_TILE_BYTES = 12 * 1024 * 1024
MXU_WIDTH = 256
MXU_MACS_PER_S = 4.98e14
HBM_BYTES_PER_S = 3.2e12
STEP_OVERHEAD_S = 0.35e-6
NEG = -1e30
LOG2E = math.log2(math.e)
MESH = pl.DeviceIdType.MESH

V_ONES = 64
FOX_Q_F = 64
FOX_Q_L = 67
FOX_Q_ONES = 70
MLA_Q_L = 96

BIG = (("w_in", 2), ("w_uq", 2), ("w_uk", 2), ("w_uv", 2), ("w_o_mla", 2), ("w_o_fox", 2),
       ("w_out", 1), ("w_ff1", 2), ("w_ff2", 1))
MIXER = ("w_uq", "w_uk", "w_uv", "w_o_mla", "w_o_fox")
LATE = ("w_out", "w_ff1", "w_ff2")
EARLY_GRADS = ("w_o_mla", "w_o_fox", "w_out", "w_ff1", "w_ff2")
SMALL = ("ln_pre_mix", "ln_post_mix", "ln_pre_mlp", "ln_post_mlp", "b_in", "q_a_norm", "kv_a_norm")
ALL_W = ("ln_pre_mix", "ln_post_mix", "ln_pre_mlp", "ln_post_mlp", "w_in", "b_in", "q_a_norm", "w_uq",
         "kv_a_norm", "w_uk", "w_uv", "w_o_mla", "w_o_fox", "w_out", "w_ff1", "w_ff2")
N_CHIPS = 4
PACK_W = 1024
LOCAL_PIECES = 8

_NT = (((1,), (1,)), ((), ()))
_TN = (((0,), (0,)), ((), ()))


def _cparams(sem=None):
    return pltpu.CompilerParams(dimension_semantics=sem, vmem_limit_bytes=VMEM_LIMIT)


def _divisor_tile(n, limit, mult):
    if n <= limit:
        return n
    best = None
    t = mult
    while t <= limit:
        if n % t == 0:
            best = t
        t += mult
    assert best is not None, (n, limit, mult)
    return best


def _round_up(v, mult):
    return -(-v // mult) * mult


def _mm_tiles(m, k, n, io_bytes):
    best = None
    for tm in (2048, 1024, 512, 256, 128):
        if m % tm:
            continue
        for tn in range(LANES, min(n, 2048) + 1, LANES):
            if n % tn:
                continue
            vmem = 2 * (tm * k * 2 + k * tn * 2 + tm * tn * io_bytes) + tm * tn * 4
            if vmem > MM_VMEM_BUDGET:
                continue
            mxu = m * k * n * (_round_up(tn, MXU_WIDTH) / tn) / MXU_MACS_PER_S
            hbm = (m * k * 2 + (m // tm) * k * n * 2 + m * n * io_bytes) / HBM_BYTES_PER_S
            cost = max(mxu, hbm) + (m // tm) * (n // tn) * STEP_OVERHEAD_S
            if best is None or cost < best[0]:
                best = (cost, tm, tn)
    assert best is not None, (m, k, n)
    return best[1], best[2]


def _mm(a, b, *, out_dtype, name, bias=None, transpose_b=False, extras=(), epilogue=None):
    m, k = a.shape
    n = b.shape[0] if transpose_b else b.shape[1]
    assert (b.shape[1] if transpose_b else b.shape[0]) == k and a.dtype == BF16 and b.dtype == BF16
    out_dtypes = list(out_dtype) if isinstance(out_dtype, (list, tuple)) else [out_dtype]
    n_ex = len(extras)
    tm, tn = _mm_tiles(m, k, n, sum(jnp.dtype(dt).itemsize for dt in out_dtypes) + 4 * n_ex)

    def body(*refs):
        a_ref, b_ref = refs[:2]
        pos = 2
        bias_ref = None
        if bias is not None:
            bias_ref = refs[pos]
            pos += 1
        ex_refs = refs[pos:pos + n_ex]
        o_refs = refs[pos + n_ex:]
        if transpose_b:
            acc = lax.dot_general(a_ref[...], b_ref[...], _NT, preferred_element_type=F32)
        else:
            acc = jnp.dot(a_ref[...], b_ref[...], preferred_element_type=F32)
        if bias_ref is not None:
            acc = acc + bias_ref[...]
        vals = [acc] if epilogue is None else epilogue(acc, [r[...] for r in ex_refs])
        for ref, val in zip(o_refs, vals, strict=True):
            ref[...] = val.astype(ref.dtype)

    b_spec = pl.BlockSpec((tn, k), lambda i, j: (j, 0)) if transpose_b else pl.BlockSpec((k, tn), lambda i, j: (0, j))
    in_specs = [pl.BlockSpec((tm, k), lambda i, j: (i, 0)), b_spec]
    args = [a, b]
    if bias is not None:
        in_specs.append(pl.BlockSpec((1, tn), lambda i, j: (0, j)))
        args.append(bias)
    in_specs += [pl.BlockSpec((tm, tn), lambda i, j: (i, j)) for _ in extras]
    args += list(extras)
    res = pl.pallas_call(
        body, grid=(m // tm, n // tn), in_specs=in_specs,
        out_specs=[pl.BlockSpec((tm, tn), lambda i, j: (i, j)) for _ in out_dtypes],
        out_shape=[jax.ShapeDtypeStruct((m, n), dt) for dt in out_dtypes],
        compiler_params=_cparams(("parallel", "parallel")), name=name)(*args)
    return res if isinstance(out_dtype, (list, tuple)) else res[0]


def _mm_tn(a, b, *, name):
    s, m = a.shape
    s2, n = b.shape
    assert s == s2 and a.dtype == BF16 and b.dtype == BF16
    tm = _divisor_tile(m, 1024, LANES)
    tn = _divisor_tile(n, 2304, LANES)
    tk = _divisor_tile(s, 512, 16)
    nk = s // tk

    def body(a_ref, b_ref, o_ref, acc_ref):
        @pl.when(pl.program_id(2) == 0)
        def _():
            acc_ref[...] = jnp.zeros_like(acc_ref)

        acc_ref[...] += lax.dot_general(a_ref[...], b_ref[...], _TN, preferred_element_type=F32)

        @pl.when(pl.program_id(2) == nk - 1)
        def _():
            o_ref[...] = acc_ref[...].astype(o_ref.dtype)

    return pl.pallas_call(
        body, grid=(m // tm, n // tn, nk),
        in_specs=[pl.BlockSpec((tk, tm), lambda i, j, k: (k, i)), pl.BlockSpec((tk, tn), lambda i, j, k: (k, j))],
        out_specs=pl.BlockSpec((tm, tn), lambda i, j, k: (i, j)),
        out_shape=jax.ShapeDtypeStruct((m, n), BF16),
        scratch_shapes=[pltpu.VMEM((tm, tn), F32)],
        compiler_params=_cparams(("parallel", "parallel", "arbitrary")), name=name)(a, b)


def _rowwise(name, fn, tiled, params, outs, reds=(), reverse=False):
    wins = [t if isinstance(t, tuple) else (t, 0, t.shape[1]) for t in tiled]
    s = wins[0][0].shape[0]
    row_bytes = sum(w * arr.dtype.itemsize for arr, _, w in wins) + sum(w * jnp.dtype(d).itemsize for w, d in outs)
    ts = _divisor_tile(s, max(16, min(1024, ROW_TILE_BYTES // row_bytes)), 16)
    nt, npar, nout = len(wins), len(params), len(outs)
    n_tiles = s // ts

    def row(i):
        return n_tiles - 1 - i if reverse else i

    def body(*refs):
        tin = [r[...] for r in refs[:nt]]
        par = [r[...] for r in refs[nt:nt + npar]]
        out_refs = refs[nt + npar:nt + npar + nout]
        red_refs = refs[nt + npar + nout:]
        o, r = fn(tin, par)
        for ref, val in zip(out_refs, o, strict=True):
            ref[...] = val.astype(ref.dtype)
        if red_refs:
            @pl.when(pl.program_id(0) == 0)
            def _():
                for ref in red_refs:
                    ref[...] = jnp.zeros_like(ref)

            for ref, val in zip(red_refs, r, strict=True):
                ref[...] += val

    in_specs = [pl.BlockSpec((ts, w), functools.partial(lambda i, cb: (row(i), cb), cb=cb)) for _, cb, w in wins]
    in_specs += [pl.BlockSpec(p.shape, lambda i: (0, 0)) for p in params]
    out_specs = [pl.BlockSpec((ts, w), lambda i: (row(i), 0)) for w, _ in outs]
    out_specs += [pl.BlockSpec((1, w), lambda i: (0, 0)) for w in reds]
    out_shape = [jax.ShapeDtypeStruct((s, w), d) for w, d in outs]
    out_shape += [jax.ShapeDtypeStruct((1, w), F32) for w in reds]
    return pl.pallas_call(
        body, grid=(n_tiles,), in_specs=in_specs, out_specs=out_specs, out_shape=out_shape,
        compiler_params=_cparams(("arbitrary",)), name=name)(*[w[0] for w in wins], *params)


def _rms(x, g):
    r = lax.rsqrt(jnp.mean(x * x, axis=-1, keepdims=True) + NORM_EPS)
    return x * r * g, r


def _rms_bwd(x, g, dy):
    r = lax.rsqrt(jnp.mean(x * x, axis=-1, keepdims=True) + NORM_EPS)
    gy = dy * g
    dx = r * gy - x * (r * r * r) * jnp.mean(x * gy, axis=-1, keepdims=True)
    dg = jnp.sum(dy * (x * r), axis=0, keepdims=True)
    return dx, dg


def _sigmoid(x):
    return 1.0 / (1.0 + jnp.exp(-x))


def _split3(x):
    hi = x.astype(BF16).astype(F32)
    r = x - hi
    mid = r.astype(BF16).astype(F32)
    lo = (r - mid).astype(BF16).astype(F32)
    return hi, mid, lo


def _lane(shape):
    return lax.broadcasted_iota(jnp.int32, shape, 1)


def _put3(blk, lane, pos, pieces):
    for k, piece in enumerate(pieces):
        blk = jnp.where(lane == pos + k, piece, blk)
    return blk


def _lane_column(blk, lane, pos):
    return jnp.sum(jnp.where(lane == pos, blk, 0.0), axis=1, keepdims=True)


def _blocks(a, nh):
    return [a[:, h * LANES:(h + 1) * LANES] for h in range(nh)]


def _head_spread(nh, d):
    shift = d.bit_length() - 1
    assert 1 << shift == d
    r = lax.broadcasted_iota(jnp.int32, (nh * d, nh * LANES), 0)
    c = lax.broadcasted_iota(jnp.int32, (nh * d, nh * LANES), 1)
    return jnp.where(c == (r >> shift) * LANES + (r & (d - 1)), 1.0, 0.0).astype(BF16)


def _rope_block(x, c, sa, sb):
    return x * c + pltpu.roll(x, LANES - HALF_ROPE, 1) * sa + pltpu.roll(x, HALF_ROPE, 1) * sb


def _forget_cumsum(z, cb):
    s = z.shape[0]
    ts = _divisor_tile(s, 512, LANES)

    def body(x_ref, col_ref, carry):
        @pl.when(pl.program_id(0) == 0)
        def _():
            carry[...] = jnp.zeros_like(carry)

        x = x_ref[...]
        lf = jnp.minimum(x, 0.0) - jnp.log1p(jnp.exp(-jnp.abs(x)))
        r = lax.broadcasted_iota(jnp.int32, (ts, ts), 0)
        c = lax.broadcasted_iota(jnp.int32, (ts, ts), 1)
        tri = jnp.where(c <= r, 1.0, 0.0).astype(F32)
        col_ref[...] = jnp.dot(tri, lf, preferred_element_type=F32, precision=lax.Precision.HIGHEST) + carry[...]
        carry[...] += jnp.sum(lf, axis=0, keepdims=True)

    return pl.pallas_call(
        body, grid=(s // ts,),
        in_specs=[pl.BlockSpec((ts, LANES), lambda i: (i, cb))],
        out_specs=pl.BlockSpec((ts, LANES), lambda i: (i, 0)),
        out_shape=jax.ShapeDtypeStruct((s, LANES), F32),
        scratch_shapes=[pltpu.VMEM((1, LANES), F32)],
        compiler_params=_cparams(("arbitrary",)), name="forget_cumsum")(z)


def _forget_cumsum_bwd(dq, dk, z, cb, nh):
    s = z.shape[0]
    ts = _divisor_tile(s, 512, LANES)
    nt = s // ts
    wd = nh * LANES

    def body(dq_ref, dk_ref, x_ref, o_ref, carry):
        @pl.when(pl.program_id(0) == 0)
        def _():
            carry[...] = jnp.zeros_like(carry)

        lane = _lane((ts, LANES))
        df = jnp.zeros((ts, LANES), F32)
        for h in range(nh):
            cols = slice(h * LANES, (h + 1) * LANES)
            d_h = _lane_column(dq_ref[:, cols], lane, FOX_Q_F) - _lane_column(dk_ref[:, cols], lane, FOX_Q_ONES)
            df = jnp.where(lane == h, d_h, df)
        r = lax.broadcasted_iota(jnp.int32, (ts, ts), 0)
        c = lax.broadcasted_iota(jnp.int32, (ts, ts), 1)
        tri = jnp.where(c >= r, 1.0, 0.0).astype(F32)
        rc = jnp.dot(tri, df, preferred_element_type=F32, precision=lax.Precision.HIGHEST) + carry[...]
        carry[...] += jnp.sum(df, axis=0, keepdims=True)
        o_ref[...] = (rc * (1.0 / (1.0 + jnp.exp(x_ref[...])))).astype(o_ref.dtype)

    return pl.pallas_call(
        body, grid=(nt,),
        in_specs=[pl.BlockSpec((ts, wd), lambda i: (nt - 1 - i, 0)),
                  pl.BlockSpec((ts, wd), lambda i: (nt - 1 - i, 0)),
                  pl.BlockSpec((ts, LANES), lambda i: (nt - 1 - i, cb))],
        out_specs=pl.BlockSpec((ts, LANES), lambda i: (nt - 1 - i, 0)),
        out_shape=jax.ShapeDtypeStruct((s, LANES), BF16),
        scratch_shapes=[pltpu.VMEM((1, LANES), F32)],
        compiler_params=_cparams(("arbitrary",)), name="forget_cumsum_bwd")(dq, dk, z)


def _flash_fwd(q, k, v, scale, nh, l_lane, *, name):
    s = q.shape[0]
    t = min(ATT_TILE, s)
    half = t // 2 if t % (2 * LANES) == 0 else t
    nq = s // t
    c = scale * LOG2E

    def body(q_ref, k_ref, v_ref, o_ref, qb_ref):
        i = pl.program_id(1)
        qb = q_ref[...]

        def scores(q_rows, k0, nk):
            kb = k_ref[pl.ds(pl.multiple_of(k0, half), nk), :]
            return lax.dot_general(q_rows, kb, _NT, preferred_element_type=F32)

        def update(sc, k0, nk, carry):
            m, acc = carry
            m_new = jnp.maximum(m, jnp.max(sc, axis=1, keepdims=True))
            p = jnp.exp2((sc - m_new) * c)
            alpha = jnp.exp2((m - m_new) * c)
            vb = v_ref[pl.ds(pl.multiple_of(k0, half), nk), :]
            acc = alpha * acc + jnp.dot(p.astype(BF16), vb, preferred_element_type=F32)
            return m_new, acc

        def full_block(j, cr):
            return update(scores(qb, j * t, t), j * t, t, cr)

        def group(jj, cr):
            for n in range(FWD_GROUP):
                cr = full_block(FWD_GROUP * jj + n, cr)
            return cr

        def causal(sc):
            row = lax.broadcasted_iota(jnp.int32, sc.shape, 0)
            col = lax.broadcasted_iota(jnp.int32, sc.shape, 1)
            return jnp.where(col <= row, sc, NEG)

        init = (jnp.full((t, 1), NEG, F32), jnp.zeros((t, LANES), F32))
        n_groups = i >> FWD_GROUP_LOG2
        carry = lax.fori_loop(0, n_groups, group, init)
        carry = lax.fori_loop(n_groups * FWD_GROUP, i, full_block, carry)
        m, acc = update(causal(scores(qb, i * t, t)), i * t, t, carry)
        lane = _lane((t, LANES))
        l = _lane_column(acc, lane, V_ONES)
        o_ref[...] = (acc / l).astype(o_ref.dtype)
        big_l = m + jnp.log(l) / scale
        qb_ref[...] = _put3(qb.astype(F32), lane, l_lane, _split3(-big_l)).astype(qb_ref.dtype)

    head_rows = pl.BlockSpec((t, LANES), lambda h, i: (i, h))
    head_all = pl.BlockSpec((s, LANES), lambda h, i: (0, h))
    return pl.pallas_call(
        body, grid=(nh, nq), in_specs=[head_rows, head_all, head_all], out_specs=[head_rows, head_rows],
        out_shape=[jax.ShapeDtypeStruct(q.shape, BF16), jax.ShapeDtypeStruct(q.shape, BF16)],
        compiler_params=_cparams(("parallel", "arbitrary")), name=name)(q, k, v)


def _flash_bwd(qb, k, v, do, scale, nh, n_feat, *, name):
    s = qb.shape[0]
    t = min(ATT_TILE, s)
    half = t // 2 if t % (2 * LANES) == 0 else t
    nq = s // t
    c = scale * LOG2E

    def body(q_ref, k_ref, v_ref, do_ref, dk_ref, dv_ref, dq_ref):
        j = pl.program_id(1)
        kb = k_ref[...]
        vb = v_ref[...]

        @pl.when(j == 0)
        def _():
            dq_ref[...] = jnp.zeros_like(dq_ref)

        def part(q0, n_q, n_k, carry, q_off):
            dk_acc, dv_acc = carry
            rows = pl.ds(pl.multiple_of(q0, half), n_q)
            qblk = q_ref[rows, :]
            dob = do_ref[rows, :]
            kbb, vbb = kb[:n_k], vb[:n_k]
            st = lax.dot_general(kbb, qblk, _NT, preferred_element_type=F32)
            if q_off is not None:
                key = lax.broadcasted_iota(jnp.int32, st.shape, 0)
                qry = lax.broadcasted_iota(jnp.int32, st.shape, 1) + q_off
                st = jnp.where(key <= qry, st, NEG)
            pt = jnp.exp2(st * c)
            dv_new = jnp.dot(pt.astype(BF16), dob, preferred_element_type=F32)
            dpt = lax.dot_general(vbb, dob, _NT, preferred_element_type=F32)
            dsb = (pt * dpt).astype(BF16)
            dk_new = jnp.dot(dsb, qblk, preferred_element_type=F32)
            dq_ref[rows, :] += lax.dot_general(dsb, kbb, _TN, preferred_element_type=F32)
            if n_k == t:
                return dk_acc + dk_new, dv_acc + dv_new
            return (jnp.concatenate([dk_acc[:n_k] + dk_new, dk_acc[n_k:]], axis=0),
                    jnp.concatenate([dv_acc[:n_k] + dv_new, dv_acc[n_k:]], axis=0))

        def block(i, carry):
            return part(i * t, t, t, carry, None)

        init = (jnp.zeros((t, LANES), F32), jnp.zeros((t, LANES), F32))
        carry = part(j * t + half, t - half, t, init, half) if half < t else part(j * t, t, t, init, 0)
        if half < t:
            carry = part(j * t, half, half, carry, 0)
        rest = nq - 1 - j
        carry = lax.cond((rest & 1) == 1, lambda cr: block(j + 1, cr), lambda cr: cr, carry)
        first = j + 1 + (rest & 1)

        def pair(ii, cr):
            i0 = first + 2 * ii
            return block(i0 + 1, block(i0, cr))

        dk_acc, dv_acc = lax.fori_loop(0, rest >> 1, pair, carry)
        dk_ref[...] = dk_acc * jnp.where(_lane((t, LANES)) < n_feat, scale, 1.0)
        dv_ref[...] = dv_acc.astype(dv_ref.dtype)

        @pl.when(j == nq - 1)
        def _():
            dq_ref[...] = dq_ref[...] * jnp.where(_lane((s, LANES)) < n_feat, scale, 1.0)

    head_rows = pl.BlockSpec((t, LANES), lambda h, j: (j, h))
    head_all = pl.BlockSpec((s, LANES), lambda h, j: (0, h))
    shp = jax.ShapeDtypeStruct(qb.shape, F32)
    return pl.pallas_call(
        body, grid=(nh, nq), in_specs=[head_all, head_rows, head_rows, head_all],
        out_specs=[head_rows, head_rows, head_all],
        out_shape=[shp, jax.ShapeDtypeStruct(qb.shape, BF16), shp],
        compiler_params=_cparams(("parallel", "arbitrary")), name=name)(qb, k, v, do)


def _with_delta(do, o, nh, *, name):
    def fn(ti, pa):
        lane = _lane((ti[0].shape[0], LANES))
        out = []
        for d_blk, o_blk in zip(_blocks(ti[0], nh), _blocks(ti[1], nh), strict=True):
            delta = jnp.sum(d_blk * o_blk.astype(F32), axis=1, keepdims=True)
            out.append(_put3(d_blk, lane, V_ONES, _split3(-delta)))
        return [jnp.concatenate(out, axis=1)], []

    (res,) = _rowwise(name, fn, [do, o], [], [(do.shape[1], BF16)])
    return res


def _pad_heads(a, nh):
    d = a.shape[-1] // nh
    a = a.reshape(a.shape[:-1] + (nh, d))
    a = jnp.pad(a, [(0, 0)] * (a.ndim - 1) + [(0, LANES - d)])
    return a.reshape(a.shape[:-2] + (nh * LANES,))


def _unpad_heads(a, nh, d):
    a = a.reshape(a.shape[:-1] + (nh, LANES))[..., :d]
    return a.reshape(a.shape[:-2] + (nh * d,))


def _pad_head_rows(w, nh):
    return _pad_heads(w.T, nh).T


def _unpad_head_rows(g, nh, d):
    return _unpad_heads(g.T, nh, d).T


class _ZLayout:
    def __init__(self, d):
        fw = FOX_HEADS * FOX_DIM
        self.src = {}
        off = 0
        for nm, w in (("cq", MLA_Q_LORA), ("ckv", MLA_KV_LORA), ("kr", MLA_ROPE), ("fq", fw), ("fk", fw),
                      ("fv", fw), ("fl", FOX_HEADS), ("ga", d), ("gb", d)):
            self.src[nm] = (off, w)
            off += w
        self.dst = {}
        off = 0
        for nm, w in (("ga", d), ("gb", d), ("fq", fw), ("fk", fw), ("fv", fw), ("cq", MLA_Q_LORA),
                      ("ckv", MLA_KV_LORA), ("kr", LANES), ("fl", LANES)):
            assert off % w == 0
            self.dst[nm] = (off, w)
            off += w
        self.width = off
        self.split = self.dst["cq"][0]
        assert all((o - self.split) % w == 0 for o, w in self.dst.values() if o >= self.split)

    def to_kernel(self, w):
        def seg(nm):
            off, wd = self.src[nm]
            return w[..., off:off + wd]

        def pad(a, left, total):
            return jnp.pad(a, [(0, 0)] * (a.ndim - 1) + [(left, total - left - a.shape[-1])])

        return jnp.concatenate([seg("ga"), seg("gb"), seg("fq"), seg("fk"), seg("fv"), seg("cq"), seg("ckv"),
                                pad(seg("kr"), MLA_NOPE, LANES), pad(seg("fl"), 0, LANES)], axis=-1)

    def from_kernel(self, g):
        def seg(nm, lo=0, hi=None):
            off, wd = self.dst[nm]
            return g[..., off + lo:off + (wd if hi is None else hi)]

        return jnp.concatenate([seg("cq"), seg("ckv"), seg("kr", MLA_NOPE, MLA_NOPE + MLA_ROPE), seg("fq"), seg("fk"),
                                seg("fv"), seg("fl", 0, FOX_HEADS), seg("ga"), seg("gb")], axis=-1)


def _local_step(x, positions, target, wts, late_weights, send_grads):
    s, d = x.shape
    zl = _ZLayout(d)
    hw = MLA_HEADS * LANES
    assert MLA_HEADS == FOX_HEADS
    scale_mla = (MLA_NOPE + MLA_ROPE) ** -0.5
    scale_fox = FOX_DIM ** -0.5

    inv_freq = ROPE_THETA ** (-jnp.arange(HALF_ROPE, dtype=F32) / HALF_ROPE)
    ang = positions.astype(F32)[:, None] * inv_freq
    cos, sin = jnp.cos(ang), jnp.sin(ang)
    tail = jnp.zeros((s, LANES - MLA_NOPE - MLA_ROPE), F32)
    rc = jnp.concatenate([jnp.ones((s, MLA_NOPE), F32), cos, cos, tail], axis=1)
    ra = jnp.concatenate([jnp.zeros((s, MLA_NOPE), F32), -sin, jnp.zeros((s, HALF_ROPE), F32), tail], axis=1)
    rb = jnp.concatenate([jnp.zeros((s, MLA_NOPE + HALF_ROPE), F32), sin, tail], axis=1)

    w_in = zl.to_kernel(wts["w_in"])
    b_in = zl.to_kernel(wts["b_in"])

    def f_norm_in(ti, pa):
        y, _ = _rms(ti[0], pa[0])
        return [y], []

    (h,) = _rowwise("norm_in", f_norm_in, [x], [wts["ln_pre_mix"]], [(d, BF16)])
    z_lo = _mm(h, w_in[:, :zl.split], bias=b_in[:, :zl.split], out_dtype=BF16, name="proj_in_lo")
    z = _mm(h, w_in[:, zl.split:], bias=b_in[:, zl.split:], out_dtype=F32, name="proj_in_hi")

    def zwin(nm):
        off, wd = zl.dst[nm]
        return (z_lo, off // wd, wd) if off < zl.split else (z, (off - zl.split) // wd, wd)

    wts = {**wts, **late_weights("mixer", z)}
    w_uq = _pad_heads(wts["w_uq"], MLA_HEADS)
    w_ukv = jnp.concatenate([_pad_heads(wts["w_uk"], MLA_HEADS), _pad_heads(wts["w_uv"], MLA_HEADS)], axis=1)
    w_o_mla = _pad_head_rows(wts["w_o_mla"], MLA_HEADS)
    w_o_fox = _pad_head_rows(wts["w_o_fox"], FOX_HEADS)

    def f_mla_norms(ti, pa):
        cqn, _ = _rms(ti[0], pa[0])
        ckvn, _ = _rms(ti[1], pa[1])
        return [cqn, ckvn], []

    cqn, ckvn = _rowwise("mla_norms", f_mla_norms, [zwin("cq"), zwin("ckv")],
                         [wts["q_a_norm"], wts["kv_a_norm"]], [(MLA_Q_LORA, BF16), (MLA_KV_LORA, BF16)])
    qf = _mm(cqn, w_uq, out_dtype=F32, name="proj_uq")
    kv = _mm(ckvn, w_ukv, out_dtype=BF16, name="proj_ukv")

    def f_rope_q(ti, pa):
        xq, c_, a_, b_ = ti
        return [jnp.concatenate([_rope_block(blk, c_, a_, b_) for blk in _blocks(xq, MLA_HEADS)], axis=1)], []

    (q_mla,) = _rowwise("rope_q", f_rope_q, [qf, rc, ra, rb], [], [(hw, BF16)])

    def f_mla_kv(ti, pa):
        kn, vn, kr, c_, a_, b_ = ti
        lane = _lane(kr.shape)
        k_tail = jnp.where((lane >= MLA_Q_L) & (lane < MLA_Q_L + 3), 1.0, _rope_block(kr, c_, a_, b_))
        ones_v = (lane >= V_ONES) & (lane < V_ONES + 3)
        k_out = [jnp.where(lane < MLA_NOPE, blk.astype(F32), k_tail) for blk in _blocks(kn, MLA_HEADS)]
        v_out = [jnp.where(ones_v, 1.0, blk.astype(F32)) for blk in _blocks(vn, MLA_HEADS)]
        return [jnp.concatenate(k_out, axis=1), jnp.concatenate(v_out, axis=1)], []

    k_mla, v_mla = _rowwise("mla_kv", f_mla_kv, [(kv, 0, hw), (kv, 1, hw), zwin("kr"), rc, ra, rb], [],
                            [(hw, BF16), (hw, BF16)])
    o_mla, qb_mla = _flash_fwd(q_mla, k_mla, v_mla, scale_mla, MLA_HEADS, MLA_Q_L, name="mla_fwd")

    fl_cb = zwin("fl")[1]
    fcol = _forget_cumsum(z, fl_cb)

    def f_fox_qkv(ti, pa):
        spread = _head_spread(FOX_HEADS, FOX_DIM)
        fq, fk, fv = (jnp.dot(a, spread, preferred_element_type=F32) for a in ti[:3])
        fc = ti[3]
        lane = _lane(fc.shape)
        ones_q = (lane >= FOX_Q_ONES) & (lane < FOX_Q_ONES + 3)
        ones_k = (lane >= FOX_Q_F) & (lane < FOX_Q_ONES)
        ones_v = (lane >= V_ONES) & (lane < V_ONES + 3)
        q_out, k_out, v_out = [], [], []
        for hh, (qblk, kblk, vblk) in enumerate(zip(_blocks(fq, FOX_HEADS), _blocks(fk, FOX_HEADS),
                                                    _blocks(fv, FOX_HEADS), strict=True)):
            f_h = _lane_column(fc, lane, hh) * (1.0 / scale_fox)
            q_out.append(_put3(jnp.where(ones_q, 1.0, qblk), lane, FOX_Q_F, _split3(f_h)))
            k_out.append(_put3(jnp.where(ones_k, 1.0, kblk), lane, FOX_Q_ONES, _split3(-f_h)))
            v_out.append(jnp.where(ones_v, 1.0, vblk))
        return [jnp.concatenate(q_out, axis=1), jnp.concatenate(k_out, axis=1), jnp.concatenate(v_out, axis=1)], []

    q_fox, k_fox, v_fox = _rowwise("fox_qkv", f_fox_qkv, [zwin("fq"), zwin("fk"), zwin("fv"), fcol],
                                   [], [(hw, BF16)] * 3)
    o_fox, qb_fox = _flash_fwd(q_fox, k_fox, v_fox, scale_fox, FOX_HEADS, FOX_Q_L, name="fox_fwd")

    y_mla = _mm(o_mla, w_o_mla, out_dtype=BF16, name="proj_o_mla")
    y_fox = _mm(o_fox, w_o_fox, out_dtype=BF16, name="proj_o_fox")

    def f_gate(ti, pa):
        ga, gb, ya, yb = (a.astype(F32) for a in ti)
        return [_sigmoid(ga) * ya + _sigmoid(gb) * yb], []

    (merged,) = _rowwise("gate", f_gate, [zwin("ga"), zwin("gb"), y_mla, y_fox], [], [(d, BF16)])
    wts = {**wts, **late_weights("mlp", merged)}
    mix = _mm(merged, wts["w_out"], out_dtype=F32, name="proj_out")

    def f_resid1(ti, pa):
        xa, mx = ti
        y, _ = _rms(mx, pa[0])
        x1 = xa + y
        h2, _ = _rms(x1, pa[1])
        return [x1, h2], []

    x1, h2 = _rowwise("resid_mix", f_resid1, [x, mix], [wts["ln_post_mix"], wts["ln_pre_mlp"]], [(d, F32), (d, BF16)])

    def relu2(acc, ex):
        r = jnp.maximum(acc, 0.0)
        return [acc, r * r]

    u, act = _mm(h2, wts["w_ff1"], out_dtype=[BF16, BF16], name="ff1", epilogue=relu2)
    mo = _mm(act, wts["w_ff2"], out_dtype=F32, name="ff2")

    def f_loss(ti, pa):
        xa, mv, tg = ti
        y, _ = _rms(mv, pa[0])
        err = (xa + y) - tg
        g2 = err / d
        dmo, dg = _rms_bwd(mv, pa[0], g2)
        return [g2, dmo], [jnp.sum(err * err, axis=0, keepdims=True), dg]

    g2, d_mo, loss_cols, g_ln_post_mlp = _rowwise("loss", f_loss, [x1, mo, target], [wts["ln_post_mlp"]],
                                                  [(d, F32), (d, BF16)], [d, d])
    loss = 0.5 * jnp.sum(loss_cols) / d

    grads = {"ln_post_mlp": g_ln_post_mlp}
    grads["w_ff2"] = _mm_tn(act, d_mo, name="grad_ff2")

    def relu2_bwd(acc, ex):
        return [acc * (2.0 * jnp.maximum(ex[0], 0.0))]

    (d_u,) = _mm(d_mo, wts["w_ff2"], out_dtype=[BF16], name="ff2_bwd", transpose_b=True, extras=[u], epilogue=relu2_bwd)
    grads["w_ff1"] = _mm_tn(h2, d_u, name="grad_ff1")
    d_h2 = _mm(d_u, wts["w_ff1"], out_dtype=F32, name="ff1_bwd", transpose_b=True)

    def f_resid1_bwd(ti, pa):
        gres, dh2, x1v, mx = ti
        dx1n, dg_pre_mlp = _rms_bwd(x1v, pa[1], dh2)
        dx1 = gres + dx1n
        dmix, dg_post_mix = _rms_bwd(mx, pa[0], dx1)
        return [dx1, dmix], [dg_post_mix, dg_pre_mlp]

    d_x1, d_mix, grads["ln_post_mix"], grads["ln_pre_mlp"] = _rowwise(
        "resid_mix_bwd", f_resid1_bwd, [g2, d_h2, x1, mix], [wts["ln_post_mix"], wts["ln_pre_mlp"]],
        [(d, F32), (d, BF16)], [d, d])
    grads["w_out"] = _mm_tn(merged, d_mix, name="grad_out")
    d_merged = _mm(d_mix, wts["w_out"], out_dtype=BF16, name="proj_out_bwd", transpose_b=True)

    def f_gate_bwd(ti, pa):
        dm, ga, gb, ya, yb = (a.astype(F32) for a in ti)
        sa, sb = _sigmoid(ga), _sigmoid(gb)
        return [dm * sa, dm * sb, dm * ya * (sa * (1.0 - sa)), dm * yb * (sb * (1.0 - sb))], []

    d_ya, d_yb, d_ga, d_gb = _rowwise("gate_bwd", f_gate_bwd,
                                      [d_merged, zwin("ga"), zwin("gb"), y_mla, y_fox], [],
                                      [(d, BF16)] * 4)
    grads["w_o_mla"] = _unpad_head_rows(_mm_tn(o_mla, d_ya, name="grad_o_mla"), MLA_HEADS, MLA_V)
    grads["w_o_fox"] = _unpad_head_rows(_mm_tn(o_fox, d_yb, name="grad_o_fox"), FOX_HEADS, FOX_DIM)
    w_o_mla = w_o_mla + send_grads("early", {nm: grads[nm] for nm in EARLY_GRADS}).astype(BF16)
    do_mla = _with_delta(_mm(d_ya, w_o_mla, out_dtype=F32, name="proj_o_mla_bwd", transpose_b=True), o_mla,
                         MLA_HEADS, name="mla_delta")
    do_fox = _with_delta(_mm(d_yb, w_o_fox, out_dtype=F32, name="proj_o_fox_bwd", transpose_b=True), o_fox,
                         FOX_HEADS, name="fox_delta")

    dk_mla, dv_mla, dq_mla = _flash_bwd(qb_mla, k_mla, v_mla, do_mla, scale_mla, MLA_HEADS, MLA_NOPE + MLA_ROPE,
                                        name="mla_bwd")
    dk_fox, dv_fox, dq_fox = _flash_bwd(qb_fox, k_fox, v_fox, do_fox, scale_fox, FOX_HEADS, FOX_DIM, name="fox_bwd")
    d_fl = _forget_cumsum_bwd(dq_fox, dk_fox, z, fl_cb, FOX_HEADS)

    def f_rope_q_bwd(ti, pa):
        g, c_, a_, b_ = ti
        return [jnp.concatenate([_rope_block(blk, c_, -a_, -b_) for blk in _blocks(g, MLA_HEADS)], axis=1)], []

    (d_qf,) = _rowwise("rope_q_bwd", f_rope_q_bwd, [dq_mla, rc, ra, rb], [], [(hw, BF16)])
    grads["w_uq"] = _unpad_heads(_mm_tn(cqn, d_qf, name="grad_uq"), MLA_HEADS, MLA_NOPE + MLA_ROPE)
    d_cqn = _mm(d_qf, w_uq, out_dtype=F32, name="proj_uq_bwd", transpose_b=True)

    def f_mla_kv_bwd(ti, pa):
        gk, gv, c_, a_, b_ = ti
        k_blocks = _blocks(gk, MLA_HEADS)
        tot = k_blocks[0]
        for blk in k_blocks[1:]:
            tot = tot + blk
        return [jnp.concatenate([gk, gv], axis=1), _rope_block(tot, c_, -a_, -b_)], []

    d_kv, d_kr = _rowwise("mla_kv_bwd", f_mla_kv_bwd, [dk_mla, dv_mla, rc, ra, rb], [], [(2 * hw, BF16), (LANES, BF16)])
    g_ukv = _mm_tn(ckvn, d_kv, name="grad_ukv")
    grads["w_uk"] = _unpad_heads(g_ukv[:, :hw], MLA_HEADS, MLA_NOPE)
    grads["w_uv"] = _unpad_heads(g_ukv[:, hw:], MLA_HEADS, MLA_V)
    d_ckvn = _mm(d_kv, w_ukv, out_dtype=F32, name="proj_ukv_bwd", transpose_b=True)

    def f_mla_norms_bwd(ti, pa):
        cq, ckv, dcqn, dckvn = ti
        dcq, dg_q = _rms_bwd(cq, pa[0], dcqn)
        dckv, dg_kv = _rms_bwd(ckv, pa[1], dckvn)
        return [dcq, dckv], [dg_q, dg_kv]

    d_cq, d_ckv, grads["q_a_norm"], grads["kv_a_norm"] = _rowwise(
        "mla_norms_bwd", f_mla_norms_bwd, [zwin("cq"), zwin("ckv"), d_cqn, d_ckvn],
        [wts["q_a_norm"], wts["kv_a_norm"]], [(MLA_Q_LORA, BF16), (MLA_KV_LORA, BF16)], [MLA_Q_LORA, MLA_KV_LORA])

    def f_fox_compact(ti, pa):
        gather = _head_spread(FOX_HEADS, FOX_DIM)
        return [lax.dot_general(a.astype(BF16), gather, _NT, preferred_element_type=F32) for a in ti], []

    d_fq, d_fk, d_fv = _rowwise("fox_compact", f_fox_compact, [dq_fox, dk_fox, dv_fox], [],
                                [(FOX_HEADS * FOX_DIM, BF16)] * 3)
    d_z = jnp.concatenate([d_ga, d_gb, d_fq, d_fk, d_fv, d_cq, d_ckv, d_kr, d_fl], axis=1)
    assert d_z.shape[1] == zl.width

    def f_bias(ti, pa):
        return [], [jnp.sum(ti[0].astype(F32), axis=0, keepdims=True)]

    (g_b_in,) = _rowwise("grad_b_in", f_bias, [d_z], [], [], [zl.width])
    grads["b_in"] = zl.from_kernel(g_b_in)
    grads["w_in"] = zl.from_kernel(_mm_tn(h, d_z, name="grad_in"))
    tok = send_grads("late", {nm: grads[nm] for nm, _ in BIG if nm not in EARLY_GRADS})
    d_h = _mm(d_z, w_in, bias=jnp.zeros((1, d), F32) + tok, out_dtype=F32, name="proj_in_bwd", transpose_b=True)

    def f_norm_in_bwd(ti, pa):
        dx1v, dh, xa = ti
        dxn, dg = _rms_bwd(xa, pa[0], dh)
        return [dx1v + dxn], [dg]

    grad_x, grads["ln_pre_mix"] = _rowwise("norm_in_bwd", f_norm_in_bwd, [d_x1, d_h, x], [wts["ln_pre_mix"]],
                                           [(d, F32)], [d])
    return loss, grad_x, grads


class _PackLayout:
    def __init__(self, shapes):
        self.shapes = list(shapes)
        self.width = _round_up(max(b for _, b in shapes), LANES)
        self.bands = []
        row = 0
        shelf = []
        for idx, (a, b) in enumerate(shapes):
            if 2 * _round_up(b, LANES) > self.width:
                self.bands.append((row, _round_up(a, 32), [(idx, 0)]))
                row += _round_up(a, 32)
            else:
                shelf.append(idx)
        col, items = 0, []
        for idx in shelf:
            wb = _round_up(shapes[idx][1], LANES)
            if col + wb > self.width:
                hgt = max(_round_up(shapes[i][0], 32) for i, _ in items)
                self.bands.append((row, hgt, items))
                row += hgt
                col, items = 0, []
            items.append((idx, col))
            col += wb
        if items:
            hgt = max(_round_up(shapes[i][0], 32) for i, _ in items)
            self.bands.append((row, hgt, items))
            row += hgt
        self.rows = _round_up(row, 16 * LOCAL_PIECES)

    def pack(self, arrs):
        lead = arrs[0].shape[:-2]
        no_pad = [(0, 0)] * len(lead)
        bands = []
        for _, hgt, items in self.bands:
            parts = []
            for k, (idx, col) in enumerate(items):
                a, b = self.shapes[idx]
                nxt = items[k + 1][1] if k + 1 < len(items) else self.width
                parts.append(jnp.pad(arrs[idx], no_pad + [(0, hgt - a), (0, nxt - col - b)]))
            bands.append(parts[0] if len(parts) == 1 else jnp.concatenate(parts, axis=-1))
        used = sum(hgt for _, hgt, _ in self.bands)
        if used < self.rows:
            bands.append(jnp.zeros(lead + (self.rows - used, self.width), arrs[0].dtype))
        return jnp.concatenate(bands, axis=-2)

    def unpack(self, packed):
        out = [None] * len(self.shapes)
        for row, _, items in self.bands:
            for idx, col in items:
                a, b = self.shapes[idx]
                out[idx] = packed[..., row:row + a, col:col + b]
        return out


def _to_shards(g, axis):
    if axis == 0:
        return g.reshape(N_CHIPS, g.shape[0] // N_CHIPS, g.shape[1])
    return jnp.stack(jnp.split(g, N_CHIPS, axis=1))


def _from_shards(s4, axis):
    n, a, b = s4.shape
    if axis == 0:
        return s4.reshape(n * a, b)
    return jnp.concatenate([s4[ch] for ch in range(n)], axis=1)


ANY = pl.BlockSpec(memory_space=pl.ANY)


def _place():
    return lax.axis_index("x"), lax.axis_index("y"), lax.axis_index("c")


def _gather_weights(wpk):
    rows, wd = wpk.shape
    half = rows // 2

    def body(w_ref, out_ref, send_sems, recv_sems, local_sems):
        x, y, c = _place()
        sibling = (x, y, 1 - c)
        chips = [(1 - x, y), (x, 1 - y), (1 - x, 1 - y)]

        def slab(chip, hf):
            return out_ref.at[2 * chip[0] + chip[1], pl.ds(hf * half, half), :]

        def copy(k, chip, hf, to, src=None):
            return pltpu.make_async_remote_copy(
                src_ref=slab(chip, hf) if src is None else src, dst_ref=slab(chip, hf),
                send_sem=send_sems.at[k], recv_sem=recv_sems.at[k], device_id=to, device_id_type=MESH)

        piece = rows // LOCAL_PIECES
        mine = [pltpu.make_async_copy(w_ref.at[pl.ds(n * piece, piece), :],
                                      out_ref.at[2 * x + y, pl.ds(n * piece, piece), :], local_sems.at[n])
                for n in range(LOCAL_PIECES)]
        for cp in mine:
            cp.start()
        first = [copy(j, (x, y), c, (*chip, c), src=w_ref.at[pl.ds(c * half, half), :]) for j, chip in enumerate(chips)]
        for cp in first:
            cp.start()
        passed = [copy(3 + j, chip, c, sibling) for j, chip in enumerate(chips)]
        for j, chip in enumerate(chips):
            copy(j, chip, c, (x, y, c)).wait_recv()
            passed[j].start()
        for j, chip in enumerate(chips):
            copy(3 + j, chip, 1 - c, (x, y, c)).wait_recv()
        for cp in first + passed:
            cp.wait_send()
        for cp in mine:
            cp.wait()

    assert rows % (16 * LOCAL_PIECES) == 0
    return pl.pallas_call(
        body, out_shape=jax.ShapeDtypeStruct((N_CHIPS, rows, wd), wpk.dtype),
        in_specs=[ANY], out_specs=ANY,
        scratch_shapes=[pltpu.SemaphoreType.DMA((6,)), pltpu.SemaphoreType.DMA((6,)),
                        pltpu.SemaphoreType.DMA((LOCAL_PIECES,))],
        name="gather_weights")(wpk)


HBM = pl.BlockSpec(memory_space=pltpu.HBM)
SEM = pl.BlockSpec(memory_space=pltpu.SEMAPHORE)
EFFECT = pltpu.SideEffectType.DATAFLOW_SIDE_EFFECTING
N_LATE = 6


def _gather_late_start(wpk, tag):
    rows, wd = wpk.shape
    half = rows // 2

    def body(w_ref, land_ref, send_sems, recv_sems, w_thru, land_thru, token):
        x, y, c = _place()
        chips = [(1 - x, y), (x, 1 - y), (1 - x, 1 - y)]
        for j, chip in enumerate(chips):
            for to_core in range(2):
                pltpu.make_async_remote_copy(
                    src_ref=w_ref.at[pl.ds(c * half, half), :],
                    dst_ref=land_ref.at[2 * x + y, pl.ds(c * half, half), :],
                    send_sem=send_sems.at[2 * j + to_core], recv_sem=recv_sems.at[2 * j + c],
                    device_id=(*chip, to_core), device_id_type=MESH).start()
        token[...] = jnp.zeros_like(token)

    land = lax.empty((N_CHIPS, rows, wd), wpk.dtype)
    return pl.pallas_call(
        body, name="gather_" + tag + "_start",
        out_shape=(pltpu.SemaphoreType.DMA((N_LATE,)), pltpu.SemaphoreType.DMA((N_LATE,)),
                   pltpu.HBM(wpk.shape, wpk.dtype), pltpu.HBM(land.shape, land.dtype),
                   jax.ShapeDtypeStruct((8, LANES), F32)),
        in_specs=(HBM, HBM), out_specs=(SEM, SEM, HBM, HBM, pl.BlockSpec(memory_space=pltpu.VMEM)),
        input_output_aliases={0: 2, 1: 3},
        compiler_params=pltpu.CompilerParams(has_side_effects=EFFECT),
    )(pltpu.with_memory_space_constraint(wpk, pltpu.HBM), pltpu.with_memory_space_constraint(land, pltpu.HBM))


def _gather_late_wait(send_sems, recv_sems, w_thru, land_thru, after, tag):
    rows, wd = w_thru.shape
    half = rows // 2

    def body(w_ref, land_ref, send_sems, recv_sems, after_ref, w_dead, land_out):
        x, y, c = _place()
        for n in range(N_LATE):
            cp = pltpu.make_async_remote_copy(
                src_ref=w_ref.at[pl.ds(0, half), :], dst_ref=land_ref.at[0, pl.ds(0, half), :],
                send_sem=send_sems.at[n], recv_sem=recv_sems.at[n], device_id=(x, y, c), device_id_type=MESH)
            cp.wait_send()
            cp.wait_recv()

    return pl.pallas_call(
        body, name="gather_" + tag + "_wait",
        out_shape=(pltpu.HBM(w_thru.shape, w_thru.dtype), pltpu.HBM(land_thru.shape, land_thru.dtype)),
        in_specs=(HBM, HBM, SEM, SEM, ANY), out_specs=(HBM, HBM), input_output_aliases={0: 0, 1: 1},
        compiler_params=pltpu.CompilerParams(has_side_effects=EFFECT),
    )(w_thru, land_thru, send_sems, recv_sems, after)[1]


N_PART = 7


def _reduce_start(gbf, tag):
    _, _, hrows, wd = gbf.shape

    def body(g_ref, land_ref, send_sems, recv_sems, g_thru, land_thru, token):
        x, y, c = _place()
        chips = [(1 - x, y), (x, 1 - y), (1 - x, 1 - y)]
        for j, chip in enumerate(chips):
            for to_core in range(2):
                pltpu.make_async_remote_copy(
                    src_ref=g_ref.at[2 * chip[0] + chip[1], to_core], dst_ref=land_ref.at[2 * j + c],
                    send_sem=send_sems.at[2 * j + to_core], recv_sem=recv_sems.at[2 * j + c],
                    device_id=(*chip, to_core), device_id_type=MESH).start()
        pltpu.make_async_remote_copy(
            src_ref=g_ref.at[2 * x + y, 1 - c], dst_ref=land_ref.at[N_PART - 1],
            send_sem=send_sems.at[N_PART - 1], recv_sem=recv_sems.at[N_PART - 1],
            device_id=(x, y, 1 - c), device_id_type=MESH).start()
        token[...] = jnp.zeros_like(token)

    land = lax.empty((N_PART, hrows, wd), gbf.dtype)
    return pl.pallas_call(
        body, name="reduce_" + tag + "_start",
        out_shape=(pltpu.SemaphoreType.DMA((N_PART,)), pltpu.SemaphoreType.DMA((N_PART,)),
                   pltpu.HBM(gbf.shape, gbf.dtype), pltpu.HBM(land.shape, land.dtype),
                   jax.ShapeDtypeStruct((8, LANES), F32)),
        in_specs=(HBM, HBM), out_specs=(SEM, SEM, HBM, HBM, pl.BlockSpec(memory_space=pltpu.VMEM)),
        input_output_aliases={0: 2, 1: 3},
        compiler_params=pltpu.CompilerParams(has_side_effects=EFFECT),
    )(pltpu.with_memory_space_constraint(gbf, pltpu.HBM), pltpu.with_memory_space_constraint(land, pltpu.HBM))


def _reduce_wait(send_sems, recv_sems, g_thru, land_thru, after, tag):
    def body(g_ref, land_ref, send_sems, recv_sems, after_ref, g_dead, land_out):
        x, y, c = _place()
        for n in range(N_PART):
            cp = pltpu.make_async_remote_copy(
                src_ref=g_ref.at[0, 0], dst_ref=land_ref.at[0], send_sem=send_sems.at[n], recv_sem=recv_sems.at[n],
                device_id=(x, y, c), device_id_type=MESH)
            cp.wait_send()
            cp.wait_recv()

    return pl.pallas_call(
        body, name="reduce_" + tag + "_wait",
        out_shape=(pltpu.HBM(g_thru.shape, g_thru.dtype), pltpu.HBM(land_thru.shape, land_thru.dtype)),
        in_specs=(HBM, HBM, SEM, SEM, ANY), out_specs=(HBM, HBM), input_output_aliases={0: 0, 1: 1},
        compiler_params=pltpu.CompilerParams(has_side_effects=EFFECT),
    )(g_thru, land_thru, send_sems, recv_sems, after)[1]


def _sibling_swap(mine):
    def body(m_ref, out_ref, send_sem, recv_sem):
        x, y, c = _place()
        cp = pltpu.make_async_remote_copy(
            src_ref=m_ref, dst_ref=out_ref, send_sem=send_sem, recv_sem=recv_sem,
            device_id=(x, y, 1 - c), device_id_type=MESH)
        cp.start()
        cp.wait()

    return pl.pallas_call(
        body, out_shape=jax.ShapeDtypeStruct(mine.shape, mine.dtype), in_specs=[ANY], out_specs=ANY,
        scratch_shapes=[pltpu.SemaphoreType.DMA, pltpu.SemaphoreType.DMA], name="grad_sibling_swap")(mine)


def _adamw(w, g, m, v):
    m = ADAM_B1 * m + (1.0 - ADAM_B1) * g
    v = ADAM_B2 * v + (1.0 - ADAM_B2) * (g * g)
    m_hat = m / (1.0 - ADAM_B1 ** ADAM_STEP)
    v_hat = v / (1.0 - ADAM_B2 ** ADAM_STEP)
    delta = -ADAM_LR * (m_hat / (jnp.sqrt(v_hat) + ADAM_EPS) + ADAM_WD * w)
    return delta, m, v


def _small_allreduce_adamw(gs, ws, ms, vs):
    n_dev = 8
    n_par = len(gs)
    wd = PACK_W
    chunks = []
    for p, g in enumerate(gs):
        for off in range(0, g.shape[1], wd):
            chunks.append((p, len(chunks), off, min(wd, g.shape[1] - off)))
    rows = _round_up(len(chunks), 8)

    def body(*refs):
        g_refs, w_refs, m_refs, v_refs = (refs[k * n_par:(k + 1) * n_par] for k in range(4))
        go_refs, d_refs, mo_refs, vo_refs = (refs[(4 + k) * n_par:(5 + k) * n_par] for k in range(4))
        mine_ref, all_ref, send_sems, recv_sems = refs[8 * n_par:]
        x, y, c = _place()
        me, sibling = (x, y, c), (x, y, 1 - c)
        chips = [(1 - x, y), (x, 1 - y), (1 - x, 1 - y)]

        def slot(px, py, pc):
            return all_ref.at[4 * px + 2 * py + pc]

        def copy(k, block, to, src=None):
            return pltpu.make_async_remote_copy(
                src_ref=slot(*block) if src is None else src, dst_ref=slot(*block),
                send_sem=send_sems.at[k], recv_sem=recv_sems.at[k], device_id=to, device_id_type=MESH)

        mine_ref[...] = jnp.zeros_like(mine_ref)
        for p, row, off, width in chunks:
            mine_ref[row:row + 1, 0:width] = g_refs[p][:, off:off + width]
        all_ref[4 * x + 2 * y + c] = mine_ref[...]
        first = [copy(0, me, sibling, src=mine_ref)]
        first += [copy(1 + j, me, (*chip, c), src=mine_ref) for j, chip in enumerate(chips)]
        for cp in first:
            cp.start()
        passed = [copy(4 + j, (*chip, c), sibling) for j, chip in enumerate(chips)]
        for j, chip in enumerate(chips):
            copy(1 + j, (*chip, c), me).wait_recv()
            passed[j].start()
        copy(0, sibling, me).wait_recv()
        for j, chip in enumerate(chips):
            copy(4 + j, (*chip, 1 - c), me).wait_recv()
        for cp in first + passed:
            cp.wait_send()
        tot = jnp.zeros((rows, wd), F32)
        for dev in range(n_dev):
            tot = tot + all_ref[dev]
        mine_ref[...] = tot
        for p, row, off, width in chunks:
            cols = slice(off, off + width)
            g = mine_ref[row:row + 1, 0:width]
            delta, m_new, v_new = _adamw(w_refs[p][:, cols], g, m_refs[p][:, cols], v_refs[p][:, cols])
            go_refs[p][:, cols] = g
            d_refs[p][:, cols] = delta
            mo_refs[p][:, cols] = m_new
            vo_refs[p][:, cols] = v_new

    vm = pl.BlockSpec(memory_space=pltpu.VMEM)
    shp = [jax.ShapeDtypeStruct(g.shape, F32) for g in gs]
    res = pl.pallas_call(
        body, out_shape=shp * 4, in_specs=[vm] * (4 * n_par), out_specs=[vm] * (4 * n_par),
        scratch_shapes=[pltpu.VMEM((rows, wd), F32), pltpu.VMEM((n_dev, rows, wd), F32),
                        pltpu.SemaphoreType.DMA((7,)), pltpu.SemaphoreType.DMA((7,))],
        name="small_allreduce_adamw")(*gs, *ws, *ms, *vs)
    return [res[k * n_par:(k + 1) * n_par] for k in range(4)]


def kernel(x, positions, ln_pre_mix, ln_post_mix, ln_pre_mlp, ln_post_mlp, w_in, b_in, q_a_norm, w_uq, kv_a_norm, w_uk, w_uv, w_o_mla, w_o_fox, w_out, w_ff1, w_ff2, loss_target, m_ln_pre_mix, m_ln_post_mix, m_ln_pre_mlp, m_ln_post_mlp, m_w_in, m_b_in, m_q_a_norm, m_w_uq, m_kv_a_norm, m_w_uk, m_w_uv, m_w_o_mla, m_w_o_fox, m_w_out, m_w_ff1, m_w_ff2, v_ln_pre_mix, v_ln_post_mix, v_ln_pre_mlp, v_ln_post_mlp, v_w_in, v_b_in, v_q_a_norm, v_w_uq, v_kv_a_norm, v_w_uk, v_w_uv, v_w_o_mla, v_w_o_fox, v_w_out, v_w_ff1, v_w_ff2):
    w = dict(ln_pre_mix=ln_pre_mix, ln_post_mix=ln_post_mix, ln_pre_mlp=ln_pre_mlp, ln_post_mlp=ln_post_mlp, w_in=w_in,
             b_in=b_in, q_a_norm=q_a_norm, w_uq=w_uq, kv_a_norm=kv_a_norm, w_uk=w_uk, w_uv=w_uv, w_o_mla=w_o_mla,
             w_o_fox=w_o_fox, w_out=w_out, w_ff1=w_ff1, w_ff2=w_ff2)
    mom = dict(ln_pre_mix=m_ln_pre_mix, ln_post_mix=m_ln_post_mix, ln_pre_mlp=m_ln_pre_mlp, ln_post_mlp=m_ln_post_mlp,
               w_in=m_w_in, b_in=m_b_in, q_a_norm=m_q_a_norm, w_uq=m_w_uq, kv_a_norm=m_kv_a_norm, w_uk=m_w_uk,
               w_uv=m_w_uv, w_o_mla=m_w_o_mla, w_o_fox=m_w_o_fox, w_out=m_w_out, w_ff1=m_w_ff1, w_ff2=m_w_ff2)
    var = dict(ln_pre_mix=v_ln_pre_mix, ln_post_mix=v_ln_post_mix, ln_pre_mlp=v_ln_pre_mlp, ln_post_mlp=v_ln_post_mlp,
               w_in=v_w_in, b_in=v_b_in, q_a_norm=v_q_a_norm, w_uq=v_w_uq, kv_a_norm=v_kv_a_norm, w_uk=v_w_uk,
               w_uv=v_w_uv, w_o_mla=v_w_o_mla, w_o_fox=v_w_o_fox, w_out=v_w_out, w_ff1=v_w_ff1, w_ff2=v_w_ff2)

    big_names = [nm for nm, _ in BIG]
    c = lax.axis_index("c")
    chip = 2 * lax.axis_index("x") + lax.axis_index("y")

    axes = dict(BIG)

    def assemble(names, lay, gathered):
        return {nm: _from_shards(s4, axes[nm] - 1) for nm, s4 in zip(names, lay.unpack(gathered), strict=True)}

    groups = {"mixer": [nm for nm in big_names if nm in MIXER], "mlp": [nm for nm in big_names if nm in LATE]}
    first_names = [nm for nm in big_names if nm not in MIXER and nm not in LATE]
    full = {nm: wv for nm, wv in w.items() if nm in SMALL}
    lay_first = _PackLayout([w[nm].shape[1:] for nm in first_names])
    first = _gather_weights(lay_first.pack([w[nm][0].astype(BF16) for nm in first_names]))
    full.update(assemble(first_names, lay_first, first))
    travelling = {}
    for tag, names in groups.items():
        lay = _PackLayout([w[nm].shape[1:] for nm in names])
        shard = lay.pack([w[nm][0].astype(BF16) for nm in names])
        started = _gather_late_start(lax.optimization_barrier((shard, first))[0], tag)
        travelling[tag] = (names, lay, shard, started[:4])
        full["b_in"] = full["b_in"] + started[4][0, 0]

    def late_weights(tag, after):
        names, lay, shard, handles = travelling[tag]
        land = _gather_late_wait(*handles, after, tag)
        land = lax.dynamic_update_slice(land, shard[None], (chip, 0, 0))
        return assemble(names, lay, land)

    grad_groups = {"early": [nm for nm in big_names if nm in EARLY_GRADS],
                   "late": [nm for nm in big_names if nm not in EARLY_GRADS]}
    sent = {}

    def send_grads(tag, g):
        names = grad_groups[tag]
        lay = _PackLayout([w[nm].shape[1:] for nm in names])
        hrows = lay.rows // 2
        gbf = lay.pack([_to_shards(g[nm], axes[nm] - 1).astype(BF16) for nm in names])
        gbf = gbf.reshape(N_CHIPS, 2, hrows, lay.width)
        started = _reduce_start(gbf, tag)
        own = lax.dynamic_index_in_dim(lax.dynamic_index_in_dim(started[2], chip, axis=0, keepdims=False), c, axis=0,
                                       keepdims=False)
        sent[tag] = (names, lay, started[:4], own)
        return started[4][0, 0]

    loss_local, grad_x, grads = _local_step(x[0], positions[0], loss_target[0], full, late_weights, send_grads)

    def f_add8(ti, pa):
        tot = ti[0].astype(F32)
        for part in ti[1:]:
            tot = tot + part.astype(F32)
        return [tot], []

    reduced = []
    for tag, (names, lay, handles, own) in sent.items():
        parts = _reduce_wait(*handles, grad_x, tag)
        reduced.append(_rowwise("grad_add_" + tag, f_add8, [own] + [parts[n] for n in range(N_PART)], [],
                                [(lay.width, F32)])[0])

    assert len({lay.width for _, lay, _, _ in sent.values()}) == 1
    red = jnp.concatenate(reduced, axis=0)
    sib = _sibling_swap(red)
    lower, upper = jnp.where(c == 0, red, sib), jnp.where(c == 0, sib, red)
    g_by_name, row = {}, 0
    for names, lay, _, _ in sent.values():
        hrows = lay.rows // 2
        both = jnp.concatenate([lower[row:row + hrows], upper[row:row + hrows]], axis=0)
        g_by_name.update(zip(names, lay.unpack(both), strict=True))
        row += hrows
    g_shards = [g_by_name[nm] for nm in big_names]

    def f_adamw(ti, pa):
        wv, gv, mv, vv = ti
        return list(_adamw(wv, gv, mv, vv)), []

    out = {"grad": {}, "delta": {}, "m": {}, "v": {}}
    for nm, g_sh in zip(big_names, g_shards, strict=True):
        wd = g_sh.shape[1]
        d_sh, m_sh, v_sh = _rowwise("adamw_" + nm, f_adamw, [w[nm][0], g_sh, mom[nm][0], var[nm][0]], [], [(wd, F32)] * 3)
        out["grad"][nm], out["delta"][nm], out["m"][nm], out["v"][nm] = g_sh[None], d_sh[None], m_sh[None], v_sh[None]

    loss_row = jnp.zeros((1, LANES), F32) + loss_local
    blank = jnp.zeros((1, LANES), F32)
    small = _small_allreduce_adamw([grads[nm] for nm in SMALL] + [loss_row], [w[nm] for nm in SMALL] + [blank],
                                   [mom[nm] for nm in SMALL] + [blank], [var[nm] for nm in SMALL] + [blank])
    for kind, arrs in zip(("grad", "delta", "m", "v"), small, strict=True):
        for nm, arr in zip(SMALL, arrs[:len(SMALL)], strict=True):
            out[kind][nm] = arr
    loss = small[0][len(SMALL)][0, 0]

    return (loss, grad_x[None], *[out["grad"][nm] for nm in ALL_W], *[out["delta"][nm] for nm in ALL_W],
            *[out["m"][nm] for nm in ALL_W], *[out["v"][nm] for nm in ALL_W])
```

```python
import functools
import math

import jax
import jax.numpy as jnp
from jax import lax
from jax.experimental import pallas as pl
from jax.experimental.pallas import tpu as pltpu

F32 = jnp.float32
BF16 = jnp.bfloat16

MLA_HEADS = 8
MLA_Q_LORA = 256
MLA_KV_LORA = 128
MLA_NOPE = 64
MLA_ROPE = 32
MLA_V = 64
FOX_HEADS = 8
FOX_DIM = 64
ROPE_THETA = 10000.0
NORM_EPS = 1e-6
HALF_ROPE = MLA_ROPE // 2

ADAM_LR = 0.001
ADAM_B1 = 0.9
ADAM_B2 = 0.999
ADAM_EPS = 1e-08
ADAM_WD = 0.01
ADAM_STEP = 10

LANES = 128
VMEM_LIMIT = 56 * 1024 * 1024
ATT_TILE = 1024
FWD_GROUP_LOG2 = 1
FWD_GROUP = 1 << FWD_GROUP_LOG2
MM_VMEM_BUDGET = 40 * 1024 * 1024
ROW_TILE_BYTES = 12 * 1024 * 1024
MXU_WIDTH = 256
MXU_MACS_PER_S = 4.98e14
HBM_BYTES_PER_S = 3.2e12
STEP_OVERHEAD_S = 0.35e-6
NEG = -1e30
LOG2E = math.log2(math.e)
MESH = pl.DeviceIdType.MESH

V_ONES = 64
FOX_Q_F = 64
FOX_Q_L = 67
FOX_Q_ONES = 70
MLA_Q_L = 96

BIG = (("w_in", 2), ("w_uq", 2), ("w_uk", 2), ("w_uv", 2), ("w_o_mla", 2), ("w_o_fox", 2),
       ("w_out", 1), ("w_ff1", 2), ("w_ff2", 1))
MIXER = ("w_uq", "w_uk", "w_uv", "w_o_mla", "w_o_fox")
LATE = ("w_out", "w_ff1", "w_ff2")
EARLY_GRADS = ("w_o_mla", "w_o_fox", "w_out", "w_ff1", "w_ff2")
SMALL = ("ln_pre_mix", "ln_post_mix", "ln_pre_mlp", "ln_post_mlp", "b_in", "q_a_norm", "kv_a_norm")
ALL_W = ("ln_pre_mix", "ln_post_mix", "ln_pre_mlp", "ln_post_mlp", "w_in", "b_in", "q_a_norm", "w_uq",
         "kv_a_norm", "w_uk", "w_uv", "w_o_mla", "w_o_fox", "w_out", "w_ff1", "w_ff2")
N_CHIPS = 4
PACK_W = 1024
LOCAL_PIECES = 8

_NT = (((1,), (1,)), ((), ()))
_TN = (((0,), (0,)), ((), ()))


def _cparams(sem=None):
    return pltpu.CompilerParams(dimension_semantics=sem, vmem_limit_bytes=VMEM_LIMIT)


def _divisor_tile(n, limit, mult):
    if n <= limit:
        return n
    best = None
    t = mult
    while t <= limit:
        if n % t == 0:
            best = t
        t += mult
    assert best is not None, (n, limit, mult)
    return best


def _round_up(v, mult):
    return -(-v // mult) * mult


def _mm_tiles(m, k, n, io_bytes):
    best = None
    for tm in (2048, 1024, 512, 256, 128):
        if m % tm:
            continue
        for tn in range(LANES, min(n, 2048) + 1, LANES):
            if n % tn:
                continue
            vmem = 2 * (tm * k * 2 + k * tn * 2 + tm * tn * io_bytes) + tm * tn * 4
            if vmem > MM_VMEM_BUDGET:
                continue
            mxu = m * k * n * (_round_up(tn, MXU_WIDTH) / tn) / MXU_MACS_PER_S
            hbm = (m * k * 2 + (m // tm) * k * n * 2 + m * n * io_bytes) / HBM_BYTES_PER_S
            cost = max(mxu, hbm) + (m // tm) * (n // tn) * STEP_OVERHEAD_S
            if best is None or cost < best[0]:
                best = (cost, tm, tn)
    assert best is not None, (m, k, n)
    return best[1], best[2]


def _mm(a, b, *, out_dtype, name, bias=None, transpose_b=False, extras=(), epilogue=None):
    m, k = a.shape
    n = b.shape[0] if transpose_b else b.shape[1]
    assert (b.shape[1] if transpose_b else b.shape[0]) == k and a.dtype == BF16 and b.dtype == BF16
    out_dtypes = list(out_dtype) if isinstance(out_dtype, (list, tuple)) else [out_dtype]
    n_ex = len(extras)
    tm, tn = _mm_tiles(m, k, n, sum(jnp.dtype(dt).itemsize for dt in out_dtypes) + 4 * n_ex)

    def body(*refs):
        a_ref, b_ref = refs[:2]
        pos = 2
        bias_ref = None
        if bias is not None:
            bias_ref = refs[pos]
            pos += 1
        ex_refs = refs[pos:pos + n_ex]
        o_refs = refs[pos + n_ex:]
        if transpose_b:
            acc = lax.dot_general(a_ref[...], b_ref[...], _NT, preferred_element_type=F32)
        else:
            acc = jnp.dot(a_ref[...], b_ref[...], preferred_element_type=F32)
        if bias_ref is not None:
            acc = acc + bias_ref[...]
        vals = [acc] if epilogue is None else epilogue(acc, [r[...] for r in ex_refs])
        for ref, val in zip(o_refs, vals, strict=True):
            ref[...] = val.astype(ref.dtype)

    b_spec = pl.BlockSpec((tn, k), lambda i, j: (j, 0)) if transpose_b else pl.BlockSpec((k, tn), lambda i, j: (0, j))
    in_specs = [pl.BlockSpec((tm, k), lambda i, j: (i, 0)), b_spec]
    args = [a, b]
    if bias is not None:
        in_specs.append(pl.BlockSpec((1, tn), lambda i, j: (0, j)))
        args.append(bias)
    in_specs += [pl.BlockSpec((tm, tn), lambda i, j: (i, j)) for _ in extras]
    args += list(extras)
    res = pl.pallas_call(
        body, grid=(m // tm, n // tn), in_specs=in_specs,
        out_specs=[pl.BlockSpec((tm, tn), lambda i, j: (i, j)) for _ in out_dtypes],
        out_shape=[jax.ShapeDtypeStruct((m, n), dt) for dt in out_dtypes],
        compiler_params=_cparams(("parallel", "parallel")), name=name)(*args)
    return res if isinstance(out_dtype, (list, tuple)) else res[0]


def _mm_tn(a, b, *, name):
    s, m = a.shape
    s2, n = b.shape
    assert s == s2 and a.dtype == BF16 and b.dtype == BF16
    tm = _divisor_tile(m, 1024, LANES)
    tn = _divisor_tile(n, 2304, LANES)
    tk = _divisor_tile(s, 1024, 16)
    nk = s // tk

    def body(a_ref, b_ref, o_ref, acc_ref):
        @pl.when(pl.program_id(2) == 0)
        def _():
            acc_ref[...] = jnp.zeros_like(acc_ref)

        acc_ref[...] += lax.dot_general(a_ref[...], b_ref[...], _TN, preferred_element_type=F32)

        @pl.when(pl.program_id(2) == nk - 1)
        def _():
            o_ref[...] = acc_ref[...].astype(o_ref.dtype)

    return pl.pallas_call(
        body, grid=(m // tm, n // tn, nk),
        in_specs=[pl.BlockSpec((tk, tm), lambda i, j, k: (k, i)), pl.BlockSpec((tk, tn), lambda i, j, k: (k, j))],
        out_specs=pl.BlockSpec((tm, tn), lambda i, j, k: (i, j)),
        out_shape=jax.ShapeDtypeStruct((m, n), BF16),
        scratch_shapes=[pltpu.VMEM((tm, tn), F32)],
        compiler_params=_cparams(("parallel", "parallel", "arbitrary")), name=name)(a, b)


def _rowwise(name, fn, tiled, params, outs, reds=(), reverse=False):
    wins = [t if isinstance(t, tuple) else (t, 0, t.shape[1]) for t in tiled]
    s = wins[0][0].shape[0]
    row_bytes = sum(w * arr.dtype.itemsize for arr, _, w in wins) + sum(w * jnp.dtype(d).itemsize for w, d in outs)
    ts = _divisor_tile(s, max(16, min(1024, ROW_TILE_BYTES // row_bytes)), 16)
    nt, npar, nout = len(wins), len(params), len(outs)
    n_tiles = s // ts

    def row(i):
        return n_tiles - 1 - i if reverse else i

    def body(*refs):
        tin = [r[...] for r in refs[:nt]]
        par = [r[...] for r in refs[nt:nt + npar]]
        out_refs = refs[nt + npar:nt + npar + nout]
        red_refs = refs[nt + npar + nout:]
        o, r = fn(tin, par)
        for ref, val in zip(out_refs, o, strict=True):
            ref[...] = val.astype(ref.dtype)
        if red_refs:
            @pl.when(pl.program_id(0) == 0)
            def _():
                for ref in red_refs:
                    ref[...] = jnp.zeros_like(ref)

            for ref, val in zip(red_refs, r, strict=True):
                ref[...] += val

    in_specs = [pl.BlockSpec((ts, w), functools.partial(lambda i, cb: (row(i), cb), cb=cb)) for _, cb, w in wins]
    in_specs += [pl.BlockSpec(p.shape, lambda i: (0, 0)) for p in params]
    out_specs = [pl.BlockSpec((ts, w), lambda i: (row(i), 0)) for w, _ in outs]
    out_specs += [pl.BlockSpec((1, w), lambda i: (0, 0)) for w in reds]
    out_shape = [jax.ShapeDtypeStruct((s, w), d) for w, d in outs]
    out_shape += [jax.ShapeDtypeStruct((1, w), F32) for w in reds]
    return pl.pallas_call(
        body, grid=(n_tiles,), in_specs=in_specs, out_specs=out_specs, out_shape=out_shape,
        compiler_params=_cparams(("arbitrary",)), name=name)(*[w[0] for w in wins], *params)


def _rms(x, g):
    r = lax.rsqrt(jnp.mean(x * x, axis=-1, keepdims=True) + NORM_EPS)
    return x * r * g, r


def _rms_bwd(x, g, dy):
    r = lax.rsqrt(jnp.mean(x * x, axis=-1, keepdims=True) + NORM_EPS)
    gy = dy * g
    dx = r * gy - x * (r * r * r) * jnp.mean(x * gy, axis=-1, keepdims=True)
    dg = jnp.sum(dy * (x * r), axis=0, keepdims=True)
    return dx, dg


def _sigmoid(x):
    return 1.0 / (1.0 + jnp.exp(-x))


def _split3(x):
    hi = x.astype(BF16).astype(F32)
    r = x - hi
    mid = r.astype(BF16).astype(F32)
    lo = (r - mid).astype(BF16).astype(F32)
    return hi, mid, lo


def _lane(shape):
    return lax.broadcasted_iota(jnp.int32, shape, 1)


def _put3(blk, lane, pos, pieces):
    for k, piece in enumerate(pieces):
        blk = jnp.where(lane == pos + k, piece, blk)
    return blk


def _lane_column(blk, lane, pos):
    return jnp.sum(jnp.where(lane == pos, blk, 0.0), axis=1, keepdims=True)


def _blocks(a, nh):
    return [a[:, h * LANES:(h + 1) * LANES] for h in range(nh)]


def _head_spread(nh, d):
    shift = d.bit_length() - 1
    assert 1 << shift == d
    r = lax.broadcasted_iota(jnp.int32, (nh * d, nh * LANES), 0)
    c = lax.broadcasted_iota(jnp.int32, (nh * d, nh * LANES), 1)
    return jnp.where(c == (r >> shift) * LANES + (r & (d - 1)), 1.0, 0.0).astype(BF16)


def _rope_block(x, c, sa, sb):
    return x * c + pltpu.roll(x, LANES - HALF_ROPE, 1) * sa + pltpu.roll(x, HALF_ROPE, 1) * sb


def _forget_cumsum(z, cb):
    s = z.shape[0]
    ts = _divisor_tile(s, 512, LANES)

    def body(x_ref, col_ref, carry):
        @pl.when(pl.program_id(0) == 0)
        def _():
            carry[...] = jnp.zeros_like(carry)

        x = x_ref[...]
        lf = jnp.minimum(x, 0.0) - jnp.log1p(jnp.exp(-jnp.abs(x)))
        r = lax.broadcasted_iota(jnp.int32, (ts, ts), 0)
        c = lax.broadcasted_iota(jnp.int32, (ts, ts), 1)
        tri = jnp.where(c <= r, 1.0, 0.0).astype(F32)
        col_ref[...] = jnp.dot(tri, lf, preferred_element_type=F32, precision=lax.Precision.HIGHEST) + carry[...]
        carry[...] += jnp.sum(lf, axis=0, keepdims=True)

    return pl.pallas_call(
        body, grid=(s // ts,),
        in_specs=[pl.BlockSpec((ts, LANES), lambda i: (i, cb))],
        out_specs=pl.BlockSpec((ts, LANES), lambda i: (i, 0)),
        out_shape=jax.ShapeDtypeStruct((s, LANES), F32),
        scratch_shapes=[pltpu.VMEM((1, LANES), F32)],
        compiler_params=_cparams(("arbitrary",)), name="forget_cumsum")(z)


def _forget_cumsum_bwd(dq, dk, z, cb, nh):
    s = z.shape[0]
    ts = _divisor_tile(s, 512, LANES)
    nt = s // ts
    wd = nh * LANES

    def body(dq_ref, dk_ref, x_ref, o_ref, carry):
        @pl.when(pl.program_id(0) == 0)
        def _():
            carry[...] = jnp.zeros_like(carry)

        lane = _lane((ts, LANES))
        df = jnp.zeros((ts, LANES), F32)
        for h in range(nh):
            cols = slice(h * LANES, (h + 1) * LANES)
            d_h = _lane_column(dq_ref[:, cols], lane, FOX_Q_F) - _lane_column(dk_ref[:, cols], lane, FOX_Q_ONES)
            df = jnp.where(lane == h, d_h, df)
        r = lax.broadcasted_iota(jnp.int32, (ts, ts), 0)
        c = lax.broadcasted_iota(jnp.int32, (ts, ts), 1)
        tri = jnp.where(c >= r, 1.0, 0.0).astype(F32)
        rc = jnp.dot(tri, df, preferred_element_type=F32, precision=lax.Precision.HIGHEST) + carry[...]
        carry[...] += jnp.sum(df, axis=0, keepdims=True)
        o_ref[...] = (rc * (1.0 / (1.0 + jnp.exp(x_ref[...])))).astype(o_ref.dtype)

    return pl.pallas_call(
        body, grid=(nt,),
        in_specs=[pl.BlockSpec((ts, wd), lambda i: (nt - 1 - i, 0)),
                  pl.BlockSpec((ts, wd), lambda i: (nt - 1 - i, 0)),
                  pl.BlockSpec((ts, LANES), lambda i: (nt - 1 - i, cb))],
        out_specs=pl.BlockSpec((ts, LANES), lambda i: (nt - 1 - i, 0)),
        out_shape=jax.ShapeDtypeStruct((s, LANES), BF16),
        scratch_shapes=[pltpu.VMEM((1, LANES), F32)],
        compiler_params=_cparams(("arbitrary",)), name="forget_cumsum_bwd")(dq, dk, z)


def _flash_fwd(q, k, v, scale, nh, l_lane, *, name):
    s = q.shape[0]
    t = min(ATT_TILE, s)
    half = t // 2 if t % (2 * LANES) == 0 else t
    nq = s // t
    c = scale * LOG2E

    def body(q_ref, k_ref, v_ref, o_ref, qb_ref):
        i = pl.program_id(1)
        qb = q_ref[...]

        def scores(q_rows, k0, nk):
            kb = k_ref[pl.ds(pl.multiple_of(k0, half), nk), :]
            return lax.dot_general(q_rows, kb, _NT, preferred_element_type=F32)

        def update(sc, k0, nk, carry):
            m, acc = carry
            m_new = jnp.maximum(m, jnp.max(sc, axis=1, keepdims=True))
            p = jnp.exp2((sc - m_new) * c)
            alpha = jnp.exp2((m - m_new) * c)
            vb = v_ref[pl.ds(pl.multiple_of(k0, half), nk), :]
            acc = alpha * acc + jnp.dot(p.astype(BF16), vb, preferred_element_type=F32)
            return m_new, acc

        def full_block(j, cr):
            return update(scores(qb, j * t, t), j * t, t, cr)

        def group(jj, cr):
            for n in range(FWD_GROUP):
                cr = full_block(FWD_GROUP * jj + n, cr)
            return cr

        def causal(sc):
            row = lax.broadcasted_iota(jnp.int32, sc.shape, 0)
            col = lax.broadcasted_iota(jnp.int32, sc.shape, 1)
            return jnp.where(col <= row, sc, NEG)

        init = (jnp.full((t, 1), NEG, F32), jnp.zeros((t, LANES), F32))
        n_groups = i >> FWD_GROUP_LOG2
        carry = lax.fori_loop(0, n_groups, group, init)
        carry = lax.fori_loop(n_groups * FWD_GROUP, i, full_block, carry)
        m, acc = update(causal(scores(qb, i * t, t)), i * t, t, carry)
        lane = _lane((t, LANES))
        l = _lane_column(acc, lane, V_ONES)
        o_ref[...] = (acc / l).astype(o_ref.dtype)
        big_l = m + jnp.log(l) / scale
        qb_ref[...] = _put3(qb.astype(F32), lane, l_lane, _split3(-big_l)).astype(qb_ref.dtype)

    head_rows = pl.BlockSpec((t, LANES), lambda h, i: (i, h))
    head_all = pl.BlockSpec((s, LANES), lambda h, i: (0, h))
    return pl.pallas_call(
        body, grid=(nh, nq), in_specs=[head_rows, head_all, head_all], out_specs=[head_rows, head_rows],
        out_shape=[jax.ShapeDtypeStruct(q.shape, BF16), jax.ShapeDtypeStruct(q.shape, BF16)],
        compiler_params=_cparams(("parallel", "arbitrary")), name=name)(q, k, v)


def _flash_bwd(qb, k, v, do, scale, nh, n_feat, *, name):
    s = qb.shape[0]
    t = min(ATT_TILE, s)
    half = t // 2 if t % (2 * LANES) == 0 else t
    nq = s // t
    c = scale * LOG2E

    def body(q_ref, k_ref, v_ref, do_ref, dk_ref, dv_ref, dq_ref):
        j = pl.program_id(1)
        kb = k_ref[...]
        vb = v_ref[...]

        @pl.when(j == 0)
        def _():
            dq_ref[...] = jnp.zeros_like(dq_ref)

        def part(q0, n_q, n_k, carry, q_off):
            dk_acc, dv_acc = carry
            rows = pl.ds(pl.multiple_of(q0, half), n_q)
            qblk = q_ref[rows, :]
            dob = do_ref[rows, :]
            kbb, vbb = kb[:n_k], vb[:n_k]
            st = lax.dot_general(kbb, qblk, _NT, preferred_element_type=F32)
            if q_off is not None:
                key = lax.broadcasted_iota(jnp.int32, st.shape, 0)
                qry = lax.broadcasted_iota(jnp.int32, st.shape, 1) + q_off
                st = jnp.where(key <= qry, st, NEG)
            pt = jnp.exp2(st * c)
            dv_new = jnp.dot(pt.astype(BF16), dob, preferred_element_type=F32)
            dpt = lax.dot_general(vbb, dob, _NT, preferred_element_type=F32)
            dsb = (pt * dpt).astype(BF16)
            dk_new = jnp.dot(dsb, qblk, preferred_element_type=F32)
            dq_ref[rows, :] += lax.dot_general(dsb, kbb, _TN, preferred_element_type=F32)
            if n_k == t:
                return dk_acc + dk_new, dv_acc + dv_new
            return (jnp.concatenate([dk_acc[:n_k] + dk_new, dk_acc[n_k:]], axis=0),
                    jnp.concatenate([dv_acc[:n_k] + dv_new, dv_acc[n_k:]], axis=0))

        def block(i, carry):
            return part(i * t, t, t, carry, None)

        init = (jnp.zeros((t, LANES), F32), jnp.zeros((t, LANES), F32))
        carry = part(j * t + half, t - half, t, init, half) if half < t else part(j * t, t, t, init, 0)
        if half < t:
            carry = part(j * t, half, half, carry, 0)
        rest = nq - 1 - j
        carry = lax.cond((rest & 1) == 1, lambda cr: block(j + 1, cr), lambda cr: cr, carry)
        first = j + 1 + (rest & 1)

        def pair(ii, cr):
            i0 = first + 2 * ii
            return block(i0 + 1, block(i0, cr))

        dk_acc, dv_acc = lax.fori_loop(0, rest >> 1, pair, carry)
        dk_ref[...] = dk_acc * jnp.where(_lane((t, LANES)) < n_feat, scale, 1.0)
        dv_ref[...] = dv_acc.astype(dv_ref.dtype)

        @pl.when(j == nq - 1)
        def _():
            dq_ref[...] = dq_ref[...] * jnp.where(_lane((s, LANES)) < n_feat, scale, 1.0)

    head_rows = pl.BlockSpec((t, LANES), lambda h, j: (j, h))
    head_all = pl.BlockSpec((s, LANES), lambda h, j: (0, h))
    shp = jax.ShapeDtypeStruct(qb.shape, F32)
    return pl.pallas_call(
        body, grid=(nh, nq), in_specs=[head_all, head_rows, head_rows, head_all],
        out_specs=[head_rows, head_rows, head_all],
        out_shape=[shp, jax.ShapeDtypeStruct(qb.shape, BF16), shp],
        compiler_params=_cparams(("parallel", "arbitrary")), name=name)(qb, k, v, do)


def _with_delta(do, o, nh, *, name):
    def fn(ti, pa):
        lane = _lane((ti[0].shape[0], LANES))
        out = []
        for d_blk, o_blk in zip(_blocks(ti[0], nh), _blocks(ti[1], nh), strict=True):
            delta = jnp.sum(d_blk * o_blk.astype(F32), axis=1, keepdims=True)
            out.append(_put3(d_blk, lane, V_ONES, _split3(-delta)))
        return [jnp.concatenate(out, axis=1)], []

    (res,) = _rowwise(name, fn, [do, o], [], [(do.shape[1], BF16)])
    return res


def _pad_heads(a, nh):
    d = a.shape[-1] // nh
    a = a.reshape(a.shape[:-1] + (nh, d))
    a = jnp.pad(a, [(0, 0)] * (a.ndim - 1) + [(0, LANES - d)])
    return a.reshape(a.shape[:-2] + (nh * LANES,))


def _unpad_heads(a, nh, d):
    a = a.reshape(a.shape[:-1] + (nh, LANES))[..., :d]
    return a.reshape(a.shape[:-2] + (nh * d,))


def _pad_head_rows(w, nh):
    return _pad_heads(w.T, nh).T


def _unpad_head_rows(g, nh, d):
    return _unpad_heads(g.T, nh, d).T


class _ZLayout:
    def __init__(self, d):
        fw = FOX_HEADS * FOX_DIM
        self.src = {}
        off = 0
        for nm, w in (("cq", MLA_Q_LORA), ("ckv", MLA_KV_LORA), ("kr", MLA_ROPE), ("fq", fw), ("fk", fw),
                      ("fv", fw), ("fl", FOX_HEADS), ("ga", d), ("gb", d)):
            self.src[nm] = (off, w)
            off += w
        self.dst = {}
        off = 0
        for nm, w in (("ga", d), ("gb", d), ("fq", fw), ("fk", fw), ("fv", fw), ("cq", MLA_Q_LORA),
                      ("ckv", MLA_KV_LORA), ("kr", LANES), ("fl", LANES)):
            assert off % w == 0
            self.dst[nm] = (off, w)
            off += w
        self.width = off
        self.split = self.dst["cq"][0]
        assert all((o - self.split) % w == 0 for o, w in self.dst.values() if o >= self.split)

    def to_kernel(self, w):
        def seg(nm):
            off, wd = self.src[nm]
            return w[..., off:off + wd]

        def pad(a, left, total):
            return jnp.pad(a, [(0, 0)] * (a.ndim - 1) + [(left, total - left - a.shape[-1])])

        return jnp.concatenate([seg("ga"), seg("gb"), seg("fq"), seg("fk"), seg("fv"), seg("cq"), seg("ckv"),
                                pad(seg("kr"), MLA_NOPE, LANES), pad(seg("fl"), 0, LANES)], axis=-1)

    def from_kernel(self, g):
        def seg(nm, lo=0, hi=None):
            off, wd = self.dst[nm]
            return g[..., off + lo:off + (wd if hi is None else hi)]

        return jnp.concatenate([seg("cq"), seg("ckv"), seg("kr", MLA_NOPE, MLA_NOPE + MLA_ROPE), seg("fq"), seg("fk"),
                                seg("fv"), seg("fl", 0, FOX_HEADS), seg("ga"), seg("gb")], axis=-1)


def _local_step(x, positions, target, wts, late_weights, send_grads):
    s, d = x.shape
    zl = _ZLayout(d)
    hw = MLA_HEADS * LANES
    assert MLA_HEADS == FOX_HEADS
    scale_mla = (MLA_NOPE + MLA_ROPE) ** -0.5
    scale_fox = FOX_DIM ** -0.5

    inv_freq = ROPE_THETA ** (-jnp.arange(HALF_ROPE, dtype=F32) / HALF_ROPE)
    ang = positions.astype(F32)[:, None] * inv_freq
    cos, sin = jnp.cos(ang), jnp.sin(ang)
    tail = jnp.zeros((s, LANES - MLA_NOPE - MLA_ROPE), F32)
    rc = jnp.concatenate([jnp.ones((s, MLA_NOPE), F32), cos, cos, tail], axis=1)
    ra = jnp.concatenate([jnp.zeros((s, MLA_NOPE), F32), -sin, jnp.zeros((s, HALF_ROPE), F32), tail], axis=1)
    rb = jnp.concatenate([jnp.zeros((s, MLA_NOPE + HALF_ROPE), F32), sin, tail], axis=1)

    w_in = zl.to_kernel(wts["w_in"])
    b_in = zl.to_kernel(wts["b_in"])

    def f_norm_in(ti, pa):
        y, _ = _rms(ti[0], pa[0])
        return [y], []

    (h,) = _rowwise("norm_in", f_norm_in, [x], [wts["ln_pre_mix"]], [(d, BF16)])
    z_lo = _mm(h, w_in[:, :zl.split], bias=b_in[:, :zl.split], out_dtype=BF16, name="proj_in_lo")
    z = _mm(h, w_in[:, zl.split:], bias=b_in[:, zl.split:], out_dtype=F32, name="proj_in_hi")

    def zwin(nm):
        off, wd = zl.dst[nm]
        return (z_lo, off // wd, wd) if off < zl.split else (z, (off - zl.split) // wd, wd)

    wts = {**wts, **late_weights("mixer", z)}
    w_uq = _pad_heads(wts["w_uq"], MLA_HEADS)
    w_ukv = jnp.concatenate([_pad_heads(wts["w_uk"], MLA_HEADS), _pad_heads(wts["w_uv"], MLA_HEADS)], axis=1)
    w_o_mla = _pad_head_rows(wts["w_o_mla"], MLA_HEADS)
    w_o_fox = _pad_head_rows(wts["w_o_fox"], FOX_HEADS)

    def f_mla_norms(ti, pa):
        cqn, _ = _rms(ti[0], pa[0])
        ckvn, _ = _rms(ti[1], pa[1])
        return [cqn, ckvn], []

    cqn, ckvn = _rowwise("mla_norms", f_mla_norms, [zwin("cq"), zwin("ckv")],
                         [wts["q_a_norm"], wts["kv_a_norm"]], [(MLA_Q_LORA, BF16), (MLA_KV_LORA, BF16)])
    qf = _mm(cqn, w_uq, out_dtype=F32, name="proj_uq")
    kv = _mm(ckvn, w_ukv, out_dtype=BF16, name="proj_ukv")

    def f_rope_q(ti, pa):
        xq, c_, a_, b_ = ti
        return [jnp.concatenate([_rope_block(blk, c_, a_, b_) for blk in _blocks(xq, MLA_HEADS)], axis=1)], []

    (q_mla,) = _rowwise("rope_q", f_rope_q, [qf, rc, ra, rb], [], [(hw, BF16)])

    def f_mla_kv(ti, pa):
        kn, vn, kr, c_, a_, b_ = ti
        lane = _lane(kr.shape)
        k_tail = jnp.where((lane >= MLA_Q_L) & (lane < MLA_Q_L + 3), 1.0, _rope_block(kr, c_, a_, b_))
        ones_v = (lane >= V_ONES) & (lane < V_ONES + 3)
        k_out = [jnp.where(lane < MLA_NOPE, blk.astype(F32), k_tail) for blk in _blocks(kn, MLA_HEADS)]
        v_out = [jnp.where(ones_v, 1.0, blk.astype(F32)) for blk in _blocks(vn, MLA_HEADS)]
        return [jnp.concatenate(k_out, axis=1), jnp.concatenate(v_out, axis=1)], []

    k_mla, v_mla = _rowwise("mla_kv", f_mla_kv, [(kv, 0, hw), (kv, 1, hw), zwin("kr"), rc, ra, rb], [],
                            [(hw, BF16), (hw, BF16)])
    o_mla, qb_mla = _flash_fwd(q_mla, k_mla, v_mla, scale_mla, MLA_HEADS, MLA_Q_L, name="mla_fwd")

    fl_cb = zwin("fl")[1]
    fcol = _forget_cumsum(z, fl_cb)

    def f_fox_qkv(ti, pa):
        spread = _head_spread(FOX_HEADS, FOX_DIM)
        fq, fk, fv = (jnp.dot(a, spread, preferred_element_type=F32) for a in ti[:3])
        fc = ti[3]
        lane = _lane(fc.shape)
        ones_q = (lane >= FOX_Q_ONES) & (lane < FOX_Q_ONES + 3)
        ones_k = (lane >= FOX_Q_F) & (lane < FOX_Q_ONES)
        ones_v = (lane >= V_ONES) & (lane < V_ONES + 3)
        q_out, k_out, v_out = [], [], []
        for hh, (qblk, kblk, vblk) in enumerate(zip(_blocks(fq, FOX_HEADS), _blocks(fk, FOX_HEADS),
                                                    _blocks(fv, FOX_HEADS), strict=True)):
            f_h = _lane_column(fc, lane, hh) * (1.0 / scale_fox)
            q_out.append(_put3(jnp.where(ones_q, 1.0, qblk), lane, FOX_Q_F, _split3(f_h)))
            k_out.append(_put3(jnp.where(ones_k, 1.0, kblk), lane, FOX_Q_ONES, _split3(-f_h)))
            v_out.append(jnp.where(ones_v, 1.0, vblk))
        return [jnp.concatenate(q_out, axis=1), jnp.concatenate(k_out, axis=1), jnp.concatenate(v_out, axis=1)], []

    q_fox, k_fox, v_fox = _rowwise("fox_qkv", f_fox_qkv, [zwin("fq"), zwin("fk"), zwin("fv"), fcol],
                                   [], [(hw, BF16)] * 3)
    o_fox, qb_fox = _flash_fwd(q_fox, k_fox, v_fox, scale_fox, FOX_HEADS, FOX_Q_L, name="fox_fwd")

    y_mla = _mm(o_mla, w_o_mla, out_dtype=BF16, name="proj_o_mla")
    y_fox = _mm(o_fox, w_o_fox, out_dtype=BF16, name="proj_o_fox")

    def f_gate(ti, pa):
        ga, gb, ya, yb = (a.astype(F32) for a in ti)
        return [_sigmoid(ga) * ya + _sigmoid(gb) * yb], []

    (merged,) = _rowwise("gate", f_gate, [zwin("ga"), zwin("gb"), y_mla, y_fox], [], [(d, BF16)])
    wts = {**wts, **late_weights("mlp", merged)}
    mix = _mm(merged, wts["w_out"], out_dtype=F32, name="proj_out")

    def f_resid1(ti, pa):
        xa, mx = ti
        y, _ = _rms(mx, pa[0])
        x1 = xa + y
        h2, _ = _rms(x1, pa[1])
        return [x1, h2], []

    x1, h2 = _rowwise("resid_mix", f_resid1, [x, mix], [wts["ln_post_mix"], wts["ln_pre_mlp"]], [(d, F32), (d, BF16)])

    def relu2(acc, ex):
        r = jnp.maximum(acc, 0.0)
        return [acc, r * r]

    u, act = _mm(h2, wts["w_ff1"], out_dtype=[BF16, BF16], name="ff1", epilogue=relu2)
    mo = _mm(act, wts["w_ff2"], out_dtype=F32, name="ff2")

    def f_loss(ti, pa):
        xa, mv, tg = ti
        y, _ = _rms(mv, pa[0])
        err = (xa + y) - tg
        g2 = err / d
        dmo, dg = _rms_bwd(mv, pa[0], g2)
        return [g2, dmo], [jnp.sum(err * err, axis=0, keepdims=True), dg]

    g2, d_mo, loss_cols, g_ln_post_mlp = _rowwise("loss", f_loss, [x1, mo, target], [wts["ln_post_mlp"]],
                                                  [(d, F32), (d, BF16)], [d, d])
    loss = 0.5 * jnp.sum(loss_cols) / d

    grads = {"ln_post_mlp": g_ln_post_mlp}
    grads["w_ff2"] = _mm_tn(act, d_mo, name="grad_ff2")

    def relu2_bwd(acc, ex):
        return [acc * (2.0 * jnp.maximum(ex[0], 0.0))]

    (d_u,) = _mm(d_mo, wts["w_ff2"], out_dtype=[BF16], name="ff2_bwd", transpose_b=True, extras=[u], epilogue=relu2_bwd)
    grads["w_ff1"] = _mm_tn(h2, d_u, name="grad_ff1")
    d_h2 = _mm(d_u, wts["w_ff1"], out_dtype=F32, name="ff1_bwd", transpose_b=True)

    def f_resid1_bwd(ti, pa):
        gres, dh2, x1v, mx = ti
        dx1n, dg_pre_mlp = _rms_bwd(x1v, pa[1], dh2)
        dx1 = gres + dx1n
        dmix, dg_post_mix = _rms_bwd(mx, pa[0], dx1)
        return [dx1, dmix], [dg_post_mix, dg_pre_mlp]

    d_x1, d_mix, grads["ln_post_mix"], grads["ln_pre_mlp"] = _rowwise(
        "resid_mix_bwd", f_resid1_bwd, [g2, d_h2, x1, mix], [wts["ln_post_mix"], wts["ln_pre_mlp"]],
        [(d, F32), (d, BF16)], [d, d])
    grads["w_out"] = _mm_tn(merged, d_mix, name="grad_out")
    d_merged = _mm(d_mix, wts["w_out"], out_dtype=BF16, name="proj_out_bwd", transpose_b=True)

    def f_gate_bwd(ti, pa):
        dm, ga, gb, ya, yb = (a.astype(F32) for a in ti)
        sa, sb = _sigmoid(ga), _sigmoid(gb)
        return [dm * sa, dm * sb, dm * ya * (sa * (1.0 - sa)), dm * yb * (sb * (1.0 - sb))], []

    d_ya, d_yb, d_ga, d_gb = _rowwise("gate_bwd", f_gate_bwd,
                                      [d_merged, zwin("ga"), zwin("gb"), y_mla, y_fox], [],
                                      [(d, BF16)] * 4)
    grads["w_o_mla"] = _unpad_head_rows(_mm_tn(o_mla, d_ya, name="grad_o_mla"), MLA_HEADS, MLA_V)
    grads["w_o_fox"] = _unpad_head_rows(_mm_tn(o_fox, d_yb, name="grad_o_fox"), FOX_HEADS, FOX_DIM)
    w_o_mla = w_o_mla + send_grads("early", {nm: grads[nm] for nm in EARLY_GRADS}).astype(BF16)
    do_mla = _with_delta(_mm(d_ya, w_o_mla, out_dtype=F32, name="proj_o_mla_bwd", transpose_b=True), o_mla,
                         MLA_HEADS, name="mla_delta")
    do_fox = _with_delta(_mm(d_yb, w_o_fox, out_dtype=F32, name="proj_o_fox_bwd", transpose_b=True), o_fox,
                         FOX_HEADS, name="fox_delta")

    dk_mla, dv_mla, dq_mla = _flash_bwd(qb_mla, k_mla, v_mla, do_mla, scale_mla, MLA_HEADS, MLA_NOPE + MLA_ROPE,
                                        name="mla_bwd")
    dk_fox, dv_fox, dq_fox = _flash_bwd(qb_fox, k_fox, v_fox, do_fox, scale_fox, FOX_HEADS, FOX_DIM, name="fox_bwd")
    d_fl = _forget_cumsum_bwd(dq_fox, dk_fox, z, fl_cb, FOX_HEADS)

    def f_rope_q_bwd(ti, pa):
        g, c_, a_, b_ = ti
        return [jnp.concatenate([_rope_block(blk, c_, -a_, -b_) for blk in _blocks(g, MLA_HEADS)], axis=1)], []

    (d_qf,) = _rowwise("rope_q_bwd", f_rope_q_bwd, [dq_mla, rc, ra, rb], [], [(hw, BF16)])
    grads["w_uq"] = _unpad_heads(_mm_tn(cqn, d_qf, name="grad_uq"), MLA_HEADS, MLA_NOPE + MLA_ROPE)
    d_cqn = _mm(d_qf, w_uq, out_dtype=F32, name="proj_uq_bwd", transpose_b=True)

    def f_mla_kv_bwd(ti, pa):
        gk, gv, c_, a_, b_ = ti
        k_blocks = _blocks(gk, MLA_HEADS)
        tot = k_blocks[0]
        for blk in k_blocks[1:]:
            tot = tot + blk
        return [jnp.concatenate([gk, gv], axis=1), _rope_block(tot, c_, -a_, -b_)], []

    d_kv, d_kr = _rowwise("mla_kv_bwd", f_mla_kv_bwd, [dk_mla, dv_mla, rc, ra, rb], [], [(2 * hw, BF16), (LANES, BF16)])
    g_ukv = _mm_tn(ckvn, d_kv, name="grad_ukv")
    grads["w_uk"] = _unpad_heads(g_ukv[:, :hw], MLA_HEADS, MLA_NOPE)
    grads["w_uv"] = _unpad_heads(g_ukv[:, hw:], MLA_HEADS, MLA_V)
    d_ckvn = _mm(d_kv, w_ukv, out_dtype=F32, name="proj_ukv_bwd", transpose_b=True)

    def f_mla_norms_bwd(ti, pa):
        cq, ckv, dcqn, dckvn = ti
        dcq, dg_q = _rms_bwd(cq, pa[0], dcqn)
        dckv, dg_kv = _rms_bwd(ckv, pa[1], dckvn)
        return [dcq, dckv], [dg_q, dg_kv]

    d_cq, d_ckv, grads["q_a_norm"], grads["kv_a_norm"] = _rowwise(
        "mla_norms_bwd", f_mla_norms_bwd, [zwin("cq"), zwin("ckv"), d_cqn, d_ckvn],
        [wts["q_a_norm"], wts["kv_a_norm"]], [(MLA_Q_LORA, BF16), (MLA_KV_LORA, BF16)], [MLA_Q_LORA, MLA_KV_LORA])

    def f_fox_compact(ti, pa):
        gather = _head_spread(FOX_HEADS, FOX_DIM)
        return [lax.dot_general(a.astype(BF16), gather, _NT, preferred_element_type=F32) for a in ti], []

    d_fq, d_fk, d_fv = _rowwise("fox_compact", f_fox_compact, [dq_fox, dk_fox, dv_fox], [],
                                [(FOX_HEADS * FOX_DIM, BF16)] * 3)
    d_z = jnp.concatenate([d_ga, d_gb, d_fq, d_fk, d_fv, d_cq, d_ckv, d_kr, d_fl], axis=1)
    assert d_z.shape[1] == zl.width

    def f_bias(ti, pa):
        return [], [jnp.sum(ti[0].astype(F32), axis=0, keepdims=True)]

    (g_b_in,) = _rowwise("grad_b_in", f_bias, [d_z], [], [], [zl.width])
    grads["b_in"] = zl.from_kernel(g_b_in)
    grads["w_in"] = zl.from_kernel(_mm_tn(h, d_z, name="grad_in"))
    tok = send_grads("late", {nm: grads[nm] for nm, _ in BIG if nm not in EARLY_GRADS})
    d_h = _mm(d_z, w_in, bias=jnp.zeros((1, d), F32) + tok, out_dtype=F32, name="proj_in_bwd", transpose_b=True)

    def f_norm_in_bwd(ti, pa):
        dx1v, dh, xa = ti
        dxn, dg = _rms_bwd(xa, pa[0], dh)
        return [dx1v + dxn], [dg]

    grad_x, grads["ln_pre_mix"] = _rowwise("norm_in_bwd", f_norm_in_bwd, [d_x1, d_h, x], [wts["ln_pre_mix"]],
                                           [(d, F32)], [d])
    return loss, grad_x, grads


class _PackLayout:
    def __init__(self, shapes):
        self.shapes = list(shapes)
        self.width = _round_up(max(b for _, b in shapes), LANES)
        self.bands = []
        row = 0
        shelf = []
        for idx, (a, b) in enumerate(shapes):
            if 2 * _round_up(b, LANES) > self.width:
                self.bands.append((row, _round_up(a, 32), [(idx, 0)]))
                row += _round_up(a, 32)
            else:
                shelf.append(idx)
        col, items = 0, []
        for idx in shelf:
            wb = _round_up(shapes[idx][1], LANES)
            if col + wb > self.width:
                hgt = max(_round_up(shapes[i][0], 32) for i, _ in items)
                self.bands.append((row, hgt, items))
                row += hgt
                col, items = 0, []
            items.append((idx, col))
            col += wb
        if items:
            hgt = max(_round_up(shapes[i][0], 32) for i, _ in items)
            self.bands.append((row, hgt, items))
            row += hgt
        self.rows = _round_up(row, 16 * LOCAL_PIECES)

    def pack(self, arrs):
        lead = arrs[0].shape[:-2]
        no_pad = [(0, 0)] * len(lead)
        bands = []
        for _, hgt, items in self.bands:
            parts = []
            for k, (idx, col) in enumerate(items):
                a, b = self.shapes[idx]
                nxt = items[k + 1][1] if k + 1 < len(items) else self.width
                parts.append(jnp.pad(arrs[idx], no_pad + [(0, hgt - a), (0, nxt - col - b)]))
            bands.append(parts[0] if len(parts) == 1 else jnp.concatenate(parts, axis=-1))
        used = sum(hgt for _, hgt, _ in self.bands)
        if used < self.rows:
            bands.append(jnp.zeros(lead + (self.rows - used, self.width), arrs[0].dtype))
        return jnp.concatenate(bands, axis=-2)

    def unpack(self, packed):
        out = [None] * len(self.shapes)
        for row, _, items in self.bands:
            for idx, col in items:
                a, b = self.shapes[idx]
                out[idx] = packed[..., row:row + a, col:col + b]
        return out


def _to_shards(g, axis):
    if axis == 0:
        return g.reshape(N_CHIPS, g.shape[0] // N_CHIPS, g.shape[1])
    return jnp.stack(jnp.split(g, N_CHIPS, axis=1))


def _from_shards(s4, axis):
    n, a, b = s4.shape
    if axis == 0:
        return s4.reshape(n * a, b)
    return jnp.concatenate([s4[ch] for ch in range(n)], axis=1)


ANY = pl.BlockSpec(memory_space=pl.ANY)


def _place():
    return lax.axis_index("x"), lax.axis_index("y"), lax.axis_index("c")


def _gather_weights(wpk):
    rows, wd = wpk.shape
    half = rows // 2

    def body(w_ref, out_ref, send_sems, recv_sems, local_sems):
        x, y, c = _place()
        sibling = (x, y, 1 - c)
        chips = [(1 - x, y), (x, 1 - y), (1 - x, 1 - y)]

        def slab(chip, hf):
            return out_ref.at[2 * chip[0] + chip[1], pl.ds(hf * half, half), :]

        def copy(k, chip, hf, to, src=None):
            return pltpu.make_async_remote_copy(
                src_ref=slab(chip, hf) if src is None else src, dst_ref=slab(chip, hf),
                send_sem=send_sems.at[k], recv_sem=recv_sems.at[k], device_id=to, device_id_type=MESH)

        piece = rows // LOCAL_PIECES
        mine = [pltpu.make_async_copy(w_ref.at[pl.ds(n * piece, piece), :],
                                      out_ref.at[2 * x + y, pl.ds(n * piece, piece), :], local_sems.at[n])
                for n in range(LOCAL_PIECES)]
        for cp in mine:
            cp.start()
        first = [copy(j, (x, y), c, (*chip, c), src=w_ref.at[pl.ds(c * half, half), :]) for j, chip in enumerate(chips)]
        for cp in first:
            cp.start()
        passed = [copy(3 + j, chip, c, sibling) for j, chip in enumerate(chips)]
        for j, chip in enumerate(chips):
            copy(j, chip, c, (x, y, c)).wait_recv()
            passed[j].start()
        for j, chip in enumerate(chips):
            copy(3 + j, chip, 1 - c, (x, y, c)).wait_recv()
        for cp in first + passed:
            cp.wait_send()
        for cp in mine:
            cp.wait()

    assert rows % (16 * LOCAL_PIECES) == 0
    return pl.pallas_call(
        body, out_shape=jax.ShapeDtypeStruct((N_CHIPS, rows, wd), wpk.dtype),
        in_specs=[ANY], out_specs=ANY,
        scratch_shapes=[pltpu.SemaphoreType.DMA((6,)), pltpu.SemaphoreType.DMA((6,)),
                        pltpu.SemaphoreType.DMA((LOCAL_PIECES,))],
        name="gather_weights")(wpk)


HBM = pl.BlockSpec(memory_space=pltpu.HBM)
SEM = pl.BlockSpec(memory_space=pltpu.SEMAPHORE)
EFFECT = pltpu.SideEffectType.DATAFLOW_SIDE_EFFECTING
N_LATE = 6


def _gather_late_start(wpk, tag):
    rows, wd = wpk.shape
    half = rows // 2

    def body(w_ref, land_ref, send_sems, recv_sems, w_thru, land_thru, token):
        x, y, c = _place()
        chips = [(1 - x, y), (x, 1 - y), (1 - x, 1 - y)]
        for j, chip in enumerate(chips):
            for to_core in range(2):
                pltpu.make_async_remote_copy(
                    src_ref=w_ref.at[pl.ds(c * half, half), :],
                    dst_ref=land_ref.at[2 * x + y, pl.ds(c * half, half), :],
                    send_sem=send_sems.at[2 * j + to_core], recv_sem=recv_sems.at[2 * j + c],
                    device_id=(*chip, to_core), device_id_type=MESH).start()
        token[...] = jnp.zeros_like(token)

    land = lax.empty((N_CHIPS, rows, wd), wpk.dtype)
    return pl.pallas_call(
        body, name="gather_" + tag + "_start",
        out_shape=(pltpu.SemaphoreType.DMA((N_LATE,)), pltpu.SemaphoreType.DMA((N_LATE,)),
                   pltpu.HBM(wpk.shape, wpk.dtype), pltpu.HBM(land.shape, land.dtype),
                   jax.ShapeDtypeStruct((8, LANES), F32)),
        in_specs=(HBM, HBM), out_specs=(SEM, SEM, HBM, HBM, pl.BlockSpec(memory_space=pltpu.VMEM)),
        input_output_aliases={0: 2, 1: 3},
        compiler_params=pltpu.CompilerParams(has_side_effects=EFFECT),
    )(pltpu.with_memory_space_constraint(wpk, pltpu.HBM), pltpu.with_memory_space_constraint(land, pltpu.HBM))


def _gather_late_wait(send_sems, recv_sems, w_thru, land_thru, after, tag):
    rows, wd = w_thru.shape
    half = rows // 2

    def body(w_ref, land_ref, send_sems, recv_sems, after_ref, w_dead, land_out):
        x, y, c = _place()
        for n in range(N_LATE):
            cp = pltpu.make_async_remote_copy(
                src_ref=w_ref.at[pl.ds(0, half), :], dst_ref=land_ref.at[0, pl.ds(0, half), :],
                send_sem=send_sems.at[n], recv_sem=recv_sems.at[n], device_id=(x, y, c), device_id_type=MESH)
            cp.wait_send()
            cp.wait_recv()

    return pl.pallas_call(
        body, name="gather_" + tag + "_wait",
        out_shape=(pltpu.HBM(w_thru.shape, w_thru.dtype), pltpu.HBM(land_thru.shape, land_thru.dtype)),
        in_specs=(HBM, HBM, SEM, SEM, ANY), out_specs=(HBM, HBM), input_output_aliases={0: 0, 1: 1},
        compiler_params=pltpu.CompilerParams(has_side_effects=EFFECT),
    )(w_thru, land_thru, send_sems, recv_sems, after)[1]


N_PART = 7


def _reduce_start(gbf, tag):
    _, _, hrows, wd = gbf.shape

    def body(g_ref, land_ref, send_sems, recv_sems, g_thru, land_thru, token):
        x, y, c = _place()
        chips = [(1 - x, y), (x, 1 - y), (1 - x, 1 - y)]
        for j, chip in enumerate(chips):
            for to_core in range(2):
                pltpu.make_async_remote_copy(
                    src_ref=g_ref.at[2 * chip[0] + chip[1], to_core], dst_ref=land_ref.at[2 * j + c],
                    send_sem=send_sems.at[2 * j + to_core], recv_sem=recv_sems.at[2 * j + c],
                    device_id=(*chip, to_core), device_id_type=MESH).start()
        pltpu.make_async_remote_copy(
            src_ref=g_ref.at[2 * x + y, 1 - c], dst_ref=land_ref.at[N_PART - 1],
            send_sem=send_sems.at[N_PART - 1], recv_sem=recv_sems.at[N_PART - 1],
            device_id=(x, y, 1 - c), device_id_type=MESH).start()
        token[...] = jnp.zeros_like(token)

    land = lax.empty((N_PART, hrows, wd), gbf.dtype)
    return pl.pallas_call(
        body, name="reduce_" + tag + "_start",
        out_shape=(pltpu.SemaphoreType.DMA((N_PART,)), pltpu.SemaphoreType.DMA((N_PART,)),
                   pltpu.HBM(gbf.shape, gbf.dtype), pltpu.HBM(land.shape, land.dtype),
                   jax.ShapeDtypeStruct((8, LANES), F32)),
        in_specs=(HBM, HBM), out_specs=(SEM, SEM, HBM, HBM, pl.BlockSpec(memory_space=pltpu.VMEM)),
        input_output_aliases={0: 2, 1: 3},
        compiler_params=pltpu.CompilerParams(has_side_effects=EFFECT),
    )(pltpu.with_memory_space_constraint(gbf, pltpu.HBM), pltpu.with_memory_space_constraint(land, pltpu.HBM))


def _reduce_wait(send_sems, recv_sems, g_thru, land_thru, after, tag):
    def body(g_ref, land_ref, send_sems, recv_sems, after_ref, g_dead, land_out):
        x, y, c = _place()
        for n in range(N_PART):
            cp = pltpu.make_async_remote_copy(
                src_ref=g_ref.at[0, 0], dst_ref=land_ref.at[0], send_sem=send_sems.at[n], recv_sem=recv_sems.at[n],
                device_id=(x, y, c), device_id_type=MESH)
            cp.wait_send()
            cp.wait_recv()

    return pl.pallas_call(
        body, name="reduce_" + tag + "_wait",
        out_shape=(pltpu.HBM(g_thru.shape, g_thru.dtype), pltpu.HBM(land_thru.shape, land_thru.dtype)),
        in_specs=(HBM, HBM, SEM, SEM, ANY), out_specs=(HBM, HBM), input_output_aliases={0: 0, 1: 1},
        compiler_params=pltpu.CompilerParams(has_side_effects=EFFECT),
    )(g_thru, land_thru, send_sems, recv_sems, after)[1]


def _sibling_swap(mine):
    def body(m_ref, out_ref, send_sem, recv_sem):
        x, y, c = _place()
        cp = pltpu.make_async_remote_copy(
            src_ref=m_ref, dst_ref=out_ref, send_sem=send_sem, recv_sem=recv_sem,
            device_id=(x, y, 1 - c), device_id_type=MESH)
        cp.start()
        cp.wait()

    return pl.pallas_call(
        body, out_shape=jax.ShapeDtypeStruct(mine.shape, mine.dtype), in_specs=[ANY], out_specs=ANY,
        scratch_shapes=[pltpu.SemaphoreType.DMA, pltpu.SemaphoreType.DMA], name="grad_sibling_swap")(mine)


def _adamw(w, g, m, v):
    m = ADAM_B1 * m + (1.0 - ADAM_B1) * g
    v = ADAM_B2 * v + (1.0 - ADAM_B2) * (g * g)
    m_hat = m / (1.0 - ADAM_B1 ** ADAM_STEP)
    v_hat = v / (1.0 - ADAM_B2 ** ADAM_STEP)
    delta = -ADAM_LR * (m_hat / (jnp.sqrt(v_hat) + ADAM_EPS) + ADAM_WD * w)
    return delta, m, v


def _small_allreduce_adamw(gs, ws, ms, vs):
    n_dev = 8
    n_par = len(gs)
    wd = PACK_W
    chunks = []
    for p, g in enumerate(gs):
        for off in range(0, g.shape[1], wd):
            chunks.append((p, len(chunks), off, min(wd, g.shape[1] - off)))
    rows = _round_up(len(chunks), 8)

    def body(*refs):
        g_refs, w_refs, m_refs, v_refs = (refs[k * n_par:(k + 1) * n_par] for k in range(4))
        go_refs, d_refs, mo_refs, vo_refs = (refs[(4 + k) * n_par:(5 + k) * n_par] for k in range(4))
        mine_ref, all_ref, send_sems, recv_sems = refs[8 * n_par:]
        x, y, c = _place()
        me, sibling = (x, y, c), (x, y, 1 - c)
        chips = [(1 - x, y), (x, 1 - y), (1 - x, 1 - y)]

        def slot(px, py, pc):
            return all_ref.at[4 * px + 2 * py + pc]

        def copy(k, block, to, src=None):
            return pltpu.make_async_remote_copy(
                src_ref=slot(*block) if src is None else src, dst_ref=slot(*block),
                send_sem=send_sems.at[k], recv_sem=recv_sems.at[k], device_id=to, device_id_type=MESH)

        mine_ref[...] = jnp.zeros_like(mine_ref)
        for p, row, off, width in chunks:
            mine_ref[row:row + 1, 0:width] = g_refs[p][:, off:off + width]
        all_ref[4 * x + 2 * y + c] = mine_ref[...]
        first = [copy(0, me, sibling, src=mine_ref)]
        first += [copy(1 + j, me, (*chip, c), src=mine_ref) for j, chip in enumerate(chips)]
        for cp in first:
            cp.start()
        passed = [copy(4 + j, (*chip, c), sibling) for j, chip in enumerate(chips)]
        for j, chip in enumerate(chips):
            copy(1 + j, (*chip, c), me).wait_recv()
            passed[j].start()
        copy(0, sibling, me).wait_recv()
        for j, chip in enumerate(chips):
            copy(4 + j, (*chip, 1 - c), me).wait_recv()
        for cp in first + passed:
            cp.wait_send()
        tot = jnp.zeros((rows, wd), F32)
        for dev in range(n_dev):
            tot = tot + all_ref[dev]
        mine_ref[...] = tot
        for p, row, off, width in chunks:
            cols = slice(off, off + width)
            g = mine_ref[row:row + 1, 0:width]
            delta, m_new, v_new = _adamw(w_refs[p][:, cols], g, m_refs[p][:, cols], v_refs[p][:, cols])
            go_refs[p][:, cols] = g
            d_refs[p][:, cols] = delta
            mo_refs[p][:, cols] = m_new
            vo_refs[p][:, cols] = v_new

    vm = pl.BlockSpec(memory_space=pltpu.VMEM)
    shp = [jax.ShapeDtypeStruct(g.shape, F32) for g in gs]
    res = pl.pallas_call(
        body, out_shape=shp * 4, in_specs=[vm] * (4 * n_par), out_specs=[vm] * (4 * n_par),
        scratch_shapes=[pltpu.VMEM((rows, wd), F32), pltpu.VMEM((n_dev, rows, wd), F32),
                        pltpu.SemaphoreType.DMA((7,)), pltpu.SemaphoreType.DMA((7,))],
        name="small_allreduce_adamw")(*gs, *ws, *ms, *vs)
    return [res[k * n_par:(k + 1) * n_par] for k in range(4)]


def kernel(x, positions, ln_pre_mix, ln_post_mix, ln_pre_mlp, ln_post_mlp, w_in, b_in, q_a_norm, w_uq, kv_a_norm, w_uk, w_uv, w_o_mla, w_o_fox, w_out, w_ff1, w_ff2, loss_target, m_ln_pre_mix, m_ln_post_mix, m_ln_pre_mlp, m_ln_post_mlp, m_w_in, m_b_in, m_q_a_norm, m_w_uq, m_kv_a_norm, m_w_uk, m_w_uv, m_w_o_mla, m_w_o_fox, m_w_out, m_w_ff1, m_w_ff2, v_ln_pre_mix, v_ln_post_mix, v_ln_pre_mlp, v_ln_post_mlp, v_w_in, v_b_in, v_q_a_norm, v_w_uq, v_kv_a_norm, v_w_uk, v_w_uv, v_w_o_mla, v_w_o_fox, v_w_out, v_w_ff1, v_w_ff2):
    w = dict(ln_pre_mix=ln_pre_mix, ln_post_mix=ln_post_mix, ln_pre_mlp=ln_pre_mlp, ln_post_mlp=ln_post_mlp, w_in=w_in,
             b_in=b_in, q_a_norm=q_a_norm, w_uq=w_uq, kv_a_norm=kv_a_norm, w_uk=w_uk, w_uv=w_uv, w_o_mla=w_o_mla,
             w_o_fox=w_o_fox, w_out=w_out, w_ff1=w_ff1, w_ff2=w_ff2)
    mom = dict(ln_pre_mix=m_ln_pre_mix, ln_post_mix=m_ln_post_mix, ln_pre_mlp=m_ln_pre_mlp, ln_post_mlp=m_ln_post_mlp,
               w_in=m_w_in, b_in=m_b_in, q_a_norm=m_q_a_norm, w_uq=m_w_uq, kv_a_norm=m_kv_a_norm, w_uk=m_w_uk,
               w_uv=m_w_uv, w_o_mla=m_w_o_mla, w_o_fox=m_w_o_fox, w_out=m_w_out, w_ff1=m_w_ff1, w_ff2=m_w_ff2)
    var = dict(ln_pre_mix=v_ln_pre_mix, ln_post_mix=v_ln_post_mix, ln_pre_mlp=v_ln_pre_mlp, ln_post_mlp=v_ln_post_mlp,
               w_in=v_w_in, b_in=v_b_in, q_a_norm=v_q_a_norm, w_uq=v_w_uq, kv_a_norm=v_kv_a_norm, w_uk=v_w_uk,
               w_uv=v_w_uv, w_o_mla=v_w_o_mla, w_o_fox=v_w_o_fox, w_out=v_w_out, w_ff1=v_w_ff1, w_ff2=v_w_ff2)

    big_names = [nm for nm, _ in BIG]
    c = lax.axis_index("c")
    chip = 2 * lax.axis_index("x") + lax.axis_index("y")

    axes = dict(BIG)

    def assemble(names, lay, gathered):
        return {nm: _from_shards(s4, axes[nm] - 1) for nm, s4 in zip(names, lay.unpack(gathered), strict=True)}

    groups = {"mixer": [nm for nm in big_names if nm in MIXER], "mlp": [nm for nm in big_names if nm in LATE]}
    first_names = [nm for nm in big_names if nm not in MIXER and nm not in LATE]
    full = {nm: wv for nm, wv in w.items() if nm in SMALL}
    lay_first = _PackLayout([w[nm].shape[1:] for nm in first_names])
    first = _gather_weights(lay_first.pack([w[nm][0].astype(BF16) for nm in first_names]))
    full.update(assemble(first_names, lay_first, first))
    travelling = {}
    for tag, names in groups.items():
        lay = _PackLayout([w[nm].shape[1:] for nm in names])
        shard = lay.pack([w[nm][0].astype(BF16) for nm in names])
        started = _gather_late_start(lax.optimization_barrier((shard, first))[0], tag)
        travelling[tag] = (names, lay, shard, started[:4])
        full["b_in"] = full["b_in"] + started[4][0, 0]

    def late_weights(tag, after):
        names, lay, shard, handles = travelling[tag]
        land = _gather_late_wait(*handles, after, tag)
        land = lax.dynamic_update_slice(land, shard[None], (chip, 0, 0))
        return assemble(names, lay, land)

    grad_groups = {"early": [nm for nm in big_names if nm in EARLY_GRADS],
                   "late": [nm for nm in big_names if nm not in EARLY_GRADS]}
    sent = {}

    def send_grads(tag, g):
        names = grad_groups[tag]
        lay = _PackLayout([w[nm].shape[1:] for nm in names])
        hrows = lay.rows // 2
        gbf = lay.pack([_to_shards(g[nm], axes[nm] - 1).astype(BF16) for nm in names])
        gbf = gbf.reshape(N_CHIPS, 2, hrows, lay.width)
        started = _reduce_start(gbf, tag)
        own = lax.dynamic_index_in_dim(lax.dynamic_index_in_dim(started[2], chip, axis=0, keepdims=False), c, axis=0,
                                       keepdims=False)
        sent[tag] = (names, lay, started[:4], own)
        return started[4][0, 0]

    loss_local, grad_x, grads = _local_step(x[0], positions[0], loss_target[0], full, late_weights, send_grads)

    def f_add8(ti, pa):
        tot = ti[0].astype(F32)
        for part in ti[1:]:
            tot = tot + part.astype(F32)
        return [tot], []

    reduced = []
    for tag, (names, lay, handles, own) in sent.items():
        parts = _reduce_wait(*handles, grad_x, tag)
        reduced.append(_rowwise("grad_add_" + tag, f_add8, [own] + [parts[n] for n in range(N_PART)], [],
                                [(lay.width, F32)])[0])

    assert len({lay.width for _, lay, _, _ in sent.values()}) == 1
    red = jnp.concatenate(reduced, axis=0)
    sib = _sibling_swap(red)
    lower, upper = jnp.where(c == 0, red, sib), jnp.where(c == 0, sib, red)
    g_by_name, row = {}, 0
    for names, lay, _, _ in sent.values():
        hrows = lay.rows // 2
        both = jnp.concatenate([lower[row:row + hrows], upper[row:row + hrows]], axis=0)
        g_by_name.update(zip(names, lay.unpack(both), strict=True))
        row += hrows
    g_shards = [g_by_name[nm] for nm in big_names]

    def f_adamw(ti, pa):
        wv, gv, mv, vv = ti
        return list(_adamw(wv, gv, mv, vv)), []

    out = {"grad": {}, "delta": {}, "m": {}, "v": {}}
    for nm, g_sh in zip(big_names, g_shards, strict=True):
        wd = g_sh.shape[1]
        d_sh, m_sh, v_sh = _rowwise("adamw_" + nm, f_adamw, [w[nm][0], g_sh, mom[nm][0], var[nm][0]], [], [(wd, F32)] * 3)
        out["grad"][nm], out["delta"][nm], out["m"][nm], out["v"][nm] = g_sh[None], d_sh[None], m_sh[None], v_sh[None]

    loss_row = jnp.zeros((1, LANES), F32) + loss_local
    blank = jnp.zeros((1, LANES), F32)
    small = _small_allreduce_adamw([grads[nm] for nm in SMALL] + [loss_row], [w[nm] for nm in SMALL] + [blank],
                                   [mom[nm] for nm in SMALL] + [blank], [var[nm] for nm in SMALL] + [blank])
    for kind, arrs in zip(("grad", "delta", "m", "v"), small, strict=True):
        for nm, arr in zip(SMALL, arrs[:len(SMALL)], strict=True):
            out[kind][nm] = arr
    loss = small[0][len(SMALL)][0, 0]

    return (loss, grad_x[None], *[out["grad"][nm] for nm in ALL_W], *[out["delta"][nm] for nm in ALL_W],
            *[out["m"][nm] for nm in ALL_W], *[out["v"][nm] for nm in ALL_W])
```

```python
import functools
import math

import jax
import jax.numpy as jnp
from jax import lax
from jax.experimental import pallas as pl
from jax.experimental.pallas import tpu as pltpu

F32 = jnp.float32
BF16 = jnp.bfloat16

MLA_HEADS = 8
MLA_Q_LORA = 256
MLA_KV_LORA = 128
MLA_NOPE = 64
MLA_ROPE = 32
MLA_V = 64
FOX_HEADS = 8
FOX_DIM = 64
ROPE_THETA = 10000.0
NORM_EPS = 1e-6
HALF_ROPE = MLA_ROPE // 2

ADAM_LR = 0.001
ADAM_B1 = 0.9
ADAM_B2 = 0.999
ADAM_EPS = 1e-08
ADAM_WD = 0.01
ADAM_STEP = 10

LANES = 128
VMEM_LIMIT = 56 * 1024 * 1024
ATT_TILE = 1024
FWD_GROUP_LOG2 = 1
FWD_GROUP = 1 << FWD_GROUP_LOG2
MM_VMEM_BUDGET = 40 * 1024 * 1024
ROW_TILE_BYTES = 12 * 1024 * 1024
MXU_WIDTH = 256
MXU_MACS_PER_S = 4.98e14
HBM_BYTES_PER_S = 3.2e12
STEP_OVERHEAD_S = 0.35e-6
NEG = -1e30
LOG2E = math.log2(math.e)
MESH = pl.DeviceIdType.MESH

V_ONES = 64
FOX_Q_F = 64
FOX_Q_L = 67
FOX_Q_ONES = 70
MLA_Q_L = 96

BIG = (("w_in", 2), ("w_uq", 2), ("w_uk", 2), ("w_uv", 2), ("w_o_mla", 2), ("w_o_fox", 2),
       ("w_out", 1), ("w_ff1", 2), ("w_ff2", 1))
MIXER = ("w_uq", "w_uk", "w_uv", "w_o_mla", "w_o_fox")
LATE = ("w_out", "w_ff1", "w_ff2")
EARLY_GRADS = ("w_o_mla", "w_o_fox", "w_out", "w_ff1", "w_ff2")
SMALL = ("ln_pre_mix", "ln_post_mix", "ln_pre_mlp", "ln_post_mlp", "b_in", "q_a_norm", "kv_a_norm")
ALL_W = ("ln_pre_mix", "ln_post_mix", "ln_pre_mlp", "ln_post_mlp", "w_in", "b_in", "q_a_norm", "w_uq",
         "kv_a_norm", "w_uk", "w_uv", "w_o_mla", "w_o_fox", "w_out", "w_ff1", "w_ff2")
N_CHIPS = 4
PACK_W = 1024
LOCAL_PIECES = 8

_NT = (((1,), (1,)), ((), ()))
_TN = (((0,), (0,)), ((), ()))


def _cparams(sem=None):
    return pltpu.CompilerParams(dimension_semantics=sem, vmem_limit_bytes=VMEM_LIMIT)


def _divisor_tile(n, limit, mult):
    if n <= limit:
        return n
    best = None
    t = mult
    while t <= limit:
        if n % t == 0:
            best = t
        t += mult
    assert best is not None, (n, limit, mult)
    return best


def _round_up(v, mult):
    return -(-v // mult) * mult


def _mm_tiles(m, k, n, io_bytes):
    best = None
    for tm in (2048, 1024, 512, 256, 128):
        if m % tm:
            continue
        for tn in range(LANES, min(n, 2048) + 1, LANES):
            if n % tn:
                continue
            vmem = 2 * (tm * k * 2 + k * tn * 2 + tm * tn * io_bytes) + tm * tn * 4
            if vmem > MM_VMEM_BUDGET:
                continue
            mxu = m * k * n * (_round_up(tn, MXU_WIDTH) / tn) / MXU_MACS_PER_S
            hbm = (m * k * 2 + (m // tm) * k * n * 2 + m * n * io_bytes) / HBM_BYTES_PER_S
            cost = max(mxu, hbm) + (m // tm) * (n // tn) * STEP_OVERHEAD_S
            if best is None or cost < best[0]:
                best = (cost, tm, tn)
    assert best is not None, (m, k, n)
    return best[1], best[2]


def _mm(a, b, *, out_dtype, name, bias=None, transpose_b=False, extras=(), epilogue=None):
    m, k = a.shape
    n = b.shape[0] if transpose_b else b.shape[1]
    assert (b.shape[1] if transpose_b else b.shape[0]) == k and a.dtype == BF16 and b.dtype == BF16
    out_dtypes = list(out_dtype) if isinstance(out_dtype, (list, tuple)) else [out_dtype]
    n_ex = len(extras)
    tm, tn = _mm_tiles(m, k, n, sum(jnp.dtype(dt).itemsize for dt in out_dtypes) + 4 * n_ex)

    def body(*refs):
        a_ref, b_ref = refs[:2]
        pos = 2
        bias_ref = None
        if bias is not None:
            bias_ref = refs[pos]
            pos += 1
        ex_refs = refs[pos:pos + n_ex]
        o_refs = refs[pos + n_ex:]
        if transpose_b:
            acc = lax.dot_general(a_ref[...], b_ref[...], _NT, preferred_element_type=F32)
        else:
            acc = jnp.dot(a_ref[...], b_ref[...], preferred_element_type=F32)
        if bias_ref is not None:
            acc = acc + bias_ref[...]
        vals = [acc] if epilogue is None else epilogue(acc, [r[...] for r in ex_refs])
        for ref, val in zip(o_refs, vals, strict=True):
            ref[...] = val.astype(ref.dtype)

    b_spec = pl.BlockSpec((tn, k), lambda i, j: (j, 0)) if transpose_b else pl.BlockSpec((k, tn), lambda i, j: (0, j))
    in_specs = [pl.BlockSpec((tm, k), lambda i, j: (i, 0)), b_spec]
    args = [a, b]
    if bias is not None:
        in_specs.append(pl.BlockSpec((1, tn), lambda i, j: (0, j)))
        args.append(bias)
    in_specs += [pl.BlockSpec((tm, tn), lambda i, j: (i, j)) for _ in extras]
    args += list(extras)
    res = pl.pallas_call(
        body, grid=(m // tm, n // tn), in_specs=in_specs,
        out_specs=[pl.BlockSpec((tm, tn), lambda i, j: (i, j)) for _ in out_dtypes],
        out_shape=[jax.ShapeDtypeStruct((m, n), dt) for dt in out_dtypes],
        compiler_params=_cparams(("parallel", "parallel")), name=name)(*args)
    return res if isinstance(out_dtype, (list, tuple)) else res[0]


def _mm_tn(a, b, *, name):
    s, m = a.shape
    s2, n = b.shape
    assert s == s2 and a.dtype == BF16 and b.dtype == BF16
    tm = _divisor_tile(m, 1024, LANES)
    tn = _divisor_tile(n, 2304, LANES)
    tk = _divisor_tile(s, 2048, 16)
    nk = s // tk

    def body(a_ref, b_ref, o_ref, acc_ref):
        @pl.when(pl.program_id(2) == 0)
        def _():
            acc_ref[...] = jnp.zeros_like(acc_ref)

        acc_ref[...] += lax.dot_general(a_ref[...], b_ref[...], _TN, preferred_element_type=F32)

        @pl.when(pl.program_id(2) == nk - 1)
        def _():
            o_ref[...] = acc_ref[...].astype(o_ref.dtype)

    return pl.pallas_call(
        body, grid=(m // tm, n // tn, nk),
        in_specs=[pl.BlockSpec((tk, tm), lambda i, j, k: (k, i)), pl.BlockSpec((tk, tn), lambda i, j, k: (k, j))],
        out_specs=pl.BlockSpec((tm, tn), lambda i, j, k: (i, j)),
        out_shape=jax.ShapeDtypeStruct((m, n), BF16),
        scratch_shapes=[pltpu.VMEM((tm, tn), F32)],
        compiler_params=_cparams(("parallel", "parallel", "arbitrary")), name=name)(a, b)


def _rowwise(name, fn, tiled, params, outs, reds=(), reverse=False):
    wins = [t if isinstance(t, tuple) else (t, 0, t.shape[1]) for t in tiled]
    s = wins[0][0].shape[0]
    row_bytes = sum(w * arr.dtype.itemsize for arr, _, w in wins) + sum(w * jnp.dtype(d).itemsize for w, d in outs)
    ts = _divisor_tile(s, max(16, min(1024, ROW_TILE_BYTES // row_bytes)), 16)
    nt, npar, nout = len(wins), len(params), len(outs)
    n_tiles = s // ts

    def row(i):
        return n_tiles - 1 - i if reverse else i

    def body(*refs):
        tin = [r[...] for r in refs[:nt]]
        par = [r[...] for r in refs[nt:nt + npar]]
        out_refs = refs[nt + npar:nt + npar + nout]
        red_refs = refs[nt + npar + nout:]
        o, r = fn(tin, par)
        for ref, val in zip(out_refs, o, strict=True):
            ref[...] = val.astype(ref.dtype)
        if red_refs:
            @pl.when(pl.program_id(0) == 0)
            def _():
                for ref in red_refs:
                    ref[...] = jnp.zeros_like(ref)

            for ref, val in zip(red_refs, r, strict=True):
                ref[...] += val

    in_specs = [pl.BlockSpec((ts, w), functools.partial(lambda i, cb: (row(i), cb), cb=cb)) for _, cb, w in wins]
    in_specs += [pl.BlockSpec(p.shape, lambda i: (0, 0)) for p in params]
    out_specs = [pl.BlockSpec((ts, w), lambda i: (row(i), 0)) for w, _ in outs]
    out_specs += [pl.BlockSpec((1, w), lambda i: (0, 0)) for w in reds]
    out_shape = [jax.ShapeDtypeStruct((s, w), d) for w, d in outs]
    out_shape += [jax.ShapeDtypeStruct((1, w), F32) for w in reds]
    return pl.pallas_call(
        body, grid=(n_tiles,), in_specs=in_specs, out_specs=out_specs, out_shape=out_shape,
        compiler_params=_cparams(("arbitrary",)), name=name)(*[w[0] for w in wins], *params)


def _rms(x, g):
    r = lax.rsqrt(jnp.mean(x * x, axis=-1, keepdims=True) + NORM_EPS)
    return x * r * g, r


def _rms_bwd(x, g, dy):
    r = lax.rsqrt(jnp.mean(x * x, axis=-1, keepdims=True) + NORM_EPS)
    gy = dy * g
    dx = r * gy - x * (r * r * r) * jnp.mean(x * gy, axis=-1, keepdims=True)
    dg = jnp.sum(dy * (x * r), axis=0, keepdims=True)
    return dx, dg


def _sigmoid(x):
    return 1.0 / (1.0 + jnp.exp(-x))


def _split3(x):
    hi = x.astype(BF16).astype(F32)
    r = x - hi
    mid = r.astype(BF16).astype(F32)
    lo = (r - mid).astype(BF16).astype(F32)
    return hi, mid, lo


def _lane(shape):
    return lax.broadcasted_iota(jnp.int32, shape, 1)


def _put3(blk, lane, pos, pieces):
    for k, piece in enumerate(pieces):
        blk = jnp.where(lane == pos + k, piece, blk)
    return blk


def _lane_column(blk, lane, pos):
    return jnp.sum(jnp.where(lane == pos, blk, 0.0), axis=1, keepdims=True)


def _blocks(a, nh):
    return [a[:, h * LANES:(h + 1) * LANES] for h in range(nh)]


def _head_spread(nh, d):
    shift = d.bit_length() - 1
    assert 1 << shift == d
    r = lax.broadcasted_iota(jnp.int32, (nh * d, nh * LANES), 0)
    c = lax.broadcasted_iota(jnp.int32, (nh * d, nh * LANES), 1)
    return jnp.where(c == (r >> shift) * LANES + (r & (d - 1)), 1.0, 0.0).astype(BF16)


def _rope_block(x, c, sa, sb):
    return x * c + pltpu.roll(x, LANES - HALF_ROPE, 1) * sa + pltpu.roll(x, HALF_ROPE, 1) * sb


def _forget_cumsum(z, cb):
    s = z.shape[0]
    ts = _divisor_tile(s, 512, LANES)

    def body(x_ref, col_ref, carry):
        @pl.when(pl.program_id(0) == 0)
        def _():
            carry[...] = jnp.zeros_like(carry)

        x = x_ref[...]
        lf = jnp.minimum(x, 0.0) - jnp.log1p(jnp.exp(-jnp.abs(x)))
        r = lax.broadcasted_iota(jnp.int32, (ts, ts), 0)
        c = lax.broadcasted_iota(jnp.int32, (ts, ts), 1)
        tri = jnp.where(c <= r, 1.0, 0.0).astype(F32)
        col_ref[...] = jnp.dot(tri, lf, preferred_element_type=F32, precision=lax.Precision.HIGHEST) + carry[...]
        carry[...] += jnp.sum(lf, axis=0, keepdims=True)

    return pl.pallas_call(
        body, grid=(s // ts,),
        in_specs=[pl.BlockSpec((ts, LANES), lambda i: (i, cb))],
        out_specs=pl.BlockSpec((ts, LANES), lambda i: (i, 0)),
        out_shape=jax.ShapeDtypeStruct((s, LANES), F32),
        scratch_shapes=[pltpu.VMEM((1, LANES), F32)],
        compiler_params=_cparams(("arbitrary",)), name="forget_cumsum")(z)


def _forget_cumsum_bwd(dq, dk, z, cb, nh):
    s = z.shape[0]
    ts = _divisor_tile(s, 512, LANES)
    nt = s // ts
    wd = nh * LANES

    def body(dq_ref, dk_ref, x_ref, o_ref, carry):
        @pl.when(pl.program_id(0) == 0)
        def _():
            carry[...] = jnp.zeros_like(carry)

        lane = _lane((ts, LANES))
        df = jnp.zeros((ts, LANES), F32)
        for h in range(nh):
            cols = slice(h * LANES, (h + 1) * LANES)
            d_h = _lane_column(dq_ref[:, cols], lane, FOX_Q_F) - _lane_column(dk_ref[:, cols], lane, FOX_Q_ONES)
            df = jnp.where(lane == h, d_h, df)
        r = lax.broadcasted_iota(jnp.int32, (ts, ts), 0)
        c = lax.broadcasted_iota(jnp.int32, (ts, ts), 1)
        tri = jnp.where(c >= r, 1.0, 0.0).astype(F32)
        rc = jnp.dot(tri, df, preferred_element_type=F32, precision=lax.Precision.HIGHEST) + carry[...]
        carry[...] += jnp.sum(df, axis=0, keepdims=True)
        o_ref[...] = (rc * (1.0 / (1.0 + jnp.exp(x_ref[...])))).astype(o_ref.dtype)

    return pl.pallas_call(
        body, grid=(nt,),
        in_specs=[pl.BlockSpec((ts, wd), lambda i: (nt - 1 - i, 0)),
                  pl.BlockSpec((ts, wd), lambda i: (nt - 1 - i, 0)),
                  pl.BlockSpec((ts, LANES), lambda i: (nt - 1 - i, cb))],
        out_specs=pl.BlockSpec((ts, LANES), lambda i: (nt - 1 - i, 0)),
        out_shape=jax.ShapeDtypeStruct((s, LANES), BF16),
        scratch_shapes=[pltpu.VMEM((1, LANES), F32)],
        compiler_params=_cparams(("arbitrary",)), name="forget_cumsum_bwd")(dq, dk, z)


def _flash_fwd(q, k, v, scale, nh, l_lane, *, name):
    s = q.shape[0]
    t = min(ATT_TILE, s)
    half = t // 2 if t % (2 * LANES) == 0 else t
    nq = s // t
    c = scale * LOG2E

    def body(q_ref, k_ref, v_ref, o_ref, qb_ref):
        i = pl.program_id(1)
        qb = q_ref[...]

        def scores(q_rows, k0, nk):
            kb = k_ref[pl.ds(pl.multiple_of(k0, half), nk), :]
            return lax.dot_general(q_rows, kb, _NT, preferred_element_type=F32)

        def update(sc, k0, nk, carry):
            m, acc = carry
            m_new = jnp.maximum(m, jnp.max(sc, axis=1, keepdims=True))
            p = jnp.exp2((sc - m_new) * c)
            alpha = jnp.exp2((m - m_new) * c)
            vb = v_ref[pl.ds(pl.multiple_of(k0, half), nk), :]
            acc = alpha * acc + jnp.dot(p.astype(BF16), vb, preferred_element_type=F32)
            return m_new, acc

        def full_block(j, cr):
            return update(scores(qb, j * t, t), j * t, t, cr)

        def group(jj, cr):
            for n in range(FWD_GROUP):
                cr = full_block(FWD_GROUP * jj + n, cr)
            return cr

        def causal(sc):
            row = lax.broadcasted_iota(jnp.int32, sc.shape, 0)
            col = lax.broadcasted_iota(jnp.int32, sc.shape, 1)
            return jnp.where(col <= row, sc, NEG)

        init = (jnp.full((t, 1), NEG, F32), jnp.zeros((t, LANES), F32))
        n_groups = i >> FWD_GROUP_LOG2
        carry = lax.fori_loop(0, n_groups, group, init)
        carry = lax.fori_loop(n_groups * FWD_GROUP, i, full_block, carry)
        m, acc = update(causal(scores(qb, i * t, t)), i * t, t, carry)
        lane = _lane((t, LANES))
        l = _lane_column(acc, lane, V_ONES)
        o_ref[...] = (acc / l).astype(o_ref.dtype)
        big_l = m + jnp.log(l) / scale
        qb_ref[...] = _put3(qb.astype(F32), lane, l_lane, _split3(-big_l)).astype(qb_ref.dtype)

    head_rows = pl.BlockSpec((t, LANES), lambda h, i: (i, h))
    head_all = pl.BlockSpec((s, LANES), lambda h, i: (0, h))
    return pl.pallas_call(
        body, grid=(nh, nq), in_specs=[head_rows, head_all, head_all], out_specs=[head_rows, head_rows],
        out_shape=[jax.ShapeDtypeStruct(q.shape, BF16), jax.ShapeDtypeStruct(q.shape, BF16)],
        compiler_params=_cparams(("parallel", "arbitrary")), name=name)(q, k, v)


def _flash_bwd(qb, k, v, do, scale, nh, n_feat, *, name):
    s = qb.shape[0]
    t = min(ATT_TILE, s)
    half = t // 2 if t % (2 * LANES) == 0 else t
    nq = s // t
    c = scale * LOG2E

    def body(q_ref, k_ref, v_ref, do_ref, dk_ref, dv_ref, dq_ref):
        j = pl.program_id(1)
        kb = k_ref[...]
        vb = v_ref[...]

        @pl.when(j == 0)
        def _():
            dq_ref[...] = jnp.zeros_like(dq_ref)

        def part(q0, n_q, n_k, carry, q_off):
            dk_acc, dv_acc = carry
            rows = pl.ds(pl.multiple_of(q0, half), n_q)
            qblk = q_ref[rows, :]
            dob = do_ref[rows, :]
            kbb, vbb = kb[:n_k], vb[:n_k]
            st = lax.dot_general(kbb, qblk, _NT, preferred_element_type=F32)
            if q_off is not None:
                key = lax.broadcasted_iota(jnp.int32, st.shape, 0)
                qry = lax.broadcasted_iota(jnp.int32, st.shape, 1) + q_off
                st = jnp.where(key <= qry, st, NEG)
            pt = jnp.exp2(st * c)
            dv_new = jnp.dot(pt.astype(BF16), dob, preferred_element_type=F32)
            dpt = lax.dot_general(vbb, dob, _NT, preferred_element_type=F32)
            dsb = (pt * dpt).astype(BF16)
            dk_new = jnp.dot(dsb, qblk, preferred_element_type=F32)
            dq_ref[rows, :] += lax.dot_general(dsb, kbb, _TN, preferred_element_type=F32)
            if n_k == t:
                return dk_acc + dk_new, dv_acc + dv_new
            return (jnp.concatenate([dk_acc[:n_k] + dk_new, dk_acc[n_k:]], axis=0),
                    jnp.concatenate([dv_acc[:n_k] + dv_new, dv_acc[n_k:]], axis=0))

        def block(i, carry):
            return part(i * t, t, t, carry, None)

        init = (jnp.zeros((t, LANES), F32), jnp.zeros((t, LANES), F32))
        carry = part(j * t + half, t - half, t, init, half) if half < t else part(j * t, t, t, init, 0)
        if half < t:
            carry = part(j * t, half, half, carry, 0)
        rest = nq - 1 - j
        carry = lax.cond((rest & 1) == 1, lambda cr: block(j + 1, cr), lambda cr: cr, carry)
        first = j + 1 + (rest & 1)

        def pair(ii, cr):
            i0 = first + 2 * ii
            return block(i0 + 1, block(i0, cr))

        dk_acc, dv_acc = lax.fori_loop(0, rest >> 1, pair, carry)
        dk_ref[...] = dk_acc * jnp.where(_lane((t, LANES)) < n_feat, scale, 1.0)
        dv_ref[...] = dv_acc.astype(dv_ref.dtype)

        @pl.when(j == nq - 1)
        def _():
            dq_ref[...] = dq_ref[...] * jnp.where(_lane((s, LANES)) < n_feat, scale, 1.0)

    head_rows = pl.BlockSpec((t, LANES), lambda h, j: (j, h))
    head_all = pl.BlockSpec((s, LANES), lambda h, j: (0, h))
    shp = jax.ShapeDtypeStruct(qb.shape, F32)
    return pl.pallas_call(
        body, grid=(nh, nq), in_specs=[head_all, head_rows, head_rows, head_all],
        out_specs=[head_rows, head_rows, head_all],
        out_shape=[shp, jax.ShapeDtypeStruct(qb.shape, BF16), shp],
        compiler_params=_cparams(("parallel", "arbitrary")), name=name)(qb, k, v, do)


def _with_delta(do, o, nh, *, name):
    def fn(ti, pa):
        lane = _lane((ti[0].shape[0], LANES))
        out = []
        for d_blk, o_blk in zip(_blocks(ti[0], nh), _blocks(ti[1], nh), strict=True):
            delta = jnp.sum(d_blk * o_blk.astype(F32), axis=1, keepdims=True)
            out.append(_put3(d_blk, lane, V_ONES, _split3(-delta)))
        return [jnp.concatenate(out, axis=1)], []

    (res,) = _rowwise(name, fn, [do, o], [], [(do.shape[1], BF16)])
    return res


def _pad_heads(a, nh):
    d = a.shape[-1] // nh
    a = a.reshape(a.shape[:-1] + (nh, d))
    a = jnp.pad(a, [(0, 0)] * (a.ndim - 1) + [(0, LANES - d)])
    return a.reshape(a.shape[:-2] + (nh * LANES,))


def _unpad_heads(a, nh, d):
    a = a.reshape(a.shape[:-1] + (nh, LANES))[..., :d]
    return a.reshape(a.shape[:-2] + (nh * d,))


def _pad_head_rows(w, nh):
    return _pad_heads(w.T, nh).T


def _unpad_head_rows(g, nh, d):
    return _unpad_heads(g.T, nh, d).T


class _ZLayout:
    def __init__(self, d):
        fw = FOX_HEADS * FOX_DIM
        self.src = {}
        off = 0
        for nm, w in (("cq", MLA_Q_LORA), ("ckv", MLA_KV_LORA), ("kr", MLA_ROPE), ("fq", fw), ("fk", fw),
                      ("fv", fw), ("fl", FOX_HEADS), ("ga", d), ("gb", d)):
            self.src[nm] = (off, w)
            off += w
        self.dst = {}
        off = 0
        for nm, w in (("ga", d), ("gb", d), ("fq", fw), ("fk", fw), ("fv", fw), ("cq", MLA_Q_LORA),
                      ("ckv", MLA_KV_LORA), ("kr", LANES), ("fl", LANES)):
            assert off % w == 0
            self.dst[nm] = (off, w)
            off += w
        self.width = off
        self.split = self.dst["cq"][0]
        assert all((o - self.split) % w == 0 for o, w in self.dst.values() if o >= self.split)

    def to_kernel(self, w):
        def seg(nm):
            off, wd = self.src[nm]
            return w[..., off:off + wd]

        def pad(a, left, total):
            return jnp.pad(a, [(0, 0)] * (a.ndim - 1) + [(left, total - left - a.shape[-1])])

        return jnp.concatenate([seg("ga"), seg("gb"), seg("fq"), seg("fk"), seg("fv"), seg("cq"), seg("ckv"),
                                pad(seg("kr"), MLA_NOPE, LANES), pad(seg("fl"), 0, LANES)], axis=-1)

    def from_kernel(self, g):
        def seg(nm, lo=0, hi=None):
            off, wd = self.dst[nm]
            return g[..., off + lo:off + (wd if hi is None else hi)]

        return jnp.concatenate([seg("cq"), seg("ckv"), seg("kr", MLA_NOPE, MLA_NOPE + MLA_ROPE), seg("fq"), seg("fk"),
                                seg("fv"), seg("fl", 0, FOX_HEADS), seg("ga"), seg("gb")], axis=-1)


def _local_step(x, positions, target, wts, late_weights, send_grads):
    s, d = x.shape
    zl = _ZLayout(d)
    hw = MLA_HEADS * LANES
    assert MLA_HEADS == FOX_HEADS
    scale_mla = (MLA_NOPE + MLA_ROPE) ** -0.5
    scale_fox = FOX_DIM ** -0.5

    inv_freq = ROPE_THETA ** (-jnp.arange(HALF_ROPE, dtype=F32) / HALF_ROPE)
    ang = positions.astype(F32)[:, None] * inv_freq
    cos, sin = jnp.cos(ang), jnp.sin(ang)
    tail = jnp.zeros((s, LANES - MLA_NOPE - MLA_ROPE), F32)
    rc = jnp.concatenate([jnp.ones((s, MLA_NOPE), F32), cos, cos, tail], axis=1)
    ra = jnp.concatenate([jnp.zeros((s, MLA_NOPE), F32), -sin, jnp.zeros((s, HALF_ROPE), F32), tail], axis=1)
    rb = jnp.concatenate([jnp.zeros((s, MLA_NOPE + HALF_ROPE), F32), sin, tail], axis=1)

    w_in = zl.to_kernel(wts["w_in"])
    b_in = zl.to_kernel(wts["b_in"])

    def f_norm_in(ti, pa):
        y, _ = _rms(ti[0], pa[0])
        return [y], []

    (h,) = _rowwise("norm_in", f_norm_in, [x], [wts["ln_pre_mix"]], [(d, BF16)])
    z_lo = _mm(h, w_in[:, :zl.split], bias=b_in[:, :zl.split], out_dtype=BF16, name="proj_in_lo")
    z = _mm(h, w_in[:, zl.split:], bias=b_in[:, zl.split:], out_dtype=F32, name="proj_in_hi")

    def zwin(nm):
        off, wd = zl.dst[nm]
        return (z_lo, off // wd, wd) if off < zl.split else (z, (off - zl.split) // wd, wd)

    wts = {**wts, **late_weights("mixer", z)}
    w_uq = _pad_heads(wts["w_uq"], MLA_HEADS)
    w_ukv = jnp.concatenate([_pad_heads(wts["w_uk"], MLA_HEADS), _pad_heads(wts["w_uv"], MLA_HEADS)], axis=1)
    w_o_mla = _pad_head_rows(wts["w_o_mla"], MLA_HEADS)
    w_o_fox = _pad_head_rows(wts["w_o_fox"], FOX_HEADS)

    def f_mla_norms(ti, pa):
        cqn, _ = _rms(ti[0], pa[0])
        ckvn, _ = _rms(ti[1], pa[1])
        return [cqn, ckvn], []

    cqn, ckvn = _rowwise("mla_norms", f_mla_norms, [zwin("cq"), zwin("ckv")],
                         [wts["q_a_norm"], wts["kv_a_norm"]], [(MLA_Q_LORA, BF16), (MLA_KV_LORA, BF16)])
    qf = _mm(cqn, w_uq, out_dtype=F32, name="proj_uq")
    kv = _mm(ckvn, w_ukv, out_dtype=BF16, name="proj_ukv")

    def f_rope_q(ti, pa):
        xq, c_, a_, b_ = ti
        return [jnp.concatenate([_rope_block(blk, c_, a_, b_) for blk in _blocks(xq, MLA_HEADS)], axis=1)], []

    (q_mla,) = _rowwise("rope_q", f_rope_q, [qf, rc, ra, rb], [], [(hw, BF16)])

    def f_mla_kv(ti, pa):
        kn, vn, kr, c_, a_, b_ = ti
        lane = _lane(kr.shape)
        k_tail = jnp.where((lane >= MLA_Q_L) & (lane < MLA_Q_L + 3), 1.0, _rope_block(kr, c_, a_, b_))
        ones_v = (lane >= V_ONES) & (lane < V_ONES + 3)
        k_out = [jnp.where(lane < MLA_NOPE, blk.astype(F32), k_tail) for blk in _blocks(kn, MLA_HEADS)]
        v_out = [jnp.where(ones_v, 1.0, blk.astype(F32)) for blk in _blocks(vn, MLA_HEADS)]
        return [jnp.concatenate(k_out, axis=1), jnp.concatenate(v_out, axis=1)], []

    k_mla, v_mla = _rowwise("mla_kv", f_mla_kv, [(kv, 0, hw), (kv, 1, hw), zwin("kr"), rc, ra, rb], [],
                            [(hw, BF16), (hw, BF16)])
    o_mla, qb_mla = _flash_fwd(q_mla, k_mla, v_mla, scale_mla, MLA_HEADS, MLA_Q_L, name="mla_fwd")

    fl_cb = zwin("fl")[1]
    fcol = _forget_cumsum(z, fl_cb)

    def f_fox_qkv(ti, pa):
        spread = _head_spread(FOX_HEADS, FOX_DIM)
        fq, fk, fv = (jnp.dot(a, spread, preferred_element_type=F32) for a in ti[:3])
        fc = ti[3]
        lane = _lane(fc.shape)
        ones_q = (lane >= FOX_Q_ONES) & (lane < FOX_Q_ONES + 3)
        ones_k = (lane >= FOX_Q_F) & (lane < FOX_Q_ONES)
        ones_v = (lane >= V_ONES) & (lane < V_ONES + 3)
        q_out, k_out, v_out = [], [], []
        for hh, (qblk, kblk, vblk) in enumerate(zip(_blocks(fq, FOX_HEADS), _blocks(fk, FOX_HEADS),
                                                    _blocks(fv, FOX_HEADS), strict=True)):
            f_h = _lane_column(fc, lane, hh) * (1.0 / scale_fox)
            q_out.append(_put3(jnp.where(ones_q, 1.0, qblk), lane, FOX_Q_F, _split3(f_h)))
            k_out.append(_put3(jnp.where(ones_k, 1.0, kblk), lane, FOX_Q_ONES, _split3(-f_h)))
            v_out.append(jnp.where(ones_v, 1.0, vblk))
        return [jnp.concatenate(q_out, axis=1), jnp.concatenate(k_out, axis=1), jnp.concatenate(v_out, axis=1)], []

    q_fox, k_fox, v_fox = _rowwise("fox_qkv", f_fox_qkv, [zwin("fq"), zwin("fk"), zwin("fv"), fcol],
                                   [], [(hw, BF16)] * 3)
    o_fox, qb_fox = _flash_fwd(q_fox, k_fox, v_fox, scale_fox, FOX_HEADS, FOX_Q_L, name="fox_fwd")

    y_mla = _mm(o_mla, w_o_mla, out_dtype=BF16, name="proj_o_mla")
    y_fox = _mm(o_fox, w_o_fox, out_dtype=BF16, name="proj_o_fox")

    def f_gate(ti, pa):
        ga, gb, ya, yb = (a.astype(F32) for a in ti)
        return [_sigmoid(ga) * ya + _sigmoid(gb) * yb], []

    (merged,) = _rowwise("gate", f_gate, [zwin("ga"), zwin("gb"), y_mla, y_fox], [], [(d, BF16)])
    wts = {**wts, **late_weights("mlp", merged)}
    mix = _mm(merged, wts["w_out"], out_dtype=F32, name="proj_out")

    def f_resid1(ti, pa):
        xa, mx = ti
        y, _ = _rms(mx, pa[0])
        x1 = xa + y
        h2, _ = _rms(x1, pa[1])
        return [x1, h2], []

    x1, h2 = _rowwise("resid_mix", f_resid1, [x, mix], [wts["ln_post_mix"], wts["ln_pre_mlp"]], [(d, F32), (d, BF16)])

    def relu2(acc, ex):
        r = jnp.maximum(acc, 0.0)
        return [acc, r * r]

    u, act = _mm(h2, wts["w_ff1"], out_dtype=[BF16, BF16], name="ff1", epilogue=relu2)
    mo = _mm(act, wts["w_ff2"], out_dtype=F32, name="ff2")

    def f_loss(ti, pa):
        xa, mv, tg = ti
        y, _ = _rms(mv, pa[0])
        err = (xa + y) - tg
        g2 = err / d
        dmo, dg = _rms_bwd(mv, pa[0], g2)
        return [g2, dmo], [jnp.sum(err * err, axis=0, keepdims=True), dg]

    g2, d_mo, loss_cols, g_ln_post_mlp = _rowwise("loss", f_loss, [x1, mo, target], [wts["ln_post_mlp"]],
                                                  [(d, F32), (d, BF16)], [d, d])
    loss = 0.5 * jnp.sum(loss_cols) / d

    grads = {"ln_post_mlp": g_ln_post_mlp}
    grads["w_ff2"] = _mm_tn(act, d_mo, name="grad_ff2")

    def relu2_bwd(acc, ex):
        return [acc * (2.0 * jnp.maximum(ex[0], 0.0))]

    (d_u,) = _mm(d_mo, wts["w_ff2"], out_dtype=[BF16], name="ff2_bwd", transpose_b=True, extras=[u], epilogue=relu2_bwd)
    grads["w_ff1"] = _mm_tn(h2, d_u, name="grad_ff1")
    d_h2 = _mm(d_u, wts["w_ff1"], out_dtype=F32, name="ff1_bwd", transpose_b=True)

    def f_resid1_bwd(ti, pa):
        gres, dh2, x1v, mx = ti
        dx1n, dg_pre_mlp = _rms_bwd(x1v, pa[1], dh2)
        dx1 = gres + dx1n
        dmix, dg_post_mix = _rms_bwd(mx, pa[0], dx1)
        return [dx1, dmix], [dg_post_mix, dg_pre_mlp]

    d_x1, d_mix, grads["ln_post_mix"], grads["ln_pre_mlp"] = _rowwise(
        "resid_mix_bwd", f_resid1_bwd, [g2, d_h2, x1, mix], [wts["ln_post_mix"], wts["ln_pre_mlp"]],
        [(d, F32), (d, BF16)], [d, d])
    grads["w_out"] = _mm_tn(merged, d_mix, name="grad_out")
    d_merged = _mm(d_mix, wts["w_out"], out_dtype=BF16, name="proj_out_bwd", transpose_b=True)

    def f_gate_bwd(ti, pa):
        dm, ga, gb, ya, yb = (a.astype(F32) for a in ti)
        sa, sb = _sigmoid(ga), _sigmoid(gb)
        return [dm * sa, dm * sb, dm * ya * (sa * (1.0 - sa)), dm * yb * (sb * (1.0 - sb))], []

    d_ya, d_yb, d_ga, d_gb = _rowwise("gate_bwd", f_gate_bwd,
                                      [d_merged, zwin("ga"), zwin("gb"), y_mla, y_fox], [],
                                      [(d, BF16)] * 4)
    grads["w_o_mla"] = _unpad_head_rows(_mm_tn(o_mla, d_ya, name="grad_o_mla"), MLA_HEADS, MLA_V)
    grads["w_o_fox"] = _unpad_head_rows(_mm_tn(o_fox, d_yb, name="grad_o_fox"), FOX_HEADS, FOX_DIM)
    w_o_mla = w_o_mla + send_grads("early", {nm: grads[nm] for nm in EARLY_GRADS}).astype(BF16)
    do_mla = _with_delta(_mm(d_ya, w_o_mla, out_dtype=F32, name="proj_o_mla_bwd", transpose_b=True), o_mla,
                         MLA_HEADS, name="mla_delta")
    do_fox = _with_delta(_mm(d_yb, w_o_fox, out_dtype=F32, name="proj_o_fox_bwd", transpose_b=True), o_fox,
                         FOX_HEADS, name="fox_delta")

    dk_mla, dv_mla, dq_mla = _flash_bwd(qb_mla, k_mla, v_mla, do_mla, scale_mla, MLA_HEADS, MLA_NOPE + MLA_ROPE,
                                        name="mla_bwd")
    dk_fox, dv_fox, dq_fox = _flash_bwd(qb_fox, k_fox, v_fox, do_fox, scale_fox, FOX_HEADS, FOX_DIM, name="fox_bwd")
    d_fl = _forget_cumsum_bwd(dq_fox, dk_fox, z, fl_cb, FOX_HEADS)

    def f_rope_q_bwd(ti, pa):
        g, c_, a_, b_ = ti
        return [jnp.concatenate([_rope_block(blk, c_, -a_, -b_) for blk in _blocks(g, MLA_HEADS)], axis=1)], []

    (d_qf,) = _rowwise("rope_q_bwd", f_rope_q_bwd, [dq_mla, rc, ra, rb], [], [(hw, BF16)])
    grads["w_uq"] = _unpad_heads(_mm_tn(cqn, d_qf, name="grad_uq"), MLA_HEADS, MLA_NOPE + MLA_ROPE)
    d_cqn = _mm(d_qf, w_uq, out_dtype=F32, name="proj_uq_bwd", transpose_b=True)

    def f_mla_kv_bwd(ti, pa):
        gk, gv, c_, a_, b_ = ti
        k_blocks = _blocks(gk, MLA_HEADS)
        tot = k_blocks[0]
        for blk in k_blocks[1:]:
            tot = tot + blk
        return [jnp.concatenate([gk, gv], axis=1), _rope_block(tot, c_, -a_, -b_)], []

    d_kv, d_kr = _rowwise("mla_kv_bwd", f_mla_kv_bwd, [dk_mla, dv_mla, rc, ra, rb], [], [(2 * hw, BF16), (LANES, BF16)])
    g_ukv = _mm_tn(ckvn, d_kv, name="grad_ukv")
    grads["w_uk"] = _unpad_heads(g_ukv[:, :hw], MLA_HEADS, MLA_NOPE)
    grads["w_uv"] = _unpad_heads(g_ukv[:, hw:], MLA_HEADS, MLA_V)
    d_ckvn = _mm(d_kv, w_ukv, out_dtype=F32, name="proj_ukv_bwd", transpose_b=True)

    def f_mla_norms_bwd(ti, pa):
        cq, ckv, dcqn, dckvn = ti
        dcq, dg_q = _rms_bwd(cq, pa[0], dcqn)
        dckv, dg_kv = _rms_bwd(ckv, pa[1], dckvn)
        return [dcq, dckv], [dg_q, dg_kv]

    d_cq, d_ckv, grads["q_a_norm"], grads["kv_a_norm"] = _rowwise(
        "mla_norms_bwd", f_mla_norms_bwd, [zwin("cq"), zwin("ckv"), d_cqn, d_ckvn],
        [wts["q_a_norm"], wts["kv_a_norm"]], [(MLA_Q_LORA, BF16), (MLA_KV_LORA, BF16)], [MLA_Q_LORA, MLA_KV_LORA])

    def f_fox_compact(ti, pa):
        gather = _head_spread(FOX_HEADS, FOX_DIM)
        return [lax.dot_general(a.astype(BF16), gather, _NT, preferred_element_type=F32) for a in ti], []

    d_fq, d_fk, d_fv = _rowwise("fox_compact", f_fox_compact, [dq_fox, dk_fox, dv_fox], [],
                                [(FOX_HEADS * FOX_DIM, BF16)] * 3)
    d_z = jnp.concatenate([d_ga, d_gb, d_fq, d_fk, d_fv, d_cq, d_ckv, d_kr, d_fl], axis=1)
    assert d_z.shape[1] == zl.width

    def f_bias(ti, pa):
        return [], [jnp.sum(ti[0].astype(F32), axis=0, keepdims=True)]

    (g_b_in,) = _rowwise("grad_b_in", f_bias, [d_z], [], [], [zl.width])
    grads["b_in"] = zl.from_kernel(g_b_in)
    grads["w_in"] = zl.from_kernel(_mm_tn(h, d_z, name="grad_in"))
    tok = send_grads("late", {nm: grads[nm] for nm, _ in BIG if nm not in EARLY_GRADS})
    d_h = _mm(d_z, w_in, bias=jnp.zeros((1, d), F32) + tok, out_dtype=F32, name="proj_in_bwd", transpose_b=True)

    def f_norm_in_bwd(ti, pa):
        dx1v, dh, xa = ti
        dxn, dg = _rms_bwd(xa, pa[0], dh)
        return [dx1v + dxn], [dg]

    grad_x, grads["ln_pre_mix"] = _rowwise("norm_in_bwd", f_norm_in_bwd, [d_x1, d_h, x], [wts["ln_pre_mix"]],
                                           [(d, F32)], [d])
    return loss, grad_x, grads


class _PackLayout:
    def __init__(self, shapes):
        self.shapes = list(shapes)
        self.width = _round_up(max(b for _, b in shapes), LANES)
        self.bands = []
        row = 0
        shelf = []
        for idx, (a, b) in enumerate(shapes):
            if 2 * _round_up(b, LANES) > self.width:
                self.bands.append((row, _round_up(a, 32), [(idx, 0)]))
                row += _round_up(a, 32)
            else:
                shelf.append(idx)
        col, items = 0, []
        for idx in shelf:
            wb = _round_up(shapes[idx][1], LANES)
            if col + wb > self.width:
                hgt = max(_round_up(shapes[i][0], 32) for i, _ in items)
                self.bands.append((row, hgt, items))
                row += hgt
                col, items = 0, []
            items.append((idx, col))
            col += wb
        if items:
            hgt = max(_round_up(shapes[i][0], 32) for i, _ in items)
            self.bands.append((row, hgt, items))
            row += hgt
        self.rows = _round_up(row, 16 * LOCAL_PIECES)

    def pack(self, arrs):
        lead = arrs[0].shape[:-2]
        no_pad = [(0, 0)] * len(lead)
        bands = []
        for _, hgt, items in self.bands:
            parts = []
            for k, (idx, col) in enumerate(items):
                a, b = self.shapes[idx]
                nxt = items[k + 1][1] if k + 1 < len(items) else self.width
                parts.append(jnp.pad(arrs[idx], no_pad + [(0, hgt - a), (0, nxt - col - b)]))
            bands.append(parts[0] if len(parts) == 1 else jnp.concatenate(parts, axis=-1))
        used = sum(hgt for _, hgt, _ in self.bands)
        if used < self.rows:
            bands.append(jnp.zeros(lead + (self.rows - used, self.width), arrs[0].dtype))
        return jnp.concatenate(bands, axis=-2)

    def unpack(self, packed):
        out = [None] * len(self.shapes)
        for row, _, items in self.bands:
            for idx, col in items:
                a, b = self.shapes[idx]
                out[idx] = packed[..., row:row + a, col:col + b]
        return out


def _to_shards(g, axis):
    if axis == 0:
        return g.reshape(N_CHIPS, g.shape[0] // N_CHIPS, g.shape[1])
    return jnp.stack(jnp.split(g, N_CHIPS, axis=1))


def _from_shards(s4, axis):
    n, a, b = s4.shape
    if axis == 0:
        return s4.reshape(n * a, b)
    return jnp.concatenate([s4[ch] for ch in range(n)], axis=1)


ANY = pl.BlockSpec(memory_space=pl.ANY)


def _place():
    return lax.axis_index("x"), lax.axis_index("y"), lax.axis_index("c")


def _gather_weights(wpk):
    rows, wd = wpk.shape
    half = rows // 2

    def body(w_ref, out_ref, send_sems, recv_sems, local_sems):
        x, y, c = _place()
        sibling = (x, y, 1 - c)
        chips = [(1 - x, y), (x, 1 - y), (1 - x, 1 - y)]

        def slab(chip, hf):
            return out_ref.at[2 * chip[0] + chip[1], pl.ds(hf * half, half), :]

        def copy(k, chip, hf, to, src=None):
            return pltpu.make_async_remote_copy(
                src_ref=slab(chip, hf) if src is None else src, dst_ref=slab(chip, hf),
                send_sem=send_sems.at[k], recv_sem=recv_sems.at[k], device_id=to, device_id_type=MESH)

        piece = rows // LOCAL_PIECES
        mine = [pltpu.make_async_copy(w_ref.at[pl.ds(n * piece, piece), :],
                                      out_ref.at[2 * x + y, pl.ds(n * piece, piece), :], local_sems.at[n])
                for n in range(LOCAL_PIECES)]
        for cp in mine:
            cp.start()
        first = [copy(j, (x, y), c, (*chip, c), src=w_ref.at[pl.ds(c * half, half), :]) for j, chip in enumerate(chips)]
        for cp in first:
            cp.start()
        passed = [copy(3 + j, chip, c, sibling) for j, chip in enumerate(chips)]
        for j, chip in enumerate(chips):
            copy(j, chip, c, (x, y, c)).wait_recv()
            passed[j].start()
        for j, chip in enumerate(chips):
            copy(3 + j, chip, 1 - c, (x, y, c)).wait_recv()
        for cp in first + passed:
            cp.wait_send()
        for cp in mine:
            cp.wait()

    assert rows % (16 * LOCAL_PIECES) == 0
    return pl.pallas_call(
        body, out_shape=jax.ShapeDtypeStruct((N_CHIPS, rows, wd), wpk.dtype),
        in_specs=[ANY], out_specs=ANY,
        scratch_shapes=[pltpu.SemaphoreType.DMA((6,)), pltpu.SemaphoreType.DMA((6,)),
                        pltpu.SemaphoreType.DMA((LOCAL_PIECES,))],
        name="gather_weights")(wpk)


HBM = pl.BlockSpec(memory_space=pltpu.HBM)
SEM = pl.BlockSpec(memory_space=pltpu.SEMAPHORE)
EFFECT = pltpu.SideEffectType.DATAFLOW_SIDE_EFFECTING
N_LATE = 6


def _gather_late_start(wpk, tag):
    rows, wd = wpk.shape
    half = rows // 2

    def body(w_ref, land_ref, send_sems, recv_sems, w_thru, land_thru, token):
        x, y, c = _place()
        chips = [(1 - x, y), (x, 1 - y), (1 - x, 1 - y)]
        for j, chip in enumerate(chips):
            for to_core in range(2):
                pltpu.make_async_remote_copy(
                    src_ref=w_ref.at[pl.ds(c * half, half), :],
                    dst_ref=land_ref.at[2 * x + y, pl.ds(c * half, half), :],
                    send_sem=send_sems.at[2 * j + to_core], recv_sem=recv_sems.at[2 * j + c],
                    device_id=(*chip, to_core), device_id_type=MESH).start()
        token[...] = jnp.zeros_like(token)

    land = lax.empty((N_CHIPS, rows, wd), wpk.dtype)
    return pl.pallas_call(
        body, name="gather_" + tag + "_start",
        out_shape=(pltpu.SemaphoreType.DMA((N_LATE,)), pltpu.SemaphoreType.DMA((N_LATE,)),
                   pltpu.HBM(wpk.shape, wpk.dtype), pltpu.HBM(land.shape, land.dtype),
                   jax.ShapeDtypeStruct((8, LANES), F32)),
        in_specs=(HBM, HBM), out_specs=(SEM, SEM, HBM, HBM, pl.BlockSpec(memory_space=pltpu.VMEM)),
        input_output_aliases={0: 2, 1: 3},
        compiler_params=pltpu.CompilerParams(has_side_effects=EFFECT),
    )(pltpu.with_memory_space_constraint(wpk, pltpu.HBM), pltpu.with_memory_space_constraint(land, pltpu.HBM))


def _gather_late_wait(send_sems, recv_sems, w_thru, land_thru, after, tag):
    rows, wd = w_thru.shape
    half = rows // 2

    def body(w_ref, land_ref, send_sems, recv_sems, after_ref, w_dead, land_out):
        x, y, c = _place()
        for n in range(N_LATE):
            cp = pltpu.make_async_remote_copy(
                src_ref=w_ref.at[pl.ds(0, half), :], dst_ref=land_ref.at[0, pl.ds(0, half), :],
                send_sem=send_sems.at[n], recv_sem=recv_sems.at[n], device_id=(x, y, c), device_id_type=MESH)
            cp.wait_send()
            cp.wait_recv()

    return pl.pallas_call(
        body, name="gather_" + tag + "_wait",
        out_shape=(pltpu.HBM(w_thru.shape, w_thru.dtype), pltpu.HBM(land_thru.shape, land_thru.dtype)),
        in_specs=(HBM, HBM, SEM, SEM, ANY), out_specs=(HBM, HBM), input_output_aliases={0: 0, 1: 1},
        compiler_params=pltpu.CompilerParams(has_side_effects=EFFECT),
    )(w_thru, land_thru, send_sems, recv_sems, after)[1]


N_PART = 7


def _reduce_start(gbf, tag):
    _, _, hrows, wd = gbf.shape

    def body(g_ref, land_ref, send_sems, recv_sems, g_thru, land_thru, token):
        x, y, c = _place()
        chips = [(1 - x, y), (x, 1 - y), (1 - x, 1 - y)]
        for j, chip in enumerate(chips):
            for to_core in range(2):
                pltpu.make_async_remote_copy(
                    src_ref=g_ref.at[2 * chip[0] + chip[1], to_core], dst_ref=land_ref.at[2 * j + c],
                    send_sem=send_sems.at[2 * j + to_core], recv_sem=recv_sems.at[2 * j + c],
                    device_id=(*chip, to_core), device_id_type=MESH).start()
        pltpu.make_async_remote_copy(
            src_ref=g_ref.at[2 * x + y, 1 - c], dst_ref=land_ref.at[N_PART - 1],
            send_sem=send_sems.at[N_PART - 1], recv_sem=recv_sems.at[N_PART - 1],
            device_id=(x, y, 1 - c), device_id_type=MESH).start()
        token[...] = jnp.zeros_like(token)

    land = lax.empty((N_PART, hrows, wd), gbf.dtype)
    return pl.pallas_call(
        body, name="reduce_" + tag + "_start",
        out_shape=(pltpu.SemaphoreType.DMA((N_PART,)), pltpu.SemaphoreType.DMA((N_PART,)),
                   pltpu.HBM(gbf.shape, gbf.dtype), pltpu.HBM(land.shape, land.dtype),
                   jax.ShapeDtypeStruct((8, LANES), F32)),
        in_specs=(HBM, HBM), out_specs=(SEM, SEM, HBM, HBM, pl.BlockSpec(memory_space=pltpu.VMEM)),
        input_output_aliases={0: 2, 1: 3},
        compiler_params=pltpu.CompilerParams(has_side_effects=EFFECT),
    )(pltpu.with_memory_space_constraint(gbf, pltpu.HBM), pltpu.with_memory_space_constraint(land, pltpu.HBM))


def _reduce_wait(send_sems, recv_sems, g_thru, land_thru, after, tag):
    def body(g_ref, land_ref, send_sems, recv_sems, after_ref, g_dead, land_out):
        x, y, c = _place()
        for n in range(N_PART):
            cp = pltpu.make_async_remote_copy(
                src_ref=g_ref.at[0, 0], dst_ref=land_ref.at[0], send_sem=send_sems.at[n], recv_sem=recv_sems.at[n],
                device_id=(x, y, c), device_id_type=MESH)
            cp.wait_send()
            cp.wait_recv()

    return pl.pallas_call(
        body, name="reduce_" + tag + "_wait",
        out_shape=(pltpu.HBM(g_thru.shape, g_thru.dtype), pltpu.HBM(land_thru.shape, land_thru.dtype)),
        in_specs=(HBM, HBM, SEM, SEM, ANY), out_specs=(HBM, HBM), input_output_aliases={0: 0, 1: 1},
        compiler_params=pltpu.CompilerParams(has_side_effects=EFFECT),
    )(g_thru, land_thru, send_sems, recv_sems, after)[1]


def _sibling_swap(mine):
    def body(m_ref, out_ref, send_sem, recv_sem):
        x, y, c = _place()
        cp = pltpu.make_async_remote_copy(
            src_ref=m_ref, dst_ref=out_ref, send_sem=send_sem, recv_sem=recv_sem,
            device_id=(x, y, 1 - c), device_id_type=MESH)
        cp.start()
        cp.wait()

    return pl.pallas_call(
        body, out_shape=jax.ShapeDtypeStruct(mine.shape, mine.dtype), in_specs=[ANY], out_specs=ANY,
        scratch_shapes=[pltpu.SemaphoreType.DMA, pltpu.SemaphoreType.DMA], name="grad_sibling_swap")(mine)


def _adamw(w, g, m, v):
    m = ADAM_B1 * m + (1.0 - ADAM_B1) * g
    v = ADAM_B2 * v + (1.0 - ADAM_B2) * (g * g)
    m_hat = m / (1.0 - ADAM_B1 ** ADAM_STEP)
    v_hat = v / (1.0 - ADAM_B2 ** ADAM_STEP)
    delta = -ADAM_LR * (m_hat / (jnp.sqrt(v_hat) + ADAM_EPS) + ADAM_WD * w)
    return delta, m, v


def _small_allreduce_adamw(gs, ws, ms, vs):
    n_dev = 8
    n_par = len(gs)
    wd = PACK_W
    chunks = []
    for p, g in enumerate(gs):
        for off in range(0, g.shape[1], wd):
            chunks.append((p, len(chunks), off, min(wd, g.shape[1] - off)))
    rows = _round_up(len(chunks), 8)

    def body(*refs):
        g_refs, w_refs, m_refs, v_refs = (refs[k * n_par:(k + 1) * n_par] for k in range(4))
        go_refs, d_refs, mo_refs, vo_refs = (refs[(4 + k) * n_par:(5 + k) * n_par] for k in range(4))
        mine_ref, all_ref, send_sems, recv_sems = refs[8 * n_par:]
        x, y, c = _place()
        me, sibling = (x, y, c), (x, y, 1 - c)
        chips = [(1 - x, y), (x, 1 - y), (1 - x, 1 - y)]

        def slot(px, py, pc):
            return all_ref.at[4 * px + 2 * py + pc]

        def copy(k, block, to, src=None):
            return pltpu.make_async_remote_copy(
                src_ref=slot(*block) if src is None else src, dst_ref=slot(*block),
                send_sem=send_sems.at[k], recv_sem=recv_sems.at[k], device_id=to, device_id_type=MESH)

        mine_ref[...] = jnp.zeros_like(mine_ref)
        for p, row, off, width in chunks:
            mine_ref[row:row + 1, 0:width] = g_refs[p][:, off:off + width]
        all_ref[4 * x + 2 * y + c] = mine_ref[...]
        first = [copy(0, me, sibling, src=mine_ref)]
        first += [copy(1 + j, me, (*chip, c), src=mine_ref) for j, chip in enumerate(chips)]
        for cp in first:
            cp.start()
        passed = [copy(4 + j, (*chip, c), sibling) for j, chip in enumerate(chips)]
        for j, chip in enumerate(chips):
            copy(1 + j, (*chip, c), me).wait_recv()
            passed[j].start()
        copy(0, sibling, me).wait_recv()
        for j, chip in enumerate(chips):
            copy(4 + j, (*chip, 1 - c), me).wait_recv()
        for cp in first + passed:
            cp.wait_send()
        tot = jnp.zeros((rows, wd), F32)
        for dev in range(n_dev):
            tot = tot + all_ref[dev]
        mine_ref[...] = tot
        for p, row, off, width in chunks:
            cols = slice(off, off + width)
            g = mine_ref[row:row + 1, 0:width]
            delta, m_new, v_new = _adamw(w_refs[p][:, cols], g, m_refs[p][:, cols], v_refs[p][:, cols])
            go_refs[p][:, cols] = g
            d_refs[p][:, cols] = delta
            mo_refs[p][:, cols] = m_new
            vo_refs[p][:, cols] = v_new

    vm = pl.BlockSpec(memory_space=pltpu.VMEM)
    shp = [jax.ShapeDtypeStruct(g.shape, F32) for g in gs]
    res = pl.pallas_call(
        body, out_shape=shp * 4, in_specs=[vm] * (4 * n_par), out_specs=[vm] * (4 * n_par),
        scratch_shapes=[pltpu.VMEM((rows, wd), F32), pltpu.VMEM((n_dev, rows, wd), F32),
                        pltpu.SemaphoreType.DMA((7,)), pltpu.SemaphoreType.DMA((7,))],
        name="small_allreduce_adamw")(*gs, *ws, *ms, *vs)
    return [res[k * n_par:(k + 1) * n_par] for k in range(4)]


def kernel(x, positions, ln_pre_mix, ln_post_mix, ln_pre_mlp, ln_post_mlp, w_in, b_in, q_a_norm, w_uq, kv_a_norm, w_uk, w_uv, w_o_mla, w_o_fox, w_out, w_ff1, w_ff2, loss_target, m_ln_pre_mix, m_ln_post_mix, m_ln_pre_mlp, m_ln_post_mlp, m_w_in, m_b_in, m_q_a_norm, m_w_uq, m_kv_a_norm, m_w_uk, m_w_uv, m_w_o_mla, m_w_o_fox, m_w_out, m_w_ff1, m_w_ff2, v_ln_pre_mix, v_ln_post_mix, v_ln_pre_mlp, v_ln_post_mlp, v_w_in, v_b_in, v_q_a_norm, v_w_uq, v_kv_a_norm, v_w_uk, v_w_uv, v_w_o_mla, v_w_o_fox, v_w_out, v_w_ff1, v_w_ff2):
    w = dict(ln_pre_mix=ln_pre_mix, ln_post_mix=ln_post_mix, ln_pre_mlp=ln_pre_mlp, ln_post_mlp=ln_post_mlp, w_in=w_in,
             b_in=b_in, q_a_norm=q_a_norm, w_uq=w_uq, kv_a_norm=kv_a_norm, w_uk=w_uk, w_uv=w_uv, w_o_mla=w_o_mla,
             w_o_fox=w_o_fox, w_out=w_out, w_ff1=w_ff1, w_ff2=w_ff2)
    mom = dict(ln_pre_mix=m_ln_pre_mix, ln_post_mix=m_ln_post_mix, ln_pre_mlp=m_ln_pre_mlp, ln_post_mlp=m_ln_post_mlp,
               w_in=m_w_in, b_in=m_b_in, q_a_norm=m_q_a_norm, w_uq=m_w_uq, kv_a_norm=m_kv_a_norm, w_uk=m_w_uk,
               w_uv=m_w_uv, w_o_mla=m_w_o_mla, w_o_fox=m_w_o_fox, w_out=m_w_out, w_ff1=m_w_ff1, w_ff2=m_w_ff2)
    var = dict(ln_pre_mix=v_ln_pre_mix, ln_post_mix=v_ln_post_mix, ln_pre_mlp=v_ln_pre_mlp, ln_post_mlp=v_ln_post_mlp,
               w_in=v_w_in, b_in=v_b_in, q_a_norm=v_q_a_norm, w_uq=v_w_uq, kv_a_norm=v_kv_a_norm, w_uk=v_w_uk,
               w_uv=v_w_uv, w_o_mla=v_w_o_mla, w_o_fox=v_w_o_fox, w_out=v_w_out, w_ff1=v_w_ff1, w_ff2=v_w_ff2)

    big_names = [nm for nm, _ in BIG]
    c = lax.axis_index("c")
    chip = 2 * lax.axis_index("x") + lax.axis_index("y")

    axes = dict(BIG)

    def assemble(names, lay, gathered):
        return {nm: _from_shards(s4, axes[nm] - 1) for nm, s4 in zip(names, lay.unpack(gathered), strict=True)}

    groups = {"mixer": [nm for nm in big_names if nm in MIXER], "mlp": [nm for nm in big_names if nm in LATE]}
    first_names = [nm for nm in big_names if nm not in MIXER and nm not in LATE]
    full = {nm: wv for nm, wv in w.items() if nm in SMALL}
    lay_first = _PackLayout([w[nm].shape[1:] for nm in first_names])
    first = _gather_weights(lay_first.pack([w[nm][0].astype(BF16) for nm in first_names]))
    full.update(assemble(first_names, lay_first, first))
    travelling = {}
    for tag, names in groups.items():
        lay = _PackLayout([w[nm].shape[1:] for nm in names])
        shard = lay.pack([w[nm][0].astype(BF16) for nm in names])
        started = _gather_late_start(lax.optimization_barrier((shard, first))[0], tag)
        travelling[tag] = (names, lay, shard, started[:4])
        full["b_in"] = full["b_in"] + started[4][0, 0]

    def late_weights(tag, after):
        names, lay, shard, handles = travelling[tag]
        land = _gather_late_wait(*handles, after, tag)
        land = lax.dynamic_update_slice(land, shard[None], (chip, 0, 0))
        return assemble(names, lay, land)

    grad_groups = {"early": [nm for nm in big_names if nm in EARLY_GRADS],
                   "late": [nm for nm in big_names if nm not in EARLY_GRADS]}
    sent = {}

    def send_grads(tag, g):
        names = grad_groups[tag]
        lay = _PackLayout([w[nm].shape[1:] for nm in names])
        hrows = lay.rows // 2
        gbf = lay.pack([_to_shards(g[nm], axes[nm] - 1).astype(BF16) for nm in names])
        gbf = gbf.reshape(N_CHIPS, 2, hrows, lay.width)
        started = _reduce_start(gbf, tag)
        own = lax.dynamic_index_in_dim(lax.dynamic_index_in_dim(started[2], chip, axis=0, keepdims=False), c, axis=0,
                                       keepdims=False)
        sent[tag] = (names, lay, started[:4], own)
        return started[4][0, 0]

    loss_local, grad_x, grads = _local_step(x[0], positions[0], loss_target[0], full, late_weights, send_grads)

    def f_add8(ti, pa):
        tot = ti[0].astype(F32)
        for part in ti[1:]:
            tot = tot + part.astype(F32)
        return [tot], []

    reduced = []
    for tag, (names, lay, handles, own) in sent.items():
        parts = _reduce_wait(*handles, grad_x, tag)
        reduced.append(_rowwise("grad_add_" + tag, f_add8, [own] + [parts[n] for n in range(N_PART)], [],
                                [(lay.width, F32)])[0])

    assert len({lay.width for _, lay, _, _ in sent.values()}) == 1
    red = jnp.concatenate(reduced, axis=0)
    sib = _sibling_swap(red)
    lower, upper = jnp.where(c == 0, red, sib), jnp.where(c == 0, sib, red)
    g_by_name, row = {}, 0
    for names, lay, _, _ in sent.values():
        hrows = lay.rows // 2
        both = jnp.concatenate([lower[row:row + hrows], upper[row:row + hrows]], axis=0)
        g_by_name.update(zip(names, lay.unpack(both), strict=True))
        row += hrows
    g_shards = [g_by_name[nm] for nm in big_names]

    def f_adamw(ti, pa):
        wv, gv, mv, vv = ti
        return list(_adamw(wv, gv, mv, vv)), []

    out = {"grad": {}, "delta": {}, "m": {}, "v": {}}
    for nm, g_sh in zip(big_names, g_shards, strict=True):
        wd = g_sh.shape[1]
        d_sh, m_sh, v_sh = _rowwise("adamw_" + nm, f_adamw, [w[nm][0], g_sh, mom[nm][0], var[nm][0]], [], [(wd, F32)] * 3)
        out["grad"][nm], out["delta"][nm], out["m"][nm], out["v"][nm] = g_sh[None], d_sh[None], m_sh[None], v_sh[None]

    loss_row = jnp.zeros((1, LANES), F32) + loss_local
    blank = jnp.zeros((1, LANES), F32)
    small = _small_allreduce_adamw([grads[nm] for nm in SMALL] + [loss_row], [w[nm] for nm in SMALL] + [blank],
                                   [mom[nm] for nm in SMALL] + [blank], [var[nm] for nm in SMALL] + [blank])
    for kind, arrs in zip(("grad", "delta", "m", "v"), small, strict=True):
        for nm, arr in zip(SMALL, arrs[:len(SMALL)], strict=True):
            out[kind][nm] = arr
    loss = small[0][len(SMALL)][0, 0]

    return (loss, grad_x[None], *[out["grad"][nm] for nm in ALL_W], *[out["delta"][nm] for nm in ALL_W],
            *[out["m"][nm] for nm in ALL_W], *[out["v"][nm] for nm in ALL_W])
```

```python
import functools
import math

import jax
import jax.numpy as jnp
from jax import lax
from jax.experimental import pallas as pl
from jax.experimental.pallas import tpu as pltpu

F32 = jnp.float32
BF16 = jnp.bfloat16

MLA_HEADS = 8
MLA_Q_LORA = 256
MLA_KV_LORA = 128
MLA_NOPE = 64
MLA_ROPE = 32
MLA_V = 64
FOX_HEADS = 8
FOX_DIM = 64
ROPE_THETA = 10000.0
NORM_EPS = 1e-6
HALF_ROPE = MLA_ROPE // 2

ADAM_LR = 0.001
ADAM_B1 = 0.9
ADAM_B2 = 0.999
ADAM_EPS = 1e-08
ADAM_WD = 0.01
ADAM_STEP = 10

LANES = 128
VMEM_LIMIT = 56 * 1024 * 1024
ATT_TILE = 1024
FWD_GROUP_LOG2 = 1
FWD_GROUP = 1 << FWD_GROUP_LOG2
MM_VMEM_BUDGET = 40 * 1024 * 1024
ROW_TILE_BYTES = 12 * 1024 * 1024
MXU_WIDTH = 256
MXU_MACS_PER_S = 4.98e14
HBM_BYTES_PER_S = 3.2e12
STEP_OVERHEAD_S = 0.35e-6
NEG = -1e30
LOG2E = math.log2(math.e)
MESH = pl.DeviceIdType.MESH

V_ONES = 64
FOX_Q_F = 64
FOX_Q_L = 67
FOX_Q_ONES = 70
MLA_Q_L = 96

BIG = (("w_in", 2), ("w_uq", 2), ("w_uk", 2), ("w_uv", 2), ("w_o_mla", 2), ("w_o_fox", 2),
       ("w_out", 1), ("w_ff1", 2), ("w_ff2", 1))
MIXER = ("w_uq", "w_uk", "w_uv", "w_o_mla", "w_o_fox")
LATE = ("w_out", "w_ff1", "w_ff2")
EARLY_GRADS = ("w_o_mla", "w_o_fox", "w_out", "w_ff1", "w_ff2")
SMALL = ("ln_pre_mix", "ln_post_mix", "ln_pre_mlp", "ln_post_mlp", "b_in", "q_a_norm", "kv_a_norm")
ALL_W = ("ln_pre_mix", "ln_post_mix", "ln_pre_mlp", "ln_post_mlp", "w_in", "b_in", "q_a_norm", "w_uq",
         "kv_a_norm", "w_uk", "w_uv", "w_o_mla", "w_o_fox", "w_out", "w_ff1", "w_ff2")
N_CHIPS = 4
PACK_W = 1024
LOCAL_PIECES = 8

_NT = (((1,), (1,)), ((), ()))
_TN = (((0,), (0,)), ((), ()))


def _cparams(sem=None):
    return pltpu.CompilerParams(dimension_semantics=sem, vmem_limit_bytes=VMEM_LIMIT)


def _divisor_tile(n, limit, mult):
    if n <= limit:
        return n
    best = None
    t = mult
    while t <= limit:
        if n % t == 0:
            best = t
        t += mult
    assert best is not None, (n, limit, mult)
    return best


def _round_up(v, mult):
    return -(-v // mult) * mult


def _mm_tiles(m, k, n, io_bytes):
    best = None
    for tm in (2048, 1024, 512, 256, 128):
        if m % tm:
            continue
        for tn in range(LANES, min(n, 2048) + 1, LANES):
            if n % tn:
                continue
            vmem = 2 * (tm * k * 2 + k * tn * 2 + tm * tn * io_bytes) + tm * tn * 4
            if vmem > MM_VMEM_BUDGET:
                continue
            mxu = m * k * n * (_round_up(tn, MXU_WIDTH) / tn) / MXU_MACS_PER_S
            hbm = (m * k * 2 + (m // tm) * k * n * 2 + m * n * io_bytes) / HBM_BYTES_PER_S
            cost = max(mxu, hbm) + (m // tm) * (n // tn) * STEP_OVERHEAD_S
            if best is None or cost < best[0]:
                best = (cost, tm, tn)
    assert best is not None, (m, k, n)
    return best[1], best[2]


def _mm(a, b, *, out_dtype, name, bias=None, transpose_b=False, extras=(), epilogue=None):
    m, k = a.shape
    n = b.shape[0] if transpose_b else b.shape[1]
    assert (b.shape[1] if transpose_b else b.shape[0]) == k and a.dtype == BF16 and b.dtype == BF16
    out_dtypes = list(out_dtype) if isinstance(out_dtype, (list, tuple)) else [out_dtype]
    n_ex = len(extras)
    tm, tn = _mm_tiles(m, k, n, sum(jnp.dtype(dt).itemsize for dt in out_dtypes) + 4 * n_ex)

    def body(*refs):
        a_ref, b_ref = refs[:2]
        pos = 2
        bias_ref = None
        if bias is not None:
            bias_ref = refs[pos]
            pos += 1
        ex_refs = refs[pos:pos + n_ex]
        o_refs = refs[pos + n_ex:]
        if transpose_b:
            acc = lax.dot_general(a_ref[...], b_ref[...], _NT, preferred_element_type=F32)
        else:
            acc = jnp.dot(a_ref[...], b_ref[...], preferred_element_type=F32)
        if bias_ref is not None:
            acc = acc + bias_ref[...]
        vals = [acc] if epilogue is None else epilogue(acc, [r[...] for r in ex_refs])
        for ref, val in zip(o_refs, vals, strict=True):
            ref[...] = val.astype(ref.dtype)

    b_spec = pl.BlockSpec((tn, k), lambda i, j: (j, 0)) if transpose_b else pl.BlockSpec((k, tn), lambda i, j: (0, j))
    in_specs = [pl.BlockSpec((tm, k), lambda i, j: (i, 0)), b_spec]
    args = [a, b]
    if bias is not None:
        in_specs.append(pl.BlockSpec((1, tn), lambda i, j: (0, j)))
        args.append(bias)
    in_specs += [pl.BlockSpec((tm, tn), lambda i, j: (i, j)) for _ in extras]
    args += list(extras)
    res = pl.pallas_call(
        body, grid=(m // tm, n // tn), in_specs=in_specs,
        out_specs=[pl.BlockSpec((tm, tn), lambda i, j: (i, j)) for _ in out_dtypes],
        out_shape=[jax.ShapeDtypeStruct((m, n), dt) for dt in out_dtypes],
        compiler_params=_cparams(("parallel", "parallel")), name=name)(*args)
    return res if isinstance(out_dtype, (list, tuple)) else res[0]


def _mm_tn(a, b, *, name):
    s, m = a.shape
    s2, n = b.shape
    assert s == s2 and a.dtype == BF16 and b.dtype == BF16
    tm = _divisor_tile(m, 1024, LANES)
    tn = _divisor_tile(n, 2304, LANES)
    tk = _divisor_tile(s, 1024, 16)
    nk = s // tk

    def body(a_ref, b_ref, o_ref, acc_ref):
        @pl.when(pl.program_id(2) == 0)
        def _():
            acc_ref[...] = jnp.zeros_like(acc_ref)

        acc_ref[...] += lax.dot_general(a_ref[...], b_ref[...], _TN, preferred_element_type=F32)

        @pl.when(pl.program_id(2) == nk - 1)
        def _():
            o_ref[...] = acc_ref[...].astype(o_ref.dtype)

    return pl.pallas_call(
        body, grid=(m // tm, n // tn, nk),
        in_specs=[pl.BlockSpec((tk, tm), lambda i, j, k: (k, i)), pl.BlockSpec((tk, tn), lambda i, j, k: (k, j))],
        out_specs=pl.BlockSpec((tm, tn), lambda i, j, k: (i, j)),
        out_shape=jax.ShapeDtypeStruct((m, n), BF16),
        scratch_shapes=[pltpu.VMEM((tm, tn), F32)],
        compiler_params=_cparams(("parallel", "parallel", "arbitrary")), name=name)(a, b)


def _rowwise(name, fn, tiled, params, outs, reds=(), reverse=False):
    wins = [t if isinstance(t, tuple) else (t, 0, t.shape[1]) for t in tiled]
    s = wins[0][0].shape[0]
    row_bytes = sum(w * arr.dtype.itemsize for arr, _, w in wins) + sum(w * jnp.dtype(d).itemsize for w, d in outs)
    ts = _divisor_tile(s, max(16, min(1024, ROW_TILE_BYTES // row_bytes)), 16)
    nt, npar, nout = len(wins), len(params), len(outs)
    n_tiles = s // ts

    def row(i):
        return n_tiles - 1 - i if reverse else i

    def body(*refs):
        tin = [r[...] for r in refs[:nt]]
        par = [r[...] for r in refs[nt:nt + npar]]
        out_refs = refs[nt + npar:nt + npar + nout]
        red_refs = refs[nt + npar + nout:]
        o, r = fn(tin, par)
        for ref, val in zip(out_refs, o, strict=True):
            ref[...] = val.astype(ref.dtype)
        if red_refs:
            @pl.when(pl.program_id(0) == 0)
            def _():
                for ref in red_refs:
                    ref[...] = jnp.zeros_like(ref)

            for ref, val in zip(red_refs, r, strict=True):
                ref[...] += val

    in_specs = [pl.BlockSpec((ts, w), functools.partial(lambda i, cb: (row(i), cb), cb=cb)) for _, cb, w in wins]
    in_specs += [pl.BlockSpec(p.shape, lambda i: (0, 0)) for p in params]
    out_specs = [pl.BlockSpec((ts, w), lambda i: (row(i), 0)) for w, _ in outs]
    out_specs += [pl.BlockSpec((1, w), lambda i: (0, 0)) for w in reds]
    out_shape = [jax.ShapeDtypeStruct((s, w), d) for w, d in outs]
    out_shape += [jax.ShapeDtypeStruct((1, w), F32) for w in reds]
    return pl.pallas_call(
        body, grid=(n_tiles,), in_specs=in_specs, out_specs=out_specs, out_shape=out_shape,
        compiler_params=_cparams(("arbitrary",)), name=name)(*[w[0] for w in wins], *params)


def _rms(x, g):
    r = lax.rsqrt(jnp.mean(x * x, axis=-1, keepdims=True) + NORM_EPS)
    return x * r * g, r


def _rms_bwd(x, g, dy):
    r = lax.rsqrt(jnp.mean(x * x, axis=-1, keepdims=True) + NORM_EPS)
    gy = dy * g
    dx = r * gy - x * (r * r * r) * jnp.mean(x * gy, axis=-1, keepdims=True)
    dg = jnp.sum(dy * (x * r), axis=0, keepdims=True)
    return dx, dg


def _sigmoid(x):
    return 1.0 / (1.0 + jnp.exp(-x))


def _split3(x):
    hi = x.astype(BF16).astype(F32)
    r = x - hi
    mid = r.astype(BF16).astype(F32)
    lo = (r - mid).astype(BF16).astype(F32)
    return hi, mid, lo


def _lane(shape):
    return lax.broadcasted_iota(jnp.int32, shape, 1)


def _put3(blk, lane, pos, pieces):
    for k, piece in enumerate(pieces):
        blk = jnp.where(lane == pos + k, piece, blk)
    return blk


def _lane_column(blk, lane, pos):
    return jnp.sum(jnp.where(lane == pos, blk, 0.0), axis=1, keepdims=True)


def _blocks(a, nh):
    return [a[:, h * LANES:(h + 1) * LANES] for h in range(nh)]


def _head_spread(nh, d):
    shift = d.bit_length() - 1
    assert 1 << shift == d
    r = lax.broadcasted_iota(jnp.int32, (nh * d, nh * LANES), 0)
    c = lax.broadcasted_iota(jnp.int32, (nh * d, nh * LANES), 1)
    return jnp.where(c == (r >> shift) * LANES + (r & (d - 1)), 1.0, 0.0).astype(BF16)


def _rope_block(x, c, sa, sb):
    return x * c + pltpu.roll(x, LANES - HALF_ROPE, 1) * sa + pltpu.roll(x, HALF_ROPE, 1) * sb


def _forget_cumsum(z, cb):
    s = z.shape[0]
    ts = _divisor_tile(s, 512, LANES)

    def body(x_ref, col_ref, carry):
        @pl.when(pl.program_id(0) == 0)
        def _():
            carry[...] = jnp.zeros_like(carry)

        x = x_ref[...]
        lf = jnp.minimum(x, 0.0) - jnp.log1p(jnp.exp(-jnp.abs(x)))
        r = lax.broadcasted_iota(jnp.int32, (ts, ts), 0)
        c = lax.broadcasted_iota(jnp.int32, (ts, ts), 1)
        tri = jnp.where(c <= r, 1.0, 0.0).astype(F32)
        col_ref[...] = jnp.dot(tri, lf, preferred_element_type=F32, precision=lax.Precision.HIGHEST) + carry[...]
        carry[...] += jnp.sum(lf, axis=0, keepdims=True)

    return pl.pallas_call(
        body, grid=(s // ts,),
        in_specs=[pl.BlockSpec((ts, LANES), lambda i: (i, cb))],
        out_specs=pl.BlockSpec((ts, LANES), lambda i: (i, 0)),
        out_shape=jax.ShapeDtypeStruct((s, LANES), F32),
        scratch_shapes=[pltpu.VMEM((1, LANES), F32)],
        compiler_params=_cparams(("arbitrary",)), name="forget_cumsum")(z)


def _forget_grad(dq_blocks, dk_blocks, lane):
    df = jnp.zeros(dq_blocks[0].shape, F32)
    for h, (dq_blk, dk_blk) in enumerate(zip(dq_blocks, dk_blocks, strict=True)):
        d_h = _lane_column(dq_blk, lane, FOX_Q_F) - _lane_column(dk_blk, lane, FOX_Q_ONES)
        df = jnp.where(lane == h, d_h, df)
    return df


def _forget_cumsum_bwd(df, z, cb):
    s = z.shape[0]
    ts = _divisor_tile(s, 512, LANES)
    nt = s // ts

    def body(df_ref, x_ref, o_ref, carry):
        @pl.when(pl.program_id(0) == 0)
        def _():
            carry[...] = jnp.zeros_like(carry)

        df = df_ref[...]
        r = lax.broadcasted_iota(jnp.int32, (ts, ts), 0)
        c = lax.broadcasted_iota(jnp.int32, (ts, ts), 1)
        tri = jnp.where(c >= r, 1.0, 0.0).astype(F32)
        rc = jnp.dot(tri, df, preferred_element_type=F32, precision=lax.Precision.HIGHEST) + carry[...]
        carry[...] += jnp.sum(df, axis=0, keepdims=True)
        o_ref[...] = (rc * (1.0 / (1.0 + jnp.exp(x_ref[...])))).astype(o_ref.dtype)

    return pl.pallas_call(
        body, grid=(nt,),
        in_specs=[pl.BlockSpec((ts, LANES), lambda i: (nt - 1 - i, 0)),
                  pl.BlockSpec((ts, LANES), lambda i: (nt - 1 - i, cb))],
        out_specs=pl.BlockSpec((ts, LANES), lambda i: (nt - 1 - i, 0)),
        out_shape=jax.ShapeDtypeStruct((s, LANES), BF16),
        scratch_shapes=[pltpu.VMEM((1, LANES), F32)],
        compiler_params=_cparams(("arbitrary",)), name="forget_cumsum_bwd")(df, z)


def _flash_fwd(q, k, v, scale, nh, l_lane, *, name):
    s = q.shape[0]
    t = min(ATT_TILE, s)
    half = t // 2 if t % (2 * LANES) == 0 else t
    nq = s // t
    c = scale * LOG2E

    def body(q_ref, k_ref, v_ref, o_ref, qb_ref):
        i = pl.program_id(1)
        qb = q_ref[...]

        def scores(q_rows, k0, nk):
            kb = k_ref[pl.ds(pl.multiple_of(k0, half), nk), :]
            return lax.dot_general(q_rows, kb, _NT, preferred_element_type=F32)

        def update(sc, k0, nk, carry):
            m, acc = carry
            m_new = jnp.maximum(m, jnp.max(sc, axis=1, keepdims=True))
            p = jnp.exp2((sc - m_new) * c)
            alpha = jnp.exp2((m - m_new) * c)
            vb = v_ref[pl.ds(pl.multiple_of(k0, half), nk), :]
            acc = alpha * acc + jnp.dot(p.astype(BF16), vb, preferred_element_type=F32)
            return m_new, acc

        def full_block(j, cr):
            return update(scores(qb, j * t, t), j * t, t, cr)

        def group(jj, cr):
            for n in range(FWD_GROUP):
                cr = full_block(FWD_GROUP * jj + n, cr)
            return cr

        def causal(sc):
            row = lax.broadcasted_iota(jnp.int32, sc.shape, 0)
            col = lax.broadcasted_iota(jnp.int32, sc.shape, 1)
            return jnp.where(col <= row, sc, NEG)

        init = (jnp.full((t, 1), NEG, F32), jnp.zeros((t, LANES), F32))
        n_groups = i >> FWD_GROUP_LOG2
        carry = lax.fori_loop(0, n_groups, group, init)
        carry = lax.fori_loop(n_groups * FWD_GROUP, i, full_block, carry)
        m, acc = update(causal(scores(qb, i * t, t)), i * t, t, carry)
        lane = _lane((t, LANES))
        l = _lane_column(acc, lane, V_ONES)
        o_ref[...] = (acc / l).astype(o_ref.dtype)
        big_l = m + jnp.log(l) / scale
        qb_ref[...] = _put3(qb.astype(F32), lane, l_lane, _split3(-big_l)).astype(qb_ref.dtype)

    head_rows = pl.BlockSpec((t, LANES), lambda h, i: (i, h))
    head_all = pl.BlockSpec((s, LANES), lambda h, i: (0, h))
    return pl.pallas_call(
        body, grid=(nh, nq), in_specs=[head_rows, head_all, head_all], out_specs=[head_rows, head_rows],
        out_shape=[jax.ShapeDtypeStruct(q.shape, BF16), jax.ShapeDtypeStruct(q.shape, BF16)],
        compiler_params=_cparams(("parallel", "arbitrary")), name=name)(q, k, v)


def _flash_bwd(qb, k, v, do, scale, nh, n_feat, *, name):
    s = qb.shape[0]
    t = min(ATT_TILE, s)
    half = t // 2 if t % (2 * LANES) == 0 else t
    nq = s // t
    c = scale * LOG2E

    def body(q_ref, k_ref, v_ref, do_ref, dk_ref, dv_ref, dq_ref):
        j = pl.program_id(1)
        kb = k_ref[...]
        vb = v_ref[...]

        @pl.when(j == 0)
        def _():
            dq_ref[...] = jnp.zeros_like(dq_ref)

        def part(q0, n_q, n_k, carry, q_off):
            dk_acc, dv_acc = carry
            rows = pl.ds(pl.multiple_of(q0, half), n_q)
            qblk = q_ref[rows, :]
            dob = do_ref[rows, :]
            kbb, vbb = kb[:n_k], vb[:n_k]
            st = lax.dot_general(kbb, qblk, _NT, preferred_element_type=F32)
            if q_off is not None:
                key = lax.broadcasted_iota(jnp.int32, st.shape, 0)
                qry = lax.broadcasted_iota(jnp.int32, st.shape, 1) + q_off
                st = jnp.where(key <= qry, st, NEG)
            pt = jnp.exp2(st * c)
            dv_new = jnp.dot(pt.astype(BF16), dob, preferred_element_type=F32)
            dpt = lax.dot_general(vbb, dob, _NT, preferred_element_type=F32)
            dsb = (pt * dpt).astype(BF16)
            dk_new = jnp.dot(dsb, qblk, preferred_element_type=F32)
            dq_ref[rows, :] += lax.dot_general(dsb, kbb, _TN, preferred_element_type=F32)
            if n_k == t:
                return dk_acc + dk_new, dv_acc + dv_new
            return (jnp.concatenate([dk_acc[:n_k] + dk_new, dk_acc[n_k:]], axis=0),
                    jnp.concatenate([dv_acc[:n_k] + dv_new, dv_acc[n_k:]], axis=0))

        def block(i, carry):
            return part(i * t, t, t, carry, None)

        init = (jnp.zeros((t, LANES), F32), jnp.zeros((t, LANES), F32))
        carry = part(j * t + half, t - half, t, init, half) if half < t else part(j * t, t, t, init, 0)
        if half < t:
            carry = part(j * t, half, half, carry, 0)
        rest = nq - 1 - j
        carry = lax.cond((rest & 1) == 1, lambda cr: block(j + 1, cr), lambda cr: cr, carry)
        first = j + 1 + (rest & 1)

        def pair(ii, cr):
            i0 = first + 2 * ii
            return block(i0 + 1, block(i0, cr))

        dk_acc, dv_acc = lax.fori_loop(0, rest >> 1, pair, carry)
        dk_ref[...] = dk_acc * jnp.where(_lane((t, LANES)) < n_feat, scale, 1.0)
        dv_ref[...] = dv_acc.astype(dv_ref.dtype)

        @pl.when(j == nq - 1)
        def _():
            dq_ref[...] = dq_ref[...] * jnp.where(_lane((s, LANES)) < n_feat, scale, 1.0)

    head_rows = pl.BlockSpec((t, LANES), lambda h, j: (j, h))
    head_all = pl.BlockSpec((s, LANES), lambda h, j: (0, h))
    shp = jax.ShapeDtypeStruct(qb.shape, F32)
    return pl.pallas_call(
        body, grid=(nh, nq), in_specs=[head_all, head_rows, head_rows, head_all],
        out_specs=[head_rows, head_rows, head_all],
        out_shape=[shp, jax.ShapeDtypeStruct(qb.shape, BF16), shp],
        compiler_params=_cparams(("parallel", "arbitrary")), name=name)(qb, k, v, do)


def _with_delta(do, o, nh, *, name):
    def fn(ti, pa):
        lane = _lane((ti[0].shape[0], LANES))
        out = []
        for d_blk, o_blk in zip(_blocks(ti[0], nh), _blocks(ti[1], nh), strict=True):
            delta = jnp.sum(d_blk * o_blk.astype(F32), axis=1, keepdims=True)
            out.append(_put3(d_blk, lane, V_ONES, _split3(-delta)))
        return [jnp.concatenate(out, axis=1)], []

    (res,) = _rowwise(name, fn, [do, o], [], [(do.shape[1], BF16)])
    return res


def _pad_heads(a, nh):
    d = a.shape[-1] // nh
    a = a.reshape(a.shape[:-1] + (nh, d))
    a = jnp.pad(a, [(0, 0)] * (a.ndim - 1) + [(0, LANES - d)])
    return a.reshape(a.shape[:-2] + (nh * LANES,))


def _unpad_heads(a, nh, d):
    a = a.reshape(a.shape[:-1] + (nh, LANES))[..., :d]
    return a.reshape(a.shape[:-2] + (nh * d,))


def _pad_head_rows(w, nh):
    return _pad_heads(w.T, nh).T


def _unpad_head_rows(g, nh, d):
    return _unpad_heads(g.T, nh, d).T


class _ZLayout:
    def __init__(self, d):
        fw = FOX_HEADS * FOX_DIM
        self.src = {}
        off = 0
        for nm, w in (("cq", MLA_Q_LORA), ("ckv", MLA_KV_LORA), ("kr", MLA_ROPE), ("fq", fw), ("fk", fw),
                      ("fv", fw), ("fl", FOX_HEADS), ("ga", d), ("gb", d)):
            self.src[nm] = (off, w)
            off += w
        self.dst = {}
        off = 0
        for nm, w in (("ga", d), ("gb", d), ("fq", fw), ("fk", fw), ("fv", fw), ("cq", MLA_Q_LORA),
                      ("ckv", MLA_KV_LORA), ("kr", LANES), ("fl", LANES)):
            assert off % w == 0
            self.dst[nm] = (off, w)
            off += w
        self.width = off
        self.split = self.dst["cq"][0]
        assert all((o - self.split) % w == 0 for o, w in self.dst.values() if o >= self.split)

    def to_kernel(self, w):
        def seg(nm):
            off, wd = self.src[nm]
            return w[..., off:off + wd]

        def pad(a, left, total):
            return jnp.pad(a, [(0, 0)] * (a.ndim - 1) + [(left, total - left - a.shape[-1])])

        return jnp.concatenate([seg("ga"), seg("gb"), seg("fq"), seg("fk"), seg("fv"), seg("cq"), seg("ckv"),
                                pad(seg("kr"), MLA_NOPE, LANES), pad(seg("fl"), 0, LANES)], axis=-1)

    def from_kernel(self, g):
        def seg(nm, lo=0, hi=None):
            off, wd = self.dst[nm]
            return g[..., off + lo:off + (wd if hi is None else hi)]

        return jnp.concatenate([seg("cq"), seg("ckv"), seg("kr", MLA_NOPE, MLA_NOPE + MLA_ROPE), seg("fq"), seg("fk"),
                                seg("fv"), seg("fl", 0, FOX_HEADS), seg("ga"), seg("gb")], axis=-1)


def _local_step(x, positions, target, wts, late_weights, send_grads):
    s, d = x.shape
    zl = _ZLayout(d)
    hw = MLA_HEADS * LANES
    assert MLA_HEADS == FOX_HEADS
    scale_mla = (MLA_NOPE + MLA_ROPE) ** -0.5
    scale_fox = FOX_DIM ** -0.5

    inv_freq = ROPE_THETA ** (-jnp.arange(HALF_ROPE, dtype=F32) / HALF_ROPE)
    ang = positions.astype(F32)[:, None] * inv_freq
    cos, sin = jnp.cos(ang), jnp.sin(ang)
    tail = jnp.zeros((s, LANES - MLA_NOPE - MLA_ROPE), F32)
    rc = jnp.concatenate([jnp.ones((s, MLA_NOPE), F32), cos, cos, tail], axis=1)
    ra = jnp.concatenate([jnp.zeros((s, MLA_NOPE), F32), -sin, jnp.zeros((s, HALF_ROPE), F32), tail], axis=1)
    rb = jnp.concatenate([jnp.zeros((s, MLA_NOPE + HALF_ROPE), F32), sin, tail], axis=1)

    w_in = zl.to_kernel(wts["w_in"])
    b_in = zl.to_kernel(wts["b_in"])

    def f_norm_in(ti, pa):
        y, _ = _rms(ti[0], pa[0])
        return [y], []

    (h,) = _rowwise("norm_in", f_norm_in, [x], [wts["ln_pre_mix"]], [(d, BF16)])
    z_lo = _mm(h, w_in[:, :zl.split], bias=b_in[:, :zl.split], out_dtype=BF16, name="proj_in_lo")
    z = _mm(h, w_in[:, zl.split:], bias=b_in[:, zl.split:], out_dtype=F32, name="proj_in_hi")

    def zwin(nm):
        off, wd = zl.dst[nm]
        return (z_lo, off // wd, wd) if off < zl.split else (z, (off - zl.split) // wd, wd)

    wts = {**wts, **late_weights("mixer", z)}
    w_uq = _pad_heads(wts["w_uq"], MLA_HEADS)
    w_ukv = jnp.concatenate([_pad_heads(wts["w_uk"], MLA_HEADS), _pad_heads(wts["w_uv"], MLA_HEADS)], axis=1)
    w_o_mla = _pad_head_rows(wts["w_o_mla"], MLA_HEADS)
    w_o_fox = _pad_head_rows(wts["w_o_fox"], FOX_HEADS)

    def f_mla_norms(ti, pa):
        cqn, _ = _rms(ti[0], pa[0])
        ckvn, _ = _rms(ti[1], pa[1])
        return [cqn, ckvn], []

    cqn, ckvn = _rowwise("mla_norms", f_mla_norms, [zwin("cq"), zwin("ckv")],
                         [wts["q_a_norm"], wts["kv_a_norm"]], [(MLA_Q_LORA, BF16), (MLA_KV_LORA, BF16)])
    qf = _mm(cqn, w_uq, out_dtype=F32, name="proj_uq")
    kv = _mm(ckvn, w_ukv, out_dtype=BF16, name="proj_ukv")

    def f_rope_q(ti, pa):
        xq, c_, a_, b_ = ti
        return [jnp.concatenate([_rope_block(blk, c_, a_, b_) for blk in _blocks(xq, MLA_HEADS)], axis=1)], []

    (q_mla,) = _rowwise("rope_q", f_rope_q, [qf, rc, ra, rb], [], [(hw, BF16)])

    def f_mla_kv(ti, pa):
        kn, vn, kr, c_, a_, b_ = ti
        lane = _lane(kr.shape)
        k_tail = jnp.where((lane >= MLA_Q_L) & (lane < MLA_Q_L + 3), 1.0, _rope_block(kr, c_, a_, b_))
        ones_v = (lane >= V_ONES) & (lane < V_ONES + 3)
        k_out = [jnp.where(lane < MLA_NOPE, blk.astype(F32), k_tail) for blk in _blocks(kn, MLA_HEADS)]
        v_out = [jnp.where(ones_v, 1.0, blk.astype(F32)) for blk in _blocks(vn, MLA_HEADS)]
        return [jnp.concatenate(k_out, axis=1), jnp.concatenate(v_out, axis=1)], []

    k_mla, v_mla = _rowwise("mla_kv", f_mla_kv, [(kv, 0, hw), (kv, 1, hw), zwin("kr"), rc, ra, rb], [],
                            [(hw, BF16), (hw, BF16)])
    o_mla, qb_mla = _flash_fwd(q_mla, k_mla, v_mla, scale_mla, MLA_HEADS, MLA_Q_L, name="mla_fwd")

    fl_cb = zwin("fl")[1]
    fcol = _forget_cumsum(z, fl_cb)

    def f_fox_qkv(ti, pa):
        spread = _head_spread(FOX_HEADS, FOX_DIM)
        fq, fk, fv = (jnp.dot(a, spread, preferred_element_type=F32) for a in ti[:3])
        fc = ti[3]
        lane = _lane(fc.shape)
        ones_q = (lane >= FOX_Q_ONES) & (lane < FOX_Q_ONES + 3)
        ones_k = (lane >= FOX_Q_F) & (lane < FOX_Q_ONES)
        ones_v = (lane >= V_ONES) & (lane < V_ONES + 3)
        q_out, k_out, v_out = [], [], []
        for hh, (qblk, kblk, vblk) in enumerate(zip(_blocks(fq, FOX_HEADS), _blocks(fk, FOX_HEADS),
                                                    _blocks(fv, FOX_HEADS), strict=True)):
            f_h = _lane_column(fc, lane, hh) * (1.0 / scale_fox)
            q_out.append(_put3(jnp.where(ones_q, 1.0, qblk), lane, FOX_Q_F, _split3(f_h)))
            k_out.append(_put3(jnp.where(ones_k, 1.0, kblk), lane, FOX_Q_ONES, _split3(-f_h)))
            v_out.append(jnp.where(ones_v, 1.0, vblk))
        return [jnp.concatenate(q_out, axis=1), jnp.concatenate(k_out, axis=1), jnp.concatenate(v_out, axis=1)], []

    q_fox, k_fox, v_fox = _rowwise("fox_qkv", f_fox_qkv, [zwin("fq"), zwin("fk"), zwin("fv"), fcol],
                                   [], [(hw, BF16)] * 3)
    o_fox, qb_fox = _flash_fwd(q_fox, k_fox, v_fox, scale_fox, FOX_HEADS, FOX_Q_L, name="fox_fwd")

    y_mla = _mm(o_mla, w_o_mla, out_dtype=BF16, name="proj_o_mla")
    y_fox = _mm(o_fox, w_o_fox, out_dtype=BF16, name="proj_o_fox")

    def f_gate(ti, pa):
        ga, gb, ya, yb = (a.astype(F32) for a in ti)
        return [_sigmoid(ga) * ya + _sigmoid(gb) * yb], []

    (merged,) = _rowwise("gate", f_gate, [zwin("ga"), zwin("gb"), y_mla, y_fox], [], [(d, BF16)])
    wts = {**wts, **late_weights("mlp", merged)}
    mix = _mm(merged, wts["w_out"], out_dtype=F32, name="proj_out")

    def f_resid1(ti, pa):
        xa, mx = ti
        y, _ = _rms(mx, pa[0])
        x1 = xa + y
        h2, _ = _rms(x1, pa[1])
        return [x1, h2], []

    x1, h2 = _rowwise("resid_mix", f_resid1, [x, mix], [wts["ln_post_mix"], wts["ln_pre_mlp"]], [(d, F32), (d, BF16)])

    def relu2(acc, ex):
        r = jnp.maximum(acc, 0.0)
        return [acc, r * r]

    u, act = _mm(h2, wts["w_ff1"], out_dtype=[BF16, BF16], name="ff1", epilogue=relu2)
    mo = _mm(act, wts["w_ff2"], out_dtype=F32, name="ff2")

    def f_loss(ti, pa):
        xa, mv, tg = ti
        y, _ = _rms(mv, pa[0])
        err = (xa + y) - tg
        g2 = err / d
        dmo, dg = _rms_bwd(mv, pa[0], g2)
        return [g2, dmo], [jnp.sum(err * err, axis=0, keepdims=True), dg]

    g2, d_mo, loss_cols, g_ln_post_mlp = _rowwise("loss", f_loss, [x1, mo, target], [wts["ln_post_mlp"]],
                                                  [(d, F32), (d, BF16)], [d, d])
    loss = 0.5 * jnp.sum(loss_cols) / d

    grads = {"ln_post_mlp": g_ln_post_mlp}
    grads["w_ff2"] = _mm_tn(act, d_mo, name="grad_ff2")

    def relu2_bwd(acc, ex):
        return [acc * (2.0 * jnp.maximum(ex[0], 0.0))]

    (d_u,) = _mm(d_mo, wts["w_ff2"], out_dtype=[BF16], name="ff2_bwd", transpose_b=True, extras=[u], epilogue=relu2_bwd)
    grads["w_ff1"] = _mm_tn(h2, d_u, name="grad_ff1")
    d_h2 = _mm(d_u, wts["w_ff1"], out_dtype=F32, name="ff1_bwd", transpose_b=True)

    def f_resid1_bwd(ti, pa):
        gres, dh2, x1v, mx = ti
        dx1n, dg_pre_mlp = _rms_bwd(x1v, pa[1], dh2)
        dx1 = gres + dx1n
        dmix, dg_post_mix = _rms_bwd(mx, pa[0], dx1)
        return [dx1, dmix], [dg_post_mix, dg_pre_mlp]

    d_x1, d_mix, grads["ln_post_mix"], grads["ln_pre_mlp"] = _rowwise(
        "resid_mix_bwd", f_resid1_bwd, [g2, d_h2, x1, mix], [wts["ln_post_mix"], wts["ln_pre_mlp"]],
        [(d, F32), (d, BF16)], [d, d])
    grads["w_out"] = _mm_tn(merged, d_mix, name="grad_out")
    d_merged = _mm(d_mix, wts["w_out"], out_dtype=BF16, name="proj_out_bwd", transpose_b=True)

    def f_gate_bwd(ti, pa):
        dm, ga, gb, ya, yb = (a.astype(F32) for a in ti)
        sa, sb = _sigmoid(ga), _sigmoid(gb)
        return [dm * sa, dm * sb, dm * ya * (sa * (1.0 - sa)), dm * yb * (sb * (1.0 - sb))], []

    d_ya, d_yb, d_ga, d_gb = _rowwise("gate_bwd", f_gate_bwd,
                                      [d_merged, zwin("ga"), zwin("gb"), y_mla, y_fox], [],
                                      [(d, BF16)] * 4)
    grads["w_o_mla"] = _unpad_head_rows(_mm_tn(o_mla, d_ya, name="grad_o_mla"), MLA_HEADS, MLA_V)
    grads["w_o_fox"] = _unpad_head_rows(_mm_tn(o_fox, d_yb, name="grad_o_fox"), FOX_HEADS, FOX_DIM)
    w_o_mla = w_o_mla + send_grads("early", {nm: grads[nm] for nm in EARLY_GRADS}).astype(BF16)
    do_mla = _with_delta(_mm(d_ya, w_o_mla, out_dtype=F32, name="proj_o_mla_bwd", transpose_b=True), o_mla,
                         MLA_HEADS, name="mla_delta")
    do_fox = _with_delta(_mm(d_yb, w_o_fox, out_dtype=F32, name="proj_o_fox_bwd", transpose_b=True), o_fox,
                         FOX_HEADS, name="fox_delta")

    dk_mla, dv_mla, dq_mla = _flash_bwd(qb_mla, k_mla, v_mla, do_mla, scale_mla, MLA_HEADS, MLA_NOPE + MLA_ROPE,
                                        name="mla_bwd")
    dk_fox, dv_fox, dq_fox = _flash_bwd(qb_fox, k_fox, v_fox, do_fox, scale_fox, FOX_HEADS, FOX_DIM, name="fox_bwd")

    def f_rope_q_bwd(ti, pa):
        g, c_, a_, b_ = ti
        return [jnp.concatenate([_rope_block(blk, c_, -a_, -b_) for blk in _blocks(g, MLA_HEADS)], axis=1)], []

    (d_qf,) = _rowwise("rope_q_bwd", f_rope_q_bwd, [dq_mla, rc, ra, rb], [], [(hw, BF16)])
    grads["w_uq"] = _unpad_heads(_mm_tn(cqn, d_qf, name="grad_uq"), MLA_HEADS, MLA_NOPE + MLA_ROPE)
    d_cqn = _mm(d_qf, w_uq, out_dtype=F32, name="proj_uq_bwd", transpose_b=True)

    def f_mla_kv_bwd(ti, pa):
        gk, gv, c_, a_, b_ = ti
        k_blocks = _blocks(gk, MLA_HEADS)
        tot = k_blocks[0]
        for blk in k_blocks[1:]:
            tot = tot + blk
        return [jnp.concatenate([gk, gv], axis=1), _rope_block(tot, c_, -a_, -b_)], []

    d_kv, d_kr = _rowwise("mla_kv_bwd", f_mla_kv_bwd, [dk_mla, dv_mla, rc, ra, rb], [], [(2 * hw, BF16), (LANES, BF16)])
    g_ukv = _mm_tn(ckvn, d_kv, name="grad_ukv")
    grads["w_uk"] = _unpad_heads(g_ukv[:, :hw], MLA_HEADS, MLA_NOPE)
    grads["w_uv"] = _unpad_heads(g_ukv[:, hw:], MLA_HEADS, MLA_V)
    d_ckvn = _mm(d_kv, w_ukv, out_dtype=F32, name="proj_ukv_bwd", transpose_b=True)

    def f_mla_norms_bwd(ti, pa):
        cq, ckv, dcqn, dckvn = ti
        dcq, dg_q = _rms_bwd(cq, pa[0], dcqn)
        dckv, dg_kv = _rms_bwd(ckv, pa[1], dckvn)
        return [dcq, dckv], [dg_q, dg_kv]

    d_cq, d_ckv, grads["q_a_norm"], grads["kv_a_norm"] = _rowwise(
        "mla_norms_bwd", f_mla_norms_bwd, [zwin("cq"), zwin("ckv"), d_cqn, d_ckvn],
        [wts["q_a_norm"], wts["kv_a_norm"]], [(MLA_Q_LORA, BF16), (MLA_KV_LORA, BF16)], [MLA_Q_LORA, MLA_KV_LORA])

    def f_fox_compact(ti, pa):
        gather = _head_spread(FOX_HEADS, FOX_DIM)
        df = _forget_grad(_blocks(ti[0], FOX_HEADS), _blocks(ti[1], FOX_HEADS), _lane((ti[0].shape[0], LANES)))
        return [lax.dot_general(a.astype(BF16), gather, _NT, preferred_element_type=F32) for a in ti] + [df], []

    d_fq, d_fk, d_fv, d_fcum = _rowwise("fox_compact", f_fox_compact, [dq_fox, dk_fox, dv_fox], [],
                                        [(FOX_HEADS * FOX_DIM, BF16)] * 3 + [(LANES, F32)])
    d_fl = _forget_cumsum_bwd(d_fcum, z, fl_cb)
    d_z = jnp.concatenate([d_ga, d_gb, d_fq, d_fk, d_fv, d_cq, d_ckv, d_kr, d_fl], axis=1)
    assert d_z.shape[1] == zl.width

    def f_bias(ti, pa):
        return [], [jnp.sum(ti[0].astype(F32), axis=0, keepdims=True)]

    (g_b_in,) = _rowwise("grad_b_in", f_bias, [d_z], [], [], [zl.width])
    grads["b_in"] = zl.from_kernel(g_b_in)
    grads["w_in"] = zl.from_kernel(_mm_tn(h, d_z, name="grad_in"))
    tok = send_grads("late", {nm: grads[nm] for nm, _ in BIG if nm not in EARLY_GRADS})
    d_h = _mm(d_z, w_in, bias=jnp.zeros((1, d), F32) + tok, out_dtype=F32, name="proj_in_bwd", transpose_b=True)

    def f_norm_in_bwd(ti, pa):
        dx1v, dh, xa = ti
        dxn, dg = _rms_bwd(xa, pa[0], dh)
        return [dx1v + dxn], [dg]

    grad_x, grads["ln_pre_mix"] = _rowwise("norm_in_bwd", f_norm_in_bwd, [d_x1, d_h, x], [wts["ln_pre_mix"]],
                                           [(d, F32)], [d])
    return loss, grad_x, grads


class _PackLayout:
    def __init__(self, shapes):
        self.shapes = list(shapes)
        self.width = _round_up(max(b for _, b in shapes), LANES)
        self.bands = []
        row = 0
        shelf = []
        for idx, (a, b) in enumerate(shapes):
            if 2 * _round_up(b, LANES) > self.width:
                self.bands.append((row, _round_up(a, 32), [(idx, 0)]))
                row += _round_up(a, 32)
            else:
                shelf.append(idx)
        col, items = 0, []
        for idx in shelf:
            wb = _round_up(shapes[idx][1], LANES)
            if col + wb > self.width:
                hgt = max(_round_up(shapes[i][0], 32) for i, _ in items)
                self.bands.append((row, hgt, items))
                row += hgt
                col, items = 0, []
            items.append((idx, col))
            col += wb
        if items:
            hgt = max(_round_up(shapes[i][0], 32) for i, _ in items)
            self.bands.append((row, hgt, items))
            row += hgt
        self.rows = _round_up(row, 16 * LOCAL_PIECES)

    def pack(self, arrs):
        lead = arrs[0].shape[:-2]
        no_pad = [(0, 0)] * len(lead)
        bands = []
        for _, hgt, items in self.bands:
            parts = []
            for k, (idx, col) in enumerate(items):
                a, b = self.shapes[idx]
                nxt = items[k + 1][1] if k + 1 < len(items) else self.width
                parts.append(jnp.pad(arrs[idx], no_pad + [(0, hgt - a), (0, nxt - col - b)]))
            bands.append(parts[0] if len(parts) == 1 else jnp.concatenate(parts, axis=-1))
        used = sum(hgt for _, hgt, _ in self.bands)
        if used < self.rows:
            bands.append(jnp.zeros(lead + (self.rows - used, self.width), arrs[0].dtype))
        return jnp.concatenate(bands, axis=-2)

    def unpack(self, packed):
        out = [None] * len(self.shapes)
        for row, _, items in self.bands:
            for idx, col in items:
                a, b = self.shapes[idx]
                out[idx] = packed[..., row:row + a, col:col + b]
        return out


def _to_shards(g, axis):
    if axis == 0:
        return g.reshape(N_CHIPS, g.shape[0] // N_CHIPS, g.shape[1])
    return jnp.stack(jnp.split(g, N_CHIPS, axis=1))


def _from_shards(s4, axis):
    n, a, b = s4.shape
    if axis == 0:
        return s4.reshape(n * a, b)
    return jnp.concatenate([s4[ch] for ch in range(n)], axis=1)


ANY = pl.BlockSpec(memory_space=pl.ANY)


def _place():
    return lax.axis_index("x"), lax.axis_index("y"), lax.axis_index("c")


def _gather_weights(wpk):
    rows, wd = wpk.shape
    half = rows // 2

    def body(w_ref, out_ref, send_sems, recv_sems, local_sems):
        x, y, c = _place()
        sibling = (x, y, 1 - c)
        chips = [(1 - x, y), (x, 1 - y), (1 - x, 1 - y)]

        def slab(chip, hf):
            return out_ref.at[2 * chip[0] + chip[1], pl.ds(hf * half, half), :]

        def copy(k, chip, hf, to, src=None):
            return pltpu.make_async_remote_copy(
                src_ref=slab(chip, hf) if src is None else src, dst_ref=slab(chip, hf),
                send_sem=send_sems.at[k], recv_sem=recv_sems.at[k], device_id=to, device_id_type=MESH)

        piece = rows // LOCAL_PIECES
        mine = [pltpu.make_async_copy(w_ref.at[pl.ds(n * piece, piece), :],
                                      out_ref.at[2 * x + y, pl.ds(n * piece, piece), :], local_sems.at[n])
                for n in range(LOCAL_PIECES)]
        for cp in mine:
            cp.start()
        first = [copy(j, (x, y), c, (*chip, c), src=w_ref.at[pl.ds(c * half, half), :]) for j, chip in enumerate(chips)]
        for cp in first:
            cp.start()
        passed = [copy(3 + j, chip, c, sibling) for j, chip in enumerate(chips)]
        for j, chip in enumerate(chips):
            copy(j, chip, c, (x, y, c)).wait_recv()
            passed[j].start()
        for j, chip in enumerate(chips):
            copy(3 + j, chip, 1 - c, (x, y, c)).wait_recv()
        for cp in first + passed:
            cp.wait_send()
        for cp in mine:
            cp.wait()

    assert rows % (16 * LOCAL_PIECES) == 0
    return pl.pallas_call(
        body, out_shape=jax.ShapeDtypeStruct((N_CHIPS, rows, wd), wpk.dtype),
        in_specs=[ANY], out_specs=ANY,
        scratch_shapes=[pltpu.SemaphoreType.DMA((6,)), pltpu.SemaphoreType.DMA((6,)),
                        pltpu.SemaphoreType.DMA((LOCAL_PIECES,))],
        name="gather_weights")(wpk)


HBM = pl.BlockSpec(memory_space=pltpu.HBM)
SEM = pl.BlockSpec(memory_space=pltpu.SEMAPHORE)
EFFECT = pltpu.SideEffectType.DATAFLOW_SIDE_EFFECTING
N_LATE = 6


def _gather_late_start(wpk, tag):
    rows, wd = wpk.shape
    half = rows // 2

    def body(w_ref, land_ref, send_sems, recv_sems, w_thru, land_thru, token):
        x, y, c = _place()
        chips = [(1 - x, y), (x, 1 - y), (1 - x, 1 - y)]
        for j, chip in enumerate(chips):
            for to_core in range(2):
                pltpu.make_async_remote_copy(
                    src_ref=w_ref.at[pl.ds(c * half, half), :],
                    dst_ref=land_ref.at[2 * x + y, pl.ds(c * half, half), :],
                    send_sem=send_sems.at[2 * j + to_core], recv_sem=recv_sems.at[2 * j + c],
                    device_id=(*chip, to_core), device_id_type=MESH).start()
        token[...] = jnp.zeros_like(token)

    land = lax.empty((N_CHIPS, rows, wd), wpk.dtype)
    return pl.pallas_call(
        body, name="gather_" + tag + "_start",
        out_shape=(pltpu.SemaphoreType.DMA((N_LATE,)), pltpu.SemaphoreType.DMA((N_LATE,)),
                   pltpu.HBM(wpk.shape, wpk.dtype), pltpu.HBM(land.shape, land.dtype),
                   jax.ShapeDtypeStruct((8, LANES), F32)),
        in_specs=(HBM, HBM), out_specs=(SEM, SEM, HBM, HBM, pl.BlockSpec(memory_space=pltpu.VMEM)),
        input_output_aliases={0: 2, 1: 3},
        compiler_params=pltpu.CompilerParams(has_side_effects=EFFECT),
    )(pltpu.with_memory_space_constraint(wpk, pltpu.HBM), pltpu.with_memory_space_constraint(land, pltpu.HBM))


def _gather_late_wait(send_sems, recv_sems, w_thru, land_thru, after, tag):
    rows, wd = w_thru.shape
    half = rows // 2

    def body(w_ref, land_ref, send_sems, recv_sems, after_ref, w_dead, land_out):
        x, y, c = _place()
        for n in range(N_LATE):
            cp = pltpu.make_async_remote_copy(
                src_ref=w_ref.at[pl.ds(0, half), :], dst_ref=land_ref.at[0, pl.ds(0, half), :],
                send_sem=send_sems.at[n], recv_sem=recv_sems.at[n], device_id=(x, y, c), device_id_type=MESH)
            cp.wait_send()
            cp.wait_recv()

    return pl.pallas_call(
        body, name="gather_" + tag + "_wait",
        out_shape=(pltpu.HBM(w_thru.shape, w_thru.dtype), pltpu.HBM(land_thru.shape, land_thru.dtype)),
        in_specs=(HBM, HBM, SEM, SEM, ANY), out_specs=(HBM, HBM), input_output_aliases={0: 0, 1: 1},
        compiler_params=pltpu.CompilerParams(has_side_effects=EFFECT),
    )(w_thru, land_thru, send_sems, recv_sems, after)[1]


N_PART = 7


def _reduce_start(gbf, tag):
    _, _, hrows, wd = gbf.shape

    def body(g_ref, land_ref, send_sems, recv_sems, g_thru, land_thru, token):
        x, y, c = _place()
        chips = [(1 - x, y), (x, 1 - y), (1 - x, 1 - y)]
        for j, chip in enumerate(chips):
            for to_core in range(2):
                pltpu.make_async_remote_copy(
                    src_ref=g_ref.at[2 * chip[0] + chip[1], to_core], dst_ref=land_ref.at[2 * j + c],
                    send_sem=send_sems.at[2 * j + to_core], recv_sem=recv_sems.at[2 * j + c],
                    device_id=(*chip, to_core), device_id_type=MESH).start()
        pltpu.make_async_remote_copy(
            src_ref=g_ref.at[2 * x + y, 1 - c], dst_ref=land_ref.at[N_PART - 1],
            send_sem=send_sems.at[N_PART - 1], recv_sem=recv_sems.at[N_PART - 1],
            device_id=(x, y, 1 - c), device_id_type=MESH).start()
        token[...] = jnp.zeros_like(token)

    land = lax.empty((N_PART, hrows, wd), gbf.dtype)
    return pl.pallas_call(
        body, name="reduce_" + tag + "_start",
        out_shape=(pltpu.SemaphoreType.DMA((N_PART,)), pltpu.SemaphoreType.DMA((N_PART,)),
                   pltpu.HBM(gbf.shape, gbf.dtype), pltpu.HBM(land.shape, land.dtype),
                   jax.ShapeDtypeStruct((8, LANES), F32)),
        in_specs=(HBM, HBM), out_specs=(SEM, SEM, HBM, HBM, pl.BlockSpec(memory_space=pltpu.VMEM)),
        input_output_aliases={0: 2, 1: 3},
        compiler_params=pltpu.CompilerParams(has_side_effects=EFFECT),
    )(pltpu.with_memory_space_constraint(gbf, pltpu.HBM), pltpu.with_memory_space_constraint(land, pltpu.HBM))


def _reduce_wait(send_sems, recv_sems, g_thru, land_thru, after, tag):
    def body(g_ref, land_ref, send_sems, recv_sems, after_ref, g_dead, land_out):
        x, y, c = _place()
        for n in range(N_PART):
            cp = pltpu.make_async_remote_copy(
                src_ref=g_ref.at[0, 0], dst_ref=land_ref.at[0], send_sem=send_sems.at[n], recv_sem=recv_sems.at[n],
                device_id=(x, y, c), device_id_type=MESH)
            cp.wait_send()
            cp.wait_recv()

    return pl.pallas_call(
        body, name="reduce_" + tag + "_wait",
        out_shape=(pltpu.HBM(g_thru.shape, g_thru.dtype), pltpu.HBM(land_thru.shape, land_thru.dtype)),
        in_specs=(HBM, HBM, SEM, SEM, ANY), out_specs=(HBM, HBM), input_output_aliases={0: 0, 1: 1},
        compiler_params=pltpu.CompilerParams(has_side_effects=EFFECT),
    )(g_thru, land_thru, send_sems, recv_sems, after)[1]


def _sibling_swap(mine):
    def body(m_ref, out_ref, send_sem, recv_sem):
        x, y, c = _place()
        cp = pltpu.make_async_remote_copy(
            src_ref=m_ref, dst_ref=out_ref, send_sem=send_sem, recv_sem=recv_sem,
            device_id=(x, y, 1 - c), device_id_type=MESH)
        cp.start()
        cp.wait()

    return pl.pallas_call(
        body, out_shape=jax.ShapeDtypeStruct(mine.shape, mine.dtype), in_specs=[ANY], out_specs=ANY,
        scratch_shapes=[pltpu.SemaphoreType.DMA, pltpu.SemaphoreType.DMA], name="grad_sibling_swap")(mine)


def _adamw(w, g, m, v):
    m = ADAM_B1 * m + (1.0 - ADAM_B1) * g
    v = ADAM_B2 * v + (1.0 - ADAM_B2) * (g * g)
    m_hat = m / (1.0 - ADAM_B1 ** ADAM_STEP)
    v_hat = v / (1.0 - ADAM_B2 ** ADAM_STEP)
    delta = -ADAM_LR * (m_hat / (jnp.sqrt(v_hat) + ADAM_EPS) + ADAM_WD * w)
    return delta, m, v


def _small_allreduce_adamw(gs, ws, ms, vs):
    n_dev = 8
    n_par = len(gs)
    wd = PACK_W
    chunks = []
    for p, g in enumerate(gs):
        for off in range(0, g.shape[1], wd):
            chunks.append((p, len(chunks), off, min(wd, g.shape[1] - off)))
    rows = _round_up(len(chunks), 8)

    def body(*refs):
        g_refs, w_refs, m_refs, v_refs = (refs[k * n_par:(k + 1) * n_par] for k in range(4))
        go_refs, d_refs, mo_refs, vo_refs = (refs[(4 + k) * n_par:(5 + k) * n_par] for k in range(4))
        mine_ref, all_ref, send_sems, recv_sems = refs[8 * n_par:]
        x, y, c = _place()
        me, sibling = (x, y, c), (x, y, 1 - c)
        chips = [(1 - x, y), (x, 1 - y), (1 - x, 1 - y)]

        def slot(px, py, pc):
            return all_ref.at[4 * px + 2 * py + pc]

        def copy(k, block, to, src=None):
            return pltpu.make_async_remote_copy(
                src_ref=slot(*block) if src is None else src, dst_ref=slot(*block),
                send_sem=send_sems.at[k], recv_sem=recv_sems.at[k], device_id=to, device_id_type=MESH)

        mine_ref[...] = jnp.zeros_like(mine_ref)
        for p, row, off, width in chunks:
            mine_ref[row:row + 1, 0:width] = g_refs[p][:, off:off + width]
        all_ref[4 * x + 2 * y + c] = mine_ref[...]
        first = [copy(0, me, sibling, src=mine_ref)]
        first += [copy(1 + j, me, (*chip, c), src=mine_ref) for j, chip in enumerate(chips)]
        for cp in first:
            cp.start()
        passed = [copy(4 + j, (*chip, c), sibling) for j, chip in enumerate(chips)]
        for j, chip in enumerate(chips):
            copy(1 + j, (*chip, c), me).wait_recv()
            passed[j].start()
        copy(0, sibling, me).wait_recv()
        for j, chip in enumerate(chips):
            copy(4 + j, (*chip, 1 - c), me).wait_recv()
        for cp in first + passed:
            cp.wait_send()
        tot = jnp.zeros((rows, wd), F32)
        for dev in range(n_dev):
            tot = tot + all_ref[dev]
        mine_ref[...] = tot
        for p, row, off, width in chunks:
            cols = slice(off, off + width)
            g = mine_ref[row:row + 1, 0:width]
            delta, m_new, v_new = _adamw(w_refs[p][:, cols], g, m_refs[p][:, cols], v_refs[p][:, cols])
            go_refs[p][:, cols] = g
            d_refs[p][:, cols] = delta
            mo_refs[p][:, cols] = m_new
            vo_refs[p][:, cols] = v_new

    vm = pl.BlockSpec(memory_space=pltpu.VMEM)
    shp = [jax.ShapeDtypeStruct(g.shape, F32) for g in gs]
    res = pl.pallas_call(
        body, out_shape=shp * 4, in_specs=[vm] * (4 * n_par), out_specs=[vm] * (4 * n_par),
        scratch_shapes=[pltpu.VMEM((rows, wd), F32), pltpu.VMEM((n_dev, rows, wd), F32),
                        pltpu.SemaphoreType.DMA((7,)), pltpu.SemaphoreType.DMA((7,))],
        name="small_allreduce_adamw")(*gs, *ws, *ms, *vs)
    return [res[k * n_par:(k + 1) * n_par] for k in range(4)]


def kernel(x, positions, ln_pre_mix, ln_post_mix, ln_pre_mlp, ln_post_mlp, w_in, b_in, q_a_norm, w_uq, kv_a_norm, w_uk, w_uv, w_o_mla, w_o_fox, w_out, w_ff1, w_ff2, loss_target, m_ln_pre_mix, m_ln_post_mix, m_ln_pre_mlp, m_ln_post_mlp, m_w_in, m_b_in, m_q_a_norm, m_w_uq, m_kv_a_norm, m_w_uk, m_w_uv, m_w_o_mla, m_w_o_fox, m_w_out, m_w_ff1, m_w_ff2, v_ln_pre_mix, v_ln_post_mix, v_ln_pre_mlp, v_ln_post_mlp, v_w_in, v_b_in, v_q_a_norm, v_w_uq, v_kv_a_norm, v_w_uk, v_w_uv, v_w_o_mla, v_w_o_fox, v_w_out, v_w_ff1, v_w_ff2):
    w = dict(ln_pre_mix=ln_pre_mix, ln_post_mix=ln_post_mix, ln_pre_mlp=ln_pre_mlp, ln_post_mlp=ln_post_mlp, w_in=w_in,
             b_in=b_in, q_a_norm=q_a_norm, w_uq=w_uq, kv_a_norm=kv_a_norm, w_uk=w_uk, w_uv=w_uv, w_o_mla=w_o_mla,
             w_o_fox=w_o_fox, w_out=w_out, w_ff1=w_ff1, w_ff2=w_ff2)
    mom = dict(ln_pre_mix=m_ln_pre_mix, ln_post_mix=m_ln_post_mix, ln_pre_mlp=m_ln_pre_mlp, ln_post_mlp=m_ln_post_mlp,
               w_in=m_w_in, b_in=m_b_in, q_a_norm=m_q_a_norm, w_uq=m_w_uq, kv_a_norm=m_kv_a_norm, w_uk=m_w_uk,
               w_uv=m_w_uv, w_o_mla=m_w_o_mla, w_o_fox=m_w_o_fox, w_out=m_w_out, w_ff1=m_w_ff1, w_ff2=m_w_ff2)
    var = dict(ln_pre_mix=v_ln_pre_mix, ln_post_mix=v_ln_post_mix, ln_pre_mlp=v_ln_pre_mlp, ln_post_mlp=v_ln_post_mlp,
               w_in=v_w_in, b_in=v_b_in, q_a_norm=v_q_a_norm, w_uq=v_w_uq, kv_a_norm=v_kv_a_norm, w_uk=v_w_uk,
               w_uv=v_w_uv, w_o_mla=v_w_o_mla, w_o_fox=v_w_o_fox, w_out=v_w_out, w_ff1=v_w_ff1, w_ff2=v_w_ff2)

    big_names = [nm for nm, _ in BIG]
    c = lax.axis_index("c")
    chip = 2 * lax.axis_index("x") + lax.axis_index("y")

    axes = dict(BIG)

    def assemble(names, lay, gathered):
        return {nm: _from_shards(s4, axes[nm] - 1) for nm, s4 in zip(names, lay.unpack(gathered), strict=True)}

    groups = {"mixer": [nm for nm in big_names if nm in MIXER], "mlp": [nm for nm in big_names if nm in LATE]}
    first_names = [nm for nm in big_names if nm not in MIXER and nm not in LATE]
    full = {nm: wv for nm, wv in w.items() if nm in SMALL}
    lay_first = _PackLayout([w[nm].shape[1:] for nm in first_names])
    first = _gather_weights(lay_first.pack([w[nm][0].astype(BF16) for nm in first_names]))
    full.update(assemble(first_names, lay_first, first))
    travelling = {}
    for tag, names in groups.items():
        lay = _PackLayout([w[nm].shape[1:] for nm in names])
        shard = lay.pack([w[nm][0].astype(BF16) for nm in names])
        started = _gather_late_start(lax.optimization_barrier((shard, first))[0], tag)
        travelling[tag] = (names, lay, shard, started[:4])
        full["b_in"] = full["b_in"] + started[4][0, 0]

    def late_weights(tag, after):
        names, lay, shard, handles = travelling[tag]
        land = _gather_late_wait(*handles, after, tag)
        land = lax.dynamic_update_slice(land, shard[None], (chip, 0, 0))
        return assemble(names, lay, land)

    grad_groups = {"early": [nm for nm in big_names if nm in EARLY_GRADS],
                   "late": [nm for nm in big_names if nm not in EARLY_GRADS]}
    sent = {}

    def send_grads(tag, g):
        names = grad_groups[tag]
        lay = _PackLayout([w[nm].shape[1:] for nm in names])
        hrows = lay.rows // 2
        gbf = lay.pack([_to_shards(g[nm], axes[nm] - 1).astype(BF16) for nm in names])
        gbf = gbf.reshape(N_CHIPS, 2, hrows, lay.width)
        started = _reduce_start(gbf, tag)
        own = lax.dynamic_index_in_dim(lax.dynamic_index_in_dim(started[2], chip, axis=0, keepdims=False), c, axis=0,
                                       keepdims=False)
        sent[tag] = (names, lay, started[:4], own)
        return started[4][0, 0]

    loss_local, grad_x, grads = _local_step(x[0], positions[0], loss_target[0], full, late_weights, send_grads)

    def f_add8(ti, pa):
        tot = ti[0].astype(F32)
        for part in ti[1:]:
            tot = tot + part.astype(F32)
        return [tot], []

    reduced = []
    for tag, (names, lay, handles, own) in sent.items():
        parts = _reduce_wait(*handles, grad_x, tag)
        reduced.append(_rowwise("grad_add_" + tag, f_add8, [own] + [parts[n] for n in range(N_PART)], [],
                                [(lay.width, F32)])[0])

    assert len({lay.width for _, lay, _, _ in sent.values()}) == 1
    red = jnp.concatenate(reduced, axis=0)
    sib = _sibling_swap(red)
    lower, upper = jnp.where(c == 0, red, sib), jnp.where(c == 0, sib, red)
    g_by_name, row = {}, 0
    for names, lay, _, _ in sent.values():
        hrows = lay.rows // 2
        both = jnp.concatenate([lower[row:row + hrows], upper[row:row + hrows]], axis=0)
        g_by_name.update(zip(names, lay.unpack(both), strict=True))
        row += hrows
    g_shards = [g_by_name[nm] for nm in big_names]

    def f_adamw(ti, pa):
        wv, gv, mv, vv = ti
        return list(_adamw(wv, gv, mv, vv)), []

    out = {"grad": {}, "delta": {}, "m": {}, "v": {}}
    for nm, g_sh in zip(big_names, g_shards, strict=True):
        wd = g_sh.shape[1]
        d_sh, m_sh, v_sh = _rowwise("adamw_" + nm, f_adamw, [w[nm][0], g_sh, mom[nm][0], var[nm][0]], [], [(wd, F32)] * 3)
        out["grad"][nm], out["delta"][nm], out["m"][nm], out["v"][nm] = g_sh[None], d_sh[None], m_sh[None], v_sh[None]

    loss_row = jnp.zeros((1, LANES), F32) + loss_local
    blank = jnp.zeros((1, LANES), F32)
    small = _small_allreduce_adamw([grads[nm] for nm in SMALL] + [loss_row], [w[nm] for nm in SMALL] + [blank],
                                   [mom[nm] for nm in SMALL] + [blank], [var[nm] for nm in SMALL] + [blank])
    for kind, arrs in zip(("grad", "delta", "m", "v"), small, strict=True):
        for nm, arr in zip(SMALL, arrs[:len(SMALL)], strict=True):
            out[kind][nm] = arr
    loss = small[0][len(SMALL)][0, 0]

    return (loss, grad_x[None], *[out["grad"][nm] for nm in ALL_W], *[out["delta"][nm] for nm in ALL_W],
            *[out["m"][nm] for nm in ALL_W], *[out["v"][nm] for nm in ALL_W])
```

```python
import functools
import math

import jax
import jax.numpy as jnp
from jax import lax
from jax.experimental import pallas as pl
from jax.experimental.pallas import tpu as pltpu

F32 = jnp.float32
BF16 = jnp.bfloat16

MLA_HEADS = 8
MLA_Q_LORA = 256
MLA_KV_LORA = 128
MLA_NOPE = 64
MLA_ROPE = 32
MLA_V = 64
FOX_HEADS = 8
FOX_DIM = 64
ROPE_THETA = 10000.0
NORM_EPS = 1e-6
HALF_ROPE = MLA_ROPE // 2

ADAM_LR = 0.001
ADAM_B1 = 0.9
ADAM_B2 = 0.999
ADAM_EPS = 1e-08
ADAM_WD = 0.01
ADAM_STEP = 10

LANES = 128
VMEM_LIMIT = 56 * 1024 * 1024
ATT_TILE = 1024
FWD_GROUP_LOG2 = 1
FWD_GROUP = 1 << FWD_GROUP_LOG2
MM_VMEM_BUDGET = 40 * 1024 * 1024
ROW_TILE_BYTES = 12 * 1024 * 1024
MXU_WIDTH = 256
MXU_MACS_PER_S = 4.98e14
HBM_BYTES_PER_S = 3.2e12
STEP_OVERHEAD_S = 0.35e-6
NEG = -1e30
LOG2E = math.log2(math.e)
MESH = pl.DeviceIdType.MESH

V_ONES = 64
FOX_Q_F = 64
FOX_Q_L = 67
FOX_Q_ONES = 70
MLA_Q_L = 96

BIG = (("w_in", 2), ("w_uq", 2), ("w_uk", 2), ("w_uv", 2), ("w_o_mla", 2), ("w_o_fox", 2),
       ("w_out", 1), ("w_ff1", 2), ("w_ff2", 1))
MIXER = ("w_uq", "w_uk", "w_uv", "w_o_mla", "w_o_fox")
LATE = ("w_out", "w_ff1", "w_ff2")
EARLY_GRADS = ("w_o_mla", "w_o_fox", "w_out", "w_ff1", "w_ff2")
SMALL = ("ln_pre_mix", "ln_post_mix", "ln_pre_mlp", "ln_post_mlp", "b_in", "q_a_norm", "kv_a_norm")
ALL_W = ("ln_pre_mix", "ln_post_mix", "ln_pre_mlp", "ln_post_mlp", "w_in", "b_in", "q_a_norm", "w_uq",
         "kv_a_norm", "w_uk", "w_uv", "w_o_mla", "w_o_fox", "w_out", "w_ff1", "w_ff2")
N_CHIPS = 4
PACK_W = 1024
LOCAL_PIECES = 8

_NT = (((1,), (1,)), ((), ()))
_TN = (((0,), (0,)), ((), ()))


def _cparams(sem=None):
    return pltpu.CompilerParams(dimension_semantics=sem, vmem_limit_bytes=VMEM_LIMIT)


def _divisor_tile(n, limit, mult):
    if n <= limit:
        return n
    best = None
    t = mult
    while t <= limit:
        if n % t == 0:
            best = t
        t += mult
    assert best is not None, (n, limit, mult)
    return best


def _round_up(v, mult):
    return -(-v // mult) * mult


def _mm_tiles(m, k, n, io_bytes):
    best = None
    for tm in (2048, 1024, 512, 256, 128):
        if m % tm:
            continue
        for tn in range(LANES, min(n, 2048) + 1, LANES):
            if n % tn:
                continue
            vmem = 2 * (tm * k * 2 + k * tn * 2 + tm * tn * io_bytes) + tm * tn * 4
            if vmem > MM_VMEM_BUDGET:
                continue
            mxu = m * k * n * (_round_up(tn, MXU_WIDTH) / tn) / MXU_MACS_PER_S
            hbm = (m * k * 2 + (m // tm) * k * n * 2 + m * n * io_bytes) / HBM_BYTES_PER_S
            cost = max(mxu, hbm) + (m // tm) * (n // tn) * STEP_OVERHEAD_S
            if best is None or cost < best[0]:
                best = (cost, tm, tn)
    assert best is not None, (m, k, n)
    return best[1], best[2]


def _mm(a, b, *, out_dtype, name, bias=None, transpose_b=False, extras=(), epilogue=None):
    m, k = a.shape
    n = b.shape[0] if transpose_b else b.shape[1]
    assert (b.shape[1] if transpose_b else b.shape[0]) == k and a.dtype == BF16 and b.dtype == BF16
    out_dtypes = list(out_dtype) if isinstance(out_dtype, (list, tuple)) else [out_dtype]
    n_ex = len(extras)
    tm, tn = _mm_tiles(m, k, n, sum(jnp.dtype(dt).itemsize for dt in out_dtypes) + 4 * n_ex)

    def body(*refs):
        a_ref, b_ref = refs[:2]
        pos = 2
        bias_ref = None
        if bias is not None:
            bias_ref = refs[pos]
            pos += 1
        ex_refs = refs[pos:pos + n_ex]
        o_refs = refs[pos + n_ex:]
        if transpose_b:
            acc = lax.dot_general(a_ref[...], b_ref[...], _NT, preferred_element_type=F32)
        else:
            acc = jnp.dot(a_ref[...], b_ref[...], preferred_element_type=F32)
        if bias_ref is not None:
            acc = acc + bias_ref[...]
        vals = [acc] if epilogue is None else epilogue(acc, [r[...] for r in ex_refs])
        for ref, val in zip(o_refs, vals, strict=True):
            ref[...] = val.astype(ref.dtype)

    b_spec = pl.BlockSpec((tn, k), lambda i, j: (j, 0)) if transpose_b else pl.BlockSpec((k, tn), lambda i, j: (0, j))
    in_specs = [pl.BlockSpec((tm, k), lambda i, j: (i, 0)), b_spec]
    args = [a, b]
    if bias is not None:
        in_specs.append(pl.BlockSpec((1, tn), lambda i, j: (0, j)))
        args.append(bias)
    in_specs += [pl.BlockSpec((tm, tn), lambda i, j: (i, j)) for _ in extras]
    args += list(extras)
    res = pl.pallas_call(
        body, grid=(m // tm, n // tn), in_specs=in_specs,
        out_specs=[pl.BlockSpec((tm, tn), lambda i, j: (i, j)) for _ in out_dtypes],
        out_shape=[jax.ShapeDtypeStruct((m, n), dt) for dt in out_dtypes],
        compiler_params=_cparams(("parallel", "parallel")), name=name)(*args)
    return res if isinstance(out_dtype, (list, tuple)) else res[0]


def _mm_tn(a, b, *, name):
    s, m = a.shape
    s2, n = b.shape
    assert s == s2 and a.dtype == BF16 and b.dtype == BF16
    tm = _divisor_tile(m, 1024, LANES)
    tn = _divisor_tile(n, 2304, LANES)
    tk = _divisor_tile(s, 1024, 16)
    nk = s // tk

    def body(a_ref, b_ref, o_ref, acc_ref):
        @pl.when(pl.program_id(2) == 0)
        def _():
            acc_ref[...] = jnp.zeros_like(acc_ref)

        acc_ref[...] += lax.dot_general(a_ref[...], b_ref[...], _TN, preferred_element_type=F32)

        @pl.when(pl.program_id(2) == nk - 1)
        def _():
            o_ref[...] = acc_ref[...].astype(o_ref.dtype)

    return pl.pallas_call(
        body, grid=(m // tm, n // tn, nk),
        in_specs=[pl.BlockSpec((tk, tm), lambda i, j, k: (k, i)), pl.BlockSpec((tk, tn), lambda i, j, k: (k, j))],
        out_specs=pl.BlockSpec((tm, tn), lambda i, j, k: (i, j)),
        out_shape=jax.ShapeDtypeStruct((m, n), BF16),
        scratch_shapes=[pltpu.VMEM((tm, tn), F32)],
        compiler_params=_cparams(("parallel", "parallel", "arbitrary")), name=name)(a, b)


def _rowwise(name, fn, tiled, params, outs, reds=(), reverse=False):
    wins = [t if isinstance(t, tuple) else (t, 0, t.shape[1]) for t in tiled]
    s = wins[0][0].shape[0]
    row_bytes = sum(w * arr.dtype.itemsize for arr, _, w in wins) + sum(w * jnp.dtype(d).itemsize for w, d in outs)
    ts = _divisor_tile(s, max(16, min(1024, ROW_TILE_BYTES // row_bytes)), 16)
    nt, npar, nout = len(wins), len(params), len(outs)
    n_tiles = s // ts

    def row(i):
        return n_tiles - 1 - i if reverse else i

    def body(*refs):
        tin = [r[...] for r in refs[:nt]]
        par = [r[...] for r in refs[nt:nt + npar]]
        out_refs = refs[nt + npar:nt + npar + nout]
        red_refs = refs[nt + npar + nout:]
        o, r = fn(tin, par)
        for ref, val in zip(out_refs, o, strict=True):
            ref[...] = val.astype(ref.dtype)
        if red_refs:
            @pl.when(pl.program_id(0) == 0)
            def _():
                for ref in red_refs:
                    ref[...] = jnp.zeros_like(ref)

            for ref, val in zip(red_refs, r, strict=True):
                ref[...] += val

    in_specs = [pl.BlockSpec((ts, w), functools.partial(lambda i, cb: (row(i), cb), cb=cb)) for _, cb, w in wins]
    in_specs += [pl.BlockSpec(p.shape, lambda i: (0, 0)) for p in params]
    out_specs = [pl.BlockSpec((ts, w), lambda i: (row(i), 0)) for w, _ in outs]
    out_specs += [pl.BlockSpec((1, w), lambda i: (0, 0)) for w in reds]
    out_shape = [jax.ShapeDtypeStruct((s, w), d) for w, d in outs]
    out_shape += [jax.ShapeDtypeStruct((1, w), F32) for w in reds]
    return pl.pallas_call(
        body, grid=(n_tiles,), in_specs=in_specs, out_specs=out_specs, out_shape=out_shape,
        compiler_params=_cparams(("arbitrary",)), name=name)(*[w[0] for w in wins], *params)


def _rms(x, g):
    r = lax.rsqrt(jnp.mean(x * x, axis=-1, keepdims=True) + NORM_EPS)
    return x * r * g, r


def _rms_bwd(x, g, dy):
    r = lax.rsqrt(jnp.mean(x * x, axis=-1, keepdims=True) + NORM_EPS)
    gy = dy * g
    dx = r * gy - x * (r * r * r) * jnp.mean(x * gy, axis=-1, keepdims=True)
    dg = jnp.sum(dy * (x * r), axis=0, keepdims=True)
    return dx, dg


def _sigmoid(x):
    return 1.0 / (1.0 + jnp.exp(-x))


def _split3(x):
    hi = x.astype(BF16).astype(F32)
    r = x - hi
    mid = r.astype(BF16).astype(F32)
    lo = (r - mid).astype(BF16).astype(F32)
    return hi, mid, lo


def _lane(shape):
    return lax.broadcasted_iota(jnp.int32, shape, 1)


def _put3(blk, lane, pos, pieces):
    for k, piece in enumerate(pieces):
        blk = jnp.where(lane == pos + k, piece, blk)
    return blk


def _lane_column(blk, lane, pos):
    return jnp.sum(jnp.where(lane == pos, blk, 0.0), axis=1, keepdims=True)


def _blocks(a, nh):
    return [a[:, h * LANES:(h + 1) * LANES] for h in range(nh)]


def _head_spread(nh, d):
    shift = d.bit_length() - 1
    assert 1 << shift == d
    r = lax.broadcasted_iota(jnp.int32, (nh * d, nh * LANES), 0)
    c = lax.broadcasted_iota(jnp.int32, (nh * d, nh * LANES), 1)
    return jnp.where(c == (r >> shift) * LANES + (r & (d - 1)), 1.0, 0.0).astype(BF16)


def _rope_block(x, c, sa, sb):
    return x * c + pltpu.roll(x, LANES - HALF_ROPE, 1) * sa + pltpu.roll(x, HALF_ROPE, 1) * sb


def _forget_cumsum(z, cb):
    s = z.shape[0]
    ts = _divisor_tile(s, 512, LANES)

    def body(x_ref, col_ref, carry):
        @pl.when(pl.program_id(0) == 0)
        def _():
            carry[...] = jnp.zeros_like(carry)

        x = x_ref[...]
        lf = jnp.minimum(x, 0.0) - jnp.log1p(jnp.exp(-jnp.abs(x)))
        r = lax.broadcasted_iota(jnp.int32, (ts, ts), 0)
        c = lax.broadcasted_iota(jnp.int32, (ts, ts), 1)
        tri = jnp.where(c <= r, 1.0, 0.0).astype(F32)
        col_ref[...] = jnp.dot(tri, lf, preferred_element_type=F32, precision=lax.Precision.HIGHEST) + carry[...]
        carry[...] += jnp.sum(lf, axis=0, keepdims=True)

    return pl.pallas_call(
        body, grid=(s // ts,),
        in_specs=[pl.BlockSpec((ts, LANES), lambda i: (i, cb))],
        out_specs=pl.BlockSpec((ts, LANES), lambda i: (i, 0)),
        out_shape=jax.ShapeDtypeStruct((s, LANES), F32),
        scratch_shapes=[pltpu.VMEM((1, LANES), F32)],
        compiler_params=_cparams(("arbitrary",)), name="forget_cumsum")(z)


def _forget_grad(dq_blocks, dk_blocks, lane):
    df = jnp.zeros(dq_blocks[0].shape, F32)
    for h, (dq_blk, dk_blk) in enumerate(zip(dq_blocks, dk_blocks, strict=True)):
        d_h = _lane_column(dq_blk, lane, FOX_Q_F) - _lane_column(dk_blk, lane, FOX_Q_ONES)
        df = jnp.where(lane == h, d_h, df)
    return df


def _forget_cumsum_bwd(df, z, cb):
    s = z.shape[0]
    ts = _divisor_tile(s, 512, LANES)
    nt = s // ts

    def body(df_ref, x_ref, o_ref, carry):
        @pl.when(pl.program_id(0) == 0)
        def _():
            carry[...] = jnp.zeros_like(carry)

        df = df_ref[...]
        r = lax.broadcasted_iota(jnp.int32, (ts, ts), 0)
        c = lax.broadcasted_iota(jnp.int32, (ts, ts), 1)
        tri = jnp.where(c >= r, 1.0, 0.0).astype(F32)
        rc = jnp.dot(tri, df, preferred_element_type=F32, precision=lax.Precision.HIGHEST) + carry[...]
        carry[...] += jnp.sum(df, axis=0, keepdims=True)
        o_ref[...] = (rc * (1.0 / (1.0 + jnp.exp(x_ref[...])))).astype(o_ref.dtype)

    return pl.pallas_call(
        body, grid=(nt,),
        in_specs=[pl.BlockSpec((ts, LANES), lambda i: (nt - 1 - i, 0)),
                  pl.BlockSpec((ts, LANES), lambda i: (nt - 1 - i, cb))],
        out_specs=pl.BlockSpec((ts, LANES), lambda i: (nt - 1 - i, 0)),
        out_shape=jax.ShapeDtypeStruct((s, LANES), BF16),
        scratch_shapes=[pltpu.VMEM((1, LANES), F32)],
        compiler_params=_cparams(("arbitrary",)), name="forget_cumsum_bwd")(df, z)


def _flash_fwd(q, k, v, scale, nh, l_lane, *, name):
    s = q.shape[0]
    t = min(ATT_TILE, s)
    half = t // 2 if t % (2 * LANES) == 0 else t
    nq = s // t
    c = scale * LOG2E

    def body(q_ref, k_ref, v_ref, o_ref, qb_ref):
        i = pl.program_id(1)
        qb = q_ref[...]

        def scores(q_rows, k0, nk):
            kb = k_ref[pl.ds(pl.multiple_of(k0, half), nk), :]
            return lax.dot_general(q_rows, kb, _NT, preferred_element_type=F32)

        def update(sc, k0, nk, carry):
            m, acc = carry
            m_new = jnp.maximum(m, jnp.max(sc, axis=1, keepdims=True))
            p = jnp.exp2((sc - m_new) * c)
            alpha = jnp.exp2((m - m_new) * c)
            vb = v_ref[pl.ds(pl.multiple_of(k0, half), nk), :]
            acc = alpha * acc + jnp.dot(p.astype(BF16), vb, preferred_element_type=F32)
            return m_new, acc

        def full_block(j, cr):
            return update(scores(qb, j * t, t), j * t, t, cr)

        def group(jj, cr):
            for n in range(FWD_GROUP):
                cr = full_block(FWD_GROUP * jj + n, cr)
            return cr

        def causal(sc):
            row = lax.broadcasted_iota(jnp.int32, sc.shape, 0)
            col = lax.broadcasted_iota(jnp.int32, sc.shape, 1)
            return jnp.where(col <= row, sc, NEG)

        init = (jnp.full((t, 1), NEG, F32), jnp.zeros((t, LANES), F32))
        n_groups = i >> FWD_GROUP_LOG2
        carry = lax.fori_loop(0, n_groups, group, init)
        carry = lax.fori_loop(n_groups * FWD_GROUP, i, full_block, carry)
        m, acc = update(causal(scores(qb, i * t, t)), i * t, t, carry)
        lane = _lane((t, LANES))
        l = _lane_column(acc, lane, V_ONES)
        o_ref[...] = (acc / l).astype(o_ref.dtype)
        big_l = m + jnp.log(l) / scale
        qb_ref[...] = _put3(qb.astype(F32), lane, l_lane, _split3(-big_l)).astype(qb_ref.dtype)

    head_rows = pl.BlockSpec((t, LANES), lambda h, i: (i, h))
    head_all = pl.BlockSpec((s, LANES), lambda h, i: (0, h))
    return pl.pallas_call(
        body, grid=(nh, nq), in_specs=[head_rows, head_all, head_all], out_specs=[head_rows, head_rows],
        out_shape=[jax.ShapeDtypeStruct(q.shape, BF16), jax.ShapeDtypeStruct(q.shape, BF16)],
        compiler_params=_cparams(("parallel", "arbitrary")), name=name)(q, k, v)


def _flash_bwd(qb, k, v, do, scale, nh, n_feat, *, name):
    s = qb.shape[0]
    t = min(ATT_TILE, s)
    half = t // 2 if t % (2 * LANES) == 0 else t
    nq = s // t
    c = scale * LOG2E

    def body(q_ref, k_ref, v_ref, do_ref, dk_ref, dv_ref, dq_ref):
        j = pl.program_id(1)
        kb = k_ref[...]
        vb = v_ref[...]

        @pl.when(j == 0)
        def _():
            dq_ref[...] = jnp.zeros_like(dq_ref)

        def part(q0, n_q, n_k, carry, q_off):
            dk_acc, dv_acc = carry
            rows = pl.ds(pl.multiple_of(q0, half), n_q)
            qblk = q_ref[rows, :]
            dob = do_ref[rows, :]
            kbb, vbb = kb[:n_k], vb[:n_k]
            st = lax.dot_general(kbb, qblk, _NT, preferred_element_type=F32)
            if q_off is not None:
                key = lax.broadcasted_iota(jnp.int32, st.shape, 0)
                qry = lax.broadcasted_iota(jnp.int32, st.shape, 1) + q_off
                st = jnp.where(key <= qry, st, NEG)
            pt = jnp.exp2(st * c)
            dv_new = jnp.dot(pt.astype(BF16), dob, preferred_element_type=F32)
            dpt = lax.dot_general(vbb, dob, _NT, preferred_element_type=F32)
            dsb = (pt * dpt).astype(BF16)
            dk_new = jnp.dot(dsb, qblk, preferred_element_type=F32)
            dq_ref[rows, :] += lax.dot_general(dsb, kbb, _TN, preferred_element_type=F32)
            if n_k == t:
                return dk_acc + dk_new, dv_acc + dv_new
            return (jnp.concatenate([dk_acc[:n_k] + dk_new, dk_acc[n_k:]], axis=0),
                    jnp.concatenate([dv_acc[:n_k] + dv_new, dv_acc[n_k:]], axis=0))

        def block(i, carry):
            return part(i * t, t, t, carry, None)

        init = (jnp.zeros((t, LANES), F32), jnp.zeros((t, LANES), F32))
        carry = part(j * t + half, t - half, t, init, half) if half < t else part(j * t, t, t, init, 0)
        if half < t:
            carry = part(j * t, half, half, carry, 0)
        rest = nq - 1 - j
        carry = lax.cond((rest & 1) == 1, lambda cr: block(j + 1, cr), lambda cr: cr, carry)
        first = j + 1 + (rest & 1)

        def pair(ii, cr):
            i0 = first + 2 * ii
            return block(i0 + 1, block(i0, cr))

        dk_acc, dv_acc = lax.fori_loop(0, rest >> 1, pair, carry)
        dk_ref[...] = dk_acc * jnp.where(_lane((t, LANES)) < n_feat, scale, 1.0)
        dv_ref[...] = dv_acc.astype(dv_ref.dtype)

        @pl.when(j == nq - 1)
        def _():
            dq_ref[...] = dq_ref[...] * jnp.where(_lane((s, LANES)) < n_feat, scale, 1.0)

    head_rows = pl.BlockSpec((t, LANES), lambda h, j: (j, h))
    head_all = pl.BlockSpec((s, LANES), lambda h, j: (0, h))
    shp = jax.ShapeDtypeStruct(qb.shape, F32)
    return pl.pallas_call(
        body, grid=(nh, nq), in_specs=[head_all, head_rows, head_rows, head_all],
        out_specs=[head_rows, head_rows, head_all],
        out_shape=[shp, jax.ShapeDtypeStruct(qb.shape, BF16), shp],
        compiler_params=_cparams(("parallel", "arbitrary")), name=name)(qb, k, v, do)


def _with_delta(do, ex):
    n_blocks = do.shape[1] // LANES
    lane = _lane((do.shape[0], LANES))
    out = []
    for d_blk, o_blk in zip(_blocks(do, n_blocks), _blocks(ex[0], n_blocks), strict=True):
        delta = jnp.sum(d_blk * o_blk.astype(F32), axis=1, keepdims=True)
        out.append(_put3(d_blk, lane, V_ONES, _split3(-delta)))
    return [jnp.concatenate(out, axis=1)]


def _pad_heads(a, nh):
    d = a.shape[-1] // nh
    a = a.reshape(a.shape[:-1] + (nh, d))
    a = jnp.pad(a, [(0, 0)] * (a.ndim - 1) + [(0, LANES - d)])
    return a.reshape(a.shape[:-2] + (nh * LANES,))


def _unpad_heads(a, nh, d):
    a = a.reshape(a.shape[:-1] + (nh, LANES))[..., :d]
    return a.reshape(a.shape[:-2] + (nh * d,))


def _pad_head_rows(w, nh):
    return _pad_heads(w.T, nh).T


def _unpad_head_rows(g, nh, d):
    return _unpad_heads(g.T, nh, d).T


class _ZLayout:
    def __init__(self, d):
        fw = FOX_HEADS * FOX_DIM
        self.src = {}
        off = 0
        for nm, w in (("cq", MLA_Q_LORA), ("ckv", MLA_KV_LORA), ("kr", MLA_ROPE), ("fq", fw), ("fk", fw),
                      ("fv", fw), ("fl", FOX_HEADS), ("ga", d), ("gb", d)):
            self.src[nm] = (off, w)
            off += w
        self.dst = {}
        off = 0
        for nm, w in (("ga", d), ("gb", d), ("fq", fw), ("fk", fw), ("fv", fw), ("cq", MLA_Q_LORA),
                      ("ckv", MLA_KV_LORA), ("kr", LANES), ("fl", LANES)):
            assert off % w == 0
            self.dst[nm] = (off, w)
            off += w
        self.width = off
        self.split = self.dst["cq"][0]
        assert all((o - self.split) % w == 0 for o, w in self.dst.values() if o >= self.split)

    def to_kernel(self, w):
        def seg(nm):
            off, wd = self.src[nm]
            return w[..., off:off + wd]

        def pad(a, left, total):
            return jnp.pad(a, [(0, 0)] * (a.ndim - 1) + [(left, total - left - a.shape[-1])])

        return jnp.concatenate([seg("ga"), seg("gb"), seg("fq"), seg("fk"), seg("fv"), seg("cq"), seg("ckv"),
                                pad(seg("kr"), MLA_NOPE, LANES), pad(seg("fl"), 0, LANES)], axis=-1)

    def from_kernel(self, g):
        def seg(nm, lo=0, hi=None):
            off, wd = self.dst[nm]
            return g[..., off + lo:off + (wd if hi is None else hi)]

        return jnp.concatenate([seg("cq"), seg("ckv"), seg("kr", MLA_NOPE, MLA_NOPE + MLA_ROPE), seg("fq"), seg("fk"),
                                seg("fv"), seg("fl", 0, FOX_HEADS), seg("ga"), seg("gb")], axis=-1)


def _local_step(x, positions, target, wts, late_weights, send_grads):
    s, d = x.shape
    zl = _ZLayout(d)
    hw = MLA_HEADS * LANES
    assert MLA_HEADS == FOX_HEADS
    scale_mla = (MLA_NOPE + MLA_ROPE) ** -0.5
    scale_fox = FOX_DIM ** -0.5

    inv_freq = ROPE_THETA ** (-jnp.arange(HALF_ROPE, dtype=F32) / HALF_ROPE)
    ang = positions.astype(F32)[:, None] * inv_freq
    cos, sin = jnp.cos(ang), jnp.sin(ang)
    tail = jnp.zeros((s, LANES - MLA_NOPE - MLA_ROPE), F32)
    rc = jnp.concatenate([jnp.ones((s, MLA_NOPE), F32), cos, cos, tail], axis=1)
    ra = jnp.concatenate([jnp.zeros((s, MLA_NOPE), F32), -sin, jnp.zeros((s, HALF_ROPE), F32), tail], axis=1)
    rb = jnp.concatenate([jnp.zeros((s, MLA_NOPE + HALF_ROPE), F32), sin, tail], axis=1)

    w_in = zl.to_kernel(wts["w_in"])
    b_in = zl.to_kernel(wts["b_in"])

    def f_norm_in(ti, pa):
        y, _ = _rms(ti[0], pa[0])
        return [y], []

    (h,) = _rowwise("norm_in", f_norm_in, [x], [wts["ln_pre_mix"]], [(d, BF16)])
    z_lo = _mm(h, w_in[:, :zl.split], bias=b_in[:, :zl.split], out_dtype=BF16, name="proj_in_lo")
    z = _mm(h, w_in[:, zl.split:], bias=b_in[:, zl.split:], out_dtype=F32, name="proj_in_hi")

    def zwin(nm):
        off, wd = zl.dst[nm]
        return (z_lo, off // wd, wd) if off < zl.split else (z, (off - zl.split) // wd, wd)

    wts = {**wts, **late_weights("mixer", z)}
    w_uq = _pad_heads(wts["w_uq"], MLA_HEADS)
    w_ukv = jnp.concatenate([_pad_heads(wts["w_uk"], MLA_HEADS), _pad_heads(wts["w_uv"], MLA_HEADS)], axis=1)
    w_o_mla = _pad_head_rows(wts["w_o_mla"], MLA_HEADS)
    w_o_fox = _pad_head_rows(wts["w_o_fox"], FOX_HEADS)

    def f_mla_norms(ti, pa):
        cqn, _ = _rms(ti[0], pa[0])
        ckvn, _ = _rms(ti[1], pa[1])
        return [cqn, ckvn], []

    cqn, ckvn = _rowwise("mla_norms", f_mla_norms, [zwin("cq"), zwin("ckv")],
                         [wts["q_a_norm"], wts["kv_a_norm"]], [(MLA_Q_LORA, BF16), (MLA_KV_LORA, BF16)])
    qf = _mm(cqn, w_uq, out_dtype=F32, name="proj_uq")
    kv = _mm(ckvn, w_ukv, out_dtype=BF16, name="proj_ukv")

    def f_rope_q(ti, pa):
        xq, c_, a_, b_ = ti
        return [jnp.concatenate([_rope_block(blk, c_, a_, b_) for blk in _blocks(xq, MLA_HEADS)], axis=1)], []

    (q_mla,) = _rowwise("rope_q", f_rope_q, [qf, rc, ra, rb], [], [(hw, BF16)])

    def f_mla_kv(ti, pa):
        kn, vn, kr, c_, a_, b_ = ti
        lane = _lane(kr.shape)
        k_tail = jnp.where((lane >= MLA_Q_L) & (lane < MLA_Q_L + 3), 1.0, _rope_block(kr, c_, a_, b_))
        ones_v = (lane >= V_ONES) & (lane < V_ONES + 3)
        k_out = [jnp.where(lane < MLA_NOPE, blk.astype(F32), k_tail) for blk in _blocks(kn, MLA_HEADS)]
        v_out = [jnp.where(ones_v, 1.0, blk.astype(F32)) for blk in _blocks(vn, MLA_HEADS)]
        return [jnp.concatenate(k_out, axis=1), jnp.concatenate(v_out, axis=1)], []

    k_mla, v_mla = _rowwise("mla_kv", f_mla_kv, [(kv, 0, hw), (kv, 1, hw), zwin("kr"), rc, ra, rb], [],
                            [(hw, BF16), (hw, BF16)])
    o_mla, qb_mla = _flash_fwd(q_mla, k_mla, v_mla, scale_mla, MLA_HEADS, MLA_Q_L, name="mla_fwd")

    fl_cb = zwin("fl")[1]
    fcol = _forget_cumsum(z, fl_cb)

    def f_fox_qkv(ti, pa):
        spread = _head_spread(FOX_HEADS, FOX_DIM)
        fq, fk, fv = (jnp.dot(a, spread, preferred_element_type=F32) for a in ti[:3])
        fc = ti[3]
        lane = _lane(fc.shape)
        ones_q = (lane >= FOX_Q_ONES) & (lane < FOX_Q_ONES + 3)
        ones_k = (lane >= FOX_Q_F) & (lane < FOX_Q_ONES)
        ones_v = (lane >= V_ONES) & (lane < V_ONES + 3)
        q_out, k_out, v_out = [], [], []
        for hh, (qblk, kblk, vblk) in enumerate(zip(_blocks(fq, FOX_HEADS), _blocks(fk, FOX_HEADS),
                                                    _blocks(fv, FOX_HEADS), strict=True)):
            f_h = _lane_column(fc, lane, hh) * (1.0 / scale_fox)
            q_out.append(_put3(jnp.where(ones_q, 1.0, qblk), lane, FOX_Q_F, _split3(f_h)))
            k_out.append(_put3(jnp.where(ones_k, 1.0, kblk), lane, FOX_Q_ONES, _split3(-f_h)))
            v_out.append(jnp.where(ones_v, 1.0, vblk))
        return [jnp.concatenate(q_out, axis=1), jnp.concatenate(k_out, axis=1), jnp.concatenate(v_out, axis=1)], []

    q_fox, k_fox, v_fox = _rowwise("fox_qkv", f_fox_qkv, [zwin("fq"), zwin("fk"), zwin("fv"), fcol],
                                   [], [(hw, BF16)] * 3)
    o_fox, qb_fox = _flash_fwd(q_fox, k_fox, v_fox, scale_fox, FOX_HEADS, FOX_Q_L, name="fox_fwd")

    y_mla = _mm(o_mla, w_o_mla, out_dtype=BF16, name="proj_o_mla")
    y_fox = _mm(o_fox, w_o_fox, out_dtype=BF16, name="proj_o_fox")

    def f_gate(ti, pa):
        ga, gb, ya, yb = (a.astype(F32) for a in ti)
        return [_sigmoid(ga) * ya + _sigmoid(gb) * yb], []

    (merged,) = _rowwise("gate", f_gate, [zwin("ga"), zwin("gb"), y_mla, y_fox], [], [(d, BF16)])
    wts = {**wts, **late_weights("mlp", merged)}
    mix = _mm(merged, wts["w_out"], out_dtype=F32, name="proj_out")

    def f_resid1(ti, pa):
        xa, mx = ti
        y, _ = _rms(mx, pa[0])
        x1 = xa + y
        h2, _ = _rms(x1, pa[1])
        return [x1, h2], []

    x1, h2 = _rowwise("resid_mix", f_resid1, [x, mix], [wts["ln_post_mix"], wts["ln_pre_mlp"]], [(d, F32), (d, BF16)])

    def relu2(acc, ex):
        r = jnp.maximum(acc, 0.0)
        return [acc, r * r]

    u, act = _mm(h2, wts["w_ff1"], out_dtype=[BF16, BF16], name="ff1", epilogue=relu2)
    mo = _mm(act, wts["w_ff2"], out_dtype=F32, name="ff2")

    def f_loss(ti, pa):
        xa, mv, tg = ti
        y, _ = _rms(mv, pa[0])
        err = (xa + y) - tg
        g2 = err / d
        dmo, dg = _rms_bwd(mv, pa[0], g2)
        return [g2, dmo], [jnp.sum(err * err, axis=0, keepdims=True), dg]

    g2, d_mo, loss_cols, g_ln_post_mlp = _rowwise("loss", f_loss, [x1, mo, target], [wts["ln_post_mlp"]],
                                                  [(d, F32), (d, BF16)], [d, d])
    loss = 0.5 * jnp.sum(loss_cols) / d

    grads = {"ln_post_mlp": g_ln_post_mlp}
    grads["w_ff2"] = _mm_tn(act, d_mo, name="grad_ff2")

    def relu2_bwd(acc, ex):
        return [acc * (2.0 * jnp.maximum(ex[0], 0.0))]

    (d_u,) = _mm(d_mo, wts["w_ff2"], out_dtype=[BF16], name="ff2_bwd", transpose_b=True, extras=[u], epilogue=relu2_bwd)
    grads["w_ff1"] = _mm_tn(h2, d_u, name="grad_ff1")
    d_h2 = _mm(d_u, wts["w_ff1"], out_dtype=F32, name="ff1_bwd", transpose_b=True)

    def f_resid1_bwd(ti, pa):
        gres, dh2, x1v, mx = ti
        dx1n, dg_pre_mlp = _rms_bwd(x1v, pa[1], dh2)
        dx1 = gres + dx1n
        dmix, dg_post_mix = _rms_bwd(mx, pa[0], dx1)
        return [dx1, dmix], [dg_post_mix, dg_pre_mlp]

    d_x1, d_mix, grads["ln_post_mix"], grads["ln_pre_mlp"] = _rowwise(
        "resid_mix_bwd", f_resid1_bwd, [g2, d_h2, x1, mix], [wts["ln_post_mix"], wts["ln_pre_mlp"]],
        [(d, F32), (d, BF16)], [d, d])
    grads["w_out"] = _mm_tn(merged, d_mix, name="grad_out")
    d_merged = _mm(d_mix, wts["w_out"], out_dtype=BF16, name="proj_out_bwd", transpose_b=True)

    def f_gate_bwd(ti, pa):
        dm, ga, gb, ya, yb = (a.astype(F32) for a in ti)
        sa, sb = _sigmoid(ga), _sigmoid(gb)
        return [dm * sa, dm * sb, dm * ya * (sa * (1.0 - sa)), dm * yb * (sb * (1.0 - sb))], []

    d_ya, d_yb, d_ga, d_gb = _rowwise("gate_bwd", f_gate_bwd,
                                      [d_merged, zwin("ga"), zwin("gb"), y_mla, y_fox], [],
                                      [(d, BF16)] * 4)
    grads["w_o_mla"] = _unpad_head_rows(_mm_tn(o_mla, d_ya, name="grad_o_mla"), MLA_HEADS, MLA_V)
    grads["w_o_fox"] = _unpad_head_rows(_mm_tn(o_fox, d_yb, name="grad_o_fox"), FOX_HEADS, FOX_DIM)
    w_o_mla = w_o_mla + send_grads("early", {nm: grads[nm] for nm in EARLY_GRADS}).astype(BF16)
    (do_mla,) = _mm(d_ya, w_o_mla, out_dtype=[BF16], name="proj_o_mla_bwd", transpose_b=True, extras=[o_mla],
                    epilogue=_with_delta)
    (do_fox,) = _mm(d_yb, w_o_fox, out_dtype=[BF16], name="proj_o_fox_bwd", transpose_b=True, extras=[o_fox],
                    epilogue=_with_delta)

    dk_mla, dv_mla, dq_mla = _flash_bwd(qb_mla, k_mla, v_mla, do_mla, scale_mla, MLA_HEADS, MLA_NOPE + MLA_ROPE,
                                        name="mla_bwd")
    dk_fox, dv_fox, dq_fox = _flash_bwd(qb_fox, k_fox, v_fox, do_fox, scale_fox, FOX_HEADS, FOX_DIM, name="fox_bwd")

    def f_rope_q_bwd(ti, pa):
        g, c_, a_, b_ = ti
        return [jnp.concatenate([_rope_block(blk, c_, -a_, -b_) for blk in _blocks(g, MLA_HEADS)], axis=1)], []

    (d_qf,) = _rowwise("rope_q_bwd", f_rope_q_bwd, [dq_mla, rc, ra, rb], [], [(hw, BF16)])
    grads["w_uq"] = _unpad_heads(_mm_tn(cqn, d_qf, name="grad_uq"), MLA_HEADS, MLA_NOPE + MLA_ROPE)
    d_cqn = _mm(d_qf, w_uq, out_dtype=F32, name="proj_uq_bwd", transpose_b=True)

    def f_mla_kv_bwd(ti, pa):
        gk, gv, c_, a_, b_ = ti
        k_blocks = _blocks(gk, MLA_HEADS)
        tot = k_blocks[0]
        for blk in k_blocks[1:]:
            tot = tot + blk
        return [jnp.concatenate([gk, gv], axis=1), _rope_block(tot, c_, -a_, -b_)], []

    d_kv, d_kr = _rowwise("mla_kv_bwd", f_mla_kv_bwd, [dk_mla, dv_mla, rc, ra, rb], [], [(2 * hw, BF16), (LANES, BF16)])
    g_ukv = _mm_tn(ckvn, d_kv, name="grad_ukv")
    grads["w_uk"] = _unpad_heads(g_ukv[:, :hw], MLA_HEADS, MLA_NOPE)
    grads["w_uv"] = _unpad_heads(g_ukv[:, hw:], MLA_HEADS, MLA_V)
    d_ckvn = _mm(d_kv, w_ukv, out_dtype=F32, name="proj_ukv_bwd", transpose_b=True)

    def f_mla_norms_bwd(ti, pa):
        cq, ckv, dcqn, dckvn = ti
        dcq, dg_q = _rms_bwd(cq, pa[0], dcqn)
        dckv, dg_kv = _rms_bwd(ckv, pa[1], dckvn)
        return [dcq, dckv], [dg_q, dg_kv]

    d_cq, d_ckv, grads["q_a_norm"], grads["kv_a_norm"] = _rowwise(
        "mla_norms_bwd", f_mla_norms_bwd, [zwin("cq"), zwin("ckv"), d_cqn, d_ckvn],
        [wts["q_a_norm"], wts["kv_a_norm"]], [(MLA_Q_LORA, BF16), (MLA_KV_LORA, BF16)], [MLA_Q_LORA, MLA_KV_LORA])

    def f_fox_compact(ti, pa):
        gather = _head_spread(FOX_HEADS, FOX_DIM)
        df = _forget_grad(_blocks(ti[0], FOX_HEADS), _blocks(ti[1], FOX_HEADS), _lane((ti[0].shape[0], LANES)))
        return [lax.dot_general(a.astype(BF16), gather, _NT, preferred_element_type=F32) for a in ti] + [df], []

    d_fq, d_fk, d_fv, d_fcum = _rowwise("fox_compact", f_fox_compact, [dq_fox, dk_fox, dv_fox], [],
                                        [(FOX_HEADS * FOX_DIM, BF16)] * 3 + [(LANES, F32)])
    d_fl = _forget_cumsum_bwd(d_fcum, z, fl_cb)
    d_z = jnp.concatenate([d_ga, d_gb, d_fq, d_fk, d_fv, d_cq, d_ckv, d_kr, d_fl], axis=1)
    assert d_z.shape[1] == zl.width

    def f_bias(ti, pa):
        return [], [jnp.sum(ti[0].astype(F32), axis=0, keepdims=True)]

    (g_b_in,) = _rowwise("grad_b_in", f_bias, [d_z], [], [], [zl.width])
    grads["b_in"] = zl.from_kernel(g_b_in)
    grads["w_in"] = zl.from_kernel(_mm_tn(h, d_z, name="grad_in"))
    tok = send_grads("late", {nm: grads[nm] for nm, _ in BIG if nm not in EARLY_GRADS})
    d_h = _mm(d_z, w_in, bias=jnp.zeros((1, d), F32) + tok, out_dtype=F32, name="proj_in_bwd", transpose_b=True)

    def f_norm_in_bwd(ti, pa):
        dx1v, dh, xa = ti
        dxn, dg = _rms_bwd(xa, pa[0], dh)
        return [dx1v + dxn], [dg]

    grad_x, grads["ln_pre_mix"] = _rowwise("norm_in_bwd", f_norm_in_bwd, [d_x1, d_h, x], [wts["ln_pre_mix"]],
                                           [(d, F32)], [d])
    return loss, grad_x, grads


class _PackLayout:
    def __init__(self, shapes):
        self.shapes = list(shapes)
        self.width = _round_up(max(b for _, b in shapes), LANES)
        self.bands = []
        row = 0
        shelf = []
        for idx, (a, b) in enumerate(shapes):
            if 2 * _round_up(b, LANES) > self.width:
                self.bands.append((row, _round_up(a, 32), [(idx, 0)]))
                row += _round_up(a, 32)
            else:
                shelf.append(idx)
        col, items = 0, []
        for idx in shelf:
            wb = _round_up(shapes[idx][1], LANES)
            if col + wb > self.width:
                hgt = max(_round_up(shapes[i][0], 32) for i, _ in items)
                self.bands.append((row, hgt, items))
                row += hgt
                col, items = 0, []
            items.append((idx, col))
            col += wb
        if items:
            hgt = max(_round_up(shapes[i][0], 32) for i, _ in items)
            self.bands.append((row, hgt, items))
            row += hgt
        self.rows = _round_up(row, 16 * LOCAL_PIECES)

    def pack(self, arrs):
        lead = arrs[0].shape[:-2]
        no_pad = [(0, 0)] * len(lead)
        bands = []
        for _, hgt, items in self.bands:
            parts = []
            for k, (idx, col) in enumerate(items):
                a, b = self.shapes[idx]
                nxt = items[k + 1][1] if k + 1 < len(items) else self.width
                parts.append(jnp.pad(arrs[idx], no_pad + [(0, hgt - a), (0, nxt - col - b)]))
            bands.append(parts[0] if len(parts) == 1 else jnp.concatenate(parts, axis=-1))
        used = sum(hgt for _, hgt, _ in self.bands)
        if used < self.rows:
            bands.append(jnp.zeros(lead + (self.rows - used, self.width), arrs[0].dtype))
        return jnp.concatenate(bands, axis=-2)

    def unpack(self, packed):
        out = [None] * len(self.shapes)
        for row, _, items in self.bands:
            for idx, col in items:
                a, b = self.shapes[idx]
                out[idx] = packed[..., row:row + a, col:col + b]
        return out


def _to_shards(g, axis):
    if axis == 0:
        return g.reshape(N_CHIPS, g.shape[0] // N_CHIPS, g.shape[1])
    return jnp.stack(jnp.split(g, N_CHIPS, axis=1))


def _from_shards(s4, axis):
    n, a, b = s4.shape
    if axis == 0:
        return s4.reshape(n * a, b)
    return jnp.concatenate([s4[ch] for ch in range(n)], axis=1)


ANY = pl.BlockSpec(memory_space=pl.ANY)


def _place():
    return lax.axis_index("x"), lax.axis_index("y"), lax.axis_index("c")


def _gather_weights(wpk):
    rows, wd = wpk.shape
    half = rows // 2

    def body(w_ref, out_ref, send_sems, recv_sems, local_sems):
        x, y, c = _place()
        sibling = (x, y, 1 - c)
        chips = [(1 - x, y), (x, 1 - y), (1 - x, 1 - y)]

        def slab(chip, hf):
            return out_ref.at[2 * chip[0] + chip[1], pl.ds(hf * half, half), :]

        def copy(k, chip, hf, to, src=None):
            return pltpu.make_async_remote_copy(
                src_ref=slab(chip, hf) if src is None else src, dst_ref=slab(chip, hf),
                send_sem=send_sems.at[k], recv_sem=recv_sems.at[k], device_id=to, device_id_type=MESH)

        piece = rows // LOCAL_PIECES
        mine = [pltpu.make_async_copy(w_ref.at[pl.ds(n * piece, piece), :],
                                      out_ref.at[2 * x + y, pl.ds(n * piece, piece), :], local_sems.at[n])
                for n in range(LOCAL_PIECES)]
        for cp in mine:
            cp.start()
        first = [copy(j, (x, y), c, (*chip, c), src=w_ref.at[pl.ds(c * half, half), :]) for j, chip in enumerate(chips)]
        for cp in first:
            cp.start()
        passed = [copy(3 + j, chip, c, sibling) for j, chip in enumerate(chips)]
        for j, chip in enumerate(chips):
            copy(j, chip, c, (x, y, c)).wait_recv()
            passed[j].start()
        for j, chip in enumerate(chips):
            copy(3 + j, chip, 1 - c, (x, y, c)).wait_recv()
        for cp in first + passed:
            cp.wait_send()
        for cp in mine:
            cp.wait()

    assert rows % (16 * LOCAL_PIECES) == 0
    return pl.pallas_call(
        body, out_shape=jax.ShapeDtypeStruct((N_CHIPS, rows, wd), wpk.dtype),
        in_specs=[ANY], out_specs=ANY,
        scratch_shapes=[pltpu.SemaphoreType.DMA((6,)), pltpu.SemaphoreType.DMA((6,)),
                        pltpu.SemaphoreType.DMA((LOCAL_PIECES,))],
        name="gather_weights")(wpk)


HBM = pl.BlockSpec(memory_space=pltpu.HBM)
SEM = pl.BlockSpec(memory_space=pltpu.SEMAPHORE)
EFFECT = pltpu.SideEffectType.DATAFLOW_SIDE_EFFECTING
N_LATE = 6


def _gather_late_start(wpk, tag):
    rows, wd = wpk.shape
    half = rows // 2

    def body(w_ref, land_ref, send_sems, recv_sems, w_thru, land_thru, token):
        x, y, c = _place()
        chips = [(1 - x, y), (x, 1 - y), (1 - x, 1 - y)]
        for j, chip in enumerate(chips):
            for to_core in range(2):
                pltpu.make_async_remote_copy(
                    src_ref=w_ref.at[pl.ds(c * half, half), :],
                    dst_ref=land_ref.at[2 * x + y, pl.ds(c * half, half), :],
                    send_sem=send_sems.at[2 * j + to_core], recv_sem=recv_sems.at[2 * j + c],
                    device_id=(*chip, to_core), device_id_type=MESH).start()
        token[...] = jnp.zeros_like(token)

    land = lax.empty((N_CHIPS, rows, wd), wpk.dtype)
    return pl.pallas_call(
        body, name="gather_" + tag + "_start",
        out_shape=(pltpu.SemaphoreType.DMA((N_LATE,)), pltpu.SemaphoreType.DMA((N_LATE,)),
                   pltpu.HBM(wpk.shape, wpk.dtype), pltpu.HBM(land.shape, land.dtype),
                   jax.ShapeDtypeStruct((8, LANES), F32)),
        in_specs=(HBM, HBM), out_specs=(SEM, SEM, HBM, HBM, pl.BlockSpec(memory_space=pltpu.VMEM)),
        input_output_aliases={0: 2, 1: 3},
        compiler_params=pltpu.CompilerParams(has_side_effects=EFFECT),
    )(pltpu.with_memory_space_constraint(wpk, pltpu.HBM), pltpu.with_memory_space_constraint(land, pltpu.HBM))


def _gather_late_wait(send_sems, recv_sems, w_thru, land_thru, after, tag):
    rows, wd = w_thru.shape
    half = rows // 2

    def body(w_ref, land_ref, send_sems, recv_sems, after_ref, w_dead, land_out):
        x, y, c = _place()
        for n in range(N_LATE):
            cp = pltpu.make_async_remote_copy(
                src_ref=w_ref.at[pl.ds(0, half), :], dst_ref=land_ref.at[0, pl.ds(0, half), :],
                send_sem=send_sems.at[n], recv_sem=recv_sems.at[n], device_id=(x, y, c), device_id_type=MESH)
            cp.wait_send()
            cp.wait_recv()

    return pl.pallas_call(
        body, name="gather_" + tag + "_wait",
        out_shape=(pltpu.HBM(w_thru.shape, w_thru.dtype), pltpu.HBM(land_thru.shape, land_thru.dtype)),
        in_specs=(HBM, HBM, SEM, SEM, ANY), out_specs=(HBM, HBM), input_output_aliases={0: 0, 1: 1},
        compiler_params=pltpu.CompilerParams(has_side_effects=EFFECT),
    )(w_thru, land_thru, send_sems, recv_sems, after)[1]


N_PART = 7


def _reduce_start(gbf, tag):
    _, _, hrows, wd = gbf.shape

    def body(g_ref, land_ref, send_sems, recv_sems, g_thru, land_thru, token):
        x, y, c = _place()
        chips = [(1 - x, y), (x, 1 - y), (1 - x, 1 - y)]
        for j, chip in enumerate(chips):
            for to_core in range(2):
                pltpu.make_async_remote_copy(
                    src_ref=g_ref.at[2 * chip[0] + chip[1], to_core], dst_ref=land_ref.at[2 * j + c],
                    send_sem=send_sems.at[2 * j + to_core], recv_sem=recv_sems.at[2 * j + c],
                    device_id=(*chip, to_core), device_id_type=MESH).start()
        pltpu.make_async_remote_copy(
            src_ref=g_ref.at[2 * x + y, 1 - c], dst_ref=land_ref.at[N_PART - 1],
            send_sem=send_sems.at[N_PART - 1], recv_sem=recv_sems.at[N_PART - 1],
            device_id=(x, y, 1 - c), device_id_type=MESH).start()
        token[...] = jnp.zeros_like(token)

    land = lax.empty((N_PART, hrows, wd), gbf.dtype)
    return pl.pallas_call(
        body, name="reduce_" + tag + "_start",
        out_shape=(pltpu.SemaphoreType.DMA((N_PART,)), pltpu.SemaphoreType.DMA((N_PART,)),
                   pltpu.HBM(gbf.shape, gbf.dtype), pltpu.HBM(land.shape, land.dtype),
                   jax.ShapeDtypeStruct((8, LANES), F32)),
        in_specs=(HBM, HBM), out_specs=(SEM, SEM, HBM, HBM, pl.BlockSpec(memory_space=pltpu.VMEM)),
        input_output_aliases={0: 2, 1: 3},
        compiler_params=pltpu.CompilerParams(has_side_effects=EFFECT),
    )(pltpu.with_memory_space_constraint(gbf, pltpu.HBM), pltpu.with_memory_space_constraint(land, pltpu.HBM))


def _reduce_wait(send_sems, recv_sems, g_thru, land_thru, after, tag):
    def body(g_ref, land_ref, send_sems, recv_sems, after_ref, g_dead, land_out):
        x, y, c = _place()
        for n in range(N_PART):
            cp = pltpu.make_async_remote_copy(
                src_ref=g_ref.at[0, 0], dst_ref=land_ref.at[0], send_sem=send_sems.at[n], recv_sem=recv_sems.at[n],
                device_id=(x, y, c), device_id_type=MESH)
            cp.wait_send()
            cp.wait_recv()

    return pl.pallas_call(
        body, name="reduce_" + tag + "_wait",
        out_shape=(pltpu.HBM(g_thru.shape, g_thru.dtype), pltpu.HBM(land_thru.shape, land_thru.dtype)),
        in_specs=(HBM, HBM, SEM, SEM, ANY), out_specs=(HBM, HBM), input_output_aliases={0: 0, 1: 1},
        compiler_params=pltpu.CompilerParams(has_side_effects=EFFECT),
    )(g_thru, land_thru, send_sems, recv_sems, after)[1]


def _sibling_swap(mine):
    def body(m_ref, out_ref, send_sem, recv_sem):
        x, y, c = _place()
        cp = pltpu.make_async_remote_copy(
            src_ref=m_ref, dst_ref=out_ref, send_sem=send_sem, recv_sem=recv_sem,
            device_id=(x, y, 1 - c), device_id_type=MESH)
        cp.start()
        cp.wait()

    return pl.pallas_call(
        body, out_shape=jax.ShapeDtypeStruct(mine.shape, mine.dtype), in_specs=[ANY], out_specs=ANY,
        scratch_shapes=[pltpu.SemaphoreType.DMA, pltpu.SemaphoreType.DMA], name="grad_sibling_swap")(mine)


def _adamw(w, g, m, v):
    m = ADAM_B1 * m + (1.0 - ADAM_B1) * g
    v = ADAM_B2 * v + (1.0 - ADAM_B2) * (g * g)
    m_hat = m / (1.0 - ADAM_B1 ** ADAM_STEP)
    v_hat = v / (1.0 - ADAM_B2 ** ADAM_STEP)
    delta = -ADAM_LR * (m_hat / (jnp.sqrt(v_hat) + ADAM_EPS) + ADAM_WD * w)
    return delta, m, v


def _small_allreduce_adamw(gs, ws, ms, vs):
    n_dev = 8
    n_par = len(gs)
    wd = PACK_W
    chunks = []
    for p, g in enumerate(gs):
        for off in range(0, g.shape[1], wd):
            chunks.append((p, len(chunks), off, min(wd, g.shape[1] - off)))
    rows = _round_up(len(chunks), 8)

    def body(*refs):
        g_refs, w_refs, m_refs, v_refs = (refs[k * n_par:(k + 1) * n_par] for k in range(4))
        go_refs, d_refs, mo_refs, vo_refs = (refs[(4 + k) * n_par:(5 + k) * n_par] for k in range(4))
        mine_ref, all_ref, send_sems, recv_sems = refs[8 * n_par:]
        x, y, c = _place()
        me, sibling = (x, y, c), (x, y, 1 - c)
        chips = [(1 - x, y), (x, 1 - y), (1 - x, 1 - y)]

        def slot(px, py, pc):
            return all_ref.at[4 * px + 2 * py + pc]

        def copy(k, block, to, src=None):
            return pltpu.make_async_remote_copy(
                src_ref=slot(*block) if src is None else src, dst_ref=slot(*block),
                send_sem=send_sems.at[k], recv_sem=recv_sems.at[k], device_id=to, device_id_type=MESH)

        mine_ref[...] = jnp.zeros_like(mine_ref)
        for p, row, off, width in chunks:
            mine_ref[row:row + 1, 0:width] = g_refs[p][:, off:off + width]
        all_ref[4 * x + 2 * y + c] = mine_ref[...]
        first = [copy(0, me, sibling, src=mine_ref)]
        first += [copy(1 + j, me, (*chip, c), src=mine_ref) for j, chip in enumerate(chips)]
        for cp in first:
            cp.start()
        passed = [copy(4 + j, (*chip, c), sibling) for j, chip in enumerate(chips)]
        for j, chip in enumerate(chips):
            copy(1 + j, (*chip, c), me).wait_recv()
            passed[j].start()
        copy(0, sibling, me).wait_recv()
        for j, chip in enumerate(chips):
            copy(4 + j, (*chip, 1 - c), me).wait_recv()
        for cp in first + passed:
            cp.wait_send()
        tot = jnp.zeros((rows, wd), F32)
        for dev in range(n_dev):
            tot = tot + all_ref[dev]
        mine_ref[...] = tot
        for p, row, off, width in chunks:
            cols = slice(off, off + width)
            g = mine_ref[row:row + 1, 0:width]
            delta, m_new, v_new = _adamw(w_refs[p][:, cols], g, m_refs[p][:, cols], v_refs[p][:, cols])
            go_refs[p][:, cols] = g
            d_refs[p][:, cols] = delta
            mo_refs[p][:, cols] = m_new
            vo_refs[p][:, cols] = v_new

    vm = pl.BlockSpec(memory_space=pltpu.VMEM)
    shp = [jax.ShapeDtypeStruct(g.shape, F32) for g in gs]
    res = pl.pallas_call(
        body, out_shape=shp * 4, in_specs=[vm] * (4 * n_par), out_specs=[vm] * (4 * n_par),
        scratch_shapes=[pltpu.VMEM((rows, wd), F32), pltpu.VMEM((n_dev, rows, wd), F32),
                        pltpu.SemaphoreType.DMA((7,)), pltpu.SemaphoreType.DMA((7,))],
        name="small_allreduce_adamw")(*gs, *ws, *ms, *vs)
    return [res[k * n_par:(k + 1) * n_par] for k in range(4)]


def kernel(x, positions, ln_pre_mix, ln_post_mix, ln_pre_mlp, ln_post_mlp, w_in, b_in, q_a_norm, w_uq, kv_a_norm, w_uk, w_uv, w_o_mla, w_o_fox, w_out, w_ff1, w_ff2, loss_target, m_ln_pre_mix, m_ln_post_mix, m_ln_pre_mlp, m_ln_post_mlp, m_w_in, m_b_in, m_q_a_norm, m_w_uq, m_kv_a_norm, m_w_uk, m_w_uv, m_w_o_mla, m_w_o_fox, m_w_out, m_w_ff1, m_w_ff2, v_ln_pre_mix, v_ln_post_mix, v_ln_pre_mlp, v_ln_post_mlp, v_w_in, v_b_in, v_q_a_norm, v_w_uq, v_kv_a_norm, v_w_uk, v_w_uv, v_w_o_mla, v_w_o_fox, v_w_out, v_w_ff1, v_w_ff2):
    w = dict(ln_pre_mix=ln_pre_mix, ln_post_mix=ln_post_mix, ln_pre_mlp=ln_pre_mlp, ln_post_mlp=ln_post_mlp, w_in=w_in,
             b_in=b_in, q_a_norm=q_a_norm, w_uq=w_uq, kv_a_norm=kv_a_norm, w_uk=w_uk, w_uv=w_uv, w_o_mla=w_o_mla,
             w_o_fox=w_o_fox, w_out=w_out, w_ff1=w_ff1, w_ff2=w_ff2)
    mom = dict(ln_pre_mix=m_ln_pre_mix, ln_post_mix=m_ln_post_mix, ln_pre_mlp=m_ln_pre_mlp, ln_post_mlp=m_ln_post_mlp,
               w_in=m_w_in, b_in=m_b_in, q_a_norm=m_q_a_norm, w_uq=m_w_uq, kv_a_norm=m_kv_a_norm, w_uk=m_w_uk,
               w_uv=m_w_uv, w_o_mla=m_w_o_mla, w_o_fox=m_w_o_fox, w_out=m_w_out, w_ff1=m_w_ff1, w_ff2=m_w_ff2)
    var = dict(ln_pre_mix=v_ln_pre_mix, ln_post_mix=v_ln_post_mix, ln_pre_mlp=v_ln_pre_mlp, ln_post_mlp=v_ln_post_mlp,
               w_in=v_w_in, b_in=v_b_in, q_a_norm=v_q_a_norm, w_uq=v_w_uq, kv_a_norm=v_kv_a_norm, w_uk=v_w_uk,
               w_uv=v_w_uv, w_o_mla=v_w_o_mla, w_o_fox=v_w_o_fox, w_out=v_w_out, w_ff1=v_w_ff1, w_ff2=v_w_ff2)

    big_names = [nm for nm, _ in BIG]
    c = lax.axis_index("c")
    chip = 2 * lax.axis_index("x") + lax.axis_index("y")

    axes = dict(BIG)

    def assemble(names, lay, gathered):
        return {nm: _from_shards(s4, axes[nm] - 1) for nm, s4 in zip(names, lay.unpack(gathered), strict=True)}

    groups = {"mixer": [nm for nm in big_names if nm in MIXER], "mlp": [nm for nm in big_names if nm in LATE]}
    first_names = [nm for nm in big_names if nm not in MIXER and nm not in LATE]
    full = {nm: wv for nm, wv in w.items() if nm in SMALL}
    lay_first = _PackLayout([w[nm].shape[1:] for nm in first_names])
    first = _gather_weights(lay_first.pack([w[nm][0].astype(BF16) for nm in first_names]))
    full.update(assemble(first_names, lay_first, first))
    travelling = {}
    for tag, names in groups.items():
        lay = _PackLayout([w[nm].shape[1:] for nm in names])
        shard = lay.pack([w[nm][0].astype(BF16) for nm in names])
        started = _gather_late_start(lax.optimization_barrier((shard, first))[0], tag)
        travelling[tag] = (names, lay, shard, started[:4])
        full["b_in"] = full["b_in"] + started[4][0, 0]

    def late_weights(tag, after):
        names, lay, shard, handles = travelling[tag]
        land = _gather_late_wait(*handles, after, tag)
        land = lax.dynamic_update_slice(land, shard[None], (chip, 0, 0))
        return assemble(names, lay, land)

    grad_groups = {"early": [nm for nm in big_names if nm in EARLY_GRADS],
                   "late": [nm for nm in big_names if nm not in EARLY_GRADS]}
    sent = {}

    def send_grads(tag, g):
        names = grad_groups[tag]
        lay = _PackLayout([w[nm].shape[1:] for nm in names])
        hrows = lay.rows // 2
        gbf = lay.pack([_to_shards(g[nm], axes[nm] - 1).astype(BF16) for nm in names])
        gbf = gbf.reshape(N_CHIPS, 2, hrows, lay.width)
        started = _reduce_start(gbf, tag)
        own = lax.dynamic_index_in_dim(lax.dynamic_index_in_dim(started[2], chip, axis=0, keepdims=False), c, axis=0,
                                       keepdims=False)
        sent[tag] = (names, lay, started[:4], own)
        return started[4][0, 0]

    loss_local, grad_x, grads = _local_step(x[0], positions[0], loss_target[0], full, late_weights, send_grads)

    def f_add8(ti, pa):
        tot = ti[0].astype(F32)
        for part in ti[1:]:
            tot = tot + part.astype(F32)
        return [tot], []

    reduced = []
    for tag, (names, lay, handles, own) in sent.items():
        parts = _reduce_wait(*handles, grad_x, tag)
        reduced.append(_rowwise("grad_add_" + tag, f_add8, [own] + [parts[n] for n in range(N_PART)], [],
                                [(lay.width, F32)])[0])

    assert len({lay.width for _, lay, _, _ in sent.values()}) == 1
    red = jnp.concatenate(reduced, axis=0)
    sib = _sibling_swap(red)
    lower, upper = jnp.where(c == 0, red, sib), jnp.where(c == 0, sib, red)
    g_by_name, row = {}, 0
    for names, lay, _, _ in sent.values():
        hrows = lay.rows // 2
        both = jnp.concatenate([lower[row:row + hrows], upper[row:row + hrows]], axis=0)
        g_by_name.update(zip(names, lay.unpack(both), strict=True))
        row += hrows
    g_shards = [g_by_name[nm] for nm in big_names]

    def f_adamw(ti, pa):
        wv, gv, mv, vv = ti
        return list(_adamw(wv, gv, mv, vv)), []

    out = {"grad": {}, "delta": {}, "m": {}, "v": {}}
    for nm, g_sh in zip(big_names, g_shards, strict=True):
        wd = g_sh.shape[1]
        d_sh, m_sh, v_sh = _rowwise("adamw_" + nm, f_adamw, [w[nm][0], g_sh, mom[nm][0], var[nm][0]], [], [(wd, F32)] * 3)
        out["grad"][nm], out["delta"][nm], out["m"][nm], out["v"][nm] = g_sh[None], d_sh[None], m_sh[None], v_sh[None]

    loss_row = jnp.zeros((1, LANES), F32) + loss_local
    blank = jnp.zeros((1, LANES), F32)
    small = _small_allreduce_adamw([grads[nm] for nm in SMALL] + [loss_row], [w[nm] for nm in SMALL] + [blank],
                                   [mom[nm] for nm in SMALL] + [blank], [var[nm] for nm in SMALL] + [blank])
    for kind, arrs in zip(("grad", "delta", "m", "v"), small, strict=True):
        for nm, arr in zip(SMALL, arrs[:len(SMALL)], strict=True):
            out[kind][nm] = arr
    loss = small[0][len(SMALL)][0, 0]

    return (loss, grad_x[None], *[out["grad"][nm] for nm in ALL_W], *[out["delta"][nm] for nm in ALL_W],
            *[out["m"][nm] for nm in ALL_W], *[out["v"][nm] for nm in ALL_W])
```

```python
import functools
import math

import jax
import jax.numpy as jnp
from jax import lax
from jax.experimental import pallas as pl
from jax.experimental.pallas import tpu as pltpu

F32 = jnp.float32
BF16 = jnp.bfloat16

MLA_HEADS = 8
MLA_Q_LORA = 256
MLA_KV_LORA = 128
MLA_NOPE = 64
MLA_ROPE = 32
MLA_V = 64
FOX_HEADS = 8
FOX_DIM = 64
ROPE_THETA = 10000.0
NORM_EPS = 1e-6
HALF_ROPE = MLA_ROPE // 2

ADAM_LR = 0.001
ADAM_B1 = 0.9
ADAM_B2 = 0.999
ADAM_EPS = 1e-08
ADAM_WD = 0.01
ADAM_STEP = 10

LANES = 128
VMEM_LIMIT = 56 * 1024 * 1024
ATT_TILE = 1024
FWD_GROUP_LOG2 = 1
FWD_GROUP = 1 << FWD_GROUP_LOG2
MM_VMEM_BUDGET = 40 * 1024 * 1024
ROW_TILE_BYTES = 12 * 1024 * 1024
MXU_WIDTH = 256
MXU_MACS_PER_S = 4.98e14
HBM_BYTES_PER_S = 3.2e12
STEP_OVERHEAD_S = 0.35e-6
NEG = -1e30
LOG2E = math.log2(math.e)
MESH = pl.DeviceIdType.MESH

V_ONES = 64
FOX_Q_F = 64
FOX_Q_L = 67
FOX_Q_ONES = 70
MLA_Q_L = 96

BIG = (("w_in", 2), ("w_uq", 2), ("w_uk", 2), ("w_uv", 2), ("w_o_mla", 2), ("w_o_fox", 2),
       ("w_out", 1), ("w_ff1", 2), ("w_ff2", 1))
MIXER = ("w_uq", "w_uk", "w_uv", "w_o_mla", "w_o_fox")
LATE = ("w_out", "w_ff1", "w_ff2")
EARLY_GRADS = ("w_o_mla", "w_o_fox", "w_out", "w_ff1", "w_ff2")
SMALL = ("ln_pre_mix", "ln_post_mix", "ln_pre_mlp", "ln_post_mlp", "b_in", "q_a_norm", "kv_a_norm")
ALL_W = ("ln_pre_mix", "ln_post_mix", "ln_pre_mlp", "ln_post_mlp", "w_in", "b_in", "q_a_norm", "w_uq",
         "kv_a_norm", "w_uk", "w_uv", "w_o_mla", "w_o_fox", "w_out", "w_ff1", "w_ff2")
N_CHIPS = 4
PACK_W = 1024
LOCAL_PIECES = 8

_NT = (((1,), (1,)), ((), ()))
_TN = (((0,), (0,)), ((), ()))


def _cparams(sem=None):
    return pltpu.CompilerParams(dimension_semantics=sem, vmem_limit_bytes=VMEM_LIMIT)


def _divisor_tile(n, limit, mult):
    if n <= limit:
        return n
    best = None
    t = mult
    while t <= limit:
        if n % t == 0:
            best = t
        t += mult
    assert best is not None, (n, limit, mult)
    return best


def _round_up(v, mult):
    return -(-v // mult) * mult


def _mm_tiles(m, k, n, io_bytes):
    best = None
    for tm in (2048, 1024, 512, 256, 128):
        if m % tm:
            continue
        for tn in range(LANES, min(n, 2048) + 1, LANES):
            if n % tn:
                continue
            vmem = 2 * (tm * k * 2 + k * tn * 2 + tm * tn * io_bytes) + tm * tn * 4
            if vmem > MM_VMEM_BUDGET:
                continue
            mxu = m * k * n * (_round_up(tn, MXU_WIDTH) / tn) / MXU_MACS_PER_S
            hbm = (m * k * 2 + (m // tm) * k * n * 2 + m * n * io_bytes) / HBM_BYTES_PER_S
            cost = max(mxu, hbm) + (m // tm) * (n // tn) * STEP_OVERHEAD_S
            if best is None or cost < best[0]:
                best = (cost, tm, tn)
    assert best is not None, (m, k, n)
    return best[1], best[2]


def _mm(a, b, *, out_dtype, name, bias=None, transpose_b=False, extras=(), epilogue=None):
    m, k = a.shape
    n = b.shape[0] if transpose_b else b.shape[1]
    assert (b.shape[1] if transpose_b else b.shape[0]) == k and a.dtype == BF16 and b.dtype == BF16
    out_dtypes = list(out_dtype) if isinstance(out_dtype, (list, tuple)) else [out_dtype]
    n_ex = len(extras)
    tm, tn = _mm_tiles(m, k, n, sum(jnp.dtype(dt).itemsize for dt in out_dtypes) + 4 * n_ex)

    def body(*refs):
        a_ref, b_ref = refs[:2]
        pos = 2
        bias_ref = None
        if bias is not None:
            bias_ref = refs[pos]
            pos += 1
        ex_refs = refs[pos:pos + n_ex]
        o_refs = refs[pos + n_ex:]
        if transpose_b:
            acc = lax.dot_general(a_ref[...], b_ref[...], _NT, preferred_element_type=F32)
        else:
            acc = jnp.dot(a_ref[...], b_ref[...], preferred_element_type=F32)
        if bias_ref is not None:
            acc = acc + bias_ref[...]
        vals = [acc] if epilogue is None else epilogue(acc, [r[...] for r in ex_refs])
        for ref, val in zip(o_refs, vals, strict=True):
            ref[...] = val.astype(ref.dtype)

    b_spec = pl.BlockSpec((tn, k), lambda i, j: (j, 0)) if transpose_b else pl.BlockSpec((k, tn), lambda i, j: (0, j))
    in_specs = [pl.BlockSpec((tm, k), lambda i, j: (i, 0)), b_spec]
    args = [a, b]
    if bias is not None:
        in_specs.append(pl.BlockSpec((1, tn), lambda i, j: (0, j)))
        args.append(bias)
    for ex in extras:
        arr, col0 = ex if isinstance(ex, tuple) else (ex, 0)
        assert col0 % tn == 0
        in_specs.append(pl.BlockSpec((tm, tn), functools.partial(lambda i, j, off: (i, j + off), off=col0 // tn)))
        args.append(arr)
    res = pl.pallas_call(
        body, grid=(m // tm, n // tn), in_specs=in_specs,
        out_specs=[pl.BlockSpec((tm, tn), lambda i, j: (i, j)) for _ in out_dtypes],
        out_shape=[jax.ShapeDtypeStruct((m, n), dt) for dt in out_dtypes],
        compiler_params=_cparams(("parallel", "parallel")), name=name)(*args)
    return res if isinstance(out_dtype, (list, tuple)) else res[0]


def _mm_tn(a, b, *, name):
    s, m = a.shape
    s2, n = b.shape
    assert s == s2 and a.dtype == BF16 and b.dtype == BF16
    tm = _divisor_tile(m, 1024, LANES)
    tn = _divisor_tile(n, 2304, LANES)
    tk = _divisor_tile(s, 1024, 16)
    nk = s // tk

    def body(a_ref, b_ref, o_ref, acc_ref):
        @pl.when(pl.program_id(2) == 0)
        def _():
            acc_ref[...] = jnp.zeros_like(acc_ref)

        acc_ref[...] += lax.dot_general(a_ref[...], b_ref[...], _TN, preferred_element_type=F32)

        @pl.when(pl.program_id(2) == nk - 1)
        def _():
            o_ref[...] = acc_ref[...].astype(o_ref.dtype)

    return pl.pallas_call(
        body, grid=(m // tm, n // tn, nk),
        in_specs=[pl.BlockSpec((tk, tm), lambda i, j, k: (k, i)), pl.BlockSpec((tk, tn), lambda i, j, k: (k, j))],
        out_specs=pl.BlockSpec((tm, tn), lambda i, j, k: (i, j)),
        out_shape=jax.ShapeDtypeStruct((m, n), BF16),
        scratch_shapes=[pltpu.VMEM((tm, tn), F32)],
        compiler_params=_cparams(("parallel", "parallel", "arbitrary")), name=name)(a, b)


def _rowwise(name, fn, tiled, params, outs, reds=(), reverse=False):
    wins = [t if isinstance(t, tuple) else (t, 0, t.shape[1]) for t in tiled]
    s = wins[0][0].shape[0]
    row_bytes = sum(w * arr.dtype.itemsize for arr, _, w in wins) + sum(w * jnp.dtype(d).itemsize for w, d in outs)
    ts = _divisor_tile(s, max(16, min(1024, ROW_TILE_BYTES // row_bytes)), 16)
    nt, npar, nout = len(wins), len(params), len(outs)
    n_tiles = s // ts

    def row(i):
        return n_tiles - 1 - i if reverse else i

    def body(*refs):
        tin = [r[...] for r in refs[:nt]]
        par = [r[...] for r in refs[nt:nt + npar]]
        out_refs = refs[nt + npar:nt + npar + nout]
        red_refs = refs[nt + npar + nout:]
        o, r = fn(tin, par)
        for ref, val in zip(out_refs, o, strict=True):
            ref[...] = val.astype(ref.dtype)
        if red_refs:
            @pl.when(pl.program_id(0) == 0)
            def _():
                for ref in red_refs:
                    ref[...] = jnp.zeros_like(ref)

            for ref, val in zip(red_refs, r, strict=True):
                ref[...] += val

    in_specs = [pl.BlockSpec((ts, w), functools.partial(lambda i, cb: (row(i), cb), cb=cb)) for _, cb, w in wins]
    in_specs += [pl.BlockSpec(p.shape, lambda i: (0, 0)) for p in params]
    out_specs = [pl.BlockSpec((ts, w), lambda i: (row(i), 0)) for w, _ in outs]
    out_specs += [pl.BlockSpec((1, w), lambda i: (0, 0)) for w in reds]
    out_shape = [jax.ShapeDtypeStruct((s, w), d) for w, d in outs]
    out_shape += [jax.ShapeDtypeStruct((1, w), F32) for w in reds]
    return pl.pallas_call(
        body, grid=(n_tiles,), in_specs=in_specs, out_specs=out_specs, out_shape=out_shape,
        compiler_params=_cparams(("arbitrary",)), name=name)(*[w[0] for w in wins], *params)


def _rms(x, g):
    r = lax.rsqrt(jnp.mean(x * x, axis=-1, keepdims=True) + NORM_EPS)
    return x * r * g, r


def _rms_bwd(x, g, dy):
    r = lax.rsqrt(jnp.mean(x * x, axis=-1, keepdims=True) + NORM_EPS)
    gy = dy * g
    dx = r * gy - x * (r * r * r) * jnp.mean(x * gy, axis=-1, keepdims=True)
    dg = jnp.sum(dy * (x * r), axis=0, keepdims=True)
    return dx, dg


def _sigmoid(x):
    return 1.0 / (1.0 + jnp.exp(-x))


def _split3(x):
    hi = x.astype(BF16).astype(F32)
    r = x - hi
    mid = r.astype(BF16).astype(F32)
    lo = (r - mid).astype(BF16).astype(F32)
    return hi, mid, lo


def _lane(shape):
    return lax.broadcasted_iota(jnp.int32, shape, 1)


def _put3(blk, lane, pos, pieces):
    for k, piece in enumerate(pieces):
        blk = jnp.where(lane == pos + k, piece, blk)
    return blk


def _lane_column(blk, lane, pos):
    return jnp.sum(jnp.where(lane == pos, blk, 0.0), axis=1, keepdims=True)


def _blocks(a, nh):
    return [a[:, h * LANES:(h + 1) * LANES] for h in range(nh)]


def _head_spread(nh, d):
    shift = d.bit_length() - 1
    assert 1 << shift == d
    r = lax.broadcasted_iota(jnp.int32, (nh * d, nh * LANES), 0)
    c = lax.broadcasted_iota(jnp.int32, (nh * d, nh * LANES), 1)
    return jnp.where(c == (r >> shift) * LANES + (r & (d - 1)), 1.0, 0.0).astype(BF16)


def _rope_block(x, c, sa, sb):
    return x * c + pltpu.roll(x, LANES - HALF_ROPE, 1) * sa + pltpu.roll(x, HALF_ROPE, 1) * sb


def _forget_cumsum(z, cb):
    s = z.shape[0]
    ts = _divisor_tile(s, 512, LANES)

    def body(x_ref, col_ref, carry):
        @pl.when(pl.program_id(0) == 0)
        def _():
            carry[...] = jnp.zeros_like(carry)

        x = x_ref[...]
        lf = jnp.minimum(x, 0.0) - jnp.log1p(jnp.exp(-jnp.abs(x)))
        r = lax.broadcasted_iota(jnp.int32, (ts, ts), 0)
        c = lax.broadcasted_iota(jnp.int32, (ts, ts), 1)
        tri = jnp.where(c <= r, 1.0, 0.0).astype(F32)
        col_ref[...] = jnp.dot(tri, lf, preferred_element_type=F32, precision=lax.Precision.HIGHEST) + carry[...]
        carry[...] += jnp.sum(lf, axis=0, keepdims=True)

    return pl.pallas_call(
        body, grid=(s // ts,),
        in_specs=[pl.BlockSpec((ts, LANES), lambda i: (i, cb))],
        out_specs=pl.BlockSpec((ts, LANES), lambda i: (i, 0)),
        out_shape=jax.ShapeDtypeStruct((s, LANES), F32),
        scratch_shapes=[pltpu.VMEM((1, LANES), F32)],
        compiler_params=_cparams(("arbitrary",)), name="forget_cumsum")(z)


def _forget_grad(dq_blocks, dk_blocks, lane):
    df = jnp.zeros(dq_blocks[0].shape, F32)
    for h, (dq_blk, dk_blk) in enumerate(zip(dq_blocks, dk_blocks, strict=True)):
        d_h = _lane_column(dq_blk, lane, FOX_Q_F) - _lane_column(dk_blk, lane, FOX_Q_ONES)
        df = jnp.where(lane == h, d_h, df)
    return df


def _forget_cumsum_bwd(df, z, cb):
    s = z.shape[0]
    ts = _divisor_tile(s, 512, LANES)
    nt = s // ts

    def body(df_ref, x_ref, o_ref, carry):
        @pl.when(pl.program_id(0) == 0)
        def _():
            carry[...] = jnp.zeros_like(carry)

        df = df_ref[...]
        r = lax.broadcasted_iota(jnp.int32, (ts, ts), 0)
        c = lax.broadcasted_iota(jnp.int32, (ts, ts), 1)
        tri = jnp.where(c >= r, 1.0, 0.0).astype(F32)
        rc = jnp.dot(tri, df, preferred_element_type=F32, precision=lax.Precision.HIGHEST) + carry[...]
        carry[...] += jnp.sum(df, axis=0, keepdims=True)
        o_ref[...] = (rc * (1.0 / (1.0 + jnp.exp(x_ref[...])))).astype(o_ref.dtype)

    return pl.pallas_call(
        body, grid=(nt,),
        in_specs=[pl.BlockSpec((ts, LANES), lambda i: (nt - 1 - i, 0)),
                  pl.BlockSpec((ts, LANES), lambda i: (nt - 1 - i, cb))],
        out_specs=pl.BlockSpec((ts, LANES), lambda i: (nt - 1 - i, 0)),
        out_shape=jax.ShapeDtypeStruct((s, LANES), BF16),
        scratch_shapes=[pltpu.VMEM((1, LANES), F32)],
        compiler_params=_cparams(("arbitrary",)), name="forget_cumsum_bwd")(df, z)


def _flash_fwd(q, k, v, scale, nh, l_lane, *, name):
    s = q.shape[0]
    t = min(ATT_TILE, s)
    half = t // 2 if t % (2 * LANES) == 0 else t
    nq = s // t
    c = scale * LOG2E

    def body(q_ref, k_ref, v_ref, o_ref, qb_ref):
        i = pl.program_id(1)
        qb = q_ref[...]

        def scores(q_rows, k0, nk):
            kb = k_ref[pl.ds(pl.multiple_of(k0, half), nk), :]
            return lax.dot_general(q_rows, kb, _NT, preferred_element_type=F32)

        def update(sc, k0, nk, carry):
            m, acc = carry
            m_new = jnp.maximum(m, jnp.max(sc, axis=1, keepdims=True))
            p = jnp.exp2((sc - m_new) * c)
            alpha = jnp.exp2((m - m_new) * c)
            vb = v_ref[pl.ds(pl.multiple_of(k0, half), nk), :]
            acc = alpha * acc + jnp.dot(p.astype(BF16), vb, preferred_element_type=F32)
            return m_new, acc

        def full_block(j, cr):
            return update(scores(qb, j * t, t), j * t, t, cr)

        def group(jj, cr):
            for n in range(FWD_GROUP):
                cr = full_block(FWD_GROUP * jj + n, cr)
            return cr

        def causal(sc):
            row = lax.broadcasted_iota(jnp.int32, sc.shape, 0)
            col = lax.broadcasted_iota(jnp.int32, sc.shape, 1)
            return jnp.where(col <= row, sc, NEG)

        init = (jnp.full((t, 1), NEG, F32), jnp.zeros((t, LANES), F32))
        n_groups = i >> FWD_GROUP_LOG2
        carry = lax.fori_loop(0, n_groups, group, init)
        carry = lax.fori_loop(n_groups * FWD_GROUP, i, full_block, carry)
        m, acc = update(causal(scores(qb, i * t, t)), i * t, t, carry)
        lane = _lane((t, LANES))
        l = _lane_column(acc, lane, V_ONES)
        o_ref[...] = (acc / l).astype(o_ref.dtype)
        big_l = m + jnp.log(l) / scale
        qb_ref[...] = _put3(qb.astype(F32), lane, l_lane, _split3(-big_l)).astype(qb_ref.dtype)

    head_rows = pl.BlockSpec((t, LANES), lambda h, i: (i, h))
    head_all = pl.BlockSpec((s, LANES), lambda h, i: (0, h))
    return pl.pallas_call(
        body, grid=(nh, nq), in_specs=[head_rows, head_all, head_all], out_specs=[head_rows, head_rows],
        out_shape=[jax.ShapeDtypeStruct(q.shape, BF16), jax.ShapeDtypeStruct(q.shape, BF16)],
        compiler_params=_cparams(("parallel", "arbitrary")), name=name)(q, k, v)


def _flash_bwd(qb, k, v, do, scale, nh, n_feat, *, name):
    s = qb.shape[0]
    t = min(ATT_TILE, s)
    half = t // 2 if t % (2 * LANES) == 0 else t
    nq = s // t
    c = scale * LOG2E

    def body(q_ref, k_ref, v_ref, do_ref, dk_ref, dv_ref, dq_ref):
        j = pl.program_id(1)
        kb = k_ref[...]
        vb = v_ref[...]

        @pl.when(j == 0)
        def _():
            dq_ref[...] = jnp.zeros_like(dq_ref)

        def part(q0, n_q, n_k, carry, q_off):
            dk_acc, dv_acc = carry
            rows = pl.ds(pl.multiple_of(q0, half), n_q)
            qblk = q_ref[rows, :]
            dob = do_ref[rows, :]
            kbb, vbb = kb[:n_k], vb[:n_k]
            st = lax.dot_general(kbb, qblk, _NT, preferred_element_type=F32)
            if q_off is not None:
                key = lax.broadcasted_iota(jnp.int32, st.shape, 0)
                qry = lax.broadcasted_iota(jnp.int32, st.shape, 1) + q_off
                st = jnp.where(key <= qry, st, NEG)
            pt = jnp.exp2(st * c)
            dv_new = jnp.dot(pt.astype(BF16), dob, preferred_element_type=F32)
            dpt = lax.dot_general(vbb, dob, _NT, preferred_element_type=F32)
            dsb = (pt * dpt).astype(BF16)
            dk_new = jnp.dot(dsb, qblk, preferred_element_type=F32)
            dq_ref[rows, :] += lax.dot_general(dsb, kbb, _TN, preferred_element_type=F32)
            if n_k == t:
                return dk_acc + dk_new, dv_acc + dv_new
            return (jnp.concatenate([dk_acc[:n_k] + dk_new, dk_acc[n_k:]], axis=0),
                    jnp.concatenate([dv_acc[:n_k] + dv_new, dv_acc[n_k:]], axis=0))

        def block(i, carry):
            return part(i * t, t, t, carry, None)

        init = (jnp.zeros((t, LANES), F32), jnp.zeros((t, LANES), F32))
        carry = part(j * t + half, t - half, t, init, half) if half < t else part(j * t, t, t, init, 0)
        if half < t:
            carry = part(j * t, half, half, carry, 0)
        rest = nq - 1 - j
        carry = lax.cond((rest & 1) == 1, lambda cr: block(j + 1, cr), lambda cr: cr, carry)
        first = j + 1 + (rest & 1)

        def pair(ii, cr):
            i0 = first + 2 * ii
            return block(i0 + 1, block(i0, cr))

        dk_acc, dv_acc = lax.fori_loop(0, rest >> 1, pair, carry)
        dk_ref[...] = dk_acc * jnp.where(_lane((t, LANES)) < n_feat, scale, 1.0)
        dv_ref[...] = dv_acc.astype(dv_ref.dtype)

        @pl.when(j == nq - 1)
        def _():
            dq_ref[...] = dq_ref[...] * jnp.where(_lane((s, LANES)) < n_feat, scale, 1.0)

    head_rows = pl.BlockSpec((t, LANES), lambda h, j: (j, h))
    head_all = pl.BlockSpec((s, LANES), lambda h, j: (0, h))
    shp = jax.ShapeDtypeStruct(qb.shape, F32)
    return pl.pallas_call(
        body, grid=(nh, nq), in_specs=[head_all, head_rows, head_rows, head_all],
        out_specs=[head_rows, head_rows, head_all],
        out_shape=[shp, jax.ShapeDtypeStruct(qb.shape, BF16), shp],
        compiler_params=_cparams(("parallel", "arbitrary")), name=name)(qb, k, v, do)


def _with_delta(do, ex):
    n_blocks = do.shape[1] // LANES
    lane = _lane((do.shape[0], LANES))
    out = []
    for d_blk, o_blk in zip(_blocks(do, n_blocks), _blocks(ex[0], n_blocks), strict=True):
        delta = jnp.sum(d_blk * o_blk.astype(F32), axis=1, keepdims=True)
        out.append(_put3(d_blk, lane, V_ONES, _split3(-delta)))
    return [jnp.concatenate(out, axis=1)]


def _pad_heads(a, nh):
    d = a.shape[-1] // nh
    a = a.reshape(a.shape[:-1] + (nh, d))
    a = jnp.pad(a, [(0, 0)] * (a.ndim - 1) + [(0, LANES - d)])
    return a.reshape(a.shape[:-2] + (nh * LANES,))


def _unpad_heads(a, nh, d):
    a = a.reshape(a.shape[:-1] + (nh, LANES))[..., :d]
    return a.reshape(a.shape[:-2] + (nh * d,))


def _pad_head_rows(w, nh):
    return _pad_heads(w.T, nh).T


def _unpad_head_rows(g, nh, d):
    return _unpad_heads(g.T, nh, d).T


class _ZLayout:
    def __init__(self, d):
        fw = FOX_HEADS * FOX_DIM
        self.src = {}
        off = 0
        for nm, w in (("cq", MLA_Q_LORA), ("ckv", MLA_KV_LORA), ("kr", MLA_ROPE), ("fq", fw), ("fk", fw),
                      ("fv", fw), ("fl", FOX_HEADS), ("ga", d), ("gb", d)):
            self.src[nm] = (off, w)
            off += w
        self.dst = {}
        off = 0
        for nm, w in (("ga", d), ("gb", d), ("fq", fw), ("fk", fw), ("fv", fw), ("cq", MLA_Q_LORA),
                      ("ckv", MLA_KV_LORA), ("kr", LANES), ("fl", LANES)):
            assert off % w == 0
            self.dst[nm] = (off, w)
            off += w
        self.width = off
        self.split = self.dst["cq"][0]
        assert all((o - self.split) % w == 0 for o, w in self.dst.values() if o >= self.split)

    def to_kernel(self, w):
        def seg(nm):
            off, wd = self.src[nm]
            return w[..., off:off + wd]

        def pad(a, left, total):
            return jnp.pad(a, [(0, 0)] * (a.ndim - 1) + [(left, total - left - a.shape[-1])])

        return jnp.concatenate([seg("ga"), seg("gb"), seg("fq"), seg("fk"), seg("fv"), seg("cq"), seg("ckv"),
                                pad(seg("kr"), MLA_NOPE, LANES), pad(seg("fl"), 0, LANES)], axis=-1)

    def from_kernel(self, g):
        def seg(nm, lo=0, hi=None):
            off, wd = self.dst[nm]
            return g[..., off + lo:off + (wd if hi is None else hi)]

        return jnp.concatenate([seg("cq"), seg("ckv"), seg("kr", MLA_NOPE, MLA_NOPE + MLA_ROPE), seg("fq"), seg("fk"),
                                seg("fv"), seg("fl", 0, FOX_HEADS), seg("ga"), seg("gb")], axis=-1)


def _local_step(x, positions, target, wts, late_weights, send_grads):
    s, d = x.shape
    zl = _ZLayout(d)
    hw = MLA_HEADS * LANES
    assert MLA_HEADS == FOX_HEADS
    scale_mla = (MLA_NOPE + MLA_ROPE) ** -0.5
    scale_fox = FOX_DIM ** -0.5

    inv_freq = ROPE_THETA ** (-jnp.arange(HALF_ROPE, dtype=F32) / HALF_ROPE)
    ang = positions.astype(F32)[:, None] * inv_freq
    cos, sin = jnp.cos(ang), jnp.sin(ang)
    tail = jnp.zeros((s, LANES - MLA_NOPE - MLA_ROPE), F32)
    rc = jnp.concatenate([jnp.ones((s, MLA_NOPE), F32), cos, cos, tail], axis=1)
    ra = jnp.concatenate([jnp.zeros((s, MLA_NOPE), F32), -sin, jnp.zeros((s, HALF_ROPE), F32), tail], axis=1)
    rb = jnp.concatenate([jnp.zeros((s, MLA_NOPE + HALF_ROPE), F32), sin, tail], axis=1)

    w_in = zl.to_kernel(wts["w_in"])
    b_in = zl.to_kernel(wts["b_in"])

    def f_norm_in(ti, pa):
        y, _ = _rms(ti[0], pa[0])
        return [y], []

    (h,) = _rowwise("norm_in", f_norm_in, [x], [wts["ln_pre_mix"]], [(d, BF16)])
    z_lo = _mm(h, w_in[:, :zl.split], bias=b_in[:, :zl.split], out_dtype=BF16, name="proj_in_lo")
    z = _mm(h, w_in[:, zl.split:], bias=b_in[:, zl.split:], out_dtype=F32, name="proj_in_hi")

    def zwin(nm):
        off, wd = zl.dst[nm]
        return (z_lo, off // wd, wd) if off < zl.split else (z, (off - zl.split) // wd, wd)

    wts = {**wts, **late_weights("mixer", z)}
    w_uq = _pad_heads(wts["w_uq"], MLA_HEADS)
    w_ukv = jnp.concatenate([_pad_heads(wts["w_uk"], MLA_HEADS), _pad_heads(wts["w_uv"], MLA_HEADS)], axis=1)
    w_o_mla = _pad_head_rows(wts["w_o_mla"], MLA_HEADS)
    w_o_fox = _pad_head_rows(wts["w_o_fox"], FOX_HEADS)

    def f_mla_norms(ti, pa):
        cqn, _ = _rms(ti[0], pa[0])
        ckvn, _ = _rms(ti[1], pa[1])
        return [cqn, ckvn], []

    cqn, ckvn = _rowwise("mla_norms", f_mla_norms, [zwin("cq"), zwin("ckv")],
                         [wts["q_a_norm"], wts["kv_a_norm"]], [(MLA_Q_LORA, BF16), (MLA_KV_LORA, BF16)])
    qf = _mm(cqn, w_uq, out_dtype=F32, name="proj_uq")
    kv = _mm(ckvn, w_ukv, out_dtype=BF16, name="proj_ukv")

    def f_rope_q(ti, pa):
        xq, c_, a_, b_ = ti
        return [jnp.concatenate([_rope_block(blk, c_, a_, b_) for blk in _blocks(xq, MLA_HEADS)], axis=1)], []

    (q_mla,) = _rowwise("rope_q", f_rope_q, [qf, rc, ra, rb], [], [(hw, BF16)])

    def f_mla_kv(ti, pa):
        kn, vn, kr, c_, a_, b_ = ti
        lane = _lane(kr.shape)
        k_tail = jnp.where((lane >= MLA_Q_L) & (lane < MLA_Q_L + 3), 1.0, _rope_block(kr, c_, a_, b_))
        ones_v = (lane >= V_ONES) & (lane < V_ONES + 3)
        k_out = [jnp.where(lane < MLA_NOPE, blk.astype(F32), k_tail) for blk in _blocks(kn, MLA_HEADS)]
        v_out = [jnp.where(ones_v, 1.0, blk.astype(F32)) for blk in _blocks(vn, MLA_HEADS)]
        return [jnp.concatenate(k_out, axis=1), jnp.concatenate(v_out, axis=1)], []

    k_mla, v_mla = _rowwise("mla_kv", f_mla_kv, [(kv, 0, hw), (kv, 1, hw), zwin("kr"), rc, ra, rb], [],
                            [(hw, BF16), (hw, BF16)])
    o_mla, qb_mla = _flash_fwd(q_mla, k_mla, v_mla, scale_mla, MLA_HEADS, MLA_Q_L, name="mla_fwd")

    fl_cb = zwin("fl")[1]
    fcol = _forget_cumsum(z, fl_cb)

    def f_fox_qkv(ti, pa):
        spread = _head_spread(FOX_HEADS, FOX_DIM)
        fq, fk, fv = (jnp.dot(a, spread, preferred_element_type=F32) for a in ti[:3])
        fc = ti[3]
        lane = _lane(fc.shape)
        ones_q = (lane >= FOX_Q_ONES) & (lane < FOX_Q_ONES + 3)
        ones_k = (lane >= FOX_Q_F) & (lane < FOX_Q_ONES)
        ones_v = (lane >= V_ONES) & (lane < V_ONES + 3)
        q_out, k_out, v_out = [], [], []
        for hh, (qblk, kblk, vblk) in enumerate(zip(_blocks(fq, FOX_HEADS), _blocks(fk, FOX_HEADS),
                                                    _blocks(fv, FOX_HEADS), strict=True)):
            f_h = _lane_column(fc, lane, hh) * (1.0 / scale_fox)
            q_out.append(_put3(jnp.where(ones_q, 1.0, qblk), lane, FOX_Q_F, _split3(f_h)))
            k_out.append(_put3(jnp.where(ones_k, 1.0, kblk), lane, FOX_Q_ONES, _split3(-f_h)))
            v_out.append(jnp.where(ones_v, 1.0, vblk))
        return [jnp.concatenate(q_out, axis=1), jnp.concatenate(k_out, axis=1), jnp.concatenate(v_out, axis=1)], []

    q_fox, k_fox, v_fox = _rowwise("fox_qkv", f_fox_qkv, [zwin("fq"), zwin("fk"), zwin("fv"), fcol],
                                   [], [(hw, BF16)] * 3)
    o_fox, qb_fox = _flash_fwd(q_fox, k_fox, v_fox, scale_fox, FOX_HEADS, FOX_Q_L, name="fox_fwd")

    y_mla = _mm(o_mla, w_o_mla, out_dtype=BF16, name="proj_o_mla")
    y_fox = _mm(o_fox, w_o_fox, out_dtype=BF16, name="proj_o_fox")

    def f_gate(ti, pa):
        ga, gb, ya, yb = (a.astype(F32) for a in ti)
        return [_sigmoid(ga) * ya + _sigmoid(gb) * yb], []

    (merged,) = _rowwise("gate", f_gate, [zwin("ga"), zwin("gb"), y_mla, y_fox], [], [(d, BF16)])
    wts = {**wts, **late_weights("mlp", merged)}
    mix = _mm(merged, wts["w_out"], out_dtype=F32, name="proj_out")

    def f_resid1(ti, pa):
        xa, mx = ti
        y, _ = _rms(mx, pa[0])
        x1 = xa + y
        h2, _ = _rms(x1, pa[1])
        return [x1, h2], []

    x1, h2 = _rowwise("resid_mix", f_resid1, [x, mix], [wts["ln_post_mix"], wts["ln_pre_mlp"]], [(d, F32), (d, BF16)])

    def relu2(acc, ex):
        r = jnp.maximum(acc, 0.0)
        return [acc, r * r]

    u, act = _mm(h2, wts["w_ff1"], out_dtype=[BF16, BF16], name="ff1", epilogue=relu2)
    mo = _mm(act, wts["w_ff2"], out_dtype=F32, name="ff2")

    def f_loss(ti, pa):
        xa, mv, tg = ti
        y, _ = _rms(mv, pa[0])
        err = (xa + y) - tg
        g2 = err / d
        dmo, dg = _rms_bwd(mv, pa[0], g2)
        return [g2, dmo], [jnp.sum(err * err, axis=0, keepdims=True), dg]

    g2, d_mo, loss_cols, g_ln_post_mlp = _rowwise("loss", f_loss, [x1, mo, target], [wts["ln_post_mlp"]],
                                                  [(d, F32), (d, BF16)], [d, d])
    loss = 0.5 * jnp.sum(loss_cols) / d

    grads = {"ln_post_mlp": g_ln_post_mlp}
    grads["w_ff2"] = _mm_tn(act, d_mo, name="grad_ff2")

    def relu2_bwd(acc, ex):
        return [acc * (2.0 * jnp.maximum(ex[0], 0.0))]

    (d_u,) = _mm(d_mo, wts["w_ff2"], out_dtype=[BF16], name="ff2_bwd", transpose_b=True, extras=[u], epilogue=relu2_bwd)
    grads["w_ff1"] = _mm_tn(h2, d_u, name="grad_ff1")
    d_h2 = _mm(d_u, wts["w_ff1"], out_dtype=F32, name="ff1_bwd", transpose_b=True)

    def f_resid1_bwd(ti, pa):
        gres, dh2, x1v, mx = ti
        dx1n, dg_pre_mlp = _rms_bwd(x1v, pa[1], dh2)
        dx1 = gres + dx1n
        dmix, dg_post_mix = _rms_bwd(mx, pa[0], dx1)
        return [dx1, dmix], [dg_post_mix, dg_pre_mlp]

    d_x1, d_mix, grads["ln_post_mix"], grads["ln_pre_mlp"] = _rowwise(
        "resid_mix_bwd", f_resid1_bwd, [g2, d_h2, x1, mix], [wts["ln_post_mix"], wts["ln_pre_mlp"]],
        [(d, F32), (d, BF16)], [d, d])
    grads["w_out"] = _mm_tn(merged, d_mix, name="grad_out")
    def gate_bwd(dm, ex):
        ga, gb, ya, yb = (a.astype(F32) for a in ex)
        sa, sb = _sigmoid(ga), _sigmoid(gb)
        return [dm * sa, dm * sb, dm * ya * (sa * (1.0 - sa)), dm * yb * (sb * (1.0 - sb))]

    d_ya, d_yb, d_ga, d_gb = _mm(d_mix, wts["w_out"], out_dtype=[BF16] * 4, name="proj_out_bwd", transpose_b=True,
                                 extras=[(z_lo, zl.dst["ga"][0]), (z_lo, zl.dst["gb"][0]), y_mla, y_fox],
                                 epilogue=gate_bwd)
    grads["w_o_mla"] = _unpad_head_rows(_mm_tn(o_mla, d_ya, name="grad_o_mla"), MLA_HEADS, MLA_V)
    grads["w_o_fox"] = _unpad_head_rows(_mm_tn(o_fox, d_yb, name="grad_o_fox"), FOX_HEADS, FOX_DIM)
    w_o_mla = w_o_mla + send_grads("early", {nm: grads[nm] for nm in EARLY_GRADS}).astype(BF16)
    (do_mla,) = _mm(d_ya, w_o_mla, out_dtype=[BF16], name="proj_o_mla_bwd", transpose_b=True, extras=[o_mla],
                    epilogue=_with_delta)
    (do_fox,) = _mm(d_yb, w_o_fox, out_dtype=[BF16], name="proj_o_fox_bwd", transpose_b=True, extras=[o_fox],
                    epilogue=_with_delta)

    dk_mla, dv_mla, dq_mla = _flash_bwd(qb_mla, k_mla, v_mla, do_mla, scale_mla, MLA_HEADS, MLA_NOPE + MLA_ROPE,
                                        name="mla_bwd")
    dk_fox, dv_fox, dq_fox = _flash_bwd(qb_fox, k_fox, v_fox, do_fox, scale_fox, FOX_HEADS, FOX_DIM, name="fox_bwd")

    def f_rope_q_bwd(ti, pa):
        g, c_, a_, b_ = ti
        return [jnp.concatenate([_rope_block(blk, c_, -a_, -b_) for blk in _blocks(g, MLA_HEADS)], axis=1)], []

    (d_qf,) = _rowwise("rope_q_bwd", f_rope_q_bwd, [dq_mla, rc, ra, rb], [], [(hw, BF16)])
    grads["w_uq"] = _unpad_heads(_mm_tn(cqn, d_qf, name="grad_uq"), MLA_HEADS, MLA_NOPE + MLA_ROPE)
    d_cqn = _mm(d_qf, w_uq, out_dtype=F32, name="proj_uq_bwd", transpose_b=True)

    def f_mla_kv_bwd(ti, pa):
        gk, gv, c_, a_, b_ = ti
        k_blocks = _blocks(gk, MLA_HEADS)
        tot = k_blocks[0]
        for blk in k_blocks[1:]:
            tot = tot + blk
        return [jnp.concatenate([gk, gv], axis=1), _rope_block(tot, c_, -a_, -b_)], []

    d_kv, d_kr = _rowwise("mla_kv_bwd", f_mla_kv_bwd, [dk_mla, dv_mla, rc, ra, rb], [], [(2 * hw, BF16), (LANES, BF16)])
    g_ukv = _mm_tn(ckvn, d_kv, name="grad_ukv")
    grads["w_uk"] = _unpad_heads(g_ukv[:, :hw], MLA_HEADS, MLA_NOPE)
    grads["w_uv"] = _unpad_heads(g_ukv[:, hw:], MLA_HEADS, MLA_V)
    d_ckvn = _mm(d_kv, w_ukv, out_dtype=F32, name="proj_ukv_bwd", transpose_b=True)

    def f_mla_norms_bwd(ti, pa):
        cq, ckv, dcqn, dckvn = ti
        dcq, dg_q = _rms_bwd(cq, pa[0], dcqn)
        dckv, dg_kv = _rms_bwd(ckv, pa[1], dckvn)
        return [dcq, dckv], [dg_q, dg_kv]

    d_cq, d_ckv, grads["q_a_norm"], grads["kv_a_norm"] = _rowwise(
        "mla_norms_bwd", f_mla_norms_bwd, [zwin("cq"), zwin("ckv"), d_cqn, d_ckvn],
        [wts["q_a_norm"], wts["kv_a_norm"]], [(MLA_Q_LORA, BF16), (MLA_KV_LORA, BF16)], [MLA_Q_LORA, MLA_KV_LORA])

    def f_fox_compact(ti, pa):
        gather = _head_spread(FOX_HEADS, FOX_DIM)
        df = _forget_grad(_blocks(ti[0], FOX_HEADS), _blocks(ti[1], FOX_HEADS), _lane((ti[0].shape[0], LANES)))
        return [lax.dot_general(a.astype(BF16), gather, _NT, preferred_element_type=F32) for a in ti] + [df], []

    d_fq, d_fk, d_fv, d_fcum = _rowwise("fox_compact", f_fox_compact, [dq_fox, dk_fox, dv_fox], [],
                                        [(FOX_HEADS * FOX_DIM, BF16)] * 3 + [(LANES, F32)])
    d_fl = _forget_cumsum_bwd(d_fcum, z, fl_cb)
    d_z = jnp.concatenate([d_ga, d_gb, d_fq, d_fk, d_fv, d_cq, d_ckv, d_kr, d_fl], axis=1)
    assert d_z.shape[1] == zl.width

    def f_bias(ti, pa):
        return [], [jnp.sum(ti[0].astype(F32), axis=0, keepdims=True)]

    (g_b_in,) = _rowwise("grad_b_in", f_bias, [d_z], [], [], [zl.width])
    grads["b_in"] = zl.from_kernel(g_b_in)
    grads["w_in"] = zl.from_kernel(_mm_tn(h, d_z, name="grad_in"))
    tok = send_grads("late", {nm: grads[nm] for nm, _ in BIG if nm not in EARLY_GRADS})
    d_h = _mm(d_z, w_in, bias=jnp.zeros((1, d), F32) + tok, out_dtype=F32, name="proj_in_bwd", transpose_b=True)

    def f_norm_in_bwd(ti, pa):
        dx1v, dh, xa = ti
        dxn, dg = _rms_bwd(xa, pa[0], dh)
        return [dx1v + dxn], [dg]

    grad_x, grads["ln_pre_mix"] = _rowwise("norm_in_bwd", f_norm_in_bwd, [d_x1, d_h, x], [wts["ln_pre_mix"]],
                                           [(d, F32)], [d])
    return loss, grad_x, grads


class _PackLayout:
    def __init__(self, shapes):
        self.shapes = list(shapes)
        self.width = _round_up(max(b for _, b in shapes), LANES)
        self.bands = []
        row = 0
        shelf = []
        for idx, (a, b) in enumerate(shapes):
            if 2 * _round_up(b, LANES) > self.width:
                self.bands.append((row, _round_up(a, 32), [(idx, 0)]))
                row += _round_up(a, 32)
            else:
                shelf.append(idx)
        col, items = 0, []
        for idx in shelf:
            wb = _round_up(shapes[idx][1], LANES)
            if col + wb > self.width:
                hgt = max(_round_up(shapes[i][0], 32) for i, _ in items)
                self.bands.append((row, hgt, items))
                row += hgt
                col, items = 0, []
            items.append((idx, col))
            col += wb
        if items:
            hgt = max(_round_up(shapes[i][0], 32) for i, _ in items)
            self.bands.append((row, hgt, items))
            row += hgt
        self.rows = _round_up(row, 16 * LOCAL_PIECES)

    def pack(self, arrs):
        lead = arrs[0].shape[:-2]
        no_pad = [(0, 0)] * len(lead)
        bands = []
        for _, hgt, items in self.bands:
            parts = []
            for k, (idx, col) in enumerate(items):
                a, b = self.shapes[idx]
                nxt = items[k + 1][1] if k + 1 < len(items) else self.width
                parts.append(jnp.pad(arrs[idx], no_pad + [(0, hgt - a), (0, nxt - col - b)]))
            bands.append(parts[0] if len(parts) == 1 else jnp.concatenate(parts, axis=-1))
        used = sum(hgt for _, hgt, _ in self.bands)
        if used < self.rows:
            bands.append(jnp.zeros(lead + (self.rows - used, self.width), arrs[0].dtype))
        return jnp.concatenate(bands, axis=-2)

    def unpack(self, packed):
        out = [None] * len(self.shapes)
        for row, _, items in self.bands:
            for idx, col in items:
                a, b = self.shapes[idx]
                out[idx] = packed[..., row:row + a, col:col + b]
        return out


def _to_shards(g, axis):
    if axis == 0:
        return g.reshape(N_CHIPS, g.shape[0] // N_CHIPS, g.shape[1])
    return jnp.stack(jnp.split(g, N_CHIPS, axis=1))


def _from_shards(s4, axis):
    n, a, b = s4.shape
    if axis == 0:
        return s4.reshape(n * a, b)
    return jnp.concatenate([s4[ch] for ch in range(n)], axis=1)


ANY = pl.BlockSpec(memory_space=pl.ANY)


def _place():
    return lax.axis_index("x"), lax.axis_index("y"), lax.axis_index("c")


def _gather_weights(wpk):
    rows, wd = wpk.shape
    half = rows // 2

    def body(w_ref, out_ref, send_sems, recv_sems, local_sems):
        x, y, c = _place()
        sibling = (x, y, 1 - c)
        chips = [(1 - x, y), (x, 1 - y), (1 - x, 1 - y)]

        def slab(chip, hf):
            return out_ref.at[2 * chip[0] + chip[1], pl.ds(hf * half, half), :]

        def copy(k, chip, hf, to, src=None):
            return pltpu.make_async_remote_copy(
                src_ref=slab(chip, hf) if src is None else src, dst_ref=slab(chip, hf),
                send_sem=send_sems.at[k], recv_sem=recv_sems.at[k], device_id=to, device_id_type=MESH)

        piece = rows // LOCAL_PIECES
        mine = [pltpu.make_async_copy(w_ref.at[pl.ds(n * piece, piece), :],
                                      out_ref.at[2 * x + y, pl.ds(n * piece, piece), :], local_sems.at[n])
                for n in range(LOCAL_PIECES)]
        for cp in mine:
            cp.start()
        first = [copy(j, (x, y), c, (*chip, c), src=w_ref.at[pl.ds(c * half, half), :]) for j, chip in enumerate(chips)]
        for cp in first:
            cp.start()
        passed = [copy(3 + j, chip, c, sibling) for j, chip in enumerate(chips)]
        for j, chip in enumerate(chips):
            copy(j, chip, c, (x, y, c)).wait_recv()
            passed[j].start()
        for j, chip in enumerate(chips):
            copy(3 + j, chip, 1 - c, (x, y, c)).wait_recv()
        for cp in first + passed:
            cp.wait_send()
        for cp in mine:
            cp.wait()

    assert rows % (16 * LOCAL_PIECES) == 0
    return pl.pallas_call(
        body, out_shape=jax.ShapeDtypeStruct((N_CHIPS, rows, wd), wpk.dtype),
        in_specs=[ANY], out_specs=ANY,
        scratch_shapes=[pltpu.SemaphoreType.DMA((6,)), pltpu.SemaphoreType.DMA((6,)),
                        pltpu.SemaphoreType.DMA((LOCAL_PIECES,))],
        name="gather_weights")(wpk)


HBM = pl.BlockSpec(memory_space=pltpu.HBM)
SEM = pl.BlockSpec(memory_space=pltpu.SEMAPHORE)
EFFECT = pltpu.SideEffectType.DATAFLOW_SIDE_EFFECTING
N_LATE = 6


def _gather_late_start(wpk, tag):
    rows, wd = wpk.shape
    half = rows // 2

    def body(w_ref, land_ref, send_sems, recv_sems, w_thru, land_thru, token):
        x, y, c = _place()
        chips = [(1 - x, y), (x, 1 - y), (1 - x, 1 - y)]
        for j, chip in enumerate(chips):
            for to_core in range(2):
                pltpu.make_async_remote_copy(
                    src_ref=w_ref.at[pl.ds(c * half, half), :],
                    dst_ref=land_ref.at[2 * x + y, pl.ds(c * half, half), :],
                    send_sem=send_sems.at[2 * j + to_core], recv_sem=recv_sems.at[2 * j + c],
                    device_id=(*chip, to_core), device_id_type=MESH).start()
        token[...] = jnp.zeros_like(token)

    land = lax.empty((N_CHIPS, rows, wd), wpk.dtype)
    return pl.pallas_call(
        body, name="gather_" + tag + "_start",
        out_shape=(pltpu.SemaphoreType.DMA((N_LATE,)), pltpu.SemaphoreType.DMA((N_LATE,)),
                   pltpu.HBM(wpk.shape, wpk.dtype), pltpu.HBM(land.shape, land.dtype),
                   jax.ShapeDtypeStruct((8, LANES), F32)),
        in_specs=(HBM, HBM), out_specs=(SEM, SEM, HBM, HBM, pl.BlockSpec(memory_space=pltpu.VMEM)),
        input_output_aliases={0: 2, 1: 3},
        compiler_params=pltpu.CompilerParams(has_side_effects=EFFECT),
    )(pltpu.with_memory_space_constraint(wpk, pltpu.HBM), pltpu.with_memory_space_constraint(land, pltpu.HBM))


def _gather_late_wait(send_sems, recv_sems, w_thru, land_thru, after, tag):
    rows, wd = w_thru.shape
    half = rows // 2

    def body(w_ref, land_ref, send_sems, recv_sems, after_ref, w_dead, land_out):
        x, y, c = _place()
        for n in range(N_LATE):
            cp = pltpu.make_async_remote_copy(
                src_ref=w_ref.at[pl.ds(0, half), :], dst_ref=land_ref.at[0, pl.ds(0, half), :],
                send_sem=send_sems.at[n], recv_sem=recv_sems.at[n], device_id=(x, y, c), device_id_type=MESH)
            cp.wait_send()
            cp.wait_recv()

    return pl.pallas_call(
        body, name="gather_" + tag + "_wait",
        out_shape=(pltpu.HBM(w_thru.shape, w_thru.dtype), pltpu.HBM(land_thru.shape, land_thru.dtype)),
        in_specs=(HBM, HBM, SEM, SEM, ANY), out_specs=(HBM, HBM), input_output_aliases={0: 0, 1: 1},
        compiler_params=pltpu.CompilerParams(has_side_effects=EFFECT),
    )(w_thru, land_thru, send_sems, recv_sems, after)[1]


N_PART = 7


def _reduce_start(gbf, tag):
    _, _, hrows, wd = gbf.shape

    def body(g_ref, land_ref, send_sems, recv_sems, g_thru, land_thru, token):
        x, y, c = _place()
        chips = [(1 - x, y), (x, 1 - y), (1 - x, 1 - y)]
        for j, chip in enumerate(chips):
            for to_core in range(2):
                pltpu.make_async_remote_copy(
                    src_ref=g_ref.at[2 * chip[0] + chip[1], to_core], dst_ref=land_ref.at[2 * j + c],
                    send_sem=send_sems.at[2 * j + to_core], recv_sem=recv_sems.at[2 * j + c],
                    device_id=(*chip, to_core), device_id_type=MESH).start()
        pltpu.make_async_remote_copy(
            src_ref=g_ref.at[2 * x + y, 1 - c], dst_ref=land_ref.at[N_PART - 1],
            send_sem=send_sems.at[N_PART - 1], recv_sem=recv_sems.at[N_PART - 1],
            device_id=(x, y, 1 - c), device_id_type=MESH).start()
        token[...] = jnp.zeros_like(token)

    land = lax.empty((N_PART, hrows, wd), gbf.dtype)
    return pl.pallas_call(
        body, name="reduce_" + tag + "_start",
        out_shape=(pltpu.SemaphoreType.DMA((N_PART,)), pltpu.SemaphoreType.DMA((N_PART,)),
                   pltpu.HBM(gbf.shape, gbf.dtype), pltpu.HBM(land.shape, land.dtype),
                   jax.ShapeDtypeStruct((8, LANES), F32)),
        in_specs=(HBM, HBM), out_specs=(SEM, SEM, HBM, HBM, pl.BlockSpec(memory_space=pltpu.VMEM)),
        input_output_aliases={0: 2, 1: 3},
        compiler_params=pltpu.CompilerParams(has_side_effects=EFFECT),
    )(pltpu.with_memory_space_constraint(gbf, pltpu.HBM), pltpu.with_memory_space_constraint(land, pltpu.HBM))


def _reduce_wait(send_sems, recv_sems, g_thru, land_thru, after, tag):
    def body(g_ref, land_ref, send_sems, recv_sems, after_ref, g_dead, land_out):
        x, y, c = _place()
        for n in range(N_PART):
            cp = pltpu.make_async_remote_copy(
                src_ref=g_ref.at[0, 0], dst_ref=land_ref.at[0], send_sem=send_sems.at[n], recv_sem=recv_sems.at[n],
                device_id=(x, y, c), device_id_type=MESH)
            cp.wait_send()
            cp.wait_recv()

    return pl.pallas_call(
        body, name="reduce_" + tag + "_wait",
        out_shape=(pltpu.HBM(g_thru.shape, g_thru.dtype), pltpu.HBM(land_thru.shape, land_thru.dtype)),
        in_specs=(HBM, HBM, SEM, SEM, ANY), out_specs=(HBM, HBM), input_output_aliases={0: 0, 1: 1},
        compiler_params=pltpu.CompilerParams(has_side_effects=EFFECT),
    )(g_thru, land_thru, send_sems, recv_sems, after)[1]


def _sibling_swap(mine):
    def body(m_ref, out_ref, send_sem, recv_sem):
        x, y, c = _place()
        cp = pltpu.make_async_remote_copy(
            src_ref=m_ref, dst_ref=out_ref, send_sem=send_sem, recv_sem=recv_sem,
            device_id=(x, y, 1 - c), device_id_type=MESH)
        cp.start()
        cp.wait()

    return pl.pallas_call(
        body, out_shape=jax.ShapeDtypeStruct(mine.shape, mine.dtype), in_specs=[ANY], out_specs=ANY,
        scratch_shapes=[pltpu.SemaphoreType.DMA, pltpu.SemaphoreType.DMA], name="grad_sibling_swap")(mine)


def _adamw(w, g, m, v):
    m = ADAM_B1 * m + (1.0 - ADAM_B1) * g
    v = ADAM_B2 * v + (1.0 - ADAM_B2) * (g * g)
    m_hat = m / (1.0 - ADAM_B1 ** ADAM_STEP)
    v_hat = v / (1.0 - ADAM_B2 ** ADAM_STEP)
    delta = -ADAM_LR * (m_hat / (jnp.sqrt(v_hat) + ADAM_EPS) + ADAM_WD * w)
    return delta, m, v


def _small_allreduce_adamw(gs, ws, ms, vs):
    n_dev = 8
    n_par = len(gs)
    wd = PACK_W
    chunks = []
    for p, g in enumerate(gs):
        for off in range(0, g.shape[1], wd):
            chunks.append((p, len(chunks), off, min(wd, g.shape[1] - off)))
    rows = _round_up(len(chunks), 8)

    def body(*refs):
        g_refs, w_refs, m_refs, v_refs = (refs[k * n_par:(k + 1) * n_par] for k in range(4))
        go_refs, d_refs, mo_refs, vo_refs = (refs[(4 + k) * n_par:(5 + k) * n_par] for k in range(4))
        mine_ref, all_ref, send_sems, recv_sems = refs[8 * n_par:]
        x, y, c = _place()
        me, sibling = (x, y, c), (x, y, 1 - c)
        chips = [(1 - x, y), (x, 1 - y), (1 - x, 1 - y)]

        def slot(px, py, pc):
            return all_ref.at[4 * px + 2 * py + pc]

        def copy(k, block, to, src=None):
            return pltpu.make_async_remote_copy(
                src_ref=slot(*block) if src is None else src, dst_ref=slot(*block),
                send_sem=send_sems.at[k], recv_sem=recv_sems.at[k], device_id=to, device_id_type=MESH)

        mine_ref[...] = jnp.zeros_like(mine_ref)
        for p, row, off, width in chunks:
            mine_ref[row:row + 1, 0:width] = g_refs[p][:, off:off + width]
        all_ref[4 * x + 2 * y + c] = mine_ref[...]
        first = [copy(0, me, sibling, src=mine_ref)]
        first += [copy(1 + j, me, (*chip, c), src=mine_ref) for j, chip in enumerate(chips)]
        for cp in first:
            cp.start()
        passed = [copy(4 + j, (*chip, c), sibling) for j, chip in enumerate(chips)]
        for j, chip in enumerate(chips):
            copy(1 + j, (*chip, c), me).wait_recv()
            passed[j].start()
        copy(0, sibling, me).wait_recv()
        for j, chip in enumerate(chips):
            copy(4 + j, (*chip, 1 - c), me).wait_recv()
        for cp in first + passed:
            cp.wait_send()
        tot = jnp.zeros((rows, wd), F32)
        for dev in range(n_dev):
            tot = tot + all_ref[dev]
        mine_ref[...] = tot
        for p, row, off, width in chunks:
            cols = slice(off, off + width)
            g = mine_ref[row:row + 1, 0:width]
            delta, m_new, v_new = _adamw(w_refs[p][:, cols], g, m_refs[p][:, cols], v_refs[p][:, cols])
            go_refs[p][:, cols] = g
            d_refs[p][:, cols] = delta
            mo_refs[p][:, cols] = m_new
            vo_refs[p][:, cols] = v_new

    vm = pl.BlockSpec(memory_space=pltpu.VMEM)
    shp = [jax.ShapeDtypeStruct(g.shape, F32) for g in gs]
    res = pl.pallas_call(
        body, out_shape=shp * 4, in_specs=[vm] * (4 * n_par), out_specs=[vm] * (4 * n_par),
        scratch_shapes=[pltpu.VMEM((rows, wd), F32), pltpu.VMEM((n_dev, rows, wd), F32),
                        pltpu.SemaphoreType.DMA((7,)), pltpu.SemaphoreType.DMA((7,))],
        name="small_allreduce_adamw")(*gs, *ws, *ms, *vs)
    return [res[k * n_par:(k + 1) * n_par] for k in range(4)]


def kernel(x, positions, ln_pre_mix, ln_post_mix, ln_pre_mlp, ln_post_mlp, w_in, b_in, q_a_norm, w_uq, kv_a_norm, w_uk, w_uv, w_o_mla, w_o_fox, w_out, w_ff1, w_ff2, loss_target, m_ln_pre_mix, m_ln_post_mix, m_ln_pre_mlp, m_ln_post_mlp, m_w_in, m_b_in, m_q_a_norm, m_w_uq, m_kv_a_norm, m_w_uk, m_w_uv, m_w_o_mla, m_w_o_fox, m_w_out, m_w_ff1, m_w_ff2, v_ln_pre_mix, v_ln_post_mix, v_ln_pre_mlp, v_ln_post_mlp, v_w_in, v_b_in, v_q_a_norm, v_w_uq, v_kv_a_norm, v_w_uk, v_w_uv, v_w_o_mla, v_w_o_fox, v_w_out, v_w_ff1, v_w_ff2):
    w = dict(ln_pre_mix=ln_pre_mix, ln_post_mix=ln_post_mix, ln_pre_mlp=ln_pre_mlp, ln_post_mlp=ln_post_mlp, w_in=w_in,
             b_in=b_in, q_a_norm=q_a_norm, w_uq=w_uq, kv_a_norm=kv_a_norm, w_uk=w_uk, w_uv=w_uv, w_o_mla=w_o_mla,
             w_o_fox=w_o_fox, w_out=w_out, w_ff1=w_ff1, w_ff2=w_ff2)
    mom = dict(ln_pre_mix=m_ln_pre_mix, ln_post_mix=m_ln_post_mix, ln_pre_mlp=m_ln_pre_mlp, ln_post_mlp=m_ln_post_mlp,
               w_in=m_w_in, b_in=m_b_in, q_a_norm=m_q_a_norm, w_uq=m_w_uq, kv_a_norm=m_kv_a_norm, w_uk=m_w_uk,
               w_uv=m_w_uv, w_o_mla=m_w_o_mla, w_o_fox=m_w_o_fox, w_out=m_w_out, w_ff1=m_w_ff1, w_ff2=m_w_ff2)
    var = dict(ln_pre_mix=v_ln_pre_mix, ln_post_mix=v_ln_post_mix, ln_pre_mlp=v_ln_pre_mlp, ln_post_mlp=v_ln_post_mlp,
               w_in=v_w_in, b_in=v_b_in, q_a_norm=v_q_a_norm, w_uq=v_w_uq, kv_a_norm=v_kv_a_norm, w_uk=v_w_uk,
               w_uv=v_w_uv, w_o_mla=v_w_o_mla, w_o_fox=v_w_o_fox, w_out=v_w_out, w_ff1=v_w_ff1, w_ff2=v_w_ff2)

    big_names = [nm for nm, _ in BIG]
    c = lax.axis_index("c")
    chip = 2 * lax.axis_index("x") + lax.axis_index("y")

    axes = dict(BIG)

    def assemble(names, lay, gathered):
        return {nm: _from_shards(s4, axes[nm] - 1) for nm, s4 in zip(names, lay.unpack(gathered), strict=True)}

    groups = {"mixer": [nm for nm in big_names if nm in MIXER], "mlp": [nm for nm in big_names if nm in LATE]}
    first_names = [nm for nm in big_names if nm not in MIXER and nm not in LATE]
    full = {nm: wv for nm, wv in w.items() if nm in SMALL}
    lay_first = _PackLayout([w[nm].shape[1:] for nm in first_names])
    first = _gather_weights(lay_first.pack([w[nm][0].astype(BF16) for nm in first_names]))
    full.update(assemble(first_names, lay_first, first))
    travelling = {}
    for tag, names in groups.items():
        lay = _PackLayout([w[nm].shape[1:] for nm in names])
        shard = lay.pack([w[nm][0].astype(BF16) for nm in names])
        started = _gather_late_start(lax.optimization_barrier((shard, first))[0], tag)
        travelling[tag] = (names, lay, shard, started[:4])
        full["b_in"] = full["b_in"] + started[4][0, 0]

    def late_weights(tag, after):
        names, lay, shard, handles = travelling[tag]
        land = _gather_late_wait(*handles, after, tag)
        land = lax.dynamic_update_slice(land, shard[None], (chip, 0, 0))
        return assemble(names, lay, land)

    grad_groups = {"early": [nm for nm in big_names if nm in EARLY_GRADS],
                   "late": [nm for nm in big_names if nm not in EARLY_GRADS]}
    sent = {}

    def send_grads(tag, g):
        names = grad_groups[tag]
        lay = _PackLayout([w[nm].shape[1:] for nm in names])
        hrows = lay.rows // 2
        gbf = lay.pack([_to_shards(g[nm], axes[nm] - 1).astype(BF16) for nm in names])
        gbf = gbf.reshape(N_CHIPS, 2, hrows, lay.width)
        started = _reduce_start(gbf, tag)
        own = lax.dynamic_index_in_dim(lax.dynamic_index_in_dim(started[2], chip, axis=0, keepdims=False), c, axis=0,
                                       keepdims=False)
        sent[tag] = (names, lay, started[:4], own)
        return started[4][0, 0]

    loss_local, grad_x, grads = _local_step(x[0], positions[0], loss_target[0], full, late_weights, send_grads)

    def f_add8(ti, pa):
        tot = ti[0].astype(F32)
        for part in ti[1:]:
            tot = tot + part.astype(F32)
        return [tot], []

    reduced = []
    for tag, (names, lay, handles, own) in sent.items():
        parts = _reduce_wait(*handles, grad_x, tag)
        reduced.append(_rowwise("grad_add_" + tag, f_add8, [own] + [parts[n] for n in range(N_PART)], [],
                                [(lay.width, F32)])[0])

    assert len({lay.width for _, lay, _, _ in sent.values()}) == 1
    red = jnp.concatenate(reduced, axis=0)
    sib = _sibling_swap(red)
    lower, upper = jnp.where(c == 0, red, sib), jnp.where(c == 0, sib, red)
    g_by_name, row = {}, 0
    for names, lay, _, _ in sent.values():
        hrows = lay.rows // 2
        both = jnp.concatenate([lower[row:row + hrows], upper[row:row + hrows]], axis=0)
        g_by_name.update(zip(names, lay.unpack(both), strict=True))
        row += hrows
    g_shards = [g_by_name[nm] for nm in big_names]

    def f_adamw(ti, pa):
        wv, gv, mv, vv = ti
        return list(_adamw(wv, gv, mv, vv)), []

    out = {"grad": {}, "delta": {}, "m": {}, "v": {}}
    for nm, g_sh in zip(big_names, g_shards, strict=True):
        wd = g_sh.shape[1]
        d_sh, m_sh, v_sh = _rowwise("adamw_" + nm, f_adamw, [w[nm][0], g_sh, mom[nm][0], var[nm][0]], [], [(wd, F32)] * 3)
        out["grad"][nm], out["delta"][nm], out["m"][nm], out["v"][nm] = g_sh[None], d_sh[None], m_sh[None], v_sh[None]

    loss_row = jnp.zeros((1, LANES), F32) + loss_local
    blank = jnp.zeros((1, LANES), F32)
    small = _small_allreduce_adamw([grads[nm] for nm in SMALL] + [loss_row], [w[nm] for nm in SMALL] + [blank],
                                   [mom[nm] for nm in SMALL] + [blank], [var[nm] for nm in SMALL] + [blank])
    for kind, arrs in zip(("grad", "delta", "m", "v"), small, strict=True):
        for nm, arr in zip(SMALL, arrs[:len(SMALL)], strict=True):
            out[kind][nm] = arr
    loss = small[0][len(SMALL)][0, 0]

    return (loss, grad_x[None], *[out["grad"][nm] for nm in ALL_W], *[out["delta"][nm] for nm in ALL_W],
            *[out["m"][nm] for nm in ALL_W], *[out["v"][nm] for nm in ALL_W])
```

```python
import functools
import math

import jax
import jax.numpy as jnp
from jax import lax
from jax.experimental import pallas as pl
from jax.experimental.pallas import tpu as pltpu

F32 = jnp.float32
BF16 = jnp.bfloat16

MLA_HEADS = 8
MLA_Q_LORA = 256
MLA_KV_LORA = 128
MLA_NOPE = 64
MLA_ROPE = 32
MLA_V = 64
FOX_HEADS = 8
FOX_DIM = 64
ROPE_THETA = 10000.0
NORM_EPS = 1e-6
HALF_ROPE = MLA_ROPE // 2

ADAM_LR = 0.001
ADAM_B1 = 0.9
ADAM_B2 = 0.999
ADAM_EPS = 1e-08
ADAM_WD = 0.01
ADAM_STEP = 10

LANES = 128
VMEM_LIMIT = 56 * 1024 * 1024
ATT_TILE = 1024
FWD_GROUP_LOG2 = 1
FWD_GROUP = 1 << FWD_GROUP_LOG2
MM_VMEM_BUDGET = 40 * 1024 * 1024
ROW_TILE_BYTES = 12 * 1024 * 1024
MXU_WIDTH = 256
MXU_MACS_PER_S = 4.98e14
HBM_BYTES_PER_S = 3.2e12
STEP_OVERHEAD_S = 0.35e-6
NEG = -1e30
LOG2E = math.log2(math.e)
MESH = pl.DeviceIdType.MESH

V_ONES = 64
FOX_Q_F = 64
FOX_Q_L = 67
FOX_Q_ONES = 70
MLA_Q_L = 96

BIG = (("w_in", 2), ("w_uq", 2), ("w_uk", 2), ("w_uv", 2), ("w_o_mla", 2), ("w_o_fox", 2),
       ("w_out", 1), ("w_ff1", 2), ("w_ff2", 1))
MIXER = ("w_uq", "w_uk", "w_uv", "w_o_mla", "w_o_fox")
LATE = ("w_out", "w_ff1", "w_ff2")
EARLY_GRADS = ("w_o_mla", "w_o_fox", "w_out", "w_ff1", "w_ff2")
SMALL = ("ln_pre_mix", "ln_post_mix", "ln_pre_mlp", "ln_post_mlp", "b_in", "q_a_norm", "kv_a_norm")
ALL_W = ("ln_pre_mix", "ln_post_mix", "ln_pre_mlp", "ln_post_mlp", "w_in", "b_in", "q_a_norm", "w_uq",
         "kv_a_norm", "w_uk", "w_uv", "w_o_mla", "w_o_fox", "w_out", "w_ff1", "w_ff2")
N_CHIPS = 4
PACK_W = 1024
LOCAL_PIECES = 8

_NT = (((1,), (1,)), ((), ()))
_TN = (((0,), (0,)), ((), ()))


def _cparams(sem=None):
    return pltpu.CompilerParams(dimension_semantics=sem, vmem_limit_bytes=VMEM_LIMIT)


def _divisor_tile(n, limit, mult):
    if n <= limit:
        return n
    best = None
    t = mult
    while t <= limit:
        if n % t == 0:
            best = t
        t += mult
    assert best is not None, (n, limit, mult)
    return best


def _round_up(v, mult):
    return -(-v // mult) * mult


def _mm_tiles(m, k, n, io_bytes):
    best = None
    for tm in (2048, 1024, 512, 256, 128):
        if m % tm:
            continue
        for tn in range(LANES, min(n, 2048) + 1, LANES):
            if n % tn:
                continue
            vmem = 2 * (tm * k * 2 + k * tn * 2 + tm * tn * io_bytes) + tm * tn * 4
            if vmem > MM_VMEM_BUDGET:
                continue
            mxu = m * k * n * (_round_up(tn, MXU_WIDTH) / tn) / MXU_MACS_PER_S
            hbm = (m * k * 2 + (m // tm) * k * n * 2 + m * n * io_bytes) / HBM_BYTES_PER_S
            cost = max(mxu, hbm) + (m // tm) * (n // tn) * STEP_OVERHEAD_S
            if best is None or cost < best[0]:
                best = (cost, tm, tn)
    assert best is not None, (m, k, n)
    return best[1], best[2]


def _mm(a, b, *, out_dtype, name, bias=None, transpose_b=False, extras=(), epilogue=None):
    m, k = a.shape
    n = b.shape[0] if transpose_b else b.shape[1]
    assert (b.shape[1] if transpose_b else b.shape[0]) == k and a.dtype == BF16 and b.dtype == BF16
    out_dtypes = list(out_dtype) if isinstance(out_dtype, (list, tuple)) else [out_dtype]
    n_ex = len(extras)
    tm, tn = _mm_tiles(m, k, n, sum(jnp.dtype(dt).itemsize for dt in out_dtypes) + 4 * n_ex)

    def body(*refs):
        a_ref, b_ref = refs[:2]
        pos = 2
        bias_ref = None
        if bias is not None:
            bias_ref = refs[pos]
            pos += 1
        ex_refs = refs[pos:pos + n_ex]
        o_refs = refs[pos + n_ex:]
        if transpose_b:
            acc = lax.dot_general(a_ref[...], b_ref[...], _NT, preferred_element_type=F32)
        else:
            acc = jnp.dot(a_ref[...], b_ref[...], preferred_element_type=F32)
        if bias_ref is not None:
            acc = acc + bias_ref[...]
        vals = [acc] if epilogue is None else epilogue(acc, [r[...] for r in ex_refs])
        for ref, val in zip(o_refs, vals, strict=True):
            ref[...] = val.astype(ref.dtype)

    b_spec = pl.BlockSpec((tn, k), lambda i, j: (j, 0)) if transpose_b else pl.BlockSpec((k, tn), lambda i, j: (0, j))
    in_specs = [pl.BlockSpec((tm, k), lambda i, j: (i, 0)), b_spec]
    args = [a, b]
    if bias is not None:
        in_specs.append(pl.BlockSpec((1, tn), lambda i, j: (0, j)))
        args.append(bias)
    for ex in extras:
        arr, col0 = ex if isinstance(ex, tuple) else (ex, 0)
        assert col0 % tn == 0
        in_specs.append(pl.BlockSpec((tm, tn), functools.partial(lambda i, j, off: (i, j + off), off=col0 // tn)))
        args.append(arr)
    res = pl.pallas_call(
        body, grid=(m // tm, n // tn), in_specs=in_specs,
        out_specs=[pl.BlockSpec((tm, tn), lambda i, j: (i, j)) for _ in out_dtypes],
        out_shape=[jax.ShapeDtypeStruct((m, n), dt) for dt in out_dtypes],
        compiler_params=_cparams(("parallel", "parallel")), name=name)(*args)
    return res if isinstance(out_dtype, (list, tuple)) else res[0]


def _mm_tn(a, b, *, name):
    s, m = a.shape
    s2, n = b.shape
    assert s == s2 and a.dtype == BF16 and b.dtype == BF16
    tm = _divisor_tile(m, 1024, LANES)
    tn = _divisor_tile(n, 2304, LANES)
    tk = _divisor_tile(s, 1024, 16)
    nk = s // tk

    def body(a_ref, b_ref, o_ref, acc_ref):
        @pl.when(pl.program_id(2) == 0)
        def _():
            acc_ref[...] = jnp.zeros_like(acc_ref)

        acc_ref[...] += lax.dot_general(a_ref[...], b_ref[...], _TN, preferred_element_type=F32)

        @pl.when(pl.program_id(2) == nk - 1)
        def _():
            o_ref[...] = acc_ref[...].astype(o_ref.dtype)

    return pl.pallas_call(
        body, grid=(m // tm, n // tn, nk),
        in_specs=[pl.BlockSpec((tk, tm), lambda i, j, k: (k, i)), pl.BlockSpec((tk, tn), lambda i, j, k: (k, j))],
        out_specs=pl.BlockSpec((tm, tn), lambda i, j, k: (i, j)),
        out_shape=jax.ShapeDtypeStruct((m, n), BF16),
        scratch_shapes=[pltpu.VMEM((tm, tn), F32)],
        compiler_params=_cparams(("parallel", "parallel", "arbitrary")), name=name)(a, b)


def _rowwise(name, fn, tiled, params, outs, reds=(), reverse=False):
    wins = [t if isinstance(t, tuple) else (t, 0, t.shape[1]) for t in tiled]
    s = wins[0][0].shape[0]
    row_bytes = sum(w * arr.dtype.itemsize for arr, _, w in wins) + sum(w * jnp.dtype(d).itemsize for w, d in outs)
    ts = _divisor_tile(s, max(16, min(1024, ROW_TILE_BYTES // row_bytes)), 16)
    nt, npar, nout = len(wins), len(params), len(outs)
    n_tiles = s // ts

    def row(i):
        return n_tiles - 1 - i if reverse else i

    def body(*refs):
        tin = [r[...] for r in refs[:nt]]
        par = [r[...] for r in refs[nt:nt + npar]]
        out_refs = refs[nt + npar:nt + npar + nout]
        red_refs = refs[nt + npar + nout:]
        o, r = fn(tin, par)
        for ref, val in zip(out_refs, o, strict=True):
            ref[...] = val.astype(ref.dtype)
        if red_refs:
            @pl.when(pl.program_id(0) == 0)
            def _():
                for ref in red_refs:
                    ref[...] = jnp.zeros_like(ref)

            for ref, val in zip(red_refs, r, strict=True):
                ref[...] += val

    in_specs = [pl.BlockSpec((ts, w), functools.partial(lambda i, cb: (row(i), cb), cb=cb)) for _, cb, w in wins]
    in_specs += [pl.BlockSpec(p.shape, lambda i: (0, 0)) for p in params]
    out_specs = [pl.BlockSpec((ts, w), lambda i: (row(i), 0)) for w, _ in outs]
    out_specs += [pl.BlockSpec((1, w), lambda i: (0, 0)) for w in reds]
    out_shape = [jax.ShapeDtypeStruct((s, w), d) for w, d in outs]
    out_shape += [jax.ShapeDtypeStruct((1, w), F32) for w in reds]
    return pl.pallas_call(
        body, grid=(n_tiles,), in_specs=in_specs, out_specs=out_specs, out_shape=out_shape,
        compiler_params=_cparams(("arbitrary",)), name=name)(*[w[0] for w in wins], *params)


def _rms(x, g):
    r = lax.rsqrt(jnp.mean(x * x, axis=-1, keepdims=True) + NORM_EPS)
    return x * r * g, r


def _rms_bwd(x, g, dy):
    r = lax.rsqrt(jnp.mean(x * x, axis=-1, keepdims=True) + NORM_EPS)
    gy = dy * g
    dx = r * gy - x * (r * r * r) * jnp.mean(x * gy, axis=-1, keepdims=True)
    dg = jnp.sum(dy * (x * r), axis=0, keepdims=True)
    return dx, dg


def _sigmoid(x):
    return 1.0 / (1.0 + jnp.exp(-x))


def _split3(x):
    hi = x.astype(BF16).astype(F32)
    r = x - hi
    mid = r.astype(BF16).astype(F32)
    lo = (r - mid).astype(BF16).astype(F32)
    return hi, mid, lo


def _lane(shape):
    return lax.broadcasted_iota(jnp.int32, shape, 1)


def _put3(blk, lane, pos, pieces):
    for k, piece in enumerate(pieces):
        blk = jnp.where(lane == pos + k, piece, blk)
    return blk


def _lane_column(blk, lane, pos):
    return jnp.sum(jnp.where(lane == pos, blk, 0.0), axis=1, keepdims=True)


def _blocks(a, nh):
    return [a[:, h * LANES:(h + 1) * LANES] for h in range(nh)]


def _head_spread(nh, d):
    shift = d.bit_length() - 1
    assert 1 << shift == d
    r = lax.broadcasted_iota(jnp.int32, (nh * d, nh * LANES), 0)
    c = lax.broadcasted_iota(jnp.int32, (nh * d, nh * LANES), 1)
    return jnp.where(c == (r >> shift) * LANES + (r & (d - 1)), 1.0, 0.0).astype(BF16)


def _rope_block(x, c, sa, sb):
    return x * c + pltpu.roll(x, LANES - HALF_ROPE, 1) * sa + pltpu.roll(x, HALF_ROPE, 1) * sb


def _forget_cumsum(z, cb):
    s = z.shape[0]
    ts = _divisor_tile(s, 512, LANES)

    def body(x_ref, col_ref, carry):
        @pl.when(pl.program_id(0) == 0)
        def _():
            carry[...] = jnp.zeros_like(carry)

        x = x_ref[...]
        lf = jnp.minimum(x, 0.0) - jnp.log1p(jnp.exp(-jnp.abs(x)))
        r = lax.broadcasted_iota(jnp.int32, (ts, ts), 0)
        c = lax.broadcasted_iota(jnp.int32, (ts, ts), 1)
        tri = jnp.where(c <= r, 1.0, 0.0).astype(F32)
        col_ref[...] = jnp.dot(tri, lf, preferred_element_type=F32, precision=lax.Precision.HIGHEST) + carry[...]
        carry[...] += jnp.sum(lf, axis=0, keepdims=True)

    return pl.pallas_call(
        body, grid=(s // ts,),
        in_specs=[pl.BlockSpec((ts, LANES), lambda i: (i, cb))],
        out_specs=pl.BlockSpec((ts, LANES), lambda i: (i, 0)),
        out_shape=jax.ShapeDtypeStruct((s, LANES), F32),
        scratch_shapes=[pltpu.VMEM((1, LANES), F32)],
        compiler_params=_cparams(("arbitrary",)), name="forget_cumsum")(z)


def _forget_grad(dq_blocks, dk_blocks, lane):
    df = jnp.zeros(dq_blocks[0].shape, F32)
    for h, (dq_blk, dk_blk) in enumerate(zip(dq_blocks, dk_blocks, strict=True)):
        d_h = _lane_column(dq_blk, lane, FOX_Q_F) - _lane_column(dk_blk, lane, FOX_Q_ONES)
        df = jnp.where(lane == h, d_h, df)
    return df


def _forget_cumsum_bwd(df, z, cb):
    s = z.shape[0]
    ts = _divisor_tile(s, 512, LANES)
    nt = s // ts

    def body(df_ref, x_ref, o_ref, carry):
        @pl.when(pl.program_id(0) == 0)
        def _():
            carry[...] = jnp.zeros_like(carry)

        df = df_ref[...]
        r = lax.broadcasted_iota(jnp.int32, (ts, ts), 0)
        c = lax.broadcasted_iota(jnp.int32, (ts, ts), 1)
        tri = jnp.where(c >= r, 1.0, 0.0).astype(F32)
        rc = jnp.dot(tri, df, preferred_element_type=F32, precision=lax.Precision.HIGHEST) + carry[...]
        carry[...] += jnp.sum(df, axis=0, keepdims=True)
        o_ref[...] = (rc * (1.0 / (1.0 + jnp.exp(x_ref[...])))).astype(o_ref.dtype)

    return pl.pallas_call(
        body, grid=(nt,),
        in_specs=[pl.BlockSpec((ts, LANES), lambda i: (nt - 1 - i, 0)),
                  pl.BlockSpec((ts, LANES), lambda i: (nt - 1 - i, cb))],
        out_specs=pl.BlockSpec((ts, LANES), lambda i: (nt - 1 - i, 0)),
        out_shape=jax.ShapeDtypeStruct((s, LANES), BF16),
        scratch_shapes=[pltpu.VMEM((1, LANES), F32)],
        compiler_params=_cparams(("arbitrary",)), name="forget_cumsum_bwd")(df, z)


def _flash_fwd(q, k, v, scale, nh, l_lane, *, name):
    s = q.shape[0]
    t = min(ATT_TILE, s)
    half = t // 2 if t % (2 * LANES) == 0 else t
    nq = s // t
    c = scale * LOG2E

    def body(q_ref, k_ref, v_ref, o_ref, qb_ref):
        i = pl.program_id(1)
        qb = q_ref[...]

        def scores(q_rows, k0, nk):
            kb = k_ref[pl.ds(pl.multiple_of(k0, half), nk), :]
            return lax.dot_general(q_rows, kb, _NT, preferred_element_type=F32)

        def update(sc, k0, nk, carry):
            m, acc = carry
            m_new = jnp.maximum(m, jnp.max(sc, axis=1, keepdims=True))
            p = jnp.exp2((sc - m_new) * c)
            alpha = jnp.exp2((m - m_new) * c)
            vb = v_ref[pl.ds(pl.multiple_of(k0, half), nk), :]
            acc = alpha * acc + jnp.dot(p.astype(BF16), vb, preferred_element_type=F32)
            return m_new, acc

        def full_block(j, cr):
            return update(scores(qb, j * t, t), j * t, t, cr)

        def group(jj, cr):
            for n in range(FWD_GROUP):
                cr = full_block(FWD_GROUP * jj + n, cr)
            return cr

        def causal(sc):
            row = lax.broadcasted_iota(jnp.int32, sc.shape, 0)
            col = lax.broadcasted_iota(jnp.int32, sc.shape, 1)
            return jnp.where(col <= row, sc, NEG)

        init = (jnp.full((t, 1), NEG, F32), jnp.zeros((t, LANES), F32))
        n_groups = i >> FWD_GROUP_LOG2
        carry = lax.fori_loop(0, n_groups, group, init)
        carry = lax.fori_loop(n_groups * FWD_GROUP, i, full_block, carry)
        m, acc = update(causal(scores(qb, i * t, t)), i * t, t, carry)
        lane = _lane((t, LANES))
        l = _lane_column(acc, lane, V_ONES)
        o_ref[...] = (acc / l).astype(o_ref.dtype)
        big_l = m + jnp.log(l) / scale
        qb_ref[...] = _put3(qb.astype(F32), lane, l_lane, _split3(-big_l)).astype(qb_ref.dtype)

    head_rows = pl.BlockSpec((t, LANES), lambda h, i: (i, h))
    head_all = pl.BlockSpec((s, LANES), lambda h, i: (0, h))
    return pl.pallas_call(
        body, grid=(nh, nq), in_specs=[head_rows, head_all, head_all], out_specs=[head_rows, head_rows],
        out_shape=[jax.ShapeDtypeStruct(q.shape, BF16), jax.ShapeDtypeStruct(q.shape, BF16)],
        compiler_params=_cparams(("parallel", "arbitrary")), name=name)(q, k, v)


def _flash_bwd(qb, k, v, do, scale, nh, n_feat, *, name):
    s = qb.shape[0]
    t = min(ATT_TILE, s)
    half = t // 2 if t % (2 * LANES) == 0 else t
    nq = s // t
    c = scale * LOG2E

    def body(q_ref, k_ref, v_ref, do_ref, dk_ref, dv_ref, dq_ref):
        j = pl.program_id(1)
        kb = k_ref[...]
        vb = v_ref[...]

        @pl.when(j == 0)
        def _():
            dq_ref[...] = jnp.zeros_like(dq_ref)

        def part(q0, n_q, n_k, carry, q_off):
            dk_acc, dv_acc = carry
            rows = pl.ds(pl.multiple_of(q0, half), n_q)
            qblk = q_ref[rows, :]
            dob = do_ref[rows, :]
            kbb, vbb = kb[:n_k], vb[:n_k]
            st = lax.dot_general(kbb, qblk, _NT, preferred_element_type=F32)
            if q_off is not None:
                key = lax.broadcasted_iota(jnp.int32, st.shape, 0)
                qry = lax.broadcasted_iota(jnp.int32, st.shape, 1) + q_off
                st = jnp.where(key <= qry, st, NEG)
            pt = jnp.exp2(st * c)
            dv_new = jnp.dot(pt.astype(BF16), dob, preferred_element_type=F32)
            dpt = lax.dot_general(vbb, dob, _NT, preferred_element_type=F32)
            dsb = (pt * dpt).astype(BF16)
            dk_new = jnp.dot(dsb, qblk, preferred_element_type=F32)
            dq_ref[rows, :] += lax.dot_general(dsb, kbb, _TN, preferred_element_type=F32)
            if n_k == t:
                return dk_acc + dk_new, dv_acc + dv_new
            return (jnp.concatenate([dk_acc[:n_k] + dk_new, dk_acc[n_k:]], axis=0),
                    jnp.concatenate([dv_acc[:n_k] + dv_new, dv_acc[n_k:]], axis=0))

        def block(i, carry):
            return part(i * t, t, t, carry, None)

        init = (jnp.zeros((t, LANES), F32), jnp.zeros((t, LANES), F32))
        carry = part(j * t + half, t - half, t, init, half) if half < t else part(j * t, t, t, init, 0)
        if half < t:
            carry = part(j * t, half, half, carry, 0)
        rest = nq - 1 - j
        carry = lax.cond((rest & 1) == 1, lambda cr: block(j + 1, cr), lambda cr: cr, carry)
        first = j + 1 + (rest & 1)

        def pair(ii, cr):
            i0 = first + 2 * ii
            return block(i0 + 1, block(i0, cr))

        dk_acc, dv_acc = lax.fori_loop(0, rest >> 1, pair, carry)
        dk_ref[...] = dk_acc * jnp.where(_lane((t, LANES)) < n_feat, scale, 1.0)
        dv_ref[...] = dv_acc.astype(dv_ref.dtype)

        @pl.when(j == nq - 1)
        def _():
            dq_ref[...] = dq_ref[...] * jnp.where(_lane((s, LANES)) < n_feat, scale, 1.0)

    head_rows = pl.BlockSpec((t, LANES), lambda h, j: (j, h))
    head_all = pl.BlockSpec((s, LANES), lambda h, j: (0, h))
    shp = jax.ShapeDtypeStruct(qb.shape, F32)
    return pl.pallas_call(
        body, grid=(nh, nq), in_specs=[head_all, head_rows, head_rows, head_all],
        out_specs=[head_rows, head_rows, head_all],
        out_shape=[shp, jax.ShapeDtypeStruct(qb.shape, BF16), shp],
        compiler_params=_cparams(("parallel", "arbitrary")), name=name)(qb, k, v, do)


def _with_delta(do, ex):
    n_blocks = do.shape[1] // LANES
    lane = _lane((do.shape[0], LANES))
    out = []
    for d_blk, o_blk in zip(_blocks(do, n_blocks), _blocks(ex[0], n_blocks), strict=True):
        delta = jnp.sum(d_blk * o_blk.astype(F32), axis=1, keepdims=True)
        out.append(_put3(d_blk, lane, V_ONES, _split3(-delta)))
    return [jnp.concatenate(out, axis=1)]


def _pad_heads(a, nh):
    d = a.shape[-1] // nh
    a = a.reshape(a.shape[:-1] + (nh, d))
    a = jnp.pad(a, [(0, 0)] * (a.ndim - 1) + [(0, LANES - d)])
    return a.reshape(a.shape[:-2] + (nh * LANES,))


def _unpad_heads(a, nh, d):
    a = a.reshape(a.shape[:-1] + (nh, LANES))[..., :d]
    return a.reshape(a.shape[:-2] + (nh * d,))


def _pad_head_rows(w, nh):
    return _pad_heads(w.T, nh).T


def _unpad_head_rows(g, nh, d):
    return _unpad_heads(g.T, nh, d).T


class _ZLayout:
    def __init__(self, d):
        fw = FOX_HEADS * FOX_DIM
        self.src = {}
        off = 0
        for nm, w in (("cq", MLA_Q_LORA), ("ckv", MLA_KV_LORA), ("kr", MLA_ROPE), ("fq", fw), ("fk", fw),
                      ("fv", fw), ("fl", FOX_HEADS), ("ga", d), ("gb", d)):
            self.src[nm] = (off, w)
            off += w
        self.dst = {}
        off = 0
        for nm, w in (("ga", d), ("gb", d), ("fq", fw), ("fk", fw), ("fv", fw), ("cq", MLA_Q_LORA),
                      ("ckv", MLA_KV_LORA), ("kr", LANES), ("fl", LANES)):
            assert off % w == 0
            self.dst[nm] = (off, w)
            off += w
        self.width = off
        self.split = self.dst["cq"][0]
        assert all((o - self.split) % w == 0 for o, w in self.dst.values() if o >= self.split)

    def to_kernel(self, w):
        def seg(nm):
            off, wd = self.src[nm]
            return w[..., off:off + wd]

        def pad(a, left, total):
            return jnp.pad(a, [(0, 0)] * (a.ndim - 1) + [(left, total - left - a.shape[-1])])

        return jnp.concatenate([seg("ga"), seg("gb"), seg("fq"), seg("fk"), seg("fv"), seg("cq"), seg("ckv"),
                                pad(seg("kr"), MLA_NOPE, LANES), pad(seg("fl"), 0, LANES)], axis=-1)

    def from_kernel(self, g):
        def seg(nm, lo=0, hi=None):
            off, wd = self.dst[nm]
            return g[..., off + lo:off + (wd if hi is None else hi)]

        return jnp.concatenate([seg("cq"), seg("ckv"), seg("kr", MLA_NOPE, MLA_NOPE + MLA_ROPE), seg("fq"), seg("fk"),
                                seg("fv"), seg("fl", 0, FOX_HEADS), seg("ga"), seg("gb")], axis=-1)


def _local_step(x, positions, target, wts, late_weights, send_grads):
    s, d = x.shape
    zl = _ZLayout(d)
    hw = MLA_HEADS * LANES
    assert MLA_HEADS == FOX_HEADS
    scale_mla = (MLA_NOPE + MLA_ROPE) ** -0.5
    scale_fox = FOX_DIM ** -0.5

    inv_freq = ROPE_THETA ** (-jnp.arange(HALF_ROPE, dtype=F32) / HALF_ROPE)
    ang = positions.astype(F32)[:, None] * inv_freq
    cos, sin = jnp.cos(ang), jnp.sin(ang)
    tail = jnp.zeros((s, LANES - MLA_NOPE - MLA_ROPE), F32)
    rc = jnp.concatenate([jnp.ones((s, MLA_NOPE), F32), cos, cos, tail], axis=1)
    ra = jnp.concatenate([jnp.zeros((s, MLA_NOPE), F32), -sin, jnp.zeros((s, HALF_ROPE), F32), tail], axis=1)
    rb = jnp.concatenate([jnp.zeros((s, MLA_NOPE + HALF_ROPE), F32), sin, tail], axis=1)

    w_in = zl.to_kernel(wts["w_in"])
    b_in = zl.to_kernel(wts["b_in"])

    def f_norm_in(ti, pa):
        y, _ = _rms(ti[0], pa[0])
        return [y], []

    (h,) = _rowwise("norm_in", f_norm_in, [x], [wts["ln_pre_mix"]], [(d, BF16)])
    z_lo = _mm(h, w_in[:, :zl.split], bias=b_in[:, :zl.split], out_dtype=BF16, name="proj_in_lo")
    z = _mm(h, w_in[:, zl.split:], bias=b_in[:, zl.split:], out_dtype=F32, name="proj_in_hi")

    def zwin(nm):
        off, wd = zl.dst[nm]
        return (z_lo, off // wd, wd) if off < zl.split else (z, (off - zl.split) // wd, wd)

    wts = {**wts, **late_weights("mixer", z)}
    w_uq = _pad_heads(wts["w_uq"], MLA_HEADS)
    w_ukv = jnp.concatenate([_pad_heads(wts["w_uk"], MLA_HEADS), _pad_heads(wts["w_uv"], MLA_HEADS)], axis=1)
    w_o_mla = _pad_head_rows(wts["w_o_mla"], MLA_HEADS)
    w_o_fox = _pad_head_rows(wts["w_o_fox"], FOX_HEADS)

    def f_mla_norms(ti, pa):
        cqn, _ = _rms(ti[0], pa[0])
        ckvn, _ = _rms(ti[1], pa[1])
        return [cqn, ckvn], []

    cqn, ckvn = _rowwise("mla_norms", f_mla_norms, [zwin("cq"), zwin("ckv")],
                         [wts["q_a_norm"], wts["kv_a_norm"]], [(MLA_Q_LORA, BF16), (MLA_KV_LORA, BF16)])
    qf = _mm(cqn, w_uq, out_dtype=F32, name="proj_uq")
    kv = _mm(ckvn, w_ukv, out_dtype=BF16, name="proj_ukv")

    def f_rope_q(ti, pa):
        xq, c_, a_, b_ = ti
        return [jnp.concatenate([_rope_block(blk, c_, a_, b_) for blk in _blocks(xq, MLA_HEADS)], axis=1)], []

    (q_mla,) = _rowwise("rope_q", f_rope_q, [qf, rc, ra, rb], [], [(hw, BF16)])

    def f_mla_kv(ti, pa):
        kn, vn, kr, c_, a_, b_ = ti
        lane = _lane(kr.shape)
        k_tail = jnp.where((lane >= MLA_Q_L) & (lane < MLA_Q_L + 3), 1.0, _rope_block(kr, c_, a_, b_))
        ones_v = (lane >= V_ONES) & (lane < V_ONES + 3)
        k_out = [jnp.where(lane < MLA_NOPE, blk.astype(F32), k_tail) for blk in _blocks(kn, MLA_HEADS)]
        v_out = [jnp.where(ones_v, 1.0, blk.astype(F32)) for blk in _blocks(vn, MLA_HEADS)]
        return [jnp.concatenate(k_out, axis=1), jnp.concatenate(v_out, axis=1)], []

    k_mla, v_mla = _rowwise("mla_kv", f_mla_kv, [(kv, 0, hw), (kv, 1, hw), zwin("kr"), rc, ra, rb], [],
                            [(hw, BF16), (hw, BF16)])
    o_mla, qb_mla = _flash_fwd(q_mla, k_mla, v_mla, scale_mla, MLA_HEADS, MLA_Q_L, name="mla_fwd")

    fl_cb = zwin("fl")[1]
    fcol = _forget_cumsum(z, fl_cb)

    def f_fox_qkv(ti, pa):
        spread = _head_spread(FOX_HEADS, FOX_DIM)
        fq, fk, fv = (jnp.dot(a, spread, preferred_element_type=F32) for a in ti[:3])
        fc = ti[3]
        lane = _lane(fc.shape)
        ones_q = (lane >= FOX_Q_ONES) & (lane < FOX_Q_ONES + 3)
        ones_k = (lane >= FOX_Q_F) & (lane < FOX_Q_ONES)
        ones_v = (lane >= V_ONES) & (lane < V_ONES + 3)
        q_out, k_out, v_out = [], [], []
        for hh, (qblk, kblk, vblk) in enumerate(zip(_blocks(fq, FOX_HEADS), _blocks(fk, FOX_HEADS),
                                                    _blocks(fv, FOX_HEADS), strict=True)):
            f_h = _lane_column(fc, lane, hh) * (1.0 / scale_fox)
            q_out.append(_put3(jnp.where(ones_q, 1.0, qblk), lane, FOX_Q_F, _split3(f_h)))
            k_out.append(_put3(jnp.where(ones_k, 1.0, kblk), lane, FOX_Q_ONES, _split3(-f_h)))
            v_out.append(jnp.where(ones_v, 1.0, vblk))
        return [jnp.concatenate(q_out, axis=1), jnp.concatenate(k_out, axis=1), jnp.concatenate(v_out, axis=1)], []

    q_fox, k_fox, v_fox = _rowwise("fox_qkv", f_fox_qkv, [zwin("fq"), zwin("fk"), zwin("fv"), fcol],
                                   [], [(hw, BF16)] * 3)
    o_fox, qb_fox = _flash_fwd(q_fox, k_fox, v_fox, scale_fox, FOX_HEADS, FOX_Q_L, name="fox_fwd")

    y_mla = _mm(o_mla, w_o_mla, out_dtype=BF16, name="proj_o_mla")
    def gate(yb, ex):
        ga, gb, ya = (a.astype(F32) for a in ex)
        return [yb, _sigmoid(ga) * ya + _sigmoid(gb) * yb]

    y_fox, merged = _mm(o_fox, w_o_fox, out_dtype=[BF16, BF16], name="proj_o_fox",
                        extras=[(z_lo, zl.dst["ga"][0]), (z_lo, zl.dst["gb"][0]), y_mla], epilogue=gate)
    wts = {**wts, **late_weights("mlp", merged)}
    mix = _mm(merged, wts["w_out"], out_dtype=F32, name="proj_out")

    def f_resid1(ti, pa):
        xa, mx = ti
        y, _ = _rms(mx, pa[0])
        x1 = xa + y
        h2, _ = _rms(x1, pa[1])
        return [x1, h2], []

    x1, h2 = _rowwise("resid_mix", f_resid1, [x, mix], [wts["ln_post_mix"], wts["ln_pre_mlp"]], [(d, F32), (d, BF16)])

    def relu2(acc, ex):
        r = jnp.maximum(acc, 0.0)
        return [acc, r * r]

    u, act = _mm(h2, wts["w_ff1"], out_dtype=[BF16, BF16], name="ff1", epilogue=relu2)
    mo = _mm(act, wts["w_ff2"], out_dtype=F32, name="ff2")

    def f_loss(ti, pa):
        xa, mv, tg = ti
        y, _ = _rms(mv, pa[0])
        err = (xa + y) - tg
        g2 = err / d
        dmo, dg = _rms_bwd(mv, pa[0], g2)
        return [g2, dmo], [jnp.sum(err * err, axis=0, keepdims=True), dg]

    g2, d_mo, loss_cols, g_ln_post_mlp = _rowwise("loss", f_loss, [x1, mo, target], [wts["ln_post_mlp"]],
                                                  [(d, F32), (d, BF16)], [d, d])
    loss = 0.5 * jnp.sum(loss_cols) / d

    grads = {"ln_post_mlp": g_ln_post_mlp}
    grads["w_ff2"] = _mm_tn(act, d_mo, name="grad_ff2")

    def relu2_bwd(acc, ex):
        return [acc * (2.0 * jnp.maximum(ex[0], 0.0))]

    (d_u,) = _mm(d_mo, wts["w_ff2"], out_dtype=[BF16], name="ff2_bwd", transpose_b=True, extras=[u], epilogue=relu2_bwd)
    grads["w_ff1"] = _mm_tn(h2, d_u, name="grad_ff1")
    d_h2 = _mm(d_u, wts["w_ff1"], out_dtype=F32, name="ff1_bwd", transpose_b=True)

    def f_resid1_bwd(ti, pa):
        gres, dh2, x1v, mx = ti
        dx1n, dg_pre_mlp = _rms_bwd(x1v, pa[1], dh2)
        dx1 = gres + dx1n
        dmix, dg_post_mix = _rms_bwd(mx, pa[0], dx1)
        return [dx1, dmix], [dg_post_mix, dg_pre_mlp]

    d_x1, d_mix, grads["ln_post_mix"], grads["ln_pre_mlp"] = _rowwise(
        "resid_mix_bwd", f_resid1_bwd, [g2, d_h2, x1, mix], [wts["ln_post_mix"], wts["ln_pre_mlp"]],
        [(d, F32), (d, BF16)], [d, d])
    grads["w_out"] = _mm_tn(merged, d_mix, name="grad_out")
    def gate_bwd(dm, ex):
        ga, gb, ya, yb = (a.astype(F32) for a in ex)
        sa, sb = _sigmoid(ga), _sigmoid(gb)
        return [dm * sa, dm * sb, dm * ya * (sa * (1.0 - sa)), dm * yb * (sb * (1.0 - sb))]

    d_ya, d_yb, d_ga, d_gb = _mm(d_mix, wts["w_out"], out_dtype=[BF16] * 4, name="proj_out_bwd", transpose_b=True,
                                 extras=[(z_lo, zl.dst["ga"][0]), (z_lo, zl.dst["gb"][0]), y_mla, y_fox],
                                 epilogue=gate_bwd)
    grads["w_o_mla"] = _unpad_head_rows(_mm_tn(o_mla, d_ya, name="grad_o_mla"), MLA_HEADS, MLA_V)
    grads["w_o_fox"] = _unpad_head_rows(_mm_tn(o_fox, d_yb, name="grad_o_fox"), FOX_HEADS, FOX_DIM)
    w_o_mla = w_o_mla + send_grads("early", {nm: grads[nm] for nm in EARLY_GRADS}).astype(BF16)
    (do_mla,) = _mm(d_ya, w_o_mla, out_dtype=[BF16], name="proj_o_mla_bwd", transpose_b=True, extras=[o_mla],
                    epilogue=_with_delta)
    (do_fox,) = _mm(d_yb, w_o_fox, out_dtype=[BF16], name="proj_o_fox_bwd", transpose_b=True, extras=[o_fox],
                    epilogue=_with_delta)

    dk_mla, dv_mla, dq_mla = _flash_bwd(qb_mla, k_mla, v_mla, do_mla, scale_mla, MLA_HEADS, MLA_NOPE + MLA_ROPE,
                                        name="mla_bwd")
    dk_fox, dv_fox, dq_fox = _flash_bwd(qb_fox, k_fox, v_fox, do_fox, scale_fox, FOX_HEADS, FOX_DIM, name="fox_bwd")

    def f_rope_q_bwd(ti, pa):
        g, c_, a_, b_ = ti
        return [jnp.concatenate([_rope_block(blk, c_, -a_, -b_) for blk in _blocks(g, MLA_HEADS)], axis=1)], []

    (d_qf,) = _rowwise("rope_q_bwd", f_rope_q_bwd, [dq_mla, rc, ra, rb], [], [(hw, BF16)])
    grads["w_uq"] = _unpad_heads(_mm_tn(cqn, d_qf, name="grad_uq"), MLA_HEADS, MLA_NOPE + MLA_ROPE)
    d_cqn = _mm(d_qf, w_uq, out_dtype=F32, name="proj_uq_bwd", transpose_b=True)

    def f_mla_kv_bwd(ti, pa):
        gk, gv, c_, a_, b_ = ti
        k_blocks = _blocks(gk, MLA_HEADS)
        tot = k_blocks[0]
        for blk in k_blocks[1:]:
            tot = tot + blk
        return [jnp.concatenate([gk, gv], axis=1), _rope_block(tot, c_, -a_, -b_)], []

    d_kv, d_kr = _rowwise("mla_kv_bwd", f_mla_kv_bwd, [dk_mla, dv_mla, rc, ra, rb], [], [(2 * hw, BF16), (LANES, BF16)])
    g_ukv = _mm_tn(ckvn, d_kv, name="grad_ukv")
    grads["w_uk"] = _unpad_heads(g_ukv[:, :hw], MLA_HEADS, MLA_NOPE)
    grads["w_uv"] = _unpad_heads(g_ukv[:, hw:], MLA_HEADS, MLA_V)
    d_ckvn = _mm(d_kv, w_ukv, out_dtype=F32, name="proj_ukv_bwd", transpose_b=True)

    def f_mla_norms_bwd(ti, pa):
        cq, ckv, dcqn, dckvn = ti
        dcq, dg_q = _rms_bwd(cq, pa[0], dcqn)
        dckv, dg_kv = _rms_bwd(ckv, pa[1], dckvn)
        return [dcq, dckv], [dg_q, dg_kv]

    d_cq, d_ckv, grads["q_a_norm"], grads["kv_a_norm"] = _rowwise(
        "mla_norms_bwd", f_mla_norms_bwd, [zwin("cq"), zwin("ckv"), d_cqn, d_ckvn],
        [wts["q_a_norm"], wts["kv_a_norm"]], [(MLA_Q_LORA, BF16), (MLA_KV_LORA, BF16)], [MLA_Q_LORA, MLA_KV_LORA])

    def f_fox_compact(ti, pa):
        gather = _head_spread(FOX_HEADS, FOX_DIM)
        df = _forget_grad(_blocks(ti[0], FOX_HEADS), _blocks(ti[1], FOX_HEADS), _lane((ti[0].shape[0], LANES)))
        return [lax.dot_general(a.astype(BF16), gather, _NT, preferred_element_type=F32) for a in ti] + [df], []

    d_fq, d_fk, d_fv, d_fcum = _rowwise("fox_compact", f_fox_compact, [dq_fox, dk_fox, dv_fox], [],
                                        [(FOX_HEADS * FOX_DIM, BF16)] * 3 + [(LANES, F32)])
    d_fl = _forget_cumsum_bwd(d_fcum, z, fl_cb)
    d_z = jnp.concatenate([d_ga, d_gb, d_fq, d_fk, d_fv, d_cq, d_ckv, d_kr, d_fl], axis=1)
    assert d_z.shape[1] == zl.width

    def f_bias(ti, pa):
        return [], [jnp.sum(ti[0].astype(F32), axis=0, keepdims=True)]

    (g_b_in,) = _rowwise("grad_b_in", f_bias, [d_z], [], [], [zl.width])
    grads["b_in"] = zl.from_kernel(g_b_in)
    grads["w_in"] = zl.from_kernel(_mm_tn(h, d_z, name="grad_in"))
    tok = send_grads("late", {nm: grads[nm] for nm, _ in BIG if nm not in EARLY_GRADS})
    d_h = _mm(d_z, w_in, bias=jnp.zeros((1, d), F32) + tok, out_dtype=F32, name="proj_in_bwd", transpose_b=True)

    def f_norm_in_bwd(ti, pa):
        dx1v, dh, xa = ti
        dxn, dg = _rms_bwd(xa, pa[0], dh)
        return [dx1v + dxn], [dg]

    grad_x, grads["ln_pre_mix"] = _rowwise("norm_in_bwd", f_norm_in_bwd, [d_x1, d_h, x], [wts["ln_pre_mix"]],
                                           [(d, F32)], [d])
    return loss, grad_x, grads


class _PackLayout:
    def __init__(self, shapes):
        self.shapes = list(shapes)
        self.width = _round_up(max(b for _, b in shapes), LANES)
        self.bands = []
        row = 0
        shelf = []
        for idx, (a, b) in enumerate(shapes):
            if 2 * _round_up(b, LANES) > self.width:
                self.bands.append((row, _round_up(a, 32), [(idx, 0)]))
                row += _round_up(a, 32)
            else:
                shelf.append(idx)
        col, items = 0, []
        for idx in shelf:
            wb = _round_up(shapes[idx][1], LANES)
            if col + wb > self.width:
                hgt = max(_round_up(shapes[i][0], 32) for i, _ in items)
                self.bands.append((row, hgt, items))
                row += hgt
                col, items = 0, []
            items.append((idx, col))
            col += wb
        if items:
            hgt = max(_round_up(shapes[i][0], 32) for i, _ in items)
            self.bands.append((row, hgt, items))
            row += hgt
        self.rows = _round_up(row, 16 * LOCAL_PIECES)

    def pack(self, arrs):
        lead = arrs[0].shape[:-2]
        no_pad = [(0, 0)] * len(lead)
        bands = []
        for _, hgt, items in self.bands:
            parts = []
            for k, (idx, col) in enumerate(items):
                a, b = self.shapes[idx]
                nxt = items[k + 1][1] if k + 1 < len(items) else self.width
                parts.append(jnp.pad(arrs[idx], no_pad + [(0, hgt - a), (0, nxt - col - b)]))
            bands.append(parts[0] if len(parts) == 1 else jnp.concatenate(parts, axis=-1))
        used = sum(hgt for _, hgt, _ in self.bands)
        if used < self.rows:
            bands.append(jnp.zeros(lead + (self.rows - used, self.width), arrs[0].dtype))
        return jnp.concatenate(bands, axis=-2)

    def unpack(self, packed):
        out = [None] * len(self.shapes)
        for row, _, items in self.bands:
            for idx, col in items:
                a, b = self.shapes[idx]
                out[idx] = packed[..., row:row + a, col:col + b]
        return out


def _to_shards(g, axis):
    if axis == 0:
        return g.reshape(N_CHIPS, g.shape[0] // N_CHIPS, g.shape[1])
    return jnp.stack(jnp.split(g, N_CHIPS, axis=1))


def _from_shards(s4, axis):
    n, a, b = s4.shape
    if axis == 0:
        return s4.reshape(n * a, b)
    return jnp.concatenate([s4[ch] for ch in range(n)], axis=1)


ANY = pl.BlockSpec(memory_space=pl.ANY)


def _place():
    return lax.axis_index("x"), lax.axis_index("y"), lax.axis_index("c")


def _gather_weights(wpk):
    rows, wd = wpk.shape
    half = rows // 2

    def body(w_ref, out_ref, send_sems, recv_sems, local_sems):
        x, y, c = _place()
        sibling = (x, y, 1 - c)
        chips = [(1 - x, y), (x, 1 - y), (1 - x, 1 - y)]

        def slab(chip, hf):
            return out_ref.at[2 * chip[0] + chip[1], pl.ds(hf * half, half), :]

        def copy(k, chip, hf, to, src=None):
            return pltpu.make_async_remote_copy(
                src_ref=slab(chip, hf) if src is None else src, dst_ref=slab(chip, hf),
                send_sem=send_sems.at[k], recv_sem=recv_sems.at[k], device_id=to, device_id_type=MESH)

        piece = rows // LOCAL_PIECES
        mine = [pltpu.make_async_copy(w_ref.at[pl.ds(n * piece, piece), :],
                                      out_ref.at[2 * x + y, pl.ds(n * piece, piece), :], local_sems.at[n])
                for n in range(LOCAL_PIECES)]
        for cp in mine:
            cp.start()
        first = [copy(j, (x, y), c, (*chip, c), src=w_ref.at[pl.ds(c * half, half), :]) for j, chip in enumerate(chips)]
        for cp in first:
            cp.start()
        passed = [copy(3 + j, chip, c, sibling) for j, chip in enumerate(chips)]
        for j, chip in enumerate(chips):
            copy(j, chip, c, (x, y, c)).wait_recv()
            passed[j].start()
        for j, chip in enumerate(chips):
            copy(3 + j, chip, 1 - c, (x, y, c)).wait_recv()
        for cp in first + passed:
            cp.wait_send()
        for cp in mine:
            cp.wait()

    assert rows % (16 * LOCAL_PIECES) == 0
    return pl.pallas_call(
        body, out_shape=jax.ShapeDtypeStruct((N_CHIPS, rows, wd), wpk.dtype),
        in_specs=[ANY], out_specs=ANY,
        scratch_shapes=[pltpu.SemaphoreType.DMA((6,)), pltpu.SemaphoreType.DMA((6,)),
                        pltpu.SemaphoreType.DMA((LOCAL_PIECES,))],
        name="gather_weights")(wpk)


HBM = pl.BlockSpec(memory_space=pltpu.HBM)
SEM = pl.BlockSpec(memory_space=pltpu.SEMAPHORE)
EFFECT = pltpu.SideEffectType.DATAFLOW_SIDE_EFFECTING
N_LATE = 6


def _gather_late_start(wpk, tag):
    rows, wd = wpk.shape
    half = rows // 2

    def body(w_ref, land_ref, send_sems, recv_sems, w_thru, land_thru, token):
        x, y, c = _place()
        chips = [(1 - x, y), (x, 1 - y), (1 - x, 1 - y)]
        for j, chip in enumerate(chips):
            for to_core in range(2):
                pltpu.make_async_remote_copy(
                    src_ref=w_ref.at[pl.ds(c * half, half), :],
                    dst_ref=land_ref.at[2 * x + y, pl.ds(c * half, half), :],
                    send_sem=send_sems.at[2 * j + to_core], recv_sem=recv_sems.at[2 * j + c],
                    device_id=(*chip, to_core), device_id_type=MESH).start()
        token[...] = jnp.zeros_like(token)

    land = lax.empty((N_CHIPS, rows, wd), wpk.dtype)
    return pl.pallas_call(
        body, name="gather_" + tag + "_start",
        out_shape=(pltpu.SemaphoreType.DMA((N_LATE,)), pltpu.SemaphoreType.DMA((N_LATE,)),
                   pltpu.HBM(wpk.shape, wpk.dtype), pltpu.HBM(land.shape, land.dtype),
                   jax.ShapeDtypeStruct((8, LANES), F32)),
        in_specs=(HBM, HBM), out_specs=(SEM, SEM, HBM, HBM, pl.BlockSpec(memory_space=pltpu.VMEM)),
        input_output_aliases={0: 2, 1: 3},
        compiler_params=pltpu.CompilerParams(has_side_effects=EFFECT),
    )(pltpu.with_memory_space_constraint(wpk, pltpu.HBM), pltpu.with_memory_space_constraint(land, pltpu.HBM))


def _gather_late_wait(send_sems, recv_sems, w_thru, land_thru, after, tag):
    rows, wd = w_thru.shape
    half = rows // 2

    def body(w_ref, land_ref, send_sems, recv_sems, after_ref, w_dead, land_out):
        x, y, c = _place()
        for n in range(N_LATE):
            cp = pltpu.make_async_remote_copy(
                src_ref=w_ref.at[pl.ds(0, half), :], dst_ref=land_ref.at[0, pl.ds(0, half), :],
                send_sem=send_sems.at[n], recv_sem=recv_sems.at[n], device_id=(x, y, c), device_id_type=MESH)
            cp.wait_send()
            cp.wait_recv()

    return pl.pallas_call(
        body, name="gather_" + tag + "_wait",
        out_shape=(pltpu.HBM(w_thru.shape, w_thru.dtype), pltpu.HBM(land_thru.shape, land_thru.dtype)),
        in_specs=(HBM, HBM, SEM, SEM, ANY), out_specs=(HBM, HBM), input_output_aliases={0: 0, 1: 1},
        compiler_params=pltpu.CompilerParams(has_side_effects=EFFECT),
    )(w_thru, land_thru, send_sems, recv_sems, after)[1]


N_PART = 7


def _reduce_start(gbf, tag):
    _, _, hrows, wd = gbf.shape

    def body(g_ref, land_ref, send_sems, recv_sems, g_thru, land_thru, token):
        x, y, c = _place()
        chips = [(1 - x, y), (x, 1 - y), (1 - x, 1 - y)]
        for j, chip in enumerate(chips):
            for to_core in range(2):
                pltpu.make_async_remote_copy(
                    src_ref=g_ref.at[2 * chip[0] + chip[1], to_core], dst_ref=land_ref.at[2 * j + c],
                    send_sem=send_sems.at[2 * j + to_core], recv_sem=recv_sems.at[2 * j + c],
                    device_id=(*chip, to_core), device_id_type=MESH).start()
        pltpu.make_async_remote_copy(
            src_ref=g_ref.at[2 * x + y, 1 - c], dst_ref=land_ref.at[N_PART - 1],
            send_sem=send_sems.at[N_PART - 1], recv_sem=recv_sems.at[N_PART - 1],
            device_id=(x, y, 1 - c), device_id_type=MESH).start()
        token[...] = jnp.zeros_like(token)

    land = lax.empty((N_PART, hrows, wd), gbf.dtype)
    return pl.pallas_call(
        body, name="reduce_" + tag + "_start",
        out_shape=(pltpu.SemaphoreType.DMA((N_PART,)), pltpu.SemaphoreType.DMA((N_PART,)),
                   pltpu.HBM(gbf.shape, gbf.dtype), pltpu.HBM(land.shape, land.dtype),
                   jax.ShapeDtypeStruct((8, LANES), F32)),
        in_specs=(HBM, HBM), out_specs=(SEM, SEM, HBM, HBM, pl.BlockSpec(memory_space=pltpu.VMEM)),
        input_output_aliases={0: 2, 1: 3},
        compiler_params=pltpu.CompilerParams(has_side_effects=EFFECT),
    )(pltpu.with_memory_space_constraint(gbf, pltpu.HBM), pltpu.with_memory_space_constraint(land, pltpu.HBM))


def _reduce_wait(send_sems, recv_sems, g_thru, land_thru, after, tag):
    def body(g_ref, land_ref, send_sems, recv_sems, after_ref, g_dead, land_out):
        x, y, c = _place()
        for n in range(N_PART):
            cp = pltpu.make_async_remote_copy(
                src_ref=g_ref.at[0, 0], dst_ref=land_ref.at[0], send_sem=send_sems.at[n], recv_sem=recv_sems.at[n],
                device_id=(x, y, c), device_id_type=MESH)
            cp.wait_send()
            cp.wait_recv()

    return pl.pallas_call(
        body, name="reduce_" + tag + "_wait",
        out_shape=(pltpu.HBM(g_thru.shape, g_thru.dtype), pltpu.HBM(land_thru.shape, land_thru.dtype)),
        in_specs=(HBM, HBM, SEM, SEM, ANY), out_specs=(HBM, HBM), input_output_aliases={0: 0, 1: 1},
        compiler_params=pltpu.CompilerParams(has_side_effects=EFFECT),
    )(g_thru, land_thru, send_sems, recv_sems, after)[1]


def _sibling_swap(mine):
    def body(m_ref, out_ref, send_sem, recv_sem):
        x, y, c = _place()
        cp = pltpu.make_async_remote_copy(
            src_ref=m_ref, dst_ref=out_ref, send_sem=send_sem, recv_sem=recv_sem,
            device_id=(x, y, 1 - c), device_id_type=MESH)
        cp.start()
        cp.wait()

    return pl.pallas_call(
        body, out_shape=jax.ShapeDtypeStruct(mine.shape, mine.dtype), in_specs=[ANY], out_specs=ANY,
        scratch_shapes=[pltpu.SemaphoreType.DMA, pltpu.SemaphoreType.DMA], name="grad_sibling_swap")(mine)


def _adamw(w, g, m, v):
    m = ADAM_B1 * m + (1.0 - ADAM_B1) * g
    v = ADAM_B2 * v + (1.0 - ADAM_B2) * (g * g)
    m_hat = m / (1.0 - ADAM_B1 ** ADAM_STEP)
    v_hat = v / (1.0 - ADAM_B2 ** ADAM_STEP)
    delta = -ADAM_LR * (m_hat / (jnp.sqrt(v_hat) + ADAM_EPS) + ADAM_WD * w)
    return delta, m, v


def _small_allreduce_adamw(gs, ws, ms, vs):
    n_dev = 8
    n_par = len(gs)
    wd = PACK_W
    chunks = []
    for p, g in enumerate(gs):
        for off in range(0, g.shape[1], wd):
            chunks.append((p, len(chunks), off, min(wd, g.shape[1] - off)))
    rows = _round_up(len(chunks), 8)

    def body(*refs):
        g_refs, w_refs, m_refs, v_refs = (refs[k * n_par:(k + 1) * n_par] for k in range(4))
        go_refs, d_refs, mo_refs, vo_refs = (refs[(4 + k) * n_par:(5 + k) * n_par] for k in range(4))
        mine_ref, all_ref, send_sems, recv_sems = refs[8 * n_par:]
        x, y, c = _place()
        me, sibling = (x, y, c), (x, y, 1 - c)
        chips = [(1 - x, y), (x, 1 - y), (1 - x, 1 - y)]

        def slot(px, py, pc):
            return all_ref.at[4 * px + 2 * py + pc]

        def copy(k, block, to, src=None):
            return pltpu.make_async_remote_copy(
                src_ref=slot(*block) if src is None else src, dst_ref=slot(*block),
                send_sem=send_sems.at[k], recv_sem=recv_sems.at[k], device_id=to, device_id_type=MESH)

        mine_ref[...] = jnp.zeros_like(mine_ref)
        for p, row, off, width in chunks:
            mine_ref[row:row + 1, 0:width] = g_refs[p][:, off:off + width]
        all_ref[4 * x + 2 * y + c] = mine_ref[...]
        first = [copy(0, me, sibling, src=mine_ref)]
        first += [copy(1 + j, me, (*chip, c), src=mine_ref) for j, chip in enumerate(chips)]
        for cp in first:
            cp.start()
        passed = [copy(4 + j, (*chip, c), sibling) for j, chip in enumerate(chips)]
        for j, chip in enumerate(chips):
            copy(1 + j, (*chip, c), me).wait_recv()
            passed[j].start()
        copy(0, sibling, me).wait_recv()
        for j, chip in enumerate(chips):
            copy(4 + j, (*chip, 1 - c), me).wait_recv()
        for cp in first + passed:
            cp.wait_send()
        tot = jnp.zeros((rows, wd), F32)
        for dev in range(n_dev):
            tot = tot + all_ref[dev]
        mine_ref[...] = tot
        for p, row, off, width in chunks:
            cols = slice(off, off + width)
            g = mine_ref[row:row + 1, 0:width]
            delta, m_new, v_new = _adamw(w_refs[p][:, cols], g, m_refs[p][:, cols], v_refs[p][:, cols])
            go_refs[p][:, cols] = g
            d_refs[p][:, cols] = delta
            mo_refs[p][:, cols] = m_new
            vo_refs[p][:, cols] = v_new

    vm = pl.BlockSpec(memory_space=pltpu.VMEM)
    shp = [jax.ShapeDtypeStruct(g.shape, F32) for g in gs]
    res = pl.pallas_call(
        body, out_shape=shp * 4, in_specs=[vm] * (4 * n_par), out_specs=[vm] * (4 * n_par),
        scratch_shapes=[pltpu.VMEM((rows, wd), F32), pltpu.VMEM((n_dev, rows, wd), F32),
                        pltpu.SemaphoreType.DMA((7,)), pltpu.SemaphoreType.DMA((7,))],
        name="small_allreduce_adamw")(*gs, *ws, *ms, *vs)
    return [res[k * n_par:(k + 1) * n_par] for k in range(4)]


def kernel(x, positions, ln_pre_mix, ln_post_mix, ln_pre_mlp, ln_post_mlp, w_in, b_in, q_a_norm, w_uq, kv_a_norm, w_uk, w_uv, w_o_mla, w_o_fox, w_out, w_ff1, w_ff2, loss_target, m_ln_pre_mix, m_ln_post_mix, m_ln_pre_mlp, m_ln_post_mlp, m_w_in, m_b_in, m_q_a_norm, m_w_uq, m_kv_a_norm, m_w_uk, m_w_uv, m_w_o_mla, m_w_o_fox, m_w_out, m_w_ff1, m_w_ff2, v_ln_pre_mix, v_ln_post_mix, v_ln_pre_mlp, v_ln_post_mlp, v_w_in, v_b_in, v_q_a_norm, v_w_uq, v_kv_a_norm, v_w_uk, v_w_uv, v_w_o_mla, v_w_o_fox, v_w_out, v_w_ff1, v_w_ff2):
    w = dict(ln_pre_mix=ln_pre_mix, ln_post_mix=ln_post_mix, ln_pre_mlp=ln_pre_mlp, ln_post_mlp=ln_post_mlp, w_in=w_in,
             b_in=b_in, q_a_norm=q_a_norm, w_uq=w_uq, kv_a_norm=kv_a_norm, w_uk=w_uk, w_uv=w_uv, w_o_mla=w_o_mla,
             w_o_fox=w_o_fox, w_out=w_out, w_ff1=w_ff1, w_ff2=w_ff2)
    mom = dict(ln_pre_mix=m_ln_pre_mix, ln_post_mix=m_ln_post_mix, ln_pre_mlp=m_ln_pre_mlp, ln_post_mlp=m_ln_post_mlp,
               w_in=m_w_in, b_in=m_b_in, q_a_norm=m_q_a_norm, w_uq=m_w_uq, kv_a_norm=m_kv_a_norm, w_uk=m_w_uk,
               w_uv=m_w_uv, w_o_mla=m_w_o_mla, w_o_fox=m_w_o_fox, w_out=m_w_out, w_ff1=m_w_ff1, w_ff2=m_w_ff2)
    var = dict(ln_pre_mix=v_ln_pre_mix, ln_post_mix=v_ln_post_mix, ln_pre_mlp=v_ln_pre_mlp, ln_post_mlp=v_ln_post_mlp,
               w_in=v_w_in, b_in=v_b_in, q_a_norm=v_q_a_norm, w_uq=v_w_uq, kv_a_norm=v_kv_a_norm, w_uk=v_w_uk,
               w_uv=v_w_uv, w_o_mla=v_w_o_mla, w_o_fox=v_w_o_fox, w_out=v_w_out, w_ff1=v_w_ff1, w_ff2=v_w_ff2)

    big_names = [nm for nm, _ in BIG]
    c = lax.axis_index("c")
    chip = 2 * lax.axis_index("x") + lax.axis_index("y")

    axes = dict(BIG)

    def assemble(names, lay, gathered):
        return {nm: _from_shards(s4, axes[nm] - 1) for nm, s4 in zip(names, lay.unpack(gathered), strict=True)}

    groups = {"mixer": [nm for nm in big_names if nm in MIXER], "mlp": [nm for nm in big_names if nm in LATE]}
    first_names = [nm for nm in big_names if nm not in MIXER and nm not in LATE]
    full = {nm: wv for nm, wv in w.items() if nm in SMALL}
    lay_first = _PackLayout([w[nm].shape[1:] for nm in first_names])
    first = _gather_weights(lay_first.pack([w[nm][0].astype(BF16) for nm in first_names]))
    full.update(assemble(first_names, lay_first, first))
    travelling = {}
    for tag, names in groups.items():
        lay = _PackLayout([w[nm].shape[1:] for nm in names])
        shard = lay.pack([w[nm][0].astype(BF16) for nm in names])
        started = _gather_late_start(lax.optimization_barrier((shard, first))[0], tag)
        travelling[tag] = (names, lay, shard, started[:4])
        full["b_in"] = full["b_in"] + started[4][0, 0]

    def late_weights(tag, after):
        names, lay, shard, handles = travelling[tag]
        land = _gather_late_wait(*handles, after, tag)
        land = lax.dynamic_update_slice(land, shard[None], (chip, 0, 0))
        return assemble(names, lay, land)

    grad_groups = {"early": [nm for nm in big_names if nm in EARLY_GRADS],
                   "late": [nm for nm in big_names if nm not in EARLY_GRADS]}
    sent = {}

    def send_grads(tag, g):
        names = grad_groups[tag]
        lay = _PackLayout([w[nm].shape[1:] for nm in names])
        hrows = lay.rows // 2
        gbf = lay.pack([_to_shards(g[nm], axes[nm] - 1).astype(BF16) for nm in names])
        gbf = gbf.reshape(N_CHIPS, 2, hrows, lay.width)
        started = _reduce_start(gbf, tag)
        own = lax.dynamic_index_in_dim(lax.dynamic_index_in_dim(started[2], chip, axis=0, keepdims=False), c, axis=0,
                                       keepdims=False)
        sent[tag] = (names, lay, started[:4], own)
        return started[4][0, 0]

    loss_local, grad_x, grads = _local_step(x[0], positions[0], loss_target[0], full, late_weights, send_grads)

    def f_add8(ti, pa):
        tot = ti[0].astype(F32)
        for part in ti[1:]:
            tot = tot + part.astype(F32)
        return [tot], []

    reduced = []
    for tag, (names, lay, handles, own) in sent.items():
        parts = _reduce_wait(*handles, grad_x, tag)
        reduced.append(_rowwise("grad_add_" + tag, f_add8, [own] + [parts[n] for n in range(N_PART)], [],
                                [(lay.width, F32)])[0])

    assert len({lay.width for _, lay, _, _ in sent.values()}) == 1
    red = jnp.concatenate(reduced, axis=0)
    sib = _sibling_swap(red)
    lower, upper = jnp.where(c == 0, red, sib), jnp.where(c == 0, sib, red)
    g_by_name, row = {}, 0
    for names, lay, _, _ in sent.values():
        hrows = lay.rows // 2
        both = jnp.concatenate([lower[row:row + hrows], upper[row:row + hrows]], axis=0)
        g_by_name.update(zip(names, lay.unpack(both), strict=True))
        row += hrows
    g_shards = [g_by_name[nm] for nm in big_names]

    def f_adamw(ti, pa):
        wv, gv, mv, vv = ti
        return list(_adamw(wv, gv, mv, vv)), []

    out = {"grad": {}, "delta": {}, "m": {}, "v": {}}
    for nm, g_sh in zip(big_names, g_shards, strict=True):
        wd = g_sh.shape[1]
        d_sh, m_sh, v_sh = _rowwise("adamw_" + nm, f_adamw, [w[nm][0], g_sh, mom[nm][0], var[nm][0]], [], [(wd, F32)] * 3)
        out["grad"][nm], out["delta"][nm], out["m"][nm], out["v"][nm] = g_sh[None], d_sh[None], m_sh[None], v_sh[None]

    loss_row = jnp.zeros((1, LANES), F32) + loss_local
    blank = jnp.zeros((1, LANES), F32)
    small = _small_allreduce_adamw([grads[nm] for nm in SMALL] + [loss_row], [w[nm] for nm in SMALL] + [blank],
                                   [mom[nm] for nm in SMALL] + [blank], [var[nm] for nm in SMALL] + [blank])
    for kind, arrs in zip(("grad", "delta", "m", "v"), small, strict=True):
        for nm, arr in zip(SMALL, arrs[:len(SMALL)], strict=True):
            out[kind][nm] = arr
    loss = small[0][len(SMALL)][0, 0]

    return (loss, grad_x[None], *[out["grad"][nm] for nm in ALL_W], *[out["delta"][nm] for nm in ALL_W],
            *[out["m"][nm] for nm in ALL_W], *[out["v"][nm] for nm in ALL_W])
```

```python
import functools
import math

import jax
import jax.numpy as jnp
from jax import lax
from jax.experimental import pallas as pl
from jax.experimental.pallas import tpu as pltpu

F32 = jnp.float32
BF16 = jnp.bfloat16

MLA_HEADS = 8
MLA_Q_LORA = 256
MLA_KV_LORA = 128
MLA_NOPE = 64
MLA_ROPE = 32
MLA_V = 64
FOX_HEADS = 8
FOX_DIM = 64
ROPE_THETA = 10000.0
NORM_EPS = 1e-6
HALF_ROPE = MLA_ROPE // 2

ADAM_LR = 0.001
ADAM_B1 = 0.9
ADAM_B2 = 0.999
ADAM_EPS = 1e-08
ADAM_WD = 0.01
ADAM_STEP = 10

LANES = 128
VMEM_LIMIT = 56 * 1024 * 1024
ATT_TILE = 1024
FWD_GROUP_LOG2 = 1
FWD_GROUP = 1 << FWD_GROUP_LOG2
MM_VMEM_BUDGET = 40 * 1024 * 1024
ROW_TILE_BYTES = 12 * 1024 * 1024
MXU_WIDTH = 256
MXU_MACS_PER_S = 4.98e14
HBM_BYTES_PER_S = 3.2e12
STEP_OVERHEAD_S = 0.35e-6
NEG = -1e30
LOG2E = math.log2(math.e)
MESH = pl.DeviceIdType.MESH

V_ONES = 64
FOX_Q_F = 64
FOX_Q_L = 67
FOX_Q_ONES = 70
MLA_Q_L = 96

BIG = (("w_in", 2), ("w_uq", 2), ("w_uk", 2), ("w_uv", 2), ("w_o_mla", 2), ("w_o_fox", 2),
       ("w_out", 1), ("w_ff1", 2), ("w_ff2", 1))
MIXER = ("w_uq", "w_uk", "w_uv", "w_o_mla", "w_o_fox")
LATE = ("w_out", "w_ff1", "w_ff2")
EARLY_GRADS = ("w_o_mla", "w_o_fox", "w_out", "w_ff1", "w_ff2")
SMALL = ("ln_pre_mix", "ln_post_mix", "ln_pre_mlp", "ln_post_mlp", "b_in", "q_a_norm", "kv_a_norm")
ALL_W = ("ln_pre_mix", "ln_post_mix", "ln_pre_mlp", "ln_post_mlp", "w_in", "b_in", "q_a_norm", "w_uq",
         "kv_a_norm", "w_uk", "w_uv", "w_o_mla", "w_o_fox", "w_out", "w_ff1", "w_ff2")
N_CHIPS = 4
PACK_W = 1024
LOCAL_PIECES = 8

_NT = (((1,), (1,)), ((), ()))
_TN = (((0,), (0,)), ((), ()))


def _cparams(sem=None):
    return pltpu.CompilerParams(dimension_semantics=sem, vmem_limit_bytes=VMEM_LIMIT)


def _divisor_tile(n, limit, mult):
    if n <= limit:
        return n
    best = None
    t = mult
    while t <= limit:
        if n % t == 0:
            best = t
        t += mult
    assert best is not None, (n, limit, mult)
    return best


def _round_up(v, mult):
    return -(-v // mult) * mult


def _mm_tiles(m, k, n, io_bytes):
    best = None
    for tm in (2048, 1024, 512, 256, 128):
        if m % tm:
            continue
        for tn in range(LANES, min(n, 2048) + 1, LANES):
            if n % tn:
                continue
            vmem = 2 * (tm * k * 2 + k * tn * 2 + tm * tn * io_bytes) + tm * tn * 4
            if vmem > MM_VMEM_BUDGET:
                continue
            mxu = m * k * n * (_round_up(tn, MXU_WIDTH) / tn) / MXU_MACS_PER_S
            hbm = (m * k * 2 + (m // tm) * k * n * 2 + m * n * io_bytes) / HBM_BYTES_PER_S
            cost = max(mxu, hbm) + (m // tm) * (n // tn) * STEP_OVERHEAD_S
            if best is None or cost < best[0]:
                best = (cost, tm, tn)
    assert best is not None, (m, k, n)
    return best[1], best[2]


def _mm(a, b, *, out_dtype, name, bias=None, transpose_b=False, extras=(), rows=(), epilogue=None):
    m, k = a.shape
    n = b.shape[0] if transpose_b else b.shape[1]
    assert (b.shape[1] if transpose_b else b.shape[0]) == k and a.dtype == BF16 and b.dtype == BF16
    out_dtypes = list(out_dtype) if isinstance(out_dtype, (list, tuple)) else [out_dtype]
    n_ex = len(extras)
    io_bytes = sum(jnp.dtype(dt).itemsize for dt in out_dtypes) + 4 * n_ex
    if rows:
        tn = n
        tm = next(t for t in (2048, 1024, 512, 256, 128)
                  if m % t == 0 and 2 * (t * k * 2 + k * n * 2 + t * n * io_bytes) + t * n * 4 <= MM_VMEM_BUDGET)
    else:
        tm, tn = _mm_tiles(m, k, n, io_bytes)
    n_ex += len(rows)

    def body(*refs):
        a_ref, b_ref = refs[:2]
        pos = 2
        bias_ref = None
        if bias is not None:
            bias_ref = refs[pos]
            pos += 1
        ex_refs = refs[pos:pos + n_ex]
        o_refs = refs[pos + n_ex:]
        if transpose_b:
            acc = lax.dot_general(a_ref[...], b_ref[...], _NT, preferred_element_type=F32)
        else:
            acc = jnp.dot(a_ref[...], b_ref[...], preferred_element_type=F32)
        if bias_ref is not None:
            acc = acc + bias_ref[...]
        vals = [acc] if epilogue is None else epilogue(acc, [r[...] for r in ex_refs])
        for ref, val in zip(o_refs, vals, strict=True):
            ref[...] = val.astype(ref.dtype)

    b_spec = pl.BlockSpec((tn, k), lambda i, j: (j, 0)) if transpose_b else pl.BlockSpec((k, tn), lambda i, j: (0, j))
    in_specs = [pl.BlockSpec((tm, k), lambda i, j: (i, 0)), b_spec]
    args = [a, b]
    if bias is not None:
        in_specs.append(pl.BlockSpec((1, tn), lambda i, j: (0, j)))
        args.append(bias)
    for ex in extras:
        arr, col0 = ex if isinstance(ex, tuple) else (ex, 0)
        assert col0 % tn == 0
        in_specs.append(pl.BlockSpec((tm, tn), functools.partial(lambda i, j, off: (i, j + off), off=col0 // tn)))
        args.append(arr)
    for row in rows:
        in_specs.append(pl.BlockSpec((1, tn), lambda i, j: (0, j)))
        args.append(row)
    res = pl.pallas_call(
        body, grid=(m // tm, n // tn), in_specs=in_specs,
        out_specs=[pl.BlockSpec((tm, tn), lambda i, j: (i, j)) for _ in out_dtypes],
        out_shape=[jax.ShapeDtypeStruct((m, n), dt) for dt in out_dtypes],
        compiler_params=_cparams(("parallel", "parallel")), name=name)(*args)
    return res if isinstance(out_dtype, (list, tuple)) else res[0]


def _mm_tn(a, b, *, name):
    s, m = a.shape
    s2, n = b.shape
    assert s == s2 and a.dtype == BF16 and b.dtype == BF16
    tm = _divisor_tile(m, 1024, LANES)
    tn = _divisor_tile(n, 2304, LANES)
    tk = _divisor_tile(s, 1024, 16)
    nk = s // tk

    def body(a_ref, b_ref, o_ref, acc_ref):
        @pl.when(pl.program_id(2) == 0)
        def _():
            acc_ref[...] = jnp.zeros_like(acc_ref)

        acc_ref[...] += lax.dot_general(a_ref[...], b_ref[...], _TN, preferred_element_type=F32)

        @pl.when(pl.program_id(2) == nk - 1)
        def _():
            o_ref[...] = acc_ref[...].astype(o_ref.dtype)

    return pl.pallas_call(
        body, grid=(m // tm, n // tn, nk),
        in_specs=[pl.BlockSpec((tk, tm), lambda i, j, k: (k, i)), pl.BlockSpec((tk, tn), lambda i, j, k: (k, j))],
        out_specs=pl.BlockSpec((tm, tn), lambda i, j, k: (i, j)),
        out_shape=jax.ShapeDtypeStruct((m, n), BF16),
        scratch_shapes=[pltpu.VMEM((tm, tn), F32)],
        compiler_params=_cparams(("parallel", "parallel", "arbitrary")), name=name)(a, b)


def _rowwise(name, fn, tiled, params, outs, reds=(), reverse=False):
    wins = [t if isinstance(t, tuple) else (t, 0, t.shape[1]) for t in tiled]
    s = wins[0][0].shape[0]
    row_bytes = sum(w * arr.dtype.itemsize for arr, _, w in wins) + sum(w * jnp.dtype(d).itemsize for w, d in outs)
    ts = _divisor_tile(s, max(16, min(1024, ROW_TILE_BYTES // row_bytes)), 16)
    nt, npar, nout = len(wins), len(params), len(outs)
    n_tiles = s // ts

    def row(i):
        return n_tiles - 1 - i if reverse else i

    def body(*refs):
        tin = [r[...] for r in refs[:nt]]
        par = [r[...] for r in refs[nt:nt + npar]]
        out_refs = refs[nt + npar:nt + npar + nout]
        red_refs = refs[nt + npar + nout:]
        o, r = fn(tin, par)
        for ref, val in zip(out_refs, o, strict=True):
            ref[...] = val.astype(ref.dtype)
        if red_refs:
            @pl.when(pl.program_id(0) == 0)
            def _():
                for ref in red_refs:
                    ref[...] = jnp.zeros_like(ref)

            for ref, val in zip(red_refs, r, strict=True):
                ref[...] += val

    in_specs = [pl.BlockSpec((ts, w), functools.partial(lambda i, cb: (row(i), cb), cb=cb)) for _, cb, w in wins]
    in_specs += [pl.BlockSpec(p.shape, lambda i: (0, 0)) for p in params]
    out_specs = [pl.BlockSpec((ts, w), lambda i: (row(i), 0)) for w, _ in outs]
    out_specs += [pl.BlockSpec((1, w), lambda i: (0, 0)) for w in reds]
    out_shape = [jax.ShapeDtypeStruct((s, w), d) for w, d in outs]
    out_shape += [jax.ShapeDtypeStruct((1, w), F32) for w in reds]
    return pl.pallas_call(
        body, grid=(n_tiles,), in_specs=in_specs, out_specs=out_specs, out_shape=out_shape,
        compiler_params=_cparams(("arbitrary",)), name=name)(*[w[0] for w in wins], *params)


def _rms(x, g):
    r = lax.rsqrt(jnp.mean(x * x, axis=-1, keepdims=True) + NORM_EPS)
    return x * r * g, r


def _rms_bwd(x, g, dy):
    r = lax.rsqrt(jnp.mean(x * x, axis=-1, keepdims=True) + NORM_EPS)
    gy = dy * g
    dx = r * gy - x * (r * r * r) * jnp.mean(x * gy, axis=-1, keepdims=True)
    dg = jnp.sum(dy * (x * r), axis=0, keepdims=True)
    return dx, dg


def _sigmoid(x):
    return 1.0 / (1.0 + jnp.exp(-x))


def _split3(x):
    hi = x.astype(BF16).astype(F32)
    r = x - hi
    mid = r.astype(BF16).astype(F32)
    lo = (r - mid).astype(BF16).astype(F32)
    return hi, mid, lo


def _lane(shape):
    return lax.broadcasted_iota(jnp.int32, shape, 1)


def _put3(blk, lane, pos, pieces):
    for k, piece in enumerate(pieces):
        blk = jnp.where(lane == pos + k, piece, blk)
    return blk


def _lane_column(blk, lane, pos):
    return jnp.sum(jnp.where(lane == pos, blk, 0.0), axis=1, keepdims=True)


def _blocks(a, nh):
    return [a[:, h * LANES:(h + 1) * LANES] for h in range(nh)]


def _head_spread(nh, d):
    shift = d.bit_length() - 1
    assert 1 << shift == d
    r = lax.broadcasted_iota(jnp.int32, (nh * d, nh * LANES), 0)
    c = lax.broadcasted_iota(jnp.int32, (nh * d, nh * LANES), 1)
    return jnp.where(c == (r >> shift) * LANES + (r & (d - 1)), 1.0, 0.0).astype(BF16)


def _rope_block(x, c, sa, sb):
    return x * c + pltpu.roll(x, LANES - HALF_ROPE, 1) * sa + pltpu.roll(x, HALF_ROPE, 1) * sb


def _forget_cumsum(z, cb):
    s = z.shape[0]
    ts = _divisor_tile(s, 512, LANES)

    def body(x_ref, col_ref, carry):
        @pl.when(pl.program_id(0) == 0)
        def _():
            carry[...] = jnp.zeros_like(carry)

        x = x_ref[...]
        lf = jnp.minimum(x, 0.0) - jnp.log1p(jnp.exp(-jnp.abs(x)))
        r = lax.broadcasted_iota(jnp.int32, (ts, ts), 0)
        c = lax.broadcasted_iota(jnp.int32, (ts, ts), 1)
        tri = jnp.where(c <= r, 1.0, 0.0).astype(F32)
        col_ref[...] = jnp.dot(tri, lf, preferred_element_type=F32, precision=lax.Precision.HIGHEST) + carry[...]
        carry[...] += jnp.sum(lf, axis=0, keepdims=True)

    return pl.pallas_call(
        body, grid=(s // ts,),
        in_specs=[pl.BlockSpec((ts, LANES), lambda i: (i, cb))],
        out_specs=pl.BlockSpec((ts, LANES), lambda i: (i, 0)),
        out_shape=jax.ShapeDtypeStruct((s, LANES), F32),
        scratch_shapes=[pltpu.VMEM((1, LANES), F32)],
        compiler_params=_cparams(("arbitrary",)), name="forget_cumsum")(z)


def _forget_grad(dq_blocks, dk_blocks, lane):
    df = jnp.zeros(dq_blocks[0].shape, F32)
    for h, (dq_blk, dk_blk) in enumerate(zip(dq_blocks, dk_blocks, strict=True)):
        d_h = _lane_column(dq_blk, lane, FOX_Q_F) - _lane_column(dk_blk, lane, FOX_Q_ONES)
        df = jnp.where(lane == h, d_h, df)
    return df


def _forget_cumsum_bwd(df, z, cb):
    s = z.shape[0]
    ts = _divisor_tile(s, 512, LANES)
    nt = s // ts

    def body(df_ref, x_ref, o_ref, carry):
        @pl.when(pl.program_id(0) == 0)
        def _():
            carry[...] = jnp.zeros_like(carry)

        df = df_ref[...]
        r = lax.broadcasted_iota(jnp.int32, (ts, ts), 0)
        c = lax.broadcasted_iota(jnp.int32, (ts, ts), 1)
        tri = jnp.where(c >= r, 1.0, 0.0).astype(F32)
        rc = jnp.dot(tri, df, preferred_element_type=F32, precision=lax.Precision.HIGHEST) + carry[...]
        carry[...] += jnp.sum(df, axis=0, keepdims=True)
        o_ref[...] = (rc * (1.0 / (1.0 + jnp.exp(x_ref[...])))).astype(o_ref.dtype)

    return pl.pallas_call(
        body, grid=(nt,),
        in_specs=[pl.BlockSpec((ts, LANES), lambda i: (nt - 1 - i, 0)),
                  pl.BlockSpec((ts, LANES), lambda i: (nt - 1 - i, cb))],
        out_specs=pl.BlockSpec((ts, LANES), lambda i: (nt - 1 - i, 0)),
        out_shape=jax.ShapeDtypeStruct((s, LANES), BF16),
        scratch_shapes=[pltpu.VMEM((1, LANES), F32)],
        compiler_params=_cparams(("arbitrary",)), name="forget_cumsum_bwd")(df, z)


def _flash_fwd(q, k, v, scale, nh, l_lane, *, name):
    s = q.shape[0]
    t = min(ATT_TILE, s)
    half = t // 2 if t % (2 * LANES) == 0 else t
    nq = s // t
    c = scale * LOG2E

    def body(q_ref, k_ref, v_ref, o_ref, qb_ref):
        i = pl.program_id(1)
        qb = q_ref[...]

        def scores(q_rows, k0, nk):
            kb = k_ref[pl.ds(pl.multiple_of(k0, half), nk), :]
            return lax.dot_general(q_rows, kb, _NT, preferred_element_type=F32)

        def update(sc, k0, nk, carry):
            m, acc = carry
            m_new = jnp.maximum(m, jnp.max(sc, axis=1, keepdims=True))
            p = jnp.exp2((sc - m_new) * c)
            alpha = jnp.exp2((m - m_new) * c)
            vb = v_ref[pl.ds(pl.multiple_of(k0, half), nk), :]
            acc = alpha * acc + jnp.dot(p.astype(BF16), vb, preferred_element_type=F32)
            return m_new, acc

        def full_block(j, cr):
            return update(scores(qb, j * t, t), j * t, t, cr)

        def group(jj, cr):
            for n in range(FWD_GROUP):
                cr = full_block(FWD_GROUP * jj + n, cr)
            return cr

        def causal(sc):
            row = lax.broadcasted_iota(jnp.int32, sc.shape, 0)
            col = lax.broadcasted_iota(jnp.int32, sc.shape, 1)
            return jnp.where(col <= row, sc, NEG)

        init = (jnp.full((t, 1), NEG, F32), jnp.zeros((t, LANES), F32))
        n_groups = i >> FWD_GROUP_LOG2
        carry = lax.fori_loop(0, n_groups, group, init)
        carry = lax.fori_loop(n_groups * FWD_GROUP, i, full_block, carry)
        m, acc = update(causal(scores(qb, i * t, t)), i * t, t, carry)
        lane = _lane((t, LANES))
        l = _lane_column(acc, lane, V_ONES)
        o_ref[...] = (acc / l).astype(o_ref.dtype)
        big_l = m + jnp.log(l) / scale
        qb_ref[...] = _put3(qb.astype(F32), lane, l_lane, _split3(-big_l)).astype(qb_ref.dtype)

    head_rows = pl.BlockSpec((t, LANES), lambda h, i: (i, h))
    head_all = pl.BlockSpec((s, LANES), lambda h, i: (0, h))
    return pl.pallas_call(
        body, grid=(nh, nq), in_specs=[head_rows, head_all, head_all], out_specs=[head_rows, head_rows],
        out_shape=[jax.ShapeDtypeStruct(q.shape, BF16), jax.ShapeDtypeStruct(q.shape, BF16)],
        compiler_params=_cparams(("parallel", "arbitrary")), name=name)(q, k, v)


def _flash_bwd(qb, k, v, do, scale, nh, n_feat, *, name):
    s = qb.shape[0]
    t = min(ATT_TILE, s)
    half = t // 2 if t % (2 * LANES) == 0 else t
    nq = s // t
    c = scale * LOG2E

    def body(q_ref, k_ref, v_ref, do_ref, dk_ref, dv_ref, dq_ref):
        j = pl.program_id(1)
        kb = k_ref[...]
        vb = v_ref[...]

        @pl.when(j == 0)
        def _():
            dq_ref[...] = jnp.zeros_like(dq_ref)

        def part(q0, n_q, n_k, carry, q_off):
            dk_acc, dv_acc = carry
            rows = pl.ds(pl.multiple_of(q0, half), n_q)
            qblk = q_ref[rows, :]
            dob = do_ref[rows, :]
            kbb, vbb = kb[:n_k], vb[:n_k]
            st = lax.dot_general(kbb, qblk, _NT, preferred_element_type=F32)
            if q_off is not None:
                key = lax.broadcasted_iota(jnp.int32, st.shape, 0)
                qry = lax.broadcasted_iota(jnp.int32, st.shape, 1) + q_off
                st = jnp.where(key <= qry, st, NEG)
            pt = jnp.exp2(st * c)
            dv_new = jnp.dot(pt.astype(BF16), dob, preferred_element_type=F32)
            dpt = lax.dot_general(vbb, dob, _NT, preferred_element_type=F32)
            dsb = (pt * dpt).astype(BF16)
            dk_new = jnp.dot(dsb, qblk, preferred_element_type=F32)
            dq_ref[rows, :] += lax.dot_general(dsb, kbb, _TN, preferred_element_type=F32)
            if n_k == t:
                return dk_acc + dk_new, dv_acc + dv_new
            return (jnp.concatenate([dk_acc[:n_k] + dk_new, dk_acc[n_k:]], axis=0),
                    jnp.concatenate([dv_acc[:n_k] + dv_new, dv_acc[n_k:]], axis=0))

        def block(i, carry):
            return part(i * t, t, t, carry, None)

        init = (jnp.zeros((t, LANES), F32), jnp.zeros((t, LANES), F32))
        carry = part(j * t + half, t - half, t, init, half) if half < t else part(j * t, t, t, init, 0)
        if half < t:
            carry = part(j * t, half, half, carry, 0)
        rest = nq - 1 - j
        carry = lax.cond((rest & 1) == 1, lambda cr: block(j + 1, cr), lambda cr: cr, carry)
        first = j + 1 + (rest & 1)

        def pair(ii, cr):
            i0 = first + 2 * ii
            return block(i0 + 1, block(i0, cr))

        dk_acc, dv_acc = lax.fori_loop(0, rest >> 1, pair, carry)
        dk_ref[...] = dk_acc * jnp.where(_lane((t, LANES)) < n_feat, scale, 1.0)
        dv_ref[...] = dv_acc.astype(dv_ref.dtype)

        @pl.when(j == nq - 1)
        def _():
            dq_ref[...] = dq_ref[...] * jnp.where(_lane((s, LANES)) < n_feat, scale, 1.0)

    head_rows = pl.BlockSpec((t, LANES), lambda h, j: (j, h))
    head_all = pl.BlockSpec((s, LANES), lambda h, j: (0, h))
    shp = jax.ShapeDtypeStruct(qb.shape, F32)
    return pl.pallas_call(
        body, grid=(nh, nq), in_specs=[head_all, head_rows, head_rows, head_all],
        out_specs=[head_rows, head_rows, head_all],
        out_shape=[shp, jax.ShapeDtypeStruct(qb.shape, BF16), shp],
        compiler_params=_cparams(("parallel", "arbitrary")), name=name)(qb, k, v, do)


def _with_delta(do, ex):
    n_blocks = do.shape[1] // LANES
    lane = _lane((do.shape[0], LANES))
    out = []
    for d_blk, o_blk in zip(_blocks(do, n_blocks), _blocks(ex[0], n_blocks), strict=True):
        delta = jnp.sum(d_blk * o_blk.astype(F32), axis=1, keepdims=True)
        out.append(_put3(d_blk, lane, V_ONES, _split3(-delta)))
    return [jnp.concatenate(out, axis=1)]


def _pad_heads(a, nh):
    d = a.shape[-1] // nh
    a = a.reshape(a.shape[:-1] + (nh, d))
    a = jnp.pad(a, [(0, 0)] * (a.ndim - 1) + [(0, LANES - d)])
    return a.reshape(a.shape[:-2] + (nh * LANES,))


def _unpad_heads(a, nh, d):
    a = a.reshape(a.shape[:-1] + (nh, LANES))[..., :d]
    return a.reshape(a.shape[:-2] + (nh * d,))


def _pad_head_rows(w, nh):
    return _pad_heads(w.T, nh).T


def _unpad_head_rows(g, nh, d):
    return _unpad_heads(g.T, nh, d).T


class _ZLayout:
    def __init__(self, d):
        fw = FOX_HEADS * FOX_DIM
        self.src = {}
        off = 0
        for nm, w in (("cq", MLA_Q_LORA), ("ckv", MLA_KV_LORA), ("kr", MLA_ROPE), ("fq", fw), ("fk", fw),
                      ("fv", fw), ("fl", FOX_HEADS), ("ga", d), ("gb", d)):
            self.src[nm] = (off, w)
            off += w
        self.dst = {}
        off = 0
        for nm, w in (("ga", d), ("gb", d), ("fq", fw), ("fk", fw), ("fv", fw), ("cq", MLA_Q_LORA),
                      ("ckv", MLA_KV_LORA), ("kr", LANES), ("fl", LANES)):
            assert off % w == 0
            self.dst[nm] = (off, w)
            off += w
        self.width = off
        self.split = self.dst["cq"][0]
        assert all((o - self.split) % w == 0 for o, w in self.dst.values() if o >= self.split)

    def to_kernel(self, w):
        def seg(nm):
            off, wd = self.src[nm]
            return w[..., off:off + wd]

        def pad(a, left, total):
            return jnp.pad(a, [(0, 0)] * (a.ndim - 1) + [(left, total - left - a.shape[-1])])

        return jnp.concatenate([seg("ga"), seg("gb"), seg("fq"), seg("fk"), seg("fv"), seg("cq"), seg("ckv"),
                                pad(seg("kr"), MLA_NOPE, LANES), pad(seg("fl"), 0, LANES)], axis=-1)

    def from_kernel(self, g):
        def seg(nm, lo=0, hi=None):
            off, wd = self.dst[nm]
            return g[..., off + lo:off + (wd if hi is None else hi)]

        return jnp.concatenate([seg("cq"), seg("ckv"), seg("kr", MLA_NOPE, MLA_NOPE + MLA_ROPE), seg("fq"), seg("fk"),
                                seg("fv"), seg("fl", 0, FOX_HEADS), seg("ga"), seg("gb")], axis=-1)


def _local_step(x, positions, target, wts, late_weights, send_grads):
    s, d = x.shape
    zl = _ZLayout(d)
    hw = MLA_HEADS * LANES
    assert MLA_HEADS == FOX_HEADS
    scale_mla = (MLA_NOPE + MLA_ROPE) ** -0.5
    scale_fox = FOX_DIM ** -0.5

    inv_freq = ROPE_THETA ** (-jnp.arange(HALF_ROPE, dtype=F32) / HALF_ROPE)
    ang = positions.astype(F32)[:, None] * inv_freq
    cos, sin = jnp.cos(ang), jnp.sin(ang)
    tail = jnp.zeros((s, LANES - MLA_NOPE - MLA_ROPE), F32)
    rc = jnp.concatenate([jnp.ones((s, MLA_NOPE), F32), cos, cos, tail], axis=1)
    ra = jnp.concatenate([jnp.zeros((s, MLA_NOPE), F32), -sin, jnp.zeros((s, HALF_ROPE), F32), tail], axis=1)
    rb = jnp.concatenate([jnp.zeros((s, MLA_NOPE + HALF_ROPE), F32), sin, tail], axis=1)

    w_in = zl.to_kernel(wts["w_in"])
    b_in = zl.to_kernel(wts["b_in"])

    def f_norm_in(ti, pa):
        y, _ = _rms(ti[0], pa[0])
        return [y], []

    (h,) = _rowwise("norm_in", f_norm_in, [x], [wts["ln_pre_mix"]], [(d, BF16)])
    z_lo = _mm(h, w_in[:, :zl.split], bias=b_in[:, :zl.split], out_dtype=BF16, name="proj_in_lo")
    z = _mm(h, w_in[:, zl.split:], bias=b_in[:, zl.split:], out_dtype=F32, name="proj_in_hi")

    def zwin(nm):
        off, wd = zl.dst[nm]
        return (z_lo, off // wd, wd) if off < zl.split else (z, (off - zl.split) // wd, wd)

    wts = {**wts, **late_weights("mixer", z)}
    w_uq = _pad_heads(wts["w_uq"], MLA_HEADS)
    w_ukv = jnp.concatenate([_pad_heads(wts["w_uk"], MLA_HEADS), _pad_heads(wts["w_uv"], MLA_HEADS)], axis=1)
    w_o_mla = _pad_head_rows(wts["w_o_mla"], MLA_HEADS)
    w_o_fox = _pad_head_rows(wts["w_o_fox"], FOX_HEADS)

    def f_mla_norms(ti, pa):
        cqn, _ = _rms(ti[0], pa[0])
        ckvn, _ = _rms(ti[1], pa[1])
        return [cqn, ckvn], []

    cqn, ckvn = _rowwise("mla_norms", f_mla_norms, [zwin("cq"), zwin("ckv")],
                         [wts["q_a_norm"], wts["kv_a_norm"]], [(MLA_Q_LORA, BF16), (MLA_KV_LORA, BF16)])
    qf = _mm(cqn, w_uq, out_dtype=F32, name="proj_uq")
    kv = _mm(ckvn, w_ukv, out_dtype=BF16, name="proj_ukv")

    def f_rope_q(ti, pa):
        xq, c_, a_, b_ = ti
        return [jnp.concatenate([_rope_block(blk, c_, a_, b_) for blk in _blocks(xq, MLA_HEADS)], axis=1)], []

    (q_mla,) = _rowwise("rope_q", f_rope_q, [qf, rc, ra, rb], [], [(hw, BF16)])

    def f_mla_kv(ti, pa):
        kn, vn, kr, c_, a_, b_ = ti
        lane = _lane(kr.shape)
        k_tail = jnp.where((lane >= MLA_Q_L) & (lane < MLA_Q_L + 3), 1.0, _rope_block(kr, c_, a_, b_))
        ones_v = (lane >= V_ONES) & (lane < V_ONES + 3)
        k_out = [jnp.where(lane < MLA_NOPE, blk.astype(F32), k_tail) for blk in _blocks(kn, MLA_HEADS)]
        v_out = [jnp.where(ones_v, 1.0, blk.astype(F32)) for blk in _blocks(vn, MLA_HEADS)]
        return [jnp.concatenate(k_out, axis=1), jnp.concatenate(v_out, axis=1)], []

    k_mla, v_mla = _rowwise("mla_kv", f_mla_kv, [(kv, 0, hw), (kv, 1, hw), zwin("kr"), rc, ra, rb], [],
                            [(hw, BF16), (hw, BF16)])
    o_mla, qb_mla = _flash_fwd(q_mla, k_mla, v_mla, scale_mla, MLA_HEADS, MLA_Q_L, name="mla_fwd")

    fl_cb = zwin("fl")[1]
    fcol = _forget_cumsum(z, fl_cb)

    def f_fox_qkv(ti, pa):
        spread = _head_spread(FOX_HEADS, FOX_DIM)
        fq, fk, fv = (jnp.dot(a, spread, preferred_element_type=F32) for a in ti[:3])
        fc = ti[3]
        lane = _lane(fc.shape)
        ones_q = (lane >= FOX_Q_ONES) & (lane < FOX_Q_ONES + 3)
        ones_k = (lane >= FOX_Q_F) & (lane < FOX_Q_ONES)
        ones_v = (lane >= V_ONES) & (lane < V_ONES + 3)
        q_out, k_out, v_out = [], [], []
        for hh, (qblk, kblk, vblk) in enumerate(zip(_blocks(fq, FOX_HEADS), _blocks(fk, FOX_HEADS),
                                                    _blocks(fv, FOX_HEADS), strict=True)):
            f_h = _lane_column(fc, lane, hh) * (1.0 / scale_fox)
            q_out.append(_put3(jnp.where(ones_q, 1.0, qblk), lane, FOX_Q_F, _split3(f_h)))
            k_out.append(_put3(jnp.where(ones_k, 1.0, kblk), lane, FOX_Q_ONES, _split3(-f_h)))
            v_out.append(jnp.where(ones_v, 1.0, vblk))
        return [jnp.concatenate(q_out, axis=1), jnp.concatenate(k_out, axis=1), jnp.concatenate(v_out, axis=1)], []

    q_fox, k_fox, v_fox = _rowwise("fox_qkv", f_fox_qkv, [zwin("fq"), zwin("fk"), zwin("fv"), fcol],
                                   [], [(hw, BF16)] * 3)
    o_fox, qb_fox = _flash_fwd(q_fox, k_fox, v_fox, scale_fox, FOX_HEADS, FOX_Q_L, name="fox_fwd")

    y_mla = _mm(o_mla, w_o_mla, out_dtype=BF16, name="proj_o_mla")
    def gate(yb, ex):
        ga, gb, ya = (a.astype(F32) for a in ex)
        return [yb, _sigmoid(ga) * ya + _sigmoid(gb) * yb]

    y_fox, merged = _mm(o_fox, w_o_fox, out_dtype=[BF16, BF16], name="proj_o_fox",
                        extras=[(z_lo, zl.dst["ga"][0]), (z_lo, zl.dst["gb"][0]), y_mla], epilogue=gate)
    wts = {**wts, **late_weights("mlp", merged)}
    def resid_mix(mx, ex):
        xa, g_post, g_pre = ex
        y, _ = _rms(mx, g_post)
        x1 = xa + y
        h2, _ = _rms(x1, g_pre)
        return [mx, x1, h2]

    mix, x1, h2 = _mm(merged, wts["w_out"], out_dtype=[F32, F32, BF16], name="proj_out", extras=[x],
                      rows=[wts["ln_post_mix"], wts["ln_pre_mlp"]], epilogue=resid_mix)

    def relu2(acc, ex):
        r = jnp.maximum(acc, 0.0)
        return [acc, r * r]

    u, act = _mm(h2, wts["w_ff1"], out_dtype=[BF16, BF16], name="ff1", epilogue=relu2)
    mo = _mm(act, wts["w_ff2"], out_dtype=F32, name="ff2")

    def f_loss(ti, pa):
        xa, mv, tg = ti
        y, _ = _rms(mv, pa[0])
        err = (xa + y) - tg
        g2 = err / d
        dmo, dg = _rms_bwd(mv, pa[0], g2)
        return [g2, dmo], [jnp.sum(err * err, axis=0, keepdims=True), dg]

    g2, d_mo, loss_cols, g_ln_post_mlp = _rowwise("loss", f_loss, [x1, mo, target], [wts["ln_post_mlp"]],
                                                  [(d, F32), (d, BF16)], [d, d])
    loss = 0.5 * jnp.sum(loss_cols) / d

    grads = {"ln_post_mlp": g_ln_post_mlp}
    grads["w_ff2"] = _mm_tn(act, d_mo, name="grad_ff2")

    def relu2_bwd(acc, ex):
        return [acc * (2.0 * jnp.maximum(ex[0], 0.0))]

    (d_u,) = _mm(d_mo, wts["w_ff2"], out_dtype=[BF16], name="ff2_bwd", transpose_b=True, extras=[u], epilogue=relu2_bwd)
    grads["w_ff1"] = _mm_tn(h2, d_u, name="grad_ff1")
    d_h2 = _mm(d_u, wts["w_ff1"], out_dtype=F32, name="ff1_bwd", transpose_b=True)

    def f_resid1_bwd(ti, pa):
        gres, dh2, x1v, mx = ti
        dx1n, dg_pre_mlp = _rms_bwd(x1v, pa[1], dh2)
        dx1 = gres + dx1n
        dmix, dg_post_mix = _rms_bwd(mx, pa[0], dx1)
        return [dx1, dmix], [dg_post_mix, dg_pre_mlp]

    d_x1, d_mix, grads["ln_post_mix"], grads["ln_pre_mlp"] = _rowwise(
        "resid_mix_bwd", f_resid1_bwd, [g2, d_h2, x1, mix], [wts["ln_post_mix"], wts["ln_pre_mlp"]],
        [(d, F32), (d, BF16)], [d, d])
    grads["w_out"] = _mm_tn(merged, d_mix, name="grad_out")
    def gate_bwd(dm, ex):
        ga, gb, ya, yb = (a.astype(F32) for a in ex)
        sa, sb = _sigmoid(ga), _sigmoid(gb)
        return [dm * sa, dm * sb, dm * ya * (sa * (1.0 - sa)), dm * yb * (sb * (1.0 - sb))]

    d_ya, d_yb, d_ga, d_gb = _mm(d_mix, wts["w_out"], out_dtype=[BF16] * 4, name="proj_out_bwd", transpose_b=True,
                                 extras=[(z_lo, zl.dst["ga"][0]), (z_lo, zl.dst["gb"][0]), y_mla, y_fox],
                                 epilogue=gate_bwd)
    grads["w_o_mla"] = _unpad_head_rows(_mm_tn(o_mla, d_ya, name="grad_o_mla"), MLA_HEADS, MLA_V)
    grads["w_o_fox"] = _unpad_head_rows(_mm_tn(o_fox, d_yb, name="grad_o_fox"), FOX_HEADS, FOX_DIM)
    w_o_mla = w_o_mla + send_grads("early", {nm: grads[nm] for nm in EARLY_GRADS}).astype(BF16)
    (do_mla,) = _mm(d_ya, w_o_mla, out_dtype=[BF16], name="proj_o_mla_bwd", transpose_b=True, extras=[o_mla],
                    epilogue=_with_delta)
    (do_fox,) = _mm(d_yb, w_o_fox, out_dtype=[BF16], name="proj_o_fox_bwd", transpose_b=True, extras=[o_fox],
                    epilogue=_with_delta)

    dk_mla, dv_mla, dq_mla = _flash_bwd(qb_mla, k_mla, v_mla, do_mla, scale_mla, MLA_HEADS, MLA_NOPE + MLA_ROPE,
                                        name="mla_bwd")
    dk_fox, dv_fox, dq_fox = _flash_bwd(qb_fox, k_fox, v_fox, do_fox, scale_fox, FOX_HEADS, FOX_DIM, name="fox_bwd")

    def f_rope_q_bwd(ti, pa):
        g, c_, a_, b_ = ti
        return [jnp.concatenate([_rope_block(blk, c_, -a_, -b_) for blk in _blocks(g, MLA_HEADS)], axis=1)], []

    (d_qf,) = _rowwise("rope_q_bwd", f_rope_q_bwd, [dq_mla, rc, ra, rb], [], [(hw, BF16)])
    grads["w_uq"] = _unpad_heads(_mm_tn(cqn, d_qf, name="grad_uq"), MLA_HEADS, MLA_NOPE + MLA_ROPE)
    d_cqn = _mm(d_qf, w_uq, out_dtype=F32, name="proj_uq_bwd", transpose_b=True)

    def f_mla_kv_bwd(ti, pa):
        gk, gv, c_, a_, b_ = ti
        k_blocks = _blocks(gk, MLA_HEADS)
        tot = k_blocks[0]
        for blk in k_blocks[1:]:
            tot = tot + blk
        return [jnp.concatenate([gk, gv], axis=1), _rope_block(tot, c_, -a_, -b_)], []

    d_kv, d_kr = _rowwise("mla_kv_bwd", f_mla_kv_bwd, [dk_mla, dv_mla, rc, ra, rb], [], [(2 * hw, BF16), (LANES, BF16)])
    g_ukv = _mm_tn(ckvn, d_kv, name="grad_ukv")
    grads["w_uk"] = _unpad_heads(g_ukv[:, :hw], MLA_HEADS, MLA_NOPE)
    grads["w_uv"] = _unpad_heads(g_ukv[:, hw:], MLA_HEADS, MLA_V)
    d_ckvn = _mm(d_kv, w_ukv, out_dtype=F32, name="proj_ukv_bwd", transpose_b=True)

    def f_mla_norms_bwd(ti, pa):
        cq, ckv, dcqn, dckvn = ti
        dcq, dg_q = _rms_bwd(cq, pa[0], dcqn)
        dckv, dg_kv = _rms_bwd(ckv, pa[1], dckvn)
        return [dcq, dckv], [dg_q, dg_kv]

    d_cq, d_ckv, grads["q_a_norm"], grads["kv_a_norm"] = _rowwise(
        "mla_norms_bwd", f_mla_norms_bwd, [zwin("cq"), zwin("ckv"), d_cqn, d_ckvn],
        [wts["q_a_norm"], wts["kv_a_norm"]], [(MLA_Q_LORA, BF16), (MLA_KV_LORA, BF16)], [MLA_Q_LORA, MLA_KV_LORA])

    def f_fox_compact(ti, pa):
        gather = _head_spread(FOX_HEADS, FOX_DIM)
        df = _forget_grad(_blocks(ti[0], FOX_HEADS), _blocks(ti[1], FOX_HEADS), _lane((ti[0].shape[0], LANES)))
        return [lax.dot_general(a.astype(BF16), gather, _NT, preferred_element_type=F32) for a in ti] + [df], []

    d_fq, d_fk, d_fv, d_fcum = _rowwise("fox_compact", f_fox_compact, [dq_fox, dk_fox, dv_fox], [],
                                        [(FOX_HEADS * FOX_DIM, BF16)] * 3 + [(LANES, F32)])
    d_fl = _forget_cumsum_bwd(d_fcum, z, fl_cb)
    d_z = jnp.concatenate([d_ga, d_gb, d_fq, d_fk, d_fv, d_cq, d_ckv, d_kr, d_fl], axis=1)
    assert d_z.shape[1] == zl.width

    def f_bias(ti, pa):
        return [], [jnp.sum(ti[0].astype(F32), axis=0, keepdims=True)]

    (g_b_in,) = _rowwise("grad_b_in", f_bias, [d_z], [], [], [zl.width])
    grads["b_in"] = zl.from_kernel(g_b_in)
    grads["w_in"] = zl.from_kernel(_mm_tn(h, d_z, name="grad_in"))
    tok = send_grads("late", {nm: grads[nm] for nm, _ in BIG if nm not in EARLY_GRADS})
    d_h = _mm(d_z, w_in, bias=jnp.zeros((1, d), F32) + tok, out_dtype=F32, name="proj_in_bwd", transpose_b=True)

    def f_norm_in_bwd(ti, pa):
        dx1v, dh, xa = ti
        dxn, dg = _rms_bwd(xa, pa[0], dh)
        return [dx1v + dxn], [dg]

    grad_x, grads["ln_pre_mix"] = _rowwise("norm_in_bwd", f_norm_in_bwd, [d_x1, d_h, x], [wts["ln_pre_mix"]],
                                           [(d, F32)], [d])
    return loss, grad_x, grads


class _PackLayout:
    def __init__(self, shapes):
        self.shapes = list(shapes)
        self.width = _round_up(max(b for _, b in shapes), LANES)
        self.bands = []
        row = 0
        shelf = []
        for idx, (a, b) in enumerate(shapes):
            if 2 * _round_up(b, LANES) > self.width:
                self.bands.append((row, _round_up(a, 32), [(idx, 0)]))
                row += _round_up(a, 32)
            else:
                shelf.append(idx)
        col, items = 0, []
        for idx in shelf:
            wb = _round_up(shapes[idx][1], LANES)
            if col + wb > self.width:
                hgt = max(_round_up(shapes[i][0], 32) for i, _ in items)
                self.bands.append((row, hgt, items))
                row += hgt
                col, items = 0, []
            items.append((idx, col))
            col += wb
        if items:
            hgt = max(_round_up(shapes[i][0], 32) for i, _ in items)
            self.bands.append((row, hgt, items))
            row += hgt
        self.rows = _round_up(row, 16 * LOCAL_PIECES)

    def pack(self, arrs):
        lead = arrs[0].shape[:-2]
        no_pad = [(0, 0)] * len(lead)
        bands = []
        for _, hgt, items in self.bands:
            parts = []
            for k, (idx, col) in enumerate(items):
                a, b = self.shapes[idx]
                nxt = items[k + 1][1] if k + 1 < len(items) else self.width
                parts.append(jnp.pad(arrs[idx], no_pad + [(0, hgt - a), (0, nxt - col - b)]))
            bands.append(parts[0] if len(parts) == 1 else jnp.concatenate(parts, axis=-1))
        used = sum(hgt for _, hgt, _ in self.bands)
        if used < self.rows:
            bands.append(jnp.zeros(lead + (self.rows - used, self.width), arrs[0].dtype))
        return jnp.concatenate(bands, axis=-2)

    def unpack(self, packed):
        out = [None] * len(self.shapes)
        for row, _, items in self.bands:
            for idx, col in items:
                a, b = self.shapes[idx]
                out[idx] = packed[..., row:row + a, col:col + b]
        return out


def _to_shards(g, axis):
    if axis == 0:
        return g.reshape(N_CHIPS, g.shape[0] // N_CHIPS, g.shape[1])
    return jnp.stack(jnp.split(g, N_CHIPS, axis=1))


def _from_shards(s4, axis):
    n, a, b = s4.shape
    if axis == 0:
        return s4.reshape(n * a, b)
    return jnp.concatenate([s4[ch] for ch in range(n)], axis=1)


ANY = pl.BlockSpec(memory_space=pl.ANY)


def _place():
    return lax.axis_index("x"), lax.axis_index("y"), lax.axis_index("c")


def _gather_weights(wpk):
    rows, wd = wpk.shape
    half = rows // 2

    def body(w_ref, out_ref, send_sems, recv_sems, local_sems):
        x, y, c = _place()
        sibling = (x, y, 1 - c)
        chips = [(1 - x, y), (x, 1 - y), (1 - x, 1 - y)]

        def slab(chip, hf):
            return out_ref.at[2 * chip[0] + chip[1], pl.ds(hf * half, half), :]

        def copy(k, chip, hf, to, src=None):
            return pltpu.make_async_remote_copy(
                src_ref=slab(chip, hf) if src is None else src, dst_ref=slab(chip, hf),
                send_sem=send_sems.at[k], recv_sem=recv_sems.at[k], device_id=to, device_id_type=MESH)

        piece = rows // LOCAL_PIECES
        mine = [pltpu.make_async_copy(w_ref.at[pl.ds(n * piece, piece), :],
                                      out_ref.at[2 * x + y, pl.ds(n * piece, piece), :], local_sems.at[n])
                for n in range(LOCAL_PIECES)]
        for cp in mine:
            cp.start()
        first = [copy(j, (x, y), c, (*chip, c), src=w_ref.at[pl.ds(c * half, half), :]) for j, chip in enumerate(chips)]
        for cp in first:
            cp.start()
        passed = [copy(3 + j, chip, c, sibling) for j, chip in enumerate(chips)]
        for j, chip in enumerate(chips):
            copy(j, chip, c, (x, y, c)).wait_recv()
            passed[j].start()
        for j, chip in enumerate(chips):
            copy(3 + j, chip, 1 - c, (x, y, c)).wait_recv()
        for cp in first + passed:
            cp.wait_send()
        for cp in mine:
            cp.wait()

    assert rows % (16 * LOCAL_PIECES) == 0
    return pl.pallas_call(
        body, out_shape=jax.ShapeDtypeStruct((N_CHIPS, rows, wd), wpk.dtype),
        in_specs=[ANY], out_specs=ANY,
        scratch_shapes=[pltpu.SemaphoreType.DMA((6,)), pltpu.SemaphoreType.DMA((6,)),
                        pltpu.SemaphoreType.DMA((LOCAL_PIECES,))],
        name="gather_weights")(wpk)


HBM = pl.BlockSpec(memory_space=pltpu.HBM)
SEM = pl.BlockSpec(memory_space=pltpu.SEMAPHORE)
EFFECT = pltpu.SideEffectType.DATAFLOW_SIDE_EFFECTING
N_LATE = 6


def _gather_late_start(wpk, tag):
    rows, wd = wpk.shape
    half = rows // 2

    def body(w_ref, land_ref, send_sems, recv_sems, w_thru, land_thru, token):
        x, y, c = _place()
        chips = [(1 - x, y), (x, 1 - y), (1 - x, 1 - y)]
        for j, chip in enumerate(chips):
            for to_core in range(2):
                pltpu.make_async_remote_copy(
                    src_ref=w_ref.at[pl.ds(c * half, half), :],
                    dst_ref=land_ref.at[2 * x + y, pl.ds(c * half, half), :],
                    send_sem=send_sems.at[2 * j + to_core], recv_sem=recv_sems.at[2 * j + c],
                    device_id=(*chip, to_core), device_id_type=MESH).start()
        token[...] = jnp.zeros_like(token)

    land = lax.empty((N_CHIPS, rows, wd), wpk.dtype)
    return pl.pallas_call(
        body, name="gather_" + tag + "_start",
        out_shape=(pltpu.SemaphoreType.DMA((N_LATE,)), pltpu.SemaphoreType.DMA((N_LATE,)),
                   pltpu.HBM(wpk.shape, wpk.dtype), pltpu.HBM(land.shape, land.dtype),
                   jax.ShapeDtypeStruct((8, LANES), F32)),
        in_specs=(HBM, HBM), out_specs=(SEM, SEM, HBM, HBM, pl.BlockSpec(memory_space=pltpu.VMEM)),
        input_output_aliases={0: 2, 1: 3},
        compiler_params=pltpu.CompilerParams(has_side_effects=EFFECT),
    )(pltpu.with_memory_space_constraint(wpk, pltpu.HBM), pltpu.with_memory_space_constraint(land, pltpu.HBM))


def _gather_late_wait(send_sems, recv_sems, w_thru, land_thru, after, tag):
    rows, wd = w_thru.shape
    half = rows // 2

    def body(w_ref, land_ref, send_sems, recv_sems, after_ref, w_dead, land_out):
        x, y, c = _place()
        for n in range(N_LATE):
            cp = pltpu.make_async_remote_copy(
                src_ref=w_ref.at[pl.ds(0, half), :], dst_ref=land_ref.at[0, pl.ds(0, half), :],
                send_sem=send_sems.at[n], recv_sem=recv_sems.at[n], device_id=(x, y, c), device_id_type=MESH)
            cp.wait_send()
            cp.wait_recv()

    return pl.pallas_call(
        body, name="gather_" + tag + "_wait",
        out_shape=(pltpu.HBM(w_thru.shape, w_thru.dtype), pltpu.HBM(land_thru.shape, land_thru.dtype)),
        in_specs=(HBM, HBM, SEM, SEM, ANY), out_specs=(HBM, HBM), input_output_aliases={0: 0, 1: 1},
        compiler_params=pltpu.CompilerParams(has_side_effects=EFFECT),
    )(w_thru, land_thru, send_sems, recv_sems, after)[1]


N_PART = 7


def _reduce_start(gbf, tag):
    _, _, hrows, wd = gbf.shape

    def body(g_ref, land_ref, send_sems, recv_sems, g_thru, land_thru, token):
        x, y, c = _place()
        chips = [(1 - x, y), (x, 1 - y), (1 - x, 1 - y)]
        for j, chip in enumerate(chips):
            for to_core in range(2):
                pltpu.make_async_remote_copy(
                    src_ref=g_ref.at[2 * chip[0] + chip[1], to_core], dst_ref=land_ref.at[2 * j + c],
                    send_sem=send_sems.at[2 * j + to_core], recv_sem=recv_sems.at[2 * j + c],
                    device_id=(*chip, to_core), device_id_type=MESH).start()
        pltpu.make_async_remote_copy(
            src_ref=g_ref.at[2 * x + y, 1 - c], dst_ref=land_ref.at[N_PART - 1],
            send_sem=send_sems.at[N_PART - 1], recv_sem=recv_sems.at[N_PART - 1],
            device_id=(x, y, 1 - c), device_id_type=MESH).start()
        token[...] = jnp.zeros_like(token)

    land = lax.empty((N_PART, hrows, wd), gbf.dtype)
    return pl.pallas_call(
        body, name="reduce_" + tag + "_start",
        out_shape=(pltpu.SemaphoreType.DMA((N_PART,)), pltpu.SemaphoreType.DMA((N_PART,)),
                   pltpu.HBM(gbf.shape, gbf.dtype), pltpu.HBM(land.shape, land.dtype),
                   jax.ShapeDtypeStruct((8, LANES), F32)),
        in_specs=(HBM, HBM), out_specs=(SEM, SEM, HBM, HBM, pl.BlockSpec(memory_space=pltpu.VMEM)),
        input_output_aliases={0: 2, 1: 3},
        compiler_params=pltpu.CompilerParams(has_side_effects=EFFECT),
    )(pltpu.with_memory_space_constraint(gbf, pltpu.HBM), pltpu.with_memory_space_constraint(land, pltpu.HBM))


def _reduce_wait(send_sems, recv_sems, g_thru, land_thru, after, tag):
    def body(g_ref, land_ref, send_sems, recv_sems, after_ref, g_dead, land_out):
        x, y, c = _place()
        for n in range(N_PART):
            cp = pltpu.make_async_remote_copy(
                src_ref=g_ref.at[0, 0], dst_ref=land_ref.at[0], send_sem=send_sems.at[n], recv_sem=recv_sems.at[n],
                device_id=(x, y, c), device_id_type=MESH)
            cp.wait_send()
            cp.wait_recv()

    return pl.pallas_call(
        body, name="reduce_" + tag + "_wait",
        out_shape=(pltpu.HBM(g_thru.shape, g_thru.dtype), pltpu.HBM(land_thru.shape, land_thru.dtype)),
        in_specs=(HBM, HBM, SEM, SEM, ANY), out_specs=(HBM, HBM), input_output_aliases={0: 0, 1: 1},
        compiler_params=pltpu.CompilerParams(has_side_effects=EFFECT),
    )(g_thru, land_thru, send_sems, recv_sems, after)[1]


def _sibling_swap(mine):
    def body(m_ref, out_ref, send_sem, recv_sem):
        x, y, c = _place()
        cp = pltpu.make_async_remote_copy(
            src_ref=m_ref, dst_ref=out_ref, send_sem=send_sem, recv_sem=recv_sem,
            device_id=(x, y, 1 - c), device_id_type=MESH)
        cp.start()
        cp.wait()

    return pl.pallas_call(
        body, out_shape=jax.ShapeDtypeStruct(mine.shape, mine.dtype), in_specs=[ANY], out_specs=ANY,
        scratch_shapes=[pltpu.SemaphoreType.DMA, pltpu.SemaphoreType.DMA], name="grad_sibling_swap")(mine)


def _adamw(w, g, m, v):
    m = ADAM_B1 * m + (1.0 - ADAM_B1) * g
    v = ADAM_B2 * v + (1.0 - ADAM_B2) * (g * g)
    m_hat = m / (1.0 - ADAM_B1 ** ADAM_STEP)
    v_hat = v / (1.0 - ADAM_B2 ** ADAM_STEP)
    delta = -ADAM_LR * (m_hat / (jnp.sqrt(v_hat) + ADAM_EPS) + ADAM_WD * w)
    return delta, m, v


def _small_allreduce_adamw(gs, ws, ms, vs):
    n_dev = 8
    n_par = len(gs)
    wd = PACK_W
    chunks = []
    for p, g in enumerate(gs):
        for off in range(0, g.shape[1], wd):
            chunks.append((p, len(chunks), off, min(wd, g.shape[1] - off)))
    rows = _round_up(len(chunks), 8)

    def body(*refs):
        g_refs, w_refs, m_refs, v_refs = (refs[k * n_par:(k + 1) * n_par] for k in range(4))
        go_refs, d_refs, mo_refs, vo_refs = (refs[(4 + k) * n_par:(5 + k) * n_par] for k in range(4))
        mine_ref, all_ref, send_sems, recv_sems = refs[8 * n_par:]
        x, y, c = _place()
        me, sibling = (x, y, c), (x, y, 1 - c)
        chips = [(1 - x, y), (x, 1 - y), (1 - x, 1 - y)]

        def slot(px, py, pc):
            return all_ref.at[4 * px + 2 * py + pc]

        def copy(k, block, to, src=None):
            return pltpu.make_async_remote_copy(
                src_ref=slot(*block) if src is None else src, dst_ref=slot(*block),
                send_sem=send_sems.at[k], recv_sem=recv_sems.at[k], device_id=to, device_id_type=MESH)

        mine_ref[...] = jnp.zeros_like(mine_ref)
        for p, row, off, width in chunks:
            mine_ref[row:row + 1, 0:width] = g_refs[p][:, off:off + width]
        all_ref[4 * x + 2 * y + c] = mine_ref[...]
        first = [copy(0, me, sibling, src=mine_ref)]
        first += [copy(1 + j, me, (*chip, c), src=mine_ref) for j, chip in enumerate(chips)]
        for cp in first:
            cp.start()
        passed = [copy(4 + j, (*chip, c), sibling) for j, chip in enumerate(chips)]
        for j, chip in enumerate(chips):
            copy(1 + j, (*chip, c), me).wait_recv()
            passed[j].start()
        copy(0, sibling, me).wait_recv()
        for j, chip in enumerate(chips):
            copy(4 + j, (*chip, 1 - c), me).wait_recv()
        for cp in first + passed:
            cp.wait_send()
        tot = jnp.zeros((rows, wd), F32)
        for dev in range(n_dev):
            tot = tot + all_ref[dev]
        mine_ref[...] = tot
        for p, row, off, width in chunks:
            cols = slice(off, off + width)
            g = mine_ref[row:row + 1, 0:width]
            delta, m_new, v_new = _adamw(w_refs[p][:, cols], g, m_refs[p][:, cols], v_refs[p][:, cols])
            go_refs[p][:, cols] = g
            d_refs[p][:, cols] = delta
            mo_refs[p][:, cols] = m_new
            vo_refs[p][:, cols] = v_new

    vm = pl.BlockSpec(memory_space=pltpu.VMEM)
    shp = [jax.ShapeDtypeStruct(g.shape, F32) for g in gs]
    res = pl.pallas_call(
        body, out_shape=shp * 4, in_specs=[vm] * (4 * n_par), out_specs=[vm] * (4 * n_par),
        scratch_shapes=[pltpu.VMEM((rows, wd), F32), pltpu.VMEM((n_dev, rows, wd), F32),
                        pltpu.SemaphoreType.DMA((7,)), pltpu.SemaphoreType.DMA((7,))],
        name="small_allreduce_adamw")(*gs, *ws, *ms, *vs)
    return [res[k * n_par:(k + 1) * n_par] for k in range(4)]


def kernel(x, positions, ln_pre_mix, ln_post_mix, ln_pre_mlp, ln_post_mlp, w_in, b_in, q_a_norm, w_uq, kv_a_norm, w_uk, w_uv, w_o_mla, w_o_fox, w_out, w_ff1, w_ff2, loss_target, m_ln_pre_mix, m_ln_post_mix, m_ln_pre_mlp, m_ln_post_mlp, m_w_in, m_b_in, m_q_a_norm, m_w_uq, m_kv_a_norm, m_w_uk, m_w_uv, m_w_o_mla, m_w_o_fox, m_w_out, m_w_ff1, m_w_ff2, v_ln_pre_mix, v_ln_post_mix, v_ln_pre_mlp, v_ln_post_mlp, v_w_in, v_b_in, v_q_a_norm, v_w_uq, v_kv_a_norm, v_w_uk, v_w_uv, v_w_o_mla, v_w_o_fox, v_w_out, v_w_ff1, v_w_ff2):
    w = dict(ln_pre_mix=ln_pre_mix, ln_post_mix=ln_post_mix, ln_pre_mlp=ln_pre_mlp, ln_post_mlp=ln_post_mlp, w_in=w_in,
             b_in=b_in, q_a_norm=q_a_norm, w_uq=w_uq, kv_a_norm=kv_a_norm, w_uk=w_uk, w_uv=w_uv, w_o_mla=w_o_mla,
             w_o_fox=w_o_fox, w_out=w_out, w_ff1=w_ff1, w_ff2=w_ff2)
    mom = dict(ln_pre_mix=m_ln_pre_mix, ln_post_mix=m_ln_post_mix, ln_pre_mlp=m_ln_pre_mlp, ln_post_mlp=m_ln_post_mlp,
               w_in=m_w_in, b_in=m_b_in, q_a_norm=m_q_a_norm, w_uq=m_w_uq, kv_a_norm=m_kv_a_norm, w_uk=m_w_uk,
               w_uv=m_w_uv, w_o_mla=m_w_o_mla, w_o_fox=m_w_o_fox, w_out=m_w_out, w_ff1=m_w_ff1, w_ff2=m_w_ff2)
    var = dict(ln_pre_mix=v_ln_pre_mix, ln_post_mix=v_ln_post_mix, ln_pre_mlp=v_ln_pre_mlp, ln_post_mlp=v_ln_post_mlp,
               w_in=v_w_in, b_in=v_b_in, q_a_norm=v_q_a_norm, w_uq=v_w_uq, kv_a_norm=v_kv_a_norm, w_uk=v_w_uk,
               w_uv=v_w_uv, w_o_mla=v_w_o_mla, w_o_fox=v_w_o_fox, w_out=v_w_out, w_ff1=v_w_ff1, w_ff2=v_w_ff2)

    big_names = [nm for nm, _ in BIG]
    c = lax.axis_index("c")
    chip = 2 * lax.axis_index("x") + lax.axis_index("y")

    axes = dict(BIG)

    def assemble(names, lay, gathered):
        return {nm: _from_shards(s4, axes[nm] - 1) for nm, s4 in zip(names, lay.unpack(gathered), strict=True)}

    groups = {"mixer": [nm for nm in big_names if nm in MIXER], "mlp": [nm for nm in big_names if nm in LATE]}
    first_names = [nm for nm in big_names if nm not in MIXER and nm not in LATE]
    full = {nm: wv for nm, wv in w.items() if nm in SMALL}
    lay_first = _PackLayout([w[nm].shape[1:] for nm in first_names])
    first = _gather_weights(lay_first.pack([w[nm][0].astype(BF16) for nm in first_names]))
    full.update(assemble(first_names, lay_first, first))
    travelling = {}
    for tag, names in groups.items():
        lay = _PackLayout([w[nm].shape[1:] for nm in names])
        shard = lay.pack([w[nm][0].astype(BF16) for nm in names])
        started = _gather_late_start(lax.optimization_barrier((shard, first))[0], tag)
        travelling[tag] = (names, lay, shard, started[:4])
        full["b_in"] = full["b_in"] + started[4][0, 0]

    def late_weights(tag, after):
        names, lay, shard, handles = travelling[tag]
        land = _gather_late_wait(*handles, after, tag)
        land = lax.dynamic_update_slice(land, shard[None], (chip, 0, 0))
        return assemble(names, lay, land)

    grad_groups = {"early": [nm for nm in big_names if nm in EARLY_GRADS],
                   "late": [nm for nm in big_names if nm not in EARLY_GRADS]}
    sent = {}

    def send_grads(tag, g):
        names = grad_groups[tag]
        lay = _PackLayout([w[nm].shape[1:] for nm in names])
        hrows = lay.rows // 2
        gbf = lay.pack([_to_shards(g[nm], axes[nm] - 1).astype(BF16) for nm in names])
        gbf = gbf.reshape(N_CHIPS, 2, hrows, lay.width)
        started = _reduce_start(gbf, tag)
        own = lax.dynamic_index_in_dim(lax.dynamic_index_in_dim(started[2], chip, axis=0, keepdims=False), c, axis=0,
                                       keepdims=False)
        sent[tag] = (names, lay, started[:4], own)
        return started[4][0, 0]

    loss_local, grad_x, grads = _local_step(x[0], positions[0], loss_target[0], full, late_weights, send_grads)

    def f_add8(ti, pa):
        tot = ti[0].astype(F32)
        for part in ti[1:]:
            tot = tot + part.astype(F32)
        return [tot], []

    reduced = []
    for tag, (names, lay, handles, own) in sent.items():
        parts = _reduce_wait(*handles, grad_x, tag)
        reduced.append(_rowwise("grad_add_" + tag, f_add8, [own] + [parts[n] for n in range(N_PART)], [],
                                [(lay.width, F32)])[0])

    assert len({lay.width for _, lay, _, _ in sent.values()}) == 1
    red = jnp.concatenate(reduced, axis=0)
    sib = _sibling_swap(red)
    lower, upper = jnp.where(c == 0, red, sib), jnp.where(c == 0, sib, red)
    g_by_name, row = {}, 0
    for names, lay, _, _ in sent.values():
        hrows = lay.rows // 2
        both = jnp.concatenate([lower[row:row + hrows], upper[row:row + hrows]], axis=0)
        g_by_name.update(zip(names, lay.unpack(both), strict=True))
        row += hrows
    g_shards = [g_by_name[nm] for nm in big_names]

    def f_adamw(ti, pa):
        wv, gv, mv, vv = ti
        return list(_adamw(wv, gv, mv, vv)), []

    out = {"grad": {}, "delta": {}, "m": {}, "v": {}}
    for nm, g_sh in zip(big_names, g_shards, strict=True):
        wd = g_sh.shape[1]
        d_sh, m_sh, v_sh = _rowwise("adamw_" + nm, f_adamw, [w[nm][0], g_sh, mom[nm][0], var[nm][0]], [], [(wd, F32)] * 3)
        out["grad"][nm], out["delta"][nm], out["m"][nm], out["v"][nm] = g_sh[None], d_sh[None], m_sh[None], v_sh[None]

    loss_row = jnp.zeros((1, LANES), F32) + loss_local
    blank = jnp.zeros((1, LANES), F32)
    small = _small_allreduce_adamw([grads[nm] for nm in SMALL] + [loss_row], [w[nm] for nm in SMALL] + [blank],
                                   [mom[nm] for nm in SMALL] + [blank], [var[nm] for nm in SMALL] + [blank])
    for kind, arrs in zip(("grad", "delta", "m", "v"), small, strict=True):
        for nm, arr in zip(SMALL, arrs[:len(SMALL)], strict=True):
            out[kind][nm] = arr
    loss = small[0][len(SMALL)][0, 0]

    return (loss, grad_x[None], *[out["grad"][nm] for nm in ALL_W], *[out["delta"][nm] for nm in ALL_W],
            *[out["m"][nm] for nm in ALL_W], *[out["v"][nm] for nm in ALL_W])
```
